```python
import math
import jax, jax.numpy as jnp
from jax import lax
import numpy as np

D_MODEL = 1024
BATCH = 8
SEQ = 8192
DEPTH = 4

PLE_DIM = 256
N_BRANCH = 4
BRANCH_W = D_MODEL // N_BRANCH
N_IN_BLOCKS = 10
CONV_A_WIDTH = 3
ATT_HEADS = 4
ATT_HEAD_DIM = BRANCH_W // ATT_HEADS
DSW_GROUPS = ((128, 1), (512, 4), (2048, 16))
SGU_CHUNK = 128
SGU_GROUPS = 4
CONF_KERNEL = 31
FFN_HIDDEN = -(-8 * D_MODEL // (3 * 256)) * 256
EPS = 1e-6

kernel_name = 'hybrid_parallel_gated_mixers'


def rmsnorm(x, g):
    xf = x.astype(jnp.float32)
    y = xf * lax.rsqrt(jnp.mean(xf * xf, axis=-1, keepdims=True) + EPS)
    return (y * g.astype(jnp.float32)).astype(x.dtype)


def layernorm(x, g, b):
    xf = x.astype(jnp.float32)
    mu = jnp.mean(xf, axis=-1, keepdims=True)
    var = jnp.mean(jnp.square(xf - mu), axis=-1, keepdims=True)
    y = (xf - mu) * lax.rsqrt(var + EPS)
    return (y * g.astype(jnp.float32) + b.astype(jnp.float32)).astype(x.dtype)


def causal_dwconv(x, w):
    k = w.shape[0]
    return lax.conv_general_dilated(
        x, w[:, None, :].astype(x.dtype), window_strides=(1,), padding=[(k - 1, 0)],
        dimension_numbers=('NWC', 'WIO', 'NWC'), feature_group_count=x.shape[-1])


def dilated_window_group(q, k, v, window, dilation):
    bsz, nh, s, dh = q.shape
    blk = window // dilation
    span = blk * dilation
    sp = -(-s // span) * span
    nb = sp // span

    def to_blocks(t):
        t = jnp.pad(t, ((0, 0), (0, 0), (0, sp - s), (0, 0)))
        t = t.reshape(bsz, nh, sp // dilation, dilation, dh).transpose(0, 1, 3, 2, 4)
        return t.reshape(bsz, nh, dilation, nb, blk, dh)

    def with_prev(t):
        prev = jnp.pad(t, ((0, 0), (0, 0), (0, 0), (1, 0), (0, 0), (0, 0)))[:, :, :, :-1]
        return jnp.concatenate([prev, t], axis=-2)

    qb = to_blocks(q)
    kc = with_prev(to_blocks(k))
    vc = with_prev(to_blocks(v))
    scores = jnp.einsum('bhrnqc,bhrnkc->bhrnqk', qb, kc).astype(jnp.float32) * (dh ** -0.5)
    qi = jnp.arange(blk)[:, None]
    ki = jnp.arange(2 * blk)[None, :]
    dist = qi + blk - ki
    band = (dist >= 0) & (dist <= blk)
    not_before_start = (jnp.arange(nb) > 0)[:, None, None] | (ki >= blk)[None]
    mask = band[None] & not_before_start
    scores = jnp.where(mask, scores, -jnp.inf)
    m = jnp.max(scores, axis=-1, keepdims=True)
    e = jnp.exp(scores - m)
    l = jnp.sum(e, axis=-1, keepdims=True)
    o = jnp.einsum('bhrnqk,bhrnkc->bhrnqc', e, vc.astype(jnp.float32)) / l
    lse = m + jnp.log(l)

    def from_blocks(t):
        c = t.shape[-1]
        t = t.reshape(bsz, nh, dilation, sp // dilation, c).transpose(0, 1, 3, 2, 4)
        return t.reshape(bsz, nh, sp, c)[:, :, :s]

    return from_blocks(o), from_blocks(lse)[..., 0]


def dilated_attention(q, k, v):
    bsz, s, _ = q.shape
    heads = lambda t: t.reshape(bsz, s, ATT_HEADS, ATT_HEAD_DIM).transpose(0, 2, 1, 3)
    qh, kh, vh = heads(q), heads(k), heads(v)
    outs, lses = [], []
    for window, dilation in DSW_GROUPS:
        o_g, lse_g = dilated_window_group(qh, kh, vh, window, dilation)
        outs.append(o_g)
        lses.append(lse_g)
    wts = jax.nn.softmax(jnp.stack(lses), axis=0)
    o = sum(wts[g][..., None] * outs[g] for g in range(len(DSW_GROUPS)))
    return o.transpose(0, 2, 1, 3).reshape(bsz, s, BRANCH_W).astype(q.dtype)


def spatial_gating(u, v, ln_g, ln_b, w_s, b_s):
    bsz, s, c = v.shape
    v = layernorm(v, ln_g, ln_b)
    vb = v.reshape(bsz, s // SGU_CHUNK, SGU_CHUNK, SGU_GROUPS, c // SGU_GROUPS)
    causal = jnp.tril(jnp.ones((SGU_CHUNK, SGU_CHUNK), dtype=bool))
    w = jnp.where(causal[None], w_s, jnp.zeros_like(w_s))
    mixed = jnp.einsum('gts,bnsgc->bntgc', w, vb) + b_s.T[None, None, :, :, None]
    return u * mixed.reshape(bsz, s, c)


def conformer_conv(val, gate, dw, ln_g, ln_b):
    y = val * jax.nn.sigmoid(gate)
    y = causal_dwconv(y, dw)
    y = layernorm(y, ln_g, ln_b)
    return jax.nn.silu(y)


def _fwd_setup_inputs(seed: int = 0) -> dict:
    key = jax.random.key(seed)
    ks = jax.random.split(key, 22)
    nrm = lambda k, shape: jax.random.normal(k, shape, jnp.float32)
    res = (2.0 * DEPTH) ** -0.5
    bw = BRANCH_W
    return {
        'x': nrm(ks[0], (BATCH, SEQ, D_MODEL)),
        'p': nrm(ks[1], (DEPTH, BATCH, SEQ, PLE_DIM)),
        'g_mix': 1.0 + 0.02 * nrm(ks[2], (DEPTH, D_MODEL)),
        'w_in': nrm(ks[3], (DEPTH, D_MODEL, N_IN_BLOCKS * bw)) * D_MODEL ** -0.5,
        'conv_a': nrm(ks[4], (DEPTH, CONV_A_WIDTH, bw)) * CONV_A_WIDTH ** -0.5,
        'sgu_ln_g': 1.0 + 0.02 * nrm(ks[5], (DEPTH, bw)),
        'sgu_ln_b': 0.02 * nrm(ks[6], (DEPTH, bw)),
        'sgu_w': nrm(ks[7], (DEPTH, SGU_GROUPS, SGU_CHUNK, SGU_CHUNK)) * SGU_CHUNK ** -0.5,
        'sgu_b': 1.0 + 0.02 * nrm(ks[8], (DEPTH, SGU_GROUPS, SGU_CHUNK)),
        'conf_dw': nrm(ks[9], (DEPTH, CONF_KERNEL, bw)) * CONF_KERNEL ** -0.5,
        'conf_ln_g': 1.0 + 0.02 * nrm(ks[10], (DEPTH, bw)),
        'conf_ln_b': 0.02 * nrm(ks[11], (DEPTH, bw)),
        'w_branch': nrm(ks[12], (DEPTH, N_BRANCH, bw, D_MODEL)) * bw ** -0.5,
        'w_merge_gate': nrm(ks[13], (DEPTH, N_BRANCH, D_MODEL, D_MODEL)) * D_MODEL ** -0.5,
        'w_out': nrm(ks[14], (DEPTH, D_MODEL, D_MODEL)) * D_MODEL ** -0.5 * res,
        'g_ffn': 1.0 + 0.02 * nrm(ks[15], (DEPTH, D_MODEL)),
        'w_ffn_in': nrm(ks[16], (DEPTH, D_MODEL, 2 * FFN_HIDDEN)) * D_MODEL ** -0.5,
        'w_ffn_out': nrm(ks[17], (DEPTH, FFN_HIDDEN, D_MODEL)) * FFN_HIDDEN ** -0.5 * res,
        'g_ple': 1.0 + 0.02 * nrm(ks[18], (DEPTH, D_MODEL)),
        'w_ple_gate': nrm(ks[19], (DEPTH, D_MODEL, D_MODEL)) * D_MODEL ** -0.5,
        'w_ple_proj': nrm(ks[20], (DEPTH, PLE_DIM, D_MODEL)) * PLE_DIM ** -0.5,
        'g_final': 1.0 + 0.02 * nrm(ks[21], (D_MODEL,)),
    }


def _fwd_reference(x, p, g_mix, w_in, conv_a, sgu_ln_g, sgu_ln_b, sgu_w, sgu_b, conf_dw,
              conf_ln_g, conf_ln_b, w_branch, w_merge_gate, w_out, g_ffn, w_ffn_in,
              w_ffn_out, g_ple, w_ple_gate, w_ple_proj, g_final):
    for i in range(DEPTH):
        h = rmsnorm(x, g_mix[i])
        proj = h @ w_in[i]
        (a_b, a_c, a_x, q, k, v, s_u, s_v, c_val, c_gate) = jnp.split(proj, N_IN_BLOCKS, axis=-1)
        y_a = a_b * causal_dwconv(a_c * a_x, conv_a[i])
        y_b = dilated_attention(q, k, v)
        y_c = spatial_gating(s_u, s_v, sgu_ln_g[i], sgu_ln_b[i], sgu_w[i], sgu_b[i])
        y_d = conformer_conv(c_val, c_gate, conf_dw[i], conf_ln_g[i], conf_ln_b[i])
        branches = (y_a, y_b, y_c, y_d)
        merged = sum(jax.nn.sigmoid(h @ w_merge_gate[i, br]) * (branches[br] @ w_branch[i, br])
                     for br in range(N_BRANCH))
        x = x + merged @ w_out[i]
        h2 = rmsnorm(x, g_ffn[i])
        f_gate, f_up = jnp.split(h2 @ w_ffn_in[i], 2, axis=-1)
        x = x + (jax.nn.silu(f_gate) * f_up) @ w_ffn_out[i]
        h3 = rmsnorm(x, g_ple[i])
        x = x + jax.nn.sigmoid(h3 @ w_ple_gate[i]) * (p[i].astype(x.dtype) @ w_ple_proj[i])
    return rmsnorm(x, g_final)


import jax as _jax
import jax.numpy as _jnp

TWIN_FORMAT = 'train_step'
FWD_PARAMS = ['x', 'p', 'g_mix', 'w_in', 'conv_a', 'sgu_ln_g', 'sgu_ln_b', 'sgu_w', 'sgu_b', 'conf_dw', 'conf_ln_g', 'conf_ln_b', 'w_branch', 'w_merge_gate', 'w_out', 'g_ffn', 'w_ffn_in', 'w_ffn_out', 'g_ple', 'w_ple_gate', 'w_ple_proj', 'g_final']
TWIN_WEIGHTS = ['g_mix', 'w_in', 'conv_a', 'sgu_ln_g', 'sgu_ln_b', 'sgu_w', 'sgu_b', 'conf_dw', 'conf_ln_g', 'conf_ln_b', 'w_branch', 'w_merge_gate', 'w_out', 'g_ffn', 'w_ffn_in', 'w_ffn_out', 'g_ple', 'w_ple_gate', 'w_ple_proj', 'g_final']
TWIN_DIFF_INPUT = 'x'
TWIN_INPUTS = ['x', 'p', 'g_mix', 'w_in', 'conv_a', 'sgu_ln_g', 'sgu_ln_b', 'sgu_w', 'sgu_b', 'conf_dw', 'conf_ln_g', 'conf_ln_b', 'w_branch', 'w_merge_gate', 'w_out', 'g_ffn', 'w_ffn_in', 'w_ffn_out', 'g_ple', 'w_ple_gate', 'w_ple_proj', 'g_final', 'loss_target', 'm_g_mix', 'm_w_in', 'm_conv_a', 'm_sgu_ln_g', 'm_sgu_ln_b', 'm_sgu_w', 'm_sgu_b', 'm_conf_dw', 'm_conf_ln_g', 'm_conf_ln_b', 'm_w_branch', 'm_w_merge_gate', 'm_w_out', 'm_g_ffn', 'm_w_ffn_in', 'm_w_ffn_out', 'm_g_ple', 'm_w_ple_gate', 'm_w_ple_proj', 'm_g_final', 'v_g_mix', 'v_w_in', 'v_conv_a', 'v_sgu_ln_g', 'v_sgu_ln_b', 'v_sgu_w', 'v_sgu_b', 'v_conf_dw', 'v_conf_ln_g', 'v_conf_ln_b', 'v_w_branch', 'v_w_merge_gate', 'v_w_out', 'v_g_ffn', 'v_w_ffn_in', 'v_w_ffn_out', 'v_g_ple', 'v_w_ple_gate', 'v_w_ple_proj', 'v_g_final']
TWIN_OUTPUTS = ['loss', 'grad_x', 'grad_g_mix', 'grad_w_in', 'grad_conv_a', 'grad_sgu_ln_g', 'grad_sgu_ln_b', 'grad_sgu_w', 'grad_sgu_b', 'grad_conf_dw', 'grad_conf_ln_g', 'grad_conf_ln_b', 'grad_w_branch', 'grad_w_merge_gate', 'grad_w_out', 'grad_g_ffn', 'grad_w_ffn_in', 'grad_w_ffn_out', 'grad_g_ple', 'grad_w_ple_gate', 'grad_w_ple_proj', 'grad_g_final', 'delta_g_mix', 'delta_w_in', 'delta_conv_a', 'delta_sgu_ln_g', 'delta_sgu_ln_b', 'delta_sgu_w', 'delta_sgu_b', 'delta_conf_dw', 'delta_conf_ln_g', 'delta_conf_ln_b', 'delta_w_branch', 'delta_w_merge_gate', 'delta_w_out', 'delta_g_ffn', 'delta_w_ffn_in', 'delta_w_ffn_out', 'delta_g_ple', 'delta_w_ple_gate', 'delta_w_ple_proj', 'delta_g_final', 'new_m_g_mix', 'new_m_w_in', 'new_m_conv_a', 'new_m_sgu_ln_g', 'new_m_sgu_ln_b', 'new_m_sgu_w', 'new_m_sgu_b', 'new_m_conf_dw', 'new_m_conf_ln_g', 'new_m_conf_ln_b', 'new_m_w_branch', 'new_m_w_merge_gate', 'new_m_w_out', 'new_m_g_ffn', 'new_m_w_ffn_in', 'new_m_w_ffn_out', 'new_m_g_ple', 'new_m_w_ple_gate', 'new_m_w_ple_proj', 'new_m_g_final', 'new_v_g_mix', 'new_v_w_in', 'new_v_conv_a', 'new_v_sgu_ln_g', 'new_v_sgu_ln_b', 'new_v_sgu_w', 'new_v_sgu_b', 'new_v_conf_dw', 'new_v_conf_ln_g', 'new_v_conf_ln_b', 'new_v_w_branch', 'new_v_w_merge_gate', 'new_v_w_out', 'new_v_g_ffn', 'new_v_w_ffn_in', 'new_v_w_ffn_out', 'new_v_g_ple', 'new_v_w_ple_gate', 'new_v_w_ple_proj', 'new_v_g_final']
TWIN_LEAF_KINDS = {'loss': 'loss', 'grad_x': 'grad_x', 'grad_g_mix': 'grad_w', 'grad_w_in': 'grad_w', 'grad_conv_a': 'grad_w', 'grad_sgu_ln_g': 'grad_w', 'grad_sgu_ln_b': 'grad_w', 'grad_sgu_w': 'grad_w', 'grad_sgu_b': 'grad_w', 'grad_conf_dw': 'grad_w', 'grad_conf_ln_g': 'grad_w', 'grad_conf_ln_b': 'grad_w', 'grad_w_branch': 'grad_w', 'grad_w_merge_gate': 'grad_w', 'grad_w_out': 'grad_w', 'grad_g_ffn': 'grad_w', 'grad_w_ffn_in': 'grad_w', 'grad_w_ffn_out': 'grad_w', 'grad_g_ple': 'grad_w', 'grad_w_ple_gate': 'grad_w', 'grad_w_ple_proj': 'grad_w', 'grad_g_final': 'grad_w', 'delta_g_mix': 'delta_w', 'delta_w_in': 'delta_w', 'delta_conv_a': 'delta_w', 'delta_sgu_ln_g': 'delta_w', 'delta_sgu_ln_b': 'delta_w', 'delta_sgu_w': 'delta_w', 'delta_sgu_b': 'delta_w', 'delta_conf_dw': 'delta_w', 'delta_conf_ln_g': 'delta_w', 'delta_conf_ln_b': 'delta_w', 'delta_w_branch': 'delta_w', 'delta_w_merge_gate': 'delta_w', 'delta_w_out': 'delta_w', 'delta_g_ffn': 'delta_w', 'delta_w_ffn_in': 'delta_w', 'delta_w_ffn_out': 'delta_w', 'delta_g_ple': 'delta_w', 'delta_w_ple_gate': 'delta_w', 'delta_w_ple_proj': 'delta_w', 'delta_g_final': 'delta_w', 'new_m_g_mix': 'new_m', 'new_m_w_in': 'new_m', 'new_m_conv_a': 'new_m', 'new_m_sgu_ln_g': 'new_m', 'new_m_sgu_ln_b': 'new_m', 'new_m_sgu_w': 'new_m', 'new_m_sgu_b': 'new_m', 'new_m_conf_dw': 'new_m', 'new_m_conf_ln_g': 'new_m', 'new_m_conf_ln_b': 'new_m', 'new_m_w_branch': 'new_m', 'new_m_w_merge_gate': 'new_m', 'new_m_w_out': 'new_m', 'new_m_g_ffn': 'new_m', 'new_m_w_ffn_in': 'new_m', 'new_m_w_ffn_out': 'new_m', 'new_m_g_ple': 'new_m', 'new_m_w_ple_gate': 'new_m', 'new_m_w_ple_proj': 'new_m', 'new_m_g_final': 'new_m', 'new_v_g_mix': 'new_v', 'new_v_w_in': 'new_v', 'new_v_conv_a': 'new_v', 'new_v_sgu_ln_g': 'new_v', 'new_v_sgu_ln_b': 'new_v', 'new_v_sgu_w': 'new_v', 'new_v_sgu_b': 'new_v', 'new_v_conf_dw': 'new_v', 'new_v_conf_ln_g': 'new_v', 'new_v_conf_ln_b': 'new_v', 'new_v_w_branch': 'new_v', 'new_v_w_merge_gate': 'new_v', 'new_v_w_out': 'new_v', 'new_v_g_ffn': 'new_v', 'new_v_w_ffn_in': 'new_v', 'new_v_w_ffn_out': 'new_v', 'new_v_g_ple': 'new_v', 'new_v_w_ple_gate': 'new_v', 'new_v_w_ple_proj': 'new_v', 'new_v_g_final': 'new_v'}


def _forward(args):
    return _fwd_reference(*[args[k] for k in FWD_PARAMS])


def _output_shape():
    def fwd():
        inp = _fwd_setup_inputs(0)
        return _fwd_reference(*[inp[k] for k in FWD_PARAMS])
    out = _jax.eval_shape(fwd)
    return out.shape, out.dtype

N_MICROBATCH = 1
ADAM_LR = 0.001
ADAM_B1 = 0.9
ADAM_B2 = 0.999
ADAM_EPS = 1e-08
ADAM_WD = 0.01
ADAM_STEP = 10
PER_EXAMPLE_BATCH_AXIS = {'x': 0, 'p': 1, 'loss_target': 0}
SHARED_INPUTS = []
_WEIGHT_DTYPES = {'g_mix': _jnp.float32, 'w_in': _jnp.float32, 'conv_a': _jnp.float32, 'sgu_ln_g': _jnp.float32, 'sgu_ln_b': _jnp.float32, 'sgu_w': _jnp.float32, 'sgu_b': _jnp.float32, 'conf_dw': _jnp.float32, 'conf_ln_g': _jnp.float32, 'conf_ln_b': _jnp.float32, 'w_branch': _jnp.float32, 'w_merge_gate': _jnp.float32, 'w_out': _jnp.float32, 'g_ffn': _jnp.float32, 'w_ffn_in': _jnp.float32, 'w_ffn_out': _jnp.float32, 'g_ple': _jnp.float32, 'w_ple_gate': _jnp.float32, 'w_ple_proj': _jnp.float32, 'g_final': _jnp.float32}
MOMENT_SCALE = {'g_mix': 8.660869e-02, 'w_in': 5.160344e-02, 'conv_a': 7.109868e-02, 'sgu_ln_g': 5.265344e-02, 'sgu_ln_b': 5.014504e-02, 'sgu_w': 3.473407e-02, 'sgu_b': 4.894218e-02, 'conf_dw': 4.323823e-02, 'conf_ln_g': 5.196241e-02, 'conf_ln_b': 4.296068e-02, 'w_branch': 2.948155e-02, 'w_merge_gate': 1.152952e-02, 'w_out': 1.666755e-01, 'g_ffn': 5.356976e-02, 'w_ffn_in': 2.270407e-02, 'w_ffn_out': 1.048239e-01, 'g_ple': 3.685556e-02, 'w_ple_gate': 3.671329e-02, 'w_ple_proj': 9.402214e-02, 'g_final': 6.405457e+01}


def _to_microbatches(a, axis):
    t = _jnp.moveaxis(a, axis, 0)
    t = t.reshape((N_MICROBATCH, t.shape[0] // N_MICROBATCH) + t.shape[1:])
    return _jnp.moveaxis(t, 1, axis + 1)


def setup_inputs(seed: int = 0) -> dict:
    inp = _fwd_setup_inputs(seed)
    key = _jax.random.fold_in(_jax.random.key(seed), 7919)
    shape, _ = _output_shape()
    out = dict(inp)
    out["loss_target"] = _jax.random.normal(_jax.random.fold_in(key, 0), shape, _jnp.float32)
    for i, name in enumerate(TWIN_WEIGHTS):
        w = inp[name].astype(_jnp.float32)
        if MOMENT_SCALE is None:
            s = _jnp.sqrt(_jnp.mean(_jnp.square(w)) + 1e-30)
        else:
            s = MOMENT_SCALE[name]
        km, kv = _jax.random.split(_jax.random.fold_in(key, i + 1))
        out[name] = w
        out["m_" + name] = s * _jax.random.normal(km, w.shape, _jnp.float32)
        out["v_" + name] = (s * s) * _jax.random.uniform(kv, w.shape, _jnp.float32, 0.5, 1.5)
    if N_MICROBATCH > 1:
        for name, axis in PER_EXAMPLE_BATCH_AXIS.items():
            out[name] = _to_microbatches(out[name], axis)
    return {'x': out['x'], 'p': out['p'], 'g_mix': out['g_mix'], 'w_in': out['w_in'], 'conv_a': out['conv_a'], 'sgu_ln_g': out['sgu_ln_g'], 'sgu_ln_b': out['sgu_ln_b'], 'sgu_w': out['sgu_w'], 'sgu_b': out['sgu_b'], 'conf_dw': out['conf_dw'], 'conf_ln_g': out['conf_ln_g'], 'conf_ln_b': out['conf_ln_b'], 'w_branch': out['w_branch'], 'w_merge_gate': out['w_merge_gate'], 'w_out': out['w_out'], 'g_ffn': out['g_ffn'], 'w_ffn_in': out['w_ffn_in'], 'w_ffn_out': out['w_ffn_out'], 'g_ple': out['g_ple'], 'w_ple_gate': out['w_ple_gate'], 'w_ple_proj': out['w_ple_proj'], 'g_final': out['g_final'], 'loss_target': out['loss_target'], 'm_g_mix': out['m_g_mix'], 'm_w_in': out['m_w_in'], 'm_conv_a': out['m_conv_a'], 'm_sgu_ln_g': out['m_sgu_ln_g'], 'm_sgu_ln_b': out['m_sgu_ln_b'], 'm_sgu_w': out['m_sgu_w'], 'm_sgu_b': out['m_sgu_b'], 'm_conf_dw': out['m_conf_dw'], 'm_conf_ln_g': out['m_conf_ln_g'], 'm_conf_ln_b': out['m_conf_ln_b'], 'm_w_branch': out['m_w_branch'], 'm_w_merge_gate': out['m_w_merge_gate'], 'm_w_out': out['m_w_out'], 'm_g_ffn': out['m_g_ffn'], 'm_w_ffn_in': out['m_w_ffn_in'], 'm_w_ffn_out': out['m_w_ffn_out'], 'm_g_ple': out['m_g_ple'], 'm_w_ple_gate': out['m_w_ple_gate'], 'm_w_ple_proj': out['m_w_ple_proj'], 'm_g_final': out['m_g_final'], 'v_g_mix': out['v_g_mix'], 'v_w_in': out['v_w_in'], 'v_conv_a': out['v_conv_a'], 'v_sgu_ln_g': out['v_sgu_ln_g'], 'v_sgu_ln_b': out['v_sgu_ln_b'], 'v_sgu_w': out['v_sgu_w'], 'v_sgu_b': out['v_sgu_b'], 'v_conf_dw': out['v_conf_dw'], 'v_conf_ln_g': out['v_conf_ln_g'], 'v_conf_ln_b': out['v_conf_ln_b'], 'v_w_branch': out['v_w_branch'], 'v_w_merge_gate': out['v_w_merge_gate'], 'v_w_out': out['v_w_out'], 'v_g_ffn': out['v_g_ffn'], 'v_w_ffn_in': out['v_w_ffn_in'], 'v_w_ffn_out': out['v_w_ffn_out'], 'v_g_ple': out['v_g_ple'], 'v_w_ple_gate': out['v_w_ple_gate'], 'v_w_ple_proj': out['v_w_ple_proj'], 'v_g_final': out['v_g_final']}


def _loss(weights, diff, rest, loss_target):
    with _jax.named_scope("forward"):
        args = {**rest, TWIN_DIFF_INPUT: diff, **{k: w.astype(_WEIGHT_DTYPES[k]) for k, w in weights.items()}}
        y = _forward(args)
    with _jax.named_scope("loss_head"):
        err = _jnp.square(y.astype(_jnp.float32) - loss_target)
        return 0.5 * _jnp.sum(_jnp.mean(err, axis=-1)) if err.ndim else 0.5 * err


def _adamw(w, g, m, v):
    m = ADAM_B1 * m + (1.0 - ADAM_B1) * g
    v = ADAM_B2 * v + (1.0 - ADAM_B2) * _jnp.square(g)
    m_hat = m / (1.0 - ADAM_B1 ** ADAM_STEP)
    v_hat = v / (1.0 - ADAM_B2 ** ADAM_STEP)
    delta = -ADAM_LR * (m_hat / (_jnp.sqrt(v_hat) + ADAM_EPS) + ADAM_WD * w)
    return delta, m, v


def reference(x, p, g_mix, w_in, conv_a, sgu_ln_g, sgu_ln_b, sgu_w, sgu_b, conf_dw, conf_ln_g, conf_ln_b, w_branch, w_merge_gate, w_out, g_ffn, w_ffn_in, w_ffn_out, g_ple, w_ple_gate, w_ple_proj, g_final, loss_target, m_g_mix, m_w_in, m_conv_a, m_sgu_ln_g, m_sgu_ln_b, m_sgu_w, m_sgu_b, m_conf_dw, m_conf_ln_g, m_conf_ln_b, m_w_branch, m_w_merge_gate, m_w_out, m_g_ffn, m_w_ffn_in, m_w_ffn_out, m_g_ple, m_w_ple_gate, m_w_ple_proj, m_g_final, v_g_mix, v_w_in, v_conv_a, v_sgu_ln_g, v_sgu_ln_b, v_sgu_w, v_sgu_b, v_conf_dw, v_conf_ln_g, v_conf_ln_b, v_w_branch, v_w_merge_gate, v_w_out, v_g_ffn, v_w_ffn_in, v_w_ffn_out, v_g_ple, v_w_ple_gate, v_w_ple_proj, v_g_final):
    given = dict(x=x, p=p, g_mix=g_mix, w_in=w_in, conv_a=conv_a, sgu_ln_g=sgu_ln_g, sgu_ln_b=sgu_ln_b, sgu_w=sgu_w, sgu_b=sgu_b, conf_dw=conf_dw, conf_ln_g=conf_ln_g, conf_ln_b=conf_ln_b, w_branch=w_branch, w_merge_gate=w_merge_gate, w_out=w_out, g_ffn=g_ffn, w_ffn_in=w_ffn_in, w_ffn_out=w_ffn_out, g_ple=g_ple, w_ple_gate=w_ple_gate, w_ple_proj=w_ple_proj, g_final=g_final, loss_target=loss_target, m_g_mix=m_g_mix, m_w_in=m_w_in, m_conv_a=m_conv_a, m_sgu_ln_g=m_sgu_ln_g, m_sgu_ln_b=m_sgu_ln_b, m_sgu_w=m_sgu_w, m_sgu_b=m_sgu_b, m_conf_dw=m_conf_dw, m_conf_ln_g=m_conf_ln_g, m_conf_ln_b=m_conf_ln_b, m_w_branch=m_w_branch, m_w_merge_gate=m_w_merge_gate, m_w_out=m_w_out, m_g_ffn=m_g_ffn, m_w_ffn_in=m_w_ffn_in, m_w_ffn_out=m_w_ffn_out, m_g_ple=m_g_ple, m_w_ple_gate=m_w_ple_gate, m_w_ple_proj=m_w_ple_proj, m_g_final=m_g_final, v_g_mix=v_g_mix, v_w_in=v_w_in, v_conv_a=v_conv_a, v_sgu_ln_g=v_sgu_ln_g, v_sgu_ln_b=v_sgu_ln_b, v_sgu_w=v_sgu_w, v_sgu_b=v_sgu_b, v_conf_dw=v_conf_dw, v_conf_ln_g=v_conf_ln_g, v_conf_ln_b=v_conf_ln_b, v_w_branch=v_w_branch, v_w_merge_gate=v_w_merge_gate, v_w_out=v_w_out, v_g_ffn=v_g_ffn, v_w_ffn_in=v_w_ffn_in, v_w_ffn_out=v_w_ffn_out, v_g_ple=v_g_ple, v_w_ple_gate=v_w_ple_gate, v_w_ple_proj=v_w_ple_proj, v_g_final=v_g_final)
    weights = {n: given[n] for n in TWIN_WEIGHTS}
    shared = {n: given[n] for n in SHARED_INPUTS}
    per_example = {n: given[n] for n in ['x', 'p']}
    grad_fn = _jax.value_and_grad(_loss, argnums=(0, 1))

    def one_microbatch(ex, loss_target):
        ex = dict(ex)
        diff = ex.pop(TWIN_DIFF_INPUT)
        return grad_fn(weights, diff, {**shared, **ex}, loss_target)

    if N_MICROBATCH == 1:
        loss, (grad_w, grad_x) = one_microbatch(per_example, given["loss_target"])
    else:
        def body(carry, xs):
            loss_sum, grad_sum = carry
            l_k, (gw_k, gx_k) = one_microbatch(xs[0], xs[1])
            with _jax.named_scope("update"):
                return (loss_sum + l_k, _jax.tree.map(_jnp.add, grad_sum, gw_k)), gx_k

        init = (_jnp.zeros((), _jnp.float32), _jax.tree.map(_jnp.zeros_like, weights))
        (loss, grad_w), grad_x = _jax.lax.scan(body, init, (per_example, given["loss_target"]))
    with _jax.named_scope("update"):
        delta_w, new_m, new_v = {}, {}, {}
        for n in TWIN_WEIGHTS:
            delta_w[n], new_m[n], new_v[n] = _adamw(weights[n], grad_w[n], given["m_" + n], given["v_" + n])
    return (loss, grad_x, *[grad_w[n] for n in TWIN_WEIGHTS], *[delta_w[n] for n in TWIN_WEIGHTS],
            *[new_m[n] for n in TWIN_WEIGHTS], *[new_v[n] for n in TWIN_WEIGHTS])
```

```python
import functools
import math

import jax
import jax.numpy as jnp
from jax import lax
from jax.experimental import pallas as pl
from jax.experimental.pallas import tpu as pltpu

F32 = jnp.float32
BF16 = jnp.bfloat16
EPS = 1e-6
D_MODEL = 1024
BW = 256
N_BR = 4
N_IN = 10 * BW
FFN_H = 2816
N_SH = 4
HEADS = 4
HEAD_D = 64
BLK = 128
DILATIONS = (1, 4, 16)
CONF_K = 31
CONVA_K = 3
NEG = -1e30
VMEM_LIMIT = 56 * 1024 * 1024
MESH = pl.DeviceIdType.MESH

ADAM_LR, ADAM_B1, ADAM_B2, ADAM_EPS, ADAM_WD, ADAM_STEP = 0.001, 0.9, 0.999, 1e-08, 0.01, 10

bs = pl.BlockSpec


def _call(body, name, grid, in_specs, out_specs, out_shape, scratch=(), aliases=None):
    return pl.pallas_call(
        body, name=name, grid=grid, in_specs=in_specs, out_specs=out_specs, out_shape=out_shape,
        scratch_shapes=list(scratch), input_output_aliases=aliases or {},
        compiler_params=pltpu.CompilerParams(dimension_semantics=("arbitrary",) * len(grid),
                                             vmem_limit_bytes=VMEM_LIMIT))


def _sds(shape, dtype):
    return jax.ShapeDtypeStruct(shape, dtype)


def _nn(a, b):
    return jnp.dot(a, b, preferred_element_type=F32)


def _nt(a, b):
    return lax.dot_general(a, b, (((1,), (1,)), ((), ())), preferred_element_type=F32)


def _tn(a, b):
    return lax.dot_general(a, b, (((0,), (0,)), ((), ())), preferred_element_type=F32)


def _sigmoid(x):
    return 1.0 / (1.0 + jnp.exp(-x))


def _rms_fwd(x, g):
    r = lax.rsqrt(jnp.mean(x * x, axis=-1, keepdims=True) + EPS)
    return x * r * g


def _rms_bwd(dh, x, g):
    r = lax.rsqrt(jnp.mean(x * x, axis=-1, keepdims=True) + EPS)
    xr = x * r
    dxr = dh * g
    dx = r * (dxr - xr * jnp.mean(dxr * xr, axis=-1, keepdims=True))
    return dx, dh * xr


def _ln_hat(x):
    mu = jnp.mean(x, axis=-1, keepdims=True)
    xc = x - mu
    r = lax.rsqrt(jnp.mean(xc * xc, axis=-1, keepdims=True) + EPS)
    return xc * r, r


def _ln_bwd(dy, xhat, r, g):
    dxh = dy * g
    return r * (dxh - jnp.mean(dxh, axis=-1, keepdims=True) - xhat * jnp.mean(dxh * xhat, axis=-1, keepdims=True))


def _colsum(v):
    return jnp.sum(v, axis=0, keepdims=True)


def _causal_conv(zext, w_ref, k_taps, halo):
    acc = zext[halo:] * w_ref[k_taps - 1:k_taps, :]
    for k in range(k_taps - 1):
        acc = acc + pltpu.roll(zext, k_taps - 1 - k, 0)[halo:] * w_ref[k:k + 1, :]
    return acc


def _anti_conv(dext, w_ref, k_taps, tm):
    n = dext.shape[0]
    acc = dext[:tm] * w_ref[k_taps - 1:k_taps, :]
    for s in range(1, k_taps):
        acc = acc + pltpu.roll(dext, n - s, 0)[:tm] * w_ref[k_taps - 1 - s:k_taps - s, :]
    return acc


def _conv_wgrad(dw_ref, dc, zext, k_taps, halo):
    dw_ref[k_taps - 1:k_taps, :] += _colsum(dc * zext[halo:])
    for k in range(k_taps - 1):
        dw_ref[k:k + 1, :] += _colsum(dc * pltpu.roll(zext, k_taps - 1 - k, 0)[halo:])


def _prev_blk(i, per):
    return jnp.maximum(i * per - 1, 0)


def _next_blk(i, per, last):
    return jnp.minimum((i + 1) * per, last)


def norm_in_proj(x, g, win):
    t = x.shape[0]
    tm = min(1024, t)
    ns = win.shape[2]

    def body(x_ref, g_ref, w_ref, h_ref, o_ref):
        h = _rms_fwd(x_ref[...], g_ref[...]).astype(BF16)
        h_ref[...] = h
        o_ref[...] = _nn(h, w_ref[...])

    return _call(
        body, "norm_in_proj", (t // tm, N_SH),
        [bs((tm, D_MODEL), lambda i, j: (i, 0)), bs((1, D_MODEL), lambda i, j: (0, 0)),
         bs((None, D_MODEL, ns), lambda i, j: (j, 0, 0))],
        [bs((tm, D_MODEL), lambda i, j: (i, 0)), bs((tm, ns), lambda i, j: (i, j))],
        [_sds((t, D_MODEL), BF16), _sds((t, N_SH * ns), F32)])(x, g, win)


def merge_fwd(h, ys, wg, wbr):
    t = h.shape[0]
    tm = min(1024, t)

    def body(h_ref, ya, yb, yc, yd, wg_ref, wb_ref, m_ref, g_ref, b_ref):
        hh = h_ref[...]
        acc = None
        for k, y_ref in enumerate((ya, yb, yc, yd)):
            g = _sigmoid(_nn(hh, wg_ref[k]))
            b = _nn(y_ref[...], wb_ref[k])
            g_ref[k] = g.astype(BF16)
            b_ref[k] = b.astype(BF16)
            acc = g * b if acc is None else acc + g * b
        m_ref[...] = acc.astype(BF16)

    ysp = bs((tm, BW), lambda i, j: (i, 0))
    return _call(
        body, "merge_fwd", (t // tm, N_SH),
        [bs((tm, D_MODEL), lambda i, j: (i, 0)), ysp, ysp, ysp, ysp,
         bs((N_BR, D_MODEL, BW), lambda i, j: (0, 0, j)), bs((None, N_BR, BW, BW), lambda i, j: (j, 0, 0, 0))],
        [bs((tm, BW), lambda i, j: (i, j)), bs((N_BR, tm, BW), lambda i, j: (0, i, j)),
         bs((N_BR, tm, BW), lambda i, j: (0, i, j))],
        [_sds((t, D_MODEL), BF16), _sds((N_BR, t, D_MODEL), BF16), _sds((N_BR, t, D_MODEL), BF16)])(
            h, *ys, wg, wbr)


def mm_residual(a, w, res, name):
    t, kk = a.shape
    tm, tn = min(1024, t), 512

    def body(a_ref, w_ref, r_ref, o_ref):
        o_ref[...] = r_ref[...] + _nn(a_ref[...], w_ref[...])

    return _call(
        body, name, (t // tm, D_MODEL // tn),
        [bs((tm, kk), lambda i, j: (i, 0)), bs((kk, tn), lambda i, j: (0, j)), bs((tm, tn), lambda i, j: (i, j))],
        bs((tm, tn), lambda i, j: (i, j)), _sds((t, D_MODEL), F32))(a, w, res)


def ffn_in(x, g, wfi):
    t = x.shape[0]
    tm = min(1024, t)
    ns = wfi.shape[2]

    def body(x_ref, g_ref, wg_ref, wu_ref, h_ref, f_ref, a_ref):
        h = _rms_fwd(x_ref[...], g_ref[...]).astype(BF16)
        h_ref[...] = h
        fg = _nn(h, wg_ref[...])
        fu = _nn(h, wu_ref[...])
        f_ref[0] = fg.astype(BF16)
        f_ref[1] = fu.astype(BF16)
        a_ref[...] = (fg * _sigmoid(fg) * fu).astype(BF16)

    return _call(
        body, "ffn_in", (t // tm, 2),
        [bs((tm, D_MODEL), lambda i, j: (i, 0)), bs((1, D_MODEL), lambda i, j: (0, 0)),
         bs((None, D_MODEL, ns), lambda i, j: (j, 0, 0)), bs((None, D_MODEL, ns), lambda i, j: (j + 2, 0, 0))],
        [bs((tm, D_MODEL), lambda i, j: (i, 0)), bs((2, tm, ns), lambda i, j: (0, i, j)),
         bs((tm, ns), lambda i, j: (i, j))],
        [_sds((t, D_MODEL), BF16), _sds((2, t, FFN_H), BF16), _sds((t, FFN_H), BF16)])(x, g, wfi, wfi)


def ple_fwd(x, g, wpg, p_i, wpp):
    t = x.shape[0]
    tm = min(1024, t)

    def body(x_ref, xt_ref, g_ref, wg_ref, p_ref, wp_ref, h_ref, gt_ref, pp_ref, o_ref):
        h = _rms_fwd(x_ref[...], g_ref[...]).astype(BF16)
        h_ref[...] = h
        gate = _sigmoid(_nn(h, wg_ref[...]))
        pp = _nn(p_ref[...].astype(BF16), wp_ref[...])
        gt_ref[...] = gate.astype(BF16)
        pp_ref[...] = pp.astype(BF16)
        o_ref[...] = xt_ref[...] + gate * pp

    tile = bs((tm, BW), lambda i, j: (i, j))
    return _call(
        body, "ple_fwd", (t // tm, N_SH),
        [bs((tm, D_MODEL), lambda i, j: (i, 0)), tile, bs((1, D_MODEL), lambda i, j: (0, 0)),
         bs((D_MODEL, BW), lambda i, j: (0, j)), bs((tm, BW), lambda i, j: (i, 0)),
         bs((None, BW, BW), lambda i, j: (j, 0, 0))],
        [bs((tm, D_MODEL), lambda i, j: (i, 0)), tile, tile, tile],
        [_sds((t, D_MODEL), BF16), _sds((t, D_MODEL), BF16), _sds((t, D_MODEL), BF16), _sds((t, D_MODEL), F32)])(
            x, x, g, wpg, p_i, wpp)


def loss_head(x, g, tgt):
    t = x.shape[0]
    tm = min(512, t)

    def body(x_ref, g_ref, t_ref, l_ref, dx_ref, dg_ref):
        @pl.when(pl.program_id(0) == 0)
        def _():
            l_ref[...] = jnp.zeros_like(l_ref)
            dg_ref[...] = jnp.zeros_like(dg_ref)

        xv, gv = x_ref[...], g_ref[...]
        err = _rms_fwd(xv, gv) - t_ref[...]
        part = 0.5 * jnp.sum(jnp.mean(err * err, axis=-1, keepdims=True), axis=0, keepdims=True)
        l_ref[...] += jnp.broadcast_to(part, l_ref.shape)
        dx, dgr = _rms_bwd(err * (1.0 / D_MODEL), xv, gv)
        dx_ref[...] = dx
        dg_ref[...] += _colsum(dgr)

    row = bs((tm, D_MODEL), lambda i: (i, 0))
    vec = bs((1, D_MODEL), lambda i: (0, 0))
    return _call(body, "loss_head", (t // tm,), [row, vec, row],
                 [bs((1, 128), lambda i: (0, 0)), row, vec],
                 [_sds((1, 128), F32), _sds((t, D_MODEL), F32), _sds((1, D_MODEL), F32)])(x, g, tgt)


def tn_matmul(name, a, b, grid, a_spec, b_spec, out_spec, out_shape):
    last = len(grid) - 1

    def body(a_ref, b_ref, o_ref):
        @pl.when(pl.program_id(last) == 0)
        def _():
            o_ref[...] = jnp.zeros_like(o_ref)

        o_ref[...] += _tn(a_ref[...].astype(BF16), b_ref[...].astype(BF16))

    return _call(body, name, grid, [a_spec, b_spec], out_spec, out_shape)(a, b)


def norm_bwd(name, sources, add, dx_in, x, g):
    t = x.shape[0]
    tm = min(512, t)
    offs, nk = [], 0
    for s in sources:
        offs.append(nk)
        nk += s[4]
    n_src = len(sources)
    has_add = add is not None

    def body(*refs):
        a_refs = refs[0:2 * n_src:2]
        w_refs = refs[1:2 * n_src:2]
        pos = 2 * n_src
        add_ref = refs[pos] if has_add else None
        pos += int(has_add)
        dxi_ref, x_ref, g_ref, dx_ref, dg_ref, acc_ref = refs[pos:pos + 6]
        i, k = pl.program_id(0), pl.program_id(1)

        @pl.when((i == 0) & (k == 0))
        def _():
            dg_ref[...] = jnp.zeros_like(dg_ref)

        @pl.when(k == 0)
        def _():
            acc_ref[...] = add_ref[...] if has_add else jnp.zeros_like(acc_ref)

        for si in range(n_src):
            @pl.when((k >= offs[si]) & (k < offs[si] + sources[si][4]))
            def _(si=si):
                acc_ref[...] += _nt(a_refs[si][...], w_refs[si][...])

        @pl.when(k == nk - 1)
        def _():
            dx, dgr = _rms_bwd(acc_ref[...], x_ref[...], g_ref[...])
            dx_ref[...] = dxi_ref[...] + dx
            dg_ref[...] += _colsum(dgr)

    in_specs, args = [], []
    for si, (a, a_spec, w, w_spec, steps) in enumerate(sources):
        loc = functools.partial(lambda k, o, n: jnp.clip(k - o, 0, n - 1), o=offs[si], n=steps)
        in_specs.append(a_spec(tm, loc))
        in_specs.append(w_spec(loc))
        args += [a, w]
    row = bs((tm, D_MODEL), lambda i, k: (i, 0))
    vec = bs((1, D_MODEL), lambda i, k: (0, 0))
    if has_add:
        in_specs.append(row)
        args.append(add)
    in_specs += [row, row, vec]
    args += [dx_in, x, g]
    return _call(body, name, (t // tm, nk), in_specs, [row, vec],
                 [_sds((t, D_MODEL), F32), _sds((1, D_MODEL), F32)],
                 scratch=[pltpu.VMEM((tm, D_MODEL), F32)])(*args)


def ple_bwd_pre(dx, gate, pp):
    t = dx.shape[0]
    tm = min(1024, t)

    def body(dx_ref, g_ref, p_ref, dpre_ref, dpp_ref):
        d = dx_ref[...]
        g = g_ref[...].astype(F32)
        dpre_ref[...] = (d * p_ref[...].astype(F32) * g * (1.0 - g)).astype(BF16)
        dpp_ref[...] = (d * g).astype(BF16)

    row = bs((tm, D_MODEL), lambda i: (i, 0))
    return _call(body, "ple_bwd_pre", (t // tm,), [row, row, row], [row, row],
                 [_sds((t, D_MODEL), BF16), _sds((t, D_MODEL), BF16)])(dx, gate, pp)


def ffn_bwd_act(dx, wfo, fgu):
    t = dx.shape[0]
    tm = min(1024, t)
    ns = FFN_H // 2

    def body(dx_ref, w_ref, f_ref, o_ref):
        dact = _nt(dx_ref[...].astype(BF16), w_ref[...])
        fg = f_ref[0].astype(F32)
        fu = f_ref[1].astype(F32)
        s = _sigmoid(fg)
        o_ref[0] = (dact * fu * (s * (1.0 + fg * (1.0 - s)))).astype(BF16)
        o_ref[1] = (dact * fg * s).astype(BF16)

    blk = bs((2, tm, ns), lambda i, j: (0, i, j))
    return _call(body, "ffn_bwd_act", (t // tm, 2),
                 [bs((tm, D_MODEL), lambda i, j: (i, 0)), bs((ns, D_MODEL), lambda i, j: (j, 0)), blk],
                 blk, _sds((2, t, FFN_H), BF16))(dx, wfo, fgu)


def merge_bwd_pre(dx, wout, gates, ybr):
    t = dx.shape[0]
    tm = min(1024, t)

    def body(dx_ref, w_ref, g_ref, b_ref, dpre_ref, dyb_ref):
        dm = _nt(dx_ref[...].astype(BF16), w_ref[...])
        for k in range(N_BR):
            g = g_ref[k].astype(F32)
            dpre_ref[k] = (dm * b_ref[k].astype(F32) * g * (1.0 - g)).astype(BF16)
            dyb_ref[k] = (dm * g).astype(BF16)

    blk = bs((N_BR, tm, BW), lambda i, j: (0, i, j))
    return _call(body, "merge_bwd_pre", (t // tm, N_SH),
                 [bs((tm, D_MODEL), lambda i, j: (i, 0)), bs((BW, D_MODEL), lambda i, j: (j, 0)), blk, blk],
                 [blk, blk], [_sds((N_BR, t, D_MODEL), BF16), _sds((N_BR, t, D_MODEL), BF16)])(dx, wout, gates, ybr)


def branch_out_bwd(dyb, wbr):
    t = dyb.shape[1]
    tm = min(1024, t)

    def body(d_ref, w_ref, o_ref):
        acc = None
        for s in range(N_SH):
            part = _nt(d_ref[:, s * BW:(s + 1) * BW], w_ref[s])
            acc = part if acc is None else acc + part
        o_ref[...] = acc

    return _call(body, "branch_out_bwd", (t // tm, N_BR),
                 [bs((None, tm, D_MODEL), lambda i, k: (k, i, 0)), bs((N_SH, None, BW, BW), lambda i, k: (0, k, 0, 0))],
                 bs((None, tm, BW), lambda i, k: (k, i, 0)), _sds((N_BR, t, BW), F32))(dyb, wbr)


def conva_fwd(proj, wa):
    t = proj.shape[0]
    tm, halo = min(512, t), 8
    per = tm // halo

    def body(b_ref, c_ref, x_ref, ch_ref, xh_ref, w_ref, y_ref):
        zh = jnp.where(pl.program_id(0) > 0, ch_ref[...] * xh_ref[...], 0.0)
        zext = jnp.concatenate([zh, c_ref[...] * x_ref[...]], axis=0)
        y_ref[...] = (b_ref[...] * _causal_conv(zext, w_ref, CONVA_K, halo)).astype(BF16)

    col = lambda c: bs((tm, BW), lambda i: (i, c))
    hal = lambda c: bs((halo, BW), lambda i: (_prev_blk(i, per), c))
    return _call(body, "conva_fwd", (t // tm,),
                 [col(0), col(1), col(2), hal(1), hal(2), bs((CONVA_K, BW), lambda i: (0, 0))],
                 bs((tm, BW), lambda i: (i, 0)), _sds((t, BW), BF16))(proj, proj, proj, proj, proj, wa)


def conva_bwd(proj, dys, wa):
    t = proj.shape[0]
    tm, halo = min(512, t), 8
    per = tm // halo
    last = t // halo - 1
    nt = t // tm

    def body(b_ref, c_ref, x_ref, ch_ref, xh_ref, bn_ref, dy_ref, dyn_ref, w_ref, db_ref, dc_ref, dxx_ref, dw_ref):
        i = pl.program_id(0)

        @pl.when(i == 0)
        def _():
            dw_ref[...] = jnp.zeros_like(dw_ref)

        zh = jnp.where(i > 0, ch_ref[...] * xh_ref[...], 0.0)
        cv, xv = c_ref[...], x_ref[...]
        zext = jnp.concatenate([zh, cv * xv], axis=0)
        dy = dy_ref[...]
        dconv = dy * b_ref[...]
        dcn = jnp.where(i < nt - 1, dyn_ref[...] * bn_ref[...], 0.0)
        dz = _anti_conv(jnp.concatenate([dconv, dcn], axis=0), w_ref, CONVA_K, tm)
        db_ref[...] = (dy * _causal_conv(zext, w_ref, CONVA_K, halo)).astype(BF16)
        dc_ref[...] = (dz * xv).astype(BF16)
        dxx_ref[...] = (dz * cv).astype(BF16)
        _conv_wgrad(dw_ref, dconv, zext, CONVA_K, halo)

    col = lambda c: bs((tm, BW), lambda i: (i, c))
    hal = lambda c: bs((halo, BW), lambda i: (_prev_blk(i, per), c))
    nxt = bs((halo, BW), lambda i: (_next_blk(i, per, last), 0))
    wsp = bs((CONVA_K, BW), lambda i: (0, 0))
    outs = _call(body, "conva_bwd", (t // tm,),
                 [col(0), col(1), col(2), hal(1), hal(2), nxt,
                  bs((None, tm, BW), lambda i: (0, i, 0)), bs((None, halo, BW), lambda i: (0, _next_blk(i, per, last), 0)), wsp],
                 [bs((tm, BW), lambda i: (i, 0))] * 3 + [wsp],
                 [_sds((t, BW), BF16)] * 3 + [_sds((CONVA_K, BW), F32)])(proj, proj, proj, proj, proj, proj, dys, dys, wa)
    return outs[:3], outs[3]


def _head_masks():
    lane = lax.broadcasted_iota(jnp.int32, (1, BW), 1)
    return [(lane >= h * HEAD_D) & (lane < (h + 1) * HEAD_D) for h in range(HEADS)]


def _band_masks():
    qi = lax.broadcasted_iota(jnp.int32, (BLK, BLK), 0)
    ki = lax.broadcasted_iota(jnp.int32, (BLK, BLK), 1)
    return ki >= qi, ki <= qi


def attn_fwd_group(proj, d):
    t = proj.shape[0]
    rows = t // d
    qb = min(512, rows)
    nb = qb // BLK
    scale = HEAD_D ** -0.5
    pv = proj.reshape(rows, d * N_IN)

    def body(q_ref, k_ref, v_ref, kh_ref, vh_ref, o_ref, l64_ref, l128_ref):
        n = pl.program_id(1)
        hm = _head_masks()
        m_prev, m_cur = _band_masks()
        for b in range(nb):
            rs = slice(b * BLK, (b + 1) * BLK)
            q = q_ref[rs, :]
            kc = k_ref[rs, :].astype(BF16)
            vc = v_ref[rs, :].astype(BF16)
            if b == 0:
                kp, vp = kh_ref[...].astype(BF16), vh_ref[...].astype(BF16)
                mp = m_prev & (n > 0)
            else:
                ps = slice((b - 1) * BLK, b * BLK)
                kp, vp = k_ref[ps, :].astype(BF16), v_ref[ps, :].astype(BF16)
                mp = m_prev
            o_acc = jnp.zeros((BLK, BW), F32)
            l_acc = jnp.zeros((BLK, BW), F32)
            for h in range(HEADS):
                qm = jnp.where(hm[h], q, 0.0).astype(BF16)
                sp = jnp.where(mp, _nt(qm, kp) * scale, NEG)
                sc = jnp.where(m_cur, _nt(qm, kc) * scale, NEG)
                m = jnp.maximum(jnp.max(sp, axis=-1, keepdims=True), jnp.max(sc, axis=-1, keepdims=True))
                ep, ec = jnp.exp(sp - m), jnp.exp(sc - m)
                l = jnp.sum(ep, axis=-1, keepdims=True) + jnp.sum(ec, axis=-1, keepdims=True)
                oh = (_nn(ep.astype(BF16), vp) + _nn(ec.astype(BF16), vc)) / l
                lse = m + jnp.log(l)
                o_acc = jnp.where(hm[h], oh, o_acc)
                l_acc = jnp.where(hm[h], lse, l_acc)
                l128_ref[rs, h * BLK:(h + 1) * BLK] = jnp.broadcast_to(lse, (BLK, BLK))
            o_ref[rs, :] = o_acc
            l64_ref[rs, :] = l_acc

    per = qb // BLK
    main = lambda c: bs((qb, BW), lambda r, n: (n, r * 10 + c))
    hal = lambda c: bs((BLK, BW), lambda r, n: (_prev_blk(n, per), r * 10 + c))
    o, l64, l128 = _call(
        body, f"attn_fwd_d{d}", (d, rows // qb),
        [main(3), main(4), main(5), hal(4), hal(5)],
        [bs((qb, BW), lambda r, n: (n, r)), bs((qb, BW), lambda r, n: (n, r)), bs((qb, HEADS * BLK), lambda r, n: (n, r))],
        [_sds((rows, d * BW), F32), _sds((rows, d * BW), F32), _sds((rows, d * HEADS * BLK), F32)])(pv, pv, pv, pv, pv)
    return o.reshape(t, BW), l64.reshape(t, BW), l128.reshape(t, HEADS * BLK)


def attn_merge(os_, l64s, l128s):
    t = os_[0].shape[0]
    tm = min(1024, t)

    def lse3(a, b, c):
        m = jnp.maximum(jnp.maximum(a, b), c)
        return m + jnp.log(jnp.exp(a - m) + jnp.exp(b - m) + jnp.exp(c - m))

    def body(o0, o1, o2, a0, a1, a2, b0, b1, b2, y_ref, o_ref, l_ref):
        ls = [a0[...], a1[...], a2[...]]
        tot = lse3(*ls)
        o = jnp.exp(ls[0] - tot) * o0[...] + jnp.exp(ls[1] - tot) * o1[...] + jnp.exp(ls[2] - tot) * o2[...]
        y_ref[...] = o.astype(BF16)
        o_ref[...] = o
        l_ref[...] = lse3(b0[...], b1[...], b2[...])

    n = bs((tm, BW), lambda i: (i, 0))
    w = bs((tm, HEADS * BLK), lambda i: (i, 0))
    return _call(body, "attn_merge", (t // tm,), [n] * 6 + [w] * 3, [n, n, w],
                 [_sds((t, BW), BF16), _sds((t, BW), F32), _sds((t, HEADS * BLK), F32)])(*os_, *l64s, *l128s)


def attn_delta(dys, o):
    t = o.shape[0]
    tm = min(1024, t)

    def body(d_ref, o_ref, out_ref):
        hm = _head_masks()
        prod = d_ref[...] * o_ref[...]
        for h in range(HEADS):
            s = jnp.sum(jnp.where(hm[h], prod, 0.0), axis=-1, keepdims=True)
            out_ref[:, h * BLK:(h + 1) * BLK] = jnp.broadcast_to(s, (tm, BLK))

    return _call(body, "attn_delta", (t // tm,),
                 [bs((None, tm, BW), lambda i: (1, i, 0)), bs((tm, BW), lambda i: (i, 0))],
                 bs((tm, HEADS * BLK), lambda i: (i, 0)), _sds((t, HEADS * BLK), F32))(dys, o)


def attn_bwd_group(proj, dys, lse, delta, acc, d):
    t = proj.shape[0]
    rows = t // d
    qb = min(512, rows)
    nb = qb // BLK
    nsteps = rows // qb
    scale = HEAD_D ** -0.5
    pv = proj.reshape(rows, d * N_IN)
    dov = dys.reshape(N_BR, rows, d * BW)
    lv = lse.reshape(rows, d * HEADS * BLK)
    dv_ = delta.reshape(rows, d * HEADS * BLK)
    has_acc = acc is not None

    def body(*refs):
        (q_ref, qn_ref, k_ref, kh_ref, v_ref, vh_ref, do_ref, don_ref, l_ref, ln_ref, dl_ref, dln_ref) = refs[:12]
        a_ref = refs[12] if has_acc else None
        o_ref = refs[-1]
        n = pl.program_id(1)
        hm = _head_masks()
        m_prev, m_cur = _band_masks()
        has_prev, has_next = n > 0, n < nsteps - 1
        dq = [jnp.zeros((BLK, BW), F32) for _ in range(nb)]
        dk = [jnp.zeros((BLK, BW), F32) for _ in range(nb)]
        dvv = [jnp.zeros((BLK, BW), F32) for _ in range(nb)]
        for qi in range(nb + 1):
            if qi < nb:
                rs = slice(qi * BLK, (qi + 1) * BLK)
                q, do, lref, dref, rsl = q_ref[rs, :], do_ref[rs, :], l_ref, dl_ref, rs
            else:
                q, do, lref, dref, rsl = qn_ref[...], don_ref[...], ln_ref, dln_ref, slice(0, BLK)
            for ki, band in ((qi - 1, m_prev), (qi, m_cur)):
                if ki >= nb:
                    continue
                if ki < 0:
                    kk, vv, mask = kh_ref[...].astype(BF16), vh_ref[...].astype(BF16), band & has_prev
                else:
                    ks = slice(ki * BLK, (ki + 1) * BLK)
                    kk, vv = k_ref[ks, :].astype(BF16), v_ref[ks, :].astype(BF16)
                    mask = band & has_next if qi == nb else band
                for h in range(HEADS):
                    qm = jnp.where(hm[h], q, 0.0).astype(BF16)
                    dom = jnp.where(hm[h], do, 0.0).astype(BF16)
                    s = _nt(qm, kk) * scale
                    p = jnp.where(mask, jnp.exp(s - lref[rsl, h * BLK:(h + 1) * BLK]), 0.0)
                    ds = (p * (_nt(dom, vv) - dref[rsl, h * BLK:(h + 1) * BLK]) * scale).astype(BF16)
                    if qi < nb:
                        dq[qi] = dq[qi] + jnp.where(hm[h], _nn(ds, kk), 0.0)
                    if ki >= 0:
                        dk[ki] = dk[ki] + _tn(ds, qm)
                        dvv[ki] = dvv[ki] + _tn(p.astype(BF16), dom)
        for b in range(nb):
            rs = slice(b * BLK, (b + 1) * BLK)
            for c, val in enumerate((dq[b], dk[b], dvv[b])):
                cs = slice(c * BW, (c + 1) * BW)
                o_ref[rs, cs] = a_ref[rs, cs] + val if has_acc else val

    per = qb // BLK
    last = rows // BLK - 1
    main = lambda c: bs((qb, BW), lambda r, n: (n, r * 10 + c))
    prv = lambda c: bs((BLK, BW), lambda r, n: (_prev_blk(n, per), r * 10 + c))
    nxt = lambda c: bs((BLK, BW), lambda r, n: (_next_blk(n, per, last), r * 10 + c))
    wide = bs((qb, HEADS * BLK), lambda r, n: (n, r))
    wide_n = bs((BLK, HEADS * BLK), lambda r, n: (_next_blk(n, per, last), r))
    accs = bs((qb, 3 * BW), lambda r, n: (n, r))
    in_specs = [main(3), nxt(3), main(4), prv(4), main(5), prv(5),
                bs((None, qb, BW), lambda r, n: (1, n, r)), bs((None, BLK, BW), lambda r, n: (1, _next_blk(n, per, last), r)),
                wide, wide_n, wide, wide_n]
    args = [pv, pv, pv, pv, pv, pv, dov, dov, lv, lv, dv_, dv_]
    if has_acc:
        in_specs.append(accs)
        args.append(acc.reshape(rows, d * 3 * BW))
    out = _call(body, f"attn_bwd_d{d}", (d, nsteps), in_specs, accs, _sds((rows, d * 3 * BW), F32),
                aliases={12: 0} if has_acc else None)(*args)
    return out.reshape(t, 3 * BW)


def _group_masks():
    lane = lax.broadcasted_iota(jnp.int32, (1, BW), 1)
    return [(lane >= g * HEAD_D) & (lane < (g + 1) * HEAD_D) for g in range(4)]


def sgu_fwd(proj, ln_g, ln_b, w_tril, b_full):
    t = proj.shape[0]
    tm = min(512, t)

    def body(u_ref, v_ref, g_ref, b_ref, w_ref, bf_ref, y_ref):
        gm = _group_masks()
        xhat, _ = _ln_hat(v_ref[...])
        vb = (xhat * g_ref[...] + b_ref[...]).astype(BF16)
        for c in range(tm // BLK):
            rs = slice(c * BLK, (c + 1) * BLK)
            vc = vb[rs, :]
            mixed = bf_ref[...]
            for g in range(4):
                mixed = mixed + jnp.where(gm[g], _nn(w_ref[g], vc), 0.0)
            y_ref[rs, :] = (u_ref[rs, :] * mixed).astype(BF16)

    vec = bs((1, BW), lambda i: (0, 0))
    return _call(body, "sgu_fwd", (t // tm,),
                 [bs((tm, BW), lambda i: (i, 6)), bs((tm, BW), lambda i: (i, 7)), vec, vec,
                  bs((4, BLK, BLK), lambda i: (0, 0, 0)), bs((BLK, BW), lambda i: (0, 0))],
                 bs((tm, BW), lambda i: (i, 0)), _sds((t, BW), BF16))(proj, proj, ln_g, ln_b, w_tril, b_full)


def sgu_bwd(proj, dys, ln_g, ln_b, w_tril, b_full):
    t = proj.shape[0]
    tm = min(512, t)

    def body(u_ref, v_ref, dy_ref, g_ref, b_ref, w_ref, bf_ref, du_ref, dv_ref, dw_ref, dbf_ref, dg_ref, db_ref, dvl_ref):
        @pl.when(pl.program_id(0) == 0)
        def _():
            dw_ref[...] = jnp.zeros_like(dw_ref)
            dbf_ref[...] = jnp.zeros_like(dbf_ref)
            dg_ref[...] = jnp.zeros_like(dg_ref)
            db_ref[...] = jnp.zeros_like(db_ref)

        gm = _group_masks()
        xhat, r = _ln_hat(v_ref[...])
        gv = g_ref[...]
        vb = (xhat * gv + b_ref[...]).astype(BF16)
        for c in range(tm // BLK):
            rs = slice(c * BLK, (c + 1) * BLK)
            vc = vb[rs, :]
            dy = dy_ref[rs, :]
            mixed = bf_ref[...]
            for g in range(4):
                mixed = mixed + jnp.where(gm[g], _nn(w_ref[g], vc), 0.0)
            du_ref[rs, :] = (dy * mixed).astype(BF16)
            dm = dy * u_ref[rs, :]
            dbf_ref[...] += dm
            dvl = jnp.zeros((BLK, BW), F32)
            for g in range(4):
                dmg = jnp.where(gm[g], dm, 0.0).astype(BF16)
                dw_ref[g] += _nt(dmg, vc)
                dvl = dvl + _tn(w_ref[g], dmg)
            dvl_ref[rs, :] = dvl
        dvl = dvl_ref[...]
        dv_ref[...] = _ln_bwd(dvl, xhat, r, gv).astype(BF16)
        dg_ref[...] += _colsum(dvl * xhat)
        db_ref[...] += _colsum(dvl)

    vec = bs((1, BW), lambda i: (0, 0))
    row = bs((tm, BW), lambda i: (i, 0))
    wsp = bs((4, BLK, BLK), lambda i: (0, 0, 0))
    bfs = bs((BLK, BW), lambda i: (0, 0))
    return _call(body, "sgu_bwd", (t // tm,),
                 [bs((tm, BW), lambda i: (i, 6)), bs((tm, BW), lambda i: (i, 7)), bs((None, tm, BW), lambda i: (2, i, 0)),
                  vec, vec, wsp, bfs],
                 [row, row, wsp, bfs, vec, vec],
                 [_sds((t, BW), BF16), _sds((t, BW), BF16), _sds((4, BLK, BLK), F32), _sds((BLK, BW), F32),
                  _sds((1, BW), F32), _sds((1, BW), F32)],
                 scratch=[pltpu.VMEM((tm, BW), F32)])(proj, proj, dys, ln_g, ln_b, w_tril, b_full)


CONF_HALO = 32


def conf_fwd(proj, dw, ln_g, ln_b):
    t = proj.shape[0]
    tm, halo = min(512, t), CONF_HALO
    per = tm // halo

    def body(v_ref, gt_ref, vh_ref, gh_ref, w_ref, g_ref, b_ref, y_ref, z_ref):
        yh = jnp.where(pl.program_id(0) > 0, vh_ref[...] * _sigmoid(gh_ref[...]), 0.0)
        yext = jnp.concatenate([yh, v_ref[...] * _sigmoid(gt_ref[...])], axis=0)
        z = _causal_conv(yext, w_ref, CONF_K, halo)
        z_ref[...] = z
        xhat, _ = _ln_hat(z)
        ln = xhat * g_ref[...] + b_ref[...]
        y_ref[...] = (ln * _sigmoid(ln)).astype(BF16)

    vec = bs((1, BW), lambda i: (0, 0))
    col = lambda c: bs((tm, BW), lambda i: (i, c))
    hal = lambda c: bs((halo, BW), lambda i: (_prev_blk(i, per), c))
    row = bs((tm, BW), lambda i: (i, 0))
    return _call(body, "conf_fwd", (t // tm,),
                 [col(8), col(9), hal(8), hal(9), bs((CONF_K, BW), lambda i: (0, 0)), vec, vec],
                 [row, row], [_sds((t, BW), BF16), _sds((t, BW), F32)])(proj, proj, proj, proj, dw, ln_g, ln_b)


def conf_bwd_ln(z, dys, ln_g, ln_b):
    t = z.shape[0]
    tm = min(1024, t)

    def body(z_ref, dy_ref, g_ref, b_ref, dz_ref, dg_ref, db_ref):
        @pl.when(pl.program_id(0) == 0)
        def _():
            dg_ref[...] = jnp.zeros_like(dg_ref)
            db_ref[...] = jnp.zeros_like(db_ref)

        gv = g_ref[...]
        xhat, r = _ln_hat(z_ref[...])
        ln = xhat * gv + b_ref[...]
        s = _sigmoid(ln)
        dln = dy_ref[...] * (s * (1.0 + ln * (1.0 - s)))
        dz_ref[...] = _ln_bwd(dln, xhat, r, gv)
        dg_ref[...] += _colsum(dln * xhat)
        db_ref[...] += _colsum(dln)

    vec = bs((1, BW), lambda i: (0, 0))
    row = bs((tm, BW), lambda i: (i, 0))
    return _call(body, "conf_bwd_ln", (t // tm,), [row, bs((None, tm, BW), lambda i: (3, i, 0)), vec, vec],
                 [row, vec, vec], [_sds((t, BW), F32), _sds((1, BW), F32), _sds((1, BW), F32)])(z, dys, ln_g, ln_b)


def conf_bwd_conv(proj, dz, dw):
    t = proj.shape[0]
    tm, halo = min(512, t), CONF_HALO
    per = tm // halo
    last = t // halo - 1
    nt = t // tm

    def body(v_ref, gt_ref, vh_ref, gh_ref, dz_ref, dzn_ref, w_ref, dv_ref, dg_ref, dw_ref):
        i = pl.program_id(0)

        @pl.when(i == 0)
        def _():
            dw_ref[...] = jnp.zeros_like(dw_ref)

        val = v_ref[...]
        sg = _sigmoid(gt_ref[...])
        yh = jnp.where(i > 0, vh_ref[...] * _sigmoid(gh_ref[...]), 0.0)
        yext = jnp.concatenate([yh, val * sg], axis=0)
        dz = dz_ref[...]
        dzn = jnp.where(i < nt - 1, dzn_ref[...], 0.0)
        dy0 = _anti_conv(jnp.concatenate([dz, dzn], axis=0), w_ref, CONF_K, tm)
        dv_ref[...] = (dy0 * sg).astype(BF16)
        dg_ref[...] = (dy0 * val * sg * (1.0 - sg)).astype(BF16)
        _conv_wgrad(dw_ref, dz, yext, CONF_K, halo)

    col = lambda c: bs((tm, BW), lambda i: (i, c))
    hal = lambda c: bs((halo, BW), lambda i: (_prev_blk(i, per), c))
    row = bs((tm, BW), lambda i: (i, 0))
    wsp = bs((CONF_K, BW), lambda i: (0, 0))
    return _call(body, "conf_bwd_conv", (t // tm,),
                 [col(8), col(9), hal(8), hal(9), row, bs((halo, BW), lambda i: (_next_blk(i, per, last), 0)), wsp],
                 [row, row, wsp], [_sds((t, BW), BF16), _sds((t, BW), BF16), _sds((CONF_K, BW), F32)])(
                     proj, proj, proj, proj, dz, dz, dw)


ANY = pl.BlockSpec(memory_space=pl.ANY)


def _place():
    return lax.axis_index("x"), lax.axis_index("y"), lax.axis_index("c")


def _comm_call(body, name, n_in, out_shape, scratch):
    return pl.pallas_call(body, name=name, in_specs=[ANY] * n_in, out_specs=[ANY] * len(out_shape), out_shape=out_shape,
                          scratch_shapes=scratch,
                          compiler_params=pltpu.CompilerParams(has_side_effects=True, vmem_limit_bytes=VMEM_LIMIT))


def gather8(v, reduce):
    rows, cols = v.shape

    def body(v_ref, o_ref, land_ref, send, recv, lsem):
        x, y, c = _place()
        me = 4 * x + 2 * y + c
        land = land_ref if reduce else o_ref
        mine = pltpu.make_async_copy(v_ref, land.at[me], lsem)
        mine.start()
        sent = []
        for j in range(1, 8):
            fx, fy, fc = (j >> 2) & 1, (j >> 1) & 1, j & 1
            tgt = (1 - x if fx else x, 1 - y if fy else y, 1 - c if fc else c)
            cp = pltpu.make_async_remote_copy(src_ref=v_ref, dst_ref=land.at[me], send_sem=send.at[j - 1],
                                              recv_sem=recv.at[j - 1], device_id=tgt, device_id_type=MESH)
            cp.start()
            sent.append(cp)
        for j in range(1, 8):
            fx, fy, fc = (j >> 2) & 1, (j >> 1) & 1, j & 1
            peer = 4 * (1 - x if fx else x) + 2 * (1 - y if fy else y) + (1 - c if fc else c)
            pltpu.make_async_remote_copy(src_ref=v_ref, dst_ref=land.at[peer], send_sem=send.at[j - 1],
                                         recv_sem=recv.at[j - 1], device_id=(x, y, c), device_id_type=MESH).wait_recv()
        for cp in sent:
            cp.wait_send()
        mine.wait()
        if reduce:
            acc = land_ref[0]
            for k in range(1, 8):
                acc = acc + land_ref[k]
            o_ref[...] = acc

    vm = pl.BlockSpec(memory_space=pltpu.VMEM)
    out_shape = _sds((rows, cols), F32) if reduce else _sds((8, rows, cols), F32)
    land_shape = (8, rows, cols) if reduce else (8, 128)
    return pl.pallas_call(
        body, name="allreduce8" if reduce else "allgather8", in_specs=[vm], out_specs=vm, out_shape=out_shape,
        scratch_shapes=[pltpu.VMEM(land_shape, F32), pltpu.SemaphoreType.DMA((7,)), pltpu.SemaphoreType.DMA((7,)),
                        pltpu.SemaphoreType.DMA],
        compiler_params=pltpu.CompilerParams(has_side_effects=True, vmem_limit_bytes=VMEM_LIMIT))(v)


def allgather_weights(shards):
    n = len(shards)

    def body(*refs):
        ins, outs = refs[:n], refs[n:2 * n]
        send, recv, lsem = refs[2 * n:]
        x, y, c = _place()
        s_me = 2 * x + y
        chips = [(1 - x, y), (x, 1 - y), (1 - x, 1 - y)]
        sibling = (x, y, 1 - c)
        started, local = [], []
        for a in range(n):
            hl = shards[a].shape[0] // 2
            half = pl.ds(c * hl, hl)
            lc = pltpu.make_async_copy(ins[a], outs[a].at[s_me], lsem.at[a])
            lc.start()
            local.append(lc)
            for j, chip in enumerate(chips):
                cp = pltpu.make_async_remote_copy(src_ref=ins[a].at[half], dst_ref=outs[a].at[s_me, half],
                                                  send_sem=send.at[6 * a + j], recv_sem=recv.at[6 * a + j],
                                                  device_id=(chip[0], chip[1], c), device_id_type=MESH)
                cp.start()
                started.append(cp)
        for a in range(n):
            hl = shards[a].shape[0] // 2
            half = pl.ds(c * hl, hl)
            for j, chip in enumerate(chips):
                s_j = 2 * chip[0] + chip[1]
                landed = outs[a].at[s_j, half]
                pltpu.make_async_remote_copy(src_ref=landed, dst_ref=landed, send_sem=send.at[6 * a + j],
                                             recv_sem=recv.at[6 * a + j], device_id=sibling, device_id_type=MESH).wait_recv()
                fw = pltpu.make_async_remote_copy(src_ref=landed, dst_ref=landed, send_sem=send.at[6 * a + 3 + j],
                                                  recv_sem=recv.at[6 * a + 3 + j], device_id=sibling, device_id_type=MESH)
                fw.start()
                started.append(fw)
        for a in range(n):
            hl = shards[a].shape[0] // 2
            other = pl.ds((1 - c) * hl, hl)
            for j, chip in enumerate(chips):
                s_j = 2 * chip[0] + chip[1]
                theirs = outs[a].at[s_j, other]
                pltpu.make_async_remote_copy(src_ref=theirs, dst_ref=theirs, send_sem=send.at[6 * a + 3 + j],
                                             recv_sem=recv.at[6 * a + 3 + j], device_id=sibling, device_id_type=MESH).wait_recv()
        for cp in started:
            cp.wait_send()
        for lc in local:
            lc.wait()

    out_shape = [_sds((N_SH,) + s.shape, s.dtype) for s in shards]
    scratch = [pltpu.SemaphoreType.DMA((6 * n,)), pltpu.SemaphoreType.DMA((6 * n,)), pltpu.SemaphoreType.DMA((n,))]
    return _comm_call(body, "allgather_weights", n, out_shape, scratch)(*shards)


def _swap_sibling(name, src, pick, out_shape):
    def body(s_ref, o_ref, send, recv):
        x, y, c = _place()
        cp = pltpu.make_async_remote_copy(src_ref=pick(s_ref, c), dst_ref=o_ref, send_sem=send, recv_sem=recv,
                                          device_id=(x, y, 1 - c), device_id_type=MESH)
        cp.start()
        cp.wait()

    return _comm_call(body, name, 1, [out_shape], [pltpu.SemaphoreType.DMA, pltpu.SemaphoreType.DMA])(src)[0]


def _rows_add(name, n_terms, rows, first_spec, other_specs, grid, out_spec, out_shape, scalar, args):
    def body(s_ref, *refs):
        acc = refs[0][...]
        for r in refs[1:n_terms]:
            acc = acc + r[...]
        refs[n_terms][...] = acc

    gs = pltpu.PrefetchScalarGridSpec(num_scalar_prefetch=1, grid=grid, in_specs=[first_spec] + other_specs,
                                      out_specs=out_spec)
    return pl.pallas_call(body, name=name, grid_spec=gs, out_shape=out_shape,
                          compiler_params=pltpu.CompilerParams(dimension_semantics=("arbitrary",) * len(grid),
                                                               vmem_limit_bytes=VMEM_LIMIT))(scalar, *args)


def reduce_scatter_grads(g):
    _, r, cols = g.shape
    hr = r // 2
    tr = 256
    nr = hr // tr
    x, y, c = _place()
    c_arr = jnp.reshape(c, (1,)).astype(jnp.int32)
    s_arr = jnp.reshape(2 * x + y, (1,)).astype(jnp.int32)

    ra = _swap_sibling("rs_swap_halves", g, lambda ref, cc: ref.at[:, pl.ds((1 - cc) * hr, hr)], _sds((N_SH, hr, cols), F32))
    blk = lambda f: bs((None, tr, cols), f)
    pa = _rows_add("rs_add_sibling", 2, tr, blk(lambda s, i, cr: (s, cr[0] * nr + i, 0)), [blk(lambda s, i, cr: (s, i, 0))],
                   (N_SH, nr), blk(lambda s, i, cr: (s, i, 0)), _sds((N_SH, hr, cols), F32), c_arr, [g, ra])

    def ici_body(p_ref, o_ref, send, recv):
        xx, yy, cc = _place()
        chips = [(1 - xx, yy), (xx, 1 - yy), (1 - xx, 1 - yy)]
        cps = []
        for j, chip in enumerate(chips):
            cp = pltpu.make_async_remote_copy(src_ref=p_ref.at[2 * chip[0] + chip[1]], dst_ref=o_ref.at[j],
                                              send_sem=send.at[j], recv_sem=recv.at[j],
                                              device_id=(chip[0], chip[1], cc), device_id_type=MESH)
            cp.start()
            cps.append(cp)
        for cp in cps:
            cp.wait()

    rb = _comm_call(ici_body, "rs_send_partials", 1, [_sds((3, hr, cols), F32)],
                    [pltpu.SemaphoreType.DMA((3,)), pltpu.SemaphoreType.DMA((3,))])(pa)[0]
    fin = _rows_add("rs_add_chips", 4, tr, blk(lambda i, sr: (sr[0], i, 0)),
                    [blk(functools.partial(lambda i, sr, j: (j, i, 0), j=j)) for j in range(3)],
                    (nr,), bs((tr, cols), lambda i, sr: (i, 0)), _sds((hr, cols), F32), s_arr, [pa, rb, rb, rb])

    def bcast_body(f_ref, o_ref, send, recv, lsem):
        xx, yy, cc = _place()
        lc = pltpu.make_async_copy(f_ref, o_ref.at[cc], lsem)
        lc.start()
        cp = pltpu.make_async_remote_copy(src_ref=f_ref, dst_ref=o_ref.at[cc], send_sem=send, recv_sem=recv,
                                          device_id=(xx, yy, 1 - cc), device_id_type=MESH)
        cp.start()
        pltpu.make_async_remote_copy(src_ref=f_ref, dst_ref=o_ref.at[1 - cc], send_sem=send, recv_sem=recv,
                                     device_id=(xx, yy, 1 - cc), device_id_type=MESH).wait_recv()
        cp.wait_send()
        lc.wait()

    full = _comm_call(bcast_body, "rs_share_halves", 1, [_sds((2, hr, cols), F32)],
                      [pltpu.SemaphoreType.DMA, pltpu.SemaphoreType.DMA, pltpu.SemaphoreType.DMA])(fin)[0]
    return full.reshape(r, cols)


def adamw(w, g, m, v):
    shape = w.shape
    cols = shape[-1]
    rows = math.prod(shape[:-1]) if len(shape) > 1 else 1
    tr = 256 if rows % 256 == 0 and rows > 256 else rows
    c1 = 1.0 - ADAM_B1 ** ADAM_STEP
    c2 = 1.0 - ADAM_B2 ** ADAM_STEP

    def body(w_ref, g_ref, m_ref, v_ref, d_ref, nm_ref, nv_ref):
        gv = g_ref[...]
        nm = ADAM_B1 * m_ref[...] + (1.0 - ADAM_B1) * gv
        nv = ADAM_B2 * v_ref[...] + (1.0 - ADAM_B2) * (gv * gv)
        nm_ref[...] = nm
        nv_ref[...] = nv
        d_ref[...] = -ADAM_LR * ((nm / c1) / (jnp.sqrt(nv / c2) + ADAM_EPS) + ADAM_WD * w_ref[...])

    row = bs((tr, cols), lambda i: (i, 0))
    outs = _call(body, "adamw", (rows // tr,), [row] * 4, [row] * 3, [_sds((rows, cols), F32)] * 3)(
        *[a.reshape(rows, cols) for a in (w, g, m, v)])
    return [o.reshape(shape) for o in outs]


def layer_fwd(x, p_i, w):
    h, proj = norm_in_proj(x, w["g_mix"], w["win"])
    ya = conva_fwd(proj, w["conv_a"])
    groups = [attn_fwd_group(proj, d) for d in DILATIONS]
    yb, o32, lse = attn_merge([g[0] for g in groups], [g[1] for g in groups], [g[2] for g in groups])
    yc = sgu_fwd(proj, w["sgu_ln_g"], w["sgu_ln_b"], w["sgu_wt"], w["sgu_bf"])
    yd, z = conf_fwd(proj, w["conf_dw"], w["conf_ln_g"], w["conf_ln_b"])
    ys = (ya, yb, yc, yd)
    merged, gates, ybr = merge_fwd(h, ys, w["wg"], w["wbr"])
    x1 = mm_residual(merged, w["wout"], x, "attn_out")
    h2, fgu, act = ffn_in(x1, w["g_ffn"], w["wfi"])
    x2 = mm_residual(act, w["wfo"], x1, "ffn_out")
    h3, gate, pp, x3 = ple_fwd(x2, w["g_ple"], w["wpg"], p_i, w["wpp"])
    saved = dict(x=x, h=h, proj=proj, ys=ys, o32=o32, lse=lse, z=z, merged=merged, gates=gates, ybr=ybr, x1=x1,
                 h2=h2, fgu=fgu, act=act, x2=x2, h3=h3, gate=gate, pp=pp)
    return x3, saved


def layer_bwd(dx3, p_i, w, s):
    t = dx3.shape[0]
    tr = min(1024, t)
    nr = t // tr
    ns_fi = FFN_H // 2
    big, small = {}, {}

    dpre, dpp = ple_bwd_pre(dx3, s["gate"], s["pp"])
    big["wpg"] = tn_matmul("dw_ple_gate", s["h3"], dpre, (nr,), bs((tr, D_MODEL), lambda r: (r, 0)),
                           bs((tr, D_MODEL), lambda r: (r, 0)), bs((D_MODEL, D_MODEL), lambda r: (0, 0)),
                           _sds((D_MODEL, D_MODEL), F32))
    big["wpp"] = tn_matmul("dw_ple_proj", p_i, dpp, (N_SH, nr), bs((tr, BW), lambda j, r: (r, 0)),
                           bs((tr, BW), lambda j, r: (r, j)), bs((None, BW, BW), lambda j, r: (j, 0, 0)),
                           _sds((N_SH, BW, BW), F32))
    dx2, small["g_ple"] = norm_bwd(
        "ple_norm_bwd",
        [(dpre, lambda tm, loc: bs((tm, D_MODEL), lambda i, k: (i, loc(k))), w["wpg"],
          lambda loc: bs((D_MODEL, D_MODEL), lambda i, k: (0, loc(k))), 1)],
        None, dx3, s["x2"], w["g_ple"])

    df = ffn_bwd_act(dx2, w["wfo"], s["fgu"])
    big["wfo"] = tn_matmul("dw_ffn_out", s["act"], dx2, (2, nr), bs((tr, ns_fi), lambda j, r: (r, j)),
                           bs((tr, D_MODEL), lambda j, r: (r, 0)), bs((ns_fi, D_MODEL), lambda j, r: (j, 0)),
                           _sds((FFN_H, D_MODEL), F32))
    big["wfi"] = tn_matmul("dw_ffn_in", s["h2"], df, (N_SH, nr), bs((tr, D_MODEL), lambda j, r: (r, 0)),
                           bs((None, tr, ns_fi), lambda j, r: (j // 2, r, j % 2)),
                           bs((None, D_MODEL, ns_fi), lambda j, r: (j, 0, 0)), _sds((N_SH, D_MODEL, ns_fi), F32))
    dx1, small["g_ffn"] = norm_bwd(
        "ffn_norm_bwd",
        [(df, lambda tm, loc: bs((None, tm, ns_fi), lambda i, k: (loc(k) // 2, i, loc(k) % 2)), w["wfi"],
          lambda loc: bs((None, D_MODEL, ns_fi), lambda i, k: (loc(k), 0, 0)), N_SH)],
        None, dx2, s["x1"], w["g_ffn"])

    dpre_m, dyb = merge_bwd_pre(dx1, w["wout"], s["gates"], s["ybr"])
    big["wout"] = tn_matmul("dw_out", s["merged"], dx1, (nr,), bs((tr, D_MODEL), lambda r: (r, 0)),
                            bs((tr, D_MODEL), lambda r: (r, 0)), bs((D_MODEL, D_MODEL), lambda r: (0, 0)),
                            _sds((D_MODEL, D_MODEL), F32))
    big["wg"] = tn_matmul("dw_merge_gate", s["h"], dpre_m, (N_BR, nr), bs((tr, D_MODEL), lambda k, r: (r, 0)),
                          bs((None, tr, D_MODEL), lambda k, r: (k, r, 0)),
                          bs((None, D_MODEL, D_MODEL), lambda k, r: (k, 0, 0)), _sds((N_BR, D_MODEL, D_MODEL), F32))
    big["wbr"] = [
        tn_matmul("dw_branch", s["ys"][k], dyb, (nr,), bs((tr, BW), lambda r: (r, 0)),
                  bs((None, tr, D_MODEL), functools.partial(lambda r, kk: (kk, r, 0), kk=k)),
                  bs((BW, D_MODEL), lambda r: (0, 0)), _sds((BW, D_MODEL), F32))
        for k in range(N_BR)]
    dys = branch_out_bwd(dyb, w["wbr"])

    (dab, dac, dax), small["conv_a"] = conva_bwd(s["proj"], dys, w["conv_a"])
    delta = attn_delta(dys, s["o32"])
    acc = None
    for d in DILATIONS:
        acc = attn_bwd_group(s["proj"], dys, s["lse"], delta, acc, d)
    du, dv, d_sw, d_sbf, small["sgu_ln_g"], small["sgu_ln_b"] = sgu_bwd(
        s["proj"], dys, w["sgu_ln_g"], w["sgu_ln_b"], w["sgu_wt"], w["sgu_bf"])
    small["sgu_w"] = jnp.where(jnp.tril(jnp.ones((BLK, BLK), bool))[None], d_sw, 0.0)
    small["sgu_b"] = jnp.sum(d_sbf.reshape(BLK, 4, HEAD_D), axis=-1).T
    dz, small["conf_ln_g"], small["conf_ln_b"] = conf_bwd_ln(s["z"], dys, w["conf_ln_g"], w["conf_ln_b"])
    dval, dgate, small["conf_dw"] = conf_bwd_conv(s["proj"], dz, w["conf_dw"])
    dproj = jnp.concatenate([dab, dac, dax, acc.astype(BF16), du, dv, dval, dgate], axis=1)

    ns_in = N_IN // N_SH
    big["win"] = tn_matmul("dw_in", s["h"], dproj, (N_SH, nr), bs((tr, D_MODEL), lambda j, r: (r, 0)),
                           bs((tr, ns_in), lambda j, r: (r, j)), bs((None, D_MODEL, ns_in), lambda j, r: (j, 0, 0)),
                           _sds((N_SH, D_MODEL, ns_in), F32))
    dx, small["g_mix"] = norm_bwd(
        "mix_norm_bwd",
        [(dpre_m, lambda tm, loc: bs((None, tm, D_MODEL), lambda i, k: (loc(k), i, 0)), w["wg"],
          lambda loc: bs((None, D_MODEL, D_MODEL), lambda i, k: (loc(k), 0, 0)), N_BR),
         (dproj, lambda tm, loc: bs((tm, ns_in), lambda i, k: (i, loc(k))), w["win"],
          lambda loc: bs((None, D_MODEL, ns_in), lambda i, k: (loc(k), 0, 0)), N_SH)],
        None, dx1, s["x"], w["g_mix"])
    return dx, big, small


def pack_big_grads(big):
    c = D_MODEL
    parts = [
        big["win"].reshape(N_SH, -1, c),
        jnp.stack([b.reshape(BW, N_SH, BW).transpose(1, 0, 2) for b in big["wbr"]], axis=1).reshape(N_SH, -1, c),
        big["wg"].reshape(N_BR, N_SH, BW, c).transpose(1, 0, 2, 3).reshape(N_SH, -1, c),
        big["wout"].reshape(N_SH, -1, c),
        big["wfi"].reshape(N_SH, -1, c),
        big["wfo"].reshape(N_SH, -1, c),
        big["wpg"].reshape(N_SH, -1, c),
        big["wpp"].reshape(N_SH, -1, c),
    ]
    return jnp.concatenate(parts, axis=1)


BIG_SHARD_SHAPES = (("w_in", (D_MODEL, N_IN // N_SH)), ("w_branch", (N_BR, BW, BW)), ("w_merge_gate", (N_BR, BW, D_MODEL)),
                    ("w_out", (BW, D_MODEL)), ("w_ffn_in", (D_MODEL, FFN_H // 2)), ("w_ffn_out", (FFN_H // N_SH, D_MODEL)),
                    ("w_ple_gate", (BW, D_MODEL)), ("w_ple_proj", (BW, BW)))


def unpack_big_grads(flat):
    out, row = {}, 0
    for name, shape in BIG_SHARD_SHAPES:
        n = math.prod(shape) // D_MODEL
        out[name] = flat[row:row + n].reshape(shape)
        row += n
    return out


SMALL_NAMES = ("g_mix", "conv_a", "sgu_ln_g", "sgu_ln_b", "sgu_w", "sgu_b", "conf_dw", "conf_ln_g", "conf_ln_b",
               "g_ffn", "g_ple")


def _pack_rows(arrays, rows):
    flat = jnp.concatenate([a.reshape(-1) for a in arrays])
    return jnp.pad(flat, (0, rows * D_MODEL - flat.shape[0])).reshape(rows, D_MODEL)


def _unpack_rows(packed, shapes):
    flat, out, pos = packed.reshape(-1), [], 0
    for shape in shapes:
        n = math.prod(shape)
        out.append(flat[pos:pos + n].reshape(shape))
        pos += n
    return out


def kernel(x, p, g_mix, w_in, conv_a, sgu_ln_g, sgu_ln_b, sgu_w, sgu_b, conf_dw, conf_ln_g, conf_ln_b, w_branch, w_merge_gate, w_out, g_ffn, w_ffn_in, w_ffn_out, g_ple, w_ple_gate, w_ple_proj, g_final, loss_target, m_g_mix, m_w_in, m_conv_a, m_sgu_ln_g, m_sgu_ln_b, m_sgu_w, m_sgu_b, m_conf_dw, m_conf_ln_g, m_conf_ln_b, m_w_branch, m_w_merge_gate, m_w_out, m_g_ffn, m_w_ffn_in, m_w_ffn_out, m_g_ple, m_w_ple_gate, m_w_ple_proj, m_g_final, v_g_mix, v_w_in, v_conv_a, v_sgu_ln_g, v_sgu_ln_b, v_sgu_w, v_sgu_b, v_conf_dw, v_conf_ln_g, v_conf_ln_b, v_w_branch, v_w_merge_gate, v_w_out, v_g_ffn, v_w_ffn_in, v_w_ffn_out, v_g_ple, v_w_ple_gate, v_w_ple_proj, v_g_final):
    weights = dict(g_mix=g_mix, w_in=w_in, conv_a=conv_a, sgu_ln_g=sgu_ln_g, sgu_ln_b=sgu_ln_b, sgu_w=sgu_w, sgu_b=sgu_b,
                   conf_dw=conf_dw, conf_ln_g=conf_ln_g, conf_ln_b=conf_ln_b, w_branch=w_branch, w_merge_gate=w_merge_gate,
                   w_out=w_out, g_ffn=g_ffn, w_ffn_in=w_ffn_in, w_ffn_out=w_ffn_out, g_ple=g_ple, w_ple_gate=w_ple_gate,
                   w_ple_proj=w_ple_proj, g_final=g_final)
    m_in = dict(g_mix=m_g_mix, w_in=m_w_in, conv_a=m_conv_a, sgu_ln_g=m_sgu_ln_g, sgu_ln_b=m_sgu_ln_b, sgu_w=m_sgu_w,
                sgu_b=m_sgu_b, conf_dw=m_conf_dw, conf_ln_g=m_conf_ln_g, conf_ln_b=m_conf_ln_b, w_branch=m_w_branch,
                w_merge_gate=m_w_merge_gate, w_out=m_w_out, g_ffn=m_g_ffn, w_ffn_in=m_w_ffn_in, w_ffn_out=m_w_ffn_out,
                g_ple=m_g_ple, w_ple_gate=m_w_ple_gate, w_ple_proj=m_w_ple_proj, g_final=m_g_final)
    v_in = dict(g_mix=v_g_mix, w_in=v_w_in, conv_a=v_conv_a, sgu_ln_g=v_sgu_ln_g, sgu_ln_b=v_sgu_ln_b, sgu_w=v_sgu_w,
                sgu_b=v_sgu_b, conf_dw=v_conf_dw, conf_ln_g=v_conf_ln_g, conf_ln_b=v_conf_ln_b, w_branch=v_w_branch,
                w_merge_gate=v_w_merge_gate, w_out=v_w_out, g_ffn=v_g_ffn, w_ffn_in=v_w_ffn_in, w_ffn_out=v_w_ffn_out,
                g_ple=v_g_ple, w_ple_gate=v_w_ple_gate, w_ple_proj=v_w_ple_proj, g_final=v_g_final)
    order = ("g_mix", "w_in", "conv_a", "sgu_ln_g", "sgu_ln_b", "sgu_w", "sgu_b", "conf_dw", "conf_ln_g", "conf_ln_b",
             "w_branch", "w_merge_gate", "w_out", "g_ffn", "w_ffn_in", "w_ffn_out", "g_ple", "w_ple_gate", "w_ple_proj",
             "g_final")
    depth = g_mix.shape[0]
    xs, tgt = x[0], loss_target[0]
    cw = BW // N_SH

    conv_rows = 16
    allc = gather8(_pack_rows([conv_a, conf_dw], conv_rows), reduce=False)
    shards = [_unpack_rows(allc[2 * s], [conv_a.shape, conf_dw.shape]) for s in range(N_SH)]
    conv_a_full = jnp.concatenate([sh[0] for sh in shards], axis=-1)
    conf_dw_full = jnp.concatenate([sh[1] for sh in shards], axis=-1)

    tril = jnp.tril(jnp.ones((BLK, BLK), bool))
    layers = []
    for i in range(depth):
        got = allgather_weights(
            [w_in[i].astype(BF16), w_branch[i].astype(BF16)] + [w_merge_gate[i, k].astype(BF16) for k in range(N_BR)]
            + [w_out[i].astype(BF16), w_ffn_in[i].astype(BF16), w_ffn_out[i].astype(BF16), w_ple_gate[i].astype(BF16),
               w_ple_proj[i].astype(BF16)])
        vec = lambda a: a[i].reshape(1, -1)
        layers.append(dict(
            win=got[0], wbr=got[1], wg=jnp.stack([g.reshape(D_MODEL, D_MODEL) for g in got[2:6]]),
            wout=got[6].reshape(D_MODEL, D_MODEL), wfi=got[7], wfo=got[8].reshape(FFN_H, D_MODEL),
            wpg=got[9].reshape(D_MODEL, D_MODEL), wpp=got[10],
            g_mix=vec(g_mix), g_ffn=vec(g_ffn), g_ple=vec(g_ple), conv_a=conv_a_full[i], conf_dw=conf_dw_full[i],
            sgu_ln_g=vec(sgu_ln_g), sgu_ln_b=vec(sgu_ln_b), conf_ln_g=vec(conf_ln_g), conf_ln_b=vec(conf_ln_b),
            sgu_wt=jnp.where(tril[None], sgu_w[i], 0.0).astype(BF16),
            sgu_bf=jnp.repeat(sgu_b[i].T, HEAD_D, axis=1)))

    act, saved = xs, []
    for i in range(depth):
        act, sv = layer_fwd(act, p[i, 0], layers[i])
        saved.append(sv)
    loss_part, dx, dg_final = loss_head(act, g_final.reshape(1, -1), tgt)

    big_red = [None] * depth
    small_parts = [None] * depth
    for i in reversed(range(depth)):
        dx, big, small_parts[i] = layer_bwd(dx, p[i, 0], layers[i], saved[i])
        big_red[i] = unpack_big_grads(reduce_scatter_grads(pack_big_grads(big)))

    small_list = [jnp.stack([small_parts[i][n].reshape(weights[n].shape[1:] if n not in ("conv_a", "conf_dw")
                                                       else small_parts[i][n].shape) for i in range(depth)])
                  for n in SMALL_NAMES]
    small_list += [dg_final.reshape(-1), loss_part[0, :1]]
    small_shapes = [a.shape for a in small_list]
    n_small = sum(math.prod(sh) for sh in small_shapes)
    small_rows = -(-n_small // (8 * D_MODEL)) * 8
    red = _unpack_rows(gather8(_pack_rows(small_list, small_rows), reduce=True), small_shapes)
    grads = dict(zip(SMALL_NAMES, red[:len(SMALL_NAMES)]))
    grads["g_final"] = red[-2]
    loss = red[-1].reshape(())
    my_shard = 2 * lax.axis_index("x") + lax.axis_index("y")
    for n in ("conv_a", "conf_dw"):
        grads[n] = lax.dynamic_slice_in_dim(grads[n], my_shard * cw, cw, axis=2)
    for name, _ in BIG_SHARD_SHAPES:
        grads[name] = jnp.stack([big_red[i][name] for i in range(depth)])

    small_all = [n for n in order if n not in dict(BIG_SHARD_SHAPES)]
    sm_shapes = [weights[n].shape for n in small_all]
    n_sm = sum(math.prod(sh) for sh in sm_shapes)
    sm_rows = -(-n_sm // (8 * D_MODEL)) * 8
    packed = [_pack_rows([src[n] for n in small_all], sm_rows) for src in (weights, grads, m_in, v_in)]
    sm_out = [_unpack_rows(o, sm_shapes) for o in adamw(*packed)]
    delta, new_m, new_v = ({n: o[k] for k, n in enumerate(small_all)} for o in sm_out)
    for name, _ in BIG_SHARD_SHAPES:
        delta[name], new_m[name], new_v[name] = adamw(weights[name], grads[name], m_in[name], v_in[name])

    return (loss, dx[None], *[grads[n] for n in order], *[delta[n] for n in order], *[new_m[n] for n in order],
            *[new_v[n] for n in order])
```

```python
import functools
import math

import jax
import jax.numpy as jnp
from jax import lax
from jax.experimental import pallas as pl
from jax.experimental.pallas import tpu as pltpu

F32 = jnp.float32
BF16 = jnp.bfloat16
EPS = 1e-6
D_MODEL = 1024
BW = 256
N_BR = 4
N_IN = 10 * BW
FFN_H = 2816
N_SH = 4
HEADS = 4
HEAD_D = 64
BLK = 128
DILATIONS = (1, 4, 16)
CONF_K = 31
CONVA_K = 3
NEG = -1e30
VMEM_LIMIT = 56 * 1024 * 1024
MESH = pl.DeviceIdType.MESH

ADAM_LR, ADAM_B1, ADAM_B2, ADAM_EPS, ADAM_WD, ADAM_STEP = 0.001, 0.9, 0.999, 1e-08, 0.01, 10

bs = pl.BlockSpec
ANY = pl.BlockSpec(memory_space=pl.ANY)


def _call(body, name, grid, in_specs, out_specs, out_shape, scratch=(), aliases=None):
    return pl.pallas_call(
        body, name=name, grid=grid, in_specs=in_specs, out_specs=out_specs, out_shape=out_shape,
        scratch_shapes=list(scratch), input_output_aliases=aliases or {},
        compiler_params=pltpu.CompilerParams(dimension_semantics=("arbitrary",) * len(grid),
                                             vmem_limit_bytes=VMEM_LIMIT))


def _sds(shape, dtype):
    return jax.ShapeDtypeStruct(shape, dtype)


def _nn(a, b):
    return jnp.dot(a, b, preferred_element_type=F32)


def _nt(a, b):
    return lax.dot_general(a, b, (((1,), (1,)), ((), ())), preferred_element_type=F32)


def _tn(a, b):
    return lax.dot_general(a, b, (((0,), (0,)), ((), ())), preferred_element_type=F32)


def _sigmoid(x):
    return 1.0 / (1.0 + jnp.exp(-x))


def _rms_fwd(x, g):
    r = lax.rsqrt(jnp.mean(x * x, axis=-1, keepdims=True) + EPS)
    return x * r * g


def _rms_bwd(dh, x, g):
    r = lax.rsqrt(jnp.mean(x * x, axis=-1, keepdims=True) + EPS)
    xr = x * r
    dxr = dh * g
    dx = r * (dxr - xr * jnp.mean(dxr * xr, axis=-1, keepdims=True))
    return dx, dh * xr


def _ln_hat(x):
    mu = jnp.mean(x, axis=-1, keepdims=True)
    xc = x - mu
    r = lax.rsqrt(jnp.mean(xc * xc, axis=-1, keepdims=True) + EPS)
    return xc * r, r


def _ln_bwd(dy, xhat, r, g):
    dxh = dy * g
    return r * (dxh - jnp.mean(dxh, axis=-1, keepdims=True) - xhat * jnp.mean(dxh * xhat, axis=-1, keepdims=True))


def _colsum(v):
    return jnp.sum(v, axis=0, keepdims=True)


def _causal_conv(zext, w_ref, k_taps, halo):
    acc = zext[halo:] * w_ref[k_taps - 1:k_taps, :]
    for k in range(k_taps - 1):
        acc = acc + pltpu.roll(zext, k_taps - 1 - k, 0)[halo:] * w_ref[k:k + 1, :]
    return acc


def _anti_conv(dext, w_ref, k_taps, tm):
    n = dext.shape[0]
    acc = dext[:tm] * w_ref[k_taps - 1:k_taps, :]
    for s in range(1, k_taps):
        acc = acc + pltpu.roll(dext, n - s, 0)[:tm] * w_ref[k_taps - 1 - s:k_taps - s, :]
    return acc


def _conv_wgrad(dw_ref, dc, zext, k_taps, halo):
    dw_ref[k_taps - 1:k_taps, :] += _colsum(dc * zext[halo:])
    for k in range(k_taps - 1):
        dw_ref[k:k + 1, :] += _colsum(dc * pltpu.roll(zext, k_taps - 1 - k, 0)[halo:])


def _prev_blk(i, per):
    return jnp.maximum(i * per - 1, 0)


def _next_blk(i, per, last):
    return jnp.minimum((i + 1) * per, last)


def norm_in_proj(x, g, win):
    t = x.shape[0]
    tm = min(1024, t)
    ns = win.shape[2]

    def body(x_ref, g_ref, w_ref, h_ref, o_ref):
        h = _rms_fwd(x_ref[...], g_ref[...]).astype(BF16)
        h_ref[...] = h
        o_ref[...] = _nn(h, w_ref[...])

    return _call(
        body, "norm_in_proj", (t // tm, N_SH),
        [bs((tm, D_MODEL), lambda i, j: (i, 0)), bs((1, D_MODEL), lambda i, j: (0, 0)),
         bs((None, D_MODEL, ns), lambda i, j: (j, 0, 0))],
        [bs((tm, D_MODEL), lambda i, j: (i, 0)), bs((tm, ns), lambda i, j: (i, j))],
        [_sds((t, D_MODEL), BF16), _sds((t, N_SH * ns), F32)])(x, g, win)


def merge_fwd(h, ys, wg, wbr):
    t = h.shape[0]
    tm = min(1024, t)

    def body(h_ref, ya, yb, yc, yd, wg_ref, wb_ref, m_ref, g_ref, b_ref):
        hh = h_ref[...]
        acc = None
        for k, y_ref in enumerate((ya, yb, yc, yd)):
            g = _sigmoid(_nn(hh, wg_ref[k]))
            b = _nn(y_ref[...], wb_ref[k])
            g_ref[k] = g.astype(BF16)
            b_ref[k] = b.astype(BF16)
            acc = g * b if acc is None else acc + g * b
        m_ref[...] = acc.astype(BF16)

    ysp = bs((tm, BW), lambda i, j: (i, 0))
    return _call(
        body, "merge_fwd", (t // tm, N_SH),
        [bs((tm, D_MODEL), lambda i, j: (i, 0)), ysp, ysp, ysp, ysp,
         bs((N_BR, D_MODEL, BW), lambda i, j: (0, 0, j)), bs((None, N_BR, BW, BW), lambda i, j: (j, 0, 0, 0))],
        [bs((tm, BW), lambda i, j: (i, j)), bs((N_BR, tm, BW), lambda i, j: (0, i, j)),
         bs((N_BR, tm, BW), lambda i, j: (0, i, j))],
        [_sds((t, D_MODEL), BF16), _sds((N_BR, t, D_MODEL), BF16), _sds((N_BR, t, D_MODEL), BF16)])(
            h, *ys, wg, wbr)


def mm_residual(a, w, res, name):
    t, kk = a.shape
    tm, tn = min(1024, t), 512

    def body(a_ref, w_ref, r_ref, o_ref):
        o_ref[...] = r_ref[...] + _nn(a_ref[...], w_ref[...])

    return _call(
        body, name, (t // tm, D_MODEL // tn),
        [bs((tm, kk), lambda i, j: (i, 0)), bs((kk, tn), lambda i, j: (0, j)), bs((tm, tn), lambda i, j: (i, j))],
        bs((tm, tn), lambda i, j: (i, j)), _sds((t, D_MODEL), F32))(a, w, res)


def ffn_in(x, g, wfi):
    t = x.shape[0]
    tm = min(1024, t)
    ns = wfi.shape[2]

    def body(x_ref, g_ref, wg_ref, wu_ref, h_ref, f_ref, a_ref):
        h = _rms_fwd(x_ref[...], g_ref[...]).astype(BF16)
        h_ref[...] = h
        fg = _nn(h, wg_ref[...])
        fu = _nn(h, wu_ref[...])
        f_ref[0] = fg.astype(BF16)
        f_ref[1] = fu.astype(BF16)
        a_ref[...] = (fg * _sigmoid(fg) * fu).astype(BF16)

    return _call(
        body, "ffn_in", (t // tm, 2),
        [bs((tm, D_MODEL), lambda i, j: (i, 0)), bs((1, D_MODEL), lambda i, j: (0, 0)),
         bs((None, D_MODEL, ns), lambda i, j: (j, 0, 0)), bs((None, D_MODEL, ns), lambda i, j: (j + 2, 0, 0))],
        [bs((tm, D_MODEL), lambda i, j: (i, 0)), bs((2, tm, ns), lambda i, j: (0, i, j)),
         bs((tm, ns), lambda i, j: (i, j))],
        [_sds((t, D_MODEL), BF16), _sds((2, t, FFN_H), BF16), _sds((t, FFN_H), BF16)])(x, g, wfi, wfi)


def ple_fwd(x, g, wpg, p_i, wpp):
    t = x.shape[0]
    tm = min(1024, t)

    def body(x_ref, xt_ref, g_ref, wg_ref, p_ref, wp_ref, h_ref, gt_ref, pp_ref, o_ref):
        h = _rms_fwd(x_ref[...], g_ref[...]).astype(BF16)
        h_ref[...] = h
        gate = _sigmoid(_nn(h, wg_ref[...]))
        pp = _nn(p_ref[...].astype(BF16), wp_ref[...])
        gt_ref[...] = gate.astype(BF16)
        pp_ref[...] = pp.astype(BF16)
        o_ref[...] = xt_ref[...] + gate * pp

    tile = bs((tm, BW), lambda i, j: (i, j))
    return _call(
        body, "ple_fwd", (t // tm, N_SH),
        [bs((tm, D_MODEL), lambda i, j: (i, 0)), tile, bs((1, D_MODEL), lambda i, j: (0, 0)),
         bs((D_MODEL, BW), lambda i, j: (0, j)), bs((tm, BW), lambda i, j: (i, 0)),
         bs((None, BW, BW), lambda i, j: (j, 0, 0))],
        [bs((tm, D_MODEL), lambda i, j: (i, 0)), tile, tile, tile],
        [_sds((t, D_MODEL), BF16), _sds((t, D_MODEL), BF16), _sds((t, D_MODEL), BF16), _sds((t, D_MODEL), F32)])(
            x, x, g, wpg, p_i, wpp)


def loss_head(x, g, tgt):
    t = x.shape[0]
    tm = min(512, t)

    def body(x_ref, g_ref, t_ref, l_ref, dx_ref, dg_ref):
        @pl.when(pl.program_id(0) == 0)
        def _():
            l_ref[...] = jnp.zeros_like(l_ref)
            dg_ref[...] = jnp.zeros_like(dg_ref)

        xv, gv = x_ref[...], g_ref[...]
        err = _rms_fwd(xv, gv) - t_ref[...]
        part = 0.5 * jnp.sum(jnp.mean(err * err, axis=-1, keepdims=True), axis=0, keepdims=True)
        l_ref[...] += jnp.broadcast_to(part, l_ref.shape)
        dx, dgr = _rms_bwd(err * (1.0 / D_MODEL), xv, gv)
        dx_ref[...] = dx
        dg_ref[...] += _colsum(dgr)

    row = bs((tm, D_MODEL), lambda i: (i, 0))
    vec = bs((1, D_MODEL), lambda i: (0, 0))
    return _call(body, "loss_head", (t // tm,), [row, vec, row],
                 [bs((1, 128), lambda i: (0, 0)), row, vec],
                 [_sds((1, 128), F32), _sds((t, D_MODEL), F32), _sds((1, D_MODEL), F32)])(x, g, tgt)


def tn_matmul(name, a, b, grid, a_spec, b_spec, out_spec, out_shape, split=0, into=None):
    last = len(grid) - 1

    def body(a_ref, b_ref, *rest):
        o_ref = rest[-1]

        @pl.when(pl.program_id(last) == 0)
        def _():
            o_ref[...] = jnp.zeros_like(o_ref)

        res = _tn(a_ref[...].astype(BF16), b_ref[...].astype(BF16))
        if split:
            rows = res.shape[0] // split
            for s in range(split):
                o_ref[s] += res[s * rows:(s + 1) * rows]
        else:
            o_ref[...] += res

    if into is None:
        return _call(body, name, grid, [a_spec, b_spec], out_spec, out_shape)(a, b)
    return _call(body, name, grid, [a_spec, b_spec, ANY], out_spec, out_shape, aliases={2: 0})(a, b, into)


def norm_bwd(name, sources, add, dx_in, x, g):
    t = x.shape[0]
    tm = min(512, t)
    offs, nk = [], 0
    for s in sources:
        offs.append(nk)
        nk += s[4]
    n_src = len(sources)
    has_add = add is not None

    def body(*refs):
        a_refs = refs[0:2 * n_src:2]
        w_refs = refs[1:2 * n_src:2]
        pos = 2 * n_src
        add_ref = refs[pos] if has_add else None
        pos += int(has_add)
        dxi_ref, x_ref, g_ref, dx_ref, dg_ref, acc_ref = refs[pos:pos + 6]
        i, k = pl.program_id(0), pl.program_id(1)

        @pl.when((i == 0) & (k == 0))
        def _():
            dg_ref[...] = jnp.zeros_like(dg_ref)

        @pl.when(k == 0)
        def _():
            acc_ref[...] = add_ref[...] if has_add else jnp.zeros_like(acc_ref)

        for si in range(n_src):
            @pl.when((k >= offs[si]) & (k < offs[si] + sources[si][4]))
            def _(si=si):
                acc_ref[...] += _nt(a_refs[si][...], w_refs[si][...])

        @pl.when(k == nk - 1)
        def _():
            dx, dgr = _rms_bwd(acc_ref[...], x_ref[...], g_ref[...])
            dx_ref[...] = dxi_ref[...] + dx
            dg_ref[...] += _colsum(dgr)

    in_specs, args = [], []
    for si, (a, a_spec, w, w_spec, steps) in enumerate(sources):
        loc = functools.partial(lambda k, o, n: jnp.clip(k - o, 0, n - 1), o=offs[si], n=steps)
        in_specs.append(a_spec(tm, loc))
        in_specs.append(w_spec(loc))
        args += [a, w]
    row = bs((tm, D_MODEL), lambda i, k: (i, 0))
    vec = bs((1, D_MODEL), lambda i, k: (0, 0))
    if has_add:
        in_specs.append(row)
        args.append(add)
    in_specs += [row, row, vec]
    args += [dx_in, x, g]
    return _call(body, name, (t // tm, nk), in_specs, [row, vec],
                 [_sds((t, D_MODEL), F32), _sds((1, D_MODEL), F32)],
                 scratch=[pltpu.VMEM((tm, D_MODEL), F32)])(*args)


def ple_bwd_pre(dx, gate, pp):
    t = dx.shape[0]
    tm = min(1024, t)

    def body(dx_ref, g_ref, p_ref, dpre_ref, dpp_ref):
        d = dx_ref[...]
        g = g_ref[...].astype(F32)
        dpre_ref[...] = (d * p_ref[...].astype(F32) * g * (1.0 - g)).astype(BF16)
        dpp_ref[...] = (d * g).astype(BF16)

    row = bs((tm, D_MODEL), lambda i: (i, 0))
    return _call(body, "ple_bwd_pre", (t // tm,), [row, row, row], [row, row],
                 [_sds((t, D_MODEL), BF16), _sds((t, D_MODEL), BF16)])(dx, gate, pp)


def ffn_bwd_act(dx, wfo, fgu):
    t = dx.shape[0]
    tm = min(1024, t)
    ns = FFN_H // 2

    def body(dx_ref, w_ref, f_ref, o_ref):
        dact = _nt(dx_ref[...].astype(BF16), w_ref[...])
        fg = f_ref[0].astype(F32)
        fu = f_ref[1].astype(F32)
        s = _sigmoid(fg)
        o_ref[0] = (dact * fu * (s * (1.0 + fg * (1.0 - s)))).astype(BF16)
        o_ref[1] = (dact * fg * s).astype(BF16)

    blk = bs((2, tm, ns), lambda i, j: (0, i, j))
    return _call(body, "ffn_bwd_act", (t // tm, 2),
                 [bs((tm, D_MODEL), lambda i, j: (i, 0)), bs((ns, D_MODEL), lambda i, j: (j, 0)), blk],
                 blk, _sds((2, t, FFN_H), BF16))(dx, wfo, fgu)


def merge_bwd_pre(dx, wout, gates, ybr):
    t = dx.shape[0]
    tm = min(1024, t)

    def body(dx_ref, w_ref, g_ref, b_ref, dpre_ref, dyb_ref):
        dm = _nt(dx_ref[...].astype(BF16), w_ref[...])
        for k in range(N_BR):
            g = g_ref[k].astype(F32)
            dpre_ref[k] = (dm * b_ref[k].astype(F32) * g * (1.0 - g)).astype(BF16)
            dyb_ref[k] = (dm * g).astype(BF16)

    blk = bs((N_BR, tm, BW), lambda i, j: (0, i, j))
    return _call(body, "merge_bwd_pre", (t // tm, N_SH),
                 [bs((tm, D_MODEL), lambda i, j: (i, 0)), bs((BW, D_MODEL), lambda i, j: (j, 0)), blk, blk],
                 [blk, blk], [_sds((N_BR, t, D_MODEL), BF16), _sds((N_BR, t, D_MODEL), BF16)])(dx, wout, gates, ybr)


def branch_out_bwd(dyb, wbr):
    t = dyb.shape[1]
    tm = min(1024, t)

    def body(d_ref, w_ref, o_ref):
        acc = None
        for s in range(N_SH):
            part = _nt(d_ref[:, s * BW:(s + 1) * BW], w_ref[s])
            acc = part if acc is None else acc + part
        o_ref[...] = acc

    return _call(body, "branch_out_bwd", (t // tm, N_BR),
                 [bs((None, tm, D_MODEL), lambda i, k: (k, i, 0)), bs((N_SH, None, BW, BW), lambda i, k: (0, k, 0, 0))],
                 bs((None, tm, BW), lambda i, k: (k, i, 0)), _sds((N_BR, t, BW), F32))(dyb, wbr)


def conva_fwd(proj, wa):
    t = proj.shape[0]
    tm, halo = min(512, t), 8
    per = tm // halo

    def body(b_ref, c_ref, x_ref, ch_ref, xh_ref, w_ref, y_ref):
        zh = jnp.where(pl.program_id(0) > 0, ch_ref[...] * xh_ref[...], 0.0)
        zext = jnp.concatenate([zh, c_ref[...] * x_ref[...]], axis=0)
        y_ref[...] = (b_ref[...] * _causal_conv(zext, w_ref, CONVA_K, halo)).astype(BF16)

    col = lambda c: bs((tm, BW), lambda i: (i, c))
    hal = lambda c: bs((halo, BW), lambda i: (_prev_blk(i, per), c))
    return _call(body, "conva_fwd", (t // tm,),
                 [col(0), col(1), col(2), hal(1), hal(2), bs((CONVA_K, BW), lambda i: (0, 0))],
                 bs((tm, BW), lambda i: (i, 0)), _sds((t, BW), BF16))(proj, proj, proj, proj, proj, wa)


def conva_bwd(proj, dys, wa):
    t = proj.shape[0]
    tm, halo = min(512, t), 8
    per = tm // halo
    last = t // halo - 1
    nt = t // tm

    def body(b_ref, c_ref, x_ref, ch_ref, xh_ref, bn_ref, dy_ref, dyn_ref, w_ref, db_ref, dc_ref, dxx_ref, dw_ref):
        i = pl.program_id(0)

        @pl.when(i == 0)
        def _():
            dw_ref[...] = jnp.zeros_like(dw_ref)

        zh = jnp.where(i > 0, ch_ref[...] * xh_ref[...], 0.0)
        cv, xv = c_ref[...], x_ref[...]
        zext = jnp.concatenate([zh, cv * xv], axis=0)
        dy = dy_ref[...]
        dconv = dy * b_ref[...]
        dcn = jnp.where(i < nt - 1, dyn_ref[...] * bn_ref[...], 0.0)
        dz = _anti_conv(jnp.concatenate([dconv, dcn], axis=0), w_ref, CONVA_K, tm)
        db_ref[...] = (dy * _causal_conv(zext, w_ref, CONVA_K, halo)).astype(BF16)
        dc_ref[...] = (dz * xv).astype(BF16)
        dxx_ref[...] = (dz * cv).astype(BF16)
        _conv_wgrad(dw_ref, dconv, zext, CONVA_K, halo)

    col = lambda c: bs((tm, BW), lambda i: (i, c))
    hal = lambda c: bs((halo, BW), lambda i: (_prev_blk(i, per), c))
    nxt = bs((halo, BW), lambda i: (_next_blk(i, per, last), 0))
    wsp = bs((CONVA_K, BW), lambda i: (0, 0))
    outs = _call(body, "conva_bwd", (t // tm,),
                 [col(0), col(1), col(2), hal(1), hal(2), nxt,
                  bs((None, tm, BW), lambda i: (0, i, 0)), bs((None, halo, BW), lambda i: (0, _next_blk(i, per, last), 0)), wsp],
                 [bs((tm, BW), lambda i: (i, 0))] * 3 + [wsp],
                 [_sds((t, BW), BF16)] * 3 + [_sds((CONVA_K, BW), F32)])(proj, proj, proj, proj, proj, proj, dys, dys, wa)
    return outs[:3], outs[3]


def _head_masks():
    lane = lax.broadcasted_iota(jnp.int32, (1, BW), 1)
    return [(lane >= h * HEAD_D) & (lane < (h + 1) * HEAD_D) for h in range(HEADS)]


def _band_masks():
    qi = lax.broadcasted_iota(jnp.int32, (BLK, BLK), 0)
    ki = lax.broadcasted_iota(jnp.int32, (BLK, BLK), 1)
    return ki >= qi, ki <= qi


def qkv_cast(proj):
    t = proj.shape[0]
    tm = min(1024, t)

    def body(p_ref, o_ref):
        o_ref[...] = p_ref[...].astype(BF16)

    return _call(body, "qkv_cast", (t // tm, 3), [bs((tm, BW), lambda i, j: (i, j + 3))],
                 bs((tm, BW), lambda i, j: (i, j)), _sds((t, 3 * BW), BF16))(proj)


def attn_fwd_group(qkv, d):
    t = qkv.shape[0]
    rows = t // d
    qb = min(512, rows)
    nb = qb // BLK
    scale = HEAD_D ** -0.5
    pv = qkv.reshape(rows, d * 3 * BW)

    def body(q_ref, k_ref, v_ref, kh_ref, vh_ref, o_ref, l64_ref, l128_ref):
        n = pl.program_id(1)
        hm = _head_masks()
        m_prev, m_cur = _band_masks()
        for b in range(nb):
            rs = slice(b * BLK, (b + 1) * BLK)
            q = q_ref[rs, :]
            kc = k_ref[rs, :].astype(BF16)
            vc = v_ref[rs, :].astype(BF16)
            if b == 0:
                kp, vp = kh_ref[...].astype(BF16), vh_ref[...].astype(BF16)
                mp = m_prev & (n > 0)
            else:
                ps = slice((b - 1) * BLK, b * BLK)
                kp, vp = k_ref[ps, :].astype(BF16), v_ref[ps, :].astype(BF16)
                mp = m_prev
            o_acc = jnp.zeros((BLK, BW), F32)
            l_acc = jnp.zeros((BLK, BW), F32)
            for h in range(HEADS):
                qm = jnp.where(hm[h], q, 0.0).astype(BF16)
                sp = jnp.where(mp, _nt(qm, kp) * scale, NEG)
                sc = jnp.where(m_cur, _nt(qm, kc) * scale, NEG)
                m = jnp.maximum(jnp.max(sp, axis=-1, keepdims=True), jnp.max(sc, axis=-1, keepdims=True))
                ep, ec = jnp.exp(sp - m), jnp.exp(sc - m)
                l = jnp.sum(ep, axis=-1, keepdims=True) + jnp.sum(ec, axis=-1, keepdims=True)
                oh = (_nn(ep.astype(BF16), vp) + _nn(ec.astype(BF16), vc)) / l
                lse = m + jnp.log(l)
                o_acc = jnp.where(hm[h], oh, o_acc)
                l_acc = jnp.where(hm[h], lse, l_acc)
                l128_ref[rs, h * BLK:(h + 1) * BLK] = jnp.broadcast_to(lse, (BLK, BLK))
            o_ref[rs, :] = o_acc
            l64_ref[rs, :] = l_acc

    per = qb // BLK
    main = lambda c: bs((qb, BW), lambda r, n: (n, r * 3 + c))
    hal = lambda c: bs((BLK, BW), lambda r, n: (_prev_blk(n, per), r * 3 + c))
    o, l64, l128 = _call(
        body, f"attn_fwd_d{d}", (d, rows // qb),
        [main(0), main(1), main(2), hal(1), hal(2)],
        [bs((qb, BW), lambda r, n: (n, r)), bs((qb, BW), lambda r, n: (n, r)), bs((qb, HEADS * BLK), lambda r, n: (n, r))],
        [_sds((rows, d * BW), F32), _sds((rows, d * BW), F32), _sds((rows, d * HEADS * BLK), F32)])(pv, pv, pv, pv, pv)
    return o.reshape(t, BW), l64.reshape(t, BW), l128.reshape(t, HEADS * BLK)


def attn_merge(os_, l64s, l128s):
    t = os_[0].shape[0]
    tm = min(1024, t)

    def lse3(a, b, c):
        m = jnp.maximum(jnp.maximum(a, b), c)
        return m + jnp.log(jnp.exp(a - m) + jnp.exp(b - m) + jnp.exp(c - m))

    def body(o0, o1, o2, a0, a1, a2, b0, b1, b2, y_ref, o_ref, l_ref):
        ls = [a0[...], a1[...], a2[...]]
        tot = lse3(*ls)
        o = jnp.exp(ls[0] - tot) * o0[...] + jnp.exp(ls[1] - tot) * o1[...] + jnp.exp(ls[2] - tot) * o2[...]
        y_ref[...] = o.astype(BF16)
        o_ref[...] = o
        l_ref[...] = lse3(b0[...], b1[...], b2[...])

    n = bs((tm, BW), lambda i: (i, 0))
    w = bs((tm, HEADS * BLK), lambda i: (i, 0))
    return _call(body, "attn_merge", (t // tm,), [n] * 6 + [w] * 3, [n, n, w],
                 [_sds((t, BW), BF16), _sds((t, BW), F32), _sds((t, HEADS * BLK), F32)])(*os_, *l64s, *l128s)


def attn_delta(dy, o):
    t = o.shape[0]
    tm = min(1024, t)

    def body(d_ref, o_ref, out_ref):
        hm = _head_masks()
        prod = d_ref[...] * o_ref[...]
        for h in range(HEADS):
            s = jnp.sum(jnp.where(hm[h], prod, 0.0), axis=-1, keepdims=True)
            out_ref[:, h * BLK:(h + 1) * BLK] = jnp.broadcast_to(s, (tm, BLK))

    return _call(body, "attn_delta", (t // tm,),
                 [bs((tm, BW), lambda i: (i, 0)), bs((tm, BW), lambda i: (i, 0))],
                 bs((tm, HEADS * BLK), lambda i: (i, 0)), _sds((t, HEADS * BLK), F32))(dy, o)


def attn_bwd_group(qkv, dy, lse, delta, acc, d):
    t = qkv.shape[0]
    rows = t // d
    qb = min(512, rows)
    nb = qb // BLK
    nsteps = rows // qb
    scale = HEAD_D ** -0.5
    pv = qkv.reshape(rows, d * 3 * BW)
    dov = dy.reshape(rows, d * BW)
    lv = lse.reshape(rows, d * HEADS * BLK)
    dv_ = delta.reshape(rows, d * HEADS * BLK)
    has_acc = acc is not None

    def body(*refs):
        (q_ref, qn_ref, k_ref, kh_ref, v_ref, vh_ref, do_ref, don_ref, l_ref, ln_ref, dl_ref, dln_ref) = refs[:12]
        a_ref = refs[12] if has_acc else None
        o_ref = refs[-1]
        n = pl.program_id(1)
        hm = _head_masks()
        m_prev, m_cur = _band_masks()
        has_prev, has_next = n > 0, n < nsteps - 1
        dq = [jnp.zeros((BLK, BW), F32) for _ in range(nb)]
        dk = [jnp.zeros((BLK, BW), F32) for _ in range(nb)]
        dvv = [jnp.zeros((BLK, BW), F32) for _ in range(nb)]
        for qi in range(nb + 1):
            if qi < nb:
                rs = slice(qi * BLK, (qi + 1) * BLK)
                q, do, lref, dref, rsl = q_ref[rs, :], do_ref[rs, :], l_ref, dl_ref, rs
            else:
                q, do, lref, dref, rsl = qn_ref[...], don_ref[...], ln_ref, dln_ref, slice(0, BLK)
            for ki, band in ((qi - 1, m_prev), (qi, m_cur)):
                if ki >= nb:
                    continue
                if ki < 0:
                    kk, vv, mask = kh_ref[...].astype(BF16), vh_ref[...].astype(BF16), band & has_prev
                else:
                    ks = slice(ki * BLK, (ki + 1) * BLK)
                    kk, vv = k_ref[ks, :].astype(BF16), v_ref[ks, :].astype(BF16)
                    mask = band & has_next if qi == nb else band
                for h in range(HEADS):
                    qm = jnp.where(hm[h], q, 0.0).astype(BF16)
                    dom = jnp.where(hm[h], do, 0.0).astype(BF16)
                    s = _nt(qm, kk) * scale
                    p = jnp.where(mask, jnp.exp(s - lref[rsl, h * BLK:(h + 1) * BLK]), 0.0)
                    ds = (p * (_nt(dom, vv) - dref[rsl, h * BLK:(h + 1) * BLK]) * scale).astype(BF16)
                    if qi < nb:
                        dq[qi] = dq[qi] + jnp.where(hm[h], _nn(ds, kk), 0.0)
                    if ki >= 0:
                        dk[ki] = dk[ki] + _tn(ds, qm)
                        dvv[ki] = dvv[ki] + _tn(p.astype(BF16), dom)
        for b in range(nb):
            rs = slice(b * BLK, (b + 1) * BLK)
            for c, val in enumerate((dq[b], dk[b], dvv[b])):
                cs = slice(c * BW, (c + 1) * BW)
                o_ref[rs, cs] = a_ref[rs, cs] + val if has_acc else val

    per = qb // BLK
    last = rows // BLK - 1
    main = lambda c: bs((qb, BW), lambda r, n: (n, r * 3 + c))
    prv = lambda c: bs((BLK, BW), lambda r, n: (_prev_blk(n, per), r * 3 + c))
    nxt = lambda c: bs((BLK, BW), lambda r, n: (_next_blk(n, per, last), r * 3 + c))
    wide = bs((qb, HEADS * BLK), lambda r, n: (n, r))
    wide_n = bs((BLK, HEADS * BLK), lambda r, n: (_next_blk(n, per, last), r))
    accs = bs((qb, 3 * BW), lambda r, n: (n, r))
    in_specs = [main(0), nxt(0), main(1), prv(1), main(2), prv(2),
                bs((qb, BW), lambda r, n: (n, r)), bs((BLK, BW), lambda r, n: (_next_blk(n, per, last), r)),
                wide, wide_n, wide, wide_n]
    args = [pv, pv, pv, pv, pv, pv, dov, dov, lv, lv, dv_, dv_]
    if has_acc:
        in_specs.append(accs)
        args.append(acc.reshape(rows, d * 3 * BW))
    out = _call(body, f"attn_bwd_d{d}", (d, nsteps), in_specs, accs, _sds((rows, d * 3 * BW), F32),
                aliases={12: 0} if has_acc else None)(*args)
    return out.reshape(t, 3 * BW)


def _group_masks():
    lane = lax.broadcasted_iota(jnp.int32, (1, BW), 1)
    return [(lane >= g * HEAD_D) & (lane < (g + 1) * HEAD_D) for g in range(4)]


def sgu_fwd(proj, ln_g, ln_b, w_tril, b_full):
    t = proj.shape[0]
    tm = min(512, t)

    def body(u_ref, v_ref, g_ref, b_ref, w_ref, bf_ref, y_ref):
        gm = _group_masks()
        xhat, _ = _ln_hat(v_ref[...])
        vb = (xhat * g_ref[...] + b_ref[...]).astype(BF16)
        for c in range(tm // BLK):
            rs = slice(c * BLK, (c + 1) * BLK)
            vc = vb[rs, :]
            mixed = bf_ref[...]
            for g in range(4):
                mixed = mixed + jnp.where(gm[g], _nn(w_ref[g], vc), 0.0)
            y_ref[rs, :] = (u_ref[rs, :] * mixed).astype(BF16)

    vec = bs((1, BW), lambda i: (0, 0))
    return _call(body, "sgu_fwd", (t // tm,),
                 [bs((tm, BW), lambda i: (i, 6)), bs((tm, BW), lambda i: (i, 7)), vec, vec,
                  bs((4, BLK, BLK), lambda i: (0, 0, 0)), bs((BLK, BW), lambda i: (0, 0))],
                 bs((tm, BW), lambda i: (i, 0)), _sds((t, BW), BF16))(proj, proj, ln_g, ln_b, w_tril, b_full)


def sgu_bwd(proj, dys, ln_g, ln_b, w_tril, b_full):
    t = proj.shape[0]
    tm = min(512, t)

    def body(u_ref, v_ref, dy_ref, g_ref, b_ref, w_ref, bf_ref, du_ref, dv_ref, dw_ref, dbf_ref, dg_ref, db_ref, dvl_ref):
        @pl.when(pl.program_id(0) == 0)
        def _():
            dw_ref[...] = jnp.zeros_like(dw_ref)
            dbf_ref[...] = jnp.zeros_like(dbf_ref)
            dg_ref[...] = jnp.zeros_like(dg_ref)
            db_ref[...] = jnp.zeros_like(db_ref)

        gm = _group_masks()
        xhat, r = _ln_hat(v_ref[...])
        gv = g_ref[...]
        vb = (xhat * gv + b_ref[...]).astype(BF16)
        for c in range(tm // BLK):
            rs = slice(c * BLK, (c + 1) * BLK)
            vc = vb[rs, :]
            dy = dy_ref[rs, :]
            mixed = bf_ref[...]
            for g in range(4):
                mixed = mixed + jnp.where(gm[g], _nn(w_ref[g], vc), 0.0)
            du_ref[rs, :] = (dy * mixed).astype(BF16)
            dm = dy * u_ref[rs, :]
            dbf_ref[...] += dm
            dvl = jnp.zeros((BLK, BW), F32)
            for g in range(4):
                dmg = jnp.where(gm[g], dm, 0.0).astype(BF16)
                dw_ref[g] += _nt(dmg, vc)
                dvl = dvl + _tn(w_ref[g], dmg)
            dvl_ref[rs, :] = dvl
        dvl = dvl_ref[...]
        dv_ref[...] = _ln_bwd(dvl, xhat, r, gv).astype(BF16)
        dg_ref[...] += _colsum(dvl * xhat)
        db_ref[...] += _colsum(dvl)

    vec = bs((1, BW), lambda i: (0, 0))
    row = bs((tm, BW), lambda i: (i, 0))
    wsp = bs((4, BLK, BLK), lambda i: (0, 0, 0))
    bfs = bs((BLK, BW), lambda i: (0, 0))
    return _call(body, "sgu_bwd", (t // tm,),
                 [bs((tm, BW), lambda i: (i, 6)), bs((tm, BW), lambda i: (i, 7)), bs((None, tm, BW), lambda i: (2, i, 0)),
                  vec, vec, wsp, bfs],
                 [row, row, wsp, bfs, vec, vec],
                 [_sds((t, BW), BF16), _sds((t, BW), BF16), _sds((4, BLK, BLK), F32), _sds((BLK, BW), F32),
                  _sds((1, BW), F32), _sds((1, BW), F32)],
                 scratch=[pltpu.VMEM((tm, BW), F32)])(proj, proj, dys, ln_g, ln_b, w_tril, b_full)


CONF_HALO = 32


def conf_fwd(proj, dw, ln_g, ln_b):
    t = proj.shape[0]
    tm, halo = min(512, t), CONF_HALO
    per = tm // halo

    def body(v_ref, gt_ref, vh_ref, gh_ref, w_ref, g_ref, b_ref, y_ref, z_ref):
        yh = jnp.where(pl.program_id(0) > 0, vh_ref[...] * _sigmoid(gh_ref[...]), 0.0)
        yext = jnp.concatenate([yh, v_ref[...] * _sigmoid(gt_ref[...])], axis=0)
        z = _causal_conv(yext, w_ref, CONF_K, halo)
        z_ref[...] = z
        xhat, _ = _ln_hat(z)
        ln = xhat * g_ref[...] + b_ref[...]
        y_ref[...] = (ln * _sigmoid(ln)).astype(BF16)

    vec = bs((1, BW), lambda i: (0, 0))
    col = lambda c: bs((tm, BW), lambda i: (i, c))
    hal = lambda c: bs((halo, BW), lambda i: (_prev_blk(i, per), c))
    row = bs((tm, BW), lambda i: (i, 0))
    return _call(body, "conf_fwd", (t // tm,),
                 [col(8), col(9), hal(8), hal(9), bs((CONF_K, BW), lambda i: (0, 0)), vec, vec],
                 [row, row], [_sds((t, BW), BF16), _sds((t, BW), F32)])(proj, proj, proj, proj, dw, ln_g, ln_b)


def conf_bwd_ln(z, dys, ln_g, ln_b):
    t = z.shape[0]
    tm = min(1024, t)

    def body(z_ref, dy_ref, g_ref, b_ref, dz_ref, dg_ref, db_ref):
        @pl.when(pl.program_id(0) == 0)
        def _():
            dg_ref[...] = jnp.zeros_like(dg_ref)
            db_ref[...] = jnp.zeros_like(db_ref)

        gv = g_ref[...]
        xhat, r = _ln_hat(z_ref[...])
        ln = xhat * gv + b_ref[...]
        s = _sigmoid(ln)
        dln = dy_ref[...] * (s * (1.0 + ln * (1.0 - s)))
        dz_ref[...] = _ln_bwd(dln, xhat, r, gv)
        dg_ref[...] += _colsum(dln * xhat)
        db_ref[...] += _colsum(dln)

    vec = bs((1, BW), lambda i: (0, 0))
    row = bs((tm, BW), lambda i: (i, 0))
    return _call(body, "conf_bwd_ln", (t // tm,), [row, bs((None, tm, BW), lambda i: (3, i, 0)), vec, vec],
                 [row, vec, vec], [_sds((t, BW), F32), _sds((1, BW), F32), _sds((1, BW), F32)])(z, dys, ln_g, ln_b)


def conf_bwd_conv(proj, dz, dw):
    t = proj.shape[0]
    tm, halo = min(512, t), CONF_HALO
    per = tm // halo
    last = t // halo - 1
    nt = t // tm

    def body(v_ref, gt_ref, vh_ref, gh_ref, dz_ref, dzn_ref, w_ref, dv_ref, dg_ref, dw_ref):
        i = pl.program_id(0)

        @pl.when(i == 0)
        def _():
            dw_ref[...] = jnp.zeros_like(dw_ref)

        val = v_ref[...]
        sg = _sigmoid(gt_ref[...])
        yh = jnp.where(i > 0, vh_ref[...] * _sigmoid(gh_ref[...]), 0.0)
        yext = jnp.concatenate([yh, val * sg], axis=0)
        dz = dz_ref[...]
        dzn = jnp.where(i < nt - 1, dzn_ref[...], 0.0)
        dy0 = _anti_conv(jnp.concatenate([dz, dzn], axis=0), w_ref, CONF_K, tm)
        dv_ref[...] = (dy0 * sg).astype(BF16)
        dg_ref[...] = (dy0 * val * sg * (1.0 - sg)).astype(BF16)
        _conv_wgrad(dw_ref, dz, yext, CONF_K, halo)

    col = lambda c: bs((tm, BW), lambda i: (i, c))
    hal = lambda c: bs((halo, BW), lambda i: (_prev_blk(i, per), c))
    row = bs((tm, BW), lambda i: (i, 0))
    wsp = bs((CONF_K, BW), lambda i: (0, 0))
    return _call(body, "conf_bwd_conv", (t // tm,),
                 [col(8), col(9), hal(8), hal(9), row, bs((halo, BW), lambda i: (_next_blk(i, per, last), 0)), wsp],
                 [row, row, wsp], [_sds((t, BW), BF16), _sds((t, BW), BF16), _sds((CONF_K, BW), F32)])(
                     proj, proj, proj, proj, dz, dz, dw)


def _place():
    return lax.axis_index("x"), lax.axis_index("y"), lax.axis_index("c")


def _comm_call(body, name, n_in, out_shape, scratch, aliases=None):
    return pl.pallas_call(body, name=name, in_specs=[ANY] * n_in, out_specs=[ANY] * len(out_shape), out_shape=out_shape,
                          scratch_shapes=scratch, input_output_aliases=aliases or {},
                          compiler_params=pltpu.CompilerParams(has_side_effects=True, vmem_limit_bytes=VMEM_LIMIT))


def gather8(v, reduce):
    rows, cols = v.shape

    def body(v_ref, o_ref, land_ref, send, recv, lsem):
        x, y, c = _place()
        me = 4 * x + 2 * y + c
        land = land_ref if reduce else o_ref
        mine = pltpu.make_async_copy(v_ref, land.at[me], lsem)
        mine.start()
        sent = []
        for j in range(1, 8):
            fx, fy, fc = (j >> 2) & 1, (j >> 1) & 1, j & 1
            tgt = (1 - x if fx else x, 1 - y if fy else y, 1 - c if fc else c)
            cp = pltpu.make_async_remote_copy(src_ref=v_ref, dst_ref=land.at[me], send_sem=send.at[j - 1],
                                              recv_sem=recv.at[j - 1], device_id=tgt, device_id_type=MESH)
            cp.start()
            sent.append(cp)
        for j in range(1, 8):
            fx, fy, fc = (j >> 2) & 1, (j >> 1) & 1, j & 1
            peer = 4 * (1 - x if fx else x) + 2 * (1 - y if fy else y) + (1 - c if fc else c)
            pltpu.make_async_remote_copy(src_ref=v_ref, dst_ref=land.at[peer], send_sem=send.at[j - 1],
                                         recv_sem=recv.at[j - 1], device_id=(x, y, c), device_id_type=MESH).wait_recv()
        for cp in sent:
            cp.wait_send()
        mine.wait()
        if reduce:
            acc = land_ref[0]
            for k in range(1, 8):
                acc = acc + land_ref[k]
            o_ref[...] = acc

    vm = pl.BlockSpec(memory_space=pltpu.VMEM)
    out_shape = _sds((rows, cols), F32) if reduce else _sds((8, rows, cols), F32)
    land_shape = (8, rows, cols) if reduce else (8, 128)
    return pl.pallas_call(
        body, name="allreduce8" if reduce else "allgather8", in_specs=[vm], out_specs=vm, out_shape=out_shape,
        scratch_shapes=[pltpu.VMEM(land_shape, F32), pltpu.SemaphoreType.DMA((7,)), pltpu.SemaphoreType.DMA((7,)),
                        pltpu.SemaphoreType.DMA],
        compiler_params=pltpu.CompilerParams(has_side_effects=True, vmem_limit_bytes=VMEM_LIMIT))(v)


def allgather_weights(bufs):
    n = len(bufs)

    def body(*refs):
        ins, outs = refs[:n], refs[n:2 * n]
        send, recv = refs[2 * n:]
        x, y, c = _place()
        s_me = 2 * x + y
        chips = [(1 - x, y), (x, 1 - y), (1 - x, 1 - y)]
        sibling = (x, y, 1 - c)
        started = []
        for a in range(n):
            hl = bufs[a].shape[1] // 2
            half = pl.ds(c * hl, hl)
            for j, chip in enumerate(chips):
                cp = pltpu.make_async_remote_copy(src_ref=ins[a].at[s_me, half], dst_ref=outs[a].at[s_me, half],
                                                  send_sem=send.at[6 * a + j], recv_sem=recv.at[6 * a + j],
                                                  device_id=(chip[0], chip[1], c), device_id_type=MESH)
                cp.start()
                started.append(cp)
        for a in range(n):
            hl = bufs[a].shape[1] // 2
            half = pl.ds(c * hl, hl)
            for j, chip in enumerate(chips):
                s_j = 2 * chip[0] + chip[1]
                landed = outs[a].at[s_j, half]
                pltpu.make_async_remote_copy(src_ref=landed, dst_ref=landed, send_sem=send.at[6 * a + j],
                                             recv_sem=recv.at[6 * a + j], device_id=sibling, device_id_type=MESH).wait_recv()
                fw = pltpu.make_async_remote_copy(src_ref=landed, dst_ref=landed, send_sem=send.at[6 * a + 3 + j],
                                                  recv_sem=recv.at[6 * a + 3 + j], device_id=sibling, device_id_type=MESH)
                fw.start()
                started.append(fw)
        for a in range(n):
            hl = bufs[a].shape[1] // 2
            other = pl.ds((1 - c) * hl, hl)
            for j, chip in enumerate(chips):
                s_j = 2 * chip[0] + chip[1]
                theirs = outs[a].at[s_j, other]
                pltpu.make_async_remote_copy(src_ref=theirs, dst_ref=theirs, send_sem=send.at[6 * a + 3 + j],
                                             recv_sem=recv.at[6 * a + 3 + j], device_id=sibling, device_id_type=MESH).wait_recv()
        for cp in started:
            cp.wait_send()

    out_shape = [_sds(b.shape, b.dtype) for b in bufs]
    scratch = [pltpu.SemaphoreType.DMA((6 * n,)), pltpu.SemaphoreType.DMA((6 * n,))]
    return _comm_call(body, "allgather_weights", n, out_shape, scratch, aliases={a: a for a in range(n)})(*bufs)


def _row_tile(rows, cols):
    best = 16
    for t in range(16, rows + 1, 16):
        if rows % t == 0 and t * cols * 4 <= 2 * 1024 * 1024:
            best = t
    return best


def reduce_scatter_grads(gs):
    n = len(gs)
    x, y, c = _place()
    scal = jnp.stack([2 * x + y, c]).astype(jnp.int32)
    hrs = [g.shape[1] // 2 for g in gs]

    def swap_body(*refs):
        ins, outs = refs[:n], refs[n:2 * n]
        send, recv = refs[2 * n:]
        xx, yy, cc = _place()
        cps = []
        for a in range(n):
            cp = pltpu.make_async_remote_copy(src_ref=ins[a].at[:, pl.ds((1 - cc) * hrs[a], hrs[a])], dst_ref=outs[a],
                                              send_sem=send.at[a], recv_sem=recv.at[a],
                                              device_id=(xx, yy, 1 - cc), device_id_type=MESH)
            cp.start()
            cps.append(cp)
        for cp in cps:
            cp.wait()

    ras = _comm_call(swap_body, "rs_swap_halves", n, [_sds((N_SH, hrs[a], gs[a].shape[2]), F32) for a in range(n)],
                     [pltpu.SemaphoreType.DMA((n,)), pltpu.SemaphoreType.DMA((n,))])(*gs)

    def add_sibling(g, ra, hr):
        cols = g.shape[2]
        tr = _row_tile(hr, cols)
        nr = hr // tr

        def body(s_ref, g_ref, r_ref, p32_ref, p16_ref):
            v = g_ref[...] + r_ref[...]
            p32_ref[...] = v
            p16_ref[...] = v.astype(BF16)

        blk = lambda f: bs((None, tr, cols), f)
        own = blk(lambda s, i, sr: (s, i, 0))
        spec = pltpu.PrefetchScalarGridSpec(num_scalar_prefetch=1, grid=(N_SH, nr),
                                            in_specs=[blk(lambda s, i, sr: (s, sr[1] * nr + i, 0)), own],
                                            out_specs=[own, own])
        return pl.pallas_call(body, name="rs_add_sibling", grid_spec=spec,
                              out_shape=[_sds((N_SH, hr, cols), F32), _sds((N_SH, hr, cols), BF16)],
                              compiler_params=pltpu.CompilerParams(dimension_semantics=("arbitrary",) * 2,
                                                                   vmem_limit_bytes=VMEM_LIMIT))(scal, g, ra)

    parts = [add_sibling(gs[a], ras[a], hrs[a]) for a in range(n)]

    def ici_body(*refs):
        ins, outs = refs[:n], refs[n:2 * n]
        send, recv = refs[2 * n:]
        xx, yy, cc = _place()
        chips = [(1 - xx, yy), (xx, 1 - yy), (1 - xx, 1 - yy)]
        cps = []
        for a in range(n):
            for j, chip in enumerate(chips):
                cp = pltpu.make_async_remote_copy(src_ref=ins[a].at[2 * chip[0] + chip[1]], dst_ref=outs[a].at[j],
                                                  send_sem=send.at[3 * a + j], recv_sem=recv.at[3 * a + j],
                                                  device_id=(chip[0], chip[1], cc), device_id_type=MESH)
                cp.start()
                cps.append(cp)
        for cp in cps:
            cp.wait()

    rbs = _comm_call(ici_body, "rs_send_partials", n, [_sds((3, hrs[a], gs[a].shape[2]), BF16) for a in range(n)],
                     [pltpu.SemaphoreType.DMA((3 * n,)), pltpu.SemaphoreType.DMA((3 * n,))])(*[p[1] for p in parts])

    def add_chips(p32, rb, hr):
        cols = p32.shape[2]
        tr = _row_tile(hr, cols)
        nr = hr // tr

        def body(s_ref, p_ref, r0, r1, r2, o_ref):
            o_ref[...] = ((p_ref[...] + r0[...].astype(F32)) + r1[...].astype(F32)) + r2[...].astype(F32)

        blk = lambda f: bs((None, tr, cols), f)
        spec = pltpu.PrefetchScalarGridSpec(
            num_scalar_prefetch=1, grid=(nr,),
            in_specs=[blk(lambda i, sr: (sr[0], i, 0))] + [blk(functools.partial(lambda i, sr, j: (j, i, 0), j=j))
                                                            for j in range(3)],
            out_specs=blk(lambda i, sr: (sr[1], i, 0)))
        return pl.pallas_call(body, name="rs_add_chips", grid_spec=spec, out_shape=_sds((2, hr, cols), F32),
                              compiler_params=pltpu.CompilerParams(dimension_semantics=("arbitrary",),
                                                                   vmem_limit_bytes=VMEM_LIMIT))(scal, p32, rb, rb, rb)

    fins = [add_chips(parts[a][0], rbs[a], hrs[a]) for a in range(n)]

    def share_body(*refs):
        ins, outs = refs[:n], refs[n:2 * n]
        send, recv = refs[2 * n:]
        xx, yy, cc = _place()
        sib = (xx, yy, 1 - cc)
        cps = []
        for a in range(n):
            cp = pltpu.make_async_remote_copy(src_ref=ins[a].at[cc], dst_ref=outs[a].at[cc], send_sem=send.at[a],
                                              recv_sem=recv.at[a], device_id=sib, device_id_type=MESH)
            cp.start()
            cps.append(cp)
        for a in range(n):
            pltpu.make_async_remote_copy(src_ref=ins[a].at[cc], dst_ref=outs[a].at[1 - cc], send_sem=send.at[a],
                                         recv_sem=recv.at[a], device_id=sib, device_id_type=MESH).wait_recv()
        for cp in cps:
            cp.wait_send()

    fulls = _comm_call(share_body, "rs_share_halves", n, [_sds(f.shape, F32) for f in fins],
                       [pltpu.SemaphoreType.DMA((n,)), pltpu.SemaphoreType.DMA((n,))],
                       aliases={a: a for a in range(n)})(*fins)
    return [f.reshape(2 * hr, f.shape[2]) for f, hr in zip(fulls, hrs)]


def adamw(w, g, m, v):
    shape = w.shape
    cols = shape[-1]
    rows = math.prod(shape[:-1]) if len(shape) > 1 else 1
    tr = 256 if rows % 256 == 0 and rows > 256 else rows
    c1 = 1.0 - ADAM_B1 ** ADAM_STEP
    c2 = 1.0 - ADAM_B2 ** ADAM_STEP

    def body(w_ref, g_ref, m_ref, v_ref, d_ref, nm_ref, nv_ref):
        gv = g_ref[...]
        nm = ADAM_B1 * m_ref[...] + (1.0 - ADAM_B1) * gv
        nv = ADAM_B2 * v_ref[...] + (1.0 - ADAM_B2) * (gv * gv)
        nm_ref[...] = nm
        nv_ref[...] = nv
        d_ref[...] = -ADAM_LR * ((nm / c1) / (jnp.sqrt(nv / c2) + ADAM_EPS) + ADAM_WD * w_ref[...])

    row = bs((tr, cols), lambda i: (i, 0))
    outs = _call(body, "adamw", (rows // tr,), [row] * 4, [row] * 3, [_sds((rows, cols), F32)] * 3)(
        *[a.reshape(rows, cols) for a in (w, g, m, v)])
    return [o.reshape(shape) for o in outs]


def layer_fwd(x, p_i, w):
    h, proj = norm_in_proj(x, w["g_mix"], w["win"])
    ya = conva_fwd(proj, w["conv_a"])
    qkv = qkv_cast(proj)
    groups = [attn_fwd_group(qkv, d) for d in DILATIONS]
    yb, o32, lse = attn_merge([g[0] for g in groups], [g[1] for g in groups], [g[2] for g in groups])
    yc = sgu_fwd(proj, w["sgu_ln_g"], w["sgu_ln_b"], w["sgu_wt"], w["sgu_bf"])
    yd, z = conf_fwd(proj, w["conf_dw"], w["conf_ln_g"], w["conf_ln_b"])
    ys = (ya, yb, yc, yd)
    merged, gates, ybr = merge_fwd(h, ys, w["wg"], w["wbr"])
    x1 = mm_residual(merged, w["wout"], x, "attn_out")
    h2, fgu, act = ffn_in(x1, w["g_ffn"], w["wfi"])
    x2 = mm_residual(act, w["wfo"], x1, "ffn_out")
    h3, gate, pp, x3 = ple_fwd(x2, w["g_ple"], w["wpg"], p_i, w["wpp"])
    saved = dict(x=x, h=h, proj=proj, qkv=qkv, ys=ys, o32=o32, lse=lse, z=z, merged=merged, gates=gates, ybr=ybr, x1=x1,
                 h2=h2, fgu=fgu, act=act, x2=x2, h3=h3, gate=gate, pp=pp)
    return x3, saved


def layer_bwd(dx3, p_i, w, s):
    t = dx3.shape[0]
    tr = min(1024, t)
    nr = t // tr
    ns_fi = FFN_H // 2
    small = {}

    dpre, dpp = ple_bwd_pre(dx3, s["gate"], s["pp"])
    ga_shape, gb_shape = _sds((N_SH, 6 * BW, D_MODEL), F32), _sds((N_SH, 5 * BW, BW), F32)
    ga_blk = lambda idx: bs((N_SH, BW, D_MODEL), idx)
    ga = tn_matmul("dw_ple_gate", s["h3"], dpre, (nr,), bs((tr, D_MODEL), lambda r: (r, 0)),
                   bs((tr, D_MODEL), lambda r: (r, 0)), ga_blk(lambda r: (0, 5, 0)), ga_shape, split=N_SH)
    gb = tn_matmul("dw_ple_proj", p_i, dpp, (N_SH, nr), bs((tr, BW), lambda j, r: (r, 0)),
                   bs((tr, BW), lambda j, r: (r, j)), bs((None, BW, BW), lambda j, r: (j, 4, 0)), gb_shape)
    dx2, small["g_ple"] = norm_bwd(
        "ple_norm_bwd",
        [(dpre, lambda tm, loc: bs((tm, D_MODEL), lambda i, k: (i, loc(k))), w["wpg"],
          lambda loc: bs((D_MODEL, D_MODEL), lambda i, k: (0, loc(k))), 1)],
        None, dx3, s["x2"], w["g_ple"])

    df = ffn_bwd_act(dx2, w["wfo"], s["fgu"])
    gfo = tn_matmul("dw_ffn_out", s["act"], dx2, (2, nr), bs((tr, ns_fi), lambda j, r: (r, j)),
                    bs((tr, D_MODEL), lambda j, r: (r, 0)), bs((2, FFN_H // N_SH, D_MODEL), lambda j, r: (j, 0, 0)),
                    _sds((N_SH, FFN_H // N_SH, D_MODEL), F32), split=2)
    gfi = tn_matmul("dw_ffn_in", s["h2"], df, (N_SH, nr), bs((tr, D_MODEL), lambda j, r: (r, 0)),
                    bs((None, tr, ns_fi), lambda j, r: (j // 2, r, j % 2)),
                    bs((None, D_MODEL, ns_fi), lambda j, r: (j, 0, 0)), _sds((N_SH, D_MODEL, ns_fi), F32))
    dx1, small["g_ffn"] = norm_bwd(
        "ffn_norm_bwd",
        [(df, lambda tm, loc: bs((None, tm, ns_fi), lambda i, k: (loc(k) // 2, i, loc(k) % 2)), w["wfi"],
          lambda loc: bs((None, D_MODEL, ns_fi), lambda i, k: (loc(k), 0, 0)), N_SH)],
        None, dx2, s["x1"], w["g_ffn"])

    dpre_m, dyb = merge_bwd_pre(dx1, w["wout"], s["gates"], s["ybr"])
    ga = tn_matmul("dw_out", s["merged"], dx1, (nr,), bs((tr, D_MODEL), lambda r: (r, 0)),
                   bs((tr, D_MODEL), lambda r: (r, 0)), ga_blk(lambda r: (0, 4, 0)), ga_shape, split=N_SH, into=ga)
    ga = tn_matmul("dw_merge_gate", s["h"], dpre_m, (N_BR, nr), bs((tr, D_MODEL), lambda k, r: (r, 0)),
                   bs((None, tr, D_MODEL), lambda k, r: (k, r, 0)), ga_blk(lambda k, r: (0, k, 0)), ga_shape,
                   split=N_SH, into=ga)
    for k in range(N_BR):
        gb = tn_matmul("dw_branch", s["ys"][k], dyb, (N_SH, nr), bs((tr, BW), lambda j, r: (r, 0)),
                       bs((None, tr, BW), functools.partial(lambda j, r, kk: (kk, r, j), kk=k)),
                       bs((None, BW, BW), functools.partial(lambda j, r, kk: (j, kk, 0), kk=k)), gb_shape, into=gb)
    dys = branch_out_bwd(dyb, w["wbr"])

    (dab, dac, dax), small["conv_a"] = conva_bwd(s["proj"], dys, w["conv_a"])
    dy_b = dys[1]
    delta = attn_delta(dy_b, s["o32"])
    acc = None
    for d in reversed(DILATIONS):
        acc = attn_bwd_group(s["qkv"], dy_b, s["lse"], delta, acc, d)
    du, dv, d_sw, d_sbf, small["sgu_ln_g"], small["sgu_ln_b"] = sgu_bwd(
        s["proj"], dys, w["sgu_ln_g"], w["sgu_ln_b"], w["sgu_wt"], w["sgu_bf"])
    small["sgu_w"] = jnp.where(jnp.tril(jnp.ones((BLK, BLK), bool))[None], d_sw, 0.0)
    small["sgu_b"] = jnp.sum(d_sbf.reshape(BLK, 4, HEAD_D), axis=-1).T
    dz, small["conf_ln_g"], small["conf_ln_b"] = conf_bwd_ln(s["z"], dys, w["conf_ln_g"], w["conf_ln_b"])
    dval, dgate, small["conf_dw"] = conf_bwd_conv(s["proj"], dz, w["conf_dw"])
    dproj = jnp.concatenate([dab, dac, dax, acc.astype(BF16), du, dv, dval, dgate], axis=1)

    ns_in = N_IN // N_SH
    gin = tn_matmul("dw_in", s["h"], dproj, (N_SH, nr), bs((tr, D_MODEL), lambda j, r: (r, 0)),
                    bs((tr, ns_in), lambda j, r: (r, j)), bs((None, D_MODEL, ns_in), lambda j, r: (j, 0, 0)),
                    _sds((N_SH, D_MODEL, ns_in), F32))
    big = [ga, gfo, gb, gin, gfi]
    dx, small["g_mix"] = norm_bwd(
        "mix_norm_bwd",
        [(dpre_m, lambda tm, loc: bs((None, tm, D_MODEL), lambda i, k: (loc(k), i, 0)), w["wg"],
          lambda loc: bs((None, D_MODEL, D_MODEL), lambda i, k: (loc(k), 0, 0)), N_BR),
         (dproj, lambda tm, loc: bs((tm, ns_in), lambda i, k: (i, loc(k))), w["win"],
          lambda loc: bs((None, D_MODEL, ns_in), lambda i, k: (loc(k), 0, 0)), N_SH)],
        None, dx1, s["x"], w["g_mix"])
    return dx, big, small


BIG_NAMES = ("w_in", "w_branch", "w_merge_gate", "w_out", "w_ffn_in", "w_ffn_out", "w_ple_gate", "w_ple_proj")


def unpack_big_grads(ga, gfo, gb, gin, gfi):
    return dict(w_in=gin, w_ffn_in=gfi, w_ffn_out=gfo,
                w_merge_gate=ga[:N_BR * BW].reshape(N_BR, BW, D_MODEL), w_out=ga[N_BR * BW:5 * BW], w_ple_gate=ga[5 * BW:],
                w_branch=gb[:N_BR * BW].reshape(N_BR, BW, BW), w_ple_proj=gb[N_BR * BW:])


SMALL_NAMES = ("g_mix", "conv_a", "sgu_ln_g", "sgu_ln_b", "sgu_w", "sgu_b", "conf_dw", "conf_ln_g", "conf_ln_b",
               "g_ffn", "g_ple")


def _pack_rows(arrays, rows):
    flat = jnp.concatenate([a.reshape(-1) for a in arrays])
    return jnp.pad(flat, (0, rows * D_MODEL - flat.shape[0])).reshape(rows, D_MODEL)


def _unpack_rows(packed, shapes):
    flat, out, pos = packed.reshape(-1), [], 0
    for shape in shapes:
        n = math.prod(shape)
        out.append(flat[pos:pos + n].reshape(shape))
        pos += n
    return out


def kernel(x, p, g_mix, w_in, conv_a, sgu_ln_g, sgu_ln_b, sgu_w, sgu_b, conf_dw, conf_ln_g, conf_ln_b, w_branch, w_merge_gate, w_out, g_ffn, w_ffn_in, w_ffn_out, g_ple, w_ple_gate, w_ple_proj, g_final, loss_target, m_g_mix, m_w_in, m_conv_a, m_sgu_ln_g, m_sgu_ln_b, m_sgu_w, m_sgu_b, m_conf_dw, m_conf_ln_g, m_conf_ln_b, m_w_branch, m_w_merge_gate, m_w_out, m_g_ffn, m_w_ffn_in, m_w_ffn_out, m_g_ple, m_w_ple_gate, m_w_ple_proj, m_g_final, v_g_mix, v_w_in, v_conv_a, v_sgu_ln_g, v_sgu_ln_b, v_sgu_w, v_sgu_b, v_conf_dw, v_conf_ln_g, v_conf_ln_b, v_w_branch, v_w_merge_gate, v_w_out, v_g_ffn, v_w_ffn_in, v_w_ffn_out, v_g_ple, v_w_ple_gate, v_w_ple_proj, v_g_final):
    weights = dict(g_mix=g_mix, w_in=w_in, conv_a=conv_a, sgu_ln_g=sgu_ln_g, sgu_ln_b=sgu_ln_b, sgu_w=sgu_w, sgu_b=sgu_b,
                   conf_dw=conf_dw, conf_ln_g=conf_ln_g, conf_ln_b=conf_ln_b, w_branch=w_branch, w_merge_gate=w_merge_gate,
                   w_out=w_out, g_ffn=g_ffn, w_ffn_in=w_ffn_in, w_ffn_out=w_ffn_out, g_ple=g_ple, w_ple_gate=w_ple_gate,
                   w_ple_proj=w_ple_proj, g_final=g_final)
    m_in = dict(g_mix=m_g_mix, w_in=m_w_in, conv_a=m_conv_a, sgu_ln_g=m_sgu_ln_g, sgu_ln_b=m_sgu_ln_b, sgu_w=m_sgu_w,
                sgu_b=m_sgu_b, conf_dw=m_conf_dw, conf_ln_g=m_conf_ln_g, conf_ln_b=m_conf_ln_b, w_branch=m_w_branch,
                w_merge_gate=m_w_merge_gate, w_out=m_w_out, g_ffn=m_g_ffn, w_ffn_in=m_w_ffn_in, w_ffn_out=m_w_ffn_out,
                g_ple=m_g_ple, w_ple_gate=m_w_ple_gate, w_ple_proj=m_w_ple_proj, g_final=m_g_final)
    v_in = dict(g_mix=v_g_mix, w_in=v_w_in, conv_a=v_conv_a, sgu_ln_g=v_sgu_ln_g, sgu_ln_b=v_sgu_ln_b, sgu_w=v_sgu_w,
                sgu_b=v_sgu_b, conf_dw=v_conf_dw, conf_ln_g=v_conf_ln_g, conf_ln_b=v_conf_ln_b, w_branch=v_w_branch,
                w_merge_gate=v_w_merge_gate, w_out=v_w_out, g_ffn=v_g_ffn, w_ffn_in=v_w_ffn_in, w_ffn_out=v_w_ffn_out,
                g_ple=v_g_ple, w_ple_gate=v_w_ple_gate, w_ple_proj=v_w_ple_proj, g_final=v_g_final)
    order = ("g_mix", "w_in", "conv_a", "sgu_ln_g", "sgu_ln_b", "sgu_w", "sgu_b", "conf_dw", "conf_ln_g", "conf_ln_b",
             "w_branch", "w_merge_gate", "w_out", "g_ffn", "w_ffn_in", "w_ffn_out", "g_ple", "w_ple_gate", "w_ple_proj",
             "g_final")
    depth = g_mix.shape[0]
    xs, tgt = x[0], loss_target[0]
    cw = BW // N_SH
    my_shard = 2 * lax.axis_index("x") + lax.axis_index("y")

    conv_rows = 16
    allc = gather8(_pack_rows([conv_a, conf_dw], conv_rows), reduce=False)
    shards = [_unpack_rows(allc[2 * s], [conv_a.shape, conf_dw.shape]) for s in range(N_SH)]
    conv_a_full = jnp.concatenate([sh[0] for sh in shards], axis=-1)
    conf_dw_full = jnp.concatenate([sh[1] for sh in shards], axis=-1)

    tril = jnp.tril(jnp.ones((BLK, BLK), bool))
    layers = []
    for i in range(depth):
        shards = ([w_in[i], w_branch[i]] + [w_merge_gate[i, k] for k in range(N_BR)]
                  + [w_out[i], w_ffn_in[i], w_ffn_out[i], w_ple_gate[i], w_ple_proj[i]])
        got = allgather_weights([
            lax.dynamic_update_slice(jnp.zeros((N_SH,) + sh.shape, BF16), sh.astype(BF16)[None], (my_shard,) + (0,) * sh.ndim)
            for sh in shards])
        vec = lambda a: a[i].reshape(1, -1)
        layers.append(dict(
            win=got[0], wbr=got[1], wg=jnp.stack([g.reshape(D_MODEL, D_MODEL) for g in got[2:6]]),
            wout=got[6].reshape(D_MODEL, D_MODEL), wfi=got[7], wfo=got[8].reshape(FFN_H, D_MODEL),
            wpg=got[9].reshape(D_MODEL, D_MODEL), wpp=got[10],
            g_mix=vec(g_mix), g_ffn=vec(g_ffn), g_ple=vec(g_ple), conv_a=conv_a_full[i], conf_dw=conf_dw_full[i],
            sgu_ln_g=vec(sgu_ln_g), sgu_ln_b=vec(sgu_ln_b), conf_ln_g=vec(conf_ln_g), conf_ln_b=vec(conf_ln_b),
            sgu_wt=jnp.where(tril[None], sgu_w[i], 0.0).astype(BF16),
            sgu_bf=jnp.repeat(sgu_b[i].T, HEAD_D, axis=1)))

    act, saved = xs, []
    for i in range(depth):
        act, sv = layer_fwd(act, p[i, 0], layers[i])
        saved.append(sv)
    loss_part, dx, dg_final = loss_head(act, g_final.reshape(1, -1), tgt)

    big_red = [None] * depth
    small_parts = [None] * depth
    for i in reversed(range(depth)):
        dx, big, small_parts[i] = layer_bwd(dx, p[i, 0], layers[i], saved[i])
        big_red[i] = unpack_big_grads(*reduce_scatter_grads(big))

    small_list = [jnp.stack([small_parts[i][n].reshape(weights[n].shape[1:] if n not in ("conv_a", "conf_dw")
                                                       else small_parts[i][n].shape) for i in range(depth)])
                  for n in SMALL_NAMES]
    small_list += [dg_final.reshape(-1), loss_part[0, :1]]
    small_shapes = [a.shape for a in small_list]
    n_small = sum(math.prod(sh) for sh in small_shapes)
    small_rows = -(-n_small // (8 * D_MODEL)) * 8
    red = _unpack_rows(gather8(_pack_rows(small_list, small_rows), reduce=True), small_shapes)
    grads = dict(zip(SMALL_NAMES, red[:len(SMALL_NAMES)]))
    grads["g_final"] = red[-2]
    loss = red[-1].reshape(())
    for n in ("conv_a", "conf_dw"):
        grads[n] = lax.dynamic_slice_in_dim(grads[n], my_shard * cw, cw, axis=2)
    for name in BIG_NAMES:
        grads[name] = jnp.stack([big_red[i][name] for i in range(depth)])

    small_all = [n for n in order if n not in BIG_NAMES]
    sm_shapes = [weights[n].shape for n in small_all]
    n_sm = sum(math.prod(sh) for sh in sm_shapes)
    sm_rows = -(-n_sm // (8 * D_MODEL)) * 8
    packed = [_pack_rows([src[n] for n in small_all], sm_rows) for src in (weights, grads, m_in, v_in)]
    sm_out = [_unpack_rows(o, sm_shapes) for o in adamw(*packed)]
    delta, new_m, new_v = ({n: o[k] for k, n in enumerate(small_all)} for o in sm_out)
    for name in BIG_NAMES:
        delta[name], new_m[name], new_v[name] = adamw(weights[name], grads[name], m_in[name], v_in[name])

    return (loss, dx[None], *[grads[n] for n in order], *[delta[n] for n in order], *[new_m[n] for n in order],
            *[new_v[n] for n in order])
```

```python
import functools
import math

import jax
import jax.numpy as jnp
from jax import lax
from jax.experimental import pallas as pl
from jax.experimental.pallas import tpu as pltpu

F32 = jnp.float32
BF16 = jnp.bfloat16
EPS = 1e-6
D_MODEL = 1024
BW = 256
N_BR = 4
N_IN = 10 * BW
FFN_H = 2816
N_SH = 4
HEADS = 4
HEAD_D = 64
BLK = 128
DILATIONS = (1, 4, 16)
CONF_K = 31
CONVA_K = 3
NEG = -1e30
VMEM_LIMIT = 56 * 1024 * 1024
MESH = pl.DeviceIdType.MESH

ADAM_LR, ADAM_B1, ADAM_B2, ADAM_EPS, ADAM_WD, ADAM_STEP = 0.001, 0.9, 0.999, 1e-08, 0.01, 10

bs = pl.BlockSpec
ANY = pl.BlockSpec(memory_space=pl.ANY)


def _call(body, name, grid, in_specs, out_specs, out_shape, scratch=(), aliases=None):
    return pl.pallas_call(
        body, name=name, grid=grid, in_specs=in_specs, out_specs=out_specs, out_shape=out_shape,
        scratch_shapes=list(scratch), input_output_aliases=aliases or {},
        compiler_params=pltpu.CompilerParams(dimension_semantics=("arbitrary",) * len(grid),
                                             vmem_limit_bytes=VMEM_LIMIT))


def _sds(shape, dtype):
    return jax.ShapeDtypeStruct(shape, dtype)


def _nn(a, b):
    return jnp.dot(a, b, preferred_element_type=F32)


def _nt(a, b):
    return lax.dot_general(a, b, (((1,), (1,)), ((), ())), preferred_element_type=F32)


def _tn(a, b):
    return lax.dot_general(a, b, (((0,), (0,)), ((), ())), preferred_element_type=F32)


def _sigmoid(x):
    return 1.0 / (1.0 + jnp.exp(-x))


def _rms_fwd(x, g):
    r = lax.rsqrt(jnp.mean(x * x, axis=-1, keepdims=True) + EPS)
    return x * r * g


def _rms_bwd(dh, x, g):
    r = lax.rsqrt(jnp.mean(x * x, axis=-1, keepdims=True) + EPS)
    xr = x * r
    dxr = dh * g
    dx = r * (dxr - xr * jnp.mean(dxr * xr, axis=-1, keepdims=True))
    return dx, dh * xr


def _ln_hat(x):
    mu = jnp.mean(x, axis=-1, keepdims=True)
    xc = x - mu
    r = lax.rsqrt(jnp.mean(xc * xc, axis=-1, keepdims=True) + EPS)
    return xc * r, r


def _ln_bwd(dy, xhat, r, g):
    dxh = dy * g
    return r * (dxh - jnp.mean(dxh, axis=-1, keepdims=True) - xhat * jnp.mean(dxh * xhat, axis=-1, keepdims=True))


def _colsum(v):
    return jnp.sum(v, axis=0, keepdims=True)


def _causal_conv(zext, w_ref, k_taps, halo):
    acc = zext[halo:] * w_ref[k_taps - 1:k_taps, :]
    for k in range(k_taps - 1):
        acc = acc + pltpu.roll(zext, k_taps - 1 - k, 0)[halo:] * w_ref[k:k + 1, :]
    return acc


def _anti_conv(dext, w_ref, k_taps, tm):
    n = dext.shape[0]
    acc = dext[:tm] * w_ref[k_taps - 1:k_taps, :]
    for s in range(1, k_taps):
        acc = acc + pltpu.roll(dext, n - s, 0)[:tm] * w_ref[k_taps - 1 - s:k_taps - s, :]
    return acc


def _conv_wgrad(dw_ref, dc, zext, k_taps, halo):
    dw_ref[k_taps - 1:k_taps, :] += _colsum(dc * zext[halo:])
    for k in range(k_taps - 1):
        dw_ref[k:k + 1, :] += _colsum(dc * pltpu.roll(zext, k_taps - 1 - k, 0)[halo:])


def _prev_blk(i, per):
    return jnp.maximum(i * per - 1, 0)


def _next_blk(i, per, last):
    return jnp.minimum((i + 1) * per, last)


def norm_in_proj(x, g, win):
    t = x.shape[0]
    tm = min(1024, t)
    ns = win.shape[2]

    def body(x_ref, g_ref, w_ref, h_ref, o_ref):
        h = _rms_fwd(x_ref[...], g_ref[...]).astype(BF16)
        h_ref[...] = h
        o_ref[...] = _nn(h, w_ref[...])

    return _call(
        body, "norm_in_proj", (t // tm, N_SH),
        [bs((tm, D_MODEL), lambda i, j: (i, 0)), bs((1, D_MODEL), lambda i, j: (0, 0)),
         bs((None, D_MODEL, ns), lambda i, j: (j, 0, 0))],
        [bs((tm, D_MODEL), lambda i, j: (i, 0)), bs((tm, ns), lambda i, j: (i, j))],
        [_sds((t, D_MODEL), BF16), _sds((t, N_SH * ns), F32)])(x, g, win)


def merge_fwd(h, ys, wg, wbr):
    t = h.shape[0]
    tm = min(1024, t)

    def body(h_ref, ya, yb, yc, yd, wg_ref, wb_ref, m_ref, g_ref, b_ref):
        hh = h_ref[...]
        acc = None
        for k, y_ref in enumerate((ya, yb, yc, yd)):
            g = _sigmoid(_nn(hh, wg_ref[k]))
            b = _nn(y_ref[...], wb_ref[k])
            g_ref[k] = g.astype(BF16)
            b_ref[k] = b.astype(BF16)
            acc = g * b if acc is None else acc + g * b
        m_ref[...] = acc.astype(BF16)

    ysp = bs((tm, BW), lambda i, j: (i, 0))
    return _call(
        body, "merge_fwd", (t // tm, N_SH),
        [bs((tm, D_MODEL), lambda i, j: (i, 0)), ysp, ysp, ysp, ysp,
         bs((N_BR, D_MODEL, BW), lambda i, j: (0, 0, j)), bs((None, N_BR, BW, BW), lambda i, j: (j, 0, 0, 0))],
        [bs((tm, BW), lambda i, j: (i, j)), bs((N_BR, tm, BW), lambda i, j: (0, i, j)),
         bs((N_BR, tm, BW), lambda i, j: (0, i, j))],
        [_sds((t, D_MODEL), BF16), _sds((N_BR, t, D_MODEL), BF16), _sds((N_BR, t, D_MODEL), BF16)])(
            h, *ys, wg, wbr)


def mm_residual(a, w, res, name):
    t, kk = a.shape
    tm, tn = min(1024, t), 512

    def body(a_ref, w_ref, r_ref, o_ref):
        o_ref[...] = r_ref[...] + _nn(a_ref[...], w_ref[...])

    return _call(
        body, name, (t // tm, D_MODEL // tn),
        [bs((tm, kk), lambda i, j: (i, 0)), bs((kk, tn), lambda i, j: (0, j)), bs((tm, tn), lambda i, j: (i, j))],
        bs((tm, tn), lambda i, j: (i, j)), _sds((t, D_MODEL), F32))(a, w, res)


def ffn_in(x, g, wfi):
    t = x.shape[0]
    tm = min(1024, t)
    ns = wfi.shape[2]

    def body(x_ref, g_ref, wg_ref, wu_ref, h_ref, f_ref, a_ref):
        h = _rms_fwd(x_ref[...], g_ref[...]).astype(BF16)
        h_ref[...] = h
        fg = _nn(h, wg_ref[...])
        fu = _nn(h, wu_ref[...])
        f_ref[0] = fg.astype(BF16)
        f_ref[1] = fu.astype(BF16)
        a_ref[...] = (fg * _sigmoid(fg) * fu).astype(BF16)

    return _call(
        body, "ffn_in", (t // tm, 2),
        [bs((tm, D_MODEL), lambda i, j: (i, 0)), bs((1, D_MODEL), lambda i, j: (0, 0)),
         bs((None, D_MODEL, ns), lambda i, j: (j, 0, 0)), bs((None, D_MODEL, ns), lambda i, j: (j + 2, 0, 0))],
        [bs((tm, D_MODEL), lambda i, j: (i, 0)), bs((2, tm, ns), lambda i, j: (0, i, j)),
         bs((tm, ns), lambda i, j: (i, j))],
        [_sds((t, D_MODEL), BF16), _sds((2, t, FFN_H), BF16), _sds((t, FFN_H), BF16)])(x, g, wfi, wfi)


def ple_fwd(x, g, wpg, p_i, wpp):
    t = x.shape[0]
    tm = min(1024, t)

    def body(x_ref, xt_ref, g_ref, wg_ref, p_ref, wp_ref, h_ref, gt_ref, pp_ref, o_ref):
        h = _rms_fwd(x_ref[...], g_ref[...]).astype(BF16)
        h_ref[...] = h
        gate = _sigmoid(_nn(h, wg_ref[...]))
        pp = _nn(p_ref[...].astype(BF16), wp_ref[...])
        gt_ref[...] = gate.astype(BF16)
        pp_ref[...] = pp.astype(BF16)
        o_ref[...] = xt_ref[...] + gate * pp

    tile = bs((tm, BW), lambda i, j: (i, j))
    return _call(
        body, "ple_fwd", (t // tm, N_SH),
        [bs((tm, D_MODEL), lambda i, j: (i, 0)), tile, bs((1, D_MODEL), lambda i, j: (0, 0)),
         bs((D_MODEL, BW), lambda i, j: (0, j)), bs((tm, BW), lambda i, j: (i, 0)),
         bs((None, BW, BW), lambda i, j: (j, 0, 0))],
        [bs((tm, D_MODEL), lambda i, j: (i, 0)), tile, tile, tile],
        [_sds((t, D_MODEL), BF16), _sds((t, D_MODEL), BF16), _sds((t, D_MODEL), BF16), _sds((t, D_MODEL), F32)])(
            x, x, g, wpg, p_i, wpp)


def loss_head(x, g, tgt):
    t = x.shape[0]
    tm = min(512, t)

    def body(x_ref, g_ref, t_ref, l_ref, dx_ref, dg_ref):
        @pl.when(pl.program_id(0) == 0)
        def _():
            l_ref[...] = jnp.zeros_like(l_ref)
            dg_ref[...] = jnp.zeros_like(dg_ref)

        xv, gv = x_ref[...], g_ref[...]
        err = _rms_fwd(xv, gv) - t_ref[...]
        part = 0.5 * jnp.sum(jnp.mean(err * err, axis=-1, keepdims=True), axis=0, keepdims=True)
        l_ref[...] += jnp.broadcast_to(part, l_ref.shape)
        dx, dgr = _rms_bwd(err * (1.0 / D_MODEL), xv, gv)
        dx_ref[...] = dx
        dg_ref[...] += _colsum(dgr)

    row = bs((tm, D_MODEL), lambda i: (i, 0))
    vec = bs((1, D_MODEL), lambda i: (0, 0))
    return _call(body, "loss_head", (t // tm,), [row, vec, row],
                 [bs((1, 128), lambda i: (0, 0)), row, vec],
                 [_sds((1, 128), F32), _sds((t, D_MODEL), F32), _sds((1, D_MODEL), F32)])(x, g, tgt)


def tn_matmul(name, a, b, grid, a_spec, b_spec, out_spec, out_shape, split=0, split_cols=0, into=None):
    last = len(grid) - 1

    def body(a_ref, b_ref, *rest):
        o_ref = rest[-1]

        @pl.when(pl.program_id(last) == 0)
        def _():
            o_ref[...] = jnp.zeros_like(o_ref)

        res = _tn(a_ref[...].astype(BF16), b_ref[...].astype(BF16))
        if split_cols:
            cols = res.shape[1] // split_cols
            for s in range(split_cols):
                o_ref[s] += res[:, s * cols:(s + 1) * cols]
        elif split:
            rows = res.shape[0] // split
            for s in range(split):
                o_ref[s] += res[s * rows:(s + 1) * rows]
        else:
            o_ref[...] += res

    if into is None:
        return _call(body, name, grid, [a_spec, b_spec], out_spec, out_shape)(a, b)
    return _call(body, name, grid, [a_spec, b_spec, ANY], out_spec, out_shape, aliases={2: 0})(a, b, into)


def norm_bwd(name, sources, add, dx_in, x, g):
    t = x.shape[0]
    tm = min(512, t)
    offs, nk = [], 0
    for s in sources:
        offs.append(nk)
        nk += s[4]
    n_src = len(sources)
    has_add = add is not None

    def body(*refs):
        a_refs = refs[0:2 * n_src:2]
        w_refs = refs[1:2 * n_src:2]
        pos = 2 * n_src
        add_ref = refs[pos] if has_add else None
        pos += int(has_add)
        dxi_ref, x_ref, g_ref, dx_ref, dg_ref, acc_ref = refs[pos:pos + 6]
        i, k = pl.program_id(0), pl.program_id(1)

        @pl.when((i == 0) & (k == 0))
        def _():
            dg_ref[...] = jnp.zeros_like(dg_ref)

        @pl.when(k == 0)
        def _():
            acc_ref[...] = add_ref[...] if has_add else jnp.zeros_like(acc_ref)

        for si in range(n_src):
            @pl.when((k >= offs[si]) & (k < offs[si] + sources[si][4]))
            def _(si=si):
                acc_ref[...] += _nt(a_refs[si][...], w_refs[si][...])

        @pl.when(k == nk - 1)
        def _():
            dx, dgr = _rms_bwd(acc_ref[...], x_ref[...], g_ref[...])
            dx_ref[...] = dxi_ref[...] + dx
            dg_ref[...] += _colsum(dgr)

    in_specs, args = [], []
    for si, (a, a_spec, w, w_spec, steps) in enumerate(sources):
        loc = functools.partial(lambda k, o, n: jnp.clip(k - o, 0, n - 1), o=offs[si], n=steps)
        in_specs.append(a_spec(tm, loc))
        in_specs.append(w_spec(loc))
        args += [a, w]
    row = bs((tm, D_MODEL), lambda i, k: (i, 0))
    vec = bs((1, D_MODEL), lambda i, k: (0, 0))
    if has_add:
        in_specs.append(row)
        args.append(add)
    in_specs += [row, row, vec]
    args += [dx_in, x, g]
    return _call(body, name, (t // tm, nk), in_specs, [row, vec],
                 [_sds((t, D_MODEL), F32), _sds((1, D_MODEL), F32)],
                 scratch=[pltpu.VMEM((tm, D_MODEL), F32)])(*args)


def ple_bwd_pre(dx, gate, pp):
    t = dx.shape[0]
    tm = min(1024, t)

    def body(dx_ref, g_ref, p_ref, dpre_ref, dpp_ref):
        d = dx_ref[...]
        g = g_ref[...].astype(F32)
        dpre_ref[...] = (d * p_ref[...].astype(F32) * g * (1.0 - g)).astype(BF16)
        dpp_ref[...] = (d * g).astype(BF16)

    row = bs((tm, D_MODEL), lambda i: (i, 0))
    return _call(body, "ple_bwd_pre", (t // tm,), [row, row, row], [row, row],
                 [_sds((t, D_MODEL), BF16), _sds((t, D_MODEL), BF16)])(dx, gate, pp)


def ffn_bwd_act(dx, wfo, fgu):
    t = dx.shape[0]
    tm = min(1024, t)
    ns = FFN_H // 2

    def body(dx_ref, w_ref, f_ref, o_ref):
        dact = _nt(dx_ref[...].astype(BF16), w_ref[...])
        fg = f_ref[0].astype(F32)
        fu = f_ref[1].astype(F32)
        s = _sigmoid(fg)
        o_ref[0] = (dact * fu * (s * (1.0 + fg * (1.0 - s)))).astype(BF16)
        o_ref[1] = (dact * fg * s).astype(BF16)

    blk = bs((2, tm, ns), lambda i, j: (0, i, j))
    return _call(body, "ffn_bwd_act", (t // tm, 2),
                 [bs((tm, D_MODEL), lambda i, j: (i, 0)), bs((ns, D_MODEL), lambda i, j: (j, 0)), blk],
                 blk, _sds((2, t, FFN_H), BF16))(dx, wfo, fgu)


def merge_bwd_pre(dx, wout, gates, ybr):
    t = dx.shape[0]
    tm = min(1024, t)

    def body(dx_ref, w_ref, g_ref, b_ref, dpre_ref, dyb_ref):
        dm = _nt(dx_ref[...].astype(BF16), w_ref[...])
        for k in range(N_BR):
            g = g_ref[k].astype(F32)
            dpre_ref[k] = (dm * b_ref[k].astype(F32) * g * (1.0 - g)).astype(BF16)
            dyb_ref[k] = (dm * g).astype(BF16)

    blk = bs((N_BR, tm, BW), lambda i, j: (0, i, j))
    return _call(body, "merge_bwd_pre", (t // tm, N_SH),
                 [bs((tm, D_MODEL), lambda i, j: (i, 0)), bs((BW, D_MODEL), lambda i, j: (j, 0)), blk, blk],
                 [blk, blk], [_sds((N_BR, t, D_MODEL), BF16), _sds((N_BR, t, D_MODEL), BF16)])(dx, wout, gates, ybr)


def branch_out_bwd(dyb, wbr):
    t = dyb.shape[1]
    tm = min(1024, t)

    def body(d_ref, w_ref, o_ref):
        acc = None
        for s in range(N_SH):
            part = _nt(d_ref[:, s * BW:(s + 1) * BW], w_ref[s])
            acc = part if acc is None else acc + part
        o_ref[...] = acc

    return _call(body, "branch_out_bwd", (t // tm, N_BR),
                 [bs((None, tm, D_MODEL), lambda i, k: (k, i, 0)), bs((N_SH, None, BW, BW), lambda i, k: (0, k, 0, 0))],
                 bs((None, tm, BW), lambda i, k: (k, i, 0)), _sds((N_BR, t, BW), F32))(dyb, wbr)


def conva_fwd(proj, wa):
    t = proj.shape[0]
    tm, halo = min(512, t), 8
    per = tm // halo

    def body(b_ref, c_ref, x_ref, ch_ref, xh_ref, w_ref, y_ref):
        zh = jnp.where(pl.program_id(0) > 0, ch_ref[...] * xh_ref[...], 0.0)
        zext = jnp.concatenate([zh, c_ref[...] * x_ref[...]], axis=0)
        y_ref[...] = (b_ref[...] * _causal_conv(zext, w_ref, CONVA_K, halo)).astype(BF16)

    col = lambda c: bs((tm, BW), lambda i: (i, c))
    hal = lambda c: bs((halo, BW), lambda i: (_prev_blk(i, per), c))
    return _call(body, "conva_fwd", (t // tm,),
                 [col(0), col(1), col(2), hal(1), hal(2), bs((CONVA_K, BW), lambda i: (0, 0))],
                 bs((tm, BW), lambda i: (i, 0)), _sds((t, BW), BF16))(proj, proj, proj, proj, proj, wa)


def conva_bwd(proj, dys, wa):
    t = proj.shape[0]
    tm, halo = min(512, t), 8
    per = tm // halo
    last = t // halo - 1
    nt = t // tm

    def body(b_ref, c_ref, x_ref, ch_ref, xh_ref, bn_ref, dy_ref, dyn_ref, w_ref, db_ref, dc_ref, dxx_ref, dw_ref):
        i = pl.program_id(0)

        @pl.when(i == 0)
        def _():
            dw_ref[...] = jnp.zeros_like(dw_ref)

        zh = jnp.where(i > 0, ch_ref[...] * xh_ref[...], 0.0)
        cv, xv = c_ref[...], x_ref[...]
        zext = jnp.concatenate([zh, cv * xv], axis=0)
        dy = dy_ref[...]
        dconv = dy * b_ref[...]
        dcn = jnp.where(i < nt - 1, dyn_ref[...] * bn_ref[...], 0.0)
        dz = _anti_conv(jnp.concatenate([dconv, dcn], axis=0), w_ref, CONVA_K, tm)
        db_ref[...] = (dy * _causal_conv(zext, w_ref, CONVA_K, halo)).astype(BF16)
        dc_ref[...] = (dz * xv).astype(BF16)
        dxx_ref[...] = (dz * cv).astype(BF16)
        _conv_wgrad(dw_ref, dconv, zext, CONVA_K, halo)

    col = lambda c: bs((tm, BW), lambda i: (i, c))
    hal = lambda c: bs((halo, BW), lambda i: (_prev_blk(i, per), c))
    nxt = bs((halo, BW), lambda i: (_next_blk(i, per, last), 0))
    wsp = bs((CONVA_K, BW), lambda i: (0, 0))
    outs = _call(body, "conva_bwd", (t // tm,),
                 [col(0), col(1), col(2), hal(1), hal(2), nxt,
                  bs((None, tm, BW), lambda i: (0, i, 0)), bs((None, halo, BW), lambda i: (0, _next_blk(i, per, last), 0)), wsp],
                 [bs((tm, BW), lambda i: (i, 0))] * 3 + [wsp],
                 [_sds((t, BW), BF16)] * 3 + [_sds((CONVA_K, BW), F32)])(proj, proj, proj, proj, proj, proj, dys, dys, wa)
    return outs[:3], outs[3]


def _head_masks():
    lane = lax.broadcasted_iota(jnp.int32, (1, BW), 1)
    return [(lane >= h * HEAD_D) & (lane < (h + 1) * HEAD_D) for h in range(HEADS)]


def _band_masks():
    qi = lax.broadcasted_iota(jnp.int32, (BLK, BLK), 0)
    ki = lax.broadcasted_iota(jnp.int32, (BLK, BLK), 1)
    return ki >= qi, ki <= qi


def qkv_cast(proj):
    t = proj.shape[0]
    tm = min(1024, t)

    def body(p_ref, o_ref):
        o_ref[...] = p_ref[...].astype(BF16)

    return _call(body, "qkv_cast", (t // tm, 3), [bs((tm, BW), lambda i, j: (i, j + 3))],
                 bs((tm, BW), lambda i, j: (i, j)), _sds((t, 3 * BW), BF16))(proj)


def attn_fwd_group(qkv, d):
    t = qkv.shape[0]
    rows = t // d
    qb = min(512, rows)
    nb = qb // BLK
    scale = HEAD_D ** -0.5
    pv = qkv.reshape(rows, d * 3 * BW)

    def body(q_ref, k_ref, v_ref, kh_ref, vh_ref, o_ref):
        n = pl.program_id(1)
        hm = _head_masks()
        m_prev, m_cur = _band_masks()
        for b in range(nb):
            rs = slice(b * BLK, (b + 1) * BLK)
            q = q_ref[rs, :]
            if b == 0:
                kp, vp = kh_ref[...], vh_ref[...]
                mp = m_prev & (n > 0)
            else:
                ps = slice((b - 1) * BLK, b * BLK)
                kp, vp = k_ref[ps, :], v_ref[ps, :]
                mp = m_prev
            qs = jnp.concatenate([jnp.where(hm[h], q, 0.0).astype(BF16) for h in range(HEADS)], axis=0)
            kcat = jnp.concatenate([kp, k_ref[rs, :]], axis=0)
            vcat = jnp.concatenate([vp, v_ref[rs, :]], axis=0)
            band = jnp.concatenate([mp, m_cur], axis=1)
            s = jnp.where(jnp.concatenate([band] * HEADS, axis=0), _nt(qs, kcat) * scale, NEG)
            m = jnp.max(s, axis=-1, keepdims=True)
            e = jnp.exp(s - m)
            l = jnp.sum(e, axis=-1, keepdims=True)
            of = _nn(e.astype(BF16), vcat) / l
            lse = m + jnp.log(l)
            o_acc = jnp.zeros((BLK, BW), F32)
            l_acc = jnp.zeros((BLK, BW), F32)
            for h in range(HEADS):
                hs = slice(h * BLK, (h + 1) * BLK)
                o_acc = jnp.where(hm[h], of[hs, :], o_acc)
                l_acc = jnp.where(hm[h], lse[hs, :], l_acc)
            o_ref[rs, :BW] = o_acc
            o_ref[rs, BW:] = l_acc

    per = qb // BLK
    main = lambda c: bs((qb, BW), lambda r, n: (n, r * 3 + c))
    hal = lambda c: bs((BLK, BW), lambda r, n: (_prev_blk(n, per), r * 3 + c))
    ol = _call(body, f"attn_fwd_d{d}", (d, rows // qb), [main(0), main(1), main(2), hal(1), hal(2)],
               bs((qb, 2 * BW), lambda r, n: (n, r)), _sds((rows, d * 2 * BW), F32))(pv, pv, pv, pv, pv)
    return ol.reshape(t, 2 * BW)


def attn_merge(ols):
    t = ols[0].shape[0]
    tm = min(1024, t)

    def lse3(a, b, c):
        m = jnp.maximum(jnp.maximum(a, b), c)
        return m + jnp.log(jnp.exp(a - m) + jnp.exp(b - m) + jnp.exp(c - m))

    def body(g0, g1, g2, y_ref, o_ref, l_ref):
        gs = [g0[...], g1[...], g2[...]]
        ls = [g[:, BW:] for g in gs]
        tot = lse3(*ls)
        o = (jnp.exp(ls[0] - tot) * gs[0][:, :BW] + jnp.exp(ls[1] - tot) * gs[1][:, :BW]
             + jnp.exp(ls[2] - tot) * gs[2][:, :BW])
        y_ref[...] = o.astype(BF16)
        o_ref[...] = o
        l_ref[...] = tot

    n = bs((tm, BW), lambda i: (i, 0))
    w = bs((tm, 2 * BW), lambda i: (i, 0))
    return _call(body, "attn_merge", (t // tm,), [w] * 3, [n, n, n],
                 [_sds((t, BW), BF16), _sds((t, BW), F32), _sds((t, BW), F32)])(*ols)


def attn_delta(dy, o, lse):
    t = o.shape[0]
    tm = min(1024, t)

    def body(d_ref, o_ref, l_ref, out_ref):
        hm = _head_masks()
        prod = d_ref[...] * o_ref[...]
        delta = jnp.zeros_like(prod)
        for h in range(HEADS):
            delta = jnp.where(hm[h], jnp.sum(jnp.where(hm[h], prod, 0.0), axis=-1, keepdims=True), delta)
        out_ref[:, :BW] = l_ref[...]
        out_ref[:, BW:] = delta

    n = bs((tm, BW), lambda i: (i, 0))
    return _call(body, "attn_delta", (t // tm,), [n, n, n], bs((tm, 2 * BW), lambda i: (i, 0)),
                 _sds((t, 2 * BW), F32))(dy, o, lse)


def attn_bwd_group(qkv, dy, ld, acc, d):
    t = qkv.shape[0]
    rows = t // d
    qb = min(512, rows)
    nb = qb // BLK
    nsteps = rows // qb
    scale = HEAD_D ** -0.5
    pv = qkv.reshape(rows, d * 3 * BW)
    dov = dy.reshape(rows, d * BW)
    ldv = ld.reshape(rows, d * 2 * BW)
    has_acc = acc is not None

    def body(*refs):
        (q_ref, qn_ref, k_ref, kh_ref, v_ref, vh_ref, do_ref, don_ref, ld_ref, ldn_ref) = refs[:10]
        a_ref = refs[10] if has_acc else None
        o_ref = refs[-1]
        n = pl.program_id(1)
        hm = _head_masks()
        m_prev, m_cur = _band_masks()
        has_prev, has_next = n > 0, n < nsteps - 1
        dq = [None] * nb
        dk = [jnp.zeros((BLK, BW), F32) for _ in range(nb)]
        dvv = [jnp.zeros((BLK, BW), F32) for _ in range(nb)]
        for qi in range(nb + 1):
            rs = slice(qi * BLK, (qi + 1) * BLK)
            ps = slice((qi - 1) * BLK, qi * BLK)
            if qi < nb:
                q, do, ldq = q_ref[rs, :], do_ref[rs, :], ld_ref[rs, :]
            else:
                q, do, ldq = qn_ref[...], don_ref[...], ldn_ref[...]
            kp, vp = (kh_ref[...], vh_ref[...]) if qi == 0 else (k_ref[ps, :], v_ref[ps, :])
            kc, vc = (k_ref[rs, :], v_ref[rs, :]) if qi < nb else (kp, vp)
            mp = m_prev & has_prev if qi == 0 else (m_prev & has_next if qi == nb else m_prev)
            mc = m_cur if qi < nb else jnp.zeros_like(m_cur)
            band = jnp.concatenate([jnp.concatenate([mp, mc], axis=1)] * HEADS, axis=0)
            qs = jnp.concatenate([jnp.where(hm[h], q, 0.0).astype(BF16) for h in range(HEADS)], axis=0)
            dos = jnp.concatenate([jnp.where(hm[h], do, 0.0).astype(BF16) for h in range(HEADS)], axis=0)
            kcat = jnp.concatenate([kp, kc], axis=0)
            vcat = jnp.concatenate([vp, vc], axis=0)
            col = lambda v, h: jnp.broadcast_to(jnp.max(jnp.where(hm[h], v, NEG), axis=-1, keepdims=True), (BLK, 2 * BLK))
            lcols = jnp.concatenate([col(ldq[:, :BW], h) for h in range(HEADS)], axis=0)
            dcols = jnp.concatenate([col(ldq[:, BW:], h) for h in range(HEADS)], axis=0)
            p = jnp.where(band, jnp.exp(_nt(qs, kcat) * scale - lcols), 0.0)
            ds = (p * (_nt(dos, vcat) - dcols) * scale).astype(BF16)
            if qi < nb:
                dqf = _nn(ds, kcat)
                acc_q = jnp.zeros((BLK, BW), F32)
                for h in range(HEADS):
                    acc_q = jnp.where(hm[h], dqf[h * BLK:(h + 1) * BLK, :], acc_q)
                dq[qi] = acc_q
            dkc = _tn(ds, qs)
            dvc = _tn(p.astype(BF16), dos)
            if qi >= 1:
                dk[qi - 1] = dk[qi - 1] + dkc[:BLK]
                dvv[qi - 1] = dvv[qi - 1] + dvc[:BLK]
            if qi < nb:
                dk[qi] = dk[qi] + dkc[BLK:]
                dvv[qi] = dvv[qi] + dvc[BLK:]
        for b in range(nb):
            rs = slice(b * BLK, (b + 1) * BLK)
            for c, val in enumerate((dq[b], dk[b], dvv[b])):
                cs = slice(c * BW, (c + 1) * BW)
                o_ref[rs, cs] = a_ref[rs, cs] + val if has_acc else val

    per = qb // BLK
    last = rows // BLK - 1
    main = lambda c: bs((qb, BW), lambda r, n: (n, r * 3 + c))
    prv = lambda c: bs((BLK, BW), lambda r, n: (_prev_blk(n, per), r * 3 + c))
    nxt = lambda c: bs((BLK, BW), lambda r, n: (_next_blk(n, per, last), r * 3 + c))
    accs = bs((qb, 3 * BW), lambda r, n: (n, r))
    in_specs = [main(0), nxt(0), main(1), prv(1), main(2), prv(2),
                bs((qb, BW), lambda r, n: (n, r)), bs((BLK, BW), lambda r, n: (_next_blk(n, per, last), r)),
                bs((qb, 2 * BW), lambda r, n: (n, r)), bs((BLK, 2 * BW), lambda r, n: (_next_blk(n, per, last), r))]
    args = [pv, pv, pv, pv, pv, pv, dov, dov, ldv, ldv]
    if has_acc:
        in_specs.append(accs)
        args.append(acc.reshape(rows, d * 3 * BW))
    out = _call(body, f"attn_bwd_d{d}", (d, nsteps), in_specs, accs, _sds((rows, d * 3 * BW), F32),
                aliases={10: 0} if has_acc else None)(*args)
    return out.reshape(t, 3 * BW)


def _group_masks():
    lane = lax.broadcasted_iota(jnp.int32, (1, BW), 1)
    return [(lane >= g * HEAD_D) & (lane < (g + 1) * HEAD_D) for g in range(4)]


def sgu_fwd(proj, ln_g, ln_b, w_tril, b_full):
    t = proj.shape[0]
    tm = min(512, t)

    def body(u_ref, v_ref, g_ref, b_ref, w_ref, bf_ref, y_ref):
        gm = _group_masks()
        xhat, _ = _ln_hat(v_ref[...])
        vb = (xhat * g_ref[...] + b_ref[...]).astype(BF16)
        for c in range(tm // BLK):
            rs = slice(c * BLK, (c + 1) * BLK)
            vc = vb[rs, :]
            mixed = bf_ref[...]
            for g in range(4):
                mixed = mixed + jnp.where(gm[g], _nn(w_ref[g], vc), 0.0)
            y_ref[rs, :] = (u_ref[rs, :] * mixed).astype(BF16)

    vec = bs((1, BW), lambda i: (0, 0))
    return _call(body, "sgu_fwd", (t // tm,),
                 [bs((tm, BW), lambda i: (i, 6)), bs((tm, BW), lambda i: (i, 7)), vec, vec,
                  bs((4, BLK, BLK), lambda i: (0, 0, 0)), bs((BLK, BW), lambda i: (0, 0))],
                 bs((tm, BW), lambda i: (i, 0)), _sds((t, BW), BF16))(proj, proj, ln_g, ln_b, w_tril, b_full)


def sgu_bwd(proj, dys, ln_g, ln_b, w_tril, b_full):
    t = proj.shape[0]
    tm = min(512, t)

    def body(u_ref, v_ref, dy_ref, g_ref, b_ref, w_ref, bf_ref, du_ref, dv_ref, dw_ref, dbf_ref, dg_ref, db_ref, dvl_ref):
        @pl.when(pl.program_id(0) == 0)
        def _():
            dw_ref[...] = jnp.zeros_like(dw_ref)
            dbf_ref[...] = jnp.zeros_like(dbf_ref)
            dg_ref[...] = jnp.zeros_like(dg_ref)
            db_ref[...] = jnp.zeros_like(db_ref)

        gm = _group_masks()
        xhat, r = _ln_hat(v_ref[...])
        gv = g_ref[...]
        vb = (xhat * gv + b_ref[...]).astype(BF16)
        for c in range(tm // BLK):
            rs = slice(c * BLK, (c + 1) * BLK)
            vc = vb[rs, :]
            dy = dy_ref[rs, :]
            mixed = bf_ref[...]
            for g in range(4):
                mixed = mixed + jnp.where(gm[g], _nn(w_ref[g], vc), 0.0)
            du_ref[rs, :] = (dy * mixed).astype(BF16)
            dm = dy * u_ref[rs, :]
            dbf_ref[...] += dm
            dvl = jnp.zeros((BLK, BW), F32)
            for g in range(4):
                dmg = jnp.where(gm[g], dm, 0.0).astype(BF16)
                dw_ref[g] += _nt(dmg, vc)
                dvl = dvl + _tn(w_ref[g], dmg)
            dvl_ref[rs, :] = dvl
        dvl = dvl_ref[...]
        dv_ref[...] = _ln_bwd(dvl, xhat, r, gv).astype(BF16)
        dg_ref[...] += _colsum(dvl * xhat)
        db_ref[...] += _colsum(dvl)

    vec = bs((1, BW), lambda i: (0, 0))
    row = bs((tm, BW), lambda i: (i, 0))
    wsp = bs((4, BLK, BLK), lambda i: (0, 0, 0))
    bfs = bs((BLK, BW), lambda i: (0, 0))
    return _call(body, "sgu_bwd", (t // tm,),
                 [bs((tm, BW), lambda i: (i, 6)), bs((tm, BW), lambda i: (i, 7)), bs((None, tm, BW), lambda i: (2, i, 0)),
                  vec, vec, wsp, bfs],
                 [row, row, wsp, bfs, vec, vec],
                 [_sds((t, BW), BF16), _sds((t, BW), BF16), _sds((4, BLK, BLK), F32), _sds((BLK, BW), F32),
                  _sds((1, BW), F32), _sds((1, BW), F32)],
                 scratch=[pltpu.VMEM((tm, BW), F32)])(proj, proj, dys, ln_g, ln_b, w_tril, b_full)


CONF_HALO = 32


def conf_fwd(proj, dw, ln_g, ln_b):
    t = proj.shape[0]
    tm, halo = min(512, t), CONF_HALO
    per = tm // halo

    def body(v_ref, gt_ref, vh_ref, gh_ref, w_ref, g_ref, b_ref, y_ref, z_ref):
        yh = jnp.where(pl.program_id(0) > 0, vh_ref[...] * _sigmoid(gh_ref[...]), 0.0)
        yext = jnp.concatenate([yh, v_ref[...] * _sigmoid(gt_ref[...])], axis=0)
        z = _causal_conv(yext, w_ref, CONF_K, halo)
        z_ref[...] = z
        xhat, _ = _ln_hat(z)
        ln = xhat * g_ref[...] + b_ref[...]
        y_ref[...] = (ln * _sigmoid(ln)).astype(BF16)

    vec = bs((1, BW), lambda i: (0, 0))
    col = lambda c: bs((tm, BW), lambda i: (i, c))
    hal = lambda c: bs((halo, BW), lambda i: (_prev_blk(i, per), c))
    row = bs((tm, BW), lambda i: (i, 0))
    return _call(body, "conf_fwd", (t // tm,),
                 [col(8), col(9), hal(8), hal(9), bs((CONF_K, BW), lambda i: (0, 0)), vec, vec],
                 [row, row], [_sds((t, BW), BF16), _sds((t, BW), F32)])(proj, proj, proj, proj, dw, ln_g, ln_b)


def conf_bwd_ln(z, dys, ln_g, ln_b):
    t = z.shape[0]
    tm = min(1024, t)

    def body(z_ref, dy_ref, g_ref, b_ref, dz_ref, dg_ref, db_ref):
        @pl.when(pl.program_id(0) == 0)
        def _():
            dg_ref[...] = jnp.zeros_like(dg_ref)
            db_ref[...] = jnp.zeros_like(db_ref)

        gv = g_ref[...]
        xhat, r = _ln_hat(z_ref[...])
        ln = xhat * gv + b_ref[...]
        s = _sigmoid(ln)
        dln = dy_ref[...] * (s * (1.0 + ln * (1.0 - s)))
        dz_ref[...] = _ln_bwd(dln, xhat, r, gv)
        dg_ref[...] += _colsum(dln * xhat)
        db_ref[...] += _colsum(dln)

    vec = bs((1, BW), lambda i: (0, 0))
    row = bs((tm, BW), lambda i: (i, 0))
    return _call(body, "conf_bwd_ln", (t // tm,), [row, bs((None, tm, BW), lambda i: (3, i, 0)), vec, vec],
                 [row, vec, vec], [_sds((t, BW), F32), _sds((1, BW), F32), _sds((1, BW), F32)])(z, dys, ln_g, ln_b)


def conf_bwd_conv(proj, dz, dw):
    t = proj.shape[0]
    tm, halo = min(512, t), CONF_HALO
    per = tm // halo
    last = t // halo - 1
    nt = t // tm

    def body(v_ref, gt_ref, vh_ref, gh_ref, dz_ref, dzn_ref, w_ref, dv_ref, dg_ref, dw_ref):
        i = pl.program_id(0)

        @pl.when(i == 0)
        def _():
            dw_ref[...] = jnp.zeros_like(dw_ref)

        val = v_ref[...]
        sg = _sigmoid(gt_ref[...])
        yh = jnp.where(i > 0, vh_ref[...] * _sigmoid(gh_ref[...]), 0.0)
        yext = jnp.concatenate([yh, val * sg], axis=0)
        dz = dz_ref[...]
        dzn = jnp.where(i < nt - 1, dzn_ref[...], 0.0)
        dy0 = _anti_conv(jnp.concatenate([dz, dzn], axis=0), w_ref, CONF_K, tm)
        dv_ref[...] = (dy0 * sg).astype(BF16)
        dg_ref[...] = (dy0 * val * sg * (1.0 - sg)).astype(BF16)
        _conv_wgrad(dw_ref, dz, yext, CONF_K, halo)

    col = lambda c: bs((tm, BW), lambda i: (i, c))
    hal = lambda c: bs((halo, BW), lambda i: (_prev_blk(i, per), c))
    row = bs((tm, BW), lambda i: (i, 0))
    wsp = bs((CONF_K, BW), lambda i: (0, 0))
    return _call(body, "conf_bwd_conv", (t // tm,),
                 [col(8), col(9), hal(8), hal(9), row, bs((halo, BW), lambda i: (_next_blk(i, per, last), 0)), wsp],
                 [row, row, wsp], [_sds((t, BW), BF16), _sds((t, BW), BF16), _sds((CONF_K, BW), F32)])(
                     proj, proj, proj, proj, dz, dz, dw)


def _place():
    return lax.axis_index("x"), lax.axis_index("y"), lax.axis_index("c")


def _comm_call(body, name, n_in, out_shape, scratch, aliases=None):
    return pl.pallas_call(body, name=name, in_specs=[ANY] * n_in, out_specs=[ANY] * len(out_shape), out_shape=out_shape,
                          scratch_shapes=scratch, input_output_aliases=aliases or {},
                          compiler_params=pltpu.CompilerParams(has_side_effects=True, vmem_limit_bytes=VMEM_LIMIT))


def gather8(v, reduce):
    rows, cols = v.shape

    def body(v_ref, o_ref, land_ref, send, recv, lsem):
        x, y, c = _place()
        me = 4 * x + 2 * y + c
        land = land_ref if reduce else o_ref
        mine = pltpu.make_async_copy(v_ref, land.at[me], lsem)
        mine.start()
        sent = []
        for j in range(1, 8):
            fx, fy, fc = (j >> 2) & 1, (j >> 1) & 1, j & 1
            tgt = (1 - x if fx else x, 1 - y if fy else y, 1 - c if fc else c)
            cp = pltpu.make_async_remote_copy(src_ref=v_ref, dst_ref=land.at[me], send_sem=send.at[j - 1],
                                              recv_sem=recv.at[j - 1], device_id=tgt, device_id_type=MESH)
            cp.start()
            sent.append(cp)
        for j in range(1, 8):
            fx, fy, fc = (j >> 2) & 1, (j >> 1) & 1, j & 1
            peer = 4 * (1 - x if fx else x) + 2 * (1 - y if fy else y) + (1 - c if fc else c)
            pltpu.make_async_remote_copy(src_ref=v_ref, dst_ref=land.at[peer], send_sem=send.at[j - 1],
                                         recv_sem=recv.at[j - 1], device_id=(x, y, c), device_id_type=MESH).wait_recv()
        for cp in sent:
            cp.wait_send()
        mine.wait()
        if reduce:
            acc = land_ref[0]
            for k in range(1, 8):
                acc = acc + land_ref[k]
            o_ref[...] = acc

    vm = pl.BlockSpec(memory_space=pltpu.VMEM)
    out_shape = _sds((rows, cols), F32) if reduce else _sds((8, rows, cols), F32)
    land_shape = (8, rows, cols) if reduce else (8, 128)
    return pl.pallas_call(
        body, name="allreduce8" if reduce else "allgather8", in_specs=[vm], out_specs=vm, out_shape=out_shape,
        scratch_shapes=[pltpu.VMEM(land_shape, F32), pltpu.SemaphoreType.DMA((7,)), pltpu.SemaphoreType.DMA((7,)),
                        pltpu.SemaphoreType.DMA],
        compiler_params=pltpu.CompilerParams(has_side_effects=True, vmem_limit_bytes=VMEM_LIMIT))(v)


def allgather_weights(bufs):
    n = len(bufs)

    def body(*refs):
        ins, outs = refs[:n], refs[n:2 * n]
        send, recv = refs[2 * n:]
        x, y, c = _place()
        s_me = 2 * x + y
        chips = [(1 - x, y), (x, 1 - y), (1 - x, 1 - y)]
        sibling = (x, y, 1 - c)
        started = []
        for a in range(n):
            hl = bufs[a].shape[1] // 2
            half = pl.ds(c * hl, hl)
            for j, chip in enumerate(chips):
                cp = pltpu.make_async_remote_copy(src_ref=ins[a].at[s_me, half], dst_ref=outs[a].at[s_me, half],
                                                  send_sem=send.at[6 * a + j], recv_sem=recv.at[6 * a + j],
                                                  device_id=(chip[0], chip[1], c), device_id_type=MESH)
                cp.start()
                started.append(cp)
        for a in range(n):
            hl = bufs[a].shape[1] // 2
            half = pl.ds(c * hl, hl)
            for j, chip in enumerate(chips):
                s_j = 2 * chip[0] + chip[1]
                landed = outs[a].at[s_j, half]
                pltpu.make_async_remote_copy(src_ref=landed, dst_ref=landed, send_sem=send.at[6 * a + j],
                                             recv_sem=recv.at[6 * a + j], device_id=sibling, device_id_type=MESH).wait_recv()
                fw = pltpu.make_async_remote_copy(src_ref=landed, dst_ref=landed, send_sem=send.at[6 * a + 3 + j],
                                                  recv_sem=recv.at[6 * a + 3 + j], device_id=sibling, device_id_type=MESH)
                fw.start()
                started.append(fw)
        for a in range(n):
            hl = bufs[a].shape[1] // 2
            other = pl.ds((1 - c) * hl, hl)
            for j, chip in enumerate(chips):
                s_j = 2 * chip[0] + chip[1]
                theirs = outs[a].at[s_j, other]
                pltpu.make_async_remote_copy(src_ref=theirs, dst_ref=theirs, send_sem=send.at[6 * a + 3 + j],
                                             recv_sem=recv.at[6 * a + 3 + j], device_id=sibling, device_id_type=MESH).wait_recv()
        for cp in started:
            cp.wait_send()

    out_shape = [_sds(b.shape, b.dtype) for b in bufs]
    scratch = [pltpu.SemaphoreType.DMA((6 * n,)), pltpu.SemaphoreType.DMA((6 * n,))]
    return _comm_call(body, "allgather_weights", n, out_shape, scratch, aliases={a: a for a in range(n)})(*bufs)


def _row_tile(rows, cols):
    best = 16
    for t in range(16, rows + 1, 16):
        if rows % t == 0 and t * cols * 4 <= 2 * 1024 * 1024:
            best = t
    return best


def reduce_scatter_grads(gs):
    n = len(gs)
    x, y, c = _place()
    scal = jnp.stack([2 * x + y, c]).astype(jnp.int32)
    hrs = [g.shape[1] // 2 for g in gs]

    def swap_body(*refs):
        ins, outs = refs[:n], refs[n:2 * n]
        send, recv = refs[2 * n:]
        xx, yy, cc = _place()
        cps = []
        for a in range(n):
            cp = pltpu.make_async_remote_copy(src_ref=ins[a].at[:, pl.ds((1 - cc) * hrs[a], hrs[a])], dst_ref=outs[a],
                                              send_sem=send.at[a], recv_sem=recv.at[a],
                                              device_id=(xx, yy, 1 - cc), device_id_type=MESH)
            cp.start()
            cps.append(cp)
        for cp in cps:
            cp.wait()

    ras = _comm_call(swap_body, "rs_swap_halves", n, [_sds((N_SH, hrs[a], gs[a].shape[2]), F32) for a in range(n)],
                     [pltpu.SemaphoreType.DMA((n,)), pltpu.SemaphoreType.DMA((n,))])(*gs)

    def add_sibling(g, ra, hr):
        cols = g.shape[2]
        tr = _row_tile(hr, cols)
        nr = hr // tr

        def body(s_ref, g_ref, r_ref, p32_ref, p16_ref):
            v = g_ref[...] + r_ref[...]
            p32_ref[...] = v
            p16_ref[...] = v.astype(BF16)

        blk = lambda f: bs((None, tr, cols), f)
        own = blk(lambda s, i, sr: (s, i, 0))
        spec = pltpu.PrefetchScalarGridSpec(num_scalar_prefetch=1, grid=(N_SH, nr),
                                            in_specs=[blk(lambda s, i, sr: (s, sr[1] * nr + i, 0)), own],
                                            out_specs=[own, own])
        return pl.pallas_call(body, name="rs_add_sibling", grid_spec=spec,
                              out_shape=[_sds((N_SH, hr, cols), F32), _sds((N_SH, hr, cols), BF16)],
                              compiler_params=pltpu.CompilerParams(dimension_semantics=("arbitrary",) * 2,
                                                                   vmem_limit_bytes=VMEM_LIMIT))(scal, g, ra)

    parts = [add_sibling(gs[a], ras[a], hrs[a]) for a in range(n)]

    def ici_body(*refs):
        ins, outs = refs[:n], refs[n:2 * n]
        send, recv = refs[2 * n:]
        xx, yy, cc = _place()
        chips = [(1 - xx, yy), (xx, 1 - yy), (1 - xx, 1 - yy)]
        cps = []
        for a in range(n):
            for j, chip in enumerate(chips):
                cp = pltpu.make_async_remote_copy(src_ref=ins[a].at[2 * chip[0] + chip[1]], dst_ref=outs[a].at[j],
                                                  send_sem=send.at[3 * a + j], recv_sem=recv.at[3 * a + j],
                                                  device_id=(chip[0], chip[1], cc), device_id_type=MESH)
                cp.start()
                cps.append(cp)
        for cp in cps:
            cp.wait()

    rbs = _comm_call(ici_body, "rs_send_partials", n, [_sds((3, hrs[a], gs[a].shape[2]), BF16) for a in range(n)],
                     [pltpu.SemaphoreType.DMA((3 * n,)), pltpu.SemaphoreType.DMA((3 * n,))])(*[p[1] for p in parts])

    def add_chips(p32, rb, hr):
        cols = p32.shape[2]
        tr = _row_tile(hr, cols)
        nr = hr // tr

        def body(s_ref, p_ref, r0, r1, r2, o_ref):
            o_ref[...] = ((p_ref[...] + r0[...].astype(F32)) + r1[...].astype(F32)) + r2[...].astype(F32)

        blk = lambda f: bs((None, tr, cols), f)
        spec = pltpu.PrefetchScalarGridSpec(
            num_scalar_prefetch=1, grid=(nr,),
            in_specs=[blk(lambda i, sr: (sr[0], i, 0))] + [blk(functools.partial(lambda i, sr, j: (j, i, 0), j=j))
                                                            for j in range(3)],
            out_specs=blk(lambda i, sr: (sr[1], i, 0)))
        return pl.pallas_call(body, name="rs_add_chips", grid_spec=spec, out_shape=_sds((2, hr, cols), F32),
                              compiler_params=pltpu.CompilerParams(dimension_semantics=("arbitrary",),
                                                                   vmem_limit_bytes=VMEM_LIMIT))(scal, p32, rb, rb, rb)

    fins = [add_chips(parts[a][0], rbs[a], hrs[a]) for a in range(n)]

    def share_body(*refs):
        ins, outs = refs[:n], refs[n:2 * n]
        send, recv = refs[2 * n:]
        xx, yy, cc = _place()
        sib = (xx, yy, 1 - cc)
        cps = []
        for a in range(n):
            cp = pltpu.make_async_remote_copy(src_ref=ins[a].at[cc], dst_ref=outs[a].at[cc], send_sem=send.at[a],
                                              recv_sem=recv.at[a], device_id=sib, device_id_type=MESH)
            cp.start()
            cps.append(cp)
        for a in range(n):
            pltpu.make_async_remote_copy(src_ref=ins[a].at[cc], dst_ref=outs[a].at[1 - cc], send_sem=send.at[a],
                                         recv_sem=recv.at[a], device_id=sib, device_id_type=MESH).wait_recv()
        for cp in cps:
            cp.wait_send()

    fulls = _comm_call(share_body, "rs_share_halves", n, [_sds(f.shape, F32) for f in fins],
                       [pltpu.SemaphoreType.DMA((n,)), pltpu.SemaphoreType.DMA((n,))],
                       aliases={a: a for a in range(n)})(*fins)
    return [f.reshape(2 * hr, f.shape[2]) for f, hr in zip(fulls, hrs)]


def adamw(w, g, m, v):
    shape = w.shape
    cols = shape[-1]
    rows = math.prod(shape[:-1]) if len(shape) > 1 else 1
    tr = 256 if rows % 256 == 0 and rows > 256 else rows
    c1 = 1.0 - ADAM_B1 ** ADAM_STEP
    c2 = 1.0 - ADAM_B2 ** ADAM_STEP

    def body(w_ref, g_ref, m_ref, v_ref, d_ref, nm_ref, nv_ref):
        gv = g_ref[...]
        nm = ADAM_B1 * m_ref[...] + (1.0 - ADAM_B1) * gv
        nv = ADAM_B2 * v_ref[...] + (1.0 - ADAM_B2) * (gv * gv)
        nm_ref[...] = nm
        nv_ref[...] = nv
        d_ref[...] = -ADAM_LR * ((nm / c1) / (jnp.sqrt(nv / c2) + ADAM_EPS) + ADAM_WD * w_ref[...])

    row = bs((tr, cols), lambda i: (i, 0))
    outs = _call(body, "adamw", (rows // tr,), [row] * 4, [row] * 3, [_sds((rows, cols), F32)] * 3)(
        *[a.reshape(rows, cols) for a in (w, g, m, v)])
    return [o.reshape(shape) for o in outs]


def layer_fwd(x, p_i, w):
    h, proj = norm_in_proj(x, w["g_mix"], w["win"])
    ya = conva_fwd(proj, w["conv_a"])
    qkv = qkv_cast(proj)
    yb, o32, lse = attn_merge([attn_fwd_group(qkv, d) for d in DILATIONS])
    yc = sgu_fwd(proj, w["sgu_ln_g"], w["sgu_ln_b"], w["sgu_wt"], w["sgu_bf"])
    yd, z = conf_fwd(proj, w["conf_dw"], w["conf_ln_g"], w["conf_ln_b"])
    ys = (ya, yb, yc, yd)
    merged, gates, ybr = merge_fwd(h, ys, w["wg"], w["wbr"])
    x1 = mm_residual(merged, w["wout"], x, "attn_out")
    h2, fgu, act = ffn_in(x1, w["g_ffn"], w["wfi"])
    x2 = mm_residual(act, w["wfo"], x1, "ffn_out")
    h3, gate, pp, x3 = ple_fwd(x2, w["g_ple"], w["wpg"], p_i, w["wpp"])
    saved = dict(x=x, h=h, proj=proj, qkv=qkv, ys=ys, o32=o32, lse=lse, z=z, merged=merged, gates=gates, ybr=ybr, x1=x1,
                 h2=h2, fgu=fgu, act=act, x2=x2, h3=h3, gate=gate, pp=pp)
    return x3, saved


def layer_bwd(dx3, p_i, w, s):
    t = dx3.shape[0]
    tr = min(1024, t)
    nr = t // tr
    ns_fi = FFN_H // 2
    small = {}

    dpre, dpp = ple_bwd_pre(dx3, s["gate"], s["pp"])
    ga_shape, gb_shape = _sds((N_SH, 6 * BW, D_MODEL), F32), _sds((N_SH, 5 * BW, BW), F32)
    ga_blk = lambda idx: bs((N_SH, BW, D_MODEL), idx)
    ga = tn_matmul("dw_ple_gate", s["h3"], dpre, (nr,), bs((tr, D_MODEL), lambda r: (r, 0)),
                   bs((tr, D_MODEL), lambda r: (r, 0)), ga_blk(lambda r: (0, 5, 0)), ga_shape, split=N_SH)
    gb = tn_matmul("dw_ple_proj", p_i, dpp, (N_SH, nr), bs((tr, BW), lambda j, r: (r, 0)),
                   bs((tr, BW), lambda j, r: (r, j)), bs((None, BW, BW), lambda j, r: (j, 4, 0)), gb_shape)
    dx2, small["g_ple"] = norm_bwd(
        "ple_norm_bwd",
        [(dpre, lambda tm, loc: bs((tm, D_MODEL), lambda i, k: (i, loc(k))), w["wpg"],
          lambda loc: bs((D_MODEL, D_MODEL), lambda i, k: (0, loc(k))), 1)],
        None, dx3, s["x2"], w["g_ple"])

    df = ffn_bwd_act(dx2, w["wfo"], s["fgu"])
    gfo = tn_matmul("dw_ffn_out", s["act"], dx2, (2, nr), bs((tr, ns_fi), lambda j, r: (r, j)),
                    bs((tr, D_MODEL), lambda j, r: (r, 0)), bs((2, FFN_H // N_SH, D_MODEL), lambda j, r: (j, 0, 0)),
                    _sds((N_SH, FFN_H // N_SH, D_MODEL), F32), split=2)
    gfi = tn_matmul("dw_ffn_in", s["h2"], df, (N_SH, nr), bs((tr, D_MODEL), lambda j, r: (r, 0)),
                    bs((None, tr, ns_fi), lambda j, r: (j // 2, r, j % 2)),
                    bs((None, D_MODEL, ns_fi), lambda j, r: (j, 0, 0)), _sds((N_SH, D_MODEL, ns_fi), F32))
    dx1, small["g_ffn"] = norm_bwd(
        "ffn_norm_bwd",
        [(df, lambda tm, loc: bs((None, tm, ns_fi), lambda i, k: (loc(k) // 2, i, loc(k) % 2)), w["wfi"],
          lambda loc: bs((None, D_MODEL, ns_fi), lambda i, k: (loc(k), 0, 0)), N_SH)],
        None, dx2, s["x1"], w["g_ffn"])

    dpre_m, dyb = merge_bwd_pre(dx1, w["wout"], s["gates"], s["ybr"])
    ga = tn_matmul("dw_out", s["merged"], dx1, (nr,), bs((tr, D_MODEL), lambda r: (r, 0)),
                   bs((tr, D_MODEL), lambda r: (r, 0)), ga_blk(lambda r: (0, 4, 0)), ga_shape, split=N_SH, into=ga)
    ga = tn_matmul("dw_merge_gate", s["h"], dpre_m, (N_BR, nr), bs((tr, D_MODEL), lambda k, r: (r, 0)),
                   bs((None, tr, D_MODEL), lambda k, r: (k, r, 0)), ga_blk(lambda k, r: (0, k, 0)), ga_shape,
                   split=N_SH, into=ga)
    for k in range(N_BR):
        gb = tn_matmul("dw_branch", s["ys"][k], dyb, (nr,), bs((tr, BW), lambda r: (r, 0)),
                       bs((None, tr, D_MODEL), functools.partial(lambda r, kk: (kk, r, 0), kk=k)),
                       bs((N_SH, BW, BW), functools.partial(lambda r, kk: (0, kk, 0), kk=k)), gb_shape,
                       split_cols=N_SH, into=gb)
    dys = branch_out_bwd(dyb, w["wbr"])

    (dab, dac, dax), small["conv_a"] = conva_bwd(s["proj"], dys, w["conv_a"])
    dy_b = dys[1]
    ld = attn_delta(dy_b, s["o32"], s["lse"])
    acc = None
    for d in reversed(DILATIONS):
        acc = attn_bwd_group(s["qkv"], dy_b, ld, acc, d)
    du, dv, d_sw, d_sbf, small["sgu_ln_g"], small["sgu_ln_b"] = sgu_bwd(
        s["proj"], dys, w["sgu_ln_g"], w["sgu_ln_b"], w["sgu_wt"], w["sgu_bf"])
    small["sgu_w"] = jnp.where(jnp.tril(jnp.ones((BLK, BLK), bool))[None], d_sw, 0.0)
    small["sgu_b"] = jnp.sum(d_sbf.reshape(BLK, 4, HEAD_D), axis=-1).T
    dz, small["conf_ln_g"], small["conf_ln_b"] = conf_bwd_ln(s["z"], dys, w["conf_ln_g"], w["conf_ln_b"])
    dval, dgate, small["conf_dw"] = conf_bwd_conv(s["proj"], dz, w["conf_dw"])
    dproj = jnp.concatenate([dab, dac, dax, acc.astype(BF16), du, dv, dval, dgate], axis=1)

    ns_in = N_IN // N_SH
    gin = tn_matmul("dw_in", s["h"], dproj, (N_SH, nr), bs((tr, D_MODEL), lambda j, r: (r, 0)),
                    bs((tr, ns_in), lambda j, r: (r, j)), bs((None, D_MODEL, ns_in), lambda j, r: (j, 0, 0)),
                    _sds((N_SH, D_MODEL, ns_in), F32))
    big = [ga, gfo, gb, gin, gfi]
    dx, small["g_mix"] = norm_bwd(
        "mix_norm_bwd",
        [(dpre_m, lambda tm, loc: bs((None, tm, D_MODEL), lambda i, k: (loc(k), i, 0)), w["wg"],
          lambda loc: bs((None, D_MODEL, D_MODEL), lambda i, k: (loc(k), 0, 0)), N_BR),
         (dproj, lambda tm, loc: bs((tm, ns_in), lambda i, k: (i, loc(k))), w["win"],
          lambda loc: bs((None, D_MODEL, ns_in), lambda i, k: (loc(k), 0, 0)), N_SH)],
        None, dx1, s["x"], w["g_mix"])
    return dx, big, small


BIG_NAMES = ("w_in", "w_branch", "w_merge_gate", "w_out", "w_ffn_in", "w_ffn_out", "w_ple_gate", "w_ple_proj")


def unpack_big_grads(ga, gfo, gb, gin, gfi):
    return dict(w_in=gin, w_ffn_in=gfi, w_ffn_out=gfo,
                w_merge_gate=ga[:N_BR * BW].reshape(N_BR, BW, D_MODEL), w_out=ga[N_BR * BW:5 * BW], w_ple_gate=ga[5 * BW:],
                w_branch=gb[:N_BR * BW].reshape(N_BR, BW, BW), w_ple_proj=gb[N_BR * BW:])


SMALL_NAMES = ("g_mix", "conv_a", "sgu_ln_g", "sgu_ln_b", "sgu_w", "sgu_b", "conf_dw", "conf_ln_g", "conf_ln_b",
               "g_ffn", "g_ple")


def _pack_rows(arrays, rows):
    flat = jnp.concatenate([a.reshape(-1) for a in arrays])
    return jnp.pad(flat, (0, rows * D_MODEL - flat.shape[0])).reshape(rows, D_MODEL)


def _unpack_rows(packed, shapes):
    flat, out, pos = packed.reshape(-1), [], 0
    for shape in shapes:
        n = math.prod(shape)
        out.append(flat[pos:pos + n].reshape(shape))
        pos += n
    return out


def kernel(x, p, g_mix, w_in, conv_a, sgu_ln_g, sgu_ln_b, sgu_w, sgu_b, conf_dw, conf_ln_g, conf_ln_b, w_branch, w_merge_gate, w_out, g_ffn, w_ffn_in, w_ffn_out, g_ple, w_ple_gate, w_ple_proj, g_final, loss_target, m_g_mix, m_w_in, m_conv_a, m_sgu_ln_g, m_sgu_ln_b, m_sgu_w, m_sgu_b, m_conf_dw, m_conf_ln_g, m_conf_ln_b, m_w_branch, m_w_merge_gate, m_w_out, m_g_ffn, m_w_ffn_in, m_w_ffn_out, m_g_ple, m_w_ple_gate, m_w_ple_proj, m_g_final, v_g_mix, v_w_in, v_conv_a, v_sgu_ln_g, v_sgu_ln_b, v_sgu_w, v_sgu_b, v_conf_dw, v_conf_ln_g, v_conf_ln_b, v_w_branch, v_w_merge_gate, v_w_out, v_g_ffn, v_w_ffn_in, v_w_ffn_out, v_g_ple, v_w_ple_gate, v_w_ple_proj, v_g_final):
    weights = dict(g_mix=g_mix, w_in=w_in, conv_a=conv_a, sgu_ln_g=sgu_ln_g, sgu_ln_b=sgu_ln_b, sgu_w=sgu_w, sgu_b=sgu_b,
                   conf_dw=conf_dw, conf_ln_g=conf_ln_g, conf_ln_b=conf_ln_b, w_branch=w_branch, w_merge_gate=w_merge_gate,
                   w_out=w_out, g_ffn=g_ffn, w_ffn_in=w_ffn_in, w_ffn_out=w_ffn_out, g_ple=g_ple, w_ple_gate=w_ple_gate,
                   w_ple_proj=w_ple_proj, g_final=g_final)
    m_in = dict(g_mix=m_g_mix, w_in=m_w_in, conv_a=m_conv_a, sgu_ln_g=m_sgu_ln_g, sgu_ln_b=m_sgu_ln_b, sgu_w=m_sgu_w,
                sgu_b=m_sgu_b, conf_dw=m_conf_dw, conf_ln_g=m_conf_ln_g, conf_ln_b=m_conf_ln_b, w_branch=m_w_branch,
                w_merge_gate=m_w_merge_gate, w_out=m_w_out, g_ffn=m_g_ffn, w_ffn_in=m_w_ffn_in, w_ffn_out=m_w_ffn_out,
                g_ple=m_g_ple, w_ple_gate=m_w_ple_gate, w_ple_proj=m_w_ple_proj, g_final=m_g_final)
    v_in = dict(g_mix=v_g_mix, w_in=v_w_in, conv_a=v_conv_a, sgu_ln_g=v_sgu_ln_g, sgu_ln_b=v_sgu_ln_b, sgu_w=v_sgu_w,
                sgu_b=v_sgu_b, conf_dw=v_conf_dw, conf_ln_g=v_conf_ln_g, conf_ln_b=v_conf_ln_b, w_branch=v_w_branch,
                w_merge_gate=v_w_merge_gate, w_out=v_w_out, g_ffn=v_g_ffn, w_ffn_in=v_w_ffn_in, w_ffn_out=v_w_ffn_out,
                g_ple=v_g_ple, w_ple_gate=v_w_ple_gate, w_ple_proj=v_w_ple_proj, g_final=v_g_final)
    order = ("g_mix", "w_in", "conv_a", "sgu_ln_g", "sgu_ln_b", "sgu_w", "sgu_b", "conf_dw", "conf_ln_g", "conf_ln_b",
             "w_branch", "w_merge_gate", "w_out", "g_ffn", "w_ffn_in", "w_ffn_out", "g_ple", "w_ple_gate", "w_ple_proj",
             "g_final")
    depth = g_mix.shape[0]
    xs, tgt = x[0], loss_target[0]
    cw = BW // N_SH
    my_shard = 2 * lax.axis_index("x") + lax.axis_index("y")

    conv_rows = 16
    allc = gather8(_pack_rows([conv_a, conf_dw], conv_rows), reduce=False)
    shards = [_unpack_rows(allc[2 * s], [conv_a.shape, conf_dw.shape]) for s in range(N_SH)]
    conv_a_full = jnp.concatenate([sh[0] for sh in shards], axis=-1)
    conf_dw_full = jnp.concatenate([sh[1] for sh in shards], axis=-1)

    tril = jnp.tril(jnp.ones((BLK, BLK), bool))
    layers = []
    for i in range(depth):
        shards = ([w_in[i], w_branch[i]] + [w_merge_gate[i, k] for k in range(N_BR)]
                  + [w_out[i], w_ffn_in[i], w_ffn_out[i], w_ple_gate[i], w_ple_proj[i]])
        got = allgather_weights([
            lax.dynamic_update_slice(jnp.zeros((N_SH,) + sh.shape, BF16), sh.astype(BF16)[None], (my_shard,) + (0,) * sh.ndim)
            for sh in shards])
        vec = lambda a: a[i].reshape(1, -1)
        layers.append(dict(
            win=got[0], wbr=got[1], wg=jnp.stack([g.reshape(D_MODEL, D_MODEL) for g in got[2:6]]),
            wout=got[6].reshape(D_MODEL, D_MODEL), wfi=got[7], wfo=got[8].reshape(FFN_H, D_MODEL),
            wpg=got[9].reshape(D_MODEL, D_MODEL), wpp=got[10],
            g_mix=vec(g_mix), g_ffn=vec(g_ffn), g_ple=vec(g_ple), conv_a=conv_a_full[i], conf_dw=conf_dw_full[i],
            sgu_ln_g=vec(sgu_ln_g), sgu_ln_b=vec(sgu_ln_b), conf_ln_g=vec(conf_ln_g), conf_ln_b=vec(conf_ln_b),
            sgu_wt=jnp.where(tril[None], sgu_w[i], 0.0).astype(BF16),
            sgu_bf=jnp.repeat(sgu_b[i].T, HEAD_D, axis=1)))

    act, saved = xs, []
    for i in range(depth):
        act, sv = layer_fwd(act, p[i, 0], layers[i])
        saved.append(sv)
    loss_part, dx, dg_final = loss_head(act, g_final.reshape(1, -1), tgt)

    big_red = [None] * depth
    small_parts = [None] * depth
    for i in reversed(range(depth)):
        dx, big, small_parts[i] = layer_bwd(dx, p[i, 0], layers[i], saved[i])
        big_red[i] = unpack_big_grads(*reduce_scatter_grads(big))

    small_list = [jnp.stack([small_parts[i][n].reshape(weights[n].shape[1:] if n not in ("conv_a", "conf_dw")
                                                       else small_parts[i][n].shape) for i in range(depth)])
                  for n in SMALL_NAMES]
    small_list += [dg_final.reshape(-1), loss_part[0, :1]]
    small_shapes = [a.shape for a in small_list]
    n_small = sum(math.prod(sh) for sh in small_shapes)
    small_rows = -(-n_small // (8 * D_MODEL)) * 8
    red = _unpack_rows(gather8(_pack_rows(small_list, small_rows), reduce=True), small_shapes)
    grads = dict(zip(SMALL_NAMES, red[:len(SMALL_NAMES)]))
    grads["g_final"] = red[-2]
    loss = red[-1].reshape(())
    for n in ("conv_a", "conf_dw"):
        grads[n] = lax.dynamic_slice_in_dim(grads[n], my_shard * cw, cw, axis=2)
    for name in BIG_NAMES:
        grads[name] = jnp.stack([big_red[i][name] for i in range(depth)])

    small_all = [n for n in order if n not in BIG_NAMES]
    sm_shapes = [weights[n].shape for n in small_all]
    n_sm = sum(math.prod(sh) for sh in sm_shapes)
    sm_rows = -(-n_sm // (8 * D_MODEL)) * 8
    packed = [_pack_rows([src[n] for n in small_all], sm_rows) for src in (weights, grads, m_in, v_in)]
    sm_out = [_unpack_rows(o, sm_shapes) for o in adamw(*packed)]
    delta, new_m, new_v = ({n: o[k] for k, n in enumerate(small_all)} for o in sm_out)
    for name in BIG_NAMES:
        delta[name], new_m[name], new_v[name] = adamw(weights[name], grads[name], m_in[name], v_in[name])

    return (loss, dx[None], *[grads[n] for n in order], *[delta[n] for n in order], *[new_m[n] for n in order],
            *[new_v[n] for n in order])
```

```python
import functools
import math

import jax
import jax.numpy as jnp
from jax import lax
from jax.experimental import pallas as pl
from jax.experimental.pallas import tpu as pltpu

F32 = jnp.float32
BF16 = jnp.bfloat16
EPS = 1e-6
D_MODEL = 1024
BW = 256
N_BR = 4
N_IN = 10 * BW
FFN_H = 2816
N_SH = 4
HEADS = 4
HEAD_D = 64
BLK = 128
DILATIONS = (1, 4, 16)
CONF_K = 31
CONVA_K = 3
NEG = -1e30
VMEM_LIMIT = 56 * 1024 * 1024
MESH = pl.DeviceIdType.MESH

ADAM_LR, ADAM_B1, ADAM_B2, ADAM_EPS, ADAM_WD, ADAM_STEP = 0.001, 0.9, 0.999, 1e-08, 0.01, 10

bs = pl.BlockSpec
ANY = pl.BlockSpec(memory_space=pl.ANY)


def _call(body, name, grid, in_specs, out_specs, out_shape, scratch=(), aliases=None, after=None):
    n_in = len(in_specs)
    kernel_body = body
    if after is not None:
        in_specs = list(in_specs) + [ANY]

        def kernel_body(*refs):
            return body(*refs[:n_in], *refs[n_in + 1:])

    call = pl.pallas_call(
        kernel_body, name=name, grid=grid, in_specs=in_specs, out_specs=out_specs, out_shape=out_shape,
        scratch_shapes=list(scratch), input_output_aliases=aliases or {},
        compiler_params=pltpu.CompilerParams(dimension_semantics=("arbitrary",) * len(grid),
                                             vmem_limit_bytes=VMEM_LIMIT))
    return call if after is None else (lambda *args: call(*args, after))


def _sds(shape, dtype):
    return jax.ShapeDtypeStruct(shape, dtype)


def _nn(a, b):
    return jnp.dot(a, b, preferred_element_type=F32)


def _nt(a, b):
    return lax.dot_general(a, b, (((1,), (1,)), ((), ())), preferred_element_type=F32)


def _tn(a, b):
    return lax.dot_general(a, b, (((0,), (0,)), ((), ())), preferred_element_type=F32)


def _sigmoid(x):
    return 1.0 / (1.0 + jnp.exp(-x))


def _rms_fwd(x, g):
    r = lax.rsqrt(jnp.mean(x * x, axis=-1, keepdims=True) + EPS)
    return x * r * g


def _rms_bwd(dh, x, g):
    r = lax.rsqrt(jnp.mean(x * x, axis=-1, keepdims=True) + EPS)
    xr = x * r
    dxr = dh * g
    dx = r * (dxr - xr * jnp.mean(dxr * xr, axis=-1, keepdims=True))
    return dx, dh * xr


def _ln_hat(x):
    mu = jnp.mean(x, axis=-1, keepdims=True)
    xc = x - mu
    r = lax.rsqrt(jnp.mean(xc * xc, axis=-1, keepdims=True) + EPS)
    return xc * r, r


def _ln_bwd(dy, xhat, r, g):
    dxh = dy * g
    return r * (dxh - jnp.mean(dxh, axis=-1, keepdims=True) - xhat * jnp.mean(dxh * xhat, axis=-1, keepdims=True))


def _colsum(v):
    return jnp.sum(v, axis=0, keepdims=True)


def _causal_conv(zext, w_ref, k_taps, halo):
    acc = zext[halo:] * w_ref[k_taps - 1:k_taps, :]
    for k in range(k_taps - 1):
        acc = acc + pltpu.roll(zext, k_taps - 1 - k, 0)[halo:] * w_ref[k:k + 1, :]
    return acc


def _anti_conv(dext, w_ref, k_taps, tm):
    n = dext.shape[0]
    acc = dext[:tm] * w_ref[k_taps - 1:k_taps, :]
    for s in range(1, k_taps):
        acc = acc + pltpu.roll(dext, n - s, 0)[:tm] * w_ref[k_taps - 1 - s:k_taps - s, :]
    return acc


def _conv_wgrad(dw_ref, dc, zext, k_taps, halo):
    dw_ref[k_taps - 1:k_taps, :] += _colsum(dc * zext[halo:])
    for k in range(k_taps - 1):
        dw_ref[k:k + 1, :] += _colsum(dc * pltpu.roll(zext, k_taps - 1 - k, 0)[halo:])


def _prev_blk(i, per):
    return jnp.maximum(i * per - 1, 0)


def _next_blk(i, per, last):
    return jnp.minimum((i + 1) * per, last)


def norm_in_proj(x, g, win, after=None):
    t = x.shape[0]
    tm = min(1024, t)
    ns = win.shape[2]

    def body(x_ref, g_ref, w_ref, h_ref, o_ref):
        h = _rms_fwd(x_ref[...], g_ref[...]).astype(BF16)
        h_ref[...] = h
        o_ref[...] = _nn(h, w_ref[...])

    return _call(
        body, "norm_in_proj", (t // tm, N_SH),
        [bs((tm, D_MODEL), lambda i, j: (i, 0)), bs((1, D_MODEL), lambda i, j: (0, 0)),
         bs((None, D_MODEL, ns), lambda i, j: (j, 0, 0))],
        [bs((tm, D_MODEL), lambda i, j: (i, 0)), bs((tm, ns), lambda i, j: (i, j))],
        [_sds((t, D_MODEL), BF16), _sds((t, N_SH * ns), F32)], after=after)(x, g, win)


def merge_fwd(h, ys, wg, wbr):
    t = h.shape[0]
    tm = min(1024, t)

    def body(h_ref, ya, yb, yc, yd, wg_ref, wb_ref, m_ref, g_ref, b_ref):
        hh = h_ref[...]
        acc = None
        for k, y_ref in enumerate((ya, yb, yc, yd)):
            g = _sigmoid(_nn(hh, wg_ref[k]))
            b = _nn(y_ref[...], wb_ref[k])
            g_ref[k] = g.astype(BF16)
            b_ref[k] = b.astype(BF16)
            acc = g * b if acc is None else acc + g * b
        m_ref[...] = acc.astype(BF16)

    ysp = bs((tm, BW), lambda i, j: (i, 0))
    return _call(
        body, "merge_fwd", (t // tm, N_SH),
        [bs((tm, D_MODEL), lambda i, j: (i, 0)), ysp, ysp, ysp, ysp,
         bs((N_BR, D_MODEL, BW), lambda i, j: (0, 0, j)), bs((None, N_BR, BW, BW), lambda i, j: (j, 0, 0, 0))],
        [bs((tm, BW), lambda i, j: (i, j)), bs((N_BR, tm, BW), lambda i, j: (0, i, j)),
         bs((N_BR, tm, BW), lambda i, j: (0, i, j))],
        [_sds((t, D_MODEL), BF16), _sds((N_BR, t, D_MODEL), BF16), _sds((N_BR, t, D_MODEL), BF16)])(
            h, *ys, wg, wbr)


def mm_residual(a, w, res, name, after=None):
    t, kk = a.shape
    tm, tn = min(1024, t), 512

    def body(a_ref, w_ref, r_ref, o_ref):
        o_ref[...] = r_ref[...] + _nn(a_ref[...], w_ref[...])

    return _call(
        body, name, (t // tm, D_MODEL // tn),
        [bs((tm, kk), lambda i, j: (i, 0)), bs((kk, tn), lambda i, j: (0, j)), bs((tm, tn), lambda i, j: (i, j))],
        bs((tm, tn), lambda i, j: (i, j)), _sds((t, D_MODEL), F32), after=after)(a, w, res)


def ffn_in(x, g, wfi):
    t = x.shape[0]
    tm = min(1024, t)
    ns = wfi.shape[2]

    def body(x_ref, g_ref, wg_ref, wu_ref, h_ref, f_ref, a_ref):
        h = _rms_fwd(x_ref[...], g_ref[...]).astype(BF16)
        h_ref[...] = h
        fg = _nn(h, wg_ref[...])
        fu = _nn(h, wu_ref[...])
        f_ref[0] = fg.astype(BF16)
        f_ref[1] = fu.astype(BF16)
        a_ref[...] = (fg * _sigmoid(fg) * fu).astype(BF16)

    return _call(
        body, "ffn_in", (t // tm, 2),
        [bs((tm, D_MODEL), lambda i, j: (i, 0)), bs((1, D_MODEL), lambda i, j: (0, 0)),
         bs((None, D_MODEL, ns), lambda i, j: (j, 0, 0)), bs((None, D_MODEL, ns), lambda i, j: (j + 2, 0, 0))],
        [bs((tm, D_MODEL), lambda i, j: (i, 0)), bs((2, tm, ns), lambda i, j: (0, i, j)),
         bs((tm, ns), lambda i, j: (i, j))],
        [_sds((t, D_MODEL), BF16), _sds((2, t, FFN_H), BF16), _sds((t, FFN_H), BF16)])(x, g, wfi, wfi)


def ple_fwd(x, g, wpg, p_i, wpp):
    t = x.shape[0]
    tm = min(1024, t)

    def body(x_ref, xt_ref, g_ref, wg_ref, p_ref, wp_ref, h_ref, gt_ref, pp_ref, o_ref):
        h = _rms_fwd(x_ref[...], g_ref[...]).astype(BF16)
        h_ref[...] = h
        gate = _sigmoid(_nn(h, wg_ref[...]))
        pp = _nn(p_ref[...].astype(BF16), wp_ref[...])
        gt_ref[...] = gate.astype(BF16)
        pp_ref[...] = pp.astype(BF16)
        o_ref[...] = xt_ref[...] + gate * pp

    tile = bs((tm, BW), lambda i, j: (i, j))
    return _call(
        body, "ple_fwd", (t // tm, N_SH),
        [bs((tm, D_MODEL), lambda i, j: (i, 0)), tile, bs((1, D_MODEL), lambda i, j: (0, 0)),
         bs((D_MODEL, BW), lambda i, j: (0, j)), bs((tm, BW), lambda i, j: (i, 0)),
         bs((None, BW, BW), lambda i, j: (j, 0, 0))],
        [bs((tm, D_MODEL), lambda i, j: (i, 0)), tile, tile, tile],
        [_sds((t, D_MODEL), BF16), _sds((t, D_MODEL), BF16), _sds((t, D_MODEL), BF16), _sds((t, D_MODEL), F32)])(
            x, x, g, wpg, p_i, wpp)


def loss_head(x, g, tgt):
    t = x.shape[0]
    tm = min(512, t)

    def body(x_ref, g_ref, t_ref, l_ref, dx_ref, dg_ref):
        @pl.when(pl.program_id(0) == 0)
        def _():
            l_ref[...] = jnp.zeros_like(l_ref)
            dg_ref[...] = jnp.zeros_like(dg_ref)

        xv, gv = x_ref[...], g_ref[...]
        err = _rms_fwd(xv, gv) - t_ref[...]
        part = 0.5 * jnp.sum(jnp.mean(err * err, axis=-1, keepdims=True), axis=0, keepdims=True)
        l_ref[...] += jnp.broadcast_to(part, l_ref.shape)
        dx, dgr = _rms_bwd(err * (1.0 / D_MODEL), xv, gv)
        dx_ref[...] = dx
        dg_ref[...] += _colsum(dgr)

    row = bs((tm, D_MODEL), lambda i: (i, 0))
    vec = bs((1, D_MODEL), lambda i: (0, 0))
    return _call(body, "loss_head", (t // tm,), [row, vec, row],
                 [bs((1, 128), lambda i: (0, 0)), row, vec],
                 [_sds((1, 128), F32), _sds((t, D_MODEL), F32), _sds((1, D_MODEL), F32)])(x, g, tgt)


def tn_matmul(name, a, b, grid, a_spec, b_spec, out_spec, out_shape, split=0, split_cols=0, into=None, after=None):
    last = len(grid) - 1

    def body(a_ref, b_ref, *rest):
        o_ref = rest[-1]

        @pl.when(pl.program_id(last) == 0)
        def _():
            o_ref[...] = jnp.zeros_like(o_ref)

        res = _tn(a_ref[...].astype(BF16), b_ref[...].astype(BF16))
        if split_cols:
            cols = res.shape[1] // split_cols
            for s in range(split_cols):
                o_ref[s] += res[:, s * cols:(s + 1) * cols]
        elif split:
            rows = res.shape[0] // split
            for s in range(split):
                o_ref[s] += res[s * rows:(s + 1) * rows]
        else:
            o_ref[...] += res

    if into is None:
        return _call(body, name, grid, [a_spec, b_spec], out_spec, out_shape, after=after)(a, b)
    return _call(body, name, grid, [a_spec, b_spec, ANY], out_spec, out_shape, aliases={2: 0}, after=after)(a, b, into)


def norm_bwd(name, sources, add, dx_in, x, g):
    t = x.shape[0]
    tm = min(512, t)
    offs, nk = [], 0
    for s in sources:
        offs.append(nk)
        nk += s[4]
    n_src = len(sources)
    has_add = add is not None

    def body(*refs):
        a_refs = refs[0:2 * n_src:2]
        w_refs = refs[1:2 * n_src:2]
        pos = 2 * n_src
        add_ref = refs[pos] if has_add else None
        pos += int(has_add)
        dxi_ref, x_ref, g_ref, dx_ref, dg_ref, acc_ref = refs[pos:pos + 6]
        i, k = pl.program_id(0), pl.program_id(1)

        @pl.when((i == 0) & (k == 0))
        def _():
            dg_ref[...] = jnp.zeros_like(dg_ref)

        @pl.when(k == 0)
        def _():
            acc_ref[...] = add_ref[...] if has_add else jnp.zeros_like(acc_ref)

        for si in range(n_src):
            @pl.when((k >= offs[si]) & (k < offs[si] + sources[si][4]))
            def _(si=si):
                acc_ref[...] += _nt(a_refs[si][...], w_refs[si][...])

        @pl.when(k == nk - 1)
        def _():
            dx, dgr = _rms_bwd(acc_ref[...], x_ref[...], g_ref[...])
            dx_ref[...] = dxi_ref[...] + dx
            dg_ref[...] += _colsum(dgr)

    in_specs, args = [], []
    for si, (a, a_spec, w, w_spec, steps) in enumerate(sources):
        loc = functools.partial(lambda k, o, n: jnp.clip(k - o, 0, n - 1), o=offs[si], n=steps)
        in_specs.append(a_spec(tm, loc))
        in_specs.append(w_spec(loc))
        args += [a, w]
    row = bs((tm, D_MODEL), lambda i, k: (i, 0))
    vec = bs((1, D_MODEL), lambda i, k: (0, 0))
    if has_add:
        in_specs.append(row)
        args.append(add)
    in_specs += [row, row, vec]
    args += [dx_in, x, g]
    return _call(body, name, (t // tm, nk), in_specs, [row, vec],
                 [_sds((t, D_MODEL), F32), _sds((1, D_MODEL), F32)],
                 scratch=[pltpu.VMEM((tm, D_MODEL), F32)])(*args)


def ple_bwd_pre(dx, gate, pp, after=None):
    t = dx.shape[0]
    tm = min(1024, t)

    def body(dx_ref, g_ref, p_ref, dpre_ref, dpp_ref):
        d = dx_ref[...]
        g = g_ref[...].astype(F32)
        dpre_ref[...] = (d * p_ref[...].astype(F32) * g * (1.0 - g)).astype(BF16)
        dpp_ref[...] = (d * g).astype(BF16)

    row = bs((tm, D_MODEL), lambda i: (i, 0))
    return _call(body, "ple_bwd_pre", (t // tm,), [row, row, row], [row, row],
                 [_sds((t, D_MODEL), BF16), _sds((t, D_MODEL), BF16)], after=after)(dx, gate, pp)


def ffn_bwd_act(dx, wfo, fgu, after=None):
    t = dx.shape[0]
    tm = min(1024, t)
    ns = FFN_H // 2

    def body(dx_ref, w_ref, f_ref, o_ref):
        dact = _nt(dx_ref[...].astype(BF16), w_ref[...])
        fg = f_ref[0].astype(F32)
        fu = f_ref[1].astype(F32)
        s = _sigmoid(fg)
        o_ref[0] = (dact * fu * (s * (1.0 + fg * (1.0 - s)))).astype(BF16)
        o_ref[1] = (dact * fg * s).astype(BF16)

    blk = bs((2, tm, ns), lambda i, j: (0, i, j))
    return _call(body, "ffn_bwd_act", (t // tm, 2),
                 [bs((tm, D_MODEL), lambda i, j: (i, 0)), bs((ns, D_MODEL), lambda i, j: (j, 0)), blk],
                 blk, _sds((2, t, FFN_H), BF16), after=after)(dx, wfo, fgu)


def merge_bwd_pre(dx, wout, gates, ybr):
    t = dx.shape[0]
    tm = min(1024, t)

    def body(dx_ref, w_ref, g_ref, b_ref, dpre_ref, dyb_ref):
        dm = _nt(dx_ref[...].astype(BF16), w_ref[...])
        for k in range(N_BR):
            g = g_ref[k].astype(F32)
            dpre_ref[k] = (dm * b_ref[k].astype(F32) * g * (1.0 - g)).astype(BF16)
            dyb_ref[k] = (dm * g).astype(BF16)

    blk = bs((N_BR, tm, BW), lambda i, j: (0, i, j))
    return _call(body, "merge_bwd_pre", (t // tm, N_SH),
                 [bs((tm, D_MODEL), lambda i, j: (i, 0)), bs((BW, D_MODEL), lambda i, j: (j, 0)), blk, blk],
                 [blk, blk], [_sds((N_BR, t, D_MODEL), BF16), _sds((N_BR, t, D_MODEL), BF16)])(dx, wout, gates, ybr)


def branch_out_bwd(dyb, wbr):
    t = dyb.shape[1]
    tm = min(1024, t)

    def body(d_ref, w_ref, o_ref):
        acc = None
        for s in range(N_SH):
            part = _nt(d_ref[:, s * BW:(s + 1) * BW], w_ref[s])
            acc = part if acc is None else acc + part
        o_ref[...] = acc

    return _call(body, "branch_out_bwd", (t // tm, N_BR),
                 [bs((None, tm, D_MODEL), lambda i, k: (k, i, 0)), bs((N_SH, None, BW, BW), lambda i, k: (0, k, 0, 0))],
                 bs((None, tm, BW), lambda i, k: (k, i, 0)), _sds((N_BR, t, BW), F32))(dyb, wbr)


def conva_fwd(proj, wa):
    t = proj.shape[0]
    tm, halo = min(512, t), 8
    per = tm // halo

    def body(b_ref, c_ref, x_ref, ch_ref, xh_ref, w_ref, y_ref):
        zh = jnp.where(pl.program_id(0) > 0, ch_ref[...] * xh_ref[...], 0.0)
        zext = jnp.concatenate([zh, c_ref[...] * x_ref[...]], axis=0)
        y_ref[...] = (b_ref[...] * _causal_conv(zext, w_ref, CONVA_K, halo)).astype(BF16)

    col = lambda c: bs((tm, BW), lambda i: (i, c))
    hal = lambda c: bs((halo, BW), lambda i: (_prev_blk(i, per), c))
    return _call(body, "conva_fwd", (t // tm,),
                 [col(0), col(1), col(2), hal(1), hal(2), bs((CONVA_K, BW), lambda i: (0, 0))],
                 bs((tm, BW), lambda i: (i, 0)), _sds((t, BW), BF16))(proj, proj, proj, proj, proj, wa)


def conva_bwd(proj, dys, wa):
    t = proj.shape[0]
    tm, halo = min(512, t), 8
    per = tm // halo
    last = t // halo - 1
    nt = t // tm

    def body(b_ref, c_ref, x_ref, ch_ref, xh_ref, bn_ref, dy_ref, dyn_ref, w_ref, db_ref, dc_ref, dxx_ref, dw_ref):
        i = pl.program_id(0)

        @pl.when(i == 0)
        def _():
            dw_ref[...] = jnp.zeros_like(dw_ref)

        zh = jnp.where(i > 0, ch_ref[...] * xh_ref[...], 0.0)
        cv, xv = c_ref[...], x_ref[...]
        zext = jnp.concatenate([zh, cv * xv], axis=0)
        dy = dy_ref[...]
        dconv = dy * b_ref[...]
        dcn = jnp.where(i < nt - 1, dyn_ref[...] * bn_ref[...], 0.0)
        dz = _anti_conv(jnp.concatenate([dconv, dcn], axis=0), w_ref, CONVA_K, tm)
        db_ref[...] = (dy * _causal_conv(zext, w_ref, CONVA_K, halo)).astype(BF16)
        dc_ref[...] = (dz * xv).astype(BF16)
        dxx_ref[...] = (dz * cv).astype(BF16)
        _conv_wgrad(dw_ref, dconv, zext, CONVA_K, halo)

    col = lambda c: bs((tm, BW), lambda i: (i, c))
    hal = lambda c: bs((halo, BW), lambda i: (_prev_blk(i, per), c))
    nxt = bs((halo, BW), lambda i: (_next_blk(i, per, last), 0))
    wsp = bs((CONVA_K, BW), lambda i: (0, 0))
    outs = _call(body, "conva_bwd", (t // tm,),
                 [col(0), col(1), col(2), hal(1), hal(2), nxt,
                  bs((None, tm, BW), lambda i: (0, i, 0)), bs((None, halo, BW), lambda i: (0, _next_blk(i, per, last), 0)), wsp],
                 [bs((tm, BW), lambda i: (i, 0))] * 3 + [wsp],
                 [_sds((t, BW), BF16)] * 3 + [_sds((CONVA_K, BW), F32)])(proj, proj, proj, proj, proj, proj, dys, dys, wa)
    return outs[:3], outs[3]


def _head_masks():
    lane = lax.broadcasted_iota(jnp.int32, (1, BW), 1)
    return [(lane >= h * HEAD_D) & (lane < (h + 1) * HEAD_D) for h in range(HEADS)]


def _band_masks():
    qi = lax.broadcasted_iota(jnp.int32, (BLK, BLK), 0)
    ki = lax.broadcasted_iota(jnp.int32, (BLK, BLK), 1)
    return ki >= qi, ki <= qi


def qkv_cast(proj):
    t = proj.shape[0]
    tm = min(1024, t)

    def body(p_ref, o_ref):
        o_ref[...] = p_ref[...].astype(BF16)

    return _call(body, "qkv_cast", (t // tm, 3), [bs((tm, BW), lambda i, j: (i, j + 3))],
                 bs((tm, BW), lambda i, j: (i, j)), _sds((t, 3 * BW), BF16))(proj)


def attn_fwd_group(qkv, d):
    t = qkv.shape[0]
    rows = t // d
    qb = min(512, rows)
    nb = qb // BLK
    scale = HEAD_D ** -0.5
    pv = qkv.reshape(rows, d * 3 * BW)

    def body(q_ref, k_ref, v_ref, kh_ref, vh_ref, o_ref):
        n = pl.program_id(1)
        hm = _head_masks()
        m_prev, m_cur = _band_masks()
        for b in range(nb):
            rs = slice(b * BLK, (b + 1) * BLK)
            q = q_ref[rs, :]
            if b == 0:
                kp, vp = kh_ref[...], vh_ref[...]
                mp = m_prev & (n > 0)
            else:
                ps = slice((b - 1) * BLK, b * BLK)
                kp, vp = k_ref[ps, :], v_ref[ps, :]
                mp = m_prev
            qs = jnp.concatenate([jnp.where(hm[h], q, 0.0).astype(BF16) for h in range(HEADS)], axis=0)
            kcat = jnp.concatenate([kp, k_ref[rs, :]], axis=0)
            vcat = jnp.concatenate([vp, v_ref[rs, :]], axis=0)
            band = jnp.concatenate([mp, m_cur], axis=1)
            s = jnp.where(jnp.concatenate([band] * HEADS, axis=0), _nt(qs, kcat) * scale, NEG)
            m = jnp.max(s, axis=-1, keepdims=True)
            e = jnp.exp(s - m)
            l = jnp.sum(e, axis=-1, keepdims=True)
            of = _nn(e.astype(BF16), vcat) / l
            lse = m + jnp.log(l)
            o_acc = jnp.zeros((BLK, BW), F32)
            l_acc = jnp.zeros((BLK, BW), F32)
            for h in range(HEADS):
                hs = slice(h * BLK, (h + 1) * BLK)
                o_acc = jnp.where(hm[h], of[hs, :], o_acc)
                l_acc = jnp.where(hm[h], lse[hs, :], l_acc)
            o_ref[rs, :BW] = o_acc
            o_ref[rs, BW:] = l_acc

    per = qb // BLK
    main = lambda c: bs((qb, BW), lambda r, n: (n, r * 3 + c))
    hal = lambda c: bs((BLK, BW), lambda r, n: (_prev_blk(n, per), r * 3 + c))
    ol = _call(body, f"attn_fwd_d{d}", (d, rows // qb), [main(0), main(1), main(2), hal(1), hal(2)],
               bs((qb, 2 * BW), lambda r, n: (n, r)), _sds((rows, d * 2 * BW), F32))(pv, pv, pv, pv, pv)
    return ol.reshape(t, 2 * BW)


def attn_merge(ols):
    t = ols[0].shape[0]
    tm = min(1024, t)

    def lse3(a, b, c):
        m = jnp.maximum(jnp.maximum(a, b), c)
        return m + jnp.log(jnp.exp(a - m) + jnp.exp(b - m) + jnp.exp(c - m))

    def body(g0, g1, g2, y_ref, o_ref, l_ref):
        gs = [g0[...], g1[...], g2[...]]
        ls = [g[:, BW:] for g in gs]
        tot = lse3(*ls)
        o = (jnp.exp(ls[0] - tot) * gs[0][:, :BW] + jnp.exp(ls[1] - tot) * gs[1][:, :BW]
             + jnp.exp(ls[2] - tot) * gs[2][:, :BW])
        y_ref[...] = o.astype(BF16)
        o_ref[...] = o
        l_ref[...] = tot

    n = bs((tm, BW), lambda i: (i, 0))
    w = bs((tm, 2 * BW), lambda i: (i, 0))
    return _call(body, "attn_merge", (t // tm,), [w] * 3, [n, n, n],
                 [_sds((t, BW), BF16), _sds((t, BW), F32), _sds((t, BW), F32)])(*ols)


def attn_delta(dy, o, lse):
    t = o.shape[0]
    tm = min(1024, t)

    def body(d_ref, o_ref, l_ref, out_ref):
        hm = _head_masks()
        prod = d_ref[...] * o_ref[...]
        delta = jnp.zeros_like(prod)
        for h in range(HEADS):
            delta = jnp.where(hm[h], jnp.sum(jnp.where(hm[h], prod, 0.0), axis=-1, keepdims=True), delta)
        out_ref[:, :BW] = l_ref[...]
        out_ref[:, BW:] = delta

    n = bs((tm, BW), lambda i: (i, 0))
    return _call(body, "attn_delta", (t // tm,), [n, n, n], bs((tm, 2 * BW), lambda i: (i, 0)),
                 _sds((t, 2 * BW), F32))(dy, o, lse)


def attn_bwd_group(qkv, dy, ld, acc, d):
    t = qkv.shape[0]
    rows = t // d
    qb = min(512, rows)
    nb = qb // BLK
    nsteps = rows // qb
    scale = HEAD_D ** -0.5
    pv = qkv.reshape(rows, d * 3 * BW)
    dov = dy.reshape(rows, d * BW)
    ldv = ld.reshape(rows, d * 2 * BW)
    has_acc = acc is not None

    def body(*refs):
        (q_ref, qn_ref, k_ref, kh_ref, v_ref, vh_ref, do_ref, don_ref, ld_ref, ldn_ref) = refs[:10]
        a_ref = refs[10] if has_acc else None
        o_ref = refs[-1]
        n = pl.program_id(1)
        hm = _head_masks()
        m_prev, m_cur = _band_masks()
        has_prev, has_next = n > 0, n < nsteps - 1
        dq = [None] * nb
        dk = [jnp.zeros((BLK, BW), F32) for _ in range(nb)]
        dvv = [jnp.zeros((BLK, BW), F32) for _ in range(nb)]
        for qi in range(nb + 1):
            rs = slice(qi * BLK, (qi + 1) * BLK)
            ps = slice((qi - 1) * BLK, qi * BLK)
            if qi < nb:
                q, do, ldq = q_ref[rs, :], do_ref[rs, :], ld_ref[rs, :]
            else:
                q, do, ldq = qn_ref[...], don_ref[...], ldn_ref[...]
            kp, vp = (kh_ref[...], vh_ref[...]) if qi == 0 else (k_ref[ps, :], v_ref[ps, :])
            kc, vc = (k_ref[rs, :], v_ref[rs, :]) if qi < nb else (kp, vp)
            mp = m_prev & has_prev if qi == 0 else (m_prev & has_next if qi == nb else m_prev)
            mc = m_cur if qi < nb else jnp.zeros_like(m_cur)
            band = jnp.concatenate([jnp.concatenate([mp, mc], axis=1)] * HEADS, axis=0)
            qs = jnp.concatenate([jnp.where(hm[h], q, 0.0).astype(BF16) for h in range(HEADS)], axis=0)
            dos = jnp.concatenate([jnp.where(hm[h], do, 0.0).astype(BF16) for h in range(HEADS)], axis=0)
            kcat = jnp.concatenate([kp, kc], axis=0)
            vcat = jnp.concatenate([vp, vc], axis=0)
            col = lambda v, h: jnp.broadcast_to(jnp.max(jnp.where(hm[h], v, NEG), axis=-1, keepdims=True), (BLK, 2 * BLK))
            lcols = jnp.concatenate([col(ldq[:, :BW], h) for h in range(HEADS)], axis=0)
            dcols = jnp.concatenate([col(ldq[:, BW:], h) for h in range(HEADS)], axis=0)
            p = jnp.where(band, jnp.exp(_nt(qs, kcat) * scale - lcols), 0.0)
            ds = (p * (_nt(dos, vcat) - dcols) * scale).astype(BF16)
            if qi < nb:
                dqf = _nn(ds, kcat)
                acc_q = jnp.zeros((BLK, BW), F32)
                for h in range(HEADS):
                    acc_q = jnp.where(hm[h], dqf[h * BLK:(h + 1) * BLK, :], acc_q)
                dq[qi] = acc_q
            dkc = _tn(ds, qs)
            dvc = _tn(p.astype(BF16), dos)
            if qi >= 1:
                dk[qi - 1] = dk[qi - 1] + dkc[:BLK]
                dvv[qi - 1] = dvv[qi - 1] + dvc[:BLK]
            if qi < nb:
                dk[qi] = dk[qi] + dkc[BLK:]
                dvv[qi] = dvv[qi] + dvc[BLK:]
        for b in range(nb):
            rs = slice(b * BLK, (b + 1) * BLK)
            for c, val in enumerate((dq[b], dk[b], dvv[b])):
                cs = slice(c * BW, (c + 1) * BW)
                o_ref[rs, cs] = a_ref[rs, cs] + val if has_acc else val

    per = qb // BLK
    last = rows // BLK - 1
    main = lambda c: bs((qb, BW), lambda r, n: (n, r * 3 + c))
    prv = lambda c: bs((BLK, BW), lambda r, n: (_prev_blk(n, per), r * 3 + c))
    nxt = lambda c: bs((BLK, BW), lambda r, n: (_next_blk(n, per, last), r * 3 + c))
    accs = bs((qb, 3 * BW), lambda r, n: (n, r))
    in_specs = [main(0), nxt(0), main(1), prv(1), main(2), prv(2),
                bs((qb, BW), lambda r, n: (n, r)), bs((BLK, BW), lambda r, n: (_next_blk(n, per, last), r)),
                bs((qb, 2 * BW), lambda r, n: (n, r)), bs((BLK, 2 * BW), lambda r, n: (_next_blk(n, per, last), r))]
    args = [pv, pv, pv, pv, pv, pv, dov, dov, ldv, ldv]
    if has_acc:
        in_specs.append(accs)
        args.append(acc.reshape(rows, d * 3 * BW))
    out = _call(body, f"attn_bwd_d{d}", (d, nsteps), in_specs, accs, _sds((rows, d * 3 * BW), F32),
                aliases={10: 0} if has_acc else None)(*args)
    return out.reshape(t, 3 * BW)


def _group_masks():
    lane = lax.broadcasted_iota(jnp.int32, (1, BW), 1)
    return [(lane >= g * HEAD_D) & (lane < (g + 1) * HEAD_D) for g in range(4)]


def sgu_fwd(proj, ln_g, ln_b, w_tril, b_full):
    t = proj.shape[0]
    tm = min(512, t)

    def body(u_ref, v_ref, g_ref, b_ref, w_ref, bf_ref, y_ref):
        gm = _group_masks()
        xhat, _ = _ln_hat(v_ref[...])
        vb = (xhat * g_ref[...] + b_ref[...]).astype(BF16)
        for c in range(tm // BLK):
            rs = slice(c * BLK, (c + 1) * BLK)
            vc = vb[rs, :]
            mixed = bf_ref[...]
            for g in range(4):
                mixed = mixed + jnp.where(gm[g], _nn(w_ref[g], vc), 0.0)
            y_ref[rs, :] = (u_ref[rs, :] * mixed).astype(BF16)

    vec = bs((1, BW), lambda i: (0, 0))
    return _call(body, "sgu_fwd", (t // tm,),
                 [bs((tm, BW), lambda i: (i, 6)), bs((tm, BW), lambda i: (i, 7)), vec, vec,
                  bs((4, BLK, BLK), lambda i: (0, 0, 0)), bs((BLK, BW), lambda i: (0, 0))],
                 bs((tm, BW), lambda i: (i, 0)), _sds((t, BW), BF16))(proj, proj, ln_g, ln_b, w_tril, b_full)


def sgu_bwd(proj, dys, ln_g, ln_b, w_tril, b_full):
    t = proj.shape[0]
    tm = min(512, t)

    def body(u_ref, v_ref, dy_ref, g_ref, b_ref, w_ref, bf_ref, du_ref, dv_ref, dw_ref, dbf_ref, dg_ref, db_ref, dvl_ref):
        @pl.when(pl.program_id(0) == 0)
        def _():
            dw_ref[...] = jnp.zeros_like(dw_ref)
            dbf_ref[...] = jnp.zeros_like(dbf_ref)
            dg_ref[...] = jnp.zeros_like(dg_ref)
            db_ref[...] = jnp.zeros_like(db_ref)

        gm = _group_masks()
        xhat, r = _ln_hat(v_ref[...])
        gv = g_ref[...]
        vb = (xhat * gv + b_ref[...]).astype(BF16)
        for c in range(tm // BLK):
            rs = slice(c * BLK, (c + 1) * BLK)
            vc = vb[rs, :]
            dy = dy_ref[rs, :]
            mixed = bf_ref[...]
            for g in range(4):
                mixed = mixed + jnp.where(gm[g], _nn(w_ref[g], vc), 0.0)
            du_ref[rs, :] = (dy * mixed).astype(BF16)
            dm = dy * u_ref[rs, :]
            dbf_ref[...] += dm
            dvl = jnp.zeros((BLK, BW), F32)
            for g in range(4):
                dmg = jnp.where(gm[g], dm, 0.0).astype(BF16)
                dw_ref[g] += _nt(dmg, vc)
                dvl = dvl + _tn(w_ref[g], dmg)
            dvl_ref[rs, :] = dvl
        dvl = dvl_ref[...]
        dv_ref[...] = _ln_bwd(dvl, xhat, r, gv).astype(BF16)
        dg_ref[...] += _colsum(dvl * xhat)
        db_ref[...] += _colsum(dvl)

    vec = bs((1, BW), lambda i: (0, 0))
    row = bs((tm, BW), lambda i: (i, 0))
    wsp = bs((4, BLK, BLK), lambda i: (0, 0, 0))
    bfs = bs((BLK, BW), lambda i: (0, 0))
    return _call(body, "sgu_bwd", (t // tm,),
                 [bs((tm, BW), lambda i: (i, 6)), bs((tm, BW), lambda i: (i, 7)), bs((None, tm, BW), lambda i: (2, i, 0)),
                  vec, vec, wsp, bfs],
                 [row, row, wsp, bfs, vec, vec],
                 [_sds((t, BW), BF16), _sds((t, BW), BF16), _sds((4, BLK, BLK), F32), _sds((BLK, BW), F32),
                  _sds((1, BW), F32), _sds((1, BW), F32)],
                 scratch=[pltpu.VMEM((tm, BW), F32)])(proj, proj, dys, ln_g, ln_b, w_tril, b_full)


CONF_HALO = 32


def conf_fwd(proj, dw, ln_g, ln_b):
    t = proj.shape[0]
    tm, halo = min(512, t), CONF_HALO
    per = tm // halo

    def body(v_ref, gt_ref, vh_ref, gh_ref, w_ref, g_ref, b_ref, y_ref, z_ref):
        yh = jnp.where(pl.program_id(0) > 0, vh_ref[...] * _sigmoid(gh_ref[...]), 0.0)
        yext = jnp.concatenate([yh, v_ref[...] * _sigmoid(gt_ref[...])], axis=0)
        z = _causal_conv(yext, w_ref, CONF_K, halo)
        z_ref[...] = z
        xhat, _ = _ln_hat(z)
        ln = xhat * g_ref[...] + b_ref[...]
        y_ref[...] = (ln * _sigmoid(ln)).astype(BF16)

    vec = bs((1, BW), lambda i: (0, 0))
    col = lambda c: bs((tm, BW), lambda i: (i, c))
    hal = lambda c: bs((halo, BW), lambda i: (_prev_blk(i, per), c))
    row = bs((tm, BW), lambda i: (i, 0))
    return _call(body, "conf_fwd", (t // tm,),
                 [col(8), col(9), hal(8), hal(9), bs((CONF_K, BW), lambda i: (0, 0)), vec, vec],
                 [row, row], [_sds((t, BW), BF16), _sds((t, BW), F32)])(proj, proj, proj, proj, dw, ln_g, ln_b)


def conf_bwd_ln(z, dys, ln_g, ln_b):
    t = z.shape[0]
    tm = min(1024, t)

    def body(z_ref, dy_ref, g_ref, b_ref, dz_ref, dg_ref, db_ref):
        @pl.when(pl.program_id(0) == 0)
        def _():
            dg_ref[...] = jnp.zeros_like(dg_ref)
            db_ref[...] = jnp.zeros_like(db_ref)

        gv = g_ref[...]
        xhat, r = _ln_hat(z_ref[...])
        ln = xhat * gv + b_ref[...]
        s = _sigmoid(ln)
        dln = dy_ref[...] * (s * (1.0 + ln * (1.0 - s)))
        dz_ref[...] = _ln_bwd(dln, xhat, r, gv)
        dg_ref[...] += _colsum(dln * xhat)
        db_ref[...] += _colsum(dln)

    vec = bs((1, BW), lambda i: (0, 0))
    row = bs((tm, BW), lambda i: (i, 0))
    return _call(body, "conf_bwd_ln", (t // tm,), [row, bs((None, tm, BW), lambda i: (3, i, 0)), vec, vec],
                 [row, vec, vec], [_sds((t, BW), F32), _sds((1, BW), F32), _sds((1, BW), F32)])(z, dys, ln_g, ln_b)


def conf_bwd_conv(proj, dz, dw):
    t = proj.shape[0]
    tm, halo = min(512, t), CONF_HALO
    per = tm // halo
    last = t // halo - 1
    nt = t // tm

    def body(v_ref, gt_ref, vh_ref, gh_ref, dz_ref, dzn_ref, w_ref, dv_ref, dg_ref, dw_ref):
        i = pl.program_id(0)

        @pl.when(i == 0)
        def _():
            dw_ref[...] = jnp.zeros_like(dw_ref)

        val = v_ref[...]
        sg = _sigmoid(gt_ref[...])
        yh = jnp.where(i > 0, vh_ref[...] * _sigmoid(gh_ref[...]), 0.0)
        yext = jnp.concatenate([yh, val * sg], axis=0)
        dz = dz_ref[...]
        dzn = jnp.where(i < nt - 1, dzn_ref[...], 0.0)
        dy0 = _anti_conv(jnp.concatenate([dz, dzn], axis=0), w_ref, CONF_K, tm)
        dv_ref[...] = (dy0 * sg).astype(BF16)
        dg_ref[...] = (dy0 * val * sg * (1.0 - sg)).astype(BF16)
        _conv_wgrad(dw_ref, dz, yext, CONF_K, halo)

    col = lambda c: bs((tm, BW), lambda i: (i, c))
    hal = lambda c: bs((halo, BW), lambda i: (_prev_blk(i, per), c))
    row = bs((tm, BW), lambda i: (i, 0))
    wsp = bs((CONF_K, BW), lambda i: (0, 0))
    return _call(body, "conf_bwd_conv", (t // tm,),
                 [col(8), col(9), hal(8), hal(9), row, bs((halo, BW), lambda i: (_next_blk(i, per, last), 0)), wsp],
                 [row, row, wsp], [_sds((t, BW), BF16), _sds((t, BW), BF16), _sds((CONF_K, BW), F32)])(
                     proj, proj, proj, proj, dz, dz, dw)


def _place():
    return lax.axis_index("x"), lax.axis_index("y"), lax.axis_index("c")


def _comm_call(body, name, n_in, out_shape, scratch, aliases=None):
    return pl.pallas_call(body, name=name, in_specs=[ANY] * n_in, out_specs=[ANY] * len(out_shape), out_shape=out_shape,
                          scratch_shapes=scratch, input_output_aliases=aliases or {},
                          compiler_params=pltpu.CompilerParams(has_side_effects=True, vmem_limit_bytes=VMEM_LIMIT))


HBM_SPEC = pl.BlockSpec(memory_space=pltpu.HBM)
SEM_SPEC = pl.BlockSpec(memory_space=pltpu.SEMAPHORE)
EFFECT = pltpu.SideEffectType.DATAFLOW_SIDE_EFFECTING


class SplitExchange:
    def __init__(self, name, bufs, plan, n_copies):
        self.name, self.bufs, self.plan, self.n = name, list(bufs), plan, n_copies

    def start(self, after):
        nb, n, plan = len(self.bufs), self.n, self.plan

        def body(*refs):
            send, recv, token = refs[nb + 1], refs[nb + 2], refs[-1]
            for k, (src, dst, _, dev) in enumerate(plan(refs[:nb])):
                pltpu.make_async_remote_copy(src_ref=src, dst_ref=dst, send_sem=send.at[k], recv_sem=recv.at[k],
                                             device_id=dev, device_id_type=MESH).start()
            token[...] = jnp.zeros_like(token)

        outs = pl.pallas_call(
            body, name=self.name + "_start",
            out_shape=(pltpu.SemaphoreType.DMA((n,)), pltpu.SemaphoreType.DMA((n,)),
                       *[pltpu.HBM(b.shape, b.dtype) for b in self.bufs], _sds((8, 128), F32)),
            in_specs=[HBM_SPEC] * nb + [ANY],
            out_specs=(SEM_SPEC, SEM_SPEC, *[HBM_SPEC] * nb, pl.BlockSpec(memory_space=pltpu.VMEM)),
            input_output_aliases={i: 2 + i for i in range(nb)},
            compiler_params=pltpu.CompilerParams(has_side_effects=EFFECT))(
                *[pltpu.with_memory_space_constraint(b, pltpu.HBM) for b in self.bufs], after)
        self.send, self.recv, self.bufs = outs[0], outs[1], list(outs[2:2 + nb])
        return outs[-1]

    def wait(self, after):
        nb, plan = len(self.bufs), self.plan

        def body(*refs):
            send, recv = refs[nb], refs[nb + 1]
            for k, (src, _, land, dev) in enumerate(plan(refs[:nb])):
                cp = pltpu.make_async_remote_copy(src_ref=src, dst_ref=land, send_sem=send.at[k], recv_sem=recv.at[k],
                                                  device_id=dev, device_id_type=MESH)
                cp.wait_send()
                cp.wait_recv()

        outs = pl.pallas_call(
            body, name=self.name + "_wait", out_shape=tuple(pltpu.HBM(b.shape, b.dtype) for b in self.bufs),
            in_specs=[HBM_SPEC] * nb + [SEM_SPEC, SEM_SPEC, ANY], out_specs=[HBM_SPEC] * nb,
            input_output_aliases={i: i for i in range(nb)},
            compiler_params=pltpu.CompilerParams(has_side_effects=EFFECT))(*self.bufs, self.send, self.recv, after)
        return list(outs)


def _chips_of(x, y):
    return [(1 - x, y), (x, 1 - y), (1 - x, 1 - y)]


def allgather_ici_plan(shapes):
    def plan(refs):
        x, y, c = _place()
        out = []
        for a, ref in enumerate(refs):
            hl = shapes[a][1] // 2
            half = pl.ds(c * hl, hl)
            for cx, cy in _chips_of(x, y):
                mine = ref.at[2 * x + y, half]
                out.append((mine, mine, ref.at[2 * cx + cy, half], (cx, cy, c)))
        return out
    return plan


def allgather_d2d_plan(shapes):
    def plan(refs):
        x, y, c = _place()
        out = []
        for a, ref in enumerate(refs):
            hl = shapes[a][1] // 2
            for cx, cy in _chips_of(x, y):
                got = ref.at[2 * cx + cy, pl.ds(c * hl, hl)]
                out.append((got, got, ref.at[2 * cx + cy, pl.ds((1 - c) * hl, hl)], (x, y, 1 - c)))
        return out
    return plan


def gather8(v, reduce):
    rows, cols = v.shape

    def body(v_ref, o_ref, land_ref, send, recv, lsem):
        x, y, c = _place()
        me = 4 * x + 2 * y + c
        land = land_ref if reduce else o_ref
        mine = pltpu.make_async_copy(v_ref, land.at[me], lsem)
        mine.start()
        sent = []
        for j in range(1, 8):
            fx, fy, fc = (j >> 2) & 1, (j >> 1) & 1, j & 1
            tgt = (1 - x if fx else x, 1 - y if fy else y, 1 - c if fc else c)
            cp = pltpu.make_async_remote_copy(src_ref=v_ref, dst_ref=land.at[me], send_sem=send.at[j - 1],
                                              recv_sem=recv.at[j - 1], device_id=tgt, device_id_type=MESH)
            cp.start()
            sent.append(cp)
        for j in range(1, 8):
            fx, fy, fc = (j >> 2) & 1, (j >> 1) & 1, j & 1
            peer = 4 * (1 - x if fx else x) + 2 * (1 - y if fy else y) + (1 - c if fc else c)
            pltpu.make_async_remote_copy(src_ref=v_ref, dst_ref=land.at[peer], send_sem=send.at[j - 1],
                                         recv_sem=recv.at[j - 1], device_id=(x, y, c), device_id_type=MESH).wait_recv()
        for cp in sent:
            cp.wait_send()
        mine.wait()
        if reduce:
            acc = land_ref[0]
            for k in range(1, 8):
                acc = acc + land_ref[k]
            o_ref[...] = acc

    vm = pl.BlockSpec(memory_space=pltpu.VMEM)
    out_shape = _sds((rows, cols), F32) if reduce else _sds((8, rows, cols), F32)
    land_shape = (8, rows, cols) if reduce else (8, 128)
    return pl.pallas_call(
        body, name="allreduce8" if reduce else "allgather8", in_specs=[vm], out_specs=vm, out_shape=out_shape,
        scratch_shapes=[pltpu.VMEM(land_shape, F32), pltpu.SemaphoreType.DMA((7,)), pltpu.SemaphoreType.DMA((7,)),
                        pltpu.SemaphoreType.DMA],
        compiler_params=pltpu.CompilerParams(has_side_effects=True, vmem_limit_bytes=VMEM_LIMIT))(v)


def allgather_weights(bufs):
    n = len(bufs)

    def body(*refs):
        ins, outs = refs[:n], refs[n:2 * n]
        send, recv = refs[2 * n:]
        x, y, c = _place()
        s_me = 2 * x + y
        chips = [(1 - x, y), (x, 1 - y), (1 - x, 1 - y)]
        sibling = (x, y, 1 - c)
        started = []
        for a in range(n):
            hl = bufs[a].shape[1] // 2
            half = pl.ds(c * hl, hl)
            for j, chip in enumerate(chips):
                cp = pltpu.make_async_remote_copy(src_ref=ins[a].at[s_me, half], dst_ref=outs[a].at[s_me, half],
                                                  send_sem=send.at[6 * a + j], recv_sem=recv.at[6 * a + j],
                                                  device_id=(chip[0], chip[1], c), device_id_type=MESH)
                cp.start()
                started.append(cp)
        for a in range(n):
            hl = bufs[a].shape[1] // 2
            half = pl.ds(c * hl, hl)
            for j, chip in enumerate(chips):
                s_j = 2 * chip[0] + chip[1]
                landed = outs[a].at[s_j, half]
                pltpu.make_async_remote_copy(src_ref=landed, dst_ref=landed, send_sem=send.at[6 * a + j],
                                             recv_sem=recv.at[6 * a + j], device_id=sibling, device_id_type=MESH).wait_recv()
                fw = pltpu.make_async_remote_copy(src_ref=landed, dst_ref=landed, send_sem=send.at[6 * a + 3 + j],
                                                  recv_sem=recv.at[6 * a + 3 + j], device_id=sibling, device_id_type=MESH)
                fw.start()
                started.append(fw)
        for a in range(n):
            hl = bufs[a].shape[1] // 2
            other = pl.ds((1 - c) * hl, hl)
            for j, chip in enumerate(chips):
                s_j = 2 * chip[0] + chip[1]
                theirs = outs[a].at[s_j, other]
                pltpu.make_async_remote_copy(src_ref=theirs, dst_ref=theirs, send_sem=send.at[6 * a + 3 + j],
                                             recv_sem=recv.at[6 * a + 3 + j], device_id=sibling, device_id_type=MESH).wait_recv()
        for cp in started:
            cp.wait_send()

    out_shape = [_sds(b.shape, b.dtype) for b in bufs]
    scratch = [pltpu.SemaphoreType.DMA((6 * n,)), pltpu.SemaphoreType.DMA((6 * n,))]
    return _comm_call(body, "allgather_weights", n, out_shape, scratch, aliases={a: a for a in range(n)})(*bufs)


def _row_tile(rows, cols):
    best = 16
    for t in range(16, rows + 1, 16):
        if rows % t == 0 and t * cols * 4 <= 2 * 1024 * 1024:
            best = t
    return best


def _rs_add_sibling(scal, g, ra, hr):
    cols = g.shape[2]
    tr = _row_tile(hr, cols)
    nr = hr // tr

    def body(s_ref, g_ref, r_ref, p32_ref, p16_ref):
        v = g_ref[...] + r_ref[...]
        p32_ref[...] = v
        p16_ref[...] = v.astype(BF16)

    blk = lambda f: bs((None, tr, cols), f)
    own = blk(lambda s, i, sr: (s, i, 0))
    spec = pltpu.PrefetchScalarGridSpec(num_scalar_prefetch=1, grid=(N_SH, nr),
                                        in_specs=[blk(lambda s, i, sr: (s, sr[1] * nr + i, 0)), own],
                                        out_specs=[own, own])
    return pl.pallas_call(body, name="rs_add_sibling", grid_spec=spec,
                          out_shape=[_sds((N_SH, hr, cols), F32), _sds((N_SH, hr, cols), BF16)],
                          compiler_params=pltpu.CompilerParams(dimension_semantics=("arbitrary",) * 2,
                                                               vmem_limit_bytes=VMEM_LIMIT))(scal, g, ra)


def _rs_add_chips(scal, p32, rb, hr):
    cols = p32.shape[2]
    tr = _row_tile(hr, cols)
    nr = hr // tr

    def body(s_ref, p_ref, r0, r1, r2, o_ref):
        o_ref[...] = ((p_ref[...] + r0[...].astype(F32)) + r1[...].astype(F32)) + r2[...].astype(F32)

    blk = lambda f: bs((None, tr, cols), f)
    spec = pltpu.PrefetchScalarGridSpec(
        num_scalar_prefetch=1, grid=(nr,),
        in_specs=[blk(lambda i, sr: (sr[0], i, 0))] + [blk(functools.partial(lambda i, sr, j: (j, i, 0), j=j))
                                                        for j in range(3)],
        out_specs=blk(lambda i, sr: (sr[1], i, 0)))
    return pl.pallas_call(body, name="rs_add_chips", grid_spec=spec, out_shape=_sds((2, hr, cols), F32),
                          compiler_params=pltpu.CompilerParams(dimension_semantics=("arbitrary",),
                                                               vmem_limit_bytes=VMEM_LIMIT))(scal, p32, rb, rb, rb)


class SplitReduceScatter:
    def __init__(self, gs):
        x, y, c = _place()
        self.scal = jnp.stack([2 * x + y, c]).astype(jnp.int32)
        self.gs, self.n = list(gs), len(gs)
        self.hrs = [g.shape[1] // 2 for g in gs]

    def swap_start(self, after):
        n, hrs = self.n, self.hrs

        def plan(refs):
            x, y, c = _place()
            return [(refs[a].at[:, pl.ds((1 - c) * hrs[a], hrs[a])], refs[n + a], refs[n + a], (x, y, 1 - c))
                    for a in range(n)]

        lands = [lax.empty((N_SH, hrs[a], g.shape[2]), F32) for a, g in enumerate(self.gs)]
        self.ex = SplitExchange("rs_swap_halves", self.gs + lands, plan, n)
        return self.ex.start(after)

    def swap_wait_send_start(self, after):
        n, hrs = self.n, self.hrs
        bufs = self.ex.wait(after)
        parts = [_rs_add_sibling(self.scal, bufs[a], bufs[n + a], hrs[a]) for a in range(n)]
        self.p32 = [p[0] for p in parts]

        def plan(refs):
            x, y, c = _place()
            return [(refs[a].at[2 * cx + cy], refs[n + a].at[j], refs[n + a].at[j], (cx, cy, c))
                    for a in range(n) for j, (cx, cy) in enumerate(_chips_of(x, y))]

        lands = [lax.empty((3, hrs[a], g.shape[2]), BF16) for a, g in enumerate(self.gs)]
        self.ex = SplitExchange("rs_send_partials", [p[1] for p in parts] + lands, plan, 3 * n)
        return self.ex.start(parts[-1][1])

    def send_wait_share_start(self, after):
        n, hrs = self.n, self.hrs
        bufs = self.ex.wait(after)
        fins = [_rs_add_chips(self.scal, self.p32[a], bufs[n + a], hrs[a]) for a in range(n)]

        def plan(refs):
            x, y, c = _place()
            return [(refs[a].at[c], refs[a].at[c], refs[a].at[1 - c], (x, y, 1 - c)) for a in range(n)]

        self.ex = SplitExchange("rs_share_halves", fins, plan, n)
        return self.ex.start(fins[-1])

    def share_wait(self, after):
        fulls = self.ex.wait(after)
        return [f.reshape(2 * hr, f.shape[2]) for f, hr in zip(fulls, self.hrs)]


def reduce_scatter_grads(gs):
    n = len(gs)
    x, y, c = _place()
    scal = jnp.stack([2 * x + y, c]).astype(jnp.int32)
    hrs = [g.shape[1] // 2 for g in gs]

    def swap_body(*refs):
        ins, outs = refs[:n], refs[n:2 * n]
        send, recv = refs[2 * n:]
        xx, yy, cc = _place()
        cps = []
        for a in range(n):
            cp = pltpu.make_async_remote_copy(src_ref=ins[a].at[:, pl.ds((1 - cc) * hrs[a], hrs[a])], dst_ref=outs[a],
                                              send_sem=send.at[a], recv_sem=recv.at[a],
                                              device_id=(xx, yy, 1 - cc), device_id_type=MESH)
            cp.start()
            cps.append(cp)
        for cp in cps:
            cp.wait()

    ras = _comm_call(swap_body, "rs_swap_halves", n, [_sds((N_SH, hrs[a], gs[a].shape[2]), F32) for a in range(n)],
                     [pltpu.SemaphoreType.DMA((n,)), pltpu.SemaphoreType.DMA((n,))])(*gs)

    parts = [_rs_add_sibling(scal, gs[a], ras[a], hrs[a]) for a in range(n)]

    def ici_body(*refs):
        ins, outs = refs[:n], refs[n:2 * n]
        send, recv = refs[2 * n:]
        xx, yy, cc = _place()
        chips = [(1 - xx, yy), (xx, 1 - yy), (1 - xx, 1 - yy)]
        cps = []
        for a in range(n):
            for j, chip in enumerate(chips):
                cp = pltpu.make_async_remote_copy(src_ref=ins[a].at[2 * chip[0] + chip[1]], dst_ref=outs[a].at[j],
                                                  send_sem=send.at[3 * a + j], recv_sem=recv.at[3 * a + j],
                                                  device_id=(chip[0], chip[1], cc), device_id_type=MESH)
                cp.start()
                cps.append(cp)
        for cp in cps:
            cp.wait()

    rbs = _comm_call(ici_body, "rs_send_partials", n, [_sds((3, hrs[a], gs[a].shape[2]), BF16) for a in range(n)],
                     [pltpu.SemaphoreType.DMA((3 * n,)), pltpu.SemaphoreType.DMA((3 * n,))])(*[p[1] for p in parts])

    fins = [_rs_add_chips(scal, parts[a][0], rbs[a], hrs[a]) for a in range(n)]

    def share_body(*refs):
        ins, outs = refs[:n], refs[n:2 * n]
        send, recv = refs[2 * n:]
        xx, yy, cc = _place()
        sib = (xx, yy, 1 - cc)
        cps = []
        for a in range(n):
            cp = pltpu.make_async_remote_copy(src_ref=ins[a].at[cc], dst_ref=outs[a].at[cc], send_sem=send.at[a],
                                              recv_sem=recv.at[a], device_id=sib, device_id_type=MESH)
            cp.start()
            cps.append(cp)
        for a in range(n):
            pltpu.make_async_remote_copy(src_ref=ins[a].at[cc], dst_ref=outs[a].at[1 - cc], send_sem=send.at[a],
                                         recv_sem=recv.at[a], device_id=sib, device_id_type=MESH).wait_recv()
        for cp in cps:
            cp.wait_send()

    fulls = _comm_call(share_body, "rs_share_halves", n, [_sds(f.shape, F32) for f in fins],
                       [pltpu.SemaphoreType.DMA((n,)), pltpu.SemaphoreType.DMA((n,))],
                       aliases={a: a for a in range(n)})(*fins)
    return [f.reshape(2 * hr, f.shape[2]) for f, hr in zip(fulls, hrs)]


def adamw(w, g, m, v):
    shape = w.shape
    cols = shape[-1]
    rows = math.prod(shape[:-1]) if len(shape) > 1 else 1
    tr = 256 if rows % 256 == 0 and rows > 256 else rows
    c1 = 1.0 - ADAM_B1 ** ADAM_STEP
    c2 = 1.0 - ADAM_B2 ** ADAM_STEP

    def body(w_ref, g_ref, m_ref, v_ref, d_ref, nm_ref, nv_ref):
        gv = g_ref[...]
        nm = ADAM_B1 * m_ref[...] + (1.0 - ADAM_B1) * gv
        nv = ADAM_B2 * v_ref[...] + (1.0 - ADAM_B2) * (gv * gv)
        nm_ref[...] = nm
        nv_ref[...] = nv
        d_ref[...] = -ADAM_LR * ((nm / c1) / (jnp.sqrt(nv / c2) + ADAM_EPS) + ADAM_WD * w_ref[...])

    row = bs((tr, cols), lambda i: (i, 0))
    outs = _call(body, "adamw", (rows // tr,), [row] * 4, [row] * 3, [_sds((rows, cols), F32)] * 3)(
        *[a.reshape(rows, cols) for a in (w, g, m, v)])
    return [o.reshape(shape) for o in outs]


def _no_hook(_):
    return None


def layer_fwd(x, p_i, w, hooks=(_no_hook,) * 3):
    h, proj = norm_in_proj(x, w["g_mix"], w["win"], after=hooks[0](x))
    ya = conva_fwd(proj, w["conv_a"])
    qkv = qkv_cast(proj)
    yb, o32, lse = attn_merge([attn_fwd_group(qkv, d) for d in DILATIONS])
    yc = sgu_fwd(proj, w["sgu_ln_g"], w["sgu_ln_b"], w["sgu_wt"], w["sgu_bf"])
    yd, z = conf_fwd(proj, w["conf_dw"], w["conf_ln_g"], w["conf_ln_b"])
    ys = (ya, yb, yc, yd)
    merged, gates, ybr = merge_fwd(h, ys, w["wg"], w["wbr"])
    x1 = mm_residual(merged, w["wout"], x, "attn_out", after=hooks[1](merged))
    h2, fgu, act = ffn_in(x1, w["g_ffn"], w["wfi"])
    x2 = mm_residual(act, w["wfo"], x1, "ffn_out", after=hooks[2](act))
    h3, gate, pp, x3 = ple_fwd(x2, w["g_ple"], w["wpg"], p_i, w["wpp"])
    saved = dict(x=x, h=h, proj=proj, qkv=qkv, ys=ys, o32=o32, lse=lse, z=z, merged=merged, gates=gates, ybr=ybr, x1=x1,
                 h2=h2, fgu=fgu, act=act, x2=x2, h3=h3, gate=gate, pp=pp)
    return x3, saved


def layer_bwd(dx3, p_i, w, s, hooks=(_no_hook,) * 4):
    t = dx3.shape[0]
    tr = min(1024, t)
    nr = t // tr
    ns_fi = FFN_H // 2
    small = {}

    dpre, dpp = ple_bwd_pre(dx3, s["gate"], s["pp"], after=hooks[0](dx3))
    ga_shape, gb_shape = _sds((N_SH, 6 * BW, D_MODEL), F32), _sds((N_SH, 5 * BW, BW), F32)
    ga_blk = lambda idx: bs((N_SH, BW, D_MODEL), idx)
    ga = tn_matmul("dw_ple_gate", s["h3"], dpre, (nr,), bs((tr, D_MODEL), lambda r: (r, 0)),
                   bs((tr, D_MODEL), lambda r: (r, 0)), ga_blk(lambda r: (0, 5, 0)), ga_shape, split=N_SH)
    gb = tn_matmul("dw_ple_proj", p_i, dpp, (N_SH, nr), bs((tr, BW), lambda j, r: (r, 0)),
                   bs((tr, BW), lambda j, r: (r, j)), bs((None, BW, BW), lambda j, r: (j, 4, 0)), gb_shape)
    dx2, small["g_ple"] = norm_bwd(
        "ple_norm_bwd",
        [(dpre, lambda tm, loc: bs((tm, D_MODEL), lambda i, k: (i, loc(k))), w["wpg"],
          lambda loc: bs((D_MODEL, D_MODEL), lambda i, k: (0, loc(k))), 1)],
        None, dx3, s["x2"], w["g_ple"])

    df = ffn_bwd_act(dx2, w["wfo"], s["fgu"], after=hooks[1](dx2))
    gfo = tn_matmul("dw_ffn_out", s["act"], dx2, (2, nr), bs((tr, ns_fi), lambda j, r: (r, j)),
                    bs((tr, D_MODEL), lambda j, r: (r, 0)), bs((2, FFN_H // N_SH, D_MODEL), lambda j, r: (j, 0, 0)),
                    _sds((N_SH, FFN_H // N_SH, D_MODEL), F32), split=2)
    gfi = tn_matmul("dw_ffn_in", s["h2"], df, (N_SH, nr), bs((tr, D_MODEL), lambda j, r: (r, 0)),
                    bs((None, tr, ns_fi), lambda j, r: (j // 2, r, j % 2)),
                    bs((None, D_MODEL, ns_fi), lambda j, r: (j, 0, 0)), _sds((N_SH, D_MODEL, ns_fi), F32))
    dx1, small["g_ffn"] = norm_bwd(
        "ffn_norm_bwd",
        [(df, lambda tm, loc: bs((None, tm, ns_fi), lambda i, k: (loc(k) // 2, i, loc(k) % 2)), w["wfi"],
          lambda loc: bs((None, D_MODEL, ns_fi), lambda i, k: (loc(k), 0, 0)), N_SH)],
        None, dx2, s["x1"], w["g_ffn"])

    dpre_m, dyb = merge_bwd_pre(dx1, w["wout"], s["gates"], s["ybr"])
    ga = tn_matmul("dw_out", s["merged"], dx1, (nr,), bs((tr, D_MODEL), lambda r: (r, 0)),
                   bs((tr, D_MODEL), lambda r: (r, 0)), ga_blk(lambda r: (0, 4, 0)), ga_shape, split=N_SH, into=ga,
                   after=hooks[2](dyb))
    ga = tn_matmul("dw_merge_gate", s["h"], dpre_m, (N_BR, nr), bs((tr, D_MODEL), lambda k, r: (r, 0)),
                   bs((None, tr, D_MODEL), lambda k, r: (k, r, 0)), ga_blk(lambda k, r: (0, k, 0)), ga_shape,
                   split=N_SH, into=ga)
    for k in range(N_BR):
        gb = tn_matmul("dw_branch", s["ys"][k], dyb, (nr,), bs((tr, BW), lambda r: (r, 0)),
                       bs((None, tr, D_MODEL), functools.partial(lambda r, kk: (kk, r, 0), kk=k)),
                       bs((N_SH, BW, BW), functools.partial(lambda r, kk: (0, kk, 0), kk=k)), gb_shape,
                       split_cols=N_SH, into=gb)
    dys = branch_out_bwd(dyb, w["wbr"])
    hooks[3](dys)

    (dab, dac, dax), small["conv_a"] = conva_bwd(s["proj"], dys, w["conv_a"])
    dy_b = dys[1]
    ld = attn_delta(dy_b, s["o32"], s["lse"])
    acc = None
    for d in reversed(DILATIONS):
        acc = attn_bwd_group(s["qkv"], dy_b, ld, acc, d)
    du, dv, d_sw, d_sbf, small["sgu_ln_g"], small["sgu_ln_b"] = sgu_bwd(
        s["proj"], dys, w["sgu_ln_g"], w["sgu_ln_b"], w["sgu_wt"], w["sgu_bf"])
    small["sgu_w"] = jnp.where(jnp.tril(jnp.ones((BLK, BLK), bool))[None], d_sw, 0.0)
    small["sgu_b"] = jnp.sum(d_sbf.reshape(BLK, 4, HEAD_D), axis=-1).T
    dz, small["conf_ln_g"], small["conf_ln_b"] = conf_bwd_ln(s["z"], dys, w["conf_ln_g"], w["conf_ln_b"])
    dval, dgate, small["conf_dw"] = conf_bwd_conv(s["proj"], dz, w["conf_dw"])
    dproj = jnp.concatenate([dab, dac, dax, acc.astype(BF16), du, dv, dval, dgate], axis=1)

    ns_in = N_IN // N_SH
    gin = tn_matmul("dw_in", s["h"], dproj, (N_SH, nr), bs((tr, D_MODEL), lambda j, r: (r, 0)),
                    bs((tr, ns_in), lambda j, r: (r, j)), bs((None, D_MODEL, ns_in), lambda j, r: (j, 0, 0)),
                    _sds((N_SH, D_MODEL, ns_in), F32))
    big = [ga, gfo, gb, gin, gfi]
    dx, small["g_mix"] = norm_bwd(
        "mix_norm_bwd",
        [(dpre_m, lambda tm, loc: bs((None, tm, D_MODEL), lambda i, k: (loc(k), i, 0)), w["wg"],
          lambda loc: bs((None, D_MODEL, D_MODEL), lambda i, k: (loc(k), 0, 0)), N_BR),
         (dproj, lambda tm, loc: bs((tm, ns_in), lambda i, k: (i, loc(k))), w["win"],
          lambda loc: bs((None, D_MODEL, ns_in), lambda i, k: (loc(k), 0, 0)), N_SH)],
        None, dx1, s["x"], w["g_mix"])
    return dx, big, small


BIG_NAMES = ("w_in", "w_branch", "w_merge_gate", "w_out", "w_ffn_in", "w_ffn_out", "w_ple_gate", "w_ple_proj")


def unpack_big_grads(ga, gfo, gb, gin, gfi):
    return dict(w_in=gin, w_ffn_in=gfi, w_ffn_out=gfo,
                w_merge_gate=ga[:N_BR * BW].reshape(N_BR, BW, D_MODEL), w_out=ga[N_BR * BW:5 * BW], w_ple_gate=ga[5 * BW:],
                w_branch=gb[:N_BR * BW].reshape(N_BR, BW, BW), w_ple_proj=gb[N_BR * BW:])


SMALL_NAMES = ("g_mix", "conv_a", "sgu_ln_g", "sgu_ln_b", "sgu_w", "sgu_b", "conf_dw", "conf_ln_g", "conf_ln_b",
               "g_ffn", "g_ple")


def _pack_rows(arrays, rows):
    flat = jnp.concatenate([a.reshape(-1) for a in arrays])
    return jnp.pad(flat, (0, rows * D_MODEL - flat.shape[0])).reshape(rows, D_MODEL)


def _unpack_rows(packed, shapes):
    flat, out, pos = packed.reshape(-1), [], 0
    for shape in shapes:
        n = math.prod(shape)
        out.append(flat[pos:pos + n].reshape(shape))
        pos += n
    return out


def kernel(x, p, g_mix, w_in, conv_a, sgu_ln_g, sgu_ln_b, sgu_w, sgu_b, conf_dw, conf_ln_g, conf_ln_b, w_branch, w_merge_gate, w_out, g_ffn, w_ffn_in, w_ffn_out, g_ple, w_ple_gate, w_ple_proj, g_final, loss_target, m_g_mix, m_w_in, m_conv_a, m_sgu_ln_g, m_sgu_ln_b, m_sgu_w, m_sgu_b, m_conf_dw, m_conf_ln_g, m_conf_ln_b, m_w_branch, m_w_merge_gate, m_w_out, m_g_ffn, m_w_ffn_in, m_w_ffn_out, m_g_ple, m_w_ple_gate, m_w_ple_proj, m_g_final, v_g_mix, v_w_in, v_conv_a, v_sgu_ln_g, v_sgu_ln_b, v_sgu_w, v_sgu_b, v_conf_dw, v_conf_ln_g, v_conf_ln_b, v_w_branch, v_w_merge_gate, v_w_out, v_g_ffn, v_w_ffn_in, v_w_ffn_out, v_g_ple, v_w_ple_gate, v_w_ple_proj, v_g_final):
    weights = dict(g_mix=g_mix, w_in=w_in, conv_a=conv_a, sgu_ln_g=sgu_ln_g, sgu_ln_b=sgu_ln_b, sgu_w=sgu_w, sgu_b=sgu_b,
                   conf_dw=conf_dw, conf_ln_g=conf_ln_g, conf_ln_b=conf_ln_b, w_branch=w_branch, w_merge_gate=w_merge_gate,
                   w_out=w_out, g_ffn=g_ffn, w_ffn_in=w_ffn_in, w_ffn_out=w_ffn_out, g_ple=g_ple, w_ple_gate=w_ple_gate,
                   w_ple_proj=w_ple_proj, g_final=g_final)
    m_in = dict(g_mix=m_g_mix, w_in=m_w_in, conv_a=m_conv_a, sgu_ln_g=m_sgu_ln_g, sgu_ln_b=m_sgu_ln_b, sgu_w=m_sgu_w,
                sgu_b=m_sgu_b, conf_dw=m_conf_dw, conf_ln_g=m_conf_ln_g, conf_ln_b=m_conf_ln_b, w_branch=m_w_branch,
                w_merge_gate=m_w_merge_gate, w_out=m_w_out, g_ffn=m_g_ffn, w_ffn_in=m_w_ffn_in, w_ffn_out=m_w_ffn_out,
                g_ple=m_g_ple, w_ple_gate=m_w_ple_gate, w_ple_proj=m_w_ple_proj, g_final=m_g_final)
    v_in = dict(g_mix=v_g_mix, w_in=v_w_in, conv_a=v_conv_a, sgu_ln_g=v_sgu_ln_g, sgu_ln_b=v_sgu_ln_b, sgu_w=v_sgu_w,
                sgu_b=v_sgu_b, conf_dw=v_conf_dw, conf_ln_g=v_conf_ln_g, conf_ln_b=v_conf_ln_b, w_branch=v_w_branch,
                w_merge_gate=v_w_merge_gate, w_out=v_w_out, g_ffn=v_g_ffn, w_ffn_in=v_w_ffn_in, w_ffn_out=v_w_ffn_out,
                g_ple=v_g_ple, w_ple_gate=v_w_ple_gate, w_ple_proj=v_w_ple_proj, g_final=v_g_final)
    order = ("g_mix", "w_in", "conv_a", "sgu_ln_g", "sgu_ln_b", "sgu_w", "sgu_b", "conf_dw", "conf_ln_g", "conf_ln_b",
             "w_branch", "w_merge_gate", "w_out", "g_ffn", "w_ffn_in", "w_ffn_out", "g_ple", "w_ple_gate", "w_ple_proj",
             "g_final")
    depth = g_mix.shape[0]
    xs, tgt = x[0], loss_target[0]
    cw = BW // N_SH
    my_shard = 2 * lax.axis_index("x") + lax.axis_index("y")

    conv_rows = 16
    allc = gather8(_pack_rows([conv_a, conf_dw], conv_rows), reduce=False)
    shards = [_unpack_rows(allc[2 * s], [conv_a.shape, conf_dw.shape]) for s in range(N_SH)]
    conv_a_full = jnp.concatenate([sh[0] for sh in shards], axis=-1)
    conf_dw_full = jnp.concatenate([sh[1] for sh in shards], axis=-1)

    tril = jnp.tril(jnp.ones((BLK, BLK), bool))
    def placed_shards(i):
        shards = ([w_in[i], w_branch[i]] + [w_merge_gate[i, k] for k in range(N_BR)]
                  + [w_out[i], w_ffn_in[i], w_ffn_out[i], w_ple_gate[i], w_ple_proj[i]])
        return [lax.dynamic_update_slice(jnp.zeros((N_SH,) + sh.shape, BF16), sh.astype(BF16)[None],
                                         (my_shard,) + (0,) * sh.ndim) for sh in shards]

    def layer_weights(i, got):
        vec = lambda a: a[i].reshape(1, -1)
        return dict(
            win=got[0], wbr=got[1], wg=jnp.stack([g.reshape(D_MODEL, D_MODEL) for g in got[2:6]]),
            wout=got[6].reshape(D_MODEL, D_MODEL), wfi=got[7], wfo=got[8].reshape(FFN_H, D_MODEL),
            wpg=got[9].reshape(D_MODEL, D_MODEL), wpp=got[10],
            g_mix=vec(g_mix), g_ffn=vec(g_ffn), g_ple=vec(g_ple), conv_a=conv_a_full[i], conf_dw=conf_dw_full[i],
            sgu_ln_g=vec(sgu_ln_g), sgu_ln_b=vec(sgu_ln_b), conf_ln_g=vec(conf_ln_g), conf_ln_b=vec(conf_ln_b),
            sgu_wt=jnp.where(tril[None], sgu_w[i], 0.0).astype(BF16),
            sgu_bf=jnp.repeat(sgu_b[i].T, HEAD_D, axis=1))

    layers = [layer_weights(0, allgather_weights(placed_shards(0)))]
    act, saved = xs, []
    for i in range(depth):
        hooks = (_no_hook,) * 3
        if i + 1 < depth:
            bufs = placed_shards(i + 1)
            shapes = [b.shape for b in bufs]
            ici = SplitExchange("allgather_ici", bufs, allgather_ici_plan(shapes), 3 * len(bufs))
            state = {}

            def ici_wait_d2d_start(arr, ici=ici, shapes=shapes, state=state):
                landed = ici.wait(arr)
                state["d2d"] = SplitExchange("allgather_d2d", landed, allgather_d2d_plan(shapes), 3 * len(landed))
                return state["d2d"].start(landed[-1])

            def d2d_wait(arr, state=state):
                state["got"] = state["d2d"].wait(arr)
                return None

            hooks = (ici.start, ici_wait_d2d_start, d2d_wait)
        act, sv = layer_fwd(act, p[i, 0], layers[i], hooks)
        saved.append(sv)
        if i + 1 < depth:
            layers.append(layer_weights(i + 1, state["got"]))
    loss_part, dx, dg_final = loss_head(act, g_final.reshape(1, -1), tgt)

    big_red = [None] * depth
    small_parts = [None] * depth
    pending = None
    for i in reversed(range(depth)):
        hooks = (_no_hook,) * 4
        result = {}
        if pending is not None:
            rs, j = pending

            def finish(arr, rs=rs, j=j, result=result):
                result[j] = rs.share_wait(arr)
                return None

            hooks = (rs.swap_start, rs.swap_wait_send_start, rs.send_wait_share_start, finish)
        dx, big, small_parts[i] = layer_bwd(dx, p[i, 0], layers[i], saved[i], hooks)
        if pending is not None:
            big_red[pending[1]] = unpack_big_grads(*result[pending[1]])
        pending = (SplitReduceScatter(big), i) if i > 0 else None
        if i == 0:
            big_red[0] = unpack_big_grads(*reduce_scatter_grads(big))

    small_list = [jnp.stack([small_parts[i][n].reshape(weights[n].shape[1:] if n not in ("conv_a", "conf_dw")
                                                       else small_parts[i][n].shape) for i in range(depth)])
                  for n in SMALL_NAMES]
    small_list += [dg_final.reshape(-1), loss_part[0, :1]]
    small_shapes = [a.shape for a in small_list]
    n_small = sum(math.prod(sh) for sh in small_shapes)
    small_rows = -(-n_small // (8 * D_MODEL)) * 8
    red = _unpack_rows(gather8(_pack_rows(small_list, small_rows), reduce=True), small_shapes)
    grads = dict(zip(SMALL_NAMES, red[:len(SMALL_NAMES)]))
    grads["g_final"] = red[-2]
    loss = red[-1].reshape(())
    for n in ("conv_a", "conf_dw"):
        grads[n] = lax.dynamic_slice_in_dim(grads[n], my_shard * cw, cw, axis=2)
    for name in BIG_NAMES:
        grads[name] = jnp.stack([big_red[i][name] for i in range(depth)])

    small_all = [n for n in order if n not in BIG_NAMES]
    sm_shapes = [weights[n].shape for n in small_all]
    n_sm = sum(math.prod(sh) for sh in sm_shapes)
    sm_rows = -(-n_sm // (8 * D_MODEL)) * 8
    packed = [_pack_rows([src[n] for n in small_all], sm_rows) for src in (weights, grads, m_in, v_in)]
    sm_out = [_unpack_rows(o, sm_shapes) for o in adamw(*packed)]
    delta, new_m, new_v = ({n: o[k] for k, n in enumerate(small_all)} for o in sm_out)
    for name in BIG_NAMES:
        delta[name], new_m[name], new_v[name] = adamw(weights[name], grads[name], m_in[name], v_in[name])

    return (loss, dx[None], *[grads[n] for n in order], *[delta[n] for n in order], *[new_m[n] for n in order],
            *[new_v[n] for n in order])
```

```python
import functools
import math

import jax
import jax.numpy as jnp
from jax import lax
from jax.experimental import pallas as pl
from jax.experimental.pallas import tpu as pltpu

F32 = jnp.float32
BF16 = jnp.bfloat16
EPS = 1e-6
D_MODEL = 1024
BW = 256
N_BR = 4
N_IN = 10 * BW
FFN_H = 2816
N_SH = 4
HEADS = 4
HEAD_D = 64
BLK = 128
DILATIONS = (1, 4, 16)
CONF_K = 31
CONVA_K = 3
NEG = -1e30
VMEM_LIMIT = 56 * 1024 * 1024
MESH = pl.DeviceIdType.MESH

ADAM_LR, ADAM_B1, ADAM_B2, ADAM_EPS, ADAM_WD, ADAM_STEP = 0.001, 0.9, 0.999, 1e-08, 0.01, 10

bs = pl.BlockSpec
ANY = pl.BlockSpec(memory_space=pl.ANY)


def _call(body, name, grid, in_specs, out_specs, out_shape, scratch=(), aliases=None, after=None):
    n_in = len(in_specs)
    kernel_body = body
    if after is not None:
        in_specs = list(in_specs) + [ANY]

        def kernel_body(*refs):
            return body(*refs[:n_in], *refs[n_in + 1:])

    call = pl.pallas_call(
        kernel_body, name=name, grid=grid, in_specs=in_specs, out_specs=out_specs, out_shape=out_shape,
        scratch_shapes=list(scratch), input_output_aliases=aliases or {},
        compiler_params=pltpu.CompilerParams(dimension_semantics=("arbitrary",) * len(grid),
                                             vmem_limit_bytes=VMEM_LIMIT))
    return call if after is None else (lambda *args: call(*args, after))


def _sds(shape, dtype):
    return jax.ShapeDtypeStruct(shape, dtype)


def _nn(a, b):
    return jnp.dot(a, b, preferred_element_type=F32)


def _nt(a, b):
    return lax.dot_general(a, b, (((1,), (1,)), ((), ())), preferred_element_type=F32)


def _tn(a, b):
    return lax.dot_general(a, b, (((0,), (0,)), ((), ())), preferred_element_type=F32)


def _sigmoid(x):
    return 1.0 / (1.0 + jnp.exp(-x))


def _rms_fwd(x, g):
    r = lax.rsqrt(jnp.mean(x * x, axis=-1, keepdims=True) + EPS)
    return x * r * g


def _rms_bwd(dh, x, g):
    r = lax.rsqrt(jnp.mean(x * x, axis=-1, keepdims=True) + EPS)
    xr = x * r
    dxr = dh * g
    dx = r * (dxr - xr * jnp.mean(dxr * xr, axis=-1, keepdims=True))
    return dx, dh * xr


def _ln_hat(x):
    mu = jnp.mean(x, axis=-1, keepdims=True)
    xc = x - mu
    r = lax.rsqrt(jnp.mean(xc * xc, axis=-1, keepdims=True) + EPS)
    return xc * r, r


def _ln_bwd(dy, xhat, r, g):
    dxh = dy * g
    return r * (dxh - jnp.mean(dxh, axis=-1, keepdims=True) - xhat * jnp.mean(dxh * xhat, axis=-1, keepdims=True))


def _colsum(v):
    return jnp.sum(v, axis=0, keepdims=True)


def _causal_conv(zext, w_ref, k_taps, halo):
    acc = zext[halo:] * w_ref[k_taps - 1:k_taps, :]
    for k in range(k_taps - 1):
        acc = acc + pltpu.roll(zext, k_taps - 1 - k, 0)[halo:] * w_ref[k:k + 1, :]
    return acc


def _anti_conv(dext, w_ref, k_taps, tm):
    n = dext.shape[0]
    acc = dext[:tm] * w_ref[k_taps - 1:k_taps, :]
    for s in range(1, k_taps):
        acc = acc + pltpu.roll(dext, n - s, 0)[:tm] * w_ref[k_taps - 1 - s:k_taps - s, :]
    return acc


def _conv_wgrad(dw_ref, dc, zext, k_taps, halo):
    dw_ref[k_taps - 1:k_taps, :] += _colsum(dc * zext[halo:])
    for k in range(k_taps - 1):
        dw_ref[k:k + 1, :] += _colsum(dc * pltpu.roll(zext, k_taps - 1 - k, 0)[halo:])


def _prev_blk(i, per):
    return jnp.maximum(i * per - 1, 0)


def _next_blk(i, per, last):
    return jnp.minimum((i + 1) * per, last)


def norm_in_proj(x, g, win, after=None):
    t = x.shape[0]
    tm = min(512, t)
    ns = win.shape[2]

    def body(x_ref, g_ref, w_ref, h_ref, o_ref, q_ref):
        h = _rms_fwd(x_ref[...], g_ref[...]).astype(BF16)
        h_ref[...] = h
        for s in range(N_SH):
            r = _nn(h, w_ref[s])
            o_ref[:, s * ns:(s + 1) * ns] = r
            if s == 1:
                q_ref[:, :2 * BW] = r[:, 3 * BW - ns:].astype(BF16)
            if s == 2:
                q_ref[:, 2 * BW:] = r[:, :6 * BW - 2 * ns].astype(BF16)

    row = lambda c: bs((tm, c), lambda i: (i, 0))
    return _call(
        body, "norm_in_proj", (t // tm,), [row(D_MODEL), bs((1, D_MODEL), lambda i: (0, 0)), _resident(win)],
        [row(D_MODEL), row(N_IN), row(3 * BW)],
        [_sds((t, D_MODEL), BF16), _sds((t, N_IN), F32), _sds((t, 3 * BW), BF16)], after=after)(x, g, win)


def merge_fwd(h, ys, wg, wbr):
    t = h.shape[0]
    tm = min(512, t)

    def body(h_ref, ya, yb, yc, yd, wg_ref, wb_ref, m_ref, g_ref, b_ref):
        hh = h_ref[...]
        for j in range(N_SH):
            cs = slice(j * BW, (j + 1) * BW)
            acc = None
            for k, y_ref in enumerate((ya, yb, yc, yd)):
                g = _sigmoid(_nn(hh, wg_ref[k, :, cs]))
                b = _nn(y_ref[...], wb_ref[j, k])
                g_ref[k, :, cs] = g.astype(BF16)
                b_ref[k, :, cs] = b.astype(BF16)
                acc = g * b if acc is None else acc + g * b
            m_ref[:, cs] = acc.astype(BF16)

    ysp = bs((tm, BW), lambda i: (i, 0))
    big = bs((N_BR, tm, D_MODEL), lambda i: (0, i, 0))
    return _call(
        body, "merge_fwd", (t // tm,),
        [bs((tm, D_MODEL), lambda i: (i, 0)), ysp, ysp, ysp, ysp, _resident(wg), _resident(wbr)],
        [bs((tm, D_MODEL), lambda i: (i, 0)), big, big],
        [_sds((t, D_MODEL), BF16), _sds((N_BR, t, D_MODEL), BF16), _sds((N_BR, t, D_MODEL), BF16)])(
            h, *ys, wg, wbr)


def mm_residual(a, w, res, name, after=None):
    t, kk = a.shape
    tm = min(512, t)

    def body(a_ref, w_ref, r_ref, o_ref):
        o_ref[...] = r_ref[...] + _nn(a_ref[...], w_ref[...])

    row = bs((tm, D_MODEL), lambda i: (i, 0))
    return _call(body, name, (t // tm,), [bs((tm, kk), lambda i: (i, 0)), _resident(w), row], row,
                 _sds((t, D_MODEL), F32), after=after)(a, w, res)


def ffn_in(x, g, wfi):
    t = x.shape[0]
    tm = min(512, t)
    ns = wfi.shape[2]

    def body(x_ref, g_ref, w_ref, h_ref, f_ref, a_ref):
        h = _rms_fwd(x_ref[...], g_ref[...]).astype(BF16)
        h_ref[...] = h
        for j in range(2):
            cs = slice(j * ns, (j + 1) * ns)
            fg = _nn(h, w_ref[j])
            fu = _nn(h, w_ref[j + 2])
            f_ref[0, :, cs] = fg.astype(BF16)
            f_ref[1, :, cs] = fu.astype(BF16)
            a_ref[:, cs] = (fg * _sigmoid(fg) * fu).astype(BF16)

    row = lambda c: bs((tm, c), lambda i: (i, 0))
    return _call(
        body, "ffn_in", (t // tm,), [row(D_MODEL), bs((1, D_MODEL), lambda i: (0, 0)), _resident(wfi)],
        [row(D_MODEL), bs((2, tm, FFN_H), lambda i: (0, i, 0)), row(FFN_H)],
        [_sds((t, D_MODEL), BF16), _sds((2, t, FFN_H), BF16), _sds((t, FFN_H), BF16)])(x, g, wfi)


def ple_fwd(x, g, wpg, p_i, wpp):
    t = x.shape[0]
    tm = min(512, t)

    def body(x_ref, g_ref, wg_ref, p_ref, wp_ref, h_ref, gt_ref, pp_ref, o_ref):
        xv = x_ref[...]
        h = _rms_fwd(xv, g_ref[...]).astype(BF16)
        h_ref[...] = h
        gate = _sigmoid(_nn(h, wg_ref[...]))
        pb = p_ref[...].astype(BF16)
        pp = jnp.concatenate([_nn(pb, wp_ref[j]) for j in range(N_SH)], axis=1)
        gt_ref[...] = gate.astype(BF16)
        pp_ref[...] = pp.astype(BF16)
        o_ref[...] = xv + gate * pp

    row = bs((tm, D_MODEL), lambda i: (i, 0))
    return _call(
        body, "ple_fwd", (t // tm,),
        [row, bs((1, D_MODEL), lambda i: (0, 0)), _resident(wpg), bs((tm, BW), lambda i: (i, 0)), _resident(wpp)],
        [row, row, row, row],
        [_sds((t, D_MODEL), BF16), _sds((t, D_MODEL), BF16), _sds((t, D_MODEL), BF16), _sds((t, D_MODEL), F32)])(
            x, g, wpg, p_i, wpp)


def loss_head(x, g, tgt):
    t = x.shape[0]
    tm = min(512, t)

    def body(x_ref, g_ref, t_ref, l_ref, dx_ref, dg_ref):
        @pl.when(pl.program_id(0) == 0)
        def _():
            l_ref[...] = jnp.zeros_like(l_ref)
            dg_ref[...] = jnp.zeros_like(dg_ref)

        xv, gv = x_ref[...], g_ref[...]
        err = _rms_fwd(xv, gv) - t_ref[...]
        part = 0.5 * jnp.sum(jnp.mean(err * err, axis=-1, keepdims=True), axis=0, keepdims=True)
        l_ref[...] += jnp.broadcast_to(part, l_ref.shape)
        dx, dgr = _rms_bwd(err * (1.0 / D_MODEL), xv, gv)
        dx_ref[...] = dx
        dg_ref[...] += _colsum(dgr)

    row = bs((tm, D_MODEL), lambda i: (i, 0))
    vec = bs((1, D_MODEL), lambda i: (0, 0))
    return _call(body, "loss_head", (t // tm,), [row, vec, row],
                 [bs((1, 128), lambda i: (0, 0)), row, vec],
                 [_sds((1, 128), F32), _sds((t, D_MODEL), F32), _sds((1, D_MODEL), F32)])(x, g, tgt)


def tn_matmul(name, a, b, grid, a_spec, b_spec, out_spec, out_shape, split=0, split_cols=0, into=None, after=None):
    last = len(grid) - 1

    def body(a_ref, b_ref, *rest):
        o_ref = rest[-1]

        @pl.when(pl.program_id(last) == 0)
        def _():
            o_ref[...] = jnp.zeros_like(o_ref)

        res = _tn(a_ref[...].astype(BF16), b_ref[...].astype(BF16))
        if split_cols:
            cols = res.shape[1] // split_cols
            for s in range(split_cols):
                o_ref[s] += res[:, s * cols:(s + 1) * cols]
        elif split:
            rows = res.shape[0] // split
            for s in range(split):
                o_ref[s] += res[s * rows:(s + 1) * rows]
        else:
            o_ref[...] += res

    if into is None:
        return _call(body, name, grid, [a_spec, b_spec], out_spec, out_shape, after=after)(a, b)
    return _call(body, name, grid, [a_spec, b_spec, ANY], out_spec, out_shape, aliases={2: 0}, after=after)(a, b, into)


def _resident(w):
    zeros = (0,) * w.ndim
    return bs(w.shape, lambda i: zeros, pipeline_mode=pl.Buffered(1))


def norm_bwd(name, sources, dx_in, x, g):
    t = x.shape[0]
    tm = min(512, t)
    n_src = len(sources)

    def body(*refs):
        dxi_ref, x_ref, g_ref, dx_ref, dg_ref = refs[2 * n_src:]

        @pl.when(pl.program_id(0) == 0)
        def _():
            dg_ref[...] = jnp.zeros_like(dg_ref)

        dh = None
        for si in range(n_src):
            for av, wv in sources[si][3](refs[2 * si], refs[2 * si + 1]):
                part = _nt(av, wv)
                dh = part if dh is None else dh + part
        dx, dgr = _rms_bwd(dh, x_ref[...], g_ref[...])
        dx_ref[...] = dxi_ref[...] + dx
        dg_ref[...] += _colsum(dgr)

    in_specs, args = [], []
    for a, a_spec, w, _ in sources:
        in_specs += [a_spec(tm), _resident(w)]
        args += [a, w]
    row = bs((tm, D_MODEL), lambda i: (i, 0))
    vec = bs((1, D_MODEL), lambda i: (0, 0))
    return _call(body, name, (t // tm,), in_specs + [row, row, vec], [row, vec],
                 [_sds((t, D_MODEL), F32), _sds((1, D_MODEL), F32)])(*args, dx_in, x, g)


def ple_bwd_pre(dx, gate, pp, after=None):
    t = dx.shape[0]
    tm = min(1024, t)

    def body(dx_ref, g_ref, p_ref, dpre_ref, dpp_ref):
        d = dx_ref[...]
        g = g_ref[...].astype(F32)
        dpre_ref[...] = (d * p_ref[...].astype(F32) * g * (1.0 - g)).astype(BF16)
        dpp_ref[...] = (d * g).astype(BF16)

    row = bs((tm, D_MODEL), lambda i: (i, 0))
    return _call(body, "ple_bwd_pre", (t // tm,), [row, row, row], [row, row],
                 [_sds((t, D_MODEL), BF16), _sds((t, D_MODEL), BF16)], after=after)(dx, gate, pp)


def ffn_bwd_act(dx, wfo, fgu, after=None):
    t = dx.shape[0]
    tm = min(512, t)
    ns = FFN_H // 2

    def body(dx_ref, w_ref, f_ref, o_ref):
        dxb = dx_ref[...].astype(BF16)
        for j in range(2):
            cs = slice(j * ns, (j + 1) * ns)
            dact = _nt(dxb, w_ref[cs, :])
            fg = f_ref[0, :, cs].astype(F32)
            fu = f_ref[1, :, cs].astype(F32)
            s = _sigmoid(fg)
            o_ref[0, :, cs] = (dact * fu * (s * (1.0 + fg * (1.0 - s)))).astype(BF16)
            o_ref[1, :, cs] = (dact * fg * s).astype(BF16)

    blk = bs((2, tm, FFN_H), lambda i: (0, i, 0))
    return _call(body, "ffn_bwd_act", (t // tm,), [bs((tm, D_MODEL), lambda i: (i, 0)), _resident(wfo), blk],
                 blk, _sds((2, t, FFN_H), BF16), after=after)(dx, wfo, fgu)


def merge_bwd(dx, wout, gates, ybr, wbr):
    t = dx.shape[0]
    tm = min(256, t)

    def body(dx_ref, w_ref, g_ref, b_ref, wb_ref, dpre_ref, dyb_ref, dy_ref):
        dm = _nt(dx_ref[...].astype(BF16), w_ref[...])
        for k in range(N_BR):
            g = g_ref[k].astype(F32)
            dpre_ref[k] = (dm * b_ref[k].astype(F32) * g * (1.0 - g)).astype(BF16)
            dyb = (dm * g).astype(BF16)
            dyb_ref[k] = dyb
            acc = None
            for s in range(N_SH):
                part = _nt(dyb[:, s * BW:(s + 1) * BW], wb_ref[s, k])
                acc = part if acc is None else acc + part
            dy_ref[k] = acc

    blk = bs((N_BR, tm, D_MODEL), lambda i: (0, i, 0))
    return _call(body, "merge_bwd", (t // tm,),
                 [bs((tm, D_MODEL), lambda i: (i, 0)), _resident(wout), blk, blk, _resident(wbr)],
                 [blk, blk, bs((N_BR, tm, BW), lambda i: (0, i, 0))],
                 [_sds((N_BR, t, D_MODEL), BF16), _sds((N_BR, t, D_MODEL), BF16), _sds((N_BR, t, BW), F32)])(
                     dx, wout, gates, ybr, wbr)


def conva_fwd(proj, wa):
    t = proj.shape[0]
    tm, halo = min(512, t), 8
    per = tm // halo

    def body(b_ref, c_ref, x_ref, ch_ref, xh_ref, w_ref, y_ref):
        zh = jnp.where(pl.program_id(0) > 0, ch_ref[...] * xh_ref[...], 0.0)
        zext = jnp.concatenate([zh, c_ref[...] * x_ref[...]], axis=0)
        y_ref[...] = (b_ref[...] * _causal_conv(zext, w_ref, CONVA_K, halo)).astype(BF16)

    col = lambda c: bs((tm, BW), lambda i: (i, c))
    hal = lambda c: bs((halo, BW), lambda i: (_prev_blk(i, per), c))
    return _call(body, "conva_fwd", (t // tm,),
                 [col(0), col(1), col(2), hal(1), hal(2), bs((CONVA_K, BW), lambda i: (0, 0))],
                 bs((tm, BW), lambda i: (i, 0)), _sds((t, BW), BF16))(proj, proj, proj, proj, proj, wa)


def conva_bwd(proj, dys, wa):
    t = proj.shape[0]
    tm, halo = min(512, t), 8
    per = tm // halo
    last = t // halo - 1
    nt = t // tm

    def body(b_ref, c_ref, x_ref, ch_ref, xh_ref, bn_ref, dy_ref, dyn_ref, w_ref, db_ref, dc_ref, dxx_ref, dw_ref):
        i = pl.program_id(0)

        @pl.when(i == 0)
        def _():
            dw_ref[...] = jnp.zeros_like(dw_ref)

        zh = jnp.where(i > 0, ch_ref[...] * xh_ref[...], 0.0)
        cv, xv = c_ref[...], x_ref[...]
        zext = jnp.concatenate([zh, cv * xv], axis=0)
        dy = dy_ref[...]
        dconv = dy * b_ref[...]
        dcn = jnp.where(i < nt - 1, dyn_ref[...] * bn_ref[...], 0.0)
        dz = _anti_conv(jnp.concatenate([dconv, dcn], axis=0), w_ref, CONVA_K, tm)
        db_ref[...] = (dy * _causal_conv(zext, w_ref, CONVA_K, halo)).astype(BF16)
        dc_ref[...] = (dz * xv).astype(BF16)
        dxx_ref[...] = (dz * cv).astype(BF16)
        _conv_wgrad(dw_ref, dconv, zext, CONVA_K, halo)

    col = lambda c: bs((tm, BW), lambda i: (i, c))
    hal = lambda c: bs((halo, BW), lambda i: (_prev_blk(i, per), c))
    nxt = bs((halo, BW), lambda i: (_next_blk(i, per, last), 0))
    wsp = bs((CONVA_K, BW), lambda i: (0, 0))
    outs = _call(body, "conva_bwd", (t // tm,),
                 [col(0), col(1), col(2), hal(1), hal(2), nxt,
                  bs((None, tm, BW), lambda i: (0, i, 0)), bs((None, halo, BW), lambda i: (0, _next_blk(i, per, last), 0)), wsp],
                 [bs((tm, BW), lambda i: (i, 0))] * 3 + [wsp],
                 [_sds((t, BW), BF16)] * 3 + [_sds((CONVA_K, BW), F32)])(proj, proj, proj, proj, proj, proj, dys, dys, wa)
    return outs[:3], outs[3]


def _head_masks():
    lane = lax.broadcasted_iota(jnp.int32, (1, BW), 1)
    return [(lane >= h * HEAD_D) & (lane < (h + 1) * HEAD_D) for h in range(HEADS)]


def _band_masks():
    qi = lax.broadcasted_iota(jnp.int32, (BLK, BLK), 0)
    ki = lax.broadcasted_iota(jnp.int32, (BLK, BLK), 1)
    return ki >= qi, ki <= qi


def attn_fwd_group(qkv, d):
    t = qkv.shape[0]
    rows = t // d
    qb = min(512, rows)
    nb = qb // BLK
    scale = HEAD_D ** -0.5
    pv = qkv.reshape(rows, d * 3 * BW)

    def body(q_ref, k_ref, v_ref, kh_ref, vh_ref, o_ref):
        n = pl.program_id(1)
        hm = _head_masks()
        m_prev, m_cur = _band_masks()
        for b in range(nb):
            rs = slice(b * BLK, (b + 1) * BLK)
            q = q_ref[rs, :]
            if b == 0:
                kp, vp = kh_ref[...], vh_ref[...]
                mp = m_prev & (n > 0)
            else:
                ps = slice((b - 1) * BLK, b * BLK)
                kp, vp = k_ref[ps, :], v_ref[ps, :]
                mp = m_prev
            qs = jnp.concatenate([jnp.where(hm[h], q, 0.0).astype(BF16) for h in range(HEADS)], axis=0)
            kcat = jnp.concatenate([kp, k_ref[rs, :]], axis=0)
            vcat = jnp.concatenate([vp, v_ref[rs, :]], axis=0)
            band = jnp.concatenate([mp, m_cur], axis=1)
            s = jnp.where(jnp.concatenate([band] * HEADS, axis=0), _nt(qs, kcat) * scale, NEG)
            m = jnp.max(s, axis=-1, keepdims=True)
            e = jnp.exp(s - m)
            l = jnp.sum(e, axis=-1, keepdims=True)
            of = _nn(e.astype(BF16), vcat) / l
            lse = m + jnp.log(l)
            o_acc = jnp.zeros((BLK, BW), F32)
            l_acc = jnp.zeros((BLK, BW), F32)
            for h in range(HEADS):
                hs = slice(h * BLK, (h + 1) * BLK)
                o_acc = jnp.where(hm[h], of[hs, :], o_acc)
                l_acc = jnp.where(hm[h], lse[hs, :], l_acc)
            o_ref[rs, :BW] = o_acc
            o_ref[rs, BW:] = l_acc

    per = qb // BLK
    main = lambda c: bs((qb, BW), lambda r, n: (n, r * 3 + c))
    hal = lambda c: bs((BLK, BW), lambda r, n: (_prev_blk(n, per), r * 3 + c))
    ol = _call(body, f"attn_fwd_d{d}", (d, rows // qb), [main(0), main(1), main(2), hal(1), hal(2)],
               bs((qb, 2 * BW), lambda r, n: (n, r)), _sds((rows, d * 2 * BW), F32))(pv, pv, pv, pv, pv)
    return ol.reshape(t, 2 * BW)


def attn_merge(ols):
    t = ols[0].shape[0]
    tm = min(1024, t)

    def lse3(a, b, c):
        m = jnp.maximum(jnp.maximum(a, b), c)
        return m + jnp.log(jnp.exp(a - m) + jnp.exp(b - m) + jnp.exp(c - m))

    def body(g0, g1, g2, y_ref, o_ref, l_ref):
        gs = [g0[...], g1[...], g2[...]]
        ls = [g[:, BW:] for g in gs]
        tot = lse3(*ls)
        o = (jnp.exp(ls[0] - tot) * gs[0][:, :BW] + jnp.exp(ls[1] - tot) * gs[1][:, :BW]
             + jnp.exp(ls[2] - tot) * gs[2][:, :BW])
        y_ref[...] = o.astype(BF16)
        o_ref[...] = o
        l_ref[...] = tot

    n = bs((tm, BW), lambda i: (i, 0))
    w = bs((tm, 2 * BW), lambda i: (i, 0))
    return _call(body, "attn_merge", (t // tm,), [w] * 3, [n, n, n],
                 [_sds((t, BW), BF16), _sds((t, BW), F32), _sds((t, BW), F32)])(*ols)


def attn_delta(dy, o, lse):
    t = o.shape[0]
    tm = min(1024, t)

    def body(d_ref, o_ref, l_ref, out_ref):
        hm = _head_masks()
        prod = d_ref[...] * o_ref[...]
        delta = jnp.zeros_like(prod)
        for h in range(HEADS):
            delta = jnp.where(hm[h], jnp.sum(jnp.where(hm[h], prod, 0.0), axis=-1, keepdims=True), delta)
        out_ref[:, :BW] = l_ref[...]
        out_ref[:, BW:] = delta

    n = bs((tm, BW), lambda i: (i, 0))
    return _call(body, "attn_delta", (t // tm,), [n, n, n], bs((tm, 2 * BW), lambda i: (i, 0)),
                 _sds((t, 2 * BW), F32))(dy, o, lse)


def attn_bwd_group(qkv, dy, ld, acc, d):
    t = qkv.shape[0]
    rows = t // d
    qb = min(512, rows)
    nb = qb // BLK
    nsteps = rows // qb
    scale = HEAD_D ** -0.5
    pv = qkv.reshape(rows, d * 3 * BW)
    dov = dy.reshape(rows, d * BW)
    ldv = ld.reshape(rows, d * 2 * BW)
    has_acc = acc is not None

    def body(*refs):
        (q_ref, qn_ref, k_ref, kh_ref, v_ref, vh_ref, do_ref, don_ref, ld_ref, ldn_ref) = refs[:10]
        a_ref = refs[10] if has_acc else None
        o_ref = refs[-1]
        n = pl.program_id(1)
        hm = _head_masks()
        m_prev, m_cur = _band_masks()
        has_prev, has_next = n > 0, n < nsteps - 1
        dq = [None] * nb
        dk = [jnp.zeros((BLK, BW), F32) for _ in range(nb)]
        dvv = [jnp.zeros((BLK, BW), F32) for _ in range(nb)]
        for qi in range(nb + 1):
            rs = slice(qi * BLK, (qi + 1) * BLK)
            ps = slice((qi - 1) * BLK, qi * BLK)
            if qi < nb:
                q, do, ldq = q_ref[rs, :], do_ref[rs, :], ld_ref[rs, :]
            else:
                q, do, ldq = qn_ref[...], don_ref[...], ldn_ref[...]
            kp, vp = (kh_ref[...], vh_ref[...]) if qi == 0 else (k_ref[ps, :], v_ref[ps, :])
            kc, vc = (k_ref[rs, :], v_ref[rs, :]) if qi < nb else (kp, vp)
            mp = m_prev & has_prev if qi == 0 else (m_prev & has_next if qi == nb else m_prev)
            mc = m_cur if qi < nb else jnp.zeros_like(m_cur)
            band = jnp.concatenate([jnp.concatenate([mp, mc], axis=1)] * HEADS, axis=0)
            qs = jnp.concatenate([jnp.where(hm[h], q, 0.0).astype(BF16) for h in range(HEADS)], axis=0)
            dos = jnp.concatenate([jnp.where(hm[h], do, 0.0).astype(BF16) for h in range(HEADS)], axis=0)
            kcat = jnp.concatenate([kp, kc], axis=0)
            vcat = jnp.concatenate([vp, vc], axis=0)
            col = lambda v, h: jnp.broadcast_to(jnp.max(jnp.where(hm[h], v, NEG), axis=-1, keepdims=True), (BLK, 2 * BLK))
            lcols = jnp.concatenate([col(ldq[:, :BW], h) for h in range(HEADS)], axis=0)
            dcols = jnp.concatenate([col(ldq[:, BW:], h) for h in range(HEADS)], axis=0)
            p = jnp.where(band, jnp.exp(_nt(qs, kcat) * scale - lcols), 0.0)
            ds = (p * (_nt(dos, vcat) - dcols) * scale).astype(BF16)
            if qi < nb:
                dqf = _nn(ds, kcat)
                acc_q = jnp.zeros((BLK, BW), F32)
                for h in range(HEADS):
                    acc_q = jnp.where(hm[h], dqf[h * BLK:(h + 1) * BLK, :], acc_q)
                dq[qi] = acc_q
            dkc = _tn(ds, qs)
            dvc = _tn(p.astype(BF16), dos)
            if qi >= 1:
                dk[qi - 1] = dk[qi - 1] + dkc[:BLK]
                dvv[qi - 1] = dvv[qi - 1] + dvc[:BLK]
            if qi < nb:
                dk[qi] = dk[qi] + dkc[BLK:]
                dvv[qi] = dvv[qi] + dvc[BLK:]
        for b in range(nb):
            rs = slice(b * BLK, (b + 1) * BLK)
            for c, val in enumerate((dq[b], dk[b], dvv[b])):
                cs = slice(c * BW, (c + 1) * BW)
                o_ref[rs, cs] = a_ref[rs, cs] + val if has_acc else val

    per = qb // BLK
    last = rows // BLK - 1
    main = lambda c: bs((qb, BW), lambda r, n: (n, r * 3 + c))
    prv = lambda c: bs((BLK, BW), lambda r, n: (_prev_blk(n, per), r * 3 + c))
    nxt = lambda c: bs((BLK, BW), lambda r, n: (_next_blk(n, per, last), r * 3 + c))
    accs = bs((qb, 3 * BW), lambda r, n: (n, r))
    in_specs = [main(0), nxt(0), main(1), prv(1), main(2), prv(2),
                bs((qb, BW), lambda r, n: (n, r)), bs((BLK, BW), lambda r, n: (_next_blk(n, per, last), r)),
                bs((qb, 2 * BW), lambda r, n: (n, r)), bs((BLK, 2 * BW), lambda r, n: (_next_blk(n, per, last), r))]
    args = [pv, pv, pv, pv, pv, pv, dov, dov, ldv, ldv]
    if has_acc:
        in_specs.append(accs)
        args.append(acc.reshape(rows, d * 3 * BW))
    out = _call(body, f"attn_bwd_d{d}", (d, nsteps), in_specs, accs, _sds((rows, d * 3 * BW), F32),
                aliases={10: 0} if has_acc else None)(*args)
    return out.reshape(t, 3 * BW)


def _group_masks():
    lane = lax.broadcasted_iota(jnp.int32, (1, BW), 1)
    return [(lane >= g * HEAD_D) & (lane < (g + 1) * HEAD_D) for g in range(4)]


def sgu_fwd(proj, ln_g, ln_b, w_tril, b_full):
    t = proj.shape[0]
    tm = min(512, t)

    def body(u_ref, v_ref, g_ref, b_ref, w_ref, bf_ref, y_ref):
        gm = _group_masks()
        xhat, _ = _ln_hat(v_ref[...])
        vb = (xhat * g_ref[...] + b_ref[...]).astype(BF16)
        for c in range(tm // BLK):
            rs = slice(c * BLK, (c + 1) * BLK)
            vc = vb[rs, :]
            mixed = bf_ref[...]
            for g in range(4):
                mixed = mixed + jnp.where(gm[g], _nn(w_ref[g], vc), 0.0)
            y_ref[rs, :] = (u_ref[rs, :] * mixed).astype(BF16)

    vec = bs((1, BW), lambda i: (0, 0))
    return _call(body, "sgu_fwd", (t // tm,),
                 [bs((tm, BW), lambda i: (i, 6)), bs((tm, BW), lambda i: (i, 7)), vec, vec,
                  bs((4, BLK, BLK), lambda i: (0, 0, 0)), bs((BLK, BW), lambda i: (0, 0))],
                 bs((tm, BW), lambda i: (i, 0)), _sds((t, BW), BF16))(proj, proj, ln_g, ln_b, w_tril, b_full)


def sgu_bwd(proj, dys, ln_g, ln_b, w_tril, b_full):
    t = proj.shape[0]
    tm = min(512, t)

    def body(u_ref, v_ref, dy_ref, g_ref, b_ref, w_ref, bf_ref, du_ref, dv_ref, dw_ref, dbf_ref, dg_ref, db_ref, dvl_ref):
        @pl.when(pl.program_id(0) == 0)
        def _():
            dw_ref[...] = jnp.zeros_like(dw_ref)
            dbf_ref[...] = jnp.zeros_like(dbf_ref)
            dg_ref[...] = jnp.zeros_like(dg_ref)
            db_ref[...] = jnp.zeros_like(db_ref)

        gm = _group_masks()
        xhat, r = _ln_hat(v_ref[...])
        gv = g_ref[...]
        vb = (xhat * gv + b_ref[...]).astype(BF16)
        for c in range(tm // BLK):
            rs = slice(c * BLK, (c + 1) * BLK)
            vc = vb[rs, :]
            dy = dy_ref[rs, :]
            mixed = bf_ref[...]
            for g in range(4):
                mixed = mixed + jnp.where(gm[g], _nn(w_ref[g], vc), 0.0)
            du_ref[rs, :] = (dy * mixed).astype(BF16)
            dm = dy * u_ref[rs, :]
            dbf_ref[...] += dm
            dvl = jnp.zeros((BLK, BW), F32)
            for g in range(4):
                dmg = jnp.where(gm[g], dm, 0.0).astype(BF16)
                dw_ref[g] += _nt(dmg, vc)
                dvl = dvl + _tn(w_ref[g], dmg)
            dvl_ref[rs, :] = dvl
        dvl = dvl_ref[...]
        dv_ref[...] = _ln_bwd(dvl, xhat, r, gv).astype(BF16)
        dg_ref[...] += _colsum(dvl * xhat)
        db_ref[...] += _colsum(dvl)

    vec = bs((1, BW), lambda i: (0, 0))
    row = bs((tm, BW), lambda i: (i, 0))
    wsp = bs((4, BLK, BLK), lambda i: (0, 0, 0))
    bfs = bs((BLK, BW), lambda i: (0, 0))
    return _call(body, "sgu_bwd", (t // tm,),
                 [bs((tm, BW), lambda i: (i, 6)), bs((tm, BW), lambda i: (i, 7)), bs((None, tm, BW), lambda i: (2, i, 0)),
                  vec, vec, wsp, bfs],
                 [row, row, wsp, bfs, vec, vec],
                 [_sds((t, BW), BF16), _sds((t, BW), BF16), _sds((4, BLK, BLK), F32), _sds((BLK, BW), F32),
                  _sds((1, BW), F32), _sds((1, BW), F32)],
                 scratch=[pltpu.VMEM((tm, BW), F32)])(proj, proj, dys, ln_g, ln_b, w_tril, b_full)


CONF_HALO = 32


def conf_fwd(proj, dw, ln_g, ln_b):
    t = proj.shape[0]
    tm, halo = min(512, t), CONF_HALO
    per = tm // halo

    def body(v_ref, gt_ref, vh_ref, gh_ref, w_ref, g_ref, b_ref, y_ref, z_ref):
        yh = jnp.where(pl.program_id(0) > 0, vh_ref[...] * _sigmoid(gh_ref[...]), 0.0)
        yext = jnp.concatenate([yh, v_ref[...] * _sigmoid(gt_ref[...])], axis=0)
        z = _causal_conv(yext, w_ref, CONF_K, halo)
        z_ref[...] = z
        xhat, _ = _ln_hat(z)
        ln = xhat * g_ref[...] + b_ref[...]
        y_ref[...] = (ln * _sigmoid(ln)).astype(BF16)

    vec = bs((1, BW), lambda i: (0, 0))
    col = lambda c: bs((tm, BW), lambda i: (i, c))
    hal = lambda c: bs((halo, BW), lambda i: (_prev_blk(i, per), c))
    row = bs((tm, BW), lambda i: (i, 0))
    return _call(body, "conf_fwd", (t // tm,),
                 [col(8), col(9), hal(8), hal(9), bs((CONF_K, BW), lambda i: (0, 0)), vec, vec],
                 [row, row], [_sds((t, BW), BF16), _sds((t, BW), F32)])(proj, proj, proj, proj, dw, ln_g, ln_b)


def conf_bwd_ln(z, dys, ln_g, ln_b):
    t = z.shape[0]
    tm = min(1024, t)

    def body(z_ref, dy_ref, g_ref, b_ref, dz_ref, dg_ref, db_ref):
        @pl.when(pl.program_id(0) == 0)
        def _():
            dg_ref[...] = jnp.zeros_like(dg_ref)
            db_ref[...] = jnp.zeros_like(db_ref)

        gv = g_ref[...]
        xhat, r = _ln_hat(z_ref[...])
        ln = xhat * gv + b_ref[...]
        s = _sigmoid(ln)
        dln = dy_ref[...] * (s * (1.0 + ln * (1.0 - s)))
        dz_ref[...] = _ln_bwd(dln, xhat, r, gv)
        dg_ref[...] += _colsum(dln * xhat)
        db_ref[...] += _colsum(dln)

    vec = bs((1, BW), lambda i: (0, 0))
    row = bs((tm, BW), lambda i: (i, 0))
    return _call(body, "conf_bwd_ln", (t // tm,), [row, bs((None, tm, BW), lambda i: (3, i, 0)), vec, vec],
                 [row, vec, vec], [_sds((t, BW), F32), _sds((1, BW), F32), _sds((1, BW), F32)])(z, dys, ln_g, ln_b)


def conf_bwd_conv(proj, dz, dw):
    t = proj.shape[0]
    tm, halo = min(512, t), CONF_HALO
    per = tm // halo
    last = t // halo - 1
    nt = t // tm

    def body(v_ref, gt_ref, vh_ref, gh_ref, dz_ref, dzn_ref, w_ref, dv_ref, dg_ref, dw_ref):
        i = pl.program_id(0)

        @pl.when(i == 0)
        def _():
            dw_ref[...] = jnp.zeros_like(dw_ref)

        val = v_ref[...]
        sg = _sigmoid(gt_ref[...])
        yh = jnp.where(i > 0, vh_ref[...] * _sigmoid(gh_ref[...]), 0.0)
        yext = jnp.concatenate([yh, val * sg], axis=0)
        dz = dz_ref[...]
        dzn = jnp.where(i < nt - 1, dzn_ref[...], 0.0)
        dy0 = _anti_conv(jnp.concatenate([dz, dzn], axis=0), w_ref, CONF_K, tm)
        dv_ref[...] = (dy0 * sg).astype(BF16)
        dg_ref[...] = (dy0 * val * sg * (1.0 - sg)).astype(BF16)
        _conv_wgrad(dw_ref, dz, yext, CONF_K, halo)

    col = lambda c: bs((tm, BW), lambda i: (i, c))
    hal = lambda c: bs((halo, BW), lambda i: (_prev_blk(i, per), c))
    row = bs((tm, BW), lambda i: (i, 0))
    wsp = bs((CONF_K, BW), lambda i: (0, 0))
    return _call(body, "conf_bwd_conv", (t // tm,),
                 [col(8), col(9), hal(8), hal(9), row, bs((halo, BW), lambda i: (_next_blk(i, per, last), 0)), wsp],
                 [row, row, wsp], [_sds((t, BW), BF16), _sds((t, BW), BF16), _sds((CONF_K, BW), F32)])(
                     proj, proj, proj, proj, dz, dz, dw)


def _place():
    return lax.axis_index("x"), lax.axis_index("y"), lax.axis_index("c")


def _comm_call(body, name, n_in, out_shape, scratch, aliases=None):
    return pl.pallas_call(body, name=name, in_specs=[ANY] * n_in, out_specs=[ANY] * len(out_shape), out_shape=out_shape,
                          scratch_shapes=scratch, input_output_aliases=aliases or {},
                          compiler_params=pltpu.CompilerParams(has_side_effects=True, vmem_limit_bytes=VMEM_LIMIT))


HBM_SPEC = pl.BlockSpec(memory_space=pltpu.HBM)
SEM_SPEC = pl.BlockSpec(memory_space=pltpu.SEMAPHORE)
EFFECT = pltpu.SideEffectType.DATAFLOW_SIDE_EFFECTING


class SplitExchange:
    def __init__(self, name, bufs, plan, n_copies):
        self.name, self.bufs, self.plan, self.n = name, list(bufs), plan, n_copies

    def start(self, after):
        nb, n, plan = len(self.bufs), self.n, self.plan

        def body(*refs):
            send, recv, token = refs[nb + 1], refs[nb + 2], refs[-1]
            for k, (src, dst, _, dev) in enumerate(plan(refs[:nb])):
                pltpu.make_async_remote_copy(src_ref=src, dst_ref=dst, send_sem=send.at[k], recv_sem=recv.at[k],
                                             device_id=dev, device_id_type=MESH).start()
            token[...] = jnp.zeros_like(token)

        outs = pl.pallas_call(
            body, name=self.name + "_start",
            out_shape=(pltpu.SemaphoreType.DMA((n,)), pltpu.SemaphoreType.DMA((n,)),
                       *[pltpu.HBM(b.shape, b.dtype) for b in self.bufs], _sds((8, 128), F32)),
            in_specs=[HBM_SPEC] * nb + [ANY],
            out_specs=(SEM_SPEC, SEM_SPEC, *[HBM_SPEC] * nb, pl.BlockSpec(memory_space=pltpu.VMEM)),
            input_output_aliases={i: 2 + i for i in range(nb)},
            compiler_params=pltpu.CompilerParams(has_side_effects=EFFECT))(
                *[pltpu.with_memory_space_constraint(b, pltpu.HBM) for b in self.bufs], after)
        self.send, self.recv, self.bufs = outs[0], outs[1], list(outs[2:2 + nb])
        return outs[-1]

    def wait(self, after):
        nb, plan = len(self.bufs), self.plan

        def body(*refs):
            send, recv = refs[nb], refs[nb + 1]
            for k, (src, _, land, dev) in enumerate(plan(refs[:nb])):
                cp = pltpu.make_async_remote_copy(src_ref=src, dst_ref=land, send_sem=send.at[k], recv_sem=recv.at[k],
                                                  device_id=dev, device_id_type=MESH)
                cp.wait_send()
                cp.wait_recv()

        outs = pl.pallas_call(
            body, name=self.name + "_wait", out_shape=tuple(pltpu.HBM(b.shape, b.dtype) for b in self.bufs),
            in_specs=[HBM_SPEC] * nb + [SEM_SPEC, SEM_SPEC, ANY], out_specs=[HBM_SPEC] * nb,
            input_output_aliases={i: i for i in range(nb)},
            compiler_params=pltpu.CompilerParams(has_side_effects=EFFECT))(*self.bufs, self.send, self.recv, after)
        return list(outs)


def _chips_of(x, y):
    return [(1 - x, y), (x, 1 - y), (1 - x, 1 - y)]


def allgather_ici_plan(shapes):
    def plan(refs):
        x, y, c = _place()
        out = []
        for a, ref in enumerate(refs):
            hl = shapes[a][1] // 2
            half = pl.ds(c * hl, hl)
            for cx, cy in _chips_of(x, y):
                mine = ref.at[2 * x + y, half]
                out.append((mine, mine, ref.at[2 * cx + cy, half], (cx, cy, c)))
        return out
    return plan


def allgather_d2d_plan(shapes):
    def plan(refs):
        x, y, c = _place()
        out = []
        for a, ref in enumerate(refs):
            hl = shapes[a][1] // 2
            for cx, cy in _chips_of(x, y):
                got = ref.at[2 * cx + cy, pl.ds(c * hl, hl)]
                out.append((got, got, ref.at[2 * cx + cy, pl.ds((1 - c) * hl, hl)], (x, y, 1 - c)))
        return out
    return plan


def gather8(v, reduce):
    rows, cols = v.shape

    def body(v_ref, o_ref, land_ref, send, recv, lsem):
        x, y, c = _place()
        me = 4 * x + 2 * y + c
        land = land_ref if reduce else o_ref
        mine = pltpu.make_async_copy(v_ref, land.at[me], lsem)
        mine.start()
        sent = []
        for j in range(1, 8):
            fx, fy, fc = (j >> 2) & 1, (j >> 1) & 1, j & 1
            tgt = (1 - x if fx else x, 1 - y if fy else y, 1 - c if fc else c)
            cp = pltpu.make_async_remote_copy(src_ref=v_ref, dst_ref=land.at[me], send_sem=send.at[j - 1],
                                              recv_sem=recv.at[j - 1], device_id=tgt, device_id_type=MESH)
            cp.start()
            sent.append(cp)
        for j in range(1, 8):
            fx, fy, fc = (j >> 2) & 1, (j >> 1) & 1, j & 1
            peer = 4 * (1 - x if fx else x) + 2 * (1 - y if fy else y) + (1 - c if fc else c)
            pltpu.make_async_remote_copy(src_ref=v_ref, dst_ref=land.at[peer], send_sem=send.at[j - 1],
                                         recv_sem=recv.at[j - 1], device_id=(x, y, c), device_id_type=MESH).wait_recv()
        for cp in sent:
            cp.wait_send()
        mine.wait()
        if reduce:
            acc = land_ref[0]
            for k in range(1, 8):
                acc = acc + land_ref[k]
            o_ref[...] = acc

    vm = pl.BlockSpec(memory_space=pltpu.VMEM)
    out_shape = _sds((rows, cols), F32) if reduce else _sds((8, rows, cols), F32)
    land_shape = (8, rows, cols) if reduce else (8, 128)
    return pl.pallas_call(
        body, name="allreduce8" if reduce else "allgather8", in_specs=[vm], out_specs=vm, out_shape=out_shape,
        scratch_shapes=[pltpu.VMEM(land_shape, F32), pltpu.SemaphoreType.DMA((7,)), pltpu.SemaphoreType.DMA((7,)),
                        pltpu.SemaphoreType.DMA],
        compiler_params=pltpu.CompilerParams(has_side_effects=True, vmem_limit_bytes=VMEM_LIMIT))(v)


def allgather_weights(bufs):
    n = len(bufs)

    def body(*refs):
        ins, outs = refs[:n], refs[n:2 * n]
        send, recv = refs[2 * n:]
        x, y, c = _place()
        s_me = 2 * x + y
        chips = [(1 - x, y), (x, 1 - y), (1 - x, 1 - y)]
        sibling = (x, y, 1 - c)
        started = []
        for a in range(n):
            hl = bufs[a].shape[1] // 2
            half = pl.ds(c * hl, hl)
            for j, chip in enumerate(chips):
                cp = pltpu.make_async_remote_copy(src_ref=ins[a].at[s_me, half], dst_ref=outs[a].at[s_me, half],
                                                  send_sem=send.at[6 * a + j], recv_sem=recv.at[6 * a + j],
                                                  device_id=(chip[0], chip[1], c), device_id_type=MESH)
                cp.start()
                started.append(cp)
        for a in range(n):
            hl = bufs[a].shape[1] // 2
            half = pl.ds(c * hl, hl)
            for j, chip in enumerate(chips):
                s_j = 2 * chip[0] + chip[1]
                landed = outs[a].at[s_j, half]
                pltpu.make_async_remote_copy(src_ref=landed, dst_ref=landed, send_sem=send.at[6 * a + j],
                                             recv_sem=recv.at[6 * a + j], device_id=sibling, device_id_type=MESH).wait_recv()
                fw = pltpu.make_async_remote_copy(src_ref=landed, dst_ref=landed, send_sem=send.at[6 * a + 3 + j],
                                                  recv_sem=recv.at[6 * a + 3 + j], device_id=sibling, device_id_type=MESH)
                fw.start()
                started.append(fw)
        for a in range(n):
            hl = bufs[a].shape[1] // 2
            other = pl.ds((1 - c) * hl, hl)
            for j, chip in enumerate(chips):
                s_j = 2 * chip[0] + chip[1]
                theirs = outs[a].at[s_j, other]
                pltpu.make_async_remote_copy(src_ref=theirs, dst_ref=theirs, send_sem=send.at[6 * a + 3 + j],
                                             recv_sem=recv.at[6 * a + 3 + j], device_id=sibling, device_id_type=MESH).wait_recv()
        for cp in started:
            cp.wait_send()

    out_shape = [_sds(b.shape, b.dtype) for b in bufs]
    scratch = [pltpu.SemaphoreType.DMA((6 * n,)), pltpu.SemaphoreType.DMA((6 * n,))]
    return _comm_call(body, "allgather_weights", n, out_shape, scratch, aliases={a: a for a in range(n)})(*bufs)


def _row_tile(rows, cols):
    best = 16
    for t in range(16, rows + 1, 16):
        if rows % t == 0 and t * cols * 4 <= 2 * 1024 * 1024:
            best = t
    return best


def _rs_add_sibling(scal, g, ra, hr):
    cols = g.shape[2]
    tr = _row_tile(hr, cols)
    nr = hr // tr

    def body(s_ref, g_ref, r_ref, p32_ref, p16_ref):
        v = g_ref[...] + r_ref[...]
        p32_ref[...] = v
        p16_ref[...] = v.astype(BF16)

    blk = lambda f: bs((None, tr, cols), f)
    own = blk(lambda s, i, sr: (s, i, 0))
    spec = pltpu.PrefetchScalarGridSpec(num_scalar_prefetch=1, grid=(N_SH, nr),
                                        in_specs=[blk(lambda s, i, sr: (s, sr[1] * nr + i, 0)), own],
                                        out_specs=[own, own])
    return pl.pallas_call(body, name="rs_add_sibling", grid_spec=spec,
                          out_shape=[_sds((N_SH, hr, cols), F32), _sds((N_SH, hr, cols), BF16)],
                          compiler_params=pltpu.CompilerParams(dimension_semantics=("arbitrary",) * 2,
                                                               vmem_limit_bytes=VMEM_LIMIT))(scal, g, ra)


def _rs_add_chips(scal, p32, rb, hr):
    cols = p32.shape[2]
    tr = _row_tile(hr, cols)
    nr = hr // tr

    def body(s_ref, p_ref, r0, r1, r2, o_ref):
        o_ref[...] = ((p_ref[...] + r0[...].astype(F32)) + r1[...].astype(F32)) + r2[...].astype(F32)

    blk = lambda f: bs((None, tr, cols), f)
    spec = pltpu.PrefetchScalarGridSpec(
        num_scalar_prefetch=1, grid=(nr,),
        in_specs=[blk(lambda i, sr: (sr[0], i, 0))] + [blk(functools.partial(lambda i, sr, j: (j, i, 0), j=j))
                                                        for j in range(3)],
        out_specs=blk(lambda i, sr: (sr[1], i, 0)))
    return pl.pallas_call(body, name="rs_add_chips", grid_spec=spec, out_shape=_sds((2, hr, cols), F32),
                          compiler_params=pltpu.CompilerParams(dimension_semantics=("arbitrary",),
                                                               vmem_limit_bytes=VMEM_LIMIT))(scal, p32, rb, rb, rb)


class SplitReduceScatter:
    def __init__(self, gs):
        x, y, c = _place()
        self.scal = jnp.stack([2 * x + y, c]).astype(jnp.int32)
        self.gs, self.n = list(gs), len(gs)
        self.hrs = [g.shape[1] // 2 for g in gs]

    def swap_start(self, after):
        n, hrs = self.n, self.hrs

        def plan(refs):
            x, y, c = _place()
            return [(refs[a].at[:, pl.ds((1 - c) * hrs[a], hrs[a])], refs[n + a], refs[n + a], (x, y, 1 - c))
                    for a in range(n)]

        lands = [lax.empty((N_SH, hrs[a], g.shape[2]), F32) for a, g in enumerate(self.gs)]
        self.ex = SplitExchange("rs_swap_halves", self.gs + lands, plan, n)
        return self.ex.start(after)

    def swap_wait_send_start(self, after):
        n, hrs = self.n, self.hrs
        bufs = self.ex.wait(after)
        parts = [_rs_add_sibling(self.scal, bufs[a], bufs[n + a], hrs[a]) for a in range(n)]
        self.p32 = [p[0] for p in parts]

        def plan(refs):
            x, y, c = _place()
            return [(refs[a].at[2 * cx + cy], refs[n + a].at[j], refs[n + a].at[j], (cx, cy, c))
                    for a in range(n) for j, (cx, cy) in enumerate(_chips_of(x, y))]

        lands = [lax.empty((3, hrs[a], g.shape[2]), BF16) for a, g in enumerate(self.gs)]
        self.ex = SplitExchange("rs_send_partials", [p[1] for p in parts] + lands, plan, 3 * n)
        return self.ex.start(parts[-1][1])

    def send_wait_share_start(self, after):
        n, hrs = self.n, self.hrs
        bufs = self.ex.wait(after)
        fins = [_rs_add_chips(self.scal, self.p32[a], bufs[n + a], hrs[a]) for a in range(n)]

        def plan(refs):
            x, y, c = _place()
            return [(refs[a].at[c], refs[a].at[c], refs[a].at[1 - c], (x, y, 1 - c)) for a in range(n)]

        self.ex = SplitExchange("rs_share_halves", fins, plan, n)
        return self.ex.start(fins[-1])

    def share_wait(self, after):
        fulls = self.ex.wait(after)
        return [f.reshape(2 * hr, f.shape[2]) for f, hr in zip(fulls, self.hrs)]


def reduce_scatter_grads(gs):
    n = len(gs)
    x, y, c = _place()
    scal = jnp.stack([2 * x + y, c]).astype(jnp.int32)
    hrs = [g.shape[1] // 2 for g in gs]

    def swap_body(*refs):
        ins, outs = refs[:n], refs[n:2 * n]
        send, recv = refs[2 * n:]
        xx, yy, cc = _place()
        cps = []
        for a in range(n):
            cp = pltpu.make_async_remote_copy(src_ref=ins[a].at[:, pl.ds((1 - cc) * hrs[a], hrs[a])], dst_ref=outs[a],
                                              send_sem=send.at[a], recv_sem=recv.at[a],
                                              device_id=(xx, yy, 1 - cc), device_id_type=MESH)
            cp.start()
            cps.append(cp)
        for cp in cps:
            cp.wait()

    ras = _comm_call(swap_body, "rs_swap_halves", n, [_sds((N_SH, hrs[a], gs[a].shape[2]), F32) for a in range(n)],
                     [pltpu.SemaphoreType.DMA((n,)), pltpu.SemaphoreType.DMA((n,))])(*gs)

    parts = [_rs_add_sibling(scal, gs[a], ras[a], hrs[a]) for a in range(n)]

    def ici_body(*refs):
        ins, outs = refs[:n], refs[n:2 * n]
        send, recv = refs[2 * n:]
        xx, yy, cc = _place()
        chips = [(1 - xx, yy), (xx, 1 - yy), (1 - xx, 1 - yy)]
        cps = []
        for a in range(n):
            for j, chip in enumerate(chips):
                cp = pltpu.make_async_remote_copy(src_ref=ins[a].at[2 * chip[0] + chip[1]], dst_ref=outs[a].at[j],
                                                  send_sem=send.at[3 * a + j], recv_sem=recv.at[3 * a + j],
                                                  device_id=(chip[0], chip[1], cc), device_id_type=MESH)
                cp.start()
                cps.append(cp)
        for cp in cps:
            cp.wait()

    rbs = _comm_call(ici_body, "rs_send_partials", n, [_sds((3, hrs[a], gs[a].shape[2]), BF16) for a in range(n)],
                     [pltpu.SemaphoreType.DMA((3 * n,)), pltpu.SemaphoreType.DMA((3 * n,))])(*[p[1] for p in parts])

    fins = [_rs_add_chips(scal, parts[a][0], rbs[a], hrs[a]) for a in range(n)]

    def share_body(*refs):
        ins, outs = refs[:n], refs[n:2 * n]
        send, recv = refs[2 * n:]
        xx, yy, cc = _place()
        sib = (xx, yy, 1 - cc)
        cps = []
        for a in range(n):
            cp = pltpu.make_async_remote_copy(src_ref=ins[a].at[cc], dst_ref=outs[a].at[cc], send_sem=send.at[a],
                                              recv_sem=recv.at[a], device_id=sib, device_id_type=MESH)
            cp.start()
            cps.append(cp)
        for a in range(n):
            pltpu.make_async_remote_copy(src_ref=ins[a].at[cc], dst_ref=outs[a].at[1 - cc], send_sem=send.at[a],
                                         recv_sem=recv.at[a], device_id=sib, device_id_type=MESH).wait_recv()
        for cp in cps:
            cp.wait_send()

    fulls = _comm_call(share_body, "rs_share_halves", n, [_sds(f.shape, F32) for f in fins],
                       [pltpu.SemaphoreType.DMA((n,)), pltpu.SemaphoreType.DMA((n,))],
                       aliases={a: a for a in range(n)})(*fins)
    return [f.reshape(2 * hr, f.shape[2]) for f, hr in zip(fulls, hrs)]


def adamw(w, g, m, v):
    shape = w.shape
    cols = shape[-1]
    rows = math.prod(shape[:-1]) if len(shape) > 1 else 1
    tr = 256 if rows % 256 == 0 and rows > 256 else rows
    c1 = 1.0 - ADAM_B1 ** ADAM_STEP
    c2 = 1.0 - ADAM_B2 ** ADAM_STEP

    def body(w_ref, g_ref, m_ref, v_ref, d_ref, nm_ref, nv_ref):
        gv = g_ref[...]
        nm = ADAM_B1 * m_ref[...] + (1.0 - ADAM_B1) * gv
        nv = ADAM_B2 * v_ref[...] + (1.0 - ADAM_B2) * (gv * gv)
        nm_ref[...] = nm
        nv_ref[...] = nv
        d_ref[...] = -ADAM_LR * ((nm / c1) / (jnp.sqrt(nv / c2) + ADAM_EPS) + ADAM_WD * w_ref[...])

    row = bs((tr, cols), lambda i: (i, 0))
    outs = _call(body, "adamw", (rows // tr,), [row] * 4, [row] * 3, [_sds((rows, cols), F32)] * 3)(
        *[a.reshape(rows, cols) for a in (w, g, m, v)])
    return [o.reshape(shape) for o in outs]


def _no_hook(_):
    return None


def layer_fwd(x, p_i, w, hooks=(_no_hook,) * 3):
    h, proj, qkv = norm_in_proj(x, w["g_mix"], w["win"], after=hooks[0](x))
    ya = conva_fwd(proj, w["conv_a"])
    yb, o32, lse = attn_merge([attn_fwd_group(qkv, d) for d in DILATIONS])
    yc = sgu_fwd(proj, w["sgu_ln_g"], w["sgu_ln_b"], w["sgu_wt"], w["sgu_bf"])
    yd, z = conf_fwd(proj, w["conf_dw"], w["conf_ln_g"], w["conf_ln_b"])
    ys = (ya, yb, yc, yd)
    merged, gates, ybr = merge_fwd(h, ys, w["wg"], w["wbr"])
    x1 = mm_residual(merged, w["wout"], x, "attn_out", after=hooks[1](merged))
    h2, fgu, act = ffn_in(x1, w["g_ffn"], w["wfi"])
    x2 = mm_residual(act, w["wfo"], x1, "ffn_out", after=hooks[2](act))
    h3, gate, pp, x3 = ple_fwd(x2, w["g_ple"], w["wpg"], p_i, w["wpp"])
    saved = dict(x=x, h=h, proj=proj, qkv=qkv, ys=ys, o32=o32, lse=lse, z=z, merged=merged, gates=gates, ybr=ybr, x1=x1,
                 h2=h2, fgu=fgu, act=act, x2=x2, h3=h3, gate=gate, pp=pp)
    return x3, saved


def layer_bwd(dx3, p_i, w, s, hooks=(_no_hook,) * 4):
    t = dx3.shape[0]
    tr = min(1024, t)
    nr = t // tr
    ns_fi = FFN_H // 2
    small = {}

    dpre, dpp = ple_bwd_pre(dx3, s["gate"], s["pp"], after=hooks[0](dx3))
    ga_shape, gb_shape = _sds((N_SH, 6 * BW, D_MODEL), F32), _sds((N_SH, 5 * BW, BW), F32)
    ga_blk = lambda idx: bs((N_SH, BW, D_MODEL), idx)
    ga = tn_matmul("dw_ple_gate", s["h3"], dpre, (nr,), bs((tr, D_MODEL), lambda r: (r, 0)),
                   bs((tr, D_MODEL), lambda r: (r, 0)), ga_blk(lambda r: (0, 5, 0)), ga_shape, split=N_SH)
    gb = tn_matmul("dw_ple_proj", p_i, dpp, (N_SH, nr), bs((tr, BW), lambda j, r: (r, 0)),
                   bs((tr, BW), lambda j, r: (r, j)), bs((None, BW, BW), lambda j, r: (j, 4, 0)), gb_shape)
    dx2, small["g_ple"] = norm_bwd(
        "ple_norm_bwd",
        [(dpre, lambda tm: bs((tm, D_MODEL), lambda i: (i, 0)), w["wpg"], lambda a, wr: [(a[...], wr[...])])],
        dx3, s["x2"], w["g_ple"])

    df = ffn_bwd_act(dx2, w["wfo"], s["fgu"], after=hooks[1](dx2))
    gfo = tn_matmul("dw_ffn_out", s["act"], dx2, (2, nr), bs((tr, ns_fi), lambda j, r: (r, j)),
                    bs((tr, D_MODEL), lambda j, r: (r, 0)), bs((2, FFN_H // N_SH, D_MODEL), lambda j, r: (j, 0, 0)),
                    _sds((N_SH, FFN_H // N_SH, D_MODEL), F32), split=2)
    gfi = tn_matmul("dw_ffn_in", s["h2"], df, (N_SH, nr), bs((tr, D_MODEL), lambda j, r: (r, 0)),
                    bs((None, tr, ns_fi), lambda j, r: (j // 2, r, j % 2)),
                    bs((None, D_MODEL, ns_fi), lambda j, r: (j, 0, 0)), _sds((N_SH, D_MODEL, ns_fi), F32))
    dx1, small["g_ffn"] = norm_bwd(
        "ffn_norm_bwd",
        [(df, lambda tm: bs((2, tm, FFN_H), lambda i: (0, i, 0)), w["wfi"],
          lambda a, wr: [(a[k // 2, :, (k % 2) * ns_fi:(k % 2 + 1) * ns_fi], wr[k]) for k in range(N_SH)])],
        dx2, s["x1"], w["g_ffn"])

    dpre_m, dyb, dys = merge_bwd(dx1, w["wout"], s["gates"], s["ybr"], w["wbr"])
    ga = tn_matmul("dw_out", s["merged"], dx1, (nr,), bs((tr, D_MODEL), lambda r: (r, 0)),
                   bs((tr, D_MODEL), lambda r: (r, 0)), ga_blk(lambda r: (0, 4, 0)), ga_shape, split=N_SH, into=ga,
                   after=hooks[2](dyb))
    ga = tn_matmul("dw_merge_gate", s["h"], dpre_m, (N_BR, nr), bs((tr, D_MODEL), lambda k, r: (r, 0)),
                   bs((None, tr, D_MODEL), lambda k, r: (k, r, 0)), ga_blk(lambda k, r: (0, k, 0)), ga_shape,
                   split=N_SH, into=ga)
    for k in range(N_BR):
        gb = tn_matmul("dw_branch", s["ys"][k], dyb, (nr,), bs((tr, BW), lambda r: (r, 0)),
                       bs((None, tr, D_MODEL), functools.partial(lambda r, kk: (kk, r, 0), kk=k)),
                       bs((N_SH, BW, BW), functools.partial(lambda r, kk: (0, kk, 0), kk=k)), gb_shape,
                       split_cols=N_SH, into=gb)
    hooks[3](gb)

    (dab, dac, dax), small["conv_a"] = conva_bwd(s["proj"], dys, w["conv_a"])
    dy_b = dys[1]
    ld = attn_delta(dy_b, s["o32"], s["lse"])
    acc = None
    for d in reversed(DILATIONS):
        acc = attn_bwd_group(s["qkv"], dy_b, ld, acc, d)
    du, dv, d_sw, d_sbf, small["sgu_ln_g"], small["sgu_ln_b"] = sgu_bwd(
        s["proj"], dys, w["sgu_ln_g"], w["sgu_ln_b"], w["sgu_wt"], w["sgu_bf"])
    small["sgu_w"] = jnp.where(jnp.tril(jnp.ones((BLK, BLK), bool))[None], d_sw, 0.0)
    small["sgu_b"] = jnp.sum(d_sbf.reshape(BLK, 4, HEAD_D), axis=-1).T
    dz, small["conf_ln_g"], small["conf_ln_b"] = conf_bwd_ln(s["z"], dys, w["conf_ln_g"], w["conf_ln_b"])
    dval, dgate, small["conf_dw"] = conf_bwd_conv(s["proj"], dz, w["conf_dw"])
    dproj = jnp.concatenate([dab, dac, dax, acc.astype(BF16), du, dv, dval, dgate], axis=1)

    ns_in = N_IN // N_SH
    gin = tn_matmul("dw_in", s["h"], dproj, (N_SH, nr), bs((tr, D_MODEL), lambda j, r: (r, 0)),
                    bs((tr, ns_in), lambda j, r: (r, j)), bs((None, D_MODEL, ns_in), lambda j, r: (j, 0, 0)),
                    _sds((N_SH, D_MODEL, ns_in), F32))
    big = [ga, gfo, gb, gin, gfi]
    dx, small["g_mix"] = norm_bwd(
        "mix_norm_bwd",
        [(dpre_m, lambda tm: bs((N_BR, tm, D_MODEL), lambda i: (0, i, 0)), w["wg"],
          lambda a, wr: [(a[k], wr[k]) for k in range(N_BR)]),
         (dproj, lambda tm: bs((tm, N_IN), lambda i: (i, 0)), w["win"],
          lambda a, wr: [(a[:, k * ns_in:(k + 1) * ns_in], wr[k]) for k in range(N_SH)])],
        dx1, s["x"], w["g_mix"])
    return dx, big, small


BIG_NAMES = ("w_in", "w_branch", "w_merge_gate", "w_out", "w_ffn_in", "w_ffn_out", "w_ple_gate", "w_ple_proj")


def unpack_big_grads(ga, gfo, gb, gin, gfi):
    return dict(w_in=gin, w_ffn_in=gfi, w_ffn_out=gfo,
                w_merge_gate=ga[:N_BR * BW].reshape(N_BR, BW, D_MODEL), w_out=ga[N_BR * BW:5 * BW], w_ple_gate=ga[5 * BW:],
                w_branch=gb[:N_BR * BW].reshape(N_BR, BW, BW), w_ple_proj=gb[N_BR * BW:])


SMALL_NAMES = ("g_mix", "conv_a", "sgu_ln_g", "sgu_ln_b", "sgu_w", "sgu_b", "conf_dw", "conf_ln_g", "conf_ln_b",
               "g_ffn", "g_ple")


def _pack_rows(arrays, rows):
    flat = jnp.concatenate([a.reshape(-1) for a in arrays])
    return jnp.pad(flat, (0, rows * D_MODEL - flat.shape[0])).reshape(rows, D_MODEL)


def _unpack_rows(packed, shapes):
    flat, out, pos = packed.reshape(-1), [], 0
    for shape in shapes:
        n = math.prod(shape)
        out.append(flat[pos:pos + n].reshape(shape))
        pos += n
    return out


def kernel(x, p, g_mix, w_in, conv_a, sgu_ln_g, sgu_ln_b, sgu_w, sgu_b, conf_dw, conf_ln_g, conf_ln_b, w_branch, w_merge_gate, w_out, g_ffn, w_ffn_in, w_ffn_out, g_ple, w_ple_gate, w_ple_proj, g_final, loss_target, m_g_mix, m_w_in, m_conv_a, m_sgu_ln_g, m_sgu_ln_b, m_sgu_w, m_sgu_b, m_conf_dw, m_conf_ln_g, m_conf_ln_b, m_w_branch, m_w_merge_gate, m_w_out, m_g_ffn, m_w_ffn_in, m_w_ffn_out, m_g_ple, m_w_ple_gate, m_w_ple_proj, m_g_final, v_g_mix, v_w_in, v_conv_a, v_sgu_ln_g, v_sgu_ln_b, v_sgu_w, v_sgu_b, v_conf_dw, v_conf_ln_g, v_conf_ln_b, v_w_branch, v_w_merge_gate, v_w_out, v_g_ffn, v_w_ffn_in, v_w_ffn_out, v_g_ple, v_w_ple_gate, v_w_ple_proj, v_g_final):
    weights = dict(g_mix=g_mix, w_in=w_in, conv_a=conv_a, sgu_ln_g=sgu_ln_g, sgu_ln_b=sgu_ln_b, sgu_w=sgu_w, sgu_b=sgu_b,
                   conf_dw=conf_dw, conf_ln_g=conf_ln_g, conf_ln_b=conf_ln_b, w_branch=w_branch, w_merge_gate=w_merge_gate,
                   w_out=w_out, g_ffn=g_ffn, w_ffn_in=w_ffn_in, w_ffn_out=w_ffn_out, g_ple=g_ple, w_ple_gate=w_ple_gate,
                   w_ple_proj=w_ple_proj, g_final=g_final)
    m_in = dict(g_mix=m_g_mix, w_in=m_w_in, conv_a=m_conv_a, sgu_ln_g=m_sgu_ln_g, sgu_ln_b=m_sgu_ln_b, sgu_w=m_sgu_w,
                sgu_b=m_sgu_b, conf_dw=m_conf_dw, conf_ln_g=m_conf_ln_g, conf_ln_b=m_conf_ln_b, w_branch=m_w_branch,
                w_merge_gate=m_w_merge_gate, w_out=m_w_out, g_ffn=m_g_ffn, w_ffn_in=m_w_ffn_in, w_ffn_out=m_w_ffn_out,
                g_ple=m_g_ple, w_ple_gate=m_w_ple_gate, w_ple_proj=m_w_ple_proj, g_final=m_g_final)
    v_in = dict(g_mix=v_g_mix, w_in=v_w_in, conv_a=v_conv_a, sgu_ln_g=v_sgu_ln_g, sgu_ln_b=v_sgu_ln_b, sgu_w=v_sgu_w,
                sgu_b=v_sgu_b, conf_dw=v_conf_dw, conf_ln_g=v_conf_ln_g, conf_ln_b=v_conf_ln_b, w_branch=v_w_branch,
                w_merge_gate=v_w_merge_gate, w_out=v_w_out, g_ffn=v_g_ffn, w_ffn_in=v_w_ffn_in, w_ffn_out=v_w_ffn_out,
                g_ple=v_g_ple, w_ple_gate=v_w_ple_gate, w_ple_proj=v_w_ple_proj, g_final=v_g_final)
    order = ("g_mix", "w_in", "conv_a", "sgu_ln_g", "sgu_ln_b", "sgu_w", "sgu_b", "conf_dw", "conf_ln_g", "conf_ln_b",
             "w_branch", "w_merge_gate", "w_out", "g_ffn", "w_ffn_in", "w_ffn_out", "g_ple", "w_ple_gate", "w_ple_proj",
             "g_final")
    depth = g_mix.shape[0]
    xs, tgt = x[0], loss_target[0]
    cw = BW // N_SH
    my_shard = 2 * lax.axis_index("x") + lax.axis_index("y")

    conv_rows = 16
    allc = gather8(_pack_rows([conv_a, conf_dw], conv_rows), reduce=False)
    shards = [_unpack_rows(allc[2 * s], [conv_a.shape, conf_dw.shape]) for s in range(N_SH)]
    conv_a_full = jnp.concatenate([sh[0] for sh in shards], axis=-1)
    conf_dw_full = jnp.concatenate([sh[1] for sh in shards], axis=-1)

    tril = jnp.tril(jnp.ones((BLK, BLK), bool))
    def placed_shards(i):
        shards = ([w_in[i], w_branch[i]] + [w_merge_gate[i, k] for k in range(N_BR)]
                  + [w_out[i], w_ffn_in[i], w_ffn_out[i], w_ple_gate[i], w_ple_proj[i]])
        return [lax.dynamic_update_slice(jnp.zeros((N_SH,) + sh.shape, BF16), sh.astype(BF16)[None],
                                         (my_shard,) + (0,) * sh.ndim) for sh in shards]

    def layer_weights(i, got):
        vec = lambda a: a[i].reshape(1, -1)
        return dict(
            win=got[0], wbr=got[1], wg=jnp.stack([g.reshape(D_MODEL, D_MODEL) for g in got[2:6]]),
            wout=got[6].reshape(D_MODEL, D_MODEL), wfi=got[7], wfo=got[8].reshape(FFN_H, D_MODEL),
            wpg=got[9].reshape(D_MODEL, D_MODEL), wpp=got[10],
            g_mix=vec(g_mix), g_ffn=vec(g_ffn), g_ple=vec(g_ple), conv_a=conv_a_full[i], conf_dw=conf_dw_full[i],
            sgu_ln_g=vec(sgu_ln_g), sgu_ln_b=vec(sgu_ln_b), conf_ln_g=vec(conf_ln_g), conf_ln_b=vec(conf_ln_b),
            sgu_wt=jnp.where(tril[None], sgu_w[i], 0.0).astype(BF16),
            sgu_bf=jnp.repeat(sgu_b[i].T, HEAD_D, axis=1))

    layers = [layer_weights(0, allgather_weights(placed_shards(0)))]
    act, saved = xs, []
    for i in range(depth):
        hooks = (_no_hook,) * 3
        if i + 1 < depth:
            bufs = placed_shards(i + 1)
            shapes = [b.shape for b in bufs]
            ici = SplitExchange("allgather_ici", bufs, allgather_ici_plan(shapes), 3 * len(bufs))
            state = {}

            def ici_wait_d2d_start(arr, ici=ici, shapes=shapes, state=state):
                landed = ici.wait(arr)
                state["d2d"] = SplitExchange("allgather_d2d", landed, allgather_d2d_plan(shapes), 3 * len(landed))
                return state["d2d"].start(landed[-1])

            def d2d_wait(arr, state=state):
                state["got"] = state["d2d"].wait(arr)
                return None

            hooks = (ici.start, ici_wait_d2d_start, d2d_wait)
        act, sv = layer_fwd(act, p[i, 0], layers[i], hooks)
        saved.append(sv)
        if i + 1 < depth:
            layers.append(layer_weights(i + 1, state["got"]))
    loss_part, dx, dg_final = loss_head(act, g_final.reshape(1, -1), tgt)

    big_red = [None] * depth
    small_parts = [None] * depth
    pending = None
    for i in reversed(range(depth)):
        hooks = (_no_hook,) * 4
        result = {}
        if pending is not None:
            rs, j = pending

            def finish(arr, rs=rs, j=j, result=result):
                result[j] = rs.share_wait(arr)
                return None

            hooks = (rs.swap_start, rs.swap_wait_send_start, rs.send_wait_share_start, finish)
        dx, big, small_parts[i] = layer_bwd(dx, p[i, 0], layers[i], saved[i], hooks)
        if pending is not None:
            big_red[pending[1]] = unpack_big_grads(*result[pending[1]])
        pending = (SplitReduceScatter(big), i) if i > 0 else None
        if i == 0:
            big_red[0] = unpack_big_grads(*reduce_scatter_grads(big))

    small_list = [jnp.stack([small_parts[i][n].reshape(weights[n].shape[1:] if n not in ("conv_a", "conf_dw")
                                                       else small_parts[i][n].shape) for i in range(depth)])
                  for n in SMALL_NAMES]
    small_list += [dg_final.reshape(-1), loss_part[0, :1]]
    small_shapes = [a.shape for a in small_list]
    n_small = sum(math.prod(sh) for sh in small_shapes)
    small_rows = -(-n_small // (8 * D_MODEL)) * 8
    red = _unpack_rows(gather8(_pack_rows(small_list, small_rows), reduce=True), small_shapes)
    grads = dict(zip(SMALL_NAMES, red[:len(SMALL_NAMES)]))
    grads["g_final"] = red[-2]
    loss = red[-1].reshape(())
    for n in ("conv_a", "conf_dw"):
        grads[n] = lax.dynamic_slice_in_dim(grads[n], my_shard * cw, cw, axis=2)
    for name in BIG_NAMES:
        grads[name] = jnp.stack([big_red[i][name] for i in range(depth)])

    small_all = [n for n in order if n not in BIG_NAMES]
    sm_shapes = [weights[n].shape for n in small_all]
    n_sm = sum(math.prod(sh) for sh in sm_shapes)
    sm_rows = -(-n_sm // (8 * D_MODEL)) * 8
    packed = [_pack_rows([src[n] for n in small_all], sm_rows) for src in (weights, grads, m_in, v_in)]
    sm_out = [_unpack_rows(o, sm_shapes) for o in adamw(*packed)]
    delta, new_m, new_v = ({n: o[k] for k, n in enumerate(small_all)} for o in sm_out)
    for name in BIG_NAMES:
        delta[name], new_m[name], new_v[name] = adamw(weights[name], grads[name], m_in[name], v_in[name])

    return (loss, dx[None], *[grads[n] for n in order], *[delta[n] for n in order], *[new_m[n] for n in order],
            *[new_v[n] for n in order])
```

```python
import functools
import math

import jax
import jax.numpy as jnp
from jax import lax
from jax.experimental import pallas as pl
from jax.experimental.pallas import tpu as pltpu

F32 = jnp.float32
BF16 = jnp.bfloat16
EPS = 1e-6
D_MODEL = 1024
BW = 256
N_BR = 4
N_IN = 10 * BW
FFN_H = 2816
N_SH = 4
HEADS = 4
HEAD_D = 64
BLK = 128
DILATIONS = (1, 4, 16)
CONF_K = 31
CONVA_K = 3
NEG = -1e30
VMEM_LIMIT = 56 * 1024 * 1024
MESH = pl.DeviceIdType.MESH

ADAM_LR, ADAM_B1, ADAM_B2, ADAM_EPS, ADAM_WD, ADAM_STEP = 0.001, 0.9, 0.999, 1e-08, 0.01, 10

bs = pl.BlockSpec
ANY = pl.BlockSpec(memory_space=pl.ANY)


def _call(body, name, grid, in_specs, out_specs, out_shape, scratch=(), aliases=None, after=None):
    n_in = len(in_specs)
    kernel_body = body
    if after is not None:
        in_specs = list(in_specs) + [ANY]

        def kernel_body(*refs):
            return body(*refs[:n_in], *refs[n_in + 1:])

    call = pl.pallas_call(
        kernel_body, name=name, grid=grid, in_specs=in_specs, out_specs=out_specs, out_shape=out_shape,
        scratch_shapes=list(scratch), input_output_aliases=aliases or {},
        compiler_params=pltpu.CompilerParams(dimension_semantics=("arbitrary",) * len(grid),
                                             vmem_limit_bytes=VMEM_LIMIT))
    return call if after is None else (lambda *args: call(*args, after))


def _sds(shape, dtype):
    return jax.ShapeDtypeStruct(shape, dtype)


def _nn(a, b):
    return jnp.dot(a, b, preferred_element_type=F32)


def _nt(a, b):
    return lax.dot_general(a, b, (((1,), (1,)), ((), ())), preferred_element_type=F32)


def _tn(a, b):
    return lax.dot_general(a, b, (((0,), (0,)), ((), ())), preferred_element_type=F32)


def _sigmoid(x):
    return 1.0 / (1.0 + jnp.exp(-x))


def _rms_fwd(x, g):
    r = lax.rsqrt(jnp.mean(x * x, axis=-1, keepdims=True) + EPS)
    return x * r * g


def _rms_bwd(dh, x, g):
    r = lax.rsqrt(jnp.mean(x * x, axis=-1, keepdims=True) + EPS)
    xr = x * r
    dxr = dh * g
    dx = r * (dxr - xr * jnp.mean(dxr * xr, axis=-1, keepdims=True))
    return dx, dh * xr


def _ln_hat(x):
    mu = jnp.mean(x, axis=-1, keepdims=True)
    xc = x - mu
    r = lax.rsqrt(jnp.mean(xc * xc, axis=-1, keepdims=True) + EPS)
    return xc * r, r


def _ln_bwd(dy, xhat, r, g):
    dxh = dy * g
    return r * (dxh - jnp.mean(dxh, axis=-1, keepdims=True) - xhat * jnp.mean(dxh * xhat, axis=-1, keepdims=True))


def _colsum(v):
    return jnp.sum(v, axis=0, keepdims=True)


def _causal_conv(zext, w_ref, k_taps, halo):
    acc = zext[halo:] * w_ref[k_taps - 1:k_taps, :]
    for k in range(k_taps - 1):
        acc = acc + pltpu.roll(zext, k_taps - 1 - k, 0)[halo:] * w_ref[k:k + 1, :]
    return acc


def _anti_conv(dext, w_ref, k_taps, tm):
    n = dext.shape[0]
    acc = dext[:tm] * w_ref[k_taps - 1:k_taps, :]
    for s in range(1, k_taps):
        acc = acc + pltpu.roll(dext, n - s, 0)[:tm] * w_ref[k_taps - 1 - s:k_taps - s, :]
    return acc


def _conv_wgrad(dw_ref, dc, zext, k_taps, halo):
    dw_ref[k_taps - 1:k_taps, :] += _colsum(dc * zext[halo:])
    for k in range(k_taps - 1):
        dw_ref[k:k + 1, :] += _colsum(dc * pltpu.roll(zext, k_taps - 1 - k, 0)[halo:])


LANES = 128


def _to_strided_view(dst_ref, chunk, scr, d, width):
    n = scr.shape[0] // d
    for c in range(width // LANES):
        scr[...] = chunk(c)
        for r in range(d):
            dst_ref[:, r * width + c * LANES:r * width + (c + 1) * LANES] = scr[pl.ds(r, n, stride=d), :].astype(dst_ref.dtype)


def _from_strided_view(src_ref, scr, d, width, c):
    n = scr.shape[0] // d
    for r in range(d):
        scr[pl.ds(r, n, stride=d), :] = src_ref[:, r * width + c * LANES:r * width + (c + 1) * LANES].astype(F32)
    return scr[...]


def _view_spec(tm, d, width):
    return bs((tm // d, d * width), lambda i: (i, 0))


def _prev_blk(i, per):
    return jnp.maximum(i * per - 1, 0)


def _next_blk(i, per, last):
    return jnp.minimum((i + 1) * per, last)


def norm_in_proj(x, g, win, after=None):
    t = x.shape[0]
    tm = min(512, t)
    ns = win.shape[2]

    def body(x_ref, g_ref, w_ref, h_ref, o_ref, q_ref, q4_ref, q16_ref, scr):
        h = _rms_fwd(x_ref[...], g_ref[...]).astype(BF16)
        h_ref[...] = h
        parts = []
        for s in range(N_SH):
            r = _nn(h, w_ref[s])
            o_ref[:, s * ns:(s + 1) * ns] = r
            if s == 1:
                parts.append(r[:, 3 * BW - ns:])
            if s == 2:
                parts.append(r[:, :6 * BW - 2 * ns])
        qf = jnp.concatenate(parts, axis=1)
        q_ref[...] = qf.astype(BF16)
        chunk = lambda c: qf[:, c * LANES:(c + 1) * LANES]
        _to_strided_view(q4_ref, chunk, scr, 4, 3 * BW)
        _to_strided_view(q16_ref, chunk, scr, 16, 3 * BW)

    row = lambda c: bs((tm, c), lambda i: (i, 0))
    return _call(
        body, "norm_in_proj", (t // tm,), [row(D_MODEL), bs((1, D_MODEL), lambda i: (0, 0)), _resident(win)],
        [row(D_MODEL), row(N_IN), row(3 * BW), _view_spec(tm, 4, 3 * BW), _view_spec(tm, 16, 3 * BW)],
        [_sds((t, D_MODEL), BF16), _sds((t, N_IN), F32), _sds((t, 3 * BW), BF16),
         _sds((t // 4, 4 * 3 * BW), BF16), _sds((t // 16, 16 * 3 * BW), BF16)],
        scratch=[pltpu.VMEM((tm, LANES), F32)], after=after)(x, g, win)


def merge_fwd(h, ys, wg, wbr):
    t = h.shape[0]
    tm = min(512, t)

    def body(h_ref, ya, yb, yc, yd, wg_ref, wb_ref, m_ref, g_ref, b_ref):
        hh = h_ref[...]
        for j in range(N_SH):
            cs = slice(j * BW, (j + 1) * BW)
            acc = None
            for k, y_ref in enumerate((ya, yb, yc, yd)):
                g = _sigmoid(_nn(hh, wg_ref[k, :, cs]))
                b = _nn(y_ref[...], wb_ref[j, k])
                g_ref[k, :, cs] = g.astype(BF16)
                b_ref[k, :, cs] = b.astype(BF16)
                acc = g * b if acc is None else acc + g * b
            m_ref[:, cs] = acc.astype(BF16)

    ysp = bs((tm, BW), lambda i: (i, 0))
    big = bs((N_BR, tm, D_MODEL), lambda i: (0, i, 0))
    return _call(
        body, "merge_fwd", (t // tm,),
        [bs((tm, D_MODEL), lambda i: (i, 0)), ysp, ysp, ysp, ysp, _resident(wg), _resident(wbr)],
        [bs((tm, D_MODEL), lambda i: (i, 0)), big, big],
        [_sds((t, D_MODEL), BF16), _sds((N_BR, t, D_MODEL), BF16), _sds((N_BR, t, D_MODEL), BF16)])(
            h, *ys, wg, wbr)


def mm_residual(a, w, res, name, after=None):
    t, kk = a.shape
    tm = min(512, t)

    def body(a_ref, w_ref, r_ref, o_ref):
        o_ref[...] = r_ref[...] + _nn(a_ref[...], w_ref[...])

    row = bs((tm, D_MODEL), lambda i: (i, 0))
    return _call(body, name, (t // tm,), [bs((tm, kk), lambda i: (i, 0)), _resident(w), row], row,
                 _sds((t, D_MODEL), F32), after=after)(a, w, res)


def ffn_in(x, g, wfi):
    t = x.shape[0]
    tm = min(512, t)
    ns = wfi.shape[2]

    def body(x_ref, g_ref, w_ref, h_ref, f_ref, a_ref):
        h = _rms_fwd(x_ref[...], g_ref[...]).astype(BF16)
        h_ref[...] = h
        for j in range(2):
            cs = slice(j * ns, (j + 1) * ns)
            fg = _nn(h, w_ref[j])
            fu = _nn(h, w_ref[j + 2])
            f_ref[0, :, cs] = fg.astype(BF16)
            f_ref[1, :, cs] = fu.astype(BF16)
            a_ref[:, cs] = (fg * _sigmoid(fg) * fu).astype(BF16)

    row = lambda c: bs((tm, c), lambda i: (i, 0))
    return _call(
        body, "ffn_in", (t // tm,), [row(D_MODEL), bs((1, D_MODEL), lambda i: (0, 0)), _resident(wfi)],
        [row(D_MODEL), bs((2, tm, FFN_H), lambda i: (0, i, 0)), row(FFN_H)],
        [_sds((t, D_MODEL), BF16), _sds((2, t, FFN_H), BF16), _sds((t, FFN_H), BF16)])(x, g, wfi)


def ple_fwd(x, g, wpg, p_i, wpp):
    t = x.shape[0]
    tm = min(512, t)

    def body(x_ref, g_ref, wg_ref, p_ref, wp_ref, h_ref, gt_ref, pp_ref, o_ref):
        xv = x_ref[...]
        h = _rms_fwd(xv, g_ref[...]).astype(BF16)
        h_ref[...] = h
        gate = _sigmoid(_nn(h, wg_ref[...]))
        pb = p_ref[...].astype(BF16)
        pp = jnp.concatenate([_nn(pb, wp_ref[j]) for j in range(N_SH)], axis=1)
        gt_ref[...] = gate.astype(BF16)
        pp_ref[...] = pp.astype(BF16)
        o_ref[...] = xv + gate * pp

    row = bs((tm, D_MODEL), lambda i: (i, 0))
    return _call(
        body, "ple_fwd", (t // tm,),
        [row, bs((1, D_MODEL), lambda i: (0, 0)), _resident(wpg), bs((tm, BW), lambda i: (i, 0)), _resident(wpp)],
        [row, row, row, row],
        [_sds((t, D_MODEL), BF16), _sds((t, D_MODEL), BF16), _sds((t, D_MODEL), BF16), _sds((t, D_MODEL), F32)])(
            x, g, wpg, p_i, wpp)


def loss_head(x, g, tgt):
    t = x.shape[0]
    tm = min(512, t)

    def body(x_ref, g_ref, t_ref, l_ref, dx_ref, dg_ref):
        @pl.when(pl.program_id(0) == 0)
        def _():
            l_ref[...] = jnp.zeros_like(l_ref)
            dg_ref[...] = jnp.zeros_like(dg_ref)

        xv, gv = x_ref[...], g_ref[...]
        err = _rms_fwd(xv, gv) - t_ref[...]
        part = 0.5 * jnp.sum(jnp.mean(err * err, axis=-1, keepdims=True), axis=0, keepdims=True)
        l_ref[...] += jnp.broadcast_to(part, l_ref.shape)
        dx, dgr = _rms_bwd(err * (1.0 / D_MODEL), xv, gv)
        dx_ref[...] = dx
        dg_ref[...] += _colsum(dgr)

    row = bs((tm, D_MODEL), lambda i: (i, 0))
    vec = bs((1, D_MODEL), lambda i: (0, 0))
    return _call(body, "loss_head", (t // tm,), [row, vec, row],
                 [bs((1, 128), lambda i: (0, 0)), row, vec],
                 [_sds((1, 128), F32), _sds((t, D_MODEL), F32), _sds((1, D_MODEL), F32)])(x, g, tgt)


def tn_matmul(name, a, b, grid, a_spec, b_spec, out_spec, out_shape, split=0, split_cols=0, into=None, after=None):
    last = len(grid) - 1

    def body(a_ref, b_ref, *rest):
        o_ref = rest[-1]

        @pl.when(pl.program_id(last) == 0)
        def _():
            o_ref[...] = jnp.zeros_like(o_ref)

        res = _tn(a_ref[...].astype(BF16), b_ref[...].astype(BF16))
        if split_cols:
            cols = res.shape[1] // split_cols
            for s in range(split_cols):
                o_ref[s] += res[:, s * cols:(s + 1) * cols]
        elif split:
            rows = res.shape[0] // split
            for s in range(split):
                o_ref[s] += res[s * rows:(s + 1) * rows]
        else:
            o_ref[...] += res

    if into is None:
        return _call(body, name, grid, [a_spec, b_spec], out_spec, out_shape, after=after)(a, b)
    return _call(body, name, grid, [a_spec, b_spec, ANY], out_spec, out_shape, aliases={2: 0}, after=after)(a, b, into)


def _resident(w):
    zeros = (0,) * w.ndim
    return bs(w.shape, lambda i: zeros, pipeline_mode=pl.Buffered(1))


def norm_bwd(name, sources, dx_in, x, g):
    t = x.shape[0]
    tm = min(512, t)
    n_src = len(sources)

    def body(*refs):
        dxi_ref, x_ref, g_ref, dx_ref, dg_ref = refs[2 * n_src:]

        @pl.when(pl.program_id(0) == 0)
        def _():
            dg_ref[...] = jnp.zeros_like(dg_ref)

        dh = None
        for si in range(n_src):
            for av, wv in sources[si][3](refs[2 * si], refs[2 * si + 1]):
                part = _nt(av, wv)
                dh = part if dh is None else dh + part
        dx, dgr = _rms_bwd(dh, x_ref[...], g_ref[...])
        dx_ref[...] = dxi_ref[...] + dx
        dg_ref[...] += _colsum(dgr)

    in_specs, args = [], []
    for a, a_spec, w, _ in sources:
        in_specs += [a_spec(tm), _resident(w)]
        args += [a, w]
    row = bs((tm, D_MODEL), lambda i: (i, 0))
    vec = bs((1, D_MODEL), lambda i: (0, 0))
    return _call(body, name, (t // tm,), in_specs + [row, row, vec], [row, vec],
                 [_sds((t, D_MODEL), F32), _sds((1, D_MODEL), F32)])(*args, dx_in, x, g)


def ple_bwd_pre(dx, gate, pp, after=None):
    t = dx.shape[0]
    tm = min(1024, t)

    def body(dx_ref, g_ref, p_ref, dpre_ref, dpp_ref):
        d = dx_ref[...]
        g = g_ref[...].astype(F32)
        dpre_ref[...] = (d * p_ref[...].astype(F32) * g * (1.0 - g)).astype(BF16)
        dpp_ref[...] = (d * g).astype(BF16)

    row = bs((tm, D_MODEL), lambda i: (i, 0))
    return _call(body, "ple_bwd_pre", (t // tm,), [row, row, row], [row, row],
                 [_sds((t, D_MODEL), BF16), _sds((t, D_MODEL), BF16)], after=after)(dx, gate, pp)


def ffn_bwd_act(dx, wfo, fgu, after=None):
    t = dx.shape[0]
    tm = min(512, t)
    ns = FFN_H // 2

    def body(dx_ref, w_ref, f_ref, o_ref):
        dxb = dx_ref[...].astype(BF16)
        for j in range(2):
            cs = slice(j * ns, (j + 1) * ns)
            dact = _nt(dxb, w_ref[cs, :])
            fg = f_ref[0, :, cs].astype(F32)
            fu = f_ref[1, :, cs].astype(F32)
            s = _sigmoid(fg)
            o_ref[0, :, cs] = (dact * fu * (s * (1.0 + fg * (1.0 - s)))).astype(BF16)
            o_ref[1, :, cs] = (dact * fg * s).astype(BF16)

    blk = bs((2, tm, FFN_H), lambda i: (0, i, 0))
    return _call(body, "ffn_bwd_act", (t // tm,), [bs((tm, D_MODEL), lambda i: (i, 0)), _resident(wfo), blk],
                 blk, _sds((2, t, FFN_H), BF16), after=after)(dx, wfo, fgu)


def merge_bwd(dx, wout, gates, ybr, wbr):
    t = dx.shape[0]
    tm = min(256, t)

    def body(dx_ref, w_ref, g_ref, b_ref, wb_ref, dpre_ref, dyb_ref, dy_ref):
        dm = _nt(dx_ref[...].astype(BF16), w_ref[...])
        for k in range(N_BR):
            g = g_ref[k].astype(F32)
            dpre_ref[k] = (dm * b_ref[k].astype(F32) * g * (1.0 - g)).astype(BF16)
            dyb = (dm * g).astype(BF16)
            dyb_ref[k] = dyb
            acc = None
            for s in range(N_SH):
                part = _nt(dyb[:, s * BW:(s + 1) * BW], wb_ref[s, k])
                acc = part if acc is None else acc + part
            dy_ref[k] = acc

    blk = bs((N_BR, tm, D_MODEL), lambda i: (0, i, 0))
    return _call(body, "merge_bwd", (t // tm,),
                 [bs((tm, D_MODEL), lambda i: (i, 0)), _resident(wout), blk, blk, _resident(wbr)],
                 [blk, blk, bs((N_BR, tm, BW), lambda i: (0, i, 0))],
                 [_sds((N_BR, t, D_MODEL), BF16), _sds((N_BR, t, D_MODEL), BF16), _sds((N_BR, t, BW), F32)])(
                     dx, wout, gates, ybr, wbr)


def conva_fwd(proj, wa):
    t = proj.shape[0]
    tm, halo = min(512, t), 8
    per = tm // halo

    def body(b_ref, c_ref, x_ref, ch_ref, xh_ref, w_ref, y_ref):
        zh = jnp.where(pl.program_id(0) > 0, ch_ref[...] * xh_ref[...], 0.0)
        zext = jnp.concatenate([zh, c_ref[...] * x_ref[...]], axis=0)
        y_ref[...] = (b_ref[...] * _causal_conv(zext, w_ref, CONVA_K, halo)).astype(BF16)

    col = lambda c: bs((tm, BW), lambda i: (i, c))
    hal = lambda c: bs((halo, BW), lambda i: (_prev_blk(i, per), c))
    return _call(body, "conva_fwd", (t // tm,),
                 [col(0), col(1), col(2), hal(1), hal(2), bs((CONVA_K, BW), lambda i: (0, 0))],
                 bs((tm, BW), lambda i: (i, 0)), _sds((t, BW), BF16))(proj, proj, proj, proj, proj, wa)


def conva_bwd(proj, dys, wa):
    t = proj.shape[0]
    tm, halo = min(512, t), 8
    per = tm // halo
    last = t // halo - 1
    nt = t // tm

    def body(b_ref, c_ref, x_ref, ch_ref, xh_ref, bn_ref, dy_ref, dyn_ref, w_ref, db_ref, dc_ref, dxx_ref, dw_ref):
        i = pl.program_id(0)

        @pl.when(i == 0)
        def _():
            dw_ref[...] = jnp.zeros_like(dw_ref)

        zh = jnp.where(i > 0, ch_ref[...] * xh_ref[...], 0.0)
        cv, xv = c_ref[...], x_ref[...]
        zext = jnp.concatenate([zh, cv * xv], axis=0)
        dy = dy_ref[...]
        dconv = dy * b_ref[...]
        dcn = jnp.where(i < nt - 1, dyn_ref[...] * bn_ref[...], 0.0)
        dz = _anti_conv(jnp.concatenate([dconv, dcn], axis=0), w_ref, CONVA_K, tm)
        db_ref[...] = (dy * _causal_conv(zext, w_ref, CONVA_K, halo)).astype(BF16)
        dc_ref[...] = (dz * xv).astype(BF16)
        dxx_ref[...] = (dz * cv).astype(BF16)
        _conv_wgrad(dw_ref, dconv, zext, CONVA_K, halo)

    col = lambda c: bs((tm, BW), lambda i: (i, c))
    hal = lambda c: bs((halo, BW), lambda i: (_prev_blk(i, per), c))
    nxt = bs((halo, BW), lambda i: (_next_blk(i, per, last), 0))
    wsp = bs((CONVA_K, BW), lambda i: (0, 0))
    outs = _call(body, "conva_bwd", (t // tm,),
                 [col(0), col(1), col(2), hal(1), hal(2), nxt,
                  bs((None, tm, BW), lambda i: (0, i, 0)), bs((None, halo, BW), lambda i: (0, _next_blk(i, per, last), 0)), wsp],
                 [bs((tm, BW), lambda i: (i, 0))] * 3 + [wsp],
                 [_sds((t, BW), BF16)] * 3 + [_sds((CONVA_K, BW), F32)])(proj, proj, proj, proj, proj, proj, dys, dys, wa)
    return outs[:3], outs[3]


def _head_masks():
    lane = lax.broadcasted_iota(jnp.int32, (1, BW), 1)
    return [(lane >= h * HEAD_D) & (lane < (h + 1) * HEAD_D) for h in range(HEADS)]


def _band_masks():
    qi = lax.broadcasted_iota(jnp.int32, (BLK, BLK), 0)
    ki = lax.broadcasted_iota(jnp.int32, (BLK, BLK), 1)
    return ki >= qi, ki <= qi


def attn_fwd_group(pv, d):
    rows = pv.shape[0]
    qb = min(512, rows)
    nb = qb // BLK
    scale = HEAD_D ** -0.5

    def body(q_ref, k_ref, v_ref, kh_ref, vh_ref, o_ref):
        n = pl.program_id(1)
        hm = _head_masks()
        m_prev, m_cur = _band_masks()
        for b in range(nb):
            rs = slice(b * BLK, (b + 1) * BLK)
            q = q_ref[rs, :]
            if b == 0:
                kp, vp = kh_ref[...], vh_ref[...]
                mp = m_prev & (n > 0)
            else:
                ps = slice((b - 1) * BLK, b * BLK)
                kp, vp = k_ref[ps, :], v_ref[ps, :]
                mp = m_prev
            qs = jnp.concatenate([jnp.where(hm[h], q, 0.0).astype(BF16) for h in range(HEADS)], axis=0)
            kcat = jnp.concatenate([kp, k_ref[rs, :]], axis=0)
            vcat = jnp.concatenate([vp, v_ref[rs, :]], axis=0)
            band = jnp.concatenate([mp, m_cur], axis=1)
            s = jnp.where(jnp.concatenate([band] * HEADS, axis=0), _nt(qs, kcat) * scale, NEG)
            m = jnp.max(s, axis=-1, keepdims=True)
            e = jnp.exp(s - m)
            l = jnp.sum(e, axis=-1, keepdims=True)
            of = _nn(e.astype(BF16), vcat) / l
            lse = m + jnp.log(l)
            o_acc = jnp.zeros((BLK, BW), F32)
            l_acc = jnp.zeros((BLK, BW), F32)
            for h in range(HEADS):
                hs = slice(h * BLK, (h + 1) * BLK)
                o_acc = jnp.where(hm[h], of[hs, :], o_acc)
                l_acc = jnp.where(hm[h], lse[hs, :], l_acc)
            o_ref[rs, :BW] = o_acc
            o_ref[rs, BW:] = l_acc

    per = qb // BLK
    main = lambda c: bs((qb, BW), lambda r, n: (n, r * 3 + c))
    hal = lambda c: bs((BLK, BW), lambda r, n: (_prev_blk(n, per), r * 3 + c))
    return _call(body, f"attn_fwd_d{d}", (d, rows // qb), [main(0), main(1), main(2), hal(1), hal(2)],
                 bs((qb, 2 * BW), lambda r, n: (n, r)), _sds((rows, d * 2 * BW), F32))(pv, pv, pv, pv, pv)


def attn_merge(ols):
    t = ols[0].shape[0]
    tm = min(512, t)
    width = 2 * BW

    def lse3(a, b, c):
        m = jnp.maximum(jnp.maximum(a, b), c)
        return m + jnp.log(jnp.exp(a - m) + jnp.exp(b - m) + jnp.exp(c - m))

    def body(g0, g1, g2, y_ref, o_ref, l_ref, scr, nat1, nat2):
        for src, nat, d in ((g1, nat1, DILATIONS[1]), (g2, nat2, DILATIONS[2])):
            for c in range(width // LANES):
                nat[:, c * LANES:(c + 1) * LANES] = _from_strided_view(src, scr, d, width, c)
        gs = [g0[...], nat1[...], nat2[...]]
        ls = [g[:, BW:] for g in gs]
        tot = lse3(*ls)
        o = (jnp.exp(ls[0] - tot) * gs[0][:, :BW] + jnp.exp(ls[1] - tot) * gs[1][:, :BW]
             + jnp.exp(ls[2] - tot) * gs[2][:, :BW])
        y_ref[...] = o.astype(BF16)
        o_ref[...] = o
        l_ref[...] = tot

    n = bs((tm, BW), lambda i: (i, 0))
    return _call(body, "attn_merge", (t // tm,),
                 [_view_spec(tm, 1, width), _view_spec(tm, DILATIONS[1], width), _view_spec(tm, DILATIONS[2], width)],
                 [n, n, n], [_sds((t, BW), BF16), _sds((t, BW), F32), _sds((t, BW), F32)],
                 scratch=[pltpu.VMEM((tm, LANES), F32), pltpu.VMEM((tm, width), F32), pltpu.VMEM((tm, width), F32)])(*ols)


def attn_delta(dys, o, lse):
    t = o.shape[0]
    tm = min(512, t)

    def body(d_ref, o_ref, l_ref, ld1, ld4, ld16, dy4, dy16, scr):
        hm = _head_masks()
        dy = d_ref[...]
        prod = dy * o_ref[...]
        delta = jnp.zeros_like(prod)
        for h in range(HEADS):
            delta = jnp.where(hm[h], jnp.sum(jnp.where(hm[h], prod, 0.0), axis=-1, keepdims=True), delta)
        ld = jnp.concatenate([l_ref[...], delta], axis=1)
        ld1[...] = ld
        for d, ld_v, dy_v in ((DILATIONS[1], ld4, dy4), (DILATIONS[2], ld16, dy16)):
            _to_strided_view(ld_v, lambda c: ld[:, c * LANES:(c + 1) * LANES], scr, d, 2 * BW)
            _to_strided_view(dy_v, lambda c: dy[:, c * LANES:(c + 1) * LANES], scr, d, BW)

    n = bs((tm, BW), lambda i: (i, 0))
    d4, d16 = DILATIONS[1], DILATIONS[2]
    outs = _call(body, "attn_delta", (t // tm,), [bs((None, tm, BW), lambda i: (1, i, 0)), n, n],
                 [_view_spec(tm, 1, 2 * BW), _view_spec(tm, d4, 2 * BW), _view_spec(tm, d16, 2 * BW),
                  _view_spec(tm, d4, BW), _view_spec(tm, d16, BW)],
                 [_sds((t, 2 * BW), F32), _sds((t // d4, d4 * 2 * BW), F32), _sds((t // d16, d16 * 2 * BW), F32),
                  _sds((t // d4, d4 * BW), F32), _sds((t // d16, d16 * BW), F32)],
                 scratch=[pltpu.VMEM((tm, LANES), F32)])(dys, o, lse)
    return outs[:3], outs[3:]


def attn_bwd_group(pv, dov, ldv, d):
    rows = pv.shape[0]
    qb = min(512, rows)
    nb = qb // BLK
    nsteps = rows // qb
    scale = HEAD_D ** -0.5

    def body(q_ref, qn_ref, k_ref, kh_ref, v_ref, vh_ref, do_ref, don_ref, ld_ref, ldn_ref, o_ref):
        n = pl.program_id(1)
        hm = _head_masks()
        m_prev, m_cur = _band_masks()
        has_prev, has_next = n > 0, n < nsteps - 1
        dq = [None] * nb
        dk = [jnp.zeros((BLK, BW), F32) for _ in range(nb)]
        dvv = [jnp.zeros((BLK, BW), F32) for _ in range(nb)]
        for qi in range(nb + 1):
            rs = slice(qi * BLK, (qi + 1) * BLK)
            ps = slice((qi - 1) * BLK, qi * BLK)
            if qi < nb:
                q, do, ldq = q_ref[rs, :], do_ref[rs, :], ld_ref[rs, :]
            else:
                q, do, ldq = qn_ref[...], don_ref[...], ldn_ref[...]
            kp, vp = (kh_ref[...], vh_ref[...]) if qi == 0 else (k_ref[ps, :], v_ref[ps, :])
            kc, vc = (k_ref[rs, :], v_ref[rs, :]) if qi < nb else (kp, vp)
            mp = m_prev & has_prev if qi == 0 else (m_prev & has_next if qi == nb else m_prev)
            mc = m_cur if qi < nb else jnp.zeros_like(m_cur)
            band = jnp.concatenate([jnp.concatenate([mp, mc], axis=1)] * HEADS, axis=0)
            qs = jnp.concatenate([jnp.where(hm[h], q, 0.0).astype(BF16) for h in range(HEADS)], axis=0)
            dos = jnp.concatenate([jnp.where(hm[h], do, 0.0).astype(BF16) for h in range(HEADS)], axis=0)
            kcat = jnp.concatenate([kp, kc], axis=0)
            vcat = jnp.concatenate([vp, vc], axis=0)
            col = lambda v, h: jnp.broadcast_to(jnp.max(jnp.where(hm[h], v, NEG), axis=-1, keepdims=True), (BLK, 2 * BLK))
            lcols = jnp.concatenate([col(ldq[:, :BW], h) for h in range(HEADS)], axis=0)
            dcols = jnp.concatenate([col(ldq[:, BW:], h) for h in range(HEADS)], axis=0)
            p = jnp.where(band, jnp.exp(_nt(qs, kcat) * scale - lcols), 0.0)
            ds = (p * (_nt(dos, vcat) - dcols) * scale).astype(BF16)
            if qi < nb:
                dqf = _nn(ds, kcat)
                acc_q = jnp.zeros((BLK, BW), F32)
                for h in range(HEADS):
                    acc_q = jnp.where(hm[h], dqf[h * BLK:(h + 1) * BLK, :], acc_q)
                dq[qi] = acc_q
            dkc = _tn(ds, qs)
            dvc = _tn(p.astype(BF16), dos)
            if qi >= 1:
                dk[qi - 1] = dk[qi - 1] + dkc[:BLK]
                dvv[qi - 1] = dvv[qi - 1] + dvc[:BLK]
            if qi < nb:
                dk[qi] = dk[qi] + dkc[BLK:]
                dvv[qi] = dvv[qi] + dvc[BLK:]
        for b in range(nb):
            rs = slice(b * BLK, (b + 1) * BLK)
            for c, val in enumerate((dq[b], dk[b], dvv[b])):
                cs = slice(c * BW, (c + 1) * BW)
                o_ref[rs, cs] = val

    per = qb // BLK
    last = rows // BLK - 1
    main = lambda c: bs((qb, BW), lambda r, n: (n, r * 3 + c))
    prv = lambda c: bs((BLK, BW), lambda r, n: (_prev_blk(n, per), r * 3 + c))
    nxt = lambda c: bs((BLK, BW), lambda r, n: (_next_blk(n, per, last), r * 3 + c))
    accs = bs((qb, 3 * BW), lambda r, n: (n, r))
    in_specs = [main(0), nxt(0), main(1), prv(1), main(2), prv(2),
                bs((qb, BW), lambda r, n: (n, r)), bs((BLK, BW), lambda r, n: (_next_blk(n, per, last), r)),
                bs((qb, 2 * BW), lambda r, n: (n, r)), bs((BLK, 2 * BW), lambda r, n: (_next_blk(n, per, last), r))]
    args = [pv, pv, pv, pv, pv, pv, dov, dov, ldv, ldv]
    return _call(body, f"attn_bwd_d{d}", (d, nsteps), in_specs, accs, _sds((rows, d * 3 * BW), F32))(*args)


def attn_bwd_finish(parts):
    t = parts[0].shape[0]
    tm = min(512, t)
    width = 3 * BW

    def body(g0, g1, g2, o_ref, scr):
        for c in range(width // LANES):
            cs = slice(c * LANES, (c + 1) * LANES)
            acc = g0[:, cs]
            acc = acc + _from_strided_view(g1, scr, DILATIONS[1], width, c)
            acc = acc + _from_strided_view(g2, scr, DILATIONS[2], width, c)
            o_ref[:, cs] = acc.astype(BF16)

    return _call(body, "attn_bwd_finish", (t // tm,),
                 [_view_spec(tm, 1, width), _view_spec(tm, DILATIONS[1], width), _view_spec(tm, DILATIONS[2], width)],
                 bs((tm, width), lambda i: (i, 0)), _sds((t, width), BF16),
                 scratch=[pltpu.VMEM((tm, LANES), F32)])(*parts)


def _group_masks():
    lane = lax.broadcasted_iota(jnp.int32, (1, BW), 1)
    return [(lane >= g * HEAD_D) & (lane < (g + 1) * HEAD_D) for g in range(4)]


def sgu_fwd(proj, ln_g, ln_b, w_tril, b_full):
    t = proj.shape[0]
    tm = min(512, t)

    def body(u_ref, v_ref, g_ref, b_ref, w_ref, bf_ref, y_ref):
        gm = _group_masks()
        xhat, _ = _ln_hat(v_ref[...])
        vb = (xhat * g_ref[...] + b_ref[...]).astype(BF16)
        for c in range(tm // BLK):
            rs = slice(c * BLK, (c + 1) * BLK)
            vc = vb[rs, :]
            mixed = bf_ref[...]
            for g in range(4):
                mixed = mixed + jnp.where(gm[g], _nn(w_ref[g], vc), 0.0)
            y_ref[rs, :] = (u_ref[rs, :] * mixed).astype(BF16)

    vec = bs((1, BW), lambda i: (0, 0))
    return _call(body, "sgu_fwd", (t // tm,),
                 [bs((tm, BW), lambda i: (i, 6)), bs((tm, BW), lambda i: (i, 7)), vec, vec,
                  bs((4, BLK, BLK), lambda i: (0, 0, 0)), bs((BLK, BW), lambda i: (0, 0))],
                 bs((tm, BW), lambda i: (i, 0)), _sds((t, BW), BF16))(proj, proj, ln_g, ln_b, w_tril, b_full)


def sgu_bwd(proj, dys, ln_g, ln_b, w_tril, b_full):
    t = proj.shape[0]
    tm = min(512, t)

    def body(u_ref, v_ref, dy_ref, g_ref, b_ref, w_ref, bf_ref, du_ref, dv_ref, dw_ref, dbf_ref, dg_ref, db_ref, dvl_ref):
        @pl.when(pl.program_id(0) == 0)
        def _():
            dw_ref[...] = jnp.zeros_like(dw_ref)
            dbf_ref[...] = jnp.zeros_like(dbf_ref)
            dg_ref[...] = jnp.zeros_like(dg_ref)
            db_ref[...] = jnp.zeros_like(db_ref)

        gm = _group_masks()
        xhat, r = _ln_hat(v_ref[...])
        gv = g_ref[...]
        vb = (xhat * gv + b_ref[...]).astype(BF16)
        for c in range(tm // BLK):
            rs = slice(c * BLK, (c + 1) * BLK)
            vc = vb[rs, :]
            dy = dy_ref[rs, :]
            mixed = bf_ref[...]
            for g in range(4):
                mixed = mixed + jnp.where(gm[g], _nn(w_ref[g], vc), 0.0)
            du_ref[rs, :] = (dy * mixed).astype(BF16)
            dm = dy * u_ref[rs, :]
            dbf_ref[...] += dm
            dvl = jnp.zeros((BLK, BW), F32)
            for g in range(4):
                dmg = jnp.where(gm[g], dm, 0.0).astype(BF16)
                dw_ref[g] += _nt(dmg, vc)
                dvl = dvl + _tn(w_ref[g], dmg)
            dvl_ref[rs, :] = dvl
        dvl = dvl_ref[...]
        dv_ref[...] = _ln_bwd(dvl, xhat, r, gv).astype(BF16)
        dg_ref[...] += _colsum(dvl * xhat)
        db_ref[...] += _colsum(dvl)

    vec = bs((1, BW), lambda i: (0, 0))
    row = bs((tm, BW), lambda i: (i, 0))
    wsp = bs((4, BLK, BLK), lambda i: (0, 0, 0))
    bfs = bs((BLK, BW), lambda i: (0, 0))
    return _call(body, "sgu_bwd", (t // tm,),
                 [bs((tm, BW), lambda i: (i, 6)), bs((tm, BW), lambda i: (i, 7)), bs((None, tm, BW), lambda i: (2, i, 0)),
                  vec, vec, wsp, bfs],
                 [row, row, wsp, bfs, vec, vec],
                 [_sds((t, BW), BF16), _sds((t, BW), BF16), _sds((4, BLK, BLK), F32), _sds((BLK, BW), F32),
                  _sds((1, BW), F32), _sds((1, BW), F32)],
                 scratch=[pltpu.VMEM((tm, BW), F32)])(proj, proj, dys, ln_g, ln_b, w_tril, b_full)


CONF_HALO = 32


def conf_fwd(proj, dw, ln_g, ln_b):
    t = proj.shape[0]
    tm, halo = min(512, t), CONF_HALO
    per = tm // halo

    def body(v_ref, gt_ref, vh_ref, gh_ref, w_ref, g_ref, b_ref, y_ref, z_ref):
        yh = jnp.where(pl.program_id(0) > 0, vh_ref[...] * _sigmoid(gh_ref[...]), 0.0)
        yext = jnp.concatenate([yh, v_ref[...] * _sigmoid(gt_ref[...])], axis=0)
        z = _causal_conv(yext, w_ref, CONF_K, halo)
        z_ref[...] = z
        xhat, _ = _ln_hat(z)
        ln = xhat * g_ref[...] + b_ref[...]
        y_ref[...] = (ln * _sigmoid(ln)).astype(BF16)

    vec = bs((1, BW), lambda i: (0, 0))
    col = lambda c: bs((tm, BW), lambda i: (i, c))
    hal = lambda c: bs((halo, BW), lambda i: (_prev_blk(i, per), c))
    row = bs((tm, BW), lambda i: (i, 0))
    return _call(body, "conf_fwd", (t // tm,),
                 [col(8), col(9), hal(8), hal(9), bs((CONF_K, BW), lambda i: (0, 0)), vec, vec],
                 [row, row], [_sds((t, BW), BF16), _sds((t, BW), F32)])(proj, proj, proj, proj, dw, ln_g, ln_b)


def conf_bwd_ln(z, dys, ln_g, ln_b):
    t = z.shape[0]
    tm = min(1024, t)

    def body(z_ref, dy_ref, g_ref, b_ref, dz_ref, dg_ref, db_ref):
        @pl.when(pl.program_id(0) == 0)
        def _():
            dg_ref[...] = jnp.zeros_like(dg_ref)
            db_ref[...] = jnp.zeros_like(db_ref)

        gv = g_ref[...]
        xhat, r = _ln_hat(z_ref[...])
        ln = xhat * gv + b_ref[...]
        s = _sigmoid(ln)
        dln = dy_ref[...] * (s * (1.0 + ln * (1.0 - s)))
        dz_ref[...] = _ln_bwd(dln, xhat, r, gv)
        dg_ref[...] += _colsum(dln * xhat)
        db_ref[...] += _colsum(dln)

    vec = bs((1, BW), lambda i: (0, 0))
    row = bs((tm, BW), lambda i: (i, 0))
    return _call(body, "conf_bwd_ln", (t // tm,), [row, bs((None, tm, BW), lambda i: (3, i, 0)), vec, vec],
                 [row, vec, vec], [_sds((t, BW), F32), _sds((1, BW), F32), _sds((1, BW), F32)])(z, dys, ln_g, ln_b)


def conf_bwd_conv(proj, dz, dw):
    t = proj.shape[0]
    tm, halo = min(512, t), CONF_HALO
    per = tm // halo
    last = t // halo - 1
    nt = t // tm

    def body(v_ref, gt_ref, vh_ref, gh_ref, dz_ref, dzn_ref, w_ref, dv_ref, dg_ref, dw_ref):
        i = pl.program_id(0)

        @pl.when(i == 0)
        def _():
            dw_ref[...] = jnp.zeros_like(dw_ref)

        val = v_ref[...]
        sg = _sigmoid(gt_ref[...])
        yh = jnp.where(i > 0, vh_ref[...] * _sigmoid(gh_ref[...]), 0.0)
        yext = jnp.concatenate([yh, val * sg], axis=0)
        dz = dz_ref[...]
        dzn = jnp.where(i < nt - 1, dzn_ref[...], 0.0)
        dy0 = _anti_conv(jnp.concatenate([dz, dzn], axis=0), w_ref, CONF_K, tm)
        dv_ref[...] = (dy0 * sg).astype(BF16)
        dg_ref[...] = (dy0 * val * sg * (1.0 - sg)).astype(BF16)
        _conv_wgrad(dw_ref, dz, yext, CONF_K, halo)

    col = lambda c: bs((tm, BW), lambda i: (i, c))
    hal = lambda c: bs((halo, BW), lambda i: (_prev_blk(i, per), c))
    row = bs((tm, BW), lambda i: (i, 0))
    wsp = bs((CONF_K, BW), lambda i: (0, 0))
    return _call(body, "conf_bwd_conv", (t // tm,),
                 [col(8), col(9), hal(8), hal(9), row, bs((halo, BW), lambda i: (_next_blk(i, per, last), 0)), wsp],
                 [row, row, wsp], [_sds((t, BW), BF16), _sds((t, BW), BF16), _sds((CONF_K, BW), F32)])(
                     proj, proj, proj, proj, dz, dz, dw)


def _place():
    return lax.axis_index("x"), lax.axis_index("y"), lax.axis_index("c")


def _comm_call(body, name, n_in, out_shape, scratch, aliases=None):
    return pl.pallas_call(body, name=name, in_specs=[ANY] * n_in, out_specs=[ANY] * len(out_shape), out_shape=out_shape,
                          scratch_shapes=scratch, input_output_aliases=aliases or {},
                          compiler_params=pltpu.CompilerParams(has_side_effects=True, vmem_limit_bytes=VMEM_LIMIT))


HBM_SPEC = pl.BlockSpec(memory_space=pltpu.HBM)
SEM_SPEC = pl.BlockSpec(memory_space=pltpu.SEMAPHORE)
EFFECT = pltpu.SideEffectType.DATAFLOW_SIDE_EFFECTING


class SplitExchange:
    def __init__(self, name, bufs, plan, n_copies):
        self.name, self.bufs, self.plan, self.n = name, list(bufs), plan, n_copies

    def start(self, after):
        nb, n, plan = len(self.bufs), self.n, self.plan

        def body(*refs):
            send, recv, token = refs[nb + 1], refs[nb + 2], refs[-1]
            for k, (src, dst, _, dev) in enumerate(plan(refs[:nb])):
                pltpu.make_async_remote_copy(src_ref=src, dst_ref=dst, send_sem=send.at[k], recv_sem=recv.at[k],
                                             device_id=dev, device_id_type=MESH).start()
            token[...] = jnp.zeros_like(token)

        outs = pl.pallas_call(
            body, name=self.name + "_start",
            out_shape=(pltpu.SemaphoreType.DMA((n,)), pltpu.SemaphoreType.DMA((n,)),
                       *[pltpu.HBM(b.shape, b.dtype) for b in self.bufs], _sds((8, 128), F32)),
            in_specs=[HBM_SPEC] * nb + [ANY],
            out_specs=(SEM_SPEC, SEM_SPEC, *[HBM_SPEC] * nb, pl.BlockSpec(memory_space=pltpu.VMEM)),
            input_output_aliases={i: 2 + i for i in range(nb)},
            compiler_params=pltpu.CompilerParams(has_side_effects=EFFECT))(
                *[pltpu.with_memory_space_constraint(b, pltpu.HBM) for b in self.bufs], after)
        self.send, self.recv, self.bufs = outs[0], outs[1], list(outs[2:2 + nb])
        return outs[-1]

    def wait(self, after):
        nb, plan = len(self.bufs), self.plan

        def body(*refs):
            send, recv = refs[nb], refs[nb + 1]
            for k, (src, _, land, dev) in enumerate(plan(refs[:nb])):
                cp = pltpu.make_async_remote_copy(src_ref=src, dst_ref=land, send_sem=send.at[k], recv_sem=recv.at[k],
                                                  device_id=dev, device_id_type=MESH)
                cp.wait_send()
                cp.wait_recv()

        outs = pl.pallas_call(
            body, name=self.name + "_wait", out_shape=tuple(pltpu.HBM(b.shape, b.dtype) for b in self.bufs),
            in_specs=[HBM_SPEC] * nb + [SEM_SPEC, SEM_SPEC, ANY], out_specs=[HBM_SPEC] * nb,
            input_output_aliases={i: i for i in range(nb)},
            compiler_params=pltpu.CompilerParams(has_side_effects=EFFECT))(*self.bufs, self.send, self.recv, after)
        return list(outs)


def _chips_of(x, y):
    return [(1 - x, y), (x, 1 - y), (1 - x, 1 - y)]


def allgather_ici_plan(shapes):
    def plan(refs):
        x, y, c = _place()
        out = []
        for a, ref in enumerate(refs):
            hl = shapes[a][1] // 2
            half = pl.ds(c * hl, hl)
            for cx, cy in _chips_of(x, y):
                mine = ref.at[2 * x + y, half]
                out.append((mine, mine, ref.at[2 * cx + cy, half], (cx, cy, c)))
        return out
    return plan


def allgather_d2d_plan(shapes):
    def plan(refs):
        x, y, c = _place()
        out = []
        for a, ref in enumerate(refs):
            hl = shapes[a][1] // 2
            for cx, cy in _chips_of(x, y):
                got = ref.at[2 * cx + cy, pl.ds(c * hl, hl)]
                out.append((got, got, ref.at[2 * cx + cy, pl.ds((1 - c) * hl, hl)], (x, y, 1 - c)))
        return out
    return plan


def gather8(v, reduce):
    rows, cols = v.shape

    def body(v_ref, o_ref, land_ref, send, recv, lsem):
        x, y, c = _place()
        me = 4 * x + 2 * y + c
        land = land_ref if reduce else o_ref
        mine = pltpu.make_async_copy(v_ref, land.at[me], lsem)
        mine.start()
        sent = []
        for j in range(1, 8):
            fx, fy, fc = (j >> 2) & 1, (j >> 1) & 1, j & 1
            tgt = (1 - x if fx else x, 1 - y if fy else y, 1 - c if fc else c)
            cp = pltpu.make_async_remote_copy(src_ref=v_ref, dst_ref=land.at[me], send_sem=send.at[j - 1],
                                              recv_sem=recv.at[j - 1], device_id=tgt, device_id_type=MESH)
            cp.start()
            sent.append(cp)
        for j in range(1, 8):
            fx, fy, fc = (j >> 2) & 1, (j >> 1) & 1, j & 1
            peer = 4 * (1 - x if fx else x) + 2 * (1 - y if fy else y) + (1 - c if fc else c)
            pltpu.make_async_remote_copy(src_ref=v_ref, dst_ref=land.at[peer], send_sem=send.at[j - 1],
                                         recv_sem=recv.at[j - 1], device_id=(x, y, c), device_id_type=MESH).wait_recv()
        for cp in sent:
            cp.wait_send()
        mine.wait()
        if reduce:
            acc = land_ref[0]
            for k in range(1, 8):
                acc = acc + land_ref[k]
            o_ref[...] = acc

    vm = pl.BlockSpec(memory_space=pltpu.VMEM)
    out_shape = _sds((rows, cols), F32) if reduce else _sds((8, rows, cols), F32)
    land_shape = (8, rows, cols) if reduce else (8, 128)
    return pl.pallas_call(
        body, name="allreduce8" if reduce else "allgather8", in_specs=[vm], out_specs=vm, out_shape=out_shape,
        scratch_shapes=[pltpu.VMEM(land_shape, F32), pltpu.SemaphoreType.DMA((7,)), pltpu.SemaphoreType.DMA((7,)),
                        pltpu.SemaphoreType.DMA],
        compiler_params=pltpu.CompilerParams(has_side_effects=True, vmem_limit_bytes=VMEM_LIMIT))(v)


def allgather_weights(bufs):
    n = len(bufs)

    def body(*refs):
        ins, outs = refs[:n], refs[n:2 * n]
        send, recv = refs[2 * n:]
        x, y, c = _place()
        s_me = 2 * x + y
        chips = [(1 - x, y), (x, 1 - y), (1 - x, 1 - y)]
        sibling = (x, y, 1 - c)
        started = []
        for a in range(n):
            hl = bufs[a].shape[1] // 2
            half = pl.ds(c * hl, hl)
            for j, chip in enumerate(chips):
                cp = pltpu.make_async_remote_copy(src_ref=ins[a].at[s_me, half], dst_ref=outs[a].at[s_me, half],
                                                  send_sem=send.at[6 * a + j], recv_sem=recv.at[6 * a + j],
                                                  device_id=(chip[0], chip[1], c), device_id_type=MESH)
                cp.start()
                started.append(cp)
        for a in range(n):
            hl = bufs[a].shape[1] // 2
            half = pl.ds(c * hl, hl)
            for j, chip in enumerate(chips):
                s_j = 2 * chip[0] + chip[1]
                landed = outs[a].at[s_j, half]
                pltpu.make_async_remote_copy(src_ref=landed, dst_ref=landed, send_sem=send.at[6 * a + j],
                                             recv_sem=recv.at[6 * a + j], device_id=sibling, device_id_type=MESH).wait_recv()
                fw = pltpu.make_async_remote_copy(src_ref=landed, dst_ref=landed, send_sem=send.at[6 * a + 3 + j],
                                                  recv_sem=recv.at[6 * a + 3 + j], device_id=sibling, device_id_type=MESH)
                fw.start()
                started.append(fw)
        for a in range(n):
            hl = bufs[a].shape[1] // 2
            other = pl.ds((1 - c) * hl, hl)
            for j, chip in enumerate(chips):
                s_j = 2 * chip[0] + chip[1]
                theirs = outs[a].at[s_j, other]
                pltpu.make_async_remote_copy(src_ref=theirs, dst_ref=theirs, send_sem=send.at[6 * a + 3 + j],
                                             recv_sem=recv.at[6 * a + 3 + j], device_id=sibling, device_id_type=MESH).wait_recv()
        for cp in started:
            cp.wait_send()

    out_shape = [_sds(b.shape, b.dtype) for b in bufs]
    scratch = [pltpu.SemaphoreType.DMA((6 * n,)), pltpu.SemaphoreType.DMA((6 * n,))]
    return _comm_call(body, "allgather_weights", n, out_shape, scratch, aliases={a: a for a in range(n)})(*bufs)


def _row_tile(rows, cols):
    best = 16
    for t in range(16, rows + 1, 16):
        if rows % t == 0 and t * cols * 4 <= 2 * 1024 * 1024:
            best = t
    return best


def _rs_add_sibling(scal, g, ra, hr):
    cols = g.shape[2]
    tr = _row_tile(hr, cols)
    nr = hr // tr

    def body(s_ref, g_ref, r_ref, p32_ref, p16_ref):
        v = g_ref[...] + r_ref[...]
        p32_ref[...] = v
        p16_ref[...] = v.astype(BF16)

    blk = lambda f: bs((None, tr, cols), f)
    own = blk(lambda s, i, sr: (s, i, 0))
    spec = pltpu.PrefetchScalarGridSpec(num_scalar_prefetch=1, grid=(N_SH, nr),
                                        in_specs=[blk(lambda s, i, sr: (s, sr[1] * nr + i, 0)), own],
                                        out_specs=[own, own])
    return pl.pallas_call(body, name="rs_add_sibling", grid_spec=spec,
                          out_shape=[_sds((N_SH, hr, cols), F32), _sds((N_SH, hr, cols), BF16)],
                          compiler_params=pltpu.CompilerParams(dimension_semantics=("arbitrary",) * 2,
                                                               vmem_limit_bytes=VMEM_LIMIT))(scal, g, ra)


def _rs_add_chips(scal, p32, rb, hr):
    cols = p32.shape[2]
    tr = _row_tile(hr, cols)
    nr = hr // tr

    def body(s_ref, p_ref, r0, r1, r2, o_ref):
        o_ref[...] = ((p_ref[...] + r0[...].astype(F32)) + r1[...].astype(F32)) + r2[...].astype(F32)

    blk = lambda f: bs((None, tr, cols), f)
    spec = pltpu.PrefetchScalarGridSpec(
        num_scalar_prefetch=1, grid=(nr,),
        in_specs=[blk(lambda i, sr: (sr[0], i, 0))] + [blk(functools.partial(lambda i, sr, j: (j, i, 0), j=j))
                                                        for j in range(3)],
        out_specs=blk(lambda i, sr: (sr[1], i, 0)))
    return pl.pallas_call(body, name="rs_add_chips", grid_spec=spec, out_shape=_sds((2, hr, cols), F32),
                          compiler_params=pltpu.CompilerParams(dimension_semantics=("arbitrary",),
                                                               vmem_limit_bytes=VMEM_LIMIT))(scal, p32, rb, rb, rb)


class SplitReduceScatter:
    def __init__(self, gs):
        x, y, c = _place()
        self.scal = jnp.stack([2 * x + y, c]).astype(jnp.int32)
        self.gs, self.n = list(gs), len(gs)
        self.hrs = [g.shape[1] // 2 for g in gs]

    def swap_start(self, after):
        n, hrs = self.n, self.hrs

        def plan(refs):
            x, y, c = _place()
            return [(refs[a].at[:, pl.ds((1 - c) * hrs[a], hrs[a])], refs[n + a], refs[n + a], (x, y, 1 - c))
                    for a in range(n)]

        lands = [lax.empty((N_SH, hrs[a], g.shape[2]), F32) for a, g in enumerate(self.gs)]
        self.ex = SplitExchange("rs_swap_halves", self.gs + lands, plan, n)
        return self.ex.start(after)

    def swap_wait_send_start(self, after):
        n, hrs = self.n, self.hrs
        bufs = self.ex.wait(after)
        parts = [_rs_add_sibling(self.scal, bufs[a], bufs[n + a], hrs[a]) for a in range(n)]
        self.p32 = [p[0] for p in parts]

        def plan(refs):
            x, y, c = _place()
            return [(refs[a].at[2 * cx + cy], refs[n + a].at[j], refs[n + a].at[j], (cx, cy, c))
                    for a in range(n) for j, (cx, cy) in enumerate(_chips_of(x, y))]

        lands = [lax.empty((3, hrs[a], g.shape[2]), BF16) for a, g in enumerate(self.gs)]
        self.ex = SplitExchange("rs_send_partials", [p[1] for p in parts] + lands, plan, 3 * n)
        return self.ex.start(parts[-1][1])

    def send_wait_share_start(self, after):
        n, hrs = self.n, self.hrs
        bufs = self.ex.wait(after)
        fins = [_rs_add_chips(self.scal, self.p32[a], bufs[n + a], hrs[a]) for a in range(n)]

        def plan(refs):
            x, y, c = _place()
            return [(refs[a].at[c], refs[a].at[c], refs[a].at[1 - c], (x, y, 1 - c)) for a in range(n)]

        self.ex = SplitExchange("rs_share_halves", fins, plan, n)
        return self.ex.start(fins[-1])

    def share_wait(self, after):
        fulls = self.ex.wait(after)
        return [f.reshape(2 * hr, f.shape[2]) for f, hr in zip(fulls, self.hrs)]


def reduce_scatter_grads(gs):
    n = len(gs)
    x, y, c = _place()
    scal = jnp.stack([2 * x + y, c]).astype(jnp.int32)
    hrs = [g.shape[1] // 2 for g in gs]

    def swap_body(*refs):
        ins, outs = refs[:n], refs[n:2 * n]
        send, recv = refs[2 * n:]
        xx, yy, cc = _place()
        cps = []
        for a in range(n):
            cp = pltpu.make_async_remote_copy(src_ref=ins[a].at[:, pl.ds((1 - cc) * hrs[a], hrs[a])], dst_ref=outs[a],
                                              send_sem=send.at[a], recv_sem=recv.at[a],
                                              device_id=(xx, yy, 1 - cc), device_id_type=MESH)
            cp.start()
            cps.append(cp)
        for cp in cps:
            cp.wait()

    ras = _comm_call(swap_body, "rs_swap_halves", n, [_sds((N_SH, hrs[a], gs[a].shape[2]), F32) for a in range(n)],
                     [pltpu.SemaphoreType.DMA((n,)), pltpu.SemaphoreType.DMA((n,))])(*gs)

    parts = [_rs_add_sibling(scal, gs[a], ras[a], hrs[a]) for a in range(n)]

    def ici_body(*refs):
        ins, outs = refs[:n], refs[n:2 * n]
        send, recv = refs[2 * n:]
        xx, yy, cc = _place()
        chips = [(1 - xx, yy), (xx, 1 - yy), (1 - xx, 1 - yy)]
        cps = []
        for a in range(n):
            for j, chip in enumerate(chips):
                cp = pltpu.make_async_remote_copy(src_ref=ins[a].at[2 * chip[0] + chip[1]], dst_ref=outs[a].at[j],
                                                  send_sem=send.at[3 * a + j], recv_sem=recv.at[3 * a + j],
                                                  device_id=(chip[0], chip[1], cc), device_id_type=MESH)
                cp.start()
                cps.append(cp)
        for cp in cps:
            cp.wait()

    rbs = _comm_call(ici_body, "rs_send_partials", n, [_sds((3, hrs[a], gs[a].shape[2]), BF16) for a in range(n)],
                     [pltpu.SemaphoreType.DMA((3 * n,)), pltpu.SemaphoreType.DMA((3 * n,))])(*[p[1] for p in parts])

    fins = [_rs_add_chips(scal, parts[a][0], rbs[a], hrs[a]) for a in range(n)]

    def share_body(*refs):
        ins, outs = refs[:n], refs[n:2 * n]
        send, recv = refs[2 * n:]
        xx, yy, cc = _place()
        sib = (xx, yy, 1 - cc)
        cps = []
        for a in range(n):
            cp = pltpu.make_async_remote_copy(src_ref=ins[a].at[cc], dst_ref=outs[a].at[cc], send_sem=send.at[a],
                                              recv_sem=recv.at[a], device_id=sib, device_id_type=MESH)
            cp.start()
            cps.append(cp)
        for a in range(n):
            pltpu.make_async_remote_copy(src_ref=ins[a].at[cc], dst_ref=outs[a].at[1 - cc], send_sem=send.at[a],
                                         recv_sem=recv.at[a], device_id=sib, device_id_type=MESH).wait_recv()
        for cp in cps:
            cp.wait_send()

    fulls = _comm_call(share_body, "rs_share_halves", n, [_sds(f.shape, F32) for f in fins],
                       [pltpu.SemaphoreType.DMA((n,)), pltpu.SemaphoreType.DMA((n,))],
                       aliases={a: a for a in range(n)})(*fins)
    return [f.reshape(2 * hr, f.shape[2]) for f, hr in zip(fulls, hrs)]


def adamw(w, g, m, v):
    shape = w.shape
    cols = shape[-1]
    rows = math.prod(shape[:-1]) if len(shape) > 1 else 1
    tr = 256 if rows % 256 == 0 and rows > 256 else rows
    c1 = 1.0 - ADAM_B1 ** ADAM_STEP
    c2 = 1.0 - ADAM_B2 ** ADAM_STEP

    def body(w_ref, g_ref, m_ref, v_ref, d_ref, nm_ref, nv_ref):
        gv = g_ref[...]
        nm = ADAM_B1 * m_ref[...] + (1.0 - ADAM_B1) * gv
        nv = ADAM_B2 * v_ref[...] + (1.0 - ADAM_B2) * (gv * gv)
        nm_ref[...] = nm
        nv_ref[...] = nv
        d_ref[...] = -ADAM_LR * ((nm / c1) / (jnp.sqrt(nv / c2) + ADAM_EPS) + ADAM_WD * w_ref[...])

    row = bs((tr, cols), lambda i: (i, 0))
    outs = _call(body, "adamw", (rows // tr,), [row] * 4, [row] * 3, [_sds((rows, cols), F32)] * 3)(
        *[a.reshape(rows, cols) for a in (w, g, m, v)])
    return [o.reshape(shape) for o in outs]


def _no_hook(_):
    return None


def layer_fwd(x, p_i, w, hooks=(_no_hook,) * 3):
    h, proj, *qkv = norm_in_proj(x, w["g_mix"], w["win"], after=hooks[0](x))
    ya = conva_fwd(proj, w["conv_a"])
    yb, o32, lse = attn_merge([attn_fwd_group(pv, d) for pv, d in zip(qkv, DILATIONS)])
    yc = sgu_fwd(proj, w["sgu_ln_g"], w["sgu_ln_b"], w["sgu_wt"], w["sgu_bf"])
    yd, z = conf_fwd(proj, w["conf_dw"], w["conf_ln_g"], w["conf_ln_b"])
    ys = (ya, yb, yc, yd)
    merged, gates, ybr = merge_fwd(h, ys, w["wg"], w["wbr"])
    x1 = mm_residual(merged, w["wout"], x, "attn_out", after=hooks[1](merged))
    h2, fgu, act = ffn_in(x1, w["g_ffn"], w["wfi"])
    x2 = mm_residual(act, w["wfo"], x1, "ffn_out", after=hooks[2](act))
    h3, gate, pp, x3 = ple_fwd(x2, w["g_ple"], w["wpg"], p_i, w["wpp"])
    saved = dict(x=x, h=h, proj=proj, qkv=qkv, ys=ys, o32=o32, lse=lse, z=z, merged=merged, gates=gates, ybr=ybr, x1=x1,
                 h2=h2, fgu=fgu, act=act, x2=x2, h3=h3, gate=gate, pp=pp)
    return x3, saved


def layer_bwd(dx3, p_i, w, s, hooks=(_no_hook,) * 4):
    t = dx3.shape[0]
    tr = min(1024, t)
    nr = t // tr
    ns_fi = FFN_H // 2
    small = {}

    dpre, dpp = ple_bwd_pre(dx3, s["gate"], s["pp"], after=hooks[0](dx3))
    ga_shape, gb_shape = _sds((N_SH, 6 * BW, D_MODEL), F32), _sds((N_SH, 5 * BW, BW), F32)
    ga_blk = lambda idx: bs((N_SH, BW, D_MODEL), idx)
    ga = tn_matmul("dw_ple_gate", s["h3"], dpre, (nr,), bs((tr, D_MODEL), lambda r: (r, 0)),
                   bs((tr, D_MODEL), lambda r: (r, 0)), ga_blk(lambda r: (0, 5, 0)), ga_shape, split=N_SH)
    gb = tn_matmul("dw_ple_proj", p_i, dpp, (N_SH, nr), bs((tr, BW), lambda j, r: (r, 0)),
                   bs((tr, BW), lambda j, r: (r, j)), bs((None, BW, BW), lambda j, r: (j, 4, 0)), gb_shape)
    dx2, small["g_ple"] = norm_bwd(
        "ple_norm_bwd",
        [(dpre, lambda tm: bs((tm, D_MODEL), lambda i: (i, 0)), w["wpg"], lambda a, wr: [(a[...], wr[...])])],
        dx3, s["x2"], w["g_ple"])

    df = ffn_bwd_act(dx2, w["wfo"], s["fgu"], after=hooks[1](dx2))
    gfo = tn_matmul("dw_ffn_out", s["act"], dx2, (2, nr), bs((tr, ns_fi), lambda j, r: (r, j)),
                    bs((tr, D_MODEL), lambda j, r: (r, 0)), bs((2, FFN_H // N_SH, D_MODEL), lambda j, r: (j, 0, 0)),
                    _sds((N_SH, FFN_H // N_SH, D_MODEL), F32), split=2)
    gfi = tn_matmul("dw_ffn_in", s["h2"], df, (N_SH, nr), bs((tr, D_MODEL), lambda j, r: (r, 0)),
                    bs((None, tr, ns_fi), lambda j, r: (j // 2, r, j % 2)),
                    bs((None, D_MODEL, ns_fi), lambda j, r: (j, 0, 0)), _sds((N_SH, D_MODEL, ns_fi), F32))
    dx1, small["g_ffn"] = norm_bwd(
        "ffn_norm_bwd",
        [(df, lambda tm: bs((2, tm, FFN_H), lambda i: (0, i, 0)), w["wfi"],
          lambda a, wr: [(a[k // 2, :, (k % 2) * ns_fi:(k % 2 + 1) * ns_fi], wr[k]) for k in range(N_SH)])],
        dx2, s["x1"], w["g_ffn"])

    dpre_m, dyb, dys = merge_bwd(dx1, w["wout"], s["gates"], s["ybr"], w["wbr"])
    ga = tn_matmul("dw_out", s["merged"], dx1, (nr,), bs((tr, D_MODEL), lambda r: (r, 0)),
                   bs((tr, D_MODEL), lambda r: (r, 0)), ga_blk(lambda r: (0, 4, 0)), ga_shape, split=N_SH, into=ga,
                   after=hooks[2](dyb))
    ga = tn_matmul("dw_merge_gate", s["h"], dpre_m, (N_BR, nr), bs((tr, D_MODEL), lambda k, r: (r, 0)),
                   bs((None, tr, D_MODEL), lambda k, r: (k, r, 0)), ga_blk(lambda k, r: (0, k, 0)), ga_shape,
                   split=N_SH, into=ga)
    for k in range(N_BR):
        gb = tn_matmul("dw_branch", s["ys"][k], dyb, (nr,), bs((tr, BW), lambda r: (r, 0)),
                       bs((None, tr, D_MODEL), functools.partial(lambda r, kk: (kk, r, 0), kk=k)),
                       bs((N_SH, BW, BW), functools.partial(lambda r, kk: (0, kk, 0), kk=k)), gb_shape,
                       split_cols=N_SH, into=gb)
    hooks[3](gb)

    (dab, dac, dax), small["conv_a"] = conva_bwd(s["proj"], dys, w["conv_a"])
    lds, dy_views = attn_delta(dys, s["o32"], s["lse"])
    dy_views = [dys[1]] + list(dy_views)
    dqkv = attn_bwd_finish([attn_bwd_group(pv, dov, ldv, d)
                            for pv, dov, ldv, d in zip(s["qkv"], dy_views, lds, DILATIONS)])
    du, dv, d_sw, d_sbf, small["sgu_ln_g"], small["sgu_ln_b"] = sgu_bwd(
        s["proj"], dys, w["sgu_ln_g"], w["sgu_ln_b"], w["sgu_wt"], w["sgu_bf"])
    small["sgu_w"] = jnp.where(jnp.tril(jnp.ones((BLK, BLK), bool))[None], d_sw, 0.0)
    small["sgu_b"] = jnp.sum(d_sbf.reshape(BLK, 4, HEAD_D), axis=-1).T
    dz, small["conf_ln_g"], small["conf_ln_b"] = conf_bwd_ln(s["z"], dys, w["conf_ln_g"], w["conf_ln_b"])
    dval, dgate, small["conf_dw"] = conf_bwd_conv(s["proj"], dz, w["conf_dw"])
    dproj = jnp.concatenate([dab, dac, dax, dqkv, du, dv, dval, dgate], axis=1)

    ns_in = N_IN // N_SH
    gin = tn_matmul("dw_in", s["h"], dproj, (N_SH, nr), bs((tr, D_MODEL), lambda j, r: (r, 0)),
                    bs((tr, ns_in), lambda j, r: (r, j)), bs((None, D_MODEL, ns_in), lambda j, r: (j, 0, 0)),
                    _sds((N_SH, D_MODEL, ns_in), F32))
    big = [ga, gfo, gb, gin, gfi]
    dx, small["g_mix"] = norm_bwd(
        "mix_norm_bwd",
        [(dpre_m, lambda tm: bs((N_BR, tm, D_MODEL), lambda i: (0, i, 0)), w["wg"],
          lambda a, wr: [(a[k], wr[k]) for k in range(N_BR)]),
         (dproj, lambda tm: bs((tm, N_IN), lambda i: (i, 0)), w["win"],
          lambda a, wr: [(a[:, k * ns_in:(k + 1) * ns_in], wr[k]) for k in range(N_SH)])],
        dx1, s["x"], w["g_mix"])
    return dx, big, small


BIG_NAMES = ("w_in", "w_branch", "w_merge_gate", "w_out", "w_ffn_in", "w_ffn_out", "w_ple_gate", "w_ple_proj")


def unpack_big_grads(ga, gfo, gb, gin, gfi):
    return dict(w_in=gin, w_ffn_in=gfi, w_ffn_out=gfo,
                w_merge_gate=ga[:N_BR * BW].reshape(N_BR, BW, D_MODEL), w_out=ga[N_BR * BW:5 * BW], w_ple_gate=ga[5 * BW:],
                w_branch=gb[:N_BR * BW].reshape(N_BR, BW, BW), w_ple_proj=gb[N_BR * BW:])


SMALL_NAMES = ("g_mix", "conv_a", "sgu_ln_g", "sgu_ln_b", "sgu_w", "sgu_b", "conf_dw", "conf_ln_g", "conf_ln_b",
               "g_ffn", "g_ple")


def _pack_rows(arrays, rows):
    flat = jnp.concatenate([a.reshape(-1) for a in arrays])
    return jnp.pad(flat, (0, rows * D_MODEL - flat.shape[0])).reshape(rows, D_MODEL)


def _unpack_rows(packed, shapes):
    flat, out, pos = packed.reshape(-1), [], 0
    for shape in shapes:
        n = math.prod(shape)
        out.append(flat[pos:pos + n].reshape(shape))
        pos += n
    return out


def kernel(x, p, g_mix, w_in, conv_a, sgu_ln_g, sgu_ln_b, sgu_w, sgu_b, conf_dw, conf_ln_g, conf_ln_b, w_branch, w_merge_gate, w_out, g_ffn, w_ffn_in, w_ffn_out, g_ple, w_ple_gate, w_ple_proj, g_final, loss_target, m_g_mix, m_w_in, m_conv_a, m_sgu_ln_g, m_sgu_ln_b, m_sgu_w, m_sgu_b, m_conf_dw, m_conf_ln_g, m_conf_ln_b, m_w_branch, m_w_merge_gate, m_w_out, m_g_ffn, m_w_ffn_in, m_w_ffn_out, m_g_ple, m_w_ple_gate, m_w_ple_proj, m_g_final, v_g_mix, v_w_in, v_conv_a, v_sgu_ln_g, v_sgu_ln_b, v_sgu_w, v_sgu_b, v_conf_dw, v_conf_ln_g, v_conf_ln_b, v_w_branch, v_w_merge_gate, v_w_out, v_g_ffn, v_w_ffn_in, v_w_ffn_out, v_g_ple, v_w_ple_gate, v_w_ple_proj, v_g_final):
    weights = dict(g_mix=g_mix, w_in=w_in, conv_a=conv_a, sgu_ln_g=sgu_ln_g, sgu_ln_b=sgu_ln_b, sgu_w=sgu_w, sgu_b=sgu_b,
                   conf_dw=conf_dw, conf_ln_g=conf_ln_g, conf_ln_b=conf_ln_b, w_branch=w_branch, w_merge_gate=w_merge_gate,
                   w_out=w_out, g_ffn=g_ffn, w_ffn_in=w_ffn_in, w_ffn_out=w_ffn_out, g_ple=g_ple, w_ple_gate=w_ple_gate,
                   w_ple_proj=w_ple_proj, g_final=g_final)
    m_in = dict(g_mix=m_g_mix, w_in=m_w_in, conv_a=m_conv_a, sgu_ln_g=m_sgu_ln_g, sgu_ln_b=m_sgu_ln_b, sgu_w=m_sgu_w,
                sgu_b=m_sgu_b, conf_dw=m_conf_dw, conf_ln_g=m_conf_ln_g, conf_ln_b=m_conf_ln_b, w_branch=m_w_branch,
                w_merge_gate=m_w_merge_gate, w_out=m_w_out, g_ffn=m_g_ffn, w_ffn_in=m_w_ffn_in, w_ffn_out=m_w_ffn_out,
                g_ple=m_g_ple, w_ple_gate=m_w_ple_gate, w_ple_proj=m_w_ple_proj, g_final=m_g_final)
    v_in = dict(g_mix=v_g_mix, w_in=v_w_in, conv_a=v_conv_a, sgu_ln_g=v_sgu_ln_g, sgu_ln_b=v_sgu_ln_b, sgu_w=v_sgu_w,
                sgu_b=v_sgu_b, conf_dw=v_conf_dw, conf_ln_g=v_conf_ln_g, conf_ln_b=v_conf_ln_b, w_branch=v_w_branch,
                w_merge_gate=v_w_merge_gate, w_out=v_w_out, g_ffn=v_g_ffn, w_ffn_in=v_w_ffn_in, w_ffn_out=v_w_ffn_out,
                g_ple=v_g_ple, w_ple_gate=v_w_ple_gate, w_ple_proj=v_w_ple_proj, g_final=v_g_final)
    order = ("g_mix", "w_in", "conv_a", "sgu_ln_g", "sgu_ln_b", "sgu_w", "sgu_b", "conf_dw", "conf_ln_g", "conf_ln_b",
             "w_branch", "w_merge_gate", "w_out", "g_ffn", "w_ffn_in", "w_ffn_out", "g_ple", "w_ple_gate", "w_ple_proj",
             "g_final")
    depth = g_mix.shape[0]
    xs, tgt = x[0], loss_target[0]
    cw = BW // N_SH
    my_shard = 2 * lax.axis_index("x") + lax.axis_index("y")

    conv_rows = 16
    allc = gather8(_pack_rows([conv_a, conf_dw], conv_rows), reduce=False)
    shards = [_unpack_rows(allc[2 * s], [conv_a.shape, conf_dw.shape]) for s in range(N_SH)]
    conv_a_full = jnp.concatenate([sh[0] for sh in shards], axis=-1)
    conf_dw_full = jnp.concatenate([sh[1] for sh in shards], axis=-1)

    tril = jnp.tril(jnp.ones((BLK, BLK), bool))
    def placed_shards(i):
        shards = ([w_in[i], w_branch[i]] + [w_merge_gate[i, k] for k in range(N_BR)]
                  + [w_out[i], w_ffn_in[i], w_ffn_out[i], w_ple_gate[i], w_ple_proj[i]])
        return [lax.dynamic_update_slice(jnp.zeros((N_SH,) + sh.shape, BF16), sh.astype(BF16)[None],
                                         (my_shard,) + (0,) * sh.ndim) for sh in shards]

    def layer_weights(i, got):
        vec = lambda a: a[i].reshape(1, -1)
        return dict(
            win=got[0], wbr=got[1], wg=jnp.stack([g.reshape(D_MODEL, D_MODEL) for g in got[2:6]]),
            wout=got[6].reshape(D_MODEL, D_MODEL), wfi=got[7], wfo=got[8].reshape(FFN_H, D_MODEL),
            wpg=got[9].reshape(D_MODEL, D_MODEL), wpp=got[10],
            g_mix=vec(g_mix), g_ffn=vec(g_ffn), g_ple=vec(g_ple), conv_a=conv_a_full[i], conf_dw=conf_dw_full[i],
            sgu_ln_g=vec(sgu_ln_g), sgu_ln_b=vec(sgu_ln_b), conf_ln_g=vec(conf_ln_g), conf_ln_b=vec(conf_ln_b),
            sgu_wt=jnp.where(tril[None], sgu_w[i], 0.0).astype(BF16),
            sgu_bf=jnp.repeat(sgu_b[i].T, HEAD_D, axis=1))

    layers = [layer_weights(0, allgather_weights(placed_shards(0)))]
    act, saved = xs, []
    for i in range(depth):
        hooks = (_no_hook,) * 3
        if i + 1 < depth:
            bufs = placed_shards(i + 1)
            shapes = [b.shape for b in bufs]
            ici = SplitExchange("allgather_ici", bufs, allgather_ici_plan(shapes), 3 * len(bufs))
            state = {}

            def ici_wait_d2d_start(arr, ici=ici, shapes=shapes, state=state):
                landed = ici.wait(arr)
                state["d2d"] = SplitExchange("allgather_d2d", landed, allgather_d2d_plan(shapes), 3 * len(landed))
                return state["d2d"].start(landed[-1])

            def d2d_wait(arr, state=state):
                state["got"] = state["d2d"].wait(arr)
                return None

            hooks = (ici.start, ici_wait_d2d_start, d2d_wait)
        act, sv = layer_fwd(act, p[i, 0], layers[i], hooks)
        saved.append(sv)
        if i + 1 < depth:
            layers.append(layer_weights(i + 1, state["got"]))
    loss_part, dx, dg_final = loss_head(act, g_final.reshape(1, -1), tgt)

    big_red = [None] * depth
    small_parts = [None] * depth
    pending = None
    for i in reversed(range(depth)):
        hooks = (_no_hook,) * 4
        result = {}
        if pending is not None:
            rs, j = pending

            def finish(arr, rs=rs, j=j, result=result):
                result[j] = rs.share_wait(arr)
                return None

            hooks = (rs.swap_start, rs.swap_wait_send_start, rs.send_wait_share_start, finish)
        dx, big, small_parts[i] = layer_bwd(dx, p[i, 0], layers[i], saved[i], hooks)
        if pending is not None:
            big_red[pending[1]] = unpack_big_grads(*result[pending[1]])
        pending = (SplitReduceScatter(big), i) if i > 0 else None
        if i == 0:
            big_red[0] = unpack_big_grads(*reduce_scatter_grads(big))

    small_list = [jnp.stack([small_parts[i][n].reshape(weights[n].shape[1:] if n not in ("conv_a", "conf_dw")
                                                       else small_parts[i][n].shape) for i in range(depth)])
                  for n in SMALL_NAMES]
    small_list += [dg_final.reshape(-1), loss_part[0, :1]]
    small_shapes = [a.shape for a in small_list]
    n_small = sum(math.prod(sh) for sh in small_shapes)
    small_rows = -(-n_small // (8 * D_MODEL)) * 8
    red = _unpack_rows(gather8(_pack_rows(small_list, small_rows), reduce=True), small_shapes)
    grads = dict(zip(SMALL_NAMES, red[:len(SMALL_NAMES)]))
    grads["g_final"] = red[-2]
    loss = red[-1].reshape(())
    for n in ("conv_a", "conf_dw"):
        grads[n] = lax.dynamic_slice_in_dim(grads[n], my_shard * cw, cw, axis=2)
    for name in BIG_NAMES:
        grads[name] = jnp.stack([big_red[i][name] for i in range(depth)])

    small_all = [n for n in order if n not in BIG_NAMES]
    sm_shapes = [weights[n].shape for n in small_all]
    n_sm = sum(math.prod(sh) for sh in sm_shapes)
    sm_rows = -(-n_sm // (8 * D_MODEL)) * 8
    packed = [_pack_rows([src[n] for n in small_all], sm_rows) for src in (weights, grads, m_in, v_in)]
    sm_out = [_unpack_rows(o, sm_shapes) for o in adamw(*packed)]
    delta, new_m, new_v = ({n: o[k] for k, n in enumerate(small_all)} for o in sm_out)
    for name in BIG_NAMES:
        delta[name], new_m[name], new_v[name] = adamw(weights[name], grads[name], m_in[name], v_in[name])

    return (loss, dx[None], *[grads[n] for n in order], *[delta[n] for n in order], *[new_m[n] for n in order],
            *[new_v[n] for n in order])
```

```python
import functools
import math

import jax
import jax.numpy as jnp
from jax import lax
from jax.experimental import pallas as pl
from jax.experimental.pallas import tpu as pltpu

F32 = jnp.float32
BF16 = jnp.bfloat16
EPS = 1e-6
D_MODEL = 1024
BW = 256
N_BR = 4
N_IN = 10 * BW
FFN_H = 2816
N_SH = 4
HEADS = 4
HEAD_D = 64
BLK = 128
DILATIONS = (1, 4, 16)
CONF_K = 31
CONVA_K = 3
NEG = -1e30
VMEM_LIMIT = 56 * 1024 * 1024
MESH = pl.DeviceIdType.MESH

ADAM_LR, ADAM_B1, ADAM_B2, ADAM_EPS, ADAM_WD, ADAM_STEP = 0.001, 0.9, 0.999, 1e-08, 0.01, 10

bs = pl.BlockSpec
ANY = pl.BlockSpec(memory_space=pl.ANY)


def _call(body, name, grid, in_specs, out_specs, out_shape, scratch=(), aliases=None, after=None):
    n_in = len(in_specs)
    kernel_body = body
    if after is not None:
        in_specs = list(in_specs) + [ANY]

        def kernel_body(*refs):
            return body(*refs[:n_in], *refs[n_in + 1:])

    call = pl.pallas_call(
        kernel_body, name=name, grid=grid, in_specs=in_specs, out_specs=out_specs, out_shape=out_shape,
        scratch_shapes=list(scratch), input_output_aliases=aliases or {},
        compiler_params=pltpu.CompilerParams(dimension_semantics=("arbitrary",) * len(grid),
                                             vmem_limit_bytes=VMEM_LIMIT))
    return call if after is None else (lambda *args: call(*args, after))


def _sds(shape, dtype):
    return jax.ShapeDtypeStruct(shape, dtype)


def _nn(a, b):
    return jnp.dot(a, b, preferred_element_type=F32)


def _nt(a, b):
    return lax.dot_general(a, b, (((1,), (1,)), ((), ())), preferred_element_type=F32)


def _tn(a, b):
    return lax.dot_general(a, b, (((0,), (0,)), ((), ())), preferred_element_type=F32)


def _sigmoid(x):
    return 1.0 / (1.0 + jnp.exp(-x))


def _rms_fwd(x, g):
    r = lax.rsqrt(jnp.mean(x * x, axis=-1, keepdims=True) + EPS)
    return x * r * g


def _rms_bwd(dh, x, g):
    r = lax.rsqrt(jnp.mean(x * x, axis=-1, keepdims=True) + EPS)
    xr = x * r
    dxr = dh * g
    dx = r * (dxr - xr * jnp.mean(dxr * xr, axis=-1, keepdims=True))
    return dx, dh * xr


def _ln_hat(x):
    mu = jnp.mean(x, axis=-1, keepdims=True)
    xc = x - mu
    r = lax.rsqrt(jnp.mean(xc * xc, axis=-1, keepdims=True) + EPS)
    return xc * r, r


def _ln_bwd(dy, xhat, r, g):
    dxh = dy * g
    return r * (dxh - jnp.mean(dxh, axis=-1, keepdims=True) - xhat * jnp.mean(dxh * xhat, axis=-1, keepdims=True))


def _colsum(v):
    return jnp.sum(v, axis=0, keepdims=True)


def _causal_conv(zext, w_ref, k_taps, halo):
    acc = zext[halo:] * w_ref[k_taps - 1:k_taps, :]
    for k in range(k_taps - 1):
        acc = acc + pltpu.roll(zext, k_taps - 1 - k, 0)[halo:] * w_ref[k:k + 1, :]
    return acc


def _anti_conv(dext, w_ref, k_taps, tm):
    n = dext.shape[0]
    acc = dext[:tm] * w_ref[k_taps - 1:k_taps, :]
    for s in range(1, k_taps):
        acc = acc + pltpu.roll(dext, n - s, 0)[:tm] * w_ref[k_taps - 1 - s:k_taps - s, :]
    return acc


def _conv_wgrad(dw_ref, dc, zext, k_taps, halo):
    dw_ref[k_taps - 1:k_taps, :] += _colsum(dc * zext[halo:])
    for k in range(k_taps - 1):
        dw_ref[k:k + 1, :] += _colsum(dc * pltpu.roll(zext, k_taps - 1 - k, 0)[halo:])


LANES = 128


def _to_strided_view(dst_ref, chunk, scr, d, width):
    n = scr.shape[0] // d
    for c in range(width // LANES):
        scr[...] = chunk(c)
        for r in range(d):
            dst_ref[:, r * width + c * LANES:r * width + (c + 1) * LANES] = scr[pl.ds(r, n, stride=d), :].astype(dst_ref.dtype)


def _from_strided_view(src_ref, scr, d, width, c):
    n = scr.shape[0] // d
    for r in range(d):
        scr[pl.ds(r, n, stride=d), :] = src_ref[:, r * width + c * LANES:r * width + (c + 1) * LANES].astype(F32)
    return scr[...]


def _view_spec(tm, d, width):
    return bs((tm // d, d * width), lambda i: (i, 0))


def _prev_blk(i, per):
    return jnp.maximum(i * per - 1, 0)


def _next_blk(i, per, last):
    return jnp.minimum((i + 1) * per, last)


def norm_in_proj(x, g, win, after=None):
    t = x.shape[0]
    tm = min(512, t)
    ns = win.shape[2]

    def body(x_ref, g_ref, w_ref, h_ref, o_ref, q_ref, q4_ref, q16_ref, scr):
        h = _rms_fwd(x_ref[...], g_ref[...]).astype(BF16)
        h_ref[...] = h
        parts = []
        for s in range(N_SH):
            r = _nn(h, w_ref[s])
            o_ref[:, s * ns:(s + 1) * ns] = r
            if s == 1:
                parts.append(r[:, 3 * BW - ns:])
            if s == 2:
                parts.append(r[:, :6 * BW - 2 * ns])
        qf = jnp.concatenate(parts, axis=1)
        q_ref[...] = qf.astype(BF16)
        chunk = lambda c: qf[:, c * LANES:(c + 1) * LANES]
        _to_strided_view(q4_ref, chunk, scr, 4, 3 * BW)
        _to_strided_view(q16_ref, chunk, scr, 16, 3 * BW)

    row = lambda c: bs((tm, c), lambda i: (i, 0))
    return _call(
        body, "norm_in_proj", (t // tm,), [row(D_MODEL), bs((1, D_MODEL), lambda i: (0, 0)), _resident(win)],
        [row(D_MODEL), row(N_IN), row(3 * BW), _view_spec(tm, 4, 3 * BW), _view_spec(tm, 16, 3 * BW)],
        [_sds((t, D_MODEL), BF16), _sds((t, N_IN), F32), _sds((t, 3 * BW), BF16),
         _sds((t // 4, 4 * 3 * BW), BF16), _sds((t // 16, 16 * 3 * BW), BF16)],
        scratch=[pltpu.VMEM((tm, LANES), F32)], after=after)(x, g, win)


def merge_fwd(h, ys, wg, wbr, after=None):
    t = h.shape[0]
    tm = min(512, t)

    def body(h_ref, ya, yb, yc, yd, wg_ref, wb_ref, m_ref, g_ref, b_ref):
        hh = h_ref[...]
        for j in range(N_SH):
            cs = slice(j * BW, (j + 1) * BW)
            acc = None
            for k, y_ref in enumerate((ya, yb, yc, yd)):
                g = _sigmoid(_nn(hh, wg_ref[k, :, cs]))
                b = _nn(y_ref[...], wb_ref[j, k])
                g_ref[k, :, cs] = g.astype(BF16)
                b_ref[k, :, cs] = b.astype(BF16)
                acc = g * b if acc is None else acc + g * b
            m_ref[:, cs] = acc.astype(BF16)

    ysp = bs((tm, BW), lambda i: (i, 0))
    big = bs((N_BR, tm, D_MODEL), lambda i: (0, i, 0))
    return _call(
        body, "merge_fwd", (t // tm,),
        [bs((tm, D_MODEL), lambda i: (i, 0)), ysp, ysp, ysp, ysp, _resident(wg), _resident(wbr)],
        [bs((tm, D_MODEL), lambda i: (i, 0)), big, big],
        [_sds((t, D_MODEL), BF16), _sds((N_BR, t, D_MODEL), BF16), _sds((N_BR, t, D_MODEL), BF16)], after=after)(
            h, *ys, wg, wbr)


def mm_residual(a, w, res, name, after=None):
    t, kk = a.shape
    tm = min(512, t)

    def body(a_ref, w_ref, r_ref, o_ref):
        o_ref[...] = r_ref[...] + _nn(a_ref[...], w_ref[...])

    row = bs((tm, D_MODEL), lambda i: (i, 0))
    return _call(body, name, (t // tm,), [bs((tm, kk), lambda i: (i, 0)), _resident(w), row], row,
                 _sds((t, D_MODEL), F32), after=after)(a, w, res)


def ffn_in(x, g, wfi):
    t = x.shape[0]
    tm = min(512, t)
    ns = wfi.shape[2]

    def body(x_ref, g_ref, w_ref, h_ref, f_ref, a_ref):
        h = _rms_fwd(x_ref[...], g_ref[...]).astype(BF16)
        h_ref[...] = h
        for j in range(2):
            cs = slice(j * ns, (j + 1) * ns)
            fg = _nn(h, w_ref[j])
            fu = _nn(h, w_ref[j + 2])
            f_ref[0, :, cs] = fg.astype(BF16)
            f_ref[1, :, cs] = fu.astype(BF16)
            a_ref[:, cs] = (fg * _sigmoid(fg) * fu).astype(BF16)

    row = lambda c: bs((tm, c), lambda i: (i, 0))
    return _call(
        body, "ffn_in", (t // tm,), [row(D_MODEL), bs((1, D_MODEL), lambda i: (0, 0)), _resident(wfi)],
        [row(D_MODEL), bs((2, tm, FFN_H), lambda i: (0, i, 0)), row(FFN_H)],
        [_sds((t, D_MODEL), BF16), _sds((2, t, FFN_H), BF16), _sds((t, FFN_H), BF16)])(x, g, wfi)


def ple_fwd(x, g, wpg, p_i, wpp):
    t = x.shape[0]
    tm = min(512, t)

    def body(x_ref, g_ref, wg_ref, p_ref, wp_ref, h_ref, gt_ref, pp_ref, o_ref):
        xv = x_ref[...]
        h = _rms_fwd(xv, g_ref[...]).astype(BF16)
        h_ref[...] = h
        gate = _sigmoid(_nn(h, wg_ref[...]))
        pb = p_ref[...].astype(BF16)
        pp = jnp.concatenate([_nn(pb, wp_ref[j]) for j in range(N_SH)], axis=1)
        gt_ref[...] = gate.astype(BF16)
        pp_ref[...] = pp.astype(BF16)
        o_ref[...] = xv + gate * pp

    row = bs((tm, D_MODEL), lambda i: (i, 0))
    return _call(
        body, "ple_fwd", (t // tm,),
        [row, bs((1, D_MODEL), lambda i: (0, 0)), _resident(wpg), bs((tm, BW), lambda i: (i, 0)), _resident(wpp)],
        [row, row, row, row],
        [_sds((t, D_MODEL), BF16), _sds((t, D_MODEL), BF16), _sds((t, D_MODEL), BF16), _sds((t, D_MODEL), F32)])(
            x, g, wpg, p_i, wpp)


def loss_head(x, g, tgt):
    t = x.shape[0]
    tm = min(512, t)

    def body(x_ref, g_ref, t_ref, l_ref, dx_ref, dg_ref):
        @pl.when(pl.program_id(0) == 0)
        def _():
            l_ref[...] = jnp.zeros_like(l_ref)
            dg_ref[...] = jnp.zeros_like(dg_ref)

        xv, gv = x_ref[...], g_ref[...]
        err = _rms_fwd(xv, gv) - t_ref[...]
        part = 0.5 * jnp.sum(jnp.mean(err * err, axis=-1, keepdims=True), axis=0, keepdims=True)
        l_ref[...] += jnp.broadcast_to(part, l_ref.shape)
        dx, dgr = _rms_bwd(err * (1.0 / D_MODEL), xv, gv)
        dx_ref[...] = dx
        dg_ref[...] += _colsum(dgr)

    row = bs((tm, D_MODEL), lambda i: (i, 0))
    vec = bs((1, D_MODEL), lambda i: (0, 0))
    return _call(body, "loss_head", (t // tm,), [row, vec, row],
                 [bs((1, 128), lambda i: (0, 0)), row, vec],
                 [_sds((1, 128), F32), _sds((t, D_MODEL), F32), _sds((1, D_MODEL), F32)])(x, g, tgt)


def tn_matmul(name, a, b, grid, a_spec, b_spec, out_spec, out_shape, split=0, split_cols=0, into=None, after=None):
    last = len(grid) - 1

    def body(a_ref, b_ref, *rest):
        o_ref = rest[-1]

        @pl.when(pl.program_id(last) == 0)
        def _():
            o_ref[...] = jnp.zeros_like(o_ref)

        res = _tn(a_ref[...].astype(BF16), b_ref[...].astype(BF16))
        if split_cols:
            cols = res.shape[1] // split_cols
            for s in range(split_cols):
                o_ref[s] += res[:, s * cols:(s + 1) * cols]
        elif split:
            rows = res.shape[0] // split
            for s in range(split):
                o_ref[s] += res[s * rows:(s + 1) * rows]
        else:
            o_ref[...] += res

    if into is None:
        return _call(body, name, grid, [a_spec, b_spec], out_spec, out_shape, after=after)(a, b)
    return _call(body, name, grid, [a_spec, b_spec, ANY], out_spec, out_shape, aliases={2: 0}, after=after)(a, b, into)


def _resident(w):
    zeros = (0,) * w.ndim
    return bs(w.shape, lambda i: zeros, pipeline_mode=pl.Buffered(1))


def norm_bwd(name, sources, dx_in, x, g):
    t = x.shape[0]
    tm = min(512, t)
    n_src = len(sources)

    def body(*refs):
        dxi_ref, x_ref, g_ref, dx_ref, dg_ref = refs[2 * n_src:]

        @pl.when(pl.program_id(0) == 0)
        def _():
            dg_ref[...] = jnp.zeros_like(dg_ref)

        dh = None
        for si in range(n_src):
            for av, wv in sources[si][3](refs[2 * si], refs[2 * si + 1]):
                part = _nt(av, wv)
                dh = part if dh is None else dh + part
        dx, dgr = _rms_bwd(dh, x_ref[...], g_ref[...])
        dx_ref[...] = dxi_ref[...] + dx
        dg_ref[...] += _colsum(dgr)

    in_specs, args = [], []
    for a, a_spec, w, _ in sources:
        in_specs += [a_spec(tm), _resident(w)]
        args += [a, w]
    row = bs((tm, D_MODEL), lambda i: (i, 0))
    vec = bs((1, D_MODEL), lambda i: (0, 0))
    return _call(body, name, (t // tm,), in_specs + [row, row, vec], [row, vec],
                 [_sds((t, D_MODEL), F32), _sds((1, D_MODEL), F32)])(*args, dx_in, x, g)


def ple_bwd_pre(dx, gate, pp, after=None):
    t = dx.shape[0]
    tm = min(1024, t)

    def body(dx_ref, g_ref, p_ref, dpre_ref, dpp_ref):
        d = dx_ref[...]
        g = g_ref[...].astype(F32)
        dpre_ref[...] = (d * p_ref[...].astype(F32) * g * (1.0 - g)).astype(BF16)
        dpp_ref[...] = (d * g).astype(BF16)

    row = bs((tm, D_MODEL), lambda i: (i, 0))
    return _call(body, "ple_bwd_pre", (t // tm,), [row, row, row], [row, row],
                 [_sds((t, D_MODEL), BF16), _sds((t, D_MODEL), BF16)], after=after)(dx, gate, pp)


def ffn_bwd_act(dx, wfo, fgu, after=None):
    t = dx.shape[0]
    tm = min(512, t)
    ns = FFN_H // 2

    def body(dx_ref, w_ref, f_ref, o_ref):
        dxb = dx_ref[...].astype(BF16)
        for j in range(2):
            cs = slice(j * ns, (j + 1) * ns)
            dact = _nt(dxb, w_ref[cs, :])
            fg = f_ref[0, :, cs].astype(F32)
            fu = f_ref[1, :, cs].astype(F32)
            s = _sigmoid(fg)
            o_ref[0, :, cs] = (dact * fu * (s * (1.0 + fg * (1.0 - s)))).astype(BF16)
            o_ref[1, :, cs] = (dact * fg * s).astype(BF16)

    blk = bs((2, tm, FFN_H), lambda i: (0, i, 0))
    return _call(body, "ffn_bwd_act", (t // tm,), [bs((tm, D_MODEL), lambda i: (i, 0)), _resident(wfo), blk],
                 blk, _sds((2, t, FFN_H), BF16), after=after)(dx, wfo, fgu)


def merge_bwd(dx, wout, gates, ybr, wbr):
    t = dx.shape[0]
    tm = min(256, t)

    def body(dx_ref, w_ref, g_ref, b_ref, wb_ref, dpre_ref, dyb_ref, dy_ref):
        dm = _nt(dx_ref[...].astype(BF16), w_ref[...])
        for k in range(N_BR):
            g = g_ref[k].astype(F32)
            dpre_ref[k] = (dm * b_ref[k].astype(F32) * g * (1.0 - g)).astype(BF16)
            dyb = (dm * g).astype(BF16)
            dyb_ref[k] = dyb
            acc = None
            for s in range(N_SH):
                part = _nt(dyb[:, s * BW:(s + 1) * BW], wb_ref[s, k])
                acc = part if acc is None else acc + part
            dy_ref[k] = acc

    blk = bs((N_BR, tm, D_MODEL), lambda i: (0, i, 0))
    return _call(body, "merge_bwd", (t // tm,),
                 [bs((tm, D_MODEL), lambda i: (i, 0)), _resident(wout), blk, blk, _resident(wbr)],
                 [blk, blk, bs((N_BR, tm, BW), lambda i: (0, i, 0))],
                 [_sds((N_BR, t, D_MODEL), BF16), _sds((N_BR, t, D_MODEL), BF16), _sds((N_BR, t, BW), F32)])(
                     dx, wout, gates, ybr, wbr)


def conva_fwd(proj, wa):
    t = proj.shape[0]
    tm, halo = min(512, t), 8
    per = tm // halo

    def body(b_ref, c_ref, x_ref, ch_ref, xh_ref, w_ref, y_ref):
        zh = jnp.where(pl.program_id(0) > 0, ch_ref[...] * xh_ref[...], 0.0)
        zext = jnp.concatenate([zh, c_ref[...] * x_ref[...]], axis=0)
        y_ref[...] = (b_ref[...] * _causal_conv(zext, w_ref, CONVA_K, halo)).astype(BF16)

    col = lambda c: bs((tm, BW), lambda i: (i, c))
    hal = lambda c: bs((halo, BW), lambda i: (_prev_blk(i, per), c))
    return _call(body, "conva_fwd", (t // tm,),
                 [col(0), col(1), col(2), hal(1), hal(2), bs((CONVA_K, BW), lambda i: (0, 0))],
                 bs((tm, BW), lambda i: (i, 0)), _sds((t, BW), BF16))(proj, proj, proj, proj, proj, wa)


def conva_bwd(proj, dys, wa, after=None):
    t = proj.shape[0]
    tm, halo = min(512, t), 8
    per = tm // halo
    last = t // halo - 1
    nt = t // tm

    def body(b_ref, c_ref, x_ref, ch_ref, xh_ref, bn_ref, dy_ref, dyn_ref, w_ref, db_ref, dc_ref, dxx_ref, dw_ref):
        i = pl.program_id(0)

        @pl.when(i == 0)
        def _():
            dw_ref[...] = jnp.zeros_like(dw_ref)

        zh = jnp.where(i > 0, ch_ref[...] * xh_ref[...], 0.0)
        cv, xv = c_ref[...], x_ref[...]
        zext = jnp.concatenate([zh, cv * xv], axis=0)
        dy = dy_ref[...]
        dconv = dy * b_ref[...]
        dcn = jnp.where(i < nt - 1, dyn_ref[...] * bn_ref[...], 0.0)
        dz = _anti_conv(jnp.concatenate([dconv, dcn], axis=0), w_ref, CONVA_K, tm)
        db_ref[...] = (dy * _causal_conv(zext, w_ref, CONVA_K, halo)).astype(BF16)
        dc_ref[...] = (dz * xv).astype(BF16)
        dxx_ref[...] = (dz * cv).astype(BF16)
        _conv_wgrad(dw_ref, dconv, zext, CONVA_K, halo)

    col = lambda c: bs((tm, BW), lambda i: (i, c))
    hal = lambda c: bs((halo, BW), lambda i: (_prev_blk(i, per), c))
    nxt = bs((halo, BW), lambda i: (_next_blk(i, per, last), 0))
    wsp = bs((CONVA_K, BW), lambda i: (0, 0))
    outs = _call(body, "conva_bwd", (t // tm,),
                 [col(0), col(1), col(2), hal(1), hal(2), nxt,
                  bs((None, tm, BW), lambda i: (0, i, 0)), bs((None, halo, BW), lambda i: (0, _next_blk(i, per, last), 0)), wsp],
                 [bs((tm, BW), lambda i: (i, 0))] * 3 + [wsp],
                 [_sds((t, BW), BF16)] * 3 + [_sds((CONVA_K, BW), F32)], after=after)(
                     proj, proj, proj, proj, proj, proj, dys, dys, wa)
    return outs[:3], outs[3]


def _head_masks():
    lane = lax.broadcasted_iota(jnp.int32, (1, BW), 1)
    return [(lane >= h * HEAD_D) & (lane < (h + 1) * HEAD_D) for h in range(HEADS)]


def _band_masks():
    qi = lax.broadcasted_iota(jnp.int32, (BLK, BLK), 0)
    ki = lax.broadcasted_iota(jnp.int32, (BLK, BLK), 1)
    return ki >= qi, ki <= qi


def attn_fwd_group(pv, d):
    rows = pv.shape[0]
    qb = min(512, rows)
    nb = qb // BLK
    scale = HEAD_D ** -0.5

    def body(q_ref, k_ref, v_ref, kh_ref, vh_ref, o_ref):
        n = pl.program_id(1)
        hm = _head_masks()
        m_prev, m_cur = _band_masks()
        for b in range(nb):
            rs = slice(b * BLK, (b + 1) * BLK)
            q = q_ref[rs, :]
            if b == 0:
                kp, vp = kh_ref[...], vh_ref[...]
                mp = m_prev & (n > 0)
            else:
                ps = slice((b - 1) * BLK, b * BLK)
                kp, vp = k_ref[ps, :], v_ref[ps, :]
                mp = m_prev
            qs = jnp.concatenate([jnp.where(hm[h], q, 0.0).astype(BF16) for h in range(HEADS)], axis=0)
            kcat = jnp.concatenate([kp, k_ref[rs, :]], axis=0)
            vcat = jnp.concatenate([vp, v_ref[rs, :]], axis=0)
            band = jnp.concatenate([mp, m_cur], axis=1)
            s = jnp.where(jnp.concatenate([band] * HEADS, axis=0), _nt(qs, kcat) * scale, NEG)
            m = jnp.max(s, axis=-1, keepdims=True)
            e = jnp.exp(s - m)
            l = jnp.sum(e, axis=-1, keepdims=True)
            of = _nn(e.astype(BF16), vcat) / l
            lse = m + jnp.log(l)
            o_acc = jnp.zeros((BLK, BW), F32)
            l_acc = jnp.zeros((BLK, BW), F32)
            for h in range(HEADS):
                hs = slice(h * BLK, (h + 1) * BLK)
                o_acc = jnp.where(hm[h], of[hs, :], o_acc)
                l_acc = jnp.where(hm[h], lse[hs, :], l_acc)
            o_ref[rs, :BW] = o_acc
            o_ref[rs, BW:] = l_acc

    per = qb // BLK
    main = lambda c: bs((qb, BW), lambda r, n: (n, r * 3 + c))
    hal = lambda c: bs((BLK, BW), lambda r, n: (_prev_blk(n, per), r * 3 + c))
    return _call(body, f"attn_fwd_d{d}", (d, rows // qb), [main(0), main(1), main(2), hal(1), hal(2)],
                 bs((qb, 2 * BW), lambda r, n: (n, r)), _sds((rows, d * 2 * BW), F32))(pv, pv, pv, pv, pv)


def attn_merge(ols):
    t = ols[0].shape[0]
    tm = min(512, t)
    width = 2 * BW

    def lse3(a, b, c):
        m = jnp.maximum(jnp.maximum(a, b), c)
        return m + jnp.log(jnp.exp(a - m) + jnp.exp(b - m) + jnp.exp(c - m))

    def body(g0, g1, g2, y_ref, o_ref, l_ref, scr, nat1, nat2):
        for src, nat, d in ((g1, nat1, DILATIONS[1]), (g2, nat2, DILATIONS[2])):
            for c in range(width // LANES):
                nat[:, c * LANES:(c + 1) * LANES] = _from_strided_view(src, scr, d, width, c)
        gs = [g0[...], nat1[...], nat2[...]]
        ls = [g[:, BW:] for g in gs]
        tot = lse3(*ls)
        o = (jnp.exp(ls[0] - tot) * gs[0][:, :BW] + jnp.exp(ls[1] - tot) * gs[1][:, :BW]
             + jnp.exp(ls[2] - tot) * gs[2][:, :BW])
        y_ref[...] = o.astype(BF16)
        o_ref[...] = o
        l_ref[...] = tot

    n = bs((tm, BW), lambda i: (i, 0))
    return _call(body, "attn_merge", (t // tm,),
                 [_view_spec(tm, 1, width), _view_spec(tm, DILATIONS[1], width), _view_spec(tm, DILATIONS[2], width)],
                 [n, n, n], [_sds((t, BW), BF16), _sds((t, BW), F32), _sds((t, BW), F32)],
                 scratch=[pltpu.VMEM((tm, LANES), F32), pltpu.VMEM((tm, width), F32), pltpu.VMEM((tm, width), F32)])(*ols)


def attn_delta(dys, o, lse):
    t = o.shape[0]
    tm = min(512, t)

    def body(d_ref, o_ref, l_ref, ld1, ld4, ld16, dy4, dy16, scr):
        hm = _head_masks()
        dy = d_ref[...]
        prod = dy * o_ref[...]
        delta = jnp.zeros_like(prod)
        for h in range(HEADS):
            delta = jnp.where(hm[h], jnp.sum(jnp.where(hm[h], prod, 0.0), axis=-1, keepdims=True), delta)
        ld = jnp.concatenate([l_ref[...], delta], axis=1)
        ld1[...] = ld
        for d, ld_v, dy_v in ((DILATIONS[1], ld4, dy4), (DILATIONS[2], ld16, dy16)):
            _to_strided_view(ld_v, lambda c: ld[:, c * LANES:(c + 1) * LANES], scr, d, 2 * BW)
            _to_strided_view(dy_v, lambda c: dy[:, c * LANES:(c + 1) * LANES], scr, d, BW)

    n = bs((tm, BW), lambda i: (i, 0))
    d4, d16 = DILATIONS[1], DILATIONS[2]
    outs = _call(body, "attn_delta", (t // tm,), [bs((None, tm, BW), lambda i: (1, i, 0)), n, n],
                 [_view_spec(tm, 1, 2 * BW), _view_spec(tm, d4, 2 * BW), _view_spec(tm, d16, 2 * BW),
                  _view_spec(tm, d4, BW), _view_spec(tm, d16, BW)],
                 [_sds((t, 2 * BW), F32), _sds((t // d4, d4 * 2 * BW), F32), _sds((t // d16, d16 * 2 * BW), F32),
                  _sds((t // d4, d4 * BW), F32), _sds((t // d16, d16 * BW), F32)],
                 scratch=[pltpu.VMEM((tm, LANES), F32)])(dys, o, lse)
    return outs[:3], outs[3:]


def attn_bwd_group(pv, dov, ldv, d):
    rows = pv.shape[0]
    qb = min(512, rows)
    nb = qb // BLK
    nsteps = rows // qb
    scale = HEAD_D ** -0.5

    def body(q_ref, qn_ref, k_ref, kh_ref, v_ref, vh_ref, do_ref, don_ref, ld_ref, ldn_ref, o_ref):
        n = pl.program_id(1)
        hm = _head_masks()
        m_prev, m_cur = _band_masks()
        has_prev, has_next = n > 0, n < nsteps - 1
        dq = [None] * nb
        dk = [jnp.zeros((BLK, BW), F32) for _ in range(nb)]
        dvv = [jnp.zeros((BLK, BW), F32) for _ in range(nb)]
        for qi in range(nb + 1):
            rs = slice(qi * BLK, (qi + 1) * BLK)
            ps = slice((qi - 1) * BLK, qi * BLK)
            if qi < nb:
                q, do, ldq = q_ref[rs, :], do_ref[rs, :], ld_ref[rs, :]
            else:
                q, do, ldq = qn_ref[...], don_ref[...], ldn_ref[...]
            kp, vp = (kh_ref[...], vh_ref[...]) if qi == 0 else (k_ref[ps, :], v_ref[ps, :])
            kc, vc = (k_ref[rs, :], v_ref[rs, :]) if qi < nb else (kp, vp)
            mp = m_prev & has_prev if qi == 0 else (m_prev & has_next if qi == nb else m_prev)
            mc = m_cur if qi < nb else jnp.zeros_like(m_cur)
            band = jnp.concatenate([jnp.concatenate([mp, mc], axis=1)] * HEADS, axis=0)
            qs = jnp.concatenate([jnp.where(hm[h], q, 0.0).astype(BF16) for h in range(HEADS)], axis=0)
            dos = jnp.concatenate([jnp.where(hm[h], do, 0.0).astype(BF16) for h in range(HEADS)], axis=0)
            kcat = jnp.concatenate([kp, kc], axis=0)
            vcat = jnp.concatenate([vp, vc], axis=0)
            col = lambda v, h: jnp.broadcast_to(jnp.max(jnp.where(hm[h], v, NEG), axis=-1, keepdims=True), (BLK, 2 * BLK))
            lcols = jnp.concatenate([col(ldq[:, :BW], h) for h in range(HEADS)], axis=0)
            dcols = jnp.concatenate([col(ldq[:, BW:], h) for h in range(HEADS)], axis=0)
            p = jnp.where(band, jnp.exp(_nt(qs, kcat) * scale - lcols), 0.0)
            ds = (p * (_nt(dos, vcat) - dcols) * scale).astype(BF16)
            if qi < nb:
                dqf = _nn(ds, kcat)
                acc_q = jnp.zeros((BLK, BW), F32)
                for h in range(HEADS):
                    acc_q = jnp.where(hm[h], dqf[h * BLK:(h + 1) * BLK, :], acc_q)
                dq[qi] = acc_q
            dkc = _tn(ds, qs)
            dvc = _tn(p.astype(BF16), dos)
            if qi >= 1:
                dk[qi - 1] = dk[qi - 1] + dkc[:BLK]
                dvv[qi - 1] = dvv[qi - 1] + dvc[:BLK]
            if qi < nb:
                dk[qi] = dk[qi] + dkc[BLK:]
                dvv[qi] = dvv[qi] + dvc[BLK:]
        for b in range(nb):
            rs = slice(b * BLK, (b + 1) * BLK)
            for c, val in enumerate((dq[b], dk[b], dvv[b])):
                cs = slice(c * BW, (c + 1) * BW)
                o_ref[rs, cs] = val

    per = qb // BLK
    last = rows // BLK - 1
    main = lambda c: bs((qb, BW), lambda r, n: (n, r * 3 + c))
    prv = lambda c: bs((BLK, BW), lambda r, n: (_prev_blk(n, per), r * 3 + c))
    nxt = lambda c: bs((BLK, BW), lambda r, n: (_next_blk(n, per, last), r * 3 + c))
    accs = bs((qb, 3 * BW), lambda r, n: (n, r))
    in_specs = [main(0), nxt(0), main(1), prv(1), main(2), prv(2),
                bs((qb, BW), lambda r, n: (n, r)), bs((BLK, BW), lambda r, n: (_next_blk(n, per, last), r)),
                bs((qb, 2 * BW), lambda r, n: (n, r)), bs((BLK, 2 * BW), lambda r, n: (_next_blk(n, per, last), r))]
    args = [pv, pv, pv, pv, pv, pv, dov, dov, ldv, ldv]
    return _call(body, f"attn_bwd_d{d}", (d, nsteps), in_specs, accs, _sds((rows, d * 3 * BW), F32))(*args)


def attn_bwd_finish(parts):
    t = parts[0].shape[0]
    tm = min(512, t)
    width = 3 * BW

    def body(g0, g1, g2, o_ref, scr):
        for c in range(width // LANES):
            cs = slice(c * LANES, (c + 1) * LANES)
            acc = g0[:, cs]
            acc = acc + _from_strided_view(g1, scr, DILATIONS[1], width, c)
            acc = acc + _from_strided_view(g2, scr, DILATIONS[2], width, c)
            o_ref[:, cs] = acc.astype(BF16)

    return _call(body, "attn_bwd_finish", (t // tm,),
                 [_view_spec(tm, 1, width), _view_spec(tm, DILATIONS[1], width), _view_spec(tm, DILATIONS[2], width)],
                 bs((tm, width), lambda i: (i, 0)), _sds((t, width), BF16),
                 scratch=[pltpu.VMEM((tm, LANES), F32)])(*parts)


def _group_masks():
    lane = lax.broadcasted_iota(jnp.int32, (1, BW), 1)
    return [(lane >= g * HEAD_D) & (lane < (g + 1) * HEAD_D) for g in range(4)]


def sgu_fwd(proj, ln_g, ln_b, w_tril, b_full):
    t = proj.shape[0]
    tm = min(512, t)

    def body(u_ref, v_ref, g_ref, b_ref, w_ref, bf_ref, y_ref):
        gm = _group_masks()
        xhat, _ = _ln_hat(v_ref[...])
        vb = (xhat * g_ref[...] + b_ref[...]).astype(BF16)
        for c in range(tm // BLK):
            rs = slice(c * BLK, (c + 1) * BLK)
            vc = vb[rs, :]
            mixed = bf_ref[...]
            for g in range(4):
                mixed = mixed + jnp.where(gm[g], _nn(w_ref[g], vc), 0.0)
            y_ref[rs, :] = (u_ref[rs, :] * mixed).astype(BF16)

    vec = bs((1, BW), lambda i: (0, 0))
    return _call(body, "sgu_fwd", (t // tm,),
                 [bs((tm, BW), lambda i: (i, 6)), bs((tm, BW), lambda i: (i, 7)), vec, vec,
                  bs((4, BLK, BLK), lambda i: (0, 0, 0)), bs((BLK, BW), lambda i: (0, 0))],
                 bs((tm, BW), lambda i: (i, 0)), _sds((t, BW), BF16))(proj, proj, ln_g, ln_b, w_tril, b_full)


def sgu_bwd(proj, dys, ln_g, ln_b, w_tril, b_full):
    t = proj.shape[0]
    tm = min(512, t)

    def body(u_ref, v_ref, dy_ref, g_ref, b_ref, w_ref, bf_ref, du_ref, dv_ref, dw_ref, dbf_ref, dg_ref, db_ref, dvl_ref):
        @pl.when(pl.program_id(0) == 0)
        def _():
            dw_ref[...] = jnp.zeros_like(dw_ref)
            dbf_ref[...] = jnp.zeros_like(dbf_ref)
            dg_ref[...] = jnp.zeros_like(dg_ref)
            db_ref[...] = jnp.zeros_like(db_ref)

        gm = _group_masks()
        xhat, r = _ln_hat(v_ref[...])
        gv = g_ref[...]
        vb = (xhat * gv + b_ref[...]).astype(BF16)
        for c in range(tm // BLK):
            rs = slice(c * BLK, (c + 1) * BLK)
            vc = vb[rs, :]
            dy = dy_ref[rs, :]
            mixed = bf_ref[...]
            for g in range(4):
                mixed = mixed + jnp.where(gm[g], _nn(w_ref[g], vc), 0.0)
            du_ref[rs, :] = (dy * mixed).astype(BF16)
            dm = dy * u_ref[rs, :]
            dbf_ref[...] += dm
            dvl = jnp.zeros((BLK, BW), F32)
            for g in range(4):
                dmg = jnp.where(gm[g], dm, 0.0).astype(BF16)
                dw_ref[g] += _nt(dmg, vc)
                dvl = dvl + _tn(w_ref[g], dmg)
            dvl_ref[rs, :] = dvl
        dvl = dvl_ref[...]
        dv_ref[...] = _ln_bwd(dvl, xhat, r, gv).astype(BF16)
        dg_ref[...] += _colsum(dvl * xhat)
        db_ref[...] += _colsum(dvl)

    vec = bs((1, BW), lambda i: (0, 0))
    row = bs((tm, BW), lambda i: (i, 0))
    wsp = bs((4, BLK, BLK), lambda i: (0, 0, 0))
    bfs = bs((BLK, BW), lambda i: (0, 0))
    return _call(body, "sgu_bwd", (t // tm,),
                 [bs((tm, BW), lambda i: (i, 6)), bs((tm, BW), lambda i: (i, 7)), bs((None, tm, BW), lambda i: (2, i, 0)),
                  vec, vec, wsp, bfs],
                 [row, row, wsp, bfs, vec, vec],
                 [_sds((t, BW), BF16), _sds((t, BW), BF16), _sds((4, BLK, BLK), F32), _sds((BLK, BW), F32),
                  _sds((1, BW), F32), _sds((1, BW), F32)],
                 scratch=[pltpu.VMEM((tm, BW), F32)])(proj, proj, dys, ln_g, ln_b, w_tril, b_full)


CONF_HALO = 32


def conf_fwd(proj, dw, ln_g, ln_b, after=None):
    t = proj.shape[0]
    tm, halo = min(512, t), CONF_HALO
    per = tm // halo

    def body(v_ref, gt_ref, vh_ref, gh_ref, w_ref, g_ref, b_ref, y_ref, z_ref):
        yh = jnp.where(pl.program_id(0) > 0, vh_ref[...] * _sigmoid(gh_ref[...]), 0.0)
        yext = jnp.concatenate([yh, v_ref[...] * _sigmoid(gt_ref[...])], axis=0)
        z = _causal_conv(yext, w_ref, CONF_K, halo)
        z_ref[...] = z
        xhat, _ = _ln_hat(z)
        ln = xhat * g_ref[...] + b_ref[...]
        y_ref[...] = (ln * _sigmoid(ln)).astype(BF16)

    vec = bs((1, BW), lambda i: (0, 0))
    col = lambda c: bs((tm, BW), lambda i: (i, c))
    hal = lambda c: bs((halo, BW), lambda i: (_prev_blk(i, per), c))
    row = bs((tm, BW), lambda i: (i, 0))
    return _call(body, "conf_fwd", (t // tm,),
                 [col(8), col(9), hal(8), hal(9), bs((CONF_K, BW), lambda i: (0, 0)), vec, vec],
                 [row, row], [_sds((t, BW), BF16), _sds((t, BW), F32)], after=after)(
                     proj, proj, proj, proj, dw, ln_g, ln_b)


def conf_bwd_ln(z, dys, ln_g, ln_b):
    t = z.shape[0]
    tm = min(1024, t)

    def body(z_ref, dy_ref, g_ref, b_ref, dz_ref, dg_ref, db_ref):
        @pl.when(pl.program_id(0) == 0)
        def _():
            dg_ref[...] = jnp.zeros_like(dg_ref)
            db_ref[...] = jnp.zeros_like(db_ref)

        gv = g_ref[...]
        xhat, r = _ln_hat(z_ref[...])
        ln = xhat * gv + b_ref[...]
        s = _sigmoid(ln)
        dln = dy_ref[...] * (s * (1.0 + ln * (1.0 - s)))
        dz_ref[...] = _ln_bwd(dln, xhat, r, gv)
        dg_ref[...] += _colsum(dln * xhat)
        db_ref[...] += _colsum(dln)

    vec = bs((1, BW), lambda i: (0, 0))
    row = bs((tm, BW), lambda i: (i, 0))
    return _call(body, "conf_bwd_ln", (t // tm,), [row, bs((None, tm, BW), lambda i: (3, i, 0)), vec, vec],
                 [row, vec, vec], [_sds((t, BW), F32), _sds((1, BW), F32), _sds((1, BW), F32)])(z, dys, ln_g, ln_b)


def conf_bwd_conv(proj, dz, dw):
    t = proj.shape[0]
    tm, halo = min(512, t), CONF_HALO
    per = tm // halo
    last = t // halo - 1
    nt = t // tm

    def body(v_ref, gt_ref, vh_ref, gh_ref, dz_ref, dzn_ref, w_ref, dv_ref, dg_ref, dw_ref):
        i = pl.program_id(0)

        @pl.when(i == 0)
        def _():
            dw_ref[...] = jnp.zeros_like(dw_ref)

        val = v_ref[...]
        sg = _sigmoid(gt_ref[...])
        yh = jnp.where(i > 0, vh_ref[...] * _sigmoid(gh_ref[...]), 0.0)
        yext = jnp.concatenate([yh, val * sg], axis=0)
        dz = dz_ref[...]
        dzn = jnp.where(i < nt - 1, dzn_ref[...], 0.0)
        dy0 = _anti_conv(jnp.concatenate([dz, dzn], axis=0), w_ref, CONF_K, tm)
        dv_ref[...] = (dy0 * sg).astype(BF16)
        dg_ref[...] = (dy0 * val * sg * (1.0 - sg)).astype(BF16)
        _conv_wgrad(dw_ref, dz, yext, CONF_K, halo)

    col = lambda c: bs((tm, BW), lambda i: (i, c))
    hal = lambda c: bs((halo, BW), lambda i: (_prev_blk(i, per), c))
    row = bs((tm, BW), lambda i: (i, 0))
    wsp = bs((CONF_K, BW), lambda i: (0, 0))
    return _call(body, "conf_bwd_conv", (t // tm,),
                 [col(8), col(9), hal(8), hal(9), row, bs((halo, BW), lambda i: (_next_blk(i, per, last), 0)), wsp],
                 [row, row, wsp], [_sds((t, BW), BF16), _sds((t, BW), BF16), _sds((CONF_K, BW), F32)])(
                     proj, proj, proj, proj, dz, dz, dw)


def _place():
    return lax.axis_index("x"), lax.axis_index("y"), lax.axis_index("c")


def _comm_call(body, name, n_in, out_shape, scratch, aliases=None):
    return pl.pallas_call(body, name=name, in_specs=[ANY] * n_in, out_specs=[ANY] * len(out_shape), out_shape=out_shape,
                          scratch_shapes=scratch, input_output_aliases=aliases or {},
                          compiler_params=pltpu.CompilerParams(has_side_effects=True, vmem_limit_bytes=VMEM_LIMIT))


HBM_SPEC = pl.BlockSpec(memory_space=pltpu.HBM)
SEM_SPEC = pl.BlockSpec(memory_space=pltpu.SEMAPHORE)
EFFECT = pltpu.SideEffectType.DATAFLOW_SIDE_EFFECTING


class SplitExchange:
    def __init__(self, name, bufs, plan, n_copies):
        self.name, self.bufs, self.plan, self.n = name, list(bufs), plan, n_copies

    def start(self, after):
        nb, n, plan = len(self.bufs), self.n, self.plan

        def body(*refs):
            send, recv, token = refs[nb + 1], refs[nb + 2], refs[-1]
            for k, (src, dst, _, dev) in enumerate(plan(refs[:nb])):
                pltpu.make_async_remote_copy(src_ref=src, dst_ref=dst, send_sem=send.at[k], recv_sem=recv.at[k],
                                             device_id=dev, device_id_type=MESH).start()
            token[...] = jnp.zeros_like(token)

        outs = pl.pallas_call(
            body, name=self.name + "_start",
            out_shape=(pltpu.SemaphoreType.DMA((n,)), pltpu.SemaphoreType.DMA((n,)),
                       *[pltpu.HBM(b.shape, b.dtype) for b in self.bufs], _sds((8, 128), F32)),
            in_specs=[HBM_SPEC] * nb + [ANY],
            out_specs=(SEM_SPEC, SEM_SPEC, *[HBM_SPEC] * nb, pl.BlockSpec(memory_space=pltpu.VMEM)),
            input_output_aliases={i: 2 + i for i in range(nb)},
            compiler_params=pltpu.CompilerParams(has_side_effects=EFFECT))(
                *[pltpu.with_memory_space_constraint(b, pltpu.HBM) for b in self.bufs], after)
        self.send, self.recv, self.bufs = outs[0], outs[1], list(outs[2:2 + nb])
        return outs[-1]

    def wait(self, after):
        nb, plan = len(self.bufs), self.plan

        def body(*refs):
            send, recv = refs[nb], refs[nb + 1]
            for k, (src, _, land, dev) in enumerate(plan(refs[:nb])):
                cp = pltpu.make_async_remote_copy(src_ref=src, dst_ref=land, send_sem=send.at[k], recv_sem=recv.at[k],
                                                  device_id=dev, device_id_type=MESH)
                cp.wait_send()
                cp.wait_recv()

        outs = pl.pallas_call(
            body, name=self.name + "_wait", out_shape=tuple(pltpu.HBM(b.shape, b.dtype) for b in self.bufs),
            in_specs=[HBM_SPEC] * nb + [SEM_SPEC, SEM_SPEC, ANY], out_specs=[HBM_SPEC] * nb,
            input_output_aliases={i: i for i in range(nb)},
            compiler_params=pltpu.CompilerParams(has_side_effects=EFFECT))(*self.bufs, self.send, self.recv, after)
        return list(outs)


def _chips_of(x, y):
    return [(1 - x, y), (x, 1 - y), (1 - x, 1 - y)]


def allgather_ici_plan(shapes):
    def plan(refs):
        x, y, c = _place()
        out = []
        for a, ref in enumerate(refs):
            hl = shapes[a][1] // 2
            half = pl.ds(c * hl, hl)
            for cx, cy in _chips_of(x, y):
                mine = ref.at[2 * x + y, half]
                out.append((mine, mine, ref.at[2 * cx + cy, half], (cx, cy, c)))
        return out
    return plan


def allgather_d2d_plan(shapes):
    def plan(refs):
        x, y, c = _place()
        out = []
        for a, ref in enumerate(refs):
            hl = shapes[a][1] // 2
            for cx, cy in _chips_of(x, y):
                got = ref.at[2 * cx + cy, pl.ds(c * hl, hl)]
                out.append((got, got, ref.at[2 * cx + cy, pl.ds((1 - c) * hl, hl)], (x, y, 1 - c)))
        return out
    return plan


def gather8(v, reduce):
    rows, cols = v.shape

    def body(v_ref, o_ref, land_ref, send, recv, lsem):
        x, y, c = _place()
        me = 4 * x + 2 * y + c
        land = land_ref if reduce else o_ref
        mine = pltpu.make_async_copy(v_ref, land.at[me], lsem)
        mine.start()
        sent = []
        for j in range(1, 8):
            fx, fy, fc = (j >> 2) & 1, (j >> 1) & 1, j & 1
            tgt = (1 - x if fx else x, 1 - y if fy else y, 1 - c if fc else c)
            cp = pltpu.make_async_remote_copy(src_ref=v_ref, dst_ref=land.at[me], send_sem=send.at[j - 1],
                                              recv_sem=recv.at[j - 1], device_id=tgt, device_id_type=MESH)
            cp.start()
            sent.append(cp)
        for j in range(1, 8):
            fx, fy, fc = (j >> 2) & 1, (j >> 1) & 1, j & 1
            peer = 4 * (1 - x if fx else x) + 2 * (1 - y if fy else y) + (1 - c if fc else c)
            pltpu.make_async_remote_copy(src_ref=v_ref, dst_ref=land.at[peer], send_sem=send.at[j - 1],
                                         recv_sem=recv.at[j - 1], device_id=(x, y, c), device_id_type=MESH).wait_recv()
        for cp in sent:
            cp.wait_send()
        mine.wait()
        if reduce:
            acc = land_ref[0]
            for k in range(1, 8):
                acc = acc + land_ref[k]
            o_ref[...] = acc

    vm = pl.BlockSpec(memory_space=pltpu.VMEM)
    out_shape = _sds((rows, cols), F32) if reduce else _sds((8, rows, cols), F32)
    land_shape = (8, rows, cols) if reduce else (8, 128)
    return pl.pallas_call(
        body, name="allreduce8" if reduce else "allgather8", in_specs=[vm], out_specs=vm, out_shape=out_shape,
        scratch_shapes=[pltpu.VMEM(land_shape, F32), pltpu.SemaphoreType.DMA((7,)), pltpu.SemaphoreType.DMA((7,)),
                        pltpu.SemaphoreType.DMA],
        compiler_params=pltpu.CompilerParams(has_side_effects=True, vmem_limit_bytes=VMEM_LIMIT))(v)


def allgather_weights(bufs):
    n = len(bufs)

    def body(*refs):
        ins, outs = refs[:n], refs[n:2 * n]
        send, recv = refs[2 * n:]
        x, y, c = _place()
        s_me = 2 * x + y
        chips = [(1 - x, y), (x, 1 - y), (1 - x, 1 - y)]
        sibling = (x, y, 1 - c)
        started = []
        for a in range(n):
            hl = bufs[a].shape[1] // 2
            half = pl.ds(c * hl, hl)
            for j, chip in enumerate(chips):
                cp = pltpu.make_async_remote_copy(src_ref=ins[a].at[s_me, half], dst_ref=outs[a].at[s_me, half],
                                                  send_sem=send.at[6 * a + j], recv_sem=recv.at[6 * a + j],
                                                  device_id=(chip[0], chip[1], c), device_id_type=MESH)
                cp.start()
                started.append(cp)
        for a in range(n):
            hl = bufs[a].shape[1] // 2
            half = pl.ds(c * hl, hl)
            for j, chip in enumerate(chips):
                s_j = 2 * chip[0] + chip[1]
                landed = outs[a].at[s_j, half]
                pltpu.make_async_remote_copy(src_ref=landed, dst_ref=landed, send_sem=send.at[6 * a + j],
                                             recv_sem=recv.at[6 * a + j], device_id=sibling, device_id_type=MESH).wait_recv()
                fw = pltpu.make_async_remote_copy(src_ref=landed, dst_ref=landed, send_sem=send.at[6 * a + 3 + j],
                                                  recv_sem=recv.at[6 * a + 3 + j], device_id=sibling, device_id_type=MESH)
                fw.start()
                started.append(fw)
        for a in range(n):
            hl = bufs[a].shape[1] // 2
            other = pl.ds((1 - c) * hl, hl)
            for j, chip in enumerate(chips):
                s_j = 2 * chip[0] + chip[1]
                theirs = outs[a].at[s_j, other]
                pltpu.make_async_remote_copy(src_ref=theirs, dst_ref=theirs, send_sem=send.at[6 * a + 3 + j],
                                             recv_sem=recv.at[6 * a + 3 + j], device_id=sibling, device_id_type=MESH).wait_recv()
        for cp in started:
            cp.wait_send()

    out_shape = [_sds(b.shape, b.dtype) for b in bufs]
    scratch = [pltpu.SemaphoreType.DMA((6 * n,)), pltpu.SemaphoreType.DMA((6 * n,))]
    return _comm_call(body, "allgather_weights", n, out_shape, scratch, aliases={a: a for a in range(n)})(*bufs)


def _row_tile(rows, cols):
    best = 16
    for t in range(16, rows + 1, 16):
        if rows % t == 0 and t * cols * 4 <= 2 * 1024 * 1024:
            best = t
    return best


def _rs_add_sibling(scal, g, ra, hr):
    cols = g.shape[2]
    tr = _row_tile(hr, cols)
    nr = hr // tr

    def body(s_ref, g_ref, r_ref, p32_ref, p16_ref):
        v = g_ref[...] + r_ref[...]
        p16_ref[...] = v.astype(BF16)

        @pl.when(pl.program_id(1) == s_ref[0])
        def _():
            p32_ref[...] = v

    blk = lambda f: bs((None, tr, cols), f)
    own = blk(lambda i, s, sr: (s, i, 0))
    spec = pltpu.PrefetchScalarGridSpec(num_scalar_prefetch=1, grid=(nr, N_SH),
                                        in_specs=[blk(lambda i, s, sr: (s, sr[1] * nr + i, 0)), own],
                                        out_specs=[bs((tr, cols), lambda i, s, sr: (i, 0)), own])
    return pl.pallas_call(body, name="rs_add_sibling", grid_spec=spec,
                          out_shape=[_sds((hr, cols), F32), _sds((N_SH, hr, cols), BF16)],
                          compiler_params=pltpu.CompilerParams(dimension_semantics=("arbitrary",) * 2,
                                                               vmem_limit_bytes=VMEM_LIMIT))(scal, g, ra)


def _rs_add_chips(scal, p32, rb, hr):
    cols = p32.shape[1]
    tr = _row_tile(hr, cols)
    nr = hr // tr

    def body(s_ref, p_ref, r0, r1, r2, o_ref):
        o_ref[...] = ((p_ref[...] + r0[...].astype(F32)) + r1[...].astype(F32)) + r2[...].astype(F32)

    blk = lambda f: bs((None, tr, cols), f)
    spec = pltpu.PrefetchScalarGridSpec(
        num_scalar_prefetch=1, grid=(nr,),
        in_specs=[bs((tr, cols), lambda i, sr: (i, 0))] + [blk(functools.partial(lambda i, sr, j: (j, i, 0), j=j))
                                                            for j in range(3)],
        out_specs=blk(lambda i, sr: (sr[1], i, 0)))
    return pl.pallas_call(body, name="rs_add_chips", grid_spec=spec, out_shape=_sds((2, hr, cols), F32),
                          compiler_params=pltpu.CompilerParams(dimension_semantics=("arbitrary",),
                                                               vmem_limit_bytes=VMEM_LIMIT))(scal, p32, rb, rb, rb)


class SplitReduceScatter:
    def __init__(self, gs):
        x, y, c = _place()
        self.scal = jnp.stack([2 * x + y, c]).astype(jnp.int32)
        self.gs, self.n = list(gs), len(gs)
        self.hrs = [g.shape[1] // 2 for g in gs]

    def swap_start(self, after):
        n, hrs = self.n, self.hrs

        def plan(refs):
            x, y, c = _place()
            return [(refs[a].at[:, pl.ds((1 - c) * hrs[a], hrs[a])], refs[n + a], refs[n + a], (x, y, 1 - c))
                    for a in range(n)]

        lands = [lax.empty((N_SH, hrs[a], g.shape[2]), F32) for a, g in enumerate(self.gs)]
        self.ex = SplitExchange("rs_swap_halves", self.gs + lands, plan, n)
        return self.ex.start(after)

    def swap_wait_send_start(self, after):
        n, hrs = self.n, self.hrs
        bufs = self.ex.wait(after)
        parts = [_rs_add_sibling(self.scal, bufs[a], bufs[n + a], hrs[a]) for a in range(n)]
        self.p32 = [p[0] for p in parts]

        def plan(refs):
            x, y, c = _place()
            return [(refs[a].at[2 * cx + cy], refs[n + a].at[j], refs[n + a].at[j], (cx, cy, c))
                    for a in range(n) for j, (cx, cy) in enumerate(_chips_of(x, y))]

        lands = [lax.empty((3, hrs[a], g.shape[2]), BF16) for a, g in enumerate(self.gs)]
        self.ex = SplitExchange("rs_send_partials", [p[1] for p in parts] + lands, plan, 3 * n)
        return self.ex.start(parts[-1][1])

    def send_wait_share_start(self, after):
        n, hrs = self.n, self.hrs
        bufs = self.ex.wait(after)
        fins = [_rs_add_chips(self.scal, self.p32[a], bufs[n + a], hrs[a]) for a in range(n)]

        def plan(refs):
            x, y, c = _place()
            return [(refs[a].at[c], refs[a].at[c], refs[a].at[1 - c], (x, y, 1 - c)) for a in range(n)]

        self.ex = SplitExchange("rs_share_halves", fins, plan, n)
        return self.ex.start(fins[-1])

    def share_wait(self, after):
        fulls = self.ex.wait(after)
        return [f.reshape(2 * hr, f.shape[2]) for f, hr in zip(fulls, self.hrs)]


def reduce_scatter_grads(gs):
    n = len(gs)
    x, y, c = _place()
    scal = jnp.stack([2 * x + y, c]).astype(jnp.int32)
    hrs = [g.shape[1] // 2 for g in gs]

    def swap_body(*refs):
        ins, outs = refs[:n], refs[n:2 * n]
        send, recv = refs[2 * n:]
        xx, yy, cc = _place()
        cps = []
        for a in range(n):
            cp = pltpu.make_async_remote_copy(src_ref=ins[a].at[:, pl.ds((1 - cc) * hrs[a], hrs[a])], dst_ref=outs[a],
                                              send_sem=send.at[a], recv_sem=recv.at[a],
                                              device_id=(xx, yy, 1 - cc), device_id_type=MESH)
            cp.start()
            cps.append(cp)
        for cp in cps:
            cp.wait()

    ras = _comm_call(swap_body, "rs_swap_halves", n, [_sds((N_SH, hrs[a], gs[a].shape[2]), F32) for a in range(n)],
                     [pltpu.SemaphoreType.DMA((n,)), pltpu.SemaphoreType.DMA((n,))])(*gs)

    parts = [_rs_add_sibling(scal, gs[a], ras[a], hrs[a]) for a in range(n)]

    def ici_body(*refs):
        ins, outs = refs[:n], refs[n:2 * n]
        send, recv = refs[2 * n:]
        xx, yy, cc = _place()
        chips = [(1 - xx, yy), (xx, 1 - yy), (1 - xx, 1 - yy)]
        cps = []
        for a in range(n):
            for j, chip in enumerate(chips):
                cp = pltpu.make_async_remote_copy(src_ref=ins[a].at[2 * chip[0] + chip[1]], dst_ref=outs[a].at[j],
                                                  send_sem=send.at[3 * a + j], recv_sem=recv.at[3 * a + j],
                                                  device_id=(chip[0], chip[1], cc), device_id_type=MESH)
                cp.start()
                cps.append(cp)
        for cp in cps:
            cp.wait()

    rbs = _comm_call(ici_body, "rs_send_partials", n, [_sds((3, hrs[a], gs[a].shape[2]), BF16) for a in range(n)],
                     [pltpu.SemaphoreType.DMA((3 * n,)), pltpu.SemaphoreType.DMA((3 * n,))])(*[p[1] for p in parts])

    fins = [_rs_add_chips(scal, parts[a][0], rbs[a], hrs[a]) for a in range(n)]

    def share_body(*refs):
        ins, outs = refs[:n], refs[n:2 * n]
        send, recv = refs[2 * n:]
        xx, yy, cc = _place()
        sib = (xx, yy, 1 - cc)
        cps = []
        for a in range(n):
            cp = pltpu.make_async_remote_copy(src_ref=ins[a].at[cc], dst_ref=outs[a].at[cc], send_sem=send.at[a],
                                              recv_sem=recv.at[a], device_id=sib, device_id_type=MESH)
            cp.start()
            cps.append(cp)
        for a in range(n):
            pltpu.make_async_remote_copy(src_ref=ins[a].at[cc], dst_ref=outs[a].at[1 - cc], send_sem=send.at[a],
                                         recv_sem=recv.at[a], device_id=sib, device_id_type=MESH).wait_recv()
        for cp in cps:
            cp.wait_send()

    fulls = _comm_call(share_body, "rs_share_halves", n, [_sds(f.shape, F32) for f in fins],
                       [pltpu.SemaphoreType.DMA((n,)), pltpu.SemaphoreType.DMA((n,))],
                       aliases={a: a for a in range(n)})(*fins)
    return [f.reshape(2 * hr, f.shape[2]) for f, hr in zip(fulls, hrs)]


def adamw(w, g, m, v):
    shape = w.shape
    cols = shape[-1]
    rows = math.prod(shape[:-1]) if len(shape) > 1 else 1
    tr = 256 if rows % 256 == 0 and rows > 256 else rows
    c1 = 1.0 - ADAM_B1 ** ADAM_STEP
    c2 = 1.0 - ADAM_B2 ** ADAM_STEP

    def body(w_ref, g_ref, m_ref, v_ref, d_ref, nm_ref, nv_ref):
        gv = g_ref[...]
        nm = ADAM_B1 * m_ref[...] + (1.0 - ADAM_B1) * gv
        nv = ADAM_B2 * v_ref[...] + (1.0 - ADAM_B2) * (gv * gv)
        nm_ref[...] = nm
        nv_ref[...] = nv
        d_ref[...] = -ADAM_LR * ((nm / c1) / (jnp.sqrt(nv / c2) + ADAM_EPS) + ADAM_WD * w_ref[...])

    row = bs((tr, cols), lambda i: (i, 0))
    outs = _call(body, "adamw", (rows // tr,), [row] * 4, [row] * 3, [_sds((rows, cols), F32)] * 3)(
        *[a.reshape(rows, cols) for a in (w, g, m, v)])
    return [o.reshape(shape) for o in outs]


class Hooks:
    def __init__(self):
        self.steps = {}

    def add(self, point, fn):
        self.steps.setdefault(point, []).append(fn)

    def run(self, point, arr, env=None):
        tok = None
        for fn in self.steps.get(point, ()):
            got = fn(arr if tok is None else tok, env)
            tok = tok if got is None else got
        return tok


def layer_fwd(x, p_i, w, hooks):
    h, proj, *qkv = norm_in_proj(x, w["g_mix"], w["win"], after=hooks.run("start", x))
    ya = conva_fwd(proj, w["conv_a"])
    yb, o32, lse = attn_merge([attn_fwd_group(pv, d) for pv, d in zip(qkv, DILATIONS)])
    yc = sgu_fwd(proj, w["sgu_ln_g"], w["sgu_ln_b"], w["sgu_wt"], w["sgu_bf"])
    yd, z = conf_fwd(proj, w["conf_dw"], w["conf_ln_g"], w["conf_ln_b"], after=hooks.run("pre_conf", yc))
    ys = (ya, yb, yc, yd)
    tok = hooks.run("pre_merge", yd)
    merged, gates, ybr = merge_fwd(h, ys, w["wg"], w["wbr"], after=tok)
    x1 = mm_residual(merged, w["wout"], x, "attn_out", after=hooks.run("post_merge", merged))
    h2, fgu, act = ffn_in(x1, w["g_ffn"], w["wfi"])
    x2 = mm_residual(act, w["wfo"], x1, "ffn_out", after=hooks.run("post_ffn_in", act))
    h3, gate, pp, x3 = ple_fwd(x2, w["g_ple"], w["wpg"], p_i, w["wpp"])
    saved = dict(x=x, h=h, proj=proj, qkv=qkv, ys=ys, o32=o32, lse=lse, z=z, merged=merged, gates=gates, ybr=ybr, x1=x1,
                 h2=h2, fgu=fgu, act=act, x2=x2, h3=h3, gate=gate, pp=pp)
    return x3, saved


def layer_bwd(dx3, p_i, w, s, hooks):
    t = dx3.shape[0]
    tr = min(1024, t)
    nr = t // tr
    ns_fi = FFN_H // 2
    small = {}

    dpre, dpp = ple_bwd_pre(dx3, s["gate"], s["pp"], after=hooks.run("start", dx3))
    ga_shape, gb_shape = _sds((N_SH, 6 * BW, D_MODEL), F32), _sds((N_SH, 5 * BW, BW), F32)
    ga_blk = lambda idx: bs((N_SH, BW, D_MODEL), idx)
    ga = tn_matmul("dw_ple_gate", s["h3"], dpre, (nr,), bs((tr, D_MODEL), lambda r: (r, 0)),
                   bs((tr, D_MODEL), lambda r: (r, 0)), ga_blk(lambda r: (0, 5, 0)), ga_shape, split=N_SH)
    gb = tn_matmul("dw_ple_proj", p_i, dpp, (N_SH, nr), bs((tr, BW), lambda j, r: (r, 0)),
                   bs((tr, BW), lambda j, r: (r, j)), bs((None, BW, BW), lambda j, r: (j, 4, 0)), gb_shape)
    dx2, small["g_ple"] = norm_bwd(
        "ple_norm_bwd",
        [(dpre, lambda tm: bs((tm, D_MODEL), lambda i: (i, 0)), w["wpg"], lambda a, wr: [(a[...], wr[...])])],
        dx3, s["x2"], w["g_ple"])

    df = ffn_bwd_act(dx2, w["wfo"], s["fgu"], after=hooks.run("pre_ffn", dx2))
    gfo = tn_matmul("dw_ffn_out", s["act"], dx2, (2, nr), bs((tr, ns_fi), lambda j, r: (r, j)),
                    bs((tr, D_MODEL), lambda j, r: (r, 0)), bs((2, FFN_H // N_SH, D_MODEL), lambda j, r: (j, 0, 0)),
                    _sds((N_SH, FFN_H // N_SH, D_MODEL), F32), split=2)
    gfi = tn_matmul("dw_ffn_in", s["h2"], df, (N_SH, nr), bs((tr, D_MODEL), lambda j, r: (r, 0)),
                    bs((None, tr, ns_fi), lambda j, r: (j // 2, r, j % 2)),
                    bs((None, D_MODEL, ns_fi), lambda j, r: (j, 0, 0)), _sds((N_SH, D_MODEL, ns_fi), F32))
    dx1, small["g_ffn"] = norm_bwd(
        "ffn_norm_bwd",
        [(df, lambda tm: bs((2, tm, FFN_H), lambda i: (0, i, 0)), w["wfi"],
          lambda a, wr: [(a[k // 2, :, (k % 2) * ns_fi:(k % 2 + 1) * ns_fi], wr[k]) for k in range(N_SH)])],
        dx2, s["x1"], w["g_ffn"])

    dpre_m, dyb, dys = merge_bwd(dx1, w["wout"], s["gates"], s["ybr"], w["wbr"])
    ga = tn_matmul("dw_out", s["merged"], dx1, (nr,), bs((tr, D_MODEL), lambda r: (r, 0)),
                   bs((tr, D_MODEL), lambda r: (r, 0)), ga_blk(lambda r: (0, 4, 0)), ga_shape, split=N_SH, into=ga,
                   after=hooks.run("pre_dw_out", dyb, dict(gfo=gfo, gfi=gfi)))
    ga = tn_matmul("dw_merge_gate", s["h"], dpre_m, (N_BR, nr), bs((tr, D_MODEL), lambda k, r: (r, 0)),
                   bs((None, tr, D_MODEL), lambda k, r: (k, r, 0)), ga_blk(lambda k, r: (0, k, 0)), ga_shape,
                   split=N_SH, into=ga)
    for k in range(N_BR):
        gb = tn_matmul("dw_branch", s["ys"][k], dyb, (nr,), bs((tr, BW), lambda r: (r, 0)),
                       bs((None, tr, D_MODEL), functools.partial(lambda r, kk: (kk, r, 0), kk=k)),
                       bs((N_SH, BW, BW), functools.partial(lambda r, kk: (0, kk, 0), kk=k)), gb_shape,
                       split_cols=N_SH, into=gb)

    (dab, dac, dax), small["conv_a"] = conva_bwd(s["proj"], dys, w["conv_a"], after=hooks.run("pre_conva", gb))
    lds, dy_views = attn_delta(dys, s["o32"], s["lse"])
    dy_views = [dys[1]] + list(dy_views)
    dqkv = attn_bwd_finish([attn_bwd_group(pv, dov, ldv, d)
                            for pv, dov, ldv, d in zip(s["qkv"], dy_views, lds, DILATIONS)])
    du, dv, d_sw, d_sbf, small["sgu_ln_g"], small["sgu_ln_b"] = sgu_bwd(
        s["proj"], dys, w["sgu_ln_g"], w["sgu_ln_b"], w["sgu_wt"], w["sgu_bf"])
    small["sgu_w"] = jnp.where(jnp.tril(jnp.ones((BLK, BLK), bool))[None], d_sw, 0.0)
    small["sgu_b"] = jnp.sum(d_sbf.reshape(BLK, 4, HEAD_D), axis=-1).T
    dz, small["conf_ln_g"], small["conf_ln_b"] = conf_bwd_ln(s["z"], dys, w["conf_ln_g"], w["conf_ln_b"])
    dval, dgate, small["conf_dw"] = conf_bwd_conv(s["proj"], dz, w["conf_dw"])
    dproj = jnp.concatenate([dab, dac, dax, dqkv, du, dv, dval, dgate], axis=1)

    ns_in = N_IN // N_SH
    gin = tn_matmul("dw_in", s["h"], dproj, (N_SH, nr), bs((tr, D_MODEL), lambda j, r: (r, 0)),
                    bs((tr, ns_in), lambda j, r: (r, j)), bs((None, D_MODEL, ns_in), lambda j, r: (j, 0, 0)),
                    _sds((N_SH, D_MODEL, ns_in), F32), after=hooks.run("pre_dw_in", dproj))
    hooks.run("end", gin)
    big = [ga, gfo, gb, gin, gfi]
    dx, small["g_mix"] = norm_bwd(
        "mix_norm_bwd",
        [(dpre_m, lambda tm: bs((N_BR, tm, D_MODEL), lambda i: (0, i, 0)), w["wg"],
          lambda a, wr: [(a[k], wr[k]) for k in range(N_BR)]),
         (dproj, lambda tm: bs((tm, N_IN), lambda i: (i, 0)), w["win"],
          lambda a, wr: [(a[:, k * ns_in:(k + 1) * ns_in], wr[k]) for k in range(N_SH)])],
        dx1, s["x"], w["g_mix"])
    return dx, big, small


BIG_NAMES = ("w_in", "w_branch", "w_merge_gate", "w_out", "w_ffn_in", "w_ffn_out", "w_ple_gate", "w_ple_proj")


def unpack_big_grads(ga, gfo, gb, gin, gfi):
    return dict(w_in=gin, w_ffn_in=gfi, w_ffn_out=gfo,
                w_merge_gate=ga[:N_BR * BW].reshape(N_BR, BW, D_MODEL), w_out=ga[N_BR * BW:5 * BW], w_ple_gate=ga[5 * BW:],
                w_branch=gb[:N_BR * BW].reshape(N_BR, BW, BW), w_ple_proj=gb[N_BR * BW:])


SMALL_NAMES = ("g_mix", "conv_a", "sgu_ln_g", "sgu_ln_b", "sgu_w", "sgu_b", "conf_dw", "conf_ln_g", "conf_ln_b",
               "g_ffn", "g_ple")


def _pack_rows(arrays, rows):
    flat = jnp.concatenate([a.reshape(-1) for a in arrays])
    return jnp.pad(flat, (0, rows * D_MODEL - flat.shape[0])).reshape(rows, D_MODEL)


def _unpack_rows(packed, shapes):
    flat, out, pos = packed.reshape(-1), [], 0
    for shape in shapes:
        n = math.prod(shape)
        out.append(flat[pos:pos + n].reshape(shape))
        pos += n
    return out


def kernel(x, p, g_mix, w_in, conv_a, sgu_ln_g, sgu_ln_b, sgu_w, sgu_b, conf_dw, conf_ln_g, conf_ln_b, w_branch, w_merge_gate, w_out, g_ffn, w_ffn_in, w_ffn_out, g_ple, w_ple_gate, w_ple_proj, g_final, loss_target, m_g_mix, m_w_in, m_conv_a, m_sgu_ln_g, m_sgu_ln_b, m_sgu_w, m_sgu_b, m_conf_dw, m_conf_ln_g, m_conf_ln_b, m_w_branch, m_w_merge_gate, m_w_out, m_g_ffn, m_w_ffn_in, m_w_ffn_out, m_g_ple, m_w_ple_gate, m_w_ple_proj, m_g_final, v_g_mix, v_w_in, v_conv_a, v_sgu_ln_g, v_sgu_ln_b, v_sgu_w, v_sgu_b, v_conf_dw, v_conf_ln_g, v_conf_ln_b, v_w_branch, v_w_merge_gate, v_w_out, v_g_ffn, v_w_ffn_in, v_w_ffn_out, v_g_ple, v_w_ple_gate, v_w_ple_proj, v_g_final):
    weights = dict(g_mix=g_mix, w_in=w_in, conv_a=conv_a, sgu_ln_g=sgu_ln_g, sgu_ln_b=sgu_ln_b, sgu_w=sgu_w, sgu_b=sgu_b,
                   conf_dw=conf_dw, conf_ln_g=conf_ln_g, conf_ln_b=conf_ln_b, w_branch=w_branch, w_merge_gate=w_merge_gate,
                   w_out=w_out, g_ffn=g_ffn, w_ffn_in=w_ffn_in, w_ffn_out=w_ffn_out, g_ple=g_ple, w_ple_gate=w_ple_gate,
                   w_ple_proj=w_ple_proj, g_final=g_final)
    m_in = dict(g_mix=m_g_mix, w_in=m_w_in, conv_a=m_conv_a, sgu_ln_g=m_sgu_ln_g, sgu_ln_b=m_sgu_ln_b, sgu_w=m_sgu_w,
                sgu_b=m_sgu_b, conf_dw=m_conf_dw, conf_ln_g=m_conf_ln_g, conf_ln_b=m_conf_ln_b, w_branch=m_w_branch,
                w_merge_gate=m_w_merge_gate, w_out=m_w_out, g_ffn=m_g_ffn, w_ffn_in=m_w_ffn_in, w_ffn_out=m_w_ffn_out,
                g_ple=m_g_ple, w_ple_gate=m_w_ple_gate, w_ple_proj=m_w_ple_proj, g_final=m_g_final)
    v_in = dict(g_mix=v_g_mix, w_in=v_w_in, conv_a=v_conv_a, sgu_ln_g=v_sgu_ln_g, sgu_ln_b=v_sgu_ln_b, sgu_w=v_sgu_w,
                sgu_b=v_sgu_b, conf_dw=v_conf_dw, conf_ln_g=v_conf_ln_g, conf_ln_b=v_conf_ln_b, w_branch=v_w_branch,
                w_merge_gate=v_w_merge_gate, w_out=v_w_out, g_ffn=v_g_ffn, w_ffn_in=v_w_ffn_in, w_ffn_out=v_w_ffn_out,
                g_ple=v_g_ple, w_ple_gate=v_w_ple_gate, w_ple_proj=v_w_ple_proj, g_final=v_g_final)
    order = ("g_mix", "w_in", "conv_a", "sgu_ln_g", "sgu_ln_b", "sgu_w", "sgu_b", "conf_dw", "conf_ln_g", "conf_ln_b",
             "w_branch", "w_merge_gate", "w_out", "g_ffn", "w_ffn_in", "w_ffn_out", "g_ple", "w_ple_gate", "w_ple_proj",
             "g_final")
    depth = g_mix.shape[0]
    xs, tgt = x[0], loss_target[0]
    cw = BW // N_SH
    my_shard = 2 * lax.axis_index("x") + lax.axis_index("y")

    conv_rows = 16
    allc = gather8(_pack_rows([conv_a, conf_dw], conv_rows), reduce=False)
    shards = [_unpack_rows(allc[2 * s], [conv_a.shape, conf_dw.shape]) for s in range(N_SH)]
    conv_a_full = jnp.concatenate([sh[0] for sh in shards], axis=-1)
    conf_dw_full = jnp.concatenate([sh[1] for sh in shards], axis=-1)

    tril = jnp.tril(jnp.ones((BLK, BLK), bool))
    def placed_shards(i):
        shards = ([w_in[i], w_branch[i]] + [w_merge_gate[i, k] for k in range(N_BR)]
                  + [w_out[i], w_ffn_in[i], w_ffn_out[i], w_ple_gate[i], w_ple_proj[i]])
        return [lax.dynamic_update_slice(jnp.zeros((N_SH,) + sh.shape, BF16), sh.astype(BF16)[None],
                                         (my_shard,) + (0,) * sh.ndim) for sh in shards]

    def small_weights(i, win):
        vec = lambda a: a[i].reshape(1, -1)
        return dict(
            win=win, g_mix=vec(g_mix), g_ffn=vec(g_ffn), g_ple=vec(g_ple), conv_a=conv_a_full[i], conf_dw=conf_dw_full[i],
            sgu_ln_g=vec(sgu_ln_g), sgu_ln_b=vec(sgu_ln_b), conf_ln_g=vec(conf_ln_g), conf_ln_b=vec(conf_ln_b),
            sgu_wt=jnp.where(tril[None], sgu_w[i], 0.0).astype(BF16),
            sgu_bf=jnp.repeat(sgu_b[i].T, HEAD_D, axis=1))

    def late_weights(got):
        return dict(wbr=got[0], wg=jnp.stack([g.reshape(D_MODEL, D_MODEL) for g in got[1:5]]),
                    wout=got[5].reshape(D_MODEL, D_MODEL), wfi=got[6], wfo=got[7].reshape(FFN_H, D_MODEL),
                    wpg=got[8].reshape(D_MODEL, D_MODEL), wpp=got[9])

    class SplitAllGather:
        def __init__(self, bufs):
            self.shapes = [b.shape for b in bufs]
            self.ici = SplitExchange("allgather_ici", bufs, allgather_ici_plan(self.shapes), 3 * len(bufs))

        def ici_start(self, after, env=None):
            return self.ici.start(after)

        def ici_wait_d2d_start(self, after, env=None):
            landed = self.ici.wait(after)
            self.d2d = SplitExchange("allgather_d2d", landed, allgather_d2d_plan(self.shapes), 3 * len(landed))
            return self.d2d.start(landed[-1])

        def d2d_wait(self, after, env=None):
            self.got = self.d2d.wait(after)
            return None

    bufs0 = placed_shards(0)
    first = SplitAllGather(bufs0[:1])
    first.d2d_wait(first.ici_wait_d2d_start(first.ici_start(xs)))
    rest = SplitAllGather(bufs0[1:])
    layers = [small_weights(0, first.got[0])]
    act, saved = xs, []
    nxt_done = None
    for i in range(depth):
        hooks = Hooks()
        if i == 0:
            hooks.add("start", rest.ici_start)
            hooks.add("pre_conf", rest.ici_wait_d2d_start)
            hooks.add("pre_merge", rest.d2d_wait)
            hooks.add("pre_merge", lambda after, env: layers[0].update(late_weights(rest.got)))
        if i + 1 < depth:
            nxt = SplitAllGather(placed_shards(i + 1))
            points = ("pre_merge", "post_ffn_in", None) if i == 0 else ("start", "post_merge", "post_ffn_in")
            hooks.add(points[0], nxt.ici_start)
            hooks.add(points[1], nxt.ici_wait_d2d_start)
            if points[2]:
                hooks.add(points[2], nxt.d2d_wait)
        act, sv = layer_fwd(act, p[i, 0], layers[i], hooks)
        saved.append(sv)
        if i + 1 < depth:
            if i == 0:
                nxt.d2d_wait(act)
            layers.append({**small_weights(i + 1, nxt.got[0]), **late_weights(nxt.got[1:])})
    loss_part, dx, dg_final = loss_head(act, g_final.reshape(1, -1), tgt)

    big_red = [None] * depth
    small_parts = [None] * depth
    pending = None
    for i in reversed(range(depth)):
        hooks = Hooks()
        result = {}
        if pending is not None:
            rs = pending[0]
            hooks.add("start", lambda after, env, rs=rs: rs.swap_start(after))
            hooks.add("pre_ffn", lambda after, env, rs=rs: rs.swap_wait_send_start(after))
            hooks.add("pre_dw_out", lambda after, env, rs=rs: rs.send_wait_share_start(after))
            hooks.add("pre_conva", lambda after, env, rs=rs, result=result: result.update(prev=rs.share_wait(after)))
        if i == 0:
            def early_start(after, env, result=result):
                result["rs"] = SplitReduceScatter([env["gfo"], env["gfi"]])
                return result["rs"].swap_start(after)

            hooks.add("pre_dw_out", early_start)
            hooks.add("pre_conva", lambda after, env, result=result: result["rs"].swap_wait_send_start(after))
            hooks.add("pre_dw_in", lambda after, env, result=result: result["rs"].send_wait_share_start(after))
            hooks.add("end", lambda after, env, result=result: result.update(early=result["rs"].share_wait(after)))
        dx, big, small_parts[i] = layer_bwd(dx, p[i, 0], layers[i], saved[i], hooks)
        if pending is not None:
            big_red[pending[1]] = unpack_big_grads(*result["prev"])
        pending = (SplitReduceScatter(big), i) if i > 0 else None
        if i == 0:
            ga, _, gb, gin, _ = big
            ga, gb, gin = reduce_scatter_grads([ga, gb, gin])
            gfo, gfi = result["early"]
            big_red[0] = unpack_big_grads(ga, gfo, gb, gin, gfi)

    small_list = [jnp.stack([small_parts[i][n].reshape(weights[n].shape[1:] if n not in ("conv_a", "conf_dw")
                                                       else small_parts[i][n].shape) for i in range(depth)])
                  for n in SMALL_NAMES]
    small_list += [dg_final.reshape(-1), loss_part[0, :1]]
    small_shapes = [a.shape for a in small_list]
    n_small = sum(math.prod(sh) for sh in small_shapes)
    small_rows = -(-n_small // (8 * D_MODEL)) * 8
    red = _unpack_rows(gather8(_pack_rows(small_list, small_rows), reduce=True), small_shapes)
    grads = dict(zip(SMALL_NAMES, red[:len(SMALL_NAMES)]))
    grads["g_final"] = red[-2]
    loss = red[-1].reshape(())
    for n in ("conv_a", "conf_dw"):
        grads[n] = lax.dynamic_slice_in_dim(grads[n], my_shard * cw, cw, axis=2)
    for name in BIG_NAMES:
        grads[name] = jnp.stack([big_red[i][name] for i in range(depth)])

    small_all = [n for n in order if n not in BIG_NAMES]
    sm_shapes = [weights[n].shape for n in small_all]
    n_sm = sum(math.prod(sh) for sh in sm_shapes)
    sm_rows = -(-n_sm // (8 * D_MODEL)) * 8
    packed = [_pack_rows([src[n] for n in small_all], sm_rows) for src in (weights, grads, m_in, v_in)]
    sm_out = [_unpack_rows(o, sm_shapes) for o in adamw(*packed)]
    delta, new_m, new_v = ({n: o[k] for k, n in enumerate(small_all)} for o in sm_out)
    for name in BIG_NAMES:
        delta[name], new_m[name], new_v[name] = adamw(weights[name], grads[name], m_in[name], v_in[name])

    return (loss, dx[None], *[grads[n] for n in order], *[delta[n] for n in order], *[new_m[n] for n in order],
            *[new_v[n] for n in order])
```

```python
import functools
import math

import jax
import jax.numpy as jnp
from jax import lax
from jax.experimental import pallas as pl
from jax.experimental.pallas import tpu as pltpu

F32 = jnp.float32
BF16 = jnp.bfloat16
EPS = 1e-6
D_MODEL = 1024
BW = 256
N_BR = 4
N_IN = 10 * BW
FFN_H = 2816
N_SH = 4
HEADS = 4
HEAD_D = 64
BLK = 128
DILATIONS = (1, 4, 16)
CONF_K = 31
CONVA_K = 3
NEG = -1e30
VMEM_LIMIT = 56 * 1024 * 1024
MESH = pl.DeviceIdType.MESH

ADAM_LR, ADAM_B1, ADAM_B2, ADAM_EPS, ADAM_WD, ADAM_STEP = 0.001, 0.9, 0.999, 1e-08, 0.01, 10

bs = pl.BlockSpec
ANY = pl.BlockSpec(memory_space=pl.ANY)


def _call(body, name, grid, in_specs, out_specs, out_shape, scratch=(), aliases=None, after=None):
    n_in = len(in_specs)
    kernel_body = body
    if after is not None:
        in_specs = list(in_specs) + [ANY]

        def kernel_body(*refs):
            return body(*refs[:n_in], *refs[n_in + 1:])

    call = pl.pallas_call(
        kernel_body, name=name, grid=grid, in_specs=in_specs, out_specs=out_specs, out_shape=out_shape,
        scratch_shapes=list(scratch), input_output_aliases=aliases or {},
        compiler_params=pltpu.CompilerParams(dimension_semantics=("arbitrary",) * len(grid),
                                             vmem_limit_bytes=VMEM_LIMIT))
    return call if after is None else (lambda *args: call(*args, after))


def _sds(shape, dtype):
    return jax.ShapeDtypeStruct(shape, dtype)


def _nn(a, b):
    return jnp.dot(a, b, preferred_element_type=F32)


def _nt(a, b):
    return lax.dot_general(a, b, (((1,), (1,)), ((), ())), preferred_element_type=F32)


def _tn(a, b):
    return lax.dot_general(a, b, (((0,), (0,)), ((), ())), preferred_element_type=F32)


def _sigmoid(x):
    return 1.0 / (1.0 + jnp.exp(-x))


def _rms_fwd(x, g):
    r = lax.rsqrt(jnp.mean(x * x, axis=-1, keepdims=True) + EPS)
    return x * r * g


def _rms_bwd(dh, x, g):
    r = lax.rsqrt(jnp.mean(x * x, axis=-1, keepdims=True) + EPS)
    xr = x * r
    dxr = dh * g
    dx = r * (dxr - xr * jnp.mean(dxr * xr, axis=-1, keepdims=True))
    return dx, dh * xr


def _ln_hat(x):
    mu = jnp.mean(x, axis=-1, keepdims=True)
    xc = x - mu
    r = lax.rsqrt(jnp.mean(xc * xc, axis=-1, keepdims=True) + EPS)
    return xc * r, r


def _ln_bwd(dy, xhat, r, g):
    dxh = dy * g
    return r * (dxh - jnp.mean(dxh, axis=-1, keepdims=True) - xhat * jnp.mean(dxh * xhat, axis=-1, keepdims=True))


def _colsum(v):
    return jnp.sum(v, axis=0, keepdims=True)


def _causal_conv(zext, w_ref, k_taps, halo):
    acc = zext[halo:] * w_ref[k_taps - 1:k_taps, :]
    for k in range(k_taps - 1):
        acc = acc + pltpu.roll(zext, k_taps - 1 - k, 0)[halo:] * w_ref[k:k + 1, :]
    return acc


def _anti_conv(dext, w_ref, k_taps, tm):
    n = dext.shape[0]
    acc = dext[:tm] * w_ref[k_taps - 1:k_taps, :]
    for s in range(1, k_taps):
        acc = acc + pltpu.roll(dext, n - s, 0)[:tm] * w_ref[k_taps - 1 - s:k_taps - s, :]
    return acc


def _conv_wgrad(dw_ref, dc, zext, k_taps, halo):
    dw_ref[k_taps - 1:k_taps, :] += _colsum(dc * zext[halo:])
    for k in range(k_taps - 1):
        dw_ref[k:k + 1, :] += _colsum(dc * pltpu.roll(zext, k_taps - 1 - k, 0)[halo:])


LANES = 128


def _to_strided_view(dst_ref, chunk, scr, d, width):
    n = scr.shape[0] // d
    for c in range(width // LANES):
        scr[...] = chunk(c)
        for r in range(d):
            dst_ref[:, r * width + c * LANES:r * width + (c + 1) * LANES] = scr[pl.ds(r, n, stride=d), :].astype(dst_ref.dtype)


def _from_strided_view(src_ref, scr, d, width, c):
    n = scr.shape[0] // d
    for r in range(d):
        scr[pl.ds(r, n, stride=d), :] = src_ref[:, r * width + c * LANES:r * width + (c + 1) * LANES].astype(F32)
    return scr[...]


def _view_spec(tm, d, width):
    return bs((tm // d, d * width), lambda i: (i, 0))


def _prev_blk(i, per):
    return jnp.maximum(i * per - 1, 0)


def _next_blk(i, per, last):
    return jnp.minimum((i + 1) * per, last)


def norm_in_proj(x, g, win, after=None):
    t = x.shape[0]
    tm = min(512, t)
    ns = win.shape[2]

    def body(x_ref, g_ref, w_ref, h_ref, o_ref, q_ref, q4_ref, q16_ref, scr):
        h = _rms_fwd(x_ref[...], g_ref[...]).astype(BF16)
        h_ref[...] = h
        parts = []
        for s in range(N_SH):
            r = _nn(h, w_ref[s])
            o_ref[:, s * ns:(s + 1) * ns] = r
            if s == 1:
                parts.append(r[:, 3 * BW - ns:])
            if s == 2:
                parts.append(r[:, :6 * BW - 2 * ns])
        qf = jnp.concatenate(parts, axis=1)
        q_ref[...] = qf.astype(BF16)
        chunk = lambda c: qf[:, c * LANES:(c + 1) * LANES]
        _to_strided_view(q4_ref, chunk, scr, 4, 3 * BW)
        _to_strided_view(q16_ref, chunk, scr, 16, 3 * BW)

    row = lambda c: bs((tm, c), lambda i: (i, 0))
    return _call(
        body, "norm_in_proj", (t // tm,), [row(D_MODEL), bs((1, D_MODEL), lambda i: (0, 0)), _resident(win)],
        [row(D_MODEL), row(N_IN), row(3 * BW), _view_spec(tm, 4, 3 * BW), _view_spec(tm, 16, 3 * BW)],
        [_sds((t, D_MODEL), BF16), _sds((t, N_IN), F32), _sds((t, 3 * BW), BF16),
         _sds((t // 4, 4 * 3 * BW), BF16), _sds((t // 16, 16 * 3 * BW), BF16)],
        scratch=[pltpu.VMEM((tm, LANES), F32)], after=after)(x, g, win)


def merge_fwd(h, ys, wg, wbr, after=None):
    t = h.shape[0]
    tm = min(512, t)

    def body(h_ref, ya, yb, yc, yd, wg_ref, wb_ref, m_ref, g_ref, b_ref):
        hh = h_ref[...]
        for j in range(N_SH):
            cs = slice(j * BW, (j + 1) * BW)
            acc = None
            for k, y_ref in enumerate((ya, yb, yc, yd)):
                g = _sigmoid(_nn(hh, wg_ref[k, :, cs]))
                b = _nn(y_ref[...], wb_ref[j, k])
                g_ref[k, :, cs] = g.astype(BF16)
                b_ref[k, :, cs] = b.astype(BF16)
                acc = g * b if acc is None else acc + g * b
            m_ref[:, cs] = acc.astype(BF16)

    ysp = bs((tm, BW), lambda i: (i, 0))
    big = bs((N_BR, tm, D_MODEL), lambda i: (0, i, 0))
    return _call(
        body, "merge_fwd", (t // tm,),
        [bs((tm, D_MODEL), lambda i: (i, 0)), ysp, ysp, ysp, ysp, _resident(wg), _resident(wbr)],
        [bs((tm, D_MODEL), lambda i: (i, 0)), big, big],
        [_sds((t, D_MODEL), BF16), _sds((N_BR, t, D_MODEL), BF16), _sds((N_BR, t, D_MODEL), BF16)], after=after)(
            h, *ys, wg, wbr)


def mm_residual(a, w, res, name, after=None):
    t, kk = a.shape
    tm = min(512, t)

    def body(a_ref, w_ref, r_ref, o_ref):
        o_ref[...] = r_ref[...] + _nn(a_ref[...], w_ref[...])

    row = bs((tm, D_MODEL), lambda i: (i, 0))
    return _call(body, name, (t // tm,), [bs((tm, kk), lambda i: (i, 0)), _resident(w), row], row,
                 _sds((t, D_MODEL), F32), after=after)(a, w, res)


def ffn_in(x, g, wfi):
    t = x.shape[0]
    tm = min(512, t)
    ns = wfi.shape[2]

    def body(x_ref, g_ref, w_ref, h_ref, f_ref, a_ref):
        h = _rms_fwd(x_ref[...], g_ref[...]).astype(BF16)
        h_ref[...] = h
        for j in range(2):
            cs = slice(j * ns, (j + 1) * ns)
            fg = _nn(h, w_ref[j])
            fu = _nn(h, w_ref[j + 2])
            f_ref[0, :, cs] = fg.astype(BF16)
            f_ref[1, :, cs] = fu.astype(BF16)
            a_ref[:, cs] = (fg * _sigmoid(fg) * fu).astype(BF16)

    row = lambda c: bs((tm, c), lambda i: (i, 0))
    return _call(
        body, "ffn_in", (t // tm,), [row(D_MODEL), bs((1, D_MODEL), lambda i: (0, 0)), _resident(wfi)],
        [row(D_MODEL), bs((2, tm, FFN_H), lambda i: (0, i, 0)), row(FFN_H)],
        [_sds((t, D_MODEL), BF16), _sds((2, t, FFN_H), BF16), _sds((t, FFN_H), BF16)])(x, g, wfi)


def ple_fwd(x, g, wpg, p_i, wpp):
    t = x.shape[0]
    tm = min(512, t)

    def body(x_ref, g_ref, wg_ref, p_ref, wp_ref, h_ref, gt_ref, pp_ref, o_ref):
        xv = x_ref[...]
        h = _rms_fwd(xv, g_ref[...]).astype(BF16)
        h_ref[...] = h
        gate = _sigmoid(_nn(h, wg_ref[...]))
        pb = p_ref[...].astype(BF16)
        pp = jnp.concatenate([_nn(pb, wp_ref[j]) for j in range(N_SH)], axis=1)
        gt_ref[...] = gate.astype(BF16)
        pp_ref[...] = pp.astype(BF16)
        o_ref[...] = xv + gate * pp

    row = bs((tm, D_MODEL), lambda i: (i, 0))
    return _call(
        body, "ple_fwd", (t // tm,),
        [row, bs((1, D_MODEL), lambda i: (0, 0)), _resident(wpg), bs((tm, BW), lambda i: (i, 0)), _resident(wpp)],
        [row, row, row, row],
        [_sds((t, D_MODEL), BF16), _sds((t, D_MODEL), BF16), _sds((t, D_MODEL), BF16), _sds((t, D_MODEL), F32)])(
            x, g, wpg, p_i, wpp)


def loss_head(x, g, tgt):
    t = x.shape[0]
    tm = min(512, t)

    def body(x_ref, g_ref, t_ref, l_ref, dx_ref, dg_ref):
        @pl.when(pl.program_id(0) == 0)
        def _():
            l_ref[...] = jnp.zeros_like(l_ref)
            dg_ref[...] = jnp.zeros_like(dg_ref)

        xv, gv = x_ref[...], g_ref[...]
        err = _rms_fwd(xv, gv) - t_ref[...]
        part = 0.5 * jnp.sum(jnp.mean(err * err, axis=-1, keepdims=True), axis=0, keepdims=True)
        l_ref[...] += jnp.broadcast_to(part, l_ref.shape)
        dx, dgr = _rms_bwd(err * (1.0 / D_MODEL), xv, gv)
        dx_ref[...] = dx
        dg_ref[...] += _colsum(dgr)

    row = bs((tm, D_MODEL), lambda i: (i, 0))
    vec = bs((1, D_MODEL), lambda i: (0, 0))
    return _call(body, "loss_head", (t // tm,), [row, vec, row],
                 [bs((1, 128), lambda i: (0, 0)), row, vec],
                 [_sds((1, 128), F32), _sds((t, D_MODEL), F32), _sds((1, D_MODEL), F32)])(x, g, tgt)


def tn_matmul(name, a, b, grid, a_spec, b_spec, out_spec, out_shape, split=0, split_cols=0, into=None, after=None):
    last = len(grid) - 1

    def body(a_ref, b_ref, *rest):
        o_ref = rest[-1]

        @pl.when(pl.program_id(last) == 0)
        def _():
            o_ref[...] = jnp.zeros_like(o_ref)

        res = _tn(a_ref[...].astype(BF16), b_ref[...].astype(BF16))
        if split_cols:
            cols = res.shape[1] // split_cols
            for s in range(split_cols):
                o_ref[s] += res[:, s * cols:(s + 1) * cols]
        elif split:
            rows = res.shape[0] // split
            for s in range(split):
                o_ref[s] += res[s * rows:(s + 1) * rows]
        else:
            o_ref[...] += res

    if into is None:
        return _call(body, name, grid, [a_spec, b_spec], out_spec, out_shape, after=after)(a, b)
    return _call(body, name, grid, [a_spec, b_spec, ANY], out_spec, out_shape, aliases={2: 0}, after=after)(a, b, into)


def _resident(w):
    zeros = (0,) * w.ndim
    return bs(w.shape, lambda i: zeros, pipeline_mode=pl.Buffered(1))


def norm_bwd(name, sources, dx_in, x, g):
    t = x.shape[0]
    tm = min(512, t)
    n_src = len(sources)

    def body(*refs):
        dxi_ref, x_ref, g_ref, dx_ref, dg_ref = refs[2 * n_src:]

        @pl.when(pl.program_id(0) == 0)
        def _():
            dg_ref[...] = jnp.zeros_like(dg_ref)

        dh = None
        for si in range(n_src):
            for av, wv in sources[si][3](refs[2 * si], refs[2 * si + 1]):
                part = _nt(av, wv)
                dh = part if dh is None else dh + part
        dx, dgr = _rms_bwd(dh, x_ref[...], g_ref[...])
        dx_ref[...] = dxi_ref[...] + dx
        dg_ref[...] += _colsum(dgr)

    in_specs, args = [], []
    for a, a_spec, w, _ in sources:
        in_specs += [a_spec(tm), _resident(w)]
        args += [a, w]
    row = bs((tm, D_MODEL), lambda i: (i, 0))
    vec = bs((1, D_MODEL), lambda i: (0, 0))
    return _call(body, name, (t // tm,), in_specs + [row, row, vec], [row, vec],
                 [_sds((t, D_MODEL), F32), _sds((1, D_MODEL), F32)])(*args, dx_in, x, g)


def ple_bwd_pre(dx, gate, pp, after=None):
    t = dx.shape[0]
    tm = min(1024, t)

    def body(dx_ref, g_ref, p_ref, dpre_ref, dpp_ref):
        d = dx_ref[...]
        g = g_ref[...].astype(F32)
        dpre_ref[...] = (d * p_ref[...].astype(F32) * g * (1.0 - g)).astype(BF16)
        dpp_ref[...] = (d * g).astype(BF16)

    row = bs((tm, D_MODEL), lambda i: (i, 0))
    return _call(body, "ple_bwd_pre", (t // tm,), [row, row, row], [row, row],
                 [_sds((t, D_MODEL), BF16), _sds((t, D_MODEL), BF16)], after=after)(dx, gate, pp)


def ffn_bwd_act(dx, wfo, fgu, after=None):
    t = dx.shape[0]
    tm = min(512, t)
    ns = FFN_H // 2

    def body(dx_ref, w_ref, f_ref, o_ref):
        dxb = dx_ref[...].astype(BF16)
        for j in range(2):
            cs = slice(j * ns, (j + 1) * ns)
            dact = _nt(dxb, w_ref[cs, :])
            fg = f_ref[0, :, cs].astype(F32)
            fu = f_ref[1, :, cs].astype(F32)
            s = _sigmoid(fg)
            o_ref[0, :, cs] = (dact * fu * (s * (1.0 + fg * (1.0 - s)))).astype(BF16)
            o_ref[1, :, cs] = (dact * fg * s).astype(BF16)

    blk = bs((2, tm, FFN_H), lambda i: (0, i, 0))
    return _call(body, "ffn_bwd_act", (t // tm,), [bs((tm, D_MODEL), lambda i: (i, 0)), _resident(wfo), blk],
                 blk, _sds((2, t, FFN_H), BF16), after=after)(dx, wfo, fgu)


def merge_bwd(dx, wout, gates, ybr, wbr):
    t = dx.shape[0]
    tm = min(256, t)

    def body(dx_ref, w_ref, g_ref, b_ref, wb_ref, dpre_ref, dyb_ref, dy_ref):
        dm = _nt(dx_ref[...].astype(BF16), w_ref[...])
        for k in range(N_BR):
            g = g_ref[k].astype(F32)
            dpre_ref[k] = (dm * b_ref[k].astype(F32) * g * (1.0 - g)).astype(BF16)
            dyb = (dm * g).astype(BF16)
            dyb_ref[k] = dyb
            acc = None
            for s in range(N_SH):
                part = _nt(dyb[:, s * BW:(s + 1) * BW], wb_ref[s, k])
                acc = part if acc is None else acc + part
            dy_ref[k] = acc

    blk = bs((N_BR, tm, D_MODEL), lambda i: (0, i, 0))
    return _call(body, "merge_bwd", (t // tm,),
                 [bs((tm, D_MODEL), lambda i: (i, 0)), _resident(wout), blk, blk, _resident(wbr)],
                 [blk, blk, bs((N_BR, tm, BW), lambda i: (0, i, 0))],
                 [_sds((N_BR, t, D_MODEL), BF16), _sds((N_BR, t, D_MODEL), BF16), _sds((N_BR, t, BW), F32)])(
                     dx, wout, gates, ybr, wbr)


def conva_fwd(proj, wa):
    t = proj.shape[0]
    tm, halo = min(512, t), 8
    per = tm // halo

    def body(b_ref, c_ref, x_ref, ch_ref, xh_ref, w_ref, y_ref):
        zh = jnp.where(pl.program_id(0) > 0, ch_ref[...] * xh_ref[...], 0.0)
        zext = jnp.concatenate([zh, c_ref[...] * x_ref[...]], axis=0)
        y_ref[...] = (b_ref[...] * _causal_conv(zext, w_ref, CONVA_K, halo)).astype(BF16)

    col = lambda c: bs((tm, BW), lambda i: (i, c))
    hal = lambda c: bs((halo, BW), lambda i: (_prev_blk(i, per), c))
    return _call(body, "conva_fwd", (t // tm,),
                 [col(0), col(1), col(2), hal(1), hal(2), bs((CONVA_K, BW), lambda i: (0, 0))],
                 bs((tm, BW), lambda i: (i, 0)), _sds((t, BW), BF16))(proj, proj, proj, proj, proj, wa)


def conva_bwd(proj, dys, wa, after=None):
    t = proj.shape[0]
    tm, halo = min(512, t), 8
    per = tm // halo
    last = t // halo - 1
    nt = t // tm

    def body(b_ref, c_ref, x_ref, ch_ref, xh_ref, bn_ref, dy_ref, dyn_ref, w_ref, db_ref, dc_ref, dxx_ref, dw_ref):
        i = pl.program_id(0)

        @pl.when(i == 0)
        def _():
            dw_ref[...] = jnp.zeros_like(dw_ref)

        zh = jnp.where(i > 0, ch_ref[...] * xh_ref[...], 0.0)
        cv, xv = c_ref[...], x_ref[...]
        zext = jnp.concatenate([zh, cv * xv], axis=0)
        dy = dy_ref[...]
        dconv = dy * b_ref[...]
        dcn = jnp.where(i < nt - 1, dyn_ref[...] * bn_ref[...], 0.0)
        dz = _anti_conv(jnp.concatenate([dconv, dcn], axis=0), w_ref, CONVA_K, tm)
        db_ref[...] = (dy * _causal_conv(zext, w_ref, CONVA_K, halo)).astype(BF16)
        dc_ref[...] = (dz * xv).astype(BF16)
        dxx_ref[...] = (dz * cv).astype(BF16)
        _conv_wgrad(dw_ref, dconv, zext, CONVA_K, halo)

    col = lambda c: bs((tm, BW), lambda i: (i, c))
    hal = lambda c: bs((halo, BW), lambda i: (_prev_blk(i, per), c))
    nxt = bs((halo, BW), lambda i: (_next_blk(i, per, last), 0))
    wsp = bs((CONVA_K, BW), lambda i: (0, 0))
    outs = _call(body, "conva_bwd", (t // tm,),
                 [col(0), col(1), col(2), hal(1), hal(2), nxt,
                  bs((None, tm, BW), lambda i: (0, i, 0)), bs((None, halo, BW), lambda i: (0, _next_blk(i, per, last), 0)), wsp],
                 [bs((tm, BW), lambda i: (i, 0))] * 3 + [wsp],
                 [_sds((t, BW), BF16)] * 3 + [_sds((CONVA_K, BW), F32)], after=after)(
                     proj, proj, proj, proj, proj, proj, dys, dys, wa)
    return outs[:3], outs[3]


def _head_masks():
    lane = lax.broadcasted_iota(jnp.int32, (1, BW), 1)
    return [(lane >= h * HEAD_D) & (lane < (h + 1) * HEAD_D) for h in range(HEADS)]


def _band_masks():
    qi = lax.broadcasted_iota(jnp.int32, (BLK, BLK), 0)
    ki = lax.broadcasted_iota(jnp.int32, (BLK, BLK), 1)
    return ki >= qi, ki <= qi


def attn_fwd_group(pv, d):
    rows = pv.shape[0]
    qb = min(512, rows)
    nb = qb // BLK
    scale = HEAD_D ** -0.5

    def body(q_ref, k_ref, v_ref, kh_ref, vh_ref, o_ref):
        n = pl.program_id(1)
        hm = _head_masks()
        m_prev, m_cur = _band_masks()
        for b in range(nb):
            rs = slice(b * BLK, (b + 1) * BLK)
            q = q_ref[rs, :]
            if b == 0:
                kp, vp = kh_ref[...], vh_ref[...]
                mp = m_prev & (n > 0)
            else:
                ps = slice((b - 1) * BLK, b * BLK)
                kp, vp = k_ref[ps, :], v_ref[ps, :]
                mp = m_prev
            qs = jnp.concatenate([jnp.where(hm[h], q, 0.0).astype(BF16) for h in range(HEADS)], axis=0)
            kcat = jnp.concatenate([kp, k_ref[rs, :]], axis=0)
            vcat = jnp.concatenate([vp, v_ref[rs, :]], axis=0)
            band = jnp.concatenate([mp, m_cur], axis=1)
            s = jnp.where(jnp.concatenate([band] * HEADS, axis=0), _nt(qs, kcat) * scale, NEG)
            m = jnp.max(s, axis=-1, keepdims=True)
            e = jnp.exp(s - m)
            l = jnp.sum(e, axis=-1, keepdims=True)
            of = _nn(e.astype(BF16), vcat) / l
            lse = m + jnp.log(l)
            o_acc = jnp.zeros((BLK, BW), F32)
            l_acc = jnp.zeros((BLK, BW), F32)
            for h in range(HEADS):
                hs = slice(h * BLK, (h + 1) * BLK)
                o_acc = jnp.where(hm[h], of[hs, :], o_acc)
                l_acc = jnp.where(hm[h], lse[hs, :], l_acc)
            o_ref[rs, :BW] = o_acc
            o_ref[rs, BW:] = l_acc

    per = qb // BLK
    main = lambda c: bs((qb, BW), lambda r, n: (n, r * 3 + c))
    hal = lambda c: bs((BLK, BW), lambda r, n: (_prev_blk(n, per), r * 3 + c))
    return _call(body, f"attn_fwd_d{d}", (d, rows // qb), [main(0), main(1), main(2), hal(1), hal(2)],
                 bs((qb, 2 * BW), lambda r, n: (n, r)), _sds((rows, d * 2 * BW), F32))(pv, pv, pv, pv, pv)


def attn_merge(ols):
    t = ols[0].shape[0]
    tm = min(512, t)
    width = 2 * BW

    def lse3(a, b, c):
        m = jnp.maximum(jnp.maximum(a, b), c)
        return m + jnp.log(jnp.exp(a - m) + jnp.exp(b - m) + jnp.exp(c - m))

    def body(g0, g1, g2, y_ref, o_ref, l_ref, scr, nat1, nat2):
        for src, nat, d in ((g1, nat1, DILATIONS[1]), (g2, nat2, DILATIONS[2])):
            for c in range(width // LANES):
                nat[:, c * LANES:(c + 1) * LANES] = _from_strided_view(src, scr, d, width, c)
        gs = [g0[...], nat1[...], nat2[...]]
        ls = [g[:, BW:] for g in gs]
        tot = lse3(*ls)
        o = (jnp.exp(ls[0] - tot) * gs[0][:, :BW] + jnp.exp(ls[1] - tot) * gs[1][:, :BW]
             + jnp.exp(ls[2] - tot) * gs[2][:, :BW])
        y_ref[...] = o.astype(BF16)
        o_ref[...] = o
        l_ref[...] = tot

    n = bs((tm, BW), lambda i: (i, 0))
    return _call(body, "attn_merge", (t // tm,),
                 [_view_spec(tm, 1, width), _view_spec(tm, DILATIONS[1], width), _view_spec(tm, DILATIONS[2], width)],
                 [n, n, n], [_sds((t, BW), BF16), _sds((t, BW), F32), _sds((t, BW), F32)],
                 scratch=[pltpu.VMEM((tm, LANES), F32), pltpu.VMEM((tm, width), F32), pltpu.VMEM((tm, width), F32)])(*ols)


def attn_delta(dys, o, lse):
    t = o.shape[0]
    tm = min(512, t)

    def body(d_ref, o_ref, l_ref, ld1, ld4, ld16, dy4, dy16, scr):
        hm = _head_masks()
        dy = d_ref[...]
        prod = dy * o_ref[...]
        delta = jnp.zeros_like(prod)
        for h in range(HEADS):
            delta = jnp.where(hm[h], jnp.sum(jnp.where(hm[h], prod, 0.0), axis=-1, keepdims=True), delta)
        ld = jnp.concatenate([l_ref[...], delta], axis=1)
        ld1[...] = ld
        for d, ld_v, dy_v in ((DILATIONS[1], ld4, dy4), (DILATIONS[2], ld16, dy16)):
            _to_strided_view(ld_v, lambda c: ld[:, c * LANES:(c + 1) * LANES], scr, d, 2 * BW)
            _to_strided_view(dy_v, lambda c: dy[:, c * LANES:(c + 1) * LANES], scr, d, BW)

    n = bs((tm, BW), lambda i: (i, 0))
    d4, d16 = DILATIONS[1], DILATIONS[2]
    outs = _call(body, "attn_delta", (t // tm,), [bs((None, tm, BW), lambda i: (1, i, 0)), n, n],
                 [_view_spec(tm, 1, 2 * BW), _view_spec(tm, d4, 2 * BW), _view_spec(tm, d16, 2 * BW),
                  _view_spec(tm, d4, BW), _view_spec(tm, d16, BW)],
                 [_sds((t, 2 * BW), F32), _sds((t // d4, d4 * 2 * BW), F32), _sds((t // d16, d16 * 2 * BW), F32),
                  _sds((t // d4, d4 * BW), F32), _sds((t // d16, d16 * BW), F32)],
                 scratch=[pltpu.VMEM((tm, LANES), F32)])(dys, o, lse)
    return outs[:3], outs[3:]


def attn_bwd_group(pv, dov, ldv, d):
    rows = pv.shape[0]
    qb = min(512, rows)
    nb = qb // BLK
    nsteps = rows // qb
    scale = HEAD_D ** -0.5

    def body(q_ref, qn_ref, k_ref, kh_ref, v_ref, vh_ref, do_ref, don_ref, ld_ref, ldn_ref, o_ref):
        n = pl.program_id(1)
        hm = _head_masks()
        m_prev, m_cur = _band_masks()
        has_prev, has_next = n > 0, n < nsteps - 1
        dq = [None] * nb
        dk = [jnp.zeros((BLK, BW), F32) for _ in range(nb)]
        dvv = [jnp.zeros((BLK, BW), F32) for _ in range(nb)]
        for qi in range(nb + 1):
            rs = slice(qi * BLK, (qi + 1) * BLK)
            ps = slice((qi - 1) * BLK, qi * BLK)
            if qi < nb:
                q, do, ldq = q_ref[rs, :], do_ref[rs, :], ld_ref[rs, :]
            else:
                q, do, ldq = qn_ref[...], don_ref[...], ldn_ref[...]
            kp, vp = (kh_ref[...], vh_ref[...]) if qi == 0 else (k_ref[ps, :], v_ref[ps, :])
            kc, vc = (k_ref[rs, :], v_ref[rs, :]) if qi < nb else (kp, vp)
            mp = m_prev & has_prev if qi == 0 else (m_prev & has_next if qi == nb else m_prev)
            mc = m_cur if qi < nb else jnp.zeros_like(m_cur)
            band = jnp.concatenate([jnp.concatenate([mp, mc], axis=1)] * HEADS, axis=0)
            qs = jnp.concatenate([jnp.where(hm[h], q, 0.0).astype(BF16) for h in range(HEADS)], axis=0)
            dos = jnp.concatenate([jnp.where(hm[h], do, 0.0).astype(BF16) for h in range(HEADS)], axis=0)
            kcat = jnp.concatenate([kp, kc], axis=0)
            vcat = jnp.concatenate([vp, vc], axis=0)
            col = lambda v, h: jnp.broadcast_to(jnp.max(jnp.where(hm[h], v, NEG), axis=-1, keepdims=True), (BLK, 2 * BLK))
            lcols = jnp.concatenate([col(ldq[:, :BW], h) for h in range(HEADS)], axis=0)
            dcols = jnp.concatenate([col(ldq[:, BW:], h) for h in range(HEADS)], axis=0)
            p = jnp.where(band, jnp.exp(_nt(qs, kcat) * scale - lcols), 0.0)
            ds = (p * (_nt(dos, vcat) - dcols) * scale).astype(BF16)
            if qi < nb:
                dqf = _nn(ds, kcat)
                acc_q = jnp.zeros((BLK, BW), F32)
                for h in range(HEADS):
                    acc_q = jnp.where(hm[h], dqf[h * BLK:(h + 1) * BLK, :], acc_q)
                dq[qi] = acc_q
            dkc = _tn(ds, qs)
            dvc = _tn(p.astype(BF16), dos)
            if qi >= 1:
                dk[qi - 1] = dk[qi - 1] + dkc[:BLK]
                dvv[qi - 1] = dvv[qi - 1] + dvc[:BLK]
            if qi < nb:
                dk[qi] = dk[qi] + dkc[BLK:]
                dvv[qi] = dvv[qi] + dvc[BLK:]
        for b in range(nb):
            rs = slice(b * BLK, (b + 1) * BLK)
            for c, val in enumerate((dq[b], dk[b], dvv[b])):
                cs = slice(c * BW, (c + 1) * BW)
                o_ref[rs, cs] = val

    per = qb // BLK
    last = rows // BLK - 1
    main = lambda c: bs((qb, BW), lambda r, n: (n, r * 3 + c))
    prv = lambda c: bs((BLK, BW), lambda r, n: (_prev_blk(n, per), r * 3 + c))
    nxt = lambda c: bs((BLK, BW), lambda r, n: (_next_blk(n, per, last), r * 3 + c))
    accs = bs((qb, 3 * BW), lambda r, n: (n, r))
    in_specs = [main(0), nxt(0), main(1), prv(1), main(2), prv(2),
                bs((qb, BW), lambda r, n: (n, r)), bs((BLK, BW), lambda r, n: (_next_blk(n, per, last), r)),
                bs((qb, 2 * BW), lambda r, n: (n, r)), bs((BLK, 2 * BW), lambda r, n: (_next_blk(n, per, last), r))]
    args = [pv, pv, pv, pv, pv, pv, dov, dov, ldv, ldv]
    return _call(body, f"attn_bwd_d{d}", (d, nsteps), in_specs, accs, _sds((rows, d * 3 * BW), F32))(*args)


def attn_bwd_finish(parts):
    t = parts[0].shape[0]
    tm = min(512, t)
    width = 3 * BW

    def body(g0, g1, g2, o_ref, scr):
        for c in range(width // LANES):
            cs = slice(c * LANES, (c + 1) * LANES)
            acc = g0[:, cs]
            acc = acc + _from_strided_view(g1, scr, DILATIONS[1], width, c)
            acc = acc + _from_strided_view(g2, scr, DILATIONS[2], width, c)
            o_ref[:, cs] = acc.astype(BF16)

    return _call(body, "attn_bwd_finish", (t // tm,),
                 [_view_spec(tm, 1, width), _view_spec(tm, DILATIONS[1], width), _view_spec(tm, DILATIONS[2], width)],
                 bs((tm, width), lambda i: (i, 0)), _sds((t, width), BF16),
                 scratch=[pltpu.VMEM((tm, LANES), F32)])(*parts)


def _group_masks():
    lane = lax.broadcasted_iota(jnp.int32, (1, BW), 1)
    return [(lane >= g * HEAD_D) & (lane < (g + 1) * HEAD_D) for g in range(4)]


def sgu_fwd(proj, ln_g, ln_b, w_tril, b_full):
    t = proj.shape[0]
    tm = min(512, t)

    def body(u_ref, v_ref, g_ref, b_ref, w_ref, bf_ref, y_ref):
        gm = _group_masks()
        xhat, _ = _ln_hat(v_ref[...])
        vb = (xhat * g_ref[...] + b_ref[...]).astype(BF16)
        for c in range(tm // BLK):
            rs = slice(c * BLK, (c + 1) * BLK)
            vc = vb[rs, :]
            mixed = bf_ref[...]
            for g in range(4):
                mixed = mixed + jnp.where(gm[g], _nn(w_ref[g], vc), 0.0)
            y_ref[rs, :] = (u_ref[rs, :] * mixed).astype(BF16)

    vec = bs((1, BW), lambda i: (0, 0))
    return _call(body, "sgu_fwd", (t // tm,),
                 [bs((tm, BW), lambda i: (i, 6)), bs((tm, BW), lambda i: (i, 7)), vec, vec,
                  bs((4, BLK, BLK), lambda i: (0, 0, 0)), bs((BLK, BW), lambda i: (0, 0))],
                 bs((tm, BW), lambda i: (i, 0)), _sds((t, BW), BF16))(proj, proj, ln_g, ln_b, w_tril, b_full)


def sgu_bwd(proj, dys, ln_g, ln_b, w_tril, b_full):
    t = proj.shape[0]
    tm = min(512, t)

    def body(u_ref, v_ref, dy_ref, g_ref, b_ref, w_ref, bf_ref, du_ref, dv_ref, dw_ref, dbf_ref, dg_ref, db_ref, dvl_ref):
        @pl.when(pl.program_id(0) == 0)
        def _():
            dw_ref[...] = jnp.zeros_like(dw_ref)
            dbf_ref[...] = jnp.zeros_like(dbf_ref)
            dg_ref[...] = jnp.zeros_like(dg_ref)
            db_ref[...] = jnp.zeros_like(db_ref)

        gm = _group_masks()
        xhat, r = _ln_hat(v_ref[...])
        gv = g_ref[...]
        vb = (xhat * gv + b_ref[...]).astype(BF16)
        for c in range(tm // BLK):
            rs = slice(c * BLK, (c + 1) * BLK)
            vc = vb[rs, :]
            dy = dy_ref[rs, :]
            mixed = bf_ref[...]
            for g in range(4):
                mixed = mixed + jnp.where(gm[g], _nn(w_ref[g], vc), 0.0)
            du_ref[rs, :] = (dy * mixed).astype(BF16)
            dm = dy * u_ref[rs, :]
            dbf_ref[...] += dm
            dvl = jnp.zeros((BLK, BW), F32)
            for g in range(4):
                dmg = jnp.where(gm[g], dm, 0.0).astype(BF16)
                dw_ref[g] += _nt(dmg, vc)
                dvl = dvl + _tn(w_ref[g], dmg)
            dvl_ref[rs, :] = dvl
        dvl = dvl_ref[...]
        dv_ref[...] = _ln_bwd(dvl, xhat, r, gv).astype(BF16)
        dg_ref[...] += _colsum(dvl * xhat)
        db_ref[...] += _colsum(dvl)

    vec = bs((1, BW), lambda i: (0, 0))
    row = bs((tm, BW), lambda i: (i, 0))
    wsp = bs((4, BLK, BLK), lambda i: (0, 0, 0))
    bfs = bs((BLK, BW), lambda i: (0, 0))
    return _call(body, "sgu_bwd", (t // tm,),
                 [bs((tm, BW), lambda i: (i, 6)), bs((tm, BW), lambda i: (i, 7)), bs((None, tm, BW), lambda i: (2, i, 0)),
                  vec, vec, wsp, bfs],
                 [row, row, wsp, bfs, vec, vec],
                 [_sds((t, BW), BF16), _sds((t, BW), BF16), _sds((4, BLK, BLK), F32), _sds((BLK, BW), F32),
                  _sds((1, BW), F32), _sds((1, BW), F32)],
                 scratch=[pltpu.VMEM((tm, BW), F32)])(proj, proj, dys, ln_g, ln_b, w_tril, b_full)


CONF_HALO = 32


def conf_fwd(proj, dw, ln_g, ln_b, after=None):
    t = proj.shape[0]
    tm, halo = min(512, t), CONF_HALO
    per = tm // halo

    def body(v_ref, gt_ref, vh_ref, gh_ref, w_ref, g_ref, b_ref, y_ref, z_ref):
        yh = jnp.where(pl.program_id(0) > 0, vh_ref[...] * _sigmoid(gh_ref[...]), 0.0)
        yext = jnp.concatenate([yh, v_ref[...] * _sigmoid(gt_ref[...])], axis=0)
        z = _causal_conv(yext, w_ref, CONF_K, halo)
        z_ref[...] = z
        xhat, _ = _ln_hat(z)
        ln = xhat * g_ref[...] + b_ref[...]
        y_ref[...] = (ln * _sigmoid(ln)).astype(BF16)

    vec = bs((1, BW), lambda i: (0, 0))
    col = lambda c: bs((tm, BW), lambda i: (i, c))
    hal = lambda c: bs((halo, BW), lambda i: (_prev_blk(i, per), c))
    row = bs((tm, BW), lambda i: (i, 0))
    return _call(body, "conf_fwd", (t // tm,),
                 [col(8), col(9), hal(8), hal(9), bs((CONF_K, BW), lambda i: (0, 0)), vec, vec],
                 [row, row], [_sds((t, BW), BF16), _sds((t, BW), F32)], after=after)(
                     proj, proj, proj, proj, dw, ln_g, ln_b)


def conf_bwd_ln(z, dys, ln_g, ln_b):
    t = z.shape[0]
    tm = min(1024, t)

    def body(z_ref, dy_ref, g_ref, b_ref, dz_ref, dg_ref, db_ref):
        @pl.when(pl.program_id(0) == 0)
        def _():
            dg_ref[...] = jnp.zeros_like(dg_ref)
            db_ref[...] = jnp.zeros_like(db_ref)

        gv = g_ref[...]
        xhat, r = _ln_hat(z_ref[...])
        ln = xhat * gv + b_ref[...]
        s = _sigmoid(ln)
        dln = dy_ref[...] * (s * (1.0 + ln * (1.0 - s)))
        dz_ref[...] = _ln_bwd(dln, xhat, r, gv)
        dg_ref[...] += _colsum(dln * xhat)
        db_ref[...] += _colsum(dln)

    vec = bs((1, BW), lambda i: (0, 0))
    row = bs((tm, BW), lambda i: (i, 0))
    return _call(body, "conf_bwd_ln", (t // tm,), [row, bs((None, tm, BW), lambda i: (3, i, 0)), vec, vec],
                 [row, vec, vec], [_sds((t, BW), F32), _sds((1, BW), F32), _sds((1, BW), F32)])(z, dys, ln_g, ln_b)


def conf_bwd_conv(proj, dz, dw):
    t = proj.shape[0]
    tm, halo = min(512, t), CONF_HALO
    per = tm // halo
    last = t // halo - 1
    nt = t // tm

    def body(v_ref, gt_ref, vh_ref, gh_ref, dz_ref, dzn_ref, w_ref, dv_ref, dg_ref, dw_ref):
        i = pl.program_id(0)

        @pl.when(i == 0)
        def _():
            dw_ref[...] = jnp.zeros_like(dw_ref)

        val = v_ref[...]
        sg = _sigmoid(gt_ref[...])
        yh = jnp.where(i > 0, vh_ref[...] * _sigmoid(gh_ref[...]), 0.0)
        yext = jnp.concatenate([yh, val * sg], axis=0)
        dz = dz_ref[...]
        dzn = jnp.where(i < nt - 1, dzn_ref[...], 0.0)
        dy0 = _anti_conv(jnp.concatenate([dz, dzn], axis=0), w_ref, CONF_K, tm)
        dv_ref[...] = (dy0 * sg).astype(BF16)
        dg_ref[...] = (dy0 * val * sg * (1.0 - sg)).astype(BF16)
        _conv_wgrad(dw_ref, dz, yext, CONF_K, halo)

    col = lambda c: bs((tm, BW), lambda i: (i, c))
    hal = lambda c: bs((halo, BW), lambda i: (_prev_blk(i, per), c))
    row = bs((tm, BW), lambda i: (i, 0))
    wsp = bs((CONF_K, BW), lambda i: (0, 0))
    return _call(body, "conf_bwd_conv", (t // tm,),
                 [col(8), col(9), hal(8), hal(9), row, bs((halo, BW), lambda i: (_next_blk(i, per, last), 0)), wsp],
                 [row, row, wsp], [_sds((t, BW), BF16), _sds((t, BW), BF16), _sds((CONF_K, BW), F32)])(
                     proj, proj, proj, proj, dz, dz, dw)


def _place():
    return lax.axis_index("x"), lax.axis_index("y"), lax.axis_index("c")


def _comm_call(body, name, n_in, out_shape, scratch, aliases=None):
    return pl.pallas_call(body, name=name, in_specs=[ANY] * n_in, out_specs=[ANY] * len(out_shape), out_shape=out_shape,
                          scratch_shapes=scratch, input_output_aliases=aliases or {},
                          compiler_params=pltpu.CompilerParams(has_side_effects=True, vmem_limit_bytes=VMEM_LIMIT))


HBM_SPEC = pl.BlockSpec(memory_space=pltpu.HBM)
SEM_SPEC = pl.BlockSpec(memory_space=pltpu.SEMAPHORE)
EFFECT = pltpu.SideEffectType.DATAFLOW_SIDE_EFFECTING


class SplitExchange:
    def __init__(self, name, bufs, plan, n_copies):
        self.name, self.bufs, self.plan, self.n = name, list(bufs), plan, n_copies

    def start(self, after):
        nb, n, plan = len(self.bufs), self.n, self.plan

        def body(*refs):
            send, recv, token = refs[nb + 1], refs[nb + 2], refs[-1]
            for k, (src, dst, _, dev) in enumerate(plan(refs[:nb])):
                pltpu.make_async_remote_copy(src_ref=src, dst_ref=dst, send_sem=send.at[k], recv_sem=recv.at[k],
                                             device_id=dev, device_id_type=MESH).start()
            token[...] = jnp.zeros_like(token)

        outs = pl.pallas_call(
            body, name=self.name + "_start",
            out_shape=(pltpu.SemaphoreType.DMA((n,)), pltpu.SemaphoreType.DMA((n,)),
                       *[pltpu.HBM(b.shape, b.dtype) for b in self.bufs], _sds((8, 128), F32)),
            in_specs=[HBM_SPEC] * nb + [ANY],
            out_specs=(SEM_SPEC, SEM_SPEC, *[HBM_SPEC] * nb, pl.BlockSpec(memory_space=pltpu.VMEM)),
            input_output_aliases={i: 2 + i for i in range(nb)},
            compiler_params=pltpu.CompilerParams(has_side_effects=EFFECT))(
                *[pltpu.with_memory_space_constraint(b, pltpu.HBM) for b in self.bufs], after)
        self.send, self.recv, self.bufs = outs[0], outs[1], list(outs[2:2 + nb])
        return outs[-1]

    def wait(self, after):
        nb, plan = len(self.bufs), self.plan
        after = list(after) if isinstance(after, (list, tuple)) else [after]

        def body(*refs):
            send, recv = refs[nb], refs[nb + 1]
            for k, (src, _, land, dev) in enumerate(plan(refs[:nb])):
                cp = pltpu.make_async_remote_copy(src_ref=src, dst_ref=land, send_sem=send.at[k], recv_sem=recv.at[k],
                                                  device_id=dev, device_id_type=MESH)
                cp.wait_send()
                cp.wait_recv()

        outs = pl.pallas_call(
            body, name=self.name + "_wait", out_shape=tuple(pltpu.HBM(b.shape, b.dtype) for b in self.bufs),
            in_specs=[HBM_SPEC] * nb + [SEM_SPEC, SEM_SPEC] + [ANY] * len(after), out_specs=[HBM_SPEC] * nb,
            input_output_aliases={i: i for i in range(nb)},
            compiler_params=pltpu.CompilerParams(has_side_effects=EFFECT))(*self.bufs, self.send, self.recv, *after)
        return list(outs)


def _chips_of(x, y):
    return [(1 - x, y), (x, 1 - y), (1 - x, 1 - y)]


def allgather_ici_plan(shapes):
    def plan(refs):
        x, y, c = _place()
        out = []
        for a, ref in enumerate(refs):
            hl = shapes[a][1] // 2
            half = pl.ds(c * hl, hl)
            for cx, cy in _chips_of(x, y):
                mine = ref.at[2 * x + y, half]
                out.append((mine, mine, ref.at[2 * cx + cy, half], (cx, cy, c)))
        return out
    return plan


def allgather_d2d_plan(shapes):
    def plan(refs):
        x, y, c = _place()
        out = []
        for a, ref in enumerate(refs):
            hl = shapes[a][1] // 2
            for cx, cy in _chips_of(x, y):
                got = ref.at[2 * cx + cy, pl.ds(c * hl, hl)]
                out.append((got, got, ref.at[2 * cx + cy, pl.ds((1 - c) * hl, hl)], (x, y, 1 - c)))
        return out
    return plan


def gather8(v, reduce):
    rows, cols = v.shape

    def body(v_ref, o_ref, land_ref, send, recv, lsem):
        x, y, c = _place()
        me = 4 * x + 2 * y + c
        land = land_ref if reduce else o_ref
        mine = pltpu.make_async_copy(v_ref, land.at[me], lsem)
        mine.start()
        sent = []
        for j in range(1, 8):
            fx, fy, fc = (j >> 2) & 1, (j >> 1) & 1, j & 1
            tgt = (1 - x if fx else x, 1 - y if fy else y, 1 - c if fc else c)
            cp = pltpu.make_async_remote_copy(src_ref=v_ref, dst_ref=land.at[me], send_sem=send.at[j - 1],
                                              recv_sem=recv.at[j - 1], device_id=tgt, device_id_type=MESH)
            cp.start()
            sent.append(cp)
        for j in range(1, 8):
            fx, fy, fc = (j >> 2) & 1, (j >> 1) & 1, j & 1
            peer = 4 * (1 - x if fx else x) + 2 * (1 - y if fy else y) + (1 - c if fc else c)
            pltpu.make_async_remote_copy(src_ref=v_ref, dst_ref=land.at[peer], send_sem=send.at[j - 1],
                                         recv_sem=recv.at[j - 1], device_id=(x, y, c), device_id_type=MESH).wait_recv()
        for cp in sent:
            cp.wait_send()
        mine.wait()
        if reduce:
            acc = land_ref[0]
            for k in range(1, 8):
                acc = acc + land_ref[k]
            o_ref[...] = acc

    vm = pl.BlockSpec(memory_space=pltpu.VMEM)
    out_shape = _sds((rows, cols), F32) if reduce else _sds((8, rows, cols), F32)
    land_shape = (8, rows, cols) if reduce else (8, 128)
    return pl.pallas_call(
        body, name="allreduce8" if reduce else "allgather8", in_specs=[vm], out_specs=vm, out_shape=out_shape,
        scratch_shapes=[pltpu.VMEM(land_shape, F32), pltpu.SemaphoreType.DMA((7,)), pltpu.SemaphoreType.DMA((7,)),
                        pltpu.SemaphoreType.DMA],
        compiler_params=pltpu.CompilerParams(has_side_effects=True, vmem_limit_bytes=VMEM_LIMIT))(v)


def allgather_weights(bufs):
    n = len(bufs)

    def body(*refs):
        ins, outs = refs[:n], refs[n:2 * n]
        send, recv = refs[2 * n:]
        x, y, c = _place()
        s_me = 2 * x + y
        chips = [(1 - x, y), (x, 1 - y), (1 - x, 1 - y)]
        sibling = (x, y, 1 - c)
        started = []
        for a in range(n):
            hl = bufs[a].shape[1] // 2
            half = pl.ds(c * hl, hl)
            for j, chip in enumerate(chips):
                cp = pltpu.make_async_remote_copy(src_ref=ins[a].at[s_me, half], dst_ref=outs[a].at[s_me, half],
                                                  send_sem=send.at[6 * a + j], recv_sem=recv.at[6 * a + j],
                                                  device_id=(chip[0], chip[1], c), device_id_type=MESH)
                cp.start()
                started.append(cp)
        for a in range(n):
            hl = bufs[a].shape[1] // 2
            half = pl.ds(c * hl, hl)
            for j, chip in enumerate(chips):
                s_j = 2 * chip[0] + chip[1]
                landed = outs[a].at[s_j, half]
                pltpu.make_async_remote_copy(src_ref=landed, dst_ref=landed, send_sem=send.at[6 * a + j],
                                             recv_sem=recv.at[6 * a + j], device_id=sibling, device_id_type=MESH).wait_recv()
                fw = pltpu.make_async_remote_copy(src_ref=landed, dst_ref=landed, send_sem=send.at[6 * a + 3 + j],
                                                  recv_sem=recv.at[6 * a + 3 + j], device_id=sibling, device_id_type=MESH)
                fw.start()
                started.append(fw)
        for a in range(n):
            hl = bufs[a].shape[1] // 2
            other = pl.ds((1 - c) * hl, hl)
            for j, chip in enumerate(chips):
                s_j = 2 * chip[0] + chip[1]
                theirs = outs[a].at[s_j, other]
                pltpu.make_async_remote_copy(src_ref=theirs, dst_ref=theirs, send_sem=send.at[6 * a + 3 + j],
                                             recv_sem=recv.at[6 * a + 3 + j], device_id=sibling, device_id_type=MESH).wait_recv()
        for cp in started:
            cp.wait_send()

    out_shape = [_sds(b.shape, b.dtype) for b in bufs]
    scratch = [pltpu.SemaphoreType.DMA((6 * n,)), pltpu.SemaphoreType.DMA((6 * n,))]
    return _comm_call(body, "allgather_weights", n, out_shape, scratch, aliases={a: a for a in range(n)})(*bufs)


def _row_tile(rows, cols):
    best = 16
    for t in range(16, rows + 1, 16):
        if rows % t == 0 and t * cols * 4 <= 2 * 1024 * 1024:
            best = t
    return best


def _rs_add_sibling(scal, g, ra, hr):
    cols = g.shape[2]
    tr = _row_tile(hr, cols)
    nr = hr // tr

    def body(s_ref, g_ref, r_ref, p32_ref, p16_ref):
        v = g_ref[...] + r_ref[...]
        p16_ref[...] = v.astype(BF16)

        @pl.when(pl.program_id(1) == s_ref[0])
        def _():
            p32_ref[...] = v

    blk = lambda f: bs((None, tr, cols), f)
    own = blk(lambda i, s, sr: (s, i, 0))
    spec = pltpu.PrefetchScalarGridSpec(num_scalar_prefetch=1, grid=(nr, N_SH),
                                        in_specs=[blk(lambda i, s, sr: (s, sr[1] * nr + i, 0)), own],
                                        out_specs=[bs((tr, cols), lambda i, s, sr: (i, 0)), own])
    return pl.pallas_call(body, name="rs_add_sibling", grid_spec=spec,
                          out_shape=[_sds((hr, cols), F32), _sds((N_SH, hr, cols), BF16)],
                          compiler_params=pltpu.CompilerParams(dimension_semantics=("arbitrary",) * 2,
                                                               vmem_limit_bytes=VMEM_LIMIT))(scal, g, ra)


def _rs_add_chips(scal, p32, rb, hr):
    cols = p32.shape[1]
    tr = _row_tile(hr, cols)
    nr = hr // tr

    def body(s_ref, p_ref, r0, r1, r2, o_ref):
        o_ref[...] = ((p_ref[...] + r0[...].astype(F32)) + r1[...].astype(F32)) + r2[...].astype(F32)

    blk = lambda f: bs((None, tr, cols), f)
    spec = pltpu.PrefetchScalarGridSpec(
        num_scalar_prefetch=1, grid=(nr,),
        in_specs=[bs((tr, cols), lambda i, sr: (i, 0))] + [blk(functools.partial(lambda i, sr, j: (j, i, 0), j=j))
                                                            for j in range(3)],
        out_specs=blk(lambda i, sr: (sr[1], i, 0)))
    return pl.pallas_call(body, name="rs_add_chips", grid_spec=spec, out_shape=_sds((2, hr, cols), F32),
                          compiler_params=pltpu.CompilerParams(dimension_semantics=("arbitrary",),
                                                               vmem_limit_bytes=VMEM_LIMIT))(scal, p32, rb, rb, rb)


class SplitReduceScatter:
    def __init__(self, gs):
        x, y, c = _place()
        self.scal = jnp.stack([2 * x + y, c]).astype(jnp.int32)
        self.gs, self.n = list(gs), len(gs)
        self.hrs = [g.shape[1] // 2 for g in gs]

    def swap_start(self, after):
        n, hrs = self.n, self.hrs

        def plan(refs):
            x, y, c = _place()
            return [(refs[a].at[:, pl.ds((1 - c) * hrs[a], hrs[a])], refs[n + a], refs[n + a], (x, y, 1 - c))
                    for a in range(n)]

        lands = [lax.empty((N_SH, hrs[a], g.shape[2]), F32) for a, g in enumerate(self.gs)]
        self.ex = SplitExchange("rs_swap_halves", self.gs + lands, plan, n)
        return self.ex.start(after)

    def swap_wait_send_start(self, after):
        n, hrs = self.n, self.hrs
        bufs = self.ex.wait(after)
        parts = [_rs_add_sibling(self.scal, bufs[a], bufs[n + a], hrs[a]) for a in range(n)]
        self.p32 = [p[0] for p in parts]

        def plan(refs):
            x, y, c = _place()
            return [(refs[a].at[2 * cx + cy], refs[n + a].at[j], refs[n + a].at[j], (cx, cy, c))
                    for a in range(n) for j, (cx, cy) in enumerate(_chips_of(x, y))]

        lands = [lax.empty((3, hrs[a], g.shape[2]), BF16) for a, g in enumerate(self.gs)]
        self.ex = SplitExchange("rs_send_partials", [p[1] for p in parts] + lands, plan, 3 * n)
        return self.ex.start(parts[-1][1])

    def send_wait_share_start(self, after):
        n, hrs = self.n, self.hrs
        bufs = self.ex.wait(after)
        fins = [_rs_add_chips(self.scal, self.p32[a], bufs[n + a], hrs[a]) for a in range(n)]

        def plan(refs):
            x, y, c = _place()
            return [(refs[a].at[c], refs[a].at[c], refs[a].at[1 - c], (x, y, 1 - c)) for a in range(n)]

        self.ex = SplitExchange("rs_share_halves", fins, plan, n)
        return self.ex.start(fins[-1])

    def share_wait(self, after):
        fulls = self.ex.wait(after)
        return [f.reshape(2 * hr, f.shape[2]) for f, hr in zip(fulls, self.hrs)]


def reduce_scatter_grads(gs):
    n = len(gs)
    x, y, c = _place()
    scal = jnp.stack([2 * x + y, c]).astype(jnp.int32)
    hrs = [g.shape[1] // 2 for g in gs]

    def swap_body(*refs):
        ins, outs = refs[:n], refs[n:2 * n]
        send, recv = refs[2 * n:]
        xx, yy, cc = _place()
        cps = []
        for a in range(n):
            cp = pltpu.make_async_remote_copy(src_ref=ins[a].at[:, pl.ds((1 - cc) * hrs[a], hrs[a])], dst_ref=outs[a],
                                              send_sem=send.at[a], recv_sem=recv.at[a],
                                              device_id=(xx, yy, 1 - cc), device_id_type=MESH)
            cp.start()
            cps.append(cp)
        for cp in cps:
            cp.wait()

    ras = _comm_call(swap_body, "rs_swap_halves", n, [_sds((N_SH, hrs[a], gs[a].shape[2]), F32) for a in range(n)],
                     [pltpu.SemaphoreType.DMA((n,)), pltpu.SemaphoreType.DMA((n,))])(*gs)

    parts = [_rs_add_sibling(scal, gs[a], ras[a], hrs[a]) for a in range(n)]

    def ici_body(*refs):
        ins, outs = refs[:n], refs[n:2 * n]
        send, recv = refs[2 * n:]
        xx, yy, cc = _place()
        chips = [(1 - xx, yy), (xx, 1 - yy), (1 - xx, 1 - yy)]
        cps = []
        for a in range(n):
            for j, chip in enumerate(chips):
                cp = pltpu.make_async_remote_copy(src_ref=ins[a].at[2 * chip[0] + chip[1]], dst_ref=outs[a].at[j],
                                                  send_sem=send.at[3 * a + j], recv_sem=recv.at[3 * a + j],
                                                  device_id=(chip[0], chip[1], cc), device_id_type=MESH)
                cp.start()
                cps.append(cp)
        for cp in cps:
            cp.wait()

    rbs = _comm_call(ici_body, "rs_send_partials", n, [_sds((3, hrs[a], gs[a].shape[2]), BF16) for a in range(n)],
                     [pltpu.SemaphoreType.DMA((3 * n,)), pltpu.SemaphoreType.DMA((3 * n,))])(*[p[1] for p in parts])

    fins = [_rs_add_chips(scal, parts[a][0], rbs[a], hrs[a]) for a in range(n)]

    def share_body(*refs):
        ins, outs = refs[:n], refs[n:2 * n]
        send, recv = refs[2 * n:]
        xx, yy, cc = _place()
        sib = (xx, yy, 1 - cc)
        cps = []
        for a in range(n):
            cp = pltpu.make_async_remote_copy(src_ref=ins[a].at[cc], dst_ref=outs[a].at[cc], send_sem=send.at[a],
                                              recv_sem=recv.at[a], device_id=sib, device_id_type=MESH)
            cp.start()
            cps.append(cp)
        for a in range(n):
            pltpu.make_async_remote_copy(src_ref=ins[a].at[cc], dst_ref=outs[a].at[1 - cc], send_sem=send.at[a],
                                         recv_sem=recv.at[a], device_id=sib, device_id_type=MESH).wait_recv()
        for cp in cps:
            cp.wait_send()

    fulls = _comm_call(share_body, "rs_share_halves", n, [_sds(f.shape, F32) for f in fins],
                       [pltpu.SemaphoreType.DMA((n,)), pltpu.SemaphoreType.DMA((n,))],
                       aliases={a: a for a in range(n)})(*fins)
    return [f.reshape(2 * hr, f.shape[2]) for f, hr in zip(fulls, hrs)]


def adamw(w, g, m, v):
    shape = w.shape
    cols = shape[-1]
    rows = math.prod(shape[:-1]) if len(shape) > 1 else 1
    tr = 256 if rows % 256 == 0 and rows > 256 else rows
    c1 = 1.0 - ADAM_B1 ** ADAM_STEP
    c2 = 1.0 - ADAM_B2 ** ADAM_STEP

    def body(w_ref, g_ref, m_ref, v_ref, d_ref, nm_ref, nv_ref):
        gv = g_ref[...]
        nm = ADAM_B1 * m_ref[...] + (1.0 - ADAM_B1) * gv
        nv = ADAM_B2 * v_ref[...] + (1.0 - ADAM_B2) * (gv * gv)
        nm_ref[...] = nm
        nv_ref[...] = nv
        d_ref[...] = -ADAM_LR * ((nm / c1) / (jnp.sqrt(nv / c2) + ADAM_EPS) + ADAM_WD * w_ref[...])

    row = bs((tr, cols), lambda i: (i, 0))
    outs = _call(body, "adamw", (rows // tr,), [row] * 4, [row] * 3, [_sds((rows, cols), F32)] * 3)(
        *[a.reshape(rows, cols) for a in (w, g, m, v)])
    return [o.reshape(shape) for o in outs]


def adamw_layers(w, gs, m, v, lo, into=None, after=None):
    shape = w.shape
    cols = shape[-1]
    rl = math.prod(shape[1:-1])
    tr = max(t_ for t_ in range(8, rl + 1, 8) if rl % t_ == 0 and t_ * cols * 4 <= 1024 * 1024)
    nb = rl // tr
    n = len(gs)
    c1 = 1.0 - ADAM_B1 ** ADAM_STEP
    c2 = 1.0 - ADAM_B2 ** ADAM_STEP

    def body(*refs):
        w_ref, m_ref, v_ref = refs[:3]
        g_refs = refs[3:3 + n]
        d_ref, nm_ref, nv_ref, go_ref = refs[-4:]
        layer = pl.program_id(0) // nb
        for k in range(n):
            @pl.when(layer == k)
            def _(k=k):
                gv = g_refs[k][...]
                nm = ADAM_B1 * m_ref[...] + (1.0 - ADAM_B1) * gv
                nv = ADAM_B2 * v_ref[...] + (1.0 - ADAM_B2) * (gv * gv)
                nm_ref[...] = nm
                nv_ref[...] = nv
                go_ref[...] = gv
                d_ref[...] = -ADAM_LR * ((nm / c1) / (jnp.sqrt(nv / c2) + ADAM_EPS) + ADAM_WD * w_ref[...])

    row = bs((tr, cols), lambda b: (lo * nb + b, 0))
    g_specs = [bs((tr, cols), functools.partial(lambda b, k: (jnp.clip(b - k * nb, 0, nb - 1), 0), k=k)) for k in range(n)]
    flat = lambda a: a.reshape(-1, cols)
    in_specs = [row] * 3 + g_specs
    args = [flat(w), flat(m), flat(v)] + [flat(g) for g in gs]
    aliases = None
    if into is not None:
        aliases = {len(in_specs) + k: k for k in range(4)}
        in_specs = in_specs + [ANY] * 4
        args = args + [flat(a) for a in into]
    outs = _call(body, "adamw_layers", (n * nb,), in_specs, [row] * 4, [_sds((shape[0] * rl, cols), F32)] * 4,
                 aliases=aliases, after=after)(*args)
    return [o.reshape(shape) for o in outs]


def allreduce8_split(vec):
    rows, cols = vec.shape

    def plan(refs):
        x, y, c = _place()
        me = 4 * x + 2 * y + c
        out = []
        for j in range(1, 8):
            px, py, pc = (1 - x if j & 4 else x), (1 - y if j & 2 else y), (1 - c if j & 1 else c)
            out.append((refs[0], refs[1].at[me], refs[1].at[4 * px + 2 * py + pc], (px, py, pc)))
        return out

    ex = SplitExchange("allreduce8", [vec, lax.empty((8, rows, cols), F32)], plan, 7)

    def finish(after):
        v, land = ex.wait(after)
        x, y, c = _place()
        me = jnp.reshape(4 * x + 2 * y + c, (1,)).astype(jnp.int32)

        def body(me_ref, v_ref, l_ref, o_ref):
            o_ref[...] = jnp.zeros_like(o_ref)
            for k in range(8):
                @pl.when(me_ref[0] == k)
                def _():
                    o_ref[...] += v_ref[...]

                @pl.when(me_ref[0] != k)
                def _(k=k):
                    o_ref[...] += l_ref[k]

        spec = pltpu.PrefetchScalarGridSpec(
            num_scalar_prefetch=1, grid=(1,),
            in_specs=[bs((rows, cols), lambda i, mr: (0, 0)), bs((8, rows, cols), lambda i, mr: (0, 0, 0))],
            out_specs=bs((rows, cols), lambda i, mr: (0, 0)))
        return pl.pallas_call(body, name="allreduce8_sum", grid_spec=spec, out_shape=_sds((rows, cols), F32),
                              compiler_params=pltpu.CompilerParams(dimension_semantics=("arbitrary",),
                                                                   vmem_limit_bytes=VMEM_LIMIT))(me, v, land)

    return ex, finish


class Hooks:
    def __init__(self):
        self.steps = {}

    def add(self, point, fn):
        self.steps.setdefault(point, []).append(fn)

    def run(self, point, arr, env=None):
        tok = None
        for fn in self.steps.get(point, ()):
            got = fn(arr if tok is None else tok, env)
            tok = tok if got is None else got
        return tok


def layer_fwd(x, p_i, w, hooks):
    h, proj, *qkv = norm_in_proj(x, w["g_mix"], w["win"], after=hooks.run("start", x))
    ya = conva_fwd(proj, w["conv_a"])
    yb, o32, lse = attn_merge([attn_fwd_group(pv, d) for pv, d in zip(qkv, DILATIONS)])
    yc = sgu_fwd(proj, w["sgu_ln_g"], w["sgu_ln_b"], w["sgu_wt"], w["sgu_bf"])
    yd, z = conf_fwd(proj, w["conf_dw"], w["conf_ln_g"], w["conf_ln_b"], after=hooks.run("pre_conf", [ya, yb, yc]))
    ys = (ya, yb, yc, yd)
    tok = hooks.run("pre_merge", yd)
    merged, gates, ybr = merge_fwd(h, ys, w["wg"], w["wbr"], after=tok)
    x1 = mm_residual(merged, w["wout"], x, "attn_out", after=hooks.run("post_merge", merged))
    h2, fgu, act = ffn_in(x1, w["g_ffn"], w["wfi"])
    x2 = mm_residual(act, w["wfo"], x1, "ffn_out", after=hooks.run("post_ffn_in", act))
    h3, gate, pp, x3 = ple_fwd(x2, w["g_ple"], w["wpg"], p_i, w["wpp"])
    saved = dict(x=x, h=h, proj=proj, qkv=qkv, ys=ys, o32=o32, lse=lse, z=z, merged=merged, gates=gates, ybr=ybr, x1=x1,
                 h2=h2, fgu=fgu, act=act, x2=x2, h3=h3, gate=gate, pp=pp)
    return x3, saved


def layer_bwd(dx3, p_i, w, s, hooks):
    t = dx3.shape[0]
    tr = min(1024, t)
    nr = t // tr
    ns_fi = FFN_H // 2
    small = {}

    dpre, dpp = ple_bwd_pre(dx3, s["gate"], s["pp"], after=hooks.run("start", dx3))
    ga_shape, gb_shape = _sds((N_SH, 6 * BW, D_MODEL), F32), _sds((N_SH, 5 * BW, BW), F32)
    ga_blk = lambda idx: bs((N_SH, BW, D_MODEL), idx)
    ga = tn_matmul("dw_ple_gate", s["h3"], dpre, (nr,), bs((tr, D_MODEL), lambda r: (r, 0)),
                   bs((tr, D_MODEL), lambda r: (r, 0)), ga_blk(lambda r: (0, 5, 0)), ga_shape, split=N_SH)
    gb = tn_matmul("dw_ple_proj", p_i, dpp, (N_SH, nr), bs((tr, BW), lambda j, r: (r, 0)),
                   bs((tr, BW), lambda j, r: (r, j)), bs((None, BW, BW), lambda j, r: (j, 4, 0)), gb_shape)
    dx2, small["g_ple"] = norm_bwd(
        "ple_norm_bwd",
        [(dpre, lambda tm: bs((tm, D_MODEL), lambda i: (i, 0)), w["wpg"], lambda a, wr: [(a[...], wr[...])])],
        dx3, s["x2"], w["g_ple"])

    df = ffn_bwd_act(dx2, w["wfo"], s["fgu"], after=hooks.run("pre_ffn", dx2))
    gfo = tn_matmul("dw_ffn_out", s["act"], dx2, (2, nr), bs((tr, ns_fi), lambda j, r: (r, j)),
                    bs((tr, D_MODEL), lambda j, r: (r, 0)), bs((2, FFN_H // N_SH, D_MODEL), lambda j, r: (j, 0, 0)),
                    _sds((N_SH, FFN_H // N_SH, D_MODEL), F32), split=2)
    gfi = tn_matmul("dw_ffn_in", s["h2"], df, (N_SH, nr), bs((tr, D_MODEL), lambda j, r: (r, 0)),
                    bs((None, tr, ns_fi), lambda j, r: (j // 2, r, j % 2)),
                    bs((None, D_MODEL, ns_fi), lambda j, r: (j, 0, 0)), _sds((N_SH, D_MODEL, ns_fi), F32))
    dx1, small["g_ffn"] = norm_bwd(
        "ffn_norm_bwd",
        [(df, lambda tm: bs((2, tm, FFN_H), lambda i: (0, i, 0)), w["wfi"],
          lambda a, wr: [(a[k // 2, :, (k % 2) * ns_fi:(k % 2 + 1) * ns_fi], wr[k]) for k in range(N_SH)])],
        dx2, s["x1"], w["g_ffn"])

    dpre_m, dyb, dys = merge_bwd(dx1, w["wout"], s["gates"], s["ybr"], w["wbr"])
    ga = tn_matmul("dw_out", s["merged"], dx1, (nr,), bs((tr, D_MODEL), lambda r: (r, 0)),
                   bs((tr, D_MODEL), lambda r: (r, 0)), ga_blk(lambda r: (0, 4, 0)), ga_shape, split=N_SH, into=ga,
                   after=hooks.run("pre_dw_out", dyb, dict(gfo=gfo, gfi=gfi)))
    ga = tn_matmul("dw_merge_gate", s["h"], dpre_m, (N_BR, nr), bs((tr, D_MODEL), lambda k, r: (r, 0)),
                   bs((None, tr, D_MODEL), lambda k, r: (k, r, 0)), ga_blk(lambda k, r: (0, k, 0)), ga_shape,
                   split=N_SH, into=ga)
    for k in range(N_BR):
        gb = tn_matmul("dw_branch", s["ys"][k], dyb, (nr,), bs((tr, BW), lambda r: (r, 0)),
                       bs((None, tr, D_MODEL), functools.partial(lambda r, kk: (kk, r, 0), kk=k)),
                       bs((N_SH, BW, BW), functools.partial(lambda r, kk: (0, kk, 0), kk=k)), gb_shape,
                       split_cols=N_SH, into=gb)

    (dab, dac, dax), small["conv_a"] = conva_bwd(s["proj"], dys, w["conv_a"], after=hooks.run("pre_conva", gb))
    lds, dy_views = attn_delta(dys, s["o32"], s["lse"])
    dy_views = [dys[1]] + list(dy_views)
    dqkv = attn_bwd_finish([attn_bwd_group(pv, dov, ldv, d)
                            for pv, dov, ldv, d in zip(s["qkv"], dy_views, lds, DILATIONS)])
    du, dv, d_sw, d_sbf, small["sgu_ln_g"], small["sgu_ln_b"] = sgu_bwd(
        s["proj"], dys, w["sgu_ln_g"], w["sgu_ln_b"], w["sgu_wt"], w["sgu_bf"])
    small["sgu_w"] = jnp.where(jnp.tril(jnp.ones((BLK, BLK), bool))[None], d_sw, 0.0)
    small["sgu_b"] = jnp.sum(d_sbf.reshape(BLK, 4, HEAD_D), axis=-1).T
    dz, small["conf_ln_g"], small["conf_ln_b"] = conf_bwd_ln(s["z"], dys, w["conf_ln_g"], w["conf_ln_b"])
    dval, dgate, small["conf_dw"] = conf_bwd_conv(s["proj"], dz, w["conf_dw"])
    dproj = jnp.concatenate([dab, dac, dax, dqkv, du, dv, dval, dgate], axis=1)

    ns_in = N_IN // N_SH
    gin = tn_matmul("dw_in", s["h"], dproj, (N_SH, nr), bs((tr, D_MODEL), lambda j, r: (r, 0)),
                    bs((tr, ns_in), lambda j, r: (r, j)), bs((None, D_MODEL, ns_in), lambda j, r: (j, 0, 0)),
                    _sds((N_SH, D_MODEL, ns_in), F32), after=hooks.run("pre_dw_in", dproj))
    hooks.run("end", gin)
    big = [ga, gfo, gb, gin, gfi]
    dx, small["g_mix"] = norm_bwd(
        "mix_norm_bwd",
        [(dpre_m, lambda tm: bs((N_BR, tm, D_MODEL), lambda i: (0, i, 0)), w["wg"],
          lambda a, wr: [(a[k], wr[k]) for k in range(N_BR)]),
         (dproj, lambda tm: bs((tm, N_IN), lambda i: (i, 0)), w["win"],
          lambda a, wr: [(a[:, k * ns_in:(k + 1) * ns_in], wr[k]) for k in range(N_SH)])],
        dx1, s["x"], w["g_mix"])
    return dx, big, small


BIG_NAMES = ("w_in", "w_branch", "w_merge_gate", "w_out", "w_ffn_in", "w_ffn_out", "w_ple_gate", "w_ple_proj")


def unpack_big_grads(ga, gfo, gb, gin, gfi):
    return dict(w_in=gin, w_ffn_in=gfi, w_ffn_out=gfo,
                w_merge_gate=ga[:N_BR * BW].reshape(N_BR, BW, D_MODEL), w_out=ga[N_BR * BW:5 * BW], w_ple_gate=ga[5 * BW:],
                w_branch=gb[:N_BR * BW].reshape(N_BR, BW, BW), w_ple_proj=gb[N_BR * BW:])


SMALL_NAMES = ("g_mix", "conv_a", "sgu_ln_g", "sgu_ln_b", "sgu_w", "sgu_b", "conf_dw", "conf_ln_g", "conf_ln_b",
               "g_ffn", "g_ple")


def _pack_rows(arrays, rows):
    flat = jnp.concatenate([a.reshape(-1) for a in arrays])
    return jnp.pad(flat, (0, rows * D_MODEL - flat.shape[0])).reshape(rows, D_MODEL)


def _unpack_rows(packed, shapes):
    flat, out, pos = packed.reshape(-1), [], 0
    for shape in shapes:
        n = math.prod(shape)
        out.append(flat[pos:pos + n].reshape(shape))
        pos += n
    return out


def kernel(x, p, g_mix, w_in, conv_a, sgu_ln_g, sgu_ln_b, sgu_w, sgu_b, conf_dw, conf_ln_g, conf_ln_b, w_branch, w_merge_gate, w_out, g_ffn, w_ffn_in, w_ffn_out, g_ple, w_ple_gate, w_ple_proj, g_final, loss_target, m_g_mix, m_w_in, m_conv_a, m_sgu_ln_g, m_sgu_ln_b, m_sgu_w, m_sgu_b, m_conf_dw, m_conf_ln_g, m_conf_ln_b, m_w_branch, m_w_merge_gate, m_w_out, m_g_ffn, m_w_ffn_in, m_w_ffn_out, m_g_ple, m_w_ple_gate, m_w_ple_proj, m_g_final, v_g_mix, v_w_in, v_conv_a, v_sgu_ln_g, v_sgu_ln_b, v_sgu_w, v_sgu_b, v_conf_dw, v_conf_ln_g, v_conf_ln_b, v_w_branch, v_w_merge_gate, v_w_out, v_g_ffn, v_w_ffn_in, v_w_ffn_out, v_g_ple, v_w_ple_gate, v_w_ple_proj, v_g_final):
    weights = dict(g_mix=g_mix, w_in=w_in, conv_a=conv_a, sgu_ln_g=sgu_ln_g, sgu_ln_b=sgu_ln_b, sgu_w=sgu_w, sgu_b=sgu_b,
                   conf_dw=conf_dw, conf_ln_g=conf_ln_g, conf_ln_b=conf_ln_b, w_branch=w_branch, w_merge_gate=w_merge_gate,
                   w_out=w_out, g_ffn=g_ffn, w_ffn_in=w_ffn_in, w_ffn_out=w_ffn_out, g_ple=g_ple, w_ple_gate=w_ple_gate,
                   w_ple_proj=w_ple_proj, g_final=g_final)
    m_in = dict(g_mix=m_g_mix, w_in=m_w_in, conv_a=m_conv_a, sgu_ln_g=m_sgu_ln_g, sgu_ln_b=m_sgu_ln_b, sgu_w=m_sgu_w,
                sgu_b=m_sgu_b, conf_dw=m_conf_dw, conf_ln_g=m_conf_ln_g, conf_ln_b=m_conf_ln_b, w_branch=m_w_branch,
                w_merge_gate=m_w_merge_gate, w_out=m_w_out, g_ffn=m_g_ffn, w_ffn_in=m_w_ffn_in, w_ffn_out=m_w_ffn_out,
                g_ple=m_g_ple, w_ple_gate=m_w_ple_gate, w_ple_proj=m_w_ple_proj, g_final=m_g_final)
    v_in = dict(g_mix=v_g_mix, w_in=v_w_in, conv_a=v_conv_a, sgu_ln_g=v_sgu_ln_g, sgu_ln_b=v_sgu_ln_b, sgu_w=v_sgu_w,
                sgu_b=v_sgu_b, conf_dw=v_conf_dw, conf_ln_g=v_conf_ln_g, conf_ln_b=v_conf_ln_b, w_branch=v_w_branch,
                w_merge_gate=v_w_merge_gate, w_out=v_w_out, g_ffn=v_g_ffn, w_ffn_in=v_w_ffn_in, w_ffn_out=v_w_ffn_out,
                g_ple=v_g_ple, w_ple_gate=v_w_ple_gate, w_ple_proj=v_w_ple_proj, g_final=v_g_final)
    order = ("g_mix", "w_in", "conv_a", "sgu_ln_g", "sgu_ln_b", "sgu_w", "sgu_b", "conf_dw", "conf_ln_g", "conf_ln_b",
             "w_branch", "w_merge_gate", "w_out", "g_ffn", "w_ffn_in", "w_ffn_out", "g_ple", "w_ple_gate", "w_ple_proj",
             "g_final")
    depth = g_mix.shape[0]
    xs, tgt = x[0], loss_target[0]
    cw = BW // N_SH
    my_shard = 2 * lax.axis_index("x") + lax.axis_index("y")

    conv_rows = 16
    allc = gather8(_pack_rows([conv_a, conf_dw], conv_rows), reduce=False)
    shards = [_unpack_rows(allc[2 * s], [conv_a.shape, conf_dw.shape]) for s in range(N_SH)]
    conv_a_full = jnp.concatenate([sh[0] for sh in shards], axis=-1)
    conf_dw_full = jnp.concatenate([sh[1] for sh in shards], axis=-1)

    tril = jnp.tril(jnp.ones((BLK, BLK), bool))
    def placed_shards(i):
        shards = ([w_in[i], w_branch[i]] + [w_merge_gate[i, k] for k in range(N_BR)]
                  + [w_out[i], w_ffn_in[i], w_ffn_out[i], w_ple_gate[i], w_ple_proj[i]])
        return [lax.dynamic_update_slice(jnp.zeros((N_SH,) + sh.shape, BF16), sh.astype(BF16)[None],
                                         (my_shard,) + (0,) * sh.ndim) for sh in shards]

    def small_weights(i, win):
        vec = lambda a: a[i].reshape(1, -1)
        return dict(
            win=win, g_mix=vec(g_mix), g_ffn=vec(g_ffn), g_ple=vec(g_ple), conv_a=conv_a_full[i], conf_dw=conf_dw_full[i],
            sgu_ln_g=vec(sgu_ln_g), sgu_ln_b=vec(sgu_ln_b), conf_ln_g=vec(conf_ln_g), conf_ln_b=vec(conf_ln_b),
            sgu_wt=jnp.where(tril[None], sgu_w[i], 0.0).astype(BF16),
            sgu_bf=jnp.repeat(sgu_b[i].T, HEAD_D, axis=1))

    def late_weights(got):
        return dict(wbr=got[0], wg=jnp.stack([g.reshape(D_MODEL, D_MODEL) for g in got[1:5]]),
                    wout=got[5].reshape(D_MODEL, D_MODEL), wfi=got[6], wfo=got[7].reshape(FFN_H, D_MODEL),
                    wpg=got[8].reshape(D_MODEL, D_MODEL), wpp=got[9])

    class SplitAllGather:
        def __init__(self, bufs):
            self.shapes = [b.shape for b in bufs]
            self.ici = SplitExchange("allgather_ici", bufs, allgather_ici_plan(self.shapes), 3 * len(bufs))

        def ici_start(self, after, env=None):
            return self.ici.start(after)

        def ici_wait_d2d_start(self, after, env=None):
            landed = self.ici.wait(after)
            self.d2d = SplitExchange("allgather_d2d", landed, allgather_d2d_plan(self.shapes), 3 * len(landed))
            return self.d2d.start(landed[-1])

        def d2d_wait(self, after, env=None):
            self.got = self.d2d.wait(after)
            return None

    bufs0 = placed_shards(0)
    first = SplitAllGather(bufs0[:1])
    first.d2d_wait(first.ici_wait_d2d_start(first.ici_start(xs)))
    rest = SplitAllGather(bufs0[1:])
    layers = [small_weights(0, first.got[0])]
    act, saved = xs, []
    nxt_done = None
    for i in range(depth):
        hooks = Hooks()
        if i == 0:
            hooks.add("start", rest.ici_start)
            hooks.add("pre_conf", rest.ici_wait_d2d_start)
            hooks.add("pre_merge", rest.d2d_wait)
            hooks.add("pre_merge", lambda after, env: layers[0].update(late_weights(rest.got)))
        if i + 1 < depth:
            nxt = SplitAllGather(placed_shards(i + 1))
            points = ("pre_merge", "post_ffn_in", None) if i == 0 else ("start", "post_merge", "post_ffn_in")
            hooks.add(points[0], nxt.ici_start)
            hooks.add(points[1], nxt.ici_wait_d2d_start)
            if points[2]:
                hooks.add(points[2], nxt.d2d_wait)
        act, sv = layer_fwd(act, p[i, 0], layers[i], hooks)
        saved.append(sv)
        if i + 1 < depth:
            if i == 0:
                nxt.d2d_wait(act)
            layers.append({**small_weights(i + 1, nxt.got[0]), **late_weights(nxt.got[1:])})
    loss_part, dx, dg_final = loss_head(act, g_final.reshape(1, -1), tgt)

    big_red = [None] * depth
    small_red = [None] * depth
    small_rows = 80
    pending = None

    def small_vector(i, small):
        parts = [small[n] for n in SMALL_NAMES]
        return _pack_rows(parts + ([dg_final, loss_part[0, :1]] if i == 0 else []), small_rows)

    for i in reversed(range(depth)):
        hooks = Hooks()
        result = {}
        if pending is not None:
            rs, j, (small_ex, small_finish) = pending
            hooks.add("start", lambda after, env, ex=small_ex: ex.start(after))
            hooks.add("start", lambda after, env, rs=rs: rs.swap_start(after))
            hooks.add("pre_ffn", lambda after, env, rs=rs: rs.swap_wait_send_start(after))
            hooks.add("pre_dw_out", lambda after, env, rs=rs: rs.send_wait_share_start(after))
            hooks.add("pre_conva", lambda after, env, rs=rs, result=result: result.update(prev=rs.share_wait(after)))
            hooks.add("pre_conva", lambda after, env, fin=small_finish, result=result: result.update(small=fin(after)))
        if i == 0:
            def early_start(after, env, result=result):
                result["rs"] = SplitReduceScatter([env["gfo"], env["gfi"]])
                return result["rs"].swap_start(after)

            hooks.add("pre_dw_out", early_start)
            hooks.add("pre_conva", lambda after, env, result=result: result["rs"].swap_wait_send_start(after))
            hooks.add("pre_dw_in", lambda after, env, result=result: result["rs"].send_wait_share_start(after))
            hooks.add("end", lambda after, env, result=result: result.update(early=result["rs"].share_wait(after)))
        dx, big, small = layer_bwd(dx, p[i, 0], layers[i], saved[i], hooks)
        if pending is not None:
            big_red[pending[1]] = unpack_big_grads(*result["prev"])
            small_red[pending[1]] = result["small"]
        if i > 0:
            pending = (SplitReduceScatter(big), i, allreduce8_split(small_vector(i, small)))

    ga, _, gb, gin, _ = big
    late = SplitReduceScatter([ga, gb, gin])
    small_ex, small_finish = allreduce8_split(small_vector(0, small))
    upd = {}

    def update_upper(names, after):
        for name in names:
            upd[name] = adamw_layers(weights[name], [big_red[i][name] for i in range(1, depth)], m_in[name], v_in[name],
                                     1, after=after)
        return [upd[name][0] for name in names]

    done = update_upper(("w_in",), late.swap_start(small_ex.start(dx)))
    done = update_upper(("w_ffn_in", "w_merge_gate", "w_ffn_out"), late.swap_wait_send_start(done))
    small_red[0] = small_finish(done)
    done = update_upper(("w_branch", "w_out", "w_ple_gate", "w_ple_proj"), late.send_wait_share_start(done))
    ga, gb, gin = late.share_wait(done)
    gfo, gfi = result["early"]
    big_red[0] = unpack_big_grads(ga, gfo, gb, gin, gfi)

    layer_shapes = [small[n].shape for n in SMALL_NAMES]
    per_layer = [_unpack_rows(small_red[i], layer_shapes + ([dg_final.shape, (1,)] if i == 0 else []))
                 for i in range(depth)]
    grads = {n: jnp.stack([per_layer[i][k].reshape(weights[n].shape[1:] if n not in ("conv_a", "conf_dw")
                                                   else per_layer[i][k].shape) for i in range(depth)])
             for k, n in enumerate(SMALL_NAMES)}
    grads["g_final"] = per_layer[0][-2].reshape(-1)
    loss = per_layer[0][-1].reshape(())
    for n in ("conv_a", "conf_dw"):
        grads[n] = lax.dynamic_slice_in_dim(grads[n], my_shard * cw, cw, axis=2)

    small_all = [n for n in order if n not in BIG_NAMES]
    sm_shapes = [weights[n].shape for n in small_all]
    n_sm = sum(math.prod(sh) for sh in sm_shapes)
    sm_rows = -(-n_sm // (8 * D_MODEL)) * 8
    packed = [_pack_rows([src[n] for n in small_all], sm_rows) for src in (weights, grads, m_in, v_in)]
    sm_out = [_unpack_rows(o, sm_shapes) for o in adamw(*packed)]
    delta, new_m, new_v = ({n: o[k] for k, n in enumerate(small_all)} for o in sm_out)
    for name in BIG_NAMES:
        delta[name], new_m[name], new_v[name], grads[name] = adamw_layers(
            weights[name], [big_red[0][name]], m_in[name], v_in[name], 0, into=upd[name])

    return (loss, dx[None], *[grads[n] for n in order], *[delta[n] for n in order], *[new_m[n] for n in order],
            *[new_v[n] for n in order])
```

```python
import functools
import math

import jax
import jax.numpy as jnp
from jax import lax
from jax.experimental import pallas as pl
from jax.experimental.pallas import tpu as pltpu

F32 = jnp.float32
BF16 = jnp.bfloat16
EPS = 1e-6
D_MODEL = 1024
BW = 256
N_BR = 4
N_IN = 10 * BW
FFN_H = 2816
N_SH = 4
HEADS = 4
HEAD_D = 64
BLK = 128
DILATIONS = (1, 4, 16)
CONF_K = 31
CONVA_K = 3
NEG = -1e30
VMEM_LIMIT = 56 * 1024 * 1024
MESH = pl.DeviceIdType.MESH

ADAM_LR, ADAM_B1, ADAM_B2, ADAM_EPS, ADAM_WD, ADAM_STEP = 0.001, 0.9, 0.999, 1e-08, 0.01, 10

bs = pl.BlockSpec
ANY = pl.BlockSpec(memory_space=pl.ANY)


def _call(body, name, grid, in_specs, out_specs, out_shape, scratch=(), aliases=None, after=None):
    n_in = len(in_specs)
    kernel_body = body
    if after is not None:
        in_specs = list(in_specs) + [ANY]

        def kernel_body(*refs):
            return body(*refs[:n_in], *refs[n_in + 1:])

    call = pl.pallas_call(
        kernel_body, name=name, grid=grid, in_specs=in_specs, out_specs=out_specs, out_shape=out_shape,
        scratch_shapes=list(scratch), input_output_aliases=aliases or {},
        compiler_params=pltpu.CompilerParams(dimension_semantics=("arbitrary",) * len(grid),
                                             vmem_limit_bytes=VMEM_LIMIT))
    return call if after is None else (lambda *args: call(*args, after))


def _sds(shape, dtype):
    return jax.ShapeDtypeStruct(shape, dtype)


def _nn(a, b):
    return jnp.dot(a, b, preferred_element_type=F32)


def _nt(a, b):
    return lax.dot_general(a, b, (((1,), (1,)), ((), ())), preferred_element_type=F32)


def _tn(a, b):
    return lax.dot_general(a, b, (((0,), (0,)), ((), ())), preferred_element_type=F32)


def _sigmoid(x):
    return 1.0 / (1.0 + jnp.exp(-x))


def _rms_fwd(x, g):
    r = lax.rsqrt(jnp.mean(x * x, axis=-1, keepdims=True) + EPS)
    return x * r * g


def _rms_bwd(dh, x, g):
    r = lax.rsqrt(jnp.mean(x * x, axis=-1, keepdims=True) + EPS)
    xr = x * r
    dxr = dh * g
    dx = r * (dxr - xr * jnp.mean(dxr * xr, axis=-1, keepdims=True))
    return dx, dh * xr


def _ln_hat(x):
    mu = jnp.mean(x, axis=-1, keepdims=True)
    xc = x - mu
    r = lax.rsqrt(jnp.mean(xc * xc, axis=-1, keepdims=True) + EPS)
    return xc * r, r


def _ln_bwd(dy, xhat, r, g):
    dxh = dy * g
    return r * (dxh - jnp.mean(dxh, axis=-1, keepdims=True) - xhat * jnp.mean(dxh * xhat, axis=-1, keepdims=True))


def _colsum(v):
    return jnp.sum(v, axis=0, keepdims=True)


def _causal_conv(zext, w_ref, k_taps, halo):
    acc = zext[halo:] * w_ref[k_taps - 1:k_taps, :]
    for k in range(k_taps - 1):
        acc = acc + pltpu.roll(zext, k_taps - 1 - k, 0)[halo:] * w_ref[k:k + 1, :]
    return acc


def _anti_conv(dext, w_ref, k_taps, tm):
    n = dext.shape[0]
    acc = dext[:tm] * w_ref[k_taps - 1:k_taps, :]
    for s in range(1, k_taps):
        acc = acc + pltpu.roll(dext, n - s, 0)[:tm] * w_ref[k_taps - 1 - s:k_taps - s, :]
    return acc


def _conv_wgrad(dw_ref, dc, zext, k_taps, halo):
    dw_ref[k_taps - 1:k_taps, :] += _colsum(dc * zext[halo:])
    for k in range(k_taps - 1):
        dw_ref[k:k + 1, :] += _colsum(dc * pltpu.roll(zext, k_taps - 1 - k, 0)[halo:])


LANES = 128


def _to_strided_view(dst_ref, chunk, scr, d, width):
    n = scr.shape[0] // d
    for c in range(width // LANES):
        scr[...] = chunk(c)
        for r in range(d):
            dst_ref[:, r * width + c * LANES:r * width + (c + 1) * LANES] = scr[pl.ds(r, n, stride=d), :].astype(dst_ref.dtype)


def _from_strided_view(src_ref, scr, d, width, c):
    n = scr.shape[0] // d
    for r in range(d):
        scr[pl.ds(r, n, stride=d), :] = src_ref[:, r * width + c * LANES:r * width + (c + 1) * LANES].astype(F32)
    return scr[...]


def _view_spec(tm, d, width):
    return bs((tm // d, d * width), lambda i: (i, 0))


def _prev_blk(i, per):
    return jnp.maximum(i * per - 1, 0)


def _next_blk(i, per, last):
    return jnp.minimum((i + 1) * per, last)


def norm_in_proj(x, g, win, after=None):
    t = x.shape[0]
    tm = min(512, t)
    ns = win.shape[2]

    def body(x_ref, g_ref, w_ref, h_ref, o_ref, q_ref, q4_ref, q16_ref, scr):
        h = _rms_fwd(x_ref[...], g_ref[...]).astype(BF16)
        h_ref[...] = h
        parts = []
        for s in range(N_SH):
            r = _nn(h, w_ref[s])
            o_ref[:, s * ns:(s + 1) * ns] = r
            if s == 1:
                parts.append(r[:, 3 * BW - ns:])
            if s == 2:
                parts.append(r[:, :6 * BW - 2 * ns])
        qf = jnp.concatenate(parts, axis=1)
        q_ref[...] = qf.astype(BF16)
        chunk = lambda c: qf[:, c * LANES:(c + 1) * LANES]
        _to_strided_view(q4_ref, chunk, scr, 4, 3 * BW)
        _to_strided_view(q16_ref, chunk, scr, 16, 3 * BW)

    row = lambda c: bs((tm, c), lambda i: (i, 0))
    return _call(
        body, "norm_in_proj", (t // tm,), [row(D_MODEL), bs((1, D_MODEL), lambda i: (0, 0)), _resident(win)],
        [row(D_MODEL), row(N_IN), row(3 * BW), _view_spec(tm, 4, 3 * BW), _view_spec(tm, 16, 3 * BW)],
        [_sds((t, D_MODEL), BF16), _sds((t, N_IN), F32), _sds((t, 3 * BW), BF16),
         _sds((t // 4, 4 * 3 * BW), BF16), _sds((t // 16, 16 * 3 * BW), BF16)],
        scratch=[pltpu.VMEM((tm, LANES), F32)], after=after)(x, g, win)


def merge_fwd(h, ys, wg, wbr, after=None):
    t = h.shape[0]
    tm = min(512, t)

    def body(h_ref, ya, yb, yc, yd, wg_ref, wb_ref, m_ref, g_ref, b_ref):
        hh = h_ref[...]
        for j in range(N_SH):
            cs = slice(j * BW, (j + 1) * BW)
            acc = None
            for k, y_ref in enumerate((ya, yb, yc, yd)):
                g = _sigmoid(_nn(hh, wg_ref[k, :, cs]))
                b = _nn(y_ref[...], wb_ref[j, k])
                g_ref[k, :, cs] = g.astype(BF16)
                b_ref[k, :, cs] = b.astype(BF16)
                acc = g * b if acc is None else acc + g * b
            m_ref[:, cs] = acc.astype(BF16)

    ysp = bs((tm, BW), lambda i: (i, 0))
    big = bs((N_BR, tm, D_MODEL), lambda i: (0, i, 0))
    return _call(
        body, "merge_fwd", (t // tm,),
        [bs((tm, D_MODEL), lambda i: (i, 0)), ysp, ysp, ysp, ysp, _resident(wg), _resident(wbr)],
        [bs((tm, D_MODEL), lambda i: (i, 0)), big, big],
        [_sds((t, D_MODEL), BF16), _sds((N_BR, t, D_MODEL), BF16), _sds((N_BR, t, D_MODEL), BF16)], after=after)(
            h, *ys, wg, wbr)


def mm_residual(a, w, res, name, after=None):
    t, kk = a.shape
    tm = min(512, t)

    def body(a_ref, w_ref, r_ref, o_ref):
        o_ref[...] = r_ref[...] + _nn(a_ref[...], w_ref[...])

    row = bs((tm, D_MODEL), lambda i: (i, 0))
    return _call(body, name, (t // tm,), [bs((tm, kk), lambda i: (i, 0)), _resident(w), row], row,
                 _sds((t, D_MODEL), F32), after=after)(a, w, res)


def ffn_in(x, g, wfi):
    t = x.shape[0]
    tm = min(512, t)
    ns = wfi.shape[2]

    def body(x_ref, g_ref, w_ref, h_ref, f_ref, a_ref):
        h = _rms_fwd(x_ref[...], g_ref[...]).astype(BF16)
        h_ref[...] = h
        for j in range(2):
            cs = slice(j * ns, (j + 1) * ns)
            fg = _nn(h, w_ref[j])
            fu = _nn(h, w_ref[j + 2])
            f_ref[0, :, cs] = fg.astype(BF16)
            f_ref[1, :, cs] = fu.astype(BF16)
            a_ref[:, cs] = (fg * _sigmoid(fg) * fu).astype(BF16)

    row = lambda c: bs((tm, c), lambda i: (i, 0))
    return _call(
        body, "ffn_in", (t // tm,), [row(D_MODEL), bs((1, D_MODEL), lambda i: (0, 0)), _resident(wfi)],
        [row(D_MODEL), bs((2, tm, FFN_H), lambda i: (0, i, 0)), row(FFN_H)],
        [_sds((t, D_MODEL), BF16), _sds((2, t, FFN_H), BF16), _sds((t, FFN_H), BF16)])(x, g, wfi)


def ple_fwd(x, g, wpg, p_i, wpp):
    t = x.shape[0]
    tm = min(512, t)

    def body(x_ref, g_ref, wg_ref, p_ref, wp_ref, h_ref, gt_ref, pp_ref, o_ref):
        xv = x_ref[...]
        h = _rms_fwd(xv, g_ref[...]).astype(BF16)
        h_ref[...] = h
        gate = _sigmoid(_nn(h, wg_ref[...]))
        pb = p_ref[...].astype(BF16)
        pp = jnp.concatenate([_nn(pb, wp_ref[j]) for j in range(N_SH)], axis=1)
        gt_ref[...] = gate.astype(BF16)
        pp_ref[...] = pp.astype(BF16)
        o_ref[...] = xv + gate * pp

    row = bs((tm, D_MODEL), lambda i: (i, 0))
    return _call(
        body, "ple_fwd", (t // tm,),
        [row, bs((1, D_MODEL), lambda i: (0, 0)), _resident(wpg), bs((tm, BW), lambda i: (i, 0)), _resident(wpp)],
        [row, row, row, row],
        [_sds((t, D_MODEL), BF16), _sds((t, D_MODEL), BF16), _sds((t, D_MODEL), BF16), _sds((t, D_MODEL), F32)])(
            x, g, wpg, p_i, wpp)


def loss_head(x, g, tgt):
    t = x.shape[0]
    tm = min(512, t)

    def body(x_ref, g_ref, t_ref, l_ref, dx_ref, dg_ref):
        @pl.when(pl.program_id(0) == 0)
        def _():
            l_ref[...] = jnp.zeros_like(l_ref)
            dg_ref[...] = jnp.zeros_like(dg_ref)

        xv, gv = x_ref[...], g_ref[...]
        err = _rms_fwd(xv, gv) - t_ref[...]
        part = 0.5 * jnp.sum(jnp.mean(err * err, axis=-1, keepdims=True), axis=0, keepdims=True)
        l_ref[...] += jnp.broadcast_to(part, l_ref.shape)
        dx, dgr = _rms_bwd(err * (1.0 / D_MODEL), xv, gv)
        dx_ref[...] = dx
        dg_ref[...] += _colsum(dgr)

    row = bs((tm, D_MODEL), lambda i: (i, 0))
    vec = bs((1, D_MODEL), lambda i: (0, 0))
    return _call(body, "loss_head", (t // tm,), [row, vec, row],
                 [bs((1, 128), lambda i: (0, 0)), row, vec],
                 [_sds((1, 128), F32), _sds((t, D_MODEL), F32), _sds((1, D_MODEL), F32)])(x, g, tgt)


def tn_matmul(name, a, b, grid, a_spec, b_spec, out_spec, out_shape, split=0, split_cols=0, into=None, after=None):
    last = len(grid) - 1

    def body(a_ref, b_ref, *rest):
        o_ref = rest[-1]

        @pl.when(pl.program_id(last) == 0)
        def _():
            o_ref[...] = jnp.zeros_like(o_ref)

        res = _tn(a_ref[...].astype(BF16), b_ref[...].astype(BF16))
        if split_cols:
            cols = res.shape[1] // split_cols
            for s in range(split_cols):
                o_ref[s] += res[:, s * cols:(s + 1) * cols]
        elif split:
            rows = res.shape[0] // split
            for s in range(split):
                o_ref[s] += res[s * rows:(s + 1) * rows]
        else:
            o_ref[...] += res

    if into is None:
        return _call(body, name, grid, [a_spec, b_spec], out_spec, out_shape, after=after)(a, b)
    return _call(body, name, grid, [a_spec, b_spec, ANY], out_spec, out_shape, aliases={2: 0}, after=after)(a, b, into)


def _resident(w):
    zeros = (0,) * w.ndim
    return bs(w.shape, lambda i: zeros, pipeline_mode=pl.Buffered(1))


def norm_bwd(name, sources, dx_in, x, g):
    t = x.shape[0]
    tm = min(512, t)
    n_src = len(sources)

    def body(*refs):
        dxi_ref, x_ref, g_ref, dx_ref, dg_ref = refs[2 * n_src:]

        @pl.when(pl.program_id(0) == 0)
        def _():
            dg_ref[...] = jnp.zeros_like(dg_ref)

        dh = None
        for si in range(n_src):
            for av, wv in sources[si][3](refs[2 * si], refs[2 * si + 1]):
                part = _nt(av, wv)
                dh = part if dh is None else dh + part
        dx, dgr = _rms_bwd(dh, x_ref[...], g_ref[...])
        dx_ref[...] = dxi_ref[...] + dx
        dg_ref[...] += _colsum(dgr)

    in_specs, args = [], []
    for a, a_spec, w, _ in sources:
        in_specs += [a_spec(tm), _resident(w)]
        args += [a, w]
    row = bs((tm, D_MODEL), lambda i: (i, 0))
    vec = bs((1, D_MODEL), lambda i: (0, 0))
    return _call(body, name, (t // tm,), in_specs + [row, row, vec], [row, vec],
                 [_sds((t, D_MODEL), F32), _sds((1, D_MODEL), F32)])(*args, dx_in, x, g)


def ple_bwd(dx, gate, pp, wpg, x, g, after=None):
    t = dx.shape[0]
    tm = min(512, t)

    def body(dx_ref, gt_ref, p_ref, w_ref, x_ref, g_ref, dpre_ref, dpp_ref, o_ref, dg_ref):
        @pl.when(pl.program_id(0) == 0)
        def _():
            dg_ref[...] = jnp.zeros_like(dg_ref)

        d = dx_ref[...]
        gt = gt_ref[...].astype(F32)
        dpre = (d * p_ref[...].astype(F32) * gt * (1.0 - gt)).astype(BF16)
        dpre_ref[...] = dpre
        dpp_ref[...] = (d * gt).astype(BF16)
        dxn, dgr = _rms_bwd(_nt(dpre, w_ref[...]), x_ref[...], g_ref[...])
        o_ref[...] = d + dxn
        dg_ref[...] += _colsum(dgr)

    row = bs((tm, D_MODEL), lambda i: (i, 0))
    vec = bs((1, D_MODEL), lambda i: (0, 0))
    return _call(body, "ple_bwd", (t // tm,), [row, row, row, _resident(wpg), row, vec], [row, row, row, vec],
                 [_sds((t, D_MODEL), BF16), _sds((t, D_MODEL), BF16), _sds((t, D_MODEL), F32), _sds((1, D_MODEL), F32)],
                 after=after)(dx, gate, pp, wpg, x, g)


def ffn_bwd(dx, wfo, fgu, wfi, x, g, after=None):
    t = dx.shape[0]
    tm = min(256, t)
    ns = FFN_H // 2

    def body(dx_ref, wo_ref, f_ref, wi_ref, x_ref, g_ref, df_ref, o_ref, dg_ref):
        @pl.when(pl.program_id(0) == 0)
        def _():
            dg_ref[...] = jnp.zeros_like(dg_ref)

        d = dx_ref[...]
        dxb = d.astype(BF16)
        dh = None
        for j in range(2):
            cs = slice(j * ns, (j + 1) * ns)
            dact = _nt(dxb, wo_ref[cs, :])
            fg = f_ref[0, :, cs].astype(F32)
            fu = f_ref[1, :, cs].astype(F32)
            s = _sigmoid(fg)
            dfg = (dact * fu * (s * (1.0 + fg * (1.0 - s)))).astype(BF16)
            dfu = (dact * fg * s).astype(BF16)
            df_ref[0, :, cs] = dfg
            df_ref[1, :, cs] = dfu
            part = _nt(dfg, wi_ref[j]) + _nt(dfu, wi_ref[j + 2])
            dh = part if dh is None else dh + part
        dxn, dgr = _rms_bwd(dh, x_ref[...], g_ref[...])
        o_ref[...] = d + dxn
        dg_ref[...] += _colsum(dgr)

    blk = bs((2, tm, FFN_H), lambda i: (0, i, 0))
    row = bs((tm, D_MODEL), lambda i: (i, 0))
    vec = bs((1, D_MODEL), lambda i: (0, 0))
    return _call(body, "ffn_bwd", (t // tm,), [row, _resident(wfo), blk, _resident(wfi), row, vec], [blk, row, vec],
                 [_sds((2, t, FFN_H), BF16), _sds((t, D_MODEL), F32), _sds((1, D_MODEL), F32)],
                 after=after)(dx, wfo, fgu, wfi, x, g)


def merge_bwd(dx, wout, gates, ybr, wbr):
    t = dx.shape[0]
    tm = min(256, t)

    def body(dx_ref, w_ref, g_ref, b_ref, wb_ref, dpre_ref, dyb_ref, dy_ref):
        dm = _nt(dx_ref[...].astype(BF16), w_ref[...])
        for k in range(N_BR):
            g = g_ref[k].astype(F32)
            dpre_ref[k] = (dm * b_ref[k].astype(F32) * g * (1.0 - g)).astype(BF16)
            dyb = (dm * g).astype(BF16)
            dyb_ref[k] = dyb
            acc = None
            for s in range(N_SH):
                part = _nt(dyb[:, s * BW:(s + 1) * BW], wb_ref[s, k])
                acc = part if acc is None else acc + part
            dy_ref[k] = acc

    blk = bs((N_BR, tm, D_MODEL), lambda i: (0, i, 0))
    return _call(body, "merge_bwd", (t // tm,),
                 [bs((tm, D_MODEL), lambda i: (i, 0)), _resident(wout), blk, blk, _resident(wbr)],
                 [blk, blk, bs((N_BR, tm, BW), lambda i: (0, i, 0))],
                 [_sds((N_BR, t, D_MODEL), BF16), _sds((N_BR, t, D_MODEL), BF16), _sds((N_BR, t, BW), F32)])(
                     dx, wout, gates, ybr, wbr)


def conva_fwd(proj, wa):
    t = proj.shape[0]
    tm, halo = min(512, t), 8
    per = tm // halo

    def body(b_ref, c_ref, x_ref, ch_ref, xh_ref, w_ref, y_ref):
        zh = jnp.where(pl.program_id(0) > 0, ch_ref[...] * xh_ref[...], 0.0)
        zext = jnp.concatenate([zh, c_ref[...] * x_ref[...]], axis=0)
        y_ref[...] = (b_ref[...] * _causal_conv(zext, w_ref, CONVA_K, halo)).astype(BF16)

    col = lambda c: bs((tm, BW), lambda i: (i, c))
    hal = lambda c: bs((halo, BW), lambda i: (_prev_blk(i, per), c))
    return _call(body, "conva_fwd", (t // tm,),
                 [col(0), col(1), col(2), hal(1), hal(2), bs((CONVA_K, BW), lambda i: (0, 0))],
                 bs((tm, BW), lambda i: (i, 0)), _sds((t, BW), BF16))(proj, proj, proj, proj, proj, wa)


def conva_bwd(proj, dys, wa, after=None):
    t = proj.shape[0]
    tm, halo = min(512, t), 8
    per = tm // halo
    last = t // halo - 1
    nt = t // tm

    def body(b_ref, c_ref, x_ref, ch_ref, xh_ref, bn_ref, dy_ref, dyn_ref, w_ref, db_ref, dc_ref, dxx_ref, dw_ref):
        i = pl.program_id(0)

        @pl.when(i == 0)
        def _():
            dw_ref[...] = jnp.zeros_like(dw_ref)

        zh = jnp.where(i > 0, ch_ref[...] * xh_ref[...], 0.0)
        cv, xv = c_ref[...], x_ref[...]
        zext = jnp.concatenate([zh, cv * xv], axis=0)
        dy = dy_ref[...]
        dconv = dy * b_ref[...]
        dcn = jnp.where(i < nt - 1, dyn_ref[...] * bn_ref[...], 0.0)
        dz = _anti_conv(jnp.concatenate([dconv, dcn], axis=0), w_ref, CONVA_K, tm)
        db_ref[...] = (dy * _causal_conv(zext, w_ref, CONVA_K, halo)).astype(BF16)
        dc_ref[...] = (dz * xv).astype(BF16)
        dxx_ref[...] = (dz * cv).astype(BF16)
        _conv_wgrad(dw_ref, dconv, zext, CONVA_K, halo)

    col = lambda c: bs((tm, BW), lambda i: (i, c))
    hal = lambda c: bs((halo, BW), lambda i: (_prev_blk(i, per), c))
    nxt = bs((halo, BW), lambda i: (_next_blk(i, per, last), 0))
    wsp = bs((CONVA_K, BW), lambda i: (0, 0))
    outs = _call(body, "conva_bwd", (t // tm,),
                 [col(0), col(1), col(2), hal(1), hal(2), nxt,
                  bs((None, tm, BW), lambda i: (0, i, 0)), bs((None, halo, BW), lambda i: (0, _next_blk(i, per, last), 0)), wsp],
                 [bs((tm, BW), lambda i: (i, 0))] * 3 + [wsp],
                 [_sds((t, BW), BF16)] * 3 + [_sds((CONVA_K, BW), F32)], after=after)(
                     proj, proj, proj, proj, proj, proj, dys, dys, wa)
    return outs[:3], outs[3]


def _head_masks():
    lane = lax.broadcasted_iota(jnp.int32, (1, BW), 1)
    return [(lane >= h * HEAD_D) & (lane < (h + 1) * HEAD_D) for h in range(HEADS)]


def _band_masks():
    qi = lax.broadcasted_iota(jnp.int32, (BLK, BLK), 0)
    ki = lax.broadcasted_iota(jnp.int32, (BLK, BLK), 1)
    return ki >= qi, ki <= qi


def attn_fwd_group(pv, d):
    rows = pv.shape[0]
    qb = min(512, rows)
    nb = qb // BLK
    scale = HEAD_D ** -0.5

    def body(q_ref, k_ref, v_ref, kh_ref, vh_ref, o_ref):
        n = pl.program_id(1)
        hm = _head_masks()
        m_prev, m_cur = _band_masks()
        for b in range(nb):
            rs = slice(b * BLK, (b + 1) * BLK)
            q = q_ref[rs, :]
            if b == 0:
                kp, vp = kh_ref[...], vh_ref[...]
                mp = m_prev & (n > 0)
            else:
                ps = slice((b - 1) * BLK, b * BLK)
                kp, vp = k_ref[ps, :], v_ref[ps, :]
                mp = m_prev
            qs = jnp.concatenate([jnp.where(hm[h], q, 0.0).astype(BF16) for h in range(HEADS)], axis=0)
            kcat = jnp.concatenate([kp, k_ref[rs, :]], axis=0)
            vcat = jnp.concatenate([vp, v_ref[rs, :]], axis=0)
            band = jnp.concatenate([mp, m_cur], axis=1)
            s = jnp.where(jnp.concatenate([band] * HEADS, axis=0), _nt(qs, kcat) * scale, NEG)
            m = jnp.max(s, axis=-1, keepdims=True)
            e = jnp.exp(s - m)
            l = jnp.sum(e, axis=-1, keepdims=True)
            of = _nn(e.astype(BF16), vcat) / l
            lse = m + jnp.log(l)
            o_acc = jnp.zeros((BLK, BW), F32)
            l_acc = jnp.zeros((BLK, BW), F32)
            for h in range(HEADS):
                hs = slice(h * BLK, (h + 1) * BLK)
                o_acc = jnp.where(hm[h], of[hs, :], o_acc)
                l_acc = jnp.where(hm[h], lse[hs, :], l_acc)
            o_ref[rs, :BW] = o_acc
            o_ref[rs, BW:] = l_acc

    per = qb // BLK
    main = lambda c: bs((qb, BW), lambda r, n: (n, r * 3 + c))
    hal = lambda c: bs((BLK, BW), lambda r, n: (_prev_blk(n, per), r * 3 + c))
    return _call(body, f"attn_fwd_d{d}", (d, rows // qb), [main(0), main(1), main(2), hal(1), hal(2)],
                 bs((qb, 2 * BW), lambda r, n: (n, r)), _sds((rows, d * 2 * BW), F32))(pv, pv, pv, pv, pv)


def attn_merge(ols):
    t = ols[0].shape[0]
    tm = min(512, t)
    width = 2 * BW

    def lse3(a, b, c):
        m = jnp.maximum(jnp.maximum(a, b), c)
        return m + jnp.log(jnp.exp(a - m) + jnp.exp(b - m) + jnp.exp(c - m))

    def body(g0, g1, g2, y_ref, o_ref, l_ref, scr, nat1, nat2):
        for src, nat, d in ((g1, nat1, DILATIONS[1]), (g2, nat2, DILATIONS[2])):
            for c in range(width // LANES):
                nat[:, c * LANES:(c + 1) * LANES] = _from_strided_view(src, scr, d, width, c)
        gs = [g0[...], nat1[...], nat2[...]]
        ls = [g[:, BW:] for g in gs]
        tot = lse3(*ls)
        o = (jnp.exp(ls[0] - tot) * gs[0][:, :BW] + jnp.exp(ls[1] - tot) * gs[1][:, :BW]
             + jnp.exp(ls[2] - tot) * gs[2][:, :BW])
        y_ref[...] = o.astype(BF16)
        o_ref[...] = o
        l_ref[...] = tot

    n = bs((tm, BW), lambda i: (i, 0))
    return _call(body, "attn_merge", (t // tm,),
                 [_view_spec(tm, 1, width), _view_spec(tm, DILATIONS[1], width), _view_spec(tm, DILATIONS[2], width)],
                 [n, n, n], [_sds((t, BW), BF16), _sds((t, BW), F32), _sds((t, BW), F32)],
                 scratch=[pltpu.VMEM((tm, LANES), F32), pltpu.VMEM((tm, width), F32), pltpu.VMEM((tm, width), F32)])(*ols)


def attn_delta(dys, o, lse):
    t = o.shape[0]
    tm = min(512, t)

    def body(d_ref, o_ref, l_ref, ld1, ld4, ld16, dy4, dy16, scr):
        hm = _head_masks()
        dy = d_ref[...]
        prod = dy * o_ref[...]
        delta = jnp.zeros_like(prod)
        for h in range(HEADS):
            delta = jnp.where(hm[h], jnp.sum(jnp.where(hm[h], prod, 0.0), axis=-1, keepdims=True), delta)
        ld = jnp.concatenate([l_ref[...], delta], axis=1)
        ld1[...] = ld
        for d, ld_v, dy_v in ((DILATIONS[1], ld4, dy4), (DILATIONS[2], ld16, dy16)):
            _to_strided_view(ld_v, lambda c: ld[:, c * LANES:(c + 1) * LANES], scr, d, 2 * BW)
            _to_strided_view(dy_v, lambda c: dy[:, c * LANES:(c + 1) * LANES], scr, d, BW)

    n = bs((tm, BW), lambda i: (i, 0))
    d4, d16 = DILATIONS[1], DILATIONS[2]
    outs = _call(body, "attn_delta", (t // tm,), [bs((None, tm, BW), lambda i: (1, i, 0)), n, n],
                 [_view_spec(tm, 1, 2 * BW), _view_spec(tm, d4, 2 * BW), _view_spec(tm, d16, 2 * BW),
                  _view_spec(tm, d4, BW), _view_spec(tm, d16, BW)],
                 [_sds((t, 2 * BW), F32), _sds((t // d4, d4 * 2 * BW), F32), _sds((t // d16, d16 * 2 * BW), F32),
                  _sds((t // d4, d4 * BW), F32), _sds((t // d16, d16 * BW), F32)],
                 scratch=[pltpu.VMEM((tm, LANES), F32)])(dys, o, lse)
    return outs[:3], outs[3:]


def attn_bwd_group(pv, dov, ldv, d):
    rows = pv.shape[0]
    qb = min(512, rows)
    nb = qb // BLK
    nsteps = rows // qb
    scale = HEAD_D ** -0.5

    def body(q_ref, qn_ref, k_ref, kh_ref, v_ref, vh_ref, do_ref, don_ref, ld_ref, ldn_ref, o_ref):
        n = pl.program_id(1)
        hm = _head_masks()
        m_prev, m_cur = _band_masks()
        has_prev, has_next = n > 0, n < nsteps - 1
        dq = [None] * nb
        dk = [jnp.zeros((BLK, BW), F32) for _ in range(nb)]
        dvv = [jnp.zeros((BLK, BW), F32) for _ in range(nb)]
        for qi in range(nb + 1):
            rs = slice(qi * BLK, (qi + 1) * BLK)
            ps = slice((qi - 1) * BLK, qi * BLK)
            if qi < nb:
                q, do, ldq = q_ref[rs, :], do_ref[rs, :], ld_ref[rs, :]
            else:
                q, do, ldq = qn_ref[...], don_ref[...], ldn_ref[...]
            kp, vp = (kh_ref[...], vh_ref[...]) if qi == 0 else (k_ref[ps, :], v_ref[ps, :])
            kc, vc = (k_ref[rs, :], v_ref[rs, :]) if qi < nb else (kp, vp)
            mp = m_prev & has_prev if qi == 0 else (m_prev & has_next if qi == nb else m_prev)
            mc = m_cur if qi < nb else jnp.zeros_like(m_cur)
            band = jnp.concatenate([jnp.concatenate([mp, mc], axis=1)] * HEADS, axis=0)
            qs = jnp.concatenate([jnp.where(hm[h], q, 0.0).astype(BF16) for h in range(HEADS)], axis=0)
            dos = jnp.concatenate([jnp.where(hm[h], do, 0.0).astype(BF16) for h in range(HEADS)], axis=0)
            kcat = jnp.concatenate([kp, kc], axis=0)
            vcat = jnp.concatenate([vp, vc], axis=0)
            col = lambda v, h: jnp.broadcast_to(jnp.max(jnp.where(hm[h], v, NEG), axis=-1, keepdims=True), (BLK, 2 * BLK))
            lcols = jnp.concatenate([col(ldq[:, :BW], h) for h in range(HEADS)], axis=0)
            dcols = jnp.concatenate([col(ldq[:, BW:], h) for h in range(HEADS)], axis=0)
            p = jnp.where(band, jnp.exp(_nt(qs, kcat) * scale - lcols), 0.0)
            ds = (p * (_nt(dos, vcat) - dcols) * scale).astype(BF16)
            if qi < nb:
                dqf = _nn(ds, kcat)
                acc_q = jnp.zeros((BLK, BW), F32)
                for h in range(HEADS):
                    acc_q = jnp.where(hm[h], dqf[h * BLK:(h + 1) * BLK, :], acc_q)
                dq[qi] = acc_q
            dkc = _tn(ds, qs)
            dvc = _tn(p.astype(BF16), dos)
            if qi >= 1:
                dk[qi - 1] = dk[qi - 1] + dkc[:BLK]
                dvv[qi - 1] = dvv[qi - 1] + dvc[:BLK]
            if qi < nb:
                dk[qi] = dk[qi] + dkc[BLK:]
                dvv[qi] = dvv[qi] + dvc[BLK:]
        for b in range(nb):
            rs = slice(b * BLK, (b + 1) * BLK)
            for c, val in enumerate((dq[b], dk[b], dvv[b])):
                cs = slice(c * BW, (c + 1) * BW)
                o_ref[rs, cs] = val

    per = qb // BLK
    last = rows // BLK - 1
    main = lambda c: bs((qb, BW), lambda r, n: (n, r * 3 + c))
    prv = lambda c: bs((BLK, BW), lambda r, n: (_prev_blk(n, per), r * 3 + c))
    nxt = lambda c: bs((BLK, BW), lambda r, n: (_next_blk(n, per, last), r * 3 + c))
    accs = bs((qb, 3 * BW), lambda r, n: (n, r))
    in_specs = [main(0), nxt(0), main(1), prv(1), main(2), prv(2),
                bs((qb, BW), lambda r, n: (n, r)), bs((BLK, BW), lambda r, n: (_next_blk(n, per, last), r)),
                bs((qb, 2 * BW), lambda r, n: (n, r)), bs((BLK, 2 * BW), lambda r, n: (_next_blk(n, per, last), r))]
    args = [pv, pv, pv, pv, pv, pv, dov, dov, ldv, ldv]
    return _call(body, f"attn_bwd_d{d}", (d, nsteps), in_specs, accs, _sds((rows, d * 3 * BW), F32))(*args)


def attn_bwd_finish(parts):
    t = parts[0].shape[0]
    tm = min(512, t)
    width = 3 * BW

    def body(g0, g1, g2, o_ref, scr):
        for c in range(width // LANES):
            cs = slice(c * LANES, (c + 1) * LANES)
            acc = g0[:, cs]
            acc = acc + _from_strided_view(g1, scr, DILATIONS[1], width, c)
            acc = acc + _from_strided_view(g2, scr, DILATIONS[2], width, c)
            o_ref[:, cs] = acc.astype(BF16)

    return _call(body, "attn_bwd_finish", (t // tm,),
                 [_view_spec(tm, 1, width), _view_spec(tm, DILATIONS[1], width), _view_spec(tm, DILATIONS[2], width)],
                 bs((tm, width), lambda i: (i, 0)), _sds((t, width), BF16),
                 scratch=[pltpu.VMEM((tm, LANES), F32)])(*parts)


def _group_masks():
    lane = lax.broadcasted_iota(jnp.int32, (1, BW), 1)
    return [(lane >= g * HEAD_D) & (lane < (g + 1) * HEAD_D) for g in range(4)]


def sgu_fwd(proj, ln_g, ln_b, w_tril, b_full):
    t = proj.shape[0]
    tm = min(512, t)

    def body(u_ref, v_ref, g_ref, b_ref, w_ref, bf_ref, y_ref):
        gm = _group_masks()
        xhat, _ = _ln_hat(v_ref[...])
        vb = (xhat * g_ref[...] + b_ref[...]).astype(BF16)
        for c in range(tm // BLK):
            rs = slice(c * BLK, (c + 1) * BLK)
            vc = vb[rs, :]
            mixed = bf_ref[...]
            for g in range(4):
                mixed = mixed + jnp.where(gm[g], _nn(w_ref[g], vc), 0.0)
            y_ref[rs, :] = (u_ref[rs, :] * mixed).astype(BF16)

    vec = bs((1, BW), lambda i: (0, 0))
    return _call(body, "sgu_fwd", (t // tm,),
                 [bs((tm, BW), lambda i: (i, 6)), bs((tm, BW), lambda i: (i, 7)), vec, vec,
                  bs((4, BLK, BLK), lambda i: (0, 0, 0)), bs((BLK, BW), lambda i: (0, 0))],
                 bs((tm, BW), lambda i: (i, 0)), _sds((t, BW), BF16))(proj, proj, ln_g, ln_b, w_tril, b_full)


def sgu_bwd(proj, dys, ln_g, ln_b, w_tril, b_full):
    t = proj.shape[0]
    tm = min(512, t)

    def body(u_ref, v_ref, dy_ref, g_ref, b_ref, w_ref, bf_ref, du_ref, dv_ref, dw_ref, dbf_ref, dg_ref, db_ref, dvl_ref):
        @pl.when(pl.program_id(0) == 0)
        def _():
            dw_ref[...] = jnp.zeros_like(dw_ref)
            dbf_ref[...] = jnp.zeros_like(dbf_ref)
            dg_ref[...] = jnp.zeros_like(dg_ref)
            db_ref[...] = jnp.zeros_like(db_ref)

        gm = _group_masks()
        xhat, r = _ln_hat(v_ref[...])
        gv = g_ref[...]
        vb = (xhat * gv + b_ref[...]).astype(BF16)
        for c in range(tm // BLK):
            rs = slice(c * BLK, (c + 1) * BLK)
            vc = vb[rs, :]
            dy = dy_ref[rs, :]
            mixed = bf_ref[...]
            for g in range(4):
                mixed = mixed + jnp.where(gm[g], _nn(w_ref[g], vc), 0.0)
            du_ref[rs, :] = (dy * mixed).astype(BF16)
            dm = dy * u_ref[rs, :]
            dbf_ref[...] += dm
            dvl = jnp.zeros((BLK, BW), F32)
            for g in range(4):
                dmg = jnp.where(gm[g], dm, 0.0).astype(BF16)
                dw_ref[g] += _nt(dmg, vc)
                dvl = dvl + _tn(w_ref[g], dmg)
            dvl_ref[rs, :] = dvl
        dvl = dvl_ref[...]
        dv_ref[...] = _ln_bwd(dvl, xhat, r, gv).astype(BF16)
        dg_ref[...] += _colsum(dvl * xhat)
        db_ref[...] += _colsum(dvl)

    vec = bs((1, BW), lambda i: (0, 0))
    row = bs((tm, BW), lambda i: (i, 0))
    wsp = bs((4, BLK, BLK), lambda i: (0, 0, 0))
    bfs = bs((BLK, BW), lambda i: (0, 0))
    return _call(body, "sgu_bwd", (t // tm,),
                 [bs((tm, BW), lambda i: (i, 6)), bs((tm, BW), lambda i: (i, 7)), bs((None, tm, BW), lambda i: (2, i, 0)),
                  vec, vec, wsp, bfs],
                 [row, row, wsp, bfs, vec, vec],
                 [_sds((t, BW), BF16), _sds((t, BW), BF16), _sds((4, BLK, BLK), F32), _sds((BLK, BW), F32),
                  _sds((1, BW), F32), _sds((1, BW), F32)],
                 scratch=[pltpu.VMEM((tm, BW), F32)])(proj, proj, dys, ln_g, ln_b, w_tril, b_full)


CONF_HALO = 32


def conf_fwd(proj, dw, ln_g, ln_b, after=None):
    t = proj.shape[0]
    tm, halo = min(512, t), CONF_HALO
    per = tm // halo

    def body(v_ref, gt_ref, vh_ref, gh_ref, w_ref, g_ref, b_ref, y_ref, z_ref):
        yh = jnp.where(pl.program_id(0) > 0, vh_ref[...] * _sigmoid(gh_ref[...]), 0.0)
        yext = jnp.concatenate([yh, v_ref[...] * _sigmoid(gt_ref[...])], axis=0)
        z = _causal_conv(yext, w_ref, CONF_K, halo)
        z_ref[...] = z
        xhat, _ = _ln_hat(z)
        ln = xhat * g_ref[...] + b_ref[...]
        y_ref[...] = (ln * _sigmoid(ln)).astype(BF16)

    vec = bs((1, BW), lambda i: (0, 0))
    col = lambda c: bs((tm, BW), lambda i: (i, c))
    hal = lambda c: bs((halo, BW), lambda i: (_prev_blk(i, per), c))
    row = bs((tm, BW), lambda i: (i, 0))
    return _call(body, "conf_fwd", (t // tm,),
                 [col(8), col(9), hal(8), hal(9), bs((CONF_K, BW), lambda i: (0, 0)), vec, vec],
                 [row, row], [_sds((t, BW), BF16), _sds((t, BW), F32)], after=after)(
                     proj, proj, proj, proj, dw, ln_g, ln_b)


def conf_bwd_ln(z, dys, ln_g, ln_b):
    t = z.shape[0]
    tm = min(1024, t)

    def body(z_ref, dy_ref, g_ref, b_ref, dz_ref, dg_ref, db_ref):
        @pl.when(pl.program_id(0) == 0)
        def _():
            dg_ref[...] = jnp.zeros_like(dg_ref)
            db_ref[...] = jnp.zeros_like(db_ref)

        gv = g_ref[...]
        xhat, r = _ln_hat(z_ref[...])
        ln = xhat * gv + b_ref[...]
        s = _sigmoid(ln)
        dln = dy_ref[...] * (s * (1.0 + ln * (1.0 - s)))
        dz_ref[...] = _ln_bwd(dln, xhat, r, gv)
        dg_ref[...] += _colsum(dln * xhat)
        db_ref[...] += _colsum(dln)

    vec = bs((1, BW), lambda i: (0, 0))
    row = bs((tm, BW), lambda i: (i, 0))
    return _call(body, "conf_bwd_ln", (t // tm,), [row, bs((None, tm, BW), lambda i: (3, i, 0)), vec, vec],
                 [row, vec, vec], [_sds((t, BW), F32), _sds((1, BW), F32), _sds((1, BW), F32)])(z, dys, ln_g, ln_b)


def conf_bwd_conv(proj, dz, dw):
    t = proj.shape[0]
    tm, halo = min(512, t), CONF_HALO
    per = tm // halo
    last = t // halo - 1
    nt = t // tm

    def body(v_ref, gt_ref, vh_ref, gh_ref, dz_ref, dzn_ref, w_ref, dv_ref, dg_ref, dw_ref):
        i = pl.program_id(0)

        @pl.when(i == 0)
        def _():
            dw_ref[...] = jnp.zeros_like(dw_ref)

        val = v_ref[...]
        sg = _sigmoid(gt_ref[...])
        yh = jnp.where(i > 0, vh_ref[...] * _sigmoid(gh_ref[...]), 0.0)
        yext = jnp.concatenate([yh, val * sg], axis=0)
        dz = dz_ref[...]
        dzn = jnp.where(i < nt - 1, dzn_ref[...], 0.0)
        dy0 = _anti_conv(jnp.concatenate([dz, dzn], axis=0), w_ref, CONF_K, tm)
        dv_ref[...] = (dy0 * sg).astype(BF16)
        dg_ref[...] = (dy0 * val * sg * (1.0 - sg)).astype(BF16)
        _conv_wgrad(dw_ref, dz, yext, CONF_K, halo)

    col = lambda c: bs((tm, BW), lambda i: (i, c))
    hal = lambda c: bs((halo, BW), lambda i: (_prev_blk(i, per), c))
    row = bs((tm, BW), lambda i: (i, 0))
    wsp = bs((CONF_K, BW), lambda i: (0, 0))
    return _call(body, "conf_bwd_conv", (t // tm,),
                 [col(8), col(9), hal(8), hal(9), row, bs((halo, BW), lambda i: (_next_blk(i, per, last), 0)), wsp],
                 [row, row, wsp], [_sds((t, BW), BF16), _sds((t, BW), BF16), _sds((CONF_K, BW), F32)])(
                     proj, proj, proj, proj, dz, dz, dw)


def _place():
    return lax.axis_index("x"), lax.axis_index("y"), lax.axis_index("c")


def _comm_call(body, name, n_in, out_shape, scratch, aliases=None):
    return pl.pallas_call(body, name=name, in_specs=[ANY] * n_in, out_specs=[ANY] * len(out_shape), out_shape=out_shape,
                          scratch_shapes=scratch, input_output_aliases=aliases or {},
                          compiler_params=pltpu.CompilerParams(has_side_effects=True, vmem_limit_bytes=VMEM_LIMIT))


HBM_SPEC = pl.BlockSpec(memory_space=pltpu.HBM)
SEM_SPEC = pl.BlockSpec(memory_space=pltpu.SEMAPHORE)
EFFECT = pltpu.SideEffectType.DATAFLOW_SIDE_EFFECTING


class SplitExchange:
    def __init__(self, name, bufs, plan, n_copies):
        self.name, self.bufs, self.plan, self.n = name, list(bufs), plan, n_copies

    def start(self, after):
        nb, n, plan = len(self.bufs), self.n, self.plan

        def body(*refs):
            send, recv, token = refs[nb + 1], refs[nb + 2], refs[-1]
            for k, (src, dst, _, dev) in enumerate(plan(refs[:nb])):
                pltpu.make_async_remote_copy(src_ref=src, dst_ref=dst, send_sem=send.at[k], recv_sem=recv.at[k],
                                             device_id=dev, device_id_type=MESH).start()
            token[...] = jnp.zeros_like(token)

        outs = pl.pallas_call(
            body, name=self.name + "_start",
            out_shape=(pltpu.SemaphoreType.DMA((n,)), pltpu.SemaphoreType.DMA((n,)),
                       *[pltpu.HBM(b.shape, b.dtype) for b in self.bufs], _sds((8, 128), F32)),
            in_specs=[HBM_SPEC] * nb + [ANY],
            out_specs=(SEM_SPEC, SEM_SPEC, *[HBM_SPEC] * nb, pl.BlockSpec(memory_space=pltpu.VMEM)),
            input_output_aliases={i: 2 + i for i in range(nb)},
            compiler_params=pltpu.CompilerParams(has_side_effects=EFFECT))(
                *[pltpu.with_memory_space_constraint(b, pltpu.HBM) for b in self.bufs], after)
        self.send, self.recv, self.bufs = outs[0], outs[1], list(outs[2:2 + nb])
        return outs[-1]

    def wait(self, after):
        nb, plan = len(self.bufs), self.plan
        after = list(after) if isinstance(after, (list, tuple)) else [after]

        def body(*refs):
            send, recv = refs[nb], refs[nb + 1]
            for k, (src, _, land, dev) in enumerate(plan(refs[:nb])):
                cp = pltpu.make_async_remote_copy(src_ref=src, dst_ref=land, send_sem=send.at[k], recv_sem=recv.at[k],
                                                  device_id=dev, device_id_type=MESH)
                cp.wait_send()
                cp.wait_recv()

        outs = pl.pallas_call(
            body, name=self.name + "_wait", out_shape=tuple(pltpu.HBM(b.shape, b.dtype) for b in self.bufs),
            in_specs=[HBM_SPEC] * nb + [SEM_SPEC, SEM_SPEC] + [ANY] * len(after), out_specs=[HBM_SPEC] * nb,
            input_output_aliases={i: i for i in range(nb)},
            compiler_params=pltpu.CompilerParams(has_side_effects=EFFECT))(*self.bufs, self.send, self.recv, *after)
        return list(outs)


def _chips_of(x, y):
    return [(1 - x, y), (x, 1 - y), (1 - x, 1 - y)]


def allgather_ici_plan(shapes):
    def plan(refs):
        x, y, c = _place()
        out = []
        for a, ref in enumerate(refs):
            hl = shapes[a][1] // 2
            half = pl.ds(c * hl, hl)
            for cx, cy in _chips_of(x, y):
                mine = ref.at[2 * x + y, half]
                out.append((mine, mine, ref.at[2 * cx + cy, half], (cx, cy, c)))
        return out
    return plan


def allgather_d2d_plan(shapes):
    def plan(refs):
        x, y, c = _place()
        out = []
        for a, ref in enumerate(refs):
            hl = shapes[a][1] // 2
            for cx, cy in _chips_of(x, y):
                got = ref.at[2 * cx + cy, pl.ds(c * hl, hl)]
                out.append((got, got, ref.at[2 * cx + cy, pl.ds((1 - c) * hl, hl)], (x, y, 1 - c)))
        return out
    return plan


def gather8(v, reduce):
    rows, cols = v.shape

    def body(v_ref, o_ref, land_ref, send, recv, lsem):
        x, y, c = _place()
        me = 4 * x + 2 * y + c
        land = land_ref if reduce else o_ref
        mine = pltpu.make_async_copy(v_ref, land.at[me], lsem)
        mine.start()
        sent = []
        for j in range(1, 8):
            fx, fy, fc = (j >> 2) & 1, (j >> 1) & 1, j & 1
            tgt = (1 - x if fx else x, 1 - y if fy else y, 1 - c if fc else c)
            cp = pltpu.make_async_remote_copy(src_ref=v_ref, dst_ref=land.at[me], send_sem=send.at[j - 1],
                                              recv_sem=recv.at[j - 1], device_id=tgt, device_id_type=MESH)
            cp.start()
            sent.append(cp)
        for j in range(1, 8):
            fx, fy, fc = (j >> 2) & 1, (j >> 1) & 1, j & 1
            peer = 4 * (1 - x if fx else x) + 2 * (1 - y if fy else y) + (1 - c if fc else c)
            pltpu.make_async_remote_copy(src_ref=v_ref, dst_ref=land.at[peer], send_sem=send.at[j - 1],
                                         recv_sem=recv.at[j - 1], device_id=(x, y, c), device_id_type=MESH).wait_recv()
        for cp in sent:
            cp.wait_send()
        mine.wait()
        if reduce:
            acc = land_ref[0]
            for k in range(1, 8):
                acc = acc + land_ref[k]
            o_ref[...] = acc

    vm = pl.BlockSpec(memory_space=pltpu.VMEM)
    out_shape = _sds((rows, cols), F32) if reduce else _sds((8, rows, cols), F32)
    land_shape = (8, rows, cols) if reduce else (8, 128)
    return pl.pallas_call(
        body, name="allreduce8" if reduce else "allgather8", in_specs=[vm], out_specs=vm, out_shape=out_shape,
        scratch_shapes=[pltpu.VMEM(land_shape, F32), pltpu.SemaphoreType.DMA((7,)), pltpu.SemaphoreType.DMA((7,)),
                        pltpu.SemaphoreType.DMA],
        compiler_params=pltpu.CompilerParams(has_side_effects=True, vmem_limit_bytes=VMEM_LIMIT))(v)


def allgather_weights(bufs):
    n = len(bufs)

    def body(*refs):
        ins, outs = refs[:n], refs[n:2 * n]
        send, recv = refs[2 * n:]
        x, y, c = _place()
        s_me = 2 * x + y
        chips = [(1 - x, y), (x, 1 - y), (1 - x, 1 - y)]
        sibling = (x, y, 1 - c)
        started = []
        for a in range(n):
            hl = bufs[a].shape[1] // 2
            half = pl.ds(c * hl, hl)
            for j, chip in enumerate(chips):
                cp = pltpu.make_async_remote_copy(src_ref=ins[a].at[s_me, half], dst_ref=outs[a].at[s_me, half],
                                                  send_sem=send.at[6 * a + j], recv_sem=recv.at[6 * a + j],
                                                  device_id=(chip[0], chip[1], c), device_id_type=MESH)
                cp.start()
                started.append(cp)
        for a in range(n):
            hl = bufs[a].shape[1] // 2
            half = pl.ds(c * hl, hl)
            for j, chip in enumerate(chips):
                s_j = 2 * chip[0] + chip[1]
                landed = outs[a].at[s_j, half]
                pltpu.make_async_remote_copy(src_ref=landed, dst_ref=landed, send_sem=send.at[6 * a + j],
                                             recv_sem=recv.at[6 * a + j], device_id=sibling, device_id_type=MESH).wait_recv()
                fw = pltpu.make_async_remote_copy(src_ref=landed, dst_ref=landed, send_sem=send.at[6 * a + 3 + j],
                                                  recv_sem=recv.at[6 * a + 3 + j], device_id=sibling, device_id_type=MESH)
                fw.start()
                started.append(fw)
        for a in range(n):
            hl = bufs[a].shape[1] // 2
            other = pl.ds((1 - c) * hl, hl)
            for j, chip in enumerate(chips):
                s_j = 2 * chip[0] + chip[1]
                theirs = outs[a].at[s_j, other]
                pltpu.make_async_remote_copy(src_ref=theirs, dst_ref=theirs, send_sem=send.at[6 * a + 3 + j],
                                             recv_sem=recv.at[6 * a + 3 + j], device_id=sibling, device_id_type=MESH).wait_recv()
        for cp in started:
            cp.wait_send()

    out_shape = [_sds(b.shape, b.dtype) for b in bufs]
    scratch = [pltpu.SemaphoreType.DMA((6 * n,)), pltpu.SemaphoreType.DMA((6 * n,))]
    return _comm_call(body, "allgather_weights", n, out_shape, scratch, aliases={a: a for a in range(n)})(*bufs)


def _row_tile(rows, cols):
    best = 16
    for t in range(16, rows + 1, 16):
        if rows % t == 0 and t * cols * 4 <= 2 * 1024 * 1024:
            best = t
    return best


def _rs_add_sibling(scal, g, ra, hr):
    cols = g.shape[2]
    tr = _row_tile(hr, cols)
    nr = hr // tr

    def body(s_ref, g_ref, r_ref, p32_ref, p16_ref):
        v = g_ref[...] + r_ref[...]
        p16_ref[...] = v.astype(BF16)

        @pl.when(pl.program_id(1) == s_ref[0])
        def _():
            p32_ref[...] = v

    blk = lambda f: bs((None, tr, cols), f)
    own = blk(lambda i, s, sr: (s, i, 0))
    spec = pltpu.PrefetchScalarGridSpec(num_scalar_prefetch=1, grid=(nr, N_SH),
                                        in_specs=[blk(lambda i, s, sr: (s, sr[1] * nr + i, 0)), own],
                                        out_specs=[bs((tr, cols), lambda i, s, sr: (i, 0)), own])
    return pl.pallas_call(body, name="rs_add_sibling", grid_spec=spec,
                          out_shape=[_sds((hr, cols), F32), _sds((N_SH, hr, cols), BF16)],
                          compiler_params=pltpu.CompilerParams(dimension_semantics=("arbitrary",) * 2,
                                                               vmem_limit_bytes=VMEM_LIMIT))(scal, g, ra)


def _rs_add_chips(scal, p32, rb, hr):
    cols = p32.shape[1]
    tr = _row_tile(hr, cols)
    nr = hr // tr

    def body(s_ref, p_ref, r0, r1, r2, o_ref):
        o_ref[...] = ((p_ref[...] + r0[...].astype(F32)) + r1[...].astype(F32)) + r2[...].astype(F32)

    blk = lambda f: bs((None, tr, cols), f)
    spec = pltpu.PrefetchScalarGridSpec(
        num_scalar_prefetch=1, grid=(nr,),
        in_specs=[bs((tr, cols), lambda i, sr: (i, 0))] + [blk(functools.partial(lambda i, sr, j: (j, i, 0), j=j))
                                                            for j in range(3)],
        out_specs=blk(lambda i, sr: (sr[1], i, 0)))
    return pl.pallas_call(body, name="rs_add_chips", grid_spec=spec, out_shape=_sds((2, hr, cols), F32),
                          compiler_params=pltpu.CompilerParams(dimension_semantics=("arbitrary",),
                                                               vmem_limit_bytes=VMEM_LIMIT))(scal, p32, rb, rb, rb)


class SplitReduceScatter:
    def __init__(self, gs):
        x, y, c = _place()
        self.scal = jnp.stack([2 * x + y, c]).astype(jnp.int32)
        self.gs, self.n = list(gs), len(gs)
        self.hrs = [g.shape[1] // 2 for g in gs]

    def swap_start(self, after):
        n, hrs = self.n, self.hrs

        def plan(refs):
            x, y, c = _place()
            return [(refs[a].at[:, pl.ds((1 - c) * hrs[a], hrs[a])], refs[n + a], refs[n + a], (x, y, 1 - c))
                    for a in range(n)]

        lands = [lax.empty((N_SH, hrs[a], g.shape[2]), F32) for a, g in enumerate(self.gs)]
        self.ex = SplitExchange("rs_swap_halves", self.gs + lands, plan, n)
        return self.ex.start(after)

    def swap_wait_send_start(self, after):
        n, hrs = self.n, self.hrs
        bufs = self.ex.wait(after)
        parts = [_rs_add_sibling(self.scal, bufs[a], bufs[n + a], hrs[a]) for a in range(n)]
        self.p32 = [p[0] for p in parts]

        def plan(refs):
            x, y, c = _place()
            return [(refs[a].at[2 * cx + cy], refs[n + a].at[j], refs[n + a].at[j], (cx, cy, c))
                    for a in range(n) for j, (cx, cy) in enumerate(_chips_of(x, y))]

        lands = [lax.empty((3, hrs[a], g.shape[2]), BF16) for a, g in enumerate(self.gs)]
        self.ex = SplitExchange("rs_send_partials", [p[1] for p in parts] + lands, plan, 3 * n)
        return self.ex.start(parts[-1][1])

    def send_wait_share_start(self, after):
        n, hrs = self.n, self.hrs
        bufs = self.ex.wait(after)
        fins = [_rs_add_chips(self.scal, self.p32[a], bufs[n + a], hrs[a]) for a in range(n)]

        def plan(refs):
            x, y, c = _place()
            return [(refs[a].at[c], refs[a].at[c], refs[a].at[1 - c], (x, y, 1 - c)) for a in range(n)]

        self.ex = SplitExchange("rs_share_halves", fins, plan, n)
        return self.ex.start(fins[-1])

    def share_wait(self, after):
        fulls = self.ex.wait(after)
        return [f.reshape(2 * hr, f.shape[2]) for f, hr in zip(fulls, self.hrs)]


def reduce_scatter_grads(gs):
    n = len(gs)
    x, y, c = _place()
    scal = jnp.stack([2 * x + y, c]).astype(jnp.int32)
    hrs = [g.shape[1] // 2 for g in gs]

    def swap_body(*refs):
        ins, outs = refs[:n], refs[n:2 * n]
        send, recv = refs[2 * n:]
        xx, yy, cc = _place()
        cps = []
        for a in range(n):
            cp = pltpu.make_async_remote_copy(src_ref=ins[a].at[:, pl.ds((1 - cc) * hrs[a], hrs[a])], dst_ref=outs[a],
                                              send_sem=send.at[a], recv_sem=recv.at[a],
                                              device_id=(xx, yy, 1 - cc), device_id_type=MESH)
            cp.start()
            cps.append(cp)
        for cp in cps:
            cp.wait()

    ras = _comm_call(swap_body, "rs_swap_halves", n, [_sds((N_SH, hrs[a], gs[a].shape[2]), F32) for a in range(n)],
                     [pltpu.SemaphoreType.DMA((n,)), pltpu.SemaphoreType.DMA((n,))])(*gs)

    parts = [_rs_add_sibling(scal, gs[a], ras[a], hrs[a]) for a in range(n)]

    def ici_body(*refs):
        ins, outs = refs[:n], refs[n:2 * n]
        send, recv = refs[2 * n:]
        xx, yy, cc = _place()
        chips = [(1 - xx, yy), (xx, 1 - yy), (1 - xx, 1 - yy)]
        cps = []
        for a in range(n):
            for j, chip in enumerate(chips):
                cp = pltpu.make_async_remote_copy(src_ref=ins[a].at[2 * chip[0] + chip[1]], dst_ref=outs[a].at[j],
                                                  send_sem=send.at[3 * a + j], recv_sem=recv.at[3 * a + j],
                                                  device_id=(chip[0], chip[1], cc), device_id_type=MESH)
                cp.start()
                cps.append(cp)
        for cp in cps:
            cp.wait()

    rbs = _comm_call(ici_body, "rs_send_partials", n, [_sds((3, hrs[a], gs[a].shape[2]), BF16) for a in range(n)],
                     [pltpu.SemaphoreType.DMA((3 * n,)), pltpu.SemaphoreType.DMA((3 * n,))])(*[p[1] for p in parts])

    fins = [_rs_add_chips(scal, parts[a][0], rbs[a], hrs[a]) for a in range(n)]

    def share_body(*refs):
        ins, outs = refs[:n], refs[n:2 * n]
        send, recv = refs[2 * n:]
        xx, yy, cc = _place()
        sib = (xx, yy, 1 - cc)
        cps = []
        for a in range(n):
            cp = pltpu.make_async_remote_copy(src_ref=ins[a].at[cc], dst_ref=outs[a].at[cc], send_sem=send.at[a],
                                              recv_sem=recv.at[a], device_id=sib, device_id_type=MESH)
            cp.start()
            cps.append(cp)
        for a in range(n):
            pltpu.make_async_remote_copy(src_ref=ins[a].at[cc], dst_ref=outs[a].at[1 - cc], send_sem=send.at[a],
                                         recv_sem=recv.at[a], device_id=sib, device_id_type=MESH).wait_recv()
        for cp in cps:
            cp.wait_send()

    fulls = _comm_call(share_body, "rs_share_halves", n, [_sds(f.shape, F32) for f in fins],
                       [pltpu.SemaphoreType.DMA((n,)), pltpu.SemaphoreType.DMA((n,))],
                       aliases={a: a for a in range(n)})(*fins)
    return [f.reshape(2 * hr, f.shape[2]) for f, hr in zip(fulls, hrs)]


def adamw(w, g, m, v):
    shape = w.shape
    cols = shape[-1]
    rows = math.prod(shape[:-1]) if len(shape) > 1 else 1
    tr = 256 if rows % 256 == 0 and rows > 256 else rows
    c1 = 1.0 - ADAM_B1 ** ADAM_STEP
    c2 = 1.0 - ADAM_B2 ** ADAM_STEP

    def body(w_ref, g_ref, m_ref, v_ref, d_ref, nm_ref, nv_ref):
        gv = g_ref[...]
        nm = ADAM_B1 * m_ref[...] + (1.0 - ADAM_B1) * gv
        nv = ADAM_B2 * v_ref[...] + (1.0 - ADAM_B2) * (gv * gv)
        nm_ref[...] = nm
        nv_ref[...] = nv
        d_ref[...] = -ADAM_LR * ((nm / c1) / (jnp.sqrt(nv / c2) + ADAM_EPS) + ADAM_WD * w_ref[...])

    row = bs((tr, cols), lambda i: (i, 0))
    outs = _call(body, "adamw", (rows // tr,), [row] * 4, [row] * 3, [_sds((rows, cols), F32)] * 3)(
        *[a.reshape(rows, cols) for a in (w, g, m, v)])
    return [o.reshape(shape) for o in outs]


def adamw_layers(w, gs, m, v, lo, into=None, after=None):
    shape = w.shape
    cols = shape[-1]
    rl = math.prod(shape[1:-1])
    tr = max(t_ for t_ in range(8, rl + 1, 8) if rl % t_ == 0 and t_ * cols * 4 <= 1024 * 1024)
    nb = rl // tr
    n = len(gs)
    c1 = 1.0 - ADAM_B1 ** ADAM_STEP
    c2 = 1.0 - ADAM_B2 ** ADAM_STEP

    def body(*refs):
        w_ref, m_ref, v_ref = refs[:3]
        g_refs = refs[3:3 + n]
        d_ref, nm_ref, nv_ref, go_ref = refs[-4:]
        layer = pl.program_id(0) // nb
        for k in range(n):
            @pl.when(layer == k)
            def _(k=k):
                gv = g_refs[k][...]
                nm = ADAM_B1 * m_ref[...] + (1.0 - ADAM_B1) * gv
                nv = ADAM_B2 * v_ref[...] + (1.0 - ADAM_B2) * (gv * gv)
                nm_ref[...] = nm
                nv_ref[...] = nv
                go_ref[...] = gv
                d_ref[...] = -ADAM_LR * ((nm / c1) / (jnp.sqrt(nv / c2) + ADAM_EPS) + ADAM_WD * w_ref[...])

    row = bs((tr, cols), lambda b: (lo * nb + b, 0))
    g_specs = [bs((tr, cols), functools.partial(lambda b, k: (jnp.clip(b - k * nb, 0, nb - 1), 0), k=k)) for k in range(n)]
    flat = lambda a: a.reshape(-1, cols)
    in_specs = [row] * 3 + g_specs
    args = [flat(w), flat(m), flat(v)] + [flat(g) for g in gs]
    aliases = None
    if into is not None:
        aliases = {len(in_specs) + k: k for k in range(4)}
        in_specs = in_specs + [ANY] * 4
        args = args + [flat(a) for a in into]
    outs = _call(body, "adamw_layers", (n * nb,), in_specs, [row] * 4, [_sds((shape[0] * rl, cols), F32)] * 4,
                 aliases=aliases, after=after)(*args)
    return [o.reshape(shape) for o in outs]


def allreduce8_split(vec):
    rows, cols = vec.shape

    def plan(refs):
        x, y, c = _place()
        me = 4 * x + 2 * y + c
        out = []
        for j in range(1, 8):
            px, py, pc = (1 - x if j & 4 else x), (1 - y if j & 2 else y), (1 - c if j & 1 else c)
            out.append((refs[0], refs[1].at[me], refs[1].at[4 * px + 2 * py + pc], (px, py, pc)))
        return out

    ex = SplitExchange("allreduce8", [vec, lax.empty((8, rows, cols), F32)], plan, 7)

    def finish(after):
        v, land = ex.wait(after)
        x, y, c = _place()
        me = jnp.reshape(4 * x + 2 * y + c, (1,)).astype(jnp.int32)

        def body(me_ref, v_ref, l_ref, o_ref):
            o_ref[...] = jnp.zeros_like(o_ref)
            for k in range(8):
                @pl.when(me_ref[0] == k)
                def _():
                    o_ref[...] += v_ref[...]

                @pl.when(me_ref[0] != k)
                def _(k=k):
                    o_ref[...] += l_ref[k]

        spec = pltpu.PrefetchScalarGridSpec(
            num_scalar_prefetch=1, grid=(1,),
            in_specs=[bs((rows, cols), lambda i, mr: (0, 0)), bs((8, rows, cols), lambda i, mr: (0, 0, 0))],
            out_specs=bs((rows, cols), lambda i, mr: (0, 0)))
        return pl.pallas_call(body, name="allreduce8_sum", grid_spec=spec, out_shape=_sds((rows, cols), F32),
                              compiler_params=pltpu.CompilerParams(dimension_semantics=("arbitrary",),
                                                                   vmem_limit_bytes=VMEM_LIMIT))(me, v, land)

    return ex, finish


class Hooks:
    def __init__(self):
        self.steps = {}

    def add(self, point, fn):
        self.steps.setdefault(point, []).append(fn)

    def run(self, point, arr, env=None):
        tok = None
        for fn in self.steps.get(point, ()):
            got = fn(arr if tok is None else tok, env)
            tok = tok if got is None else got
        return tok


def layer_fwd(x, p_i, w, hooks):
    h, proj, *qkv = norm_in_proj(x, w["g_mix"], w["win"], after=hooks.run("start", x))
    ya = conva_fwd(proj, w["conv_a"])
    yb, o32, lse = attn_merge([attn_fwd_group(pv, d) for pv, d in zip(qkv, DILATIONS)])
    yc = sgu_fwd(proj, w["sgu_ln_g"], w["sgu_ln_b"], w["sgu_wt"], w["sgu_bf"])
    yd, z = conf_fwd(proj, w["conf_dw"], w["conf_ln_g"], w["conf_ln_b"], after=hooks.run("pre_conf", [ya, yb, yc]))
    ys = (ya, yb, yc, yd)
    tok = hooks.run("pre_merge", yd)
    merged, gates, ybr = merge_fwd(h, ys, w["wg"], w["wbr"], after=tok)
    x1 = mm_residual(merged, w["wout"], x, "attn_out", after=hooks.run("post_merge", merged))
    h2, fgu, act = ffn_in(x1, w["g_ffn"], w["wfi"])
    x2 = mm_residual(act, w["wfo"], x1, "ffn_out", after=hooks.run("post_ffn_in", act))
    h3, gate, pp, x3 = ple_fwd(x2, w["g_ple"], w["wpg"], p_i, w["wpp"])
    saved = dict(x=x, h=h, proj=proj, qkv=qkv, ys=ys, o32=o32, lse=lse, z=z, merged=merged, gates=gates, ybr=ybr, x1=x1,
                 h2=h2, fgu=fgu, act=act, x2=x2, h3=h3, gate=gate, pp=pp)
    return x3, saved


def layer_bwd(dx3, p_i, w, s, hooks):
    t = dx3.shape[0]
    tr = min(1024, t)
    nr = t // tr
    ns_fi = FFN_H // 2
    small = {}

    dpre, dpp, dx2, small["g_ple"] = ple_bwd(dx3, s["gate"], s["pp"], w["wpg"], s["x2"], w["g_ple"],
                                             after=hooks.run("start", dx3))
    ga_shape, gb_shape = _sds((N_SH, 6 * BW, D_MODEL), F32), _sds((N_SH, 5 * BW, BW), F32)
    ga_blk = lambda idx: bs((N_SH, BW, D_MODEL), idx)
    ga = tn_matmul("dw_ple_gate", s["h3"], dpre, (nr,), bs((tr, D_MODEL), lambda r: (r, 0)),
                   bs((tr, D_MODEL), lambda r: (r, 0)), ga_blk(lambda r: (0, 5, 0)), ga_shape, split=N_SH)
    gb = tn_matmul("dw_ple_proj", p_i, dpp, (N_SH, nr), bs((tr, BW), lambda j, r: (r, 0)),
                   bs((tr, BW), lambda j, r: (r, j)), bs((None, BW, BW), lambda j, r: (j, 4, 0)), gb_shape)

    df, dx1, small["g_ffn"] = ffn_bwd(dx2, w["wfo"], s["fgu"], w["wfi"], s["x1"], w["g_ffn"],
                                      after=hooks.run("pre_ffn", dx2))
    gfo = tn_matmul("dw_ffn_out", s["act"], dx2, (2, nr), bs((tr, ns_fi), lambda j, r: (r, j)),
                    bs((tr, D_MODEL), lambda j, r: (r, 0)), bs((2, FFN_H // N_SH, D_MODEL), lambda j, r: (j, 0, 0)),
                    _sds((N_SH, FFN_H // N_SH, D_MODEL), F32), split=2)
    gfi = tn_matmul("dw_ffn_in", s["h2"], df, (N_SH, nr), bs((tr, D_MODEL), lambda j, r: (r, 0)),
                    bs((None, tr, ns_fi), lambda j, r: (j // 2, r, j % 2)),
                    bs((None, D_MODEL, ns_fi), lambda j, r: (j, 0, 0)), _sds((N_SH, D_MODEL, ns_fi), F32))

    dpre_m, dyb, dys = merge_bwd(dx1, w["wout"], s["gates"], s["ybr"], w["wbr"])
    ga = tn_matmul("dw_out", s["merged"], dx1, (nr,), bs((tr, D_MODEL), lambda r: (r, 0)),
                   bs((tr, D_MODEL), lambda r: (r, 0)), ga_blk(lambda r: (0, 4, 0)), ga_shape, split=N_SH, into=ga,
                   after=hooks.run("pre_dw_out", dyb, dict(gfo=gfo, gfi=gfi)))
    ga = tn_matmul("dw_merge_gate", s["h"], dpre_m, (N_BR, nr), bs((tr, D_MODEL), lambda k, r: (r, 0)),
                   bs((None, tr, D_MODEL), lambda k, r: (k, r, 0)), ga_blk(lambda k, r: (0, k, 0)), ga_shape,
                   split=N_SH, into=ga)
    for k in range(N_BR):
        gb = tn_matmul("dw_branch", s["ys"][k], dyb, (nr,), bs((tr, BW), lambda r: (r, 0)),
                       bs((None, tr, D_MODEL), functools.partial(lambda r, kk: (kk, r, 0), kk=k)),
                       bs((N_SH, BW, BW), functools.partial(lambda r, kk: (0, kk, 0), kk=k)), gb_shape,
                       split_cols=N_SH, into=gb)

    (dab, dac, dax), small["conv_a"] = conva_bwd(s["proj"], dys, w["conv_a"], after=hooks.run("pre_conva", gb))
    lds, dy_views = attn_delta(dys, s["o32"], s["lse"])
    dy_views = [dys[1]] + list(dy_views)
    dqkv = attn_bwd_finish([attn_bwd_group(pv, dov, ldv, d)
                            for pv, dov, ldv, d in zip(s["qkv"], dy_views, lds, DILATIONS)])
    du, dv, d_sw, d_sbf, small["sgu_ln_g"], small["sgu_ln_b"] = sgu_bwd(
        s["proj"], dys, w["sgu_ln_g"], w["sgu_ln_b"], w["sgu_wt"], w["sgu_bf"])
    small["sgu_w"] = jnp.where(jnp.tril(jnp.ones((BLK, BLK), bool))[None], d_sw, 0.0)
    small["sgu_b"] = jnp.sum(d_sbf.reshape(BLK, 4, HEAD_D), axis=-1).T
    dz, small["conf_ln_g"], small["conf_ln_b"] = conf_bwd_ln(s["z"], dys, w["conf_ln_g"], w["conf_ln_b"])
    dval, dgate, small["conf_dw"] = conf_bwd_conv(s["proj"], dz, w["conf_dw"])
    dproj = jnp.concatenate([dab, dac, dax, dqkv, du, dv, dval, dgate], axis=1)

    ns_in = N_IN // N_SH
    gin = tn_matmul("dw_in", s["h"], dproj, (N_SH, nr), bs((tr, D_MODEL), lambda j, r: (r, 0)),
                    bs((tr, ns_in), lambda j, r: (r, j)), bs((None, D_MODEL, ns_in), lambda j, r: (j, 0, 0)),
                    _sds((N_SH, D_MODEL, ns_in), F32), after=hooks.run("pre_dw_in", dproj))
    hooks.run("end", gin)
    big = [ga, gfo, gb, gin, gfi]
    dx, small["g_mix"] = norm_bwd(
        "mix_norm_bwd",
        [(dpre_m, lambda tm: bs((N_BR, tm, D_MODEL), lambda i: (0, i, 0)), w["wg"],
          lambda a, wr: [(a[k], wr[k]) for k in range(N_BR)]),
         (dproj, lambda tm: bs((tm, N_IN), lambda i: (i, 0)), w["win"],
          lambda a, wr: [(a[:, k * ns_in:(k + 1) * ns_in], wr[k]) for k in range(N_SH)])],
        dx1, s["x"], w["g_mix"])
    return dx, big, small


BIG_NAMES = ("w_in", "w_branch", "w_merge_gate", "w_out", "w_ffn_in", "w_ffn_out", "w_ple_gate", "w_ple_proj")


def unpack_big_grads(ga, gfo, gb, gin, gfi):
    return dict(w_in=gin, w_ffn_in=gfi, w_ffn_out=gfo,
                w_merge_gate=ga[:N_BR * BW].reshape(N_BR, BW, D_MODEL), w_out=ga[N_BR * BW:5 * BW], w_ple_gate=ga[5 * BW:],
                w_branch=gb[:N_BR * BW].reshape(N_BR, BW, BW), w_ple_proj=gb[N_BR * BW:])


SMALL_NAMES = ("g_mix", "conv_a", "sgu_ln_g", "sgu_ln_b", "sgu_w", "sgu_b", "conf_dw", "conf_ln_g", "conf_ln_b",
               "g_ffn", "g_ple")


def _pack_rows(arrays, rows):
    flat = jnp.concatenate([a.reshape(-1) for a in arrays])
    return jnp.pad(flat, (0, rows * D_MODEL - flat.shape[0])).reshape(rows, D_MODEL)


def _unpack_rows(packed, shapes):
    flat, out, pos = packed.reshape(-1), [], 0
    for shape in shapes:
        n = math.prod(shape)
        out.append(flat[pos:pos + n].reshape(shape))
        pos += n
    return out


def kernel(x, p, g_mix, w_in, conv_a, sgu_ln_g, sgu_ln_b, sgu_w, sgu_b, conf_dw, conf_ln_g, conf_ln_b, w_branch, w_merge_gate, w_out, g_ffn, w_ffn_in, w_ffn_out, g_ple, w_ple_gate, w_ple_proj, g_final, loss_target, m_g_mix, m_w_in, m_conv_a, m_sgu_ln_g, m_sgu_ln_b, m_sgu_w, m_sgu_b, m_conf_dw, m_conf_ln_g, m_conf_ln_b, m_w_branch, m_w_merge_gate, m_w_out, m_g_ffn, m_w_ffn_in, m_w_ffn_out, m_g_ple, m_w_ple_gate, m_w_ple_proj, m_g_final, v_g_mix, v_w_in, v_conv_a, v_sgu_ln_g, v_sgu_ln_b, v_sgu_w, v_sgu_b, v_conf_dw, v_conf_ln_g, v_conf_ln_b, v_w_branch, v_w_merge_gate, v_w_out, v_g_ffn, v_w_ffn_in, v_w_ffn_out, v_g_ple, v_w_ple_gate, v_w_ple_proj, v_g_final):
    weights = dict(g_mix=g_mix, w_in=w_in, conv_a=conv_a, sgu_ln_g=sgu_ln_g, sgu_ln_b=sgu_ln_b, sgu_w=sgu_w, sgu_b=sgu_b,
                   conf_dw=conf_dw, conf_ln_g=conf_ln_g, conf_ln_b=conf_ln_b, w_branch=w_branch, w_merge_gate=w_merge_gate,
                   w_out=w_out, g_ffn=g_ffn, w_ffn_in=w_ffn_in, w_ffn_out=w_ffn_out, g_ple=g_ple, w_ple_gate=w_ple_gate,
                   w_ple_proj=w_ple_proj, g_final=g_final)
    m_in = dict(g_mix=m_g_mix, w_in=m_w_in, conv_a=m_conv_a, sgu_ln_g=m_sgu_ln_g, sgu_ln_b=m_sgu_ln_b, sgu_w=m_sgu_w,
                sgu_b=m_sgu_b, conf_dw=m_conf_dw, conf_ln_g=m_conf_ln_g, conf_ln_b=m_conf_ln_b, w_branch=m_w_branch,
                w_merge_gate=m_w_merge_gate, w_out=m_w_out, g_ffn=m_g_ffn, w_ffn_in=m_w_ffn_in, w_ffn_out=m_w_ffn_out,
                g_ple=m_g_ple, w_ple_gate=m_w_ple_gate, w_ple_proj=m_w_ple_proj, g_final=m_g_final)
    v_in = dict(g_mix=v_g_mix, w_in=v_w_in, conv_a=v_conv_a, sgu_ln_g=v_sgu_ln_g, sgu_ln_b=v_sgu_ln_b, sgu_w=v_sgu_w,
                sgu_b=v_sgu_b, conf_dw=v_conf_dw, conf_ln_g=v_conf_ln_g, conf_ln_b=v_conf_ln_b, w_branch=v_w_branch,
                w_merge_gate=v_w_merge_gate, w_out=v_w_out, g_ffn=v_g_ffn, w_ffn_in=v_w_ffn_in, w_ffn_out=v_w_ffn_out,
                g_ple=v_g_ple, w_ple_gate=v_w_ple_gate, w_ple_proj=v_w_ple_proj, g_final=v_g_final)
    order = ("g_mix", "w_in", "conv_a", "sgu_ln_g", "sgu_ln_b", "sgu_w", "sgu_b", "conf_dw", "conf_ln_g", "conf_ln_b",
             "w_branch", "w_merge_gate", "w_out", "g_ffn", "w_ffn_in", "w_ffn_out", "g_ple", "w_ple_gate", "w_ple_proj",
             "g_final")
    depth = g_mix.shape[0]
    xs, tgt = x[0], loss_target[0]
    cw = BW // N_SH
    my_shard = 2 * lax.axis_index("x") + lax.axis_index("y")

    conv_rows = 16
    allc = gather8(_pack_rows([conv_a, conf_dw], conv_rows), reduce=False)
    shards = [_unpack_rows(allc[2 * s], [conv_a.shape, conf_dw.shape]) for s in range(N_SH)]
    conv_a_full = jnp.concatenate([sh[0] for sh in shards], axis=-1)
    conf_dw_full = jnp.concatenate([sh[1] for sh in shards], axis=-1)

    tril = jnp.tril(jnp.ones((BLK, BLK), bool))
    def placed_shards(i):
        shards = ([w_in[i], w_branch[i]] + [w_merge_gate[i, k] for k in range(N_BR)]
                  + [w_out[i], w_ffn_in[i], w_ffn_out[i], w_ple_gate[i], w_ple_proj[i]])
        return [lax.dynamic_update_slice(lax.empty((N_SH,) + sh.shape, BF16), sh.astype(BF16)[None],
                                         (my_shard,) + (0,) * sh.ndim) for sh in shards]

    def small_weights(i, win):
        vec = lambda a: a[i].reshape(1, -1)
        return dict(
            win=win, g_mix=vec(g_mix), g_ffn=vec(g_ffn), g_ple=vec(g_ple), conv_a=conv_a_full[i], conf_dw=conf_dw_full[i],
            sgu_ln_g=vec(sgu_ln_g), sgu_ln_b=vec(sgu_ln_b), conf_ln_g=vec(conf_ln_g), conf_ln_b=vec(conf_ln_b),
            sgu_wt=jnp.where(tril[None], sgu_w[i], 0.0).astype(BF16),
            sgu_bf=jnp.repeat(sgu_b[i].T, HEAD_D, axis=1))

    def late_weights(got):
        return dict(wbr=got[0], wg=jnp.stack([g.reshape(D_MODEL, D_MODEL) for g in got[1:5]]),
                    wout=got[5].reshape(D_MODEL, D_MODEL), wfi=got[6], wfo=got[7].reshape(FFN_H, D_MODEL),
                    wpg=got[8].reshape(D_MODEL, D_MODEL), wpp=got[9])

    class SplitAllGather:
        def __init__(self, bufs):
            self.shapes = [b.shape for b in bufs]
            self.ici = SplitExchange("allgather_ici", bufs, allgather_ici_plan(self.shapes), 3 * len(bufs))

        def ici_start(self, after, env=None):
            return self.ici.start(after)

        def ici_wait_d2d_start(self, after, env=None):
            landed = self.ici.wait(after)
            self.d2d = SplitExchange("allgather_d2d", landed, allgather_d2d_plan(self.shapes), 3 * len(landed))
            return self.d2d.start(landed[-1])

        def d2d_wait(self, after, env=None):
            self.got = self.d2d.wait(after)
            return None

    bufs0 = placed_shards(0)
    first = SplitAllGather(bufs0[:1])
    first.d2d_wait(first.ici_wait_d2d_start(first.ici_start(xs)))
    rest = SplitAllGather(bufs0[1:])
    layers = [small_weights(0, first.got[0])]
    act, saved = xs, []
    nxt_done = None
    for i in range(depth):
        hooks = Hooks()
        if i == 0:
            hooks.add("start", rest.ici_start)
            hooks.add("pre_conf", rest.ici_wait_d2d_start)
            hooks.add("pre_merge", rest.d2d_wait)
            hooks.add("pre_merge", lambda after, env: layers[0].update(late_weights(rest.got)))
        if i + 1 < depth:
            nxt = SplitAllGather(placed_shards(i + 1))
            points = ("pre_merge", "post_ffn_in", None) if i == 0 else ("start", "post_merge", "post_ffn_in")
            hooks.add(points[0], nxt.ici_start)
            hooks.add(points[1], nxt.ici_wait_d2d_start)
            if points[2]:
                hooks.add(points[2], nxt.d2d_wait)
        act, sv = layer_fwd(act, p[i, 0], layers[i], hooks)
        saved.append(sv)
        if i + 1 < depth:
            if i == 0:
                nxt.d2d_wait(act)
            layers.append({**small_weights(i + 1, nxt.got[0]), **late_weights(nxt.got[1:])})
    loss_part, dx, dg_final = loss_head(act, g_final.reshape(1, -1), tgt)

    big_red = [None] * depth
    small_red = [None] * depth
    small_rows = 80
    pending = None

    def small_vector(i, small):
        parts = [small[n] for n in SMALL_NAMES]
        return _pack_rows(parts + ([dg_final, loss_part[0, :1]] if i == 0 else []), small_rows)

    for i in reversed(range(depth)):
        hooks = Hooks()
        result = {}
        if pending is not None:
            rs, j, (small_ex, small_finish) = pending
            hooks.add("start", lambda after, env, ex=small_ex: ex.start(after))
            hooks.add("start", lambda after, env, rs=rs: rs.swap_start(after))
            hooks.add("pre_ffn", lambda after, env, rs=rs: rs.swap_wait_send_start(after))
            hooks.add("pre_dw_out", lambda after, env, rs=rs: rs.send_wait_share_start(after))
            hooks.add("pre_conva", lambda after, env, rs=rs, result=result: result.update(prev=rs.share_wait(after)))
            hooks.add("pre_conva", lambda after, env, fin=small_finish, result=result: result.update(small=fin(after)))
        if i == 0:
            def early_start(after, env, result=result):
                result["rs"] = SplitReduceScatter([env["gfo"], env["gfi"]])
                return result["rs"].swap_start(after)

            hooks.add("pre_dw_out", early_start)
            hooks.add("pre_conva", lambda after, env, result=result: result["rs"].swap_wait_send_start(after))
            hooks.add("pre_dw_in", lambda after, env, result=result: result["rs"].send_wait_share_start(after))
            hooks.add("end", lambda after, env, result=result: result.update(early=result["rs"].share_wait(after)))
        dx, big, small = layer_bwd(dx, p[i, 0], layers[i], saved[i], hooks)
        if pending is not None:
            big_red[pending[1]] = unpack_big_grads(*result["prev"])
            small_red[pending[1]] = result["small"]
        if i > 0:
            pending = (SplitReduceScatter(big), i, allreduce8_split(small_vector(i, small)))

    ga, _, gb, gin, _ = big
    late = SplitReduceScatter([ga, gb, gin])
    small_ex, small_finish = allreduce8_split(small_vector(0, small))
    upd = {}

    def update_upper(names, after):
        for name in names:
            upd[name] = adamw_layers(weights[name], [big_red[i][name] for i in range(1, depth)], m_in[name], v_in[name],
                                     1, after=after)
        return [upd[name][0] for name in names]

    done = update_upper(("w_in",), late.swap_start(small_ex.start(dx)))
    done = update_upper(("w_ffn_in", "w_merge_gate", "w_ffn_out"), late.swap_wait_send_start(done))
    small_red[0] = small_finish(done)
    done = update_upper(("w_branch", "w_out", "w_ple_gate", "w_ple_proj"), late.send_wait_share_start(done))
    ga, gb, gin = late.share_wait(done)
    gfo, gfi = result["early"]
    big_red[0] = unpack_big_grads(ga, gfo, gb, gin, gfi)

    layer_shapes = [small[n].shape for n in SMALL_NAMES]
    per_layer = [_unpack_rows(small_red[i], layer_shapes + ([dg_final.shape, (1,)] if i == 0 else []))
                 for i in range(depth)]
    grads = {n: jnp.stack([per_layer[i][k].reshape(weights[n].shape[1:] if n not in ("conv_a", "conf_dw")
                                                   else per_layer[i][k].shape) for i in range(depth)])
             for k, n in enumerate(SMALL_NAMES)}
    grads["g_final"] = per_layer[0][-2].reshape(-1)
    loss = per_layer[0][-1].reshape(())
    for n in ("conv_a", "conf_dw"):
        grads[n] = lax.dynamic_slice_in_dim(grads[n], my_shard * cw, cw, axis=2)

    small_all = [n for n in order if n not in BIG_NAMES]
    sm_shapes = [weights[n].shape for n in small_all]
    n_sm = sum(math.prod(sh) for sh in sm_shapes)
    sm_rows = -(-n_sm // (8 * D_MODEL)) * 8
    packed = [_pack_rows([src[n] for n in small_all], sm_rows) for src in (weights, grads, m_in, v_in)]
    sm_out = [_unpack_rows(o, sm_shapes) for o in adamw(*packed)]
    delta, new_m, new_v = ({n: o[k] for k, n in enumerate(small_all)} for o in sm_out)
    for name in BIG_NAMES:
        delta[name], new_m[name], new_v[name], grads[name] = adamw_layers(
            weights[name], [big_red[0][name]], m_in[name], v_in[name], 0, into=upd[name])

    return (loss, dx[None], *[grads[n] for n in order], *[delta[n] for n in order], *[new_m[n] for n in order],
            *[new_v[n] for n in order])
```

```python
import functools
import math

import jax
import jax.numpy as jnp
from jax import lax
from jax.experimental import pallas as pl
from jax.experimental.pallas import tpu as pltpu

F32 = jnp.float32
BF16 = jnp.bfloat16
EPS = 1e-6
D_MODEL = 1024
BW = 256
N_BR = 4
N_IN = 10 * BW
FFN_H = 2816
N_SH = 4
HEADS = 4
HEAD_D = 64
BLK = 128
DILATIONS = (1, 4, 16)
CONF_K = 31
CONVA_K = 3
NEG = -1e30
VMEM_LIMIT = 56 * 1024 * 1024
MESH = pl.DeviceIdType.MESH

ADAM_LR, ADAM_B1, ADAM_B2, ADAM_EPS, ADAM_WD, ADAM_STEP = 0.001, 0.9, 0.999, 1e-08, 0.01, 10

bs = pl.BlockSpec
ANY = pl.BlockSpec(memory_space=pl.ANY)


def _call(body, name, grid, in_specs, out_specs, out_shape, scratch=(), aliases=None, after=None):
    n_in = len(in_specs)
    kernel_body = body
    if after is not None:
        in_specs = list(in_specs) + [ANY]

        def kernel_body(*refs):
            return body(*refs[:n_in], *refs[n_in + 1:])

    call = pl.pallas_call(
        kernel_body, name=name, grid=grid, in_specs=in_specs, out_specs=out_specs, out_shape=out_shape,
        scratch_shapes=list(scratch), input_output_aliases=aliases or {},
        compiler_params=pltpu.CompilerParams(dimension_semantics=("arbitrary",) * len(grid),
                                             vmem_limit_bytes=VMEM_LIMIT))
    return call if after is None else (lambda *args: call(*args, after))


def _sds(shape, dtype):
    return jax.ShapeDtypeStruct(shape, dtype)


def _nn(a, b):
    return jnp.dot(a, b, preferred_element_type=F32)


def _nt(a, b):
    return lax.dot_general(a, b, (((1,), (1,)), ((), ())), preferred_element_type=F32)


def _tn(a, b):
    return lax.dot_general(a, b, (((0,), (0,)), ((), ())), preferred_element_type=F32)


def _sigmoid(x):
    return 1.0 / (1.0 + jnp.exp(-x))


def _rms_fwd(x, g):
    r = lax.rsqrt(jnp.mean(x * x, axis=-1, keepdims=True) + EPS)
    return x * r * g


def _rms_bwd(dh, x, g):
    r = lax.rsqrt(jnp.mean(x * x, axis=-1, keepdims=True) + EPS)
    xr = x * r
    dxr = dh * g
    dx = r * (dxr - xr * jnp.mean(dxr * xr, axis=-1, keepdims=True))
    return dx, dh * xr


def _ln_hat(x):
    mu = jnp.mean(x, axis=-1, keepdims=True)
    xc = x - mu
    r = lax.rsqrt(jnp.mean(xc * xc, axis=-1, keepdims=True) + EPS)
    return xc * r, r


def _ln_bwd(dy, xhat, r, g):
    dxh = dy * g
    return r * (dxh - jnp.mean(dxh, axis=-1, keepdims=True) - xhat * jnp.mean(dxh * xhat, axis=-1, keepdims=True))


def _colsum(v):
    return jnp.sum(v, axis=0, keepdims=True)


def _causal_conv(zext, w_ref, k_taps, halo):
    acc = zext[halo:] * w_ref[k_taps - 1:k_taps, :]
    for k in range(k_taps - 1):
        acc = acc + pltpu.roll(zext, k_taps - 1 - k, 0)[halo:] * w_ref[k:k + 1, :]
    return acc


def _anti_conv(dext, w_ref, k_taps, tm):
    n = dext.shape[0]
    acc = dext[:tm] * w_ref[k_taps - 1:k_taps, :]
    for s in range(1, k_taps):
        acc = acc + pltpu.roll(dext, n - s, 0)[:tm] * w_ref[k_taps - 1 - s:k_taps - s, :]
    return acc


def _conv_wgrad(dw_ref, dc, zext, k_taps, halo):
    dw_ref[k_taps - 1:k_taps, :] += _colsum(dc * zext[halo:])
    for k in range(k_taps - 1):
        dw_ref[k:k + 1, :] += _colsum(dc * pltpu.roll(zext, k_taps - 1 - k, 0)[halo:])


LANES = 128


def _to_strided_view(dst_ref, chunk, scr, d, width):
    n = scr.shape[0] // d
    for c in range(width // LANES):
        scr[...] = chunk(c)
        for r in range(d):
            dst_ref[:, r * width + c * LANES:r * width + (c + 1) * LANES] = scr[pl.ds(r, n, stride=d), :].astype(dst_ref.dtype)


def _from_strided_view(src_ref, scr, d, width, c):
    n = scr.shape[0] // d
    for r in range(d):
        scr[pl.ds(r, n, stride=d), :] = src_ref[:, r * width + c * LANES:r * width + (c + 1) * LANES].astype(F32)
    return scr[...]


def _view_spec(tm, d, width):
    return bs((tm // d, d * width), lambda i: (i, 0))


def _prev_blk(i, per):
    return jnp.maximum(i * per - 1, 0)


def _next_blk(i, per, last):
    return jnp.minimum((i + 1) * per, last)


def norm_in_proj(x, g, win, after=None):
    t = x.shape[0]
    tm = min(512, t)
    ns = win.shape[2]

    def body(x_ref, g_ref, w_ref, h_ref, o_ref, q_ref, q4_ref, q16_ref, scr):
        h = _rms_fwd(x_ref[...], g_ref[...]).astype(BF16)
        h_ref[...] = h
        parts = []
        for s in range(N_SH):
            r = _nn(h, w_ref[s])
            o_ref[:, s * ns:(s + 1) * ns] = r
            if s == 1:
                parts.append(r[:, 3 * BW - ns:])
            if s == 2:
                parts.append(r[:, :6 * BW - 2 * ns])
        qf = jnp.concatenate(parts, axis=1)
        q_ref[...] = qf.astype(BF16)
        chunk = lambda c: qf[:, c * LANES:(c + 1) * LANES]
        _to_strided_view(q4_ref, chunk, scr, 4, 3 * BW)
        _to_strided_view(q16_ref, chunk, scr, 16, 3 * BW)

    row = lambda c: bs((tm, c), lambda i: (i, 0))
    return _call(
        body, "norm_in_proj", (t // tm,), [row(D_MODEL), bs((1, D_MODEL), lambda i: (0, 0)), _resident(win)],
        [row(D_MODEL), row(N_IN), row(3 * BW), _view_spec(tm, 4, 3 * BW), _view_spec(tm, 16, 3 * BW)],
        [_sds((t, D_MODEL), BF16), _sds((t, N_IN), F32), _sds((t, 3 * BW), BF16),
         _sds((t // 4, 4 * 3 * BW), BF16), _sds((t // 16, 16 * 3 * BW), BF16)],
        scratch=[pltpu.VMEM((tm, LANES), F32)], after=after)(x, g, win)


def merge_fwd(h, ys, wg, wbr, after=None):
    t = h.shape[0]
    tm = min(256, t)

    def body(h_ref, ya, yb, yc, yd, wg_ref, wb_ref, m_ref, g_ref, b_ref):
        hh = h_ref[...]
        for j in range(N_SH):
            cs = slice(j * BW, (j + 1) * BW)
            acc = None
            for k, y_ref in enumerate((ya, yb, yc, yd)):
                g = _sigmoid(_nn(hh, wg_ref[k, :, cs]))
                b = _nn(y_ref[...], wb_ref[j, k])
                g_ref[k, :, cs] = g.astype(BF16)
                b_ref[k, :, cs] = b.astype(BF16)
                acc = g * b if acc is None else acc + g * b
            m_ref[:, cs] = acc.astype(BF16)

    ysp = bs((tm, BW), lambda i: (i, 0))
    big = bs((N_BR, tm, D_MODEL), lambda i: (0, i, 0))
    return _call(
        body, "merge_fwd", (t // tm,),
        [bs((tm, D_MODEL), lambda i: (i, 0)), ysp, ysp, ysp, ysp, _resident(wg), _resident(wbr)],
        [bs((tm, D_MODEL), lambda i: (i, 0)), big, big],
        [_sds((t, D_MODEL), BF16), _sds((N_BR, t, D_MODEL), BF16), _sds((N_BR, t, D_MODEL), BF16)], after=after)(
            h, *ys, wg, wbr)


def ffn_in(a, wout, res, g, wfi, after=None):
    t = res.shape[0]
    tm = min(256, t)
    ns = wfi.shape[2]

    def body(a_ref, wo_ref, r_ref, g_ref, w_ref, x_ref, h_ref, f_ref, act_ref):
        xv = r_ref[...] + _nn(a_ref[...], wo_ref[...])
        x_ref[...] = xv
        h = _rms_fwd(xv, g_ref[...]).astype(BF16)
        h_ref[...] = h
        for j in range(2):
            cs = slice(j * ns, (j + 1) * ns)
            fg = _nn(h, w_ref[j])
            fu = _nn(h, w_ref[j + 2])
            f_ref[0, :, cs] = fg.astype(BF16)
            f_ref[1, :, cs] = fu.astype(BF16)
            act_ref[:, cs] = (fg * _sigmoid(fg) * fu).astype(BF16)

    row = lambda c: bs((tm, c), lambda i: (i, 0))
    return _call(
        body, "ffn_in", (t // tm,),
        [row(D_MODEL), _resident(wout), row(D_MODEL), bs((1, D_MODEL), lambda i: (0, 0)), _resident(wfi)],
        [row(D_MODEL), row(D_MODEL), bs((2, tm, FFN_H), lambda i: (0, i, 0)), row(FFN_H)],
        [_sds((t, D_MODEL), F32), _sds((t, D_MODEL), BF16), _sds((2, t, FFN_H), BF16), _sds((t, FFN_H), BF16)],
        after=after)(a, wout, res, g, wfi)


def ple_fwd(a, wfo, res, g, wpg, p_i, wpp, after=None):
    t = res.shape[0]
    tm = min(256, t)

    def body(a_ref, wo_ref, r_ref, g_ref, wg_ref, p_ref, wp_ref, x_ref, h_ref, gt_ref, pp_ref, o_ref):
        xv = r_ref[...] + _nn(a_ref[...], wo_ref[...])
        x_ref[...] = xv
        h = _rms_fwd(xv, g_ref[...]).astype(BF16)
        h_ref[...] = h
        gate = _sigmoid(_nn(h, wg_ref[...]))
        pb = p_ref[...].astype(BF16)
        pp = jnp.concatenate([_nn(pb, wp_ref[j]) for j in range(N_SH)], axis=1)
        gt_ref[...] = gate.astype(BF16)
        pp_ref[...] = pp.astype(BF16)
        o_ref[...] = xv + gate * pp

    row = bs((tm, D_MODEL), lambda i: (i, 0))
    return _call(
        body, "ple_fwd", (t // tm,),
        [bs((tm, FFN_H), lambda i: (i, 0)), _resident(wfo), row, bs((1, D_MODEL), lambda i: (0, 0)), _resident(wpg),
         bs((tm, BW), lambda i: (i, 0)), _resident(wpp)],
        [row, row, row, row, row],
        [_sds((t, D_MODEL), F32), _sds((t, D_MODEL), BF16), _sds((t, D_MODEL), BF16), _sds((t, D_MODEL), BF16),
         _sds((t, D_MODEL), F32)], after=after)(a, wfo, res, g, wpg, p_i, wpp)


def loss_head(x, g, tgt):
    t = x.shape[0]
    tm = min(512, t)

    def body(x_ref, g_ref, t_ref, l_ref, dx_ref, dg_ref):
        @pl.when(pl.program_id(0) == 0)
        def _():
            l_ref[...] = jnp.zeros_like(l_ref)
            dg_ref[...] = jnp.zeros_like(dg_ref)

        xv, gv = x_ref[...], g_ref[...]
        err = _rms_fwd(xv, gv) - t_ref[...]
        part = 0.5 * jnp.sum(jnp.mean(err * err, axis=-1, keepdims=True), axis=0, keepdims=True)
        l_ref[...] += jnp.broadcast_to(part, l_ref.shape)
        dx, dgr = _rms_bwd(err * (1.0 / D_MODEL), xv, gv)
        dx_ref[...] = dx
        dg_ref[...] += _colsum(dgr)

    row = bs((tm, D_MODEL), lambda i: (i, 0))
    vec = bs((1, D_MODEL), lambda i: (0, 0))
    return _call(body, "loss_head", (t // tm,), [row, vec, row],
                 [bs((1, 128), lambda i: (0, 0)), row, vec],
                 [_sds((1, 128), F32), _sds((t, D_MODEL), F32), _sds((1, D_MODEL), F32)])(x, g, tgt)


def tn_matmul(name, a, b, grid, a_spec, b_spec, out_spec, out_shape, split=0, split_cols=0, into=None, after=None):
    last = len(grid) - 1

    def body(a_ref, b_ref, *rest):
        o_ref = rest[-1]

        @pl.when(pl.program_id(last) == 0)
        def _():
            o_ref[...] = jnp.zeros_like(o_ref)

        res = _tn(a_ref[...].astype(BF16), b_ref[...].astype(BF16))
        if split_cols:
            cols = res.shape[1] // split_cols
            for s in range(split_cols):
                o_ref[s] += res[:, s * cols:(s + 1) * cols]
        elif split:
            rows = res.shape[0] // split
            for s in range(split):
                o_ref[s] += res[s * rows:(s + 1) * rows]
        else:
            o_ref[...] += res

    if into is None:
        return _call(body, name, grid, [a_spec, b_spec], out_spec, out_shape, after=after)(a, b)
    return _call(body, name, grid, [a_spec, b_spec, ANY], out_spec, out_shape, aliases={2: 0}, after=after)(a, b, into)


def _resident(w):
    zeros = (0,) * w.ndim
    return bs(w.shape, lambda i: zeros, pipeline_mode=pl.Buffered(1))


def norm_bwd(name, sources, dx_in, x, g):
    t = x.shape[0]
    tm = min(256, t)
    n_src = len(sources)

    def body(*refs):
        dxi_ref, x_ref, g_ref, dx_ref, dg_ref = refs[2 * n_src:]

        @pl.when(pl.program_id(0) == 0)
        def _():
            dg_ref[...] = jnp.zeros_like(dg_ref)

        dh = None
        for si in range(n_src):
            for av, wv in sources[si][3](refs[2 * si], refs[2 * si + 1]):
                part = _nt(av, wv)
                dh = part if dh is None else dh + part
        dx, dgr = _rms_bwd(dh, x_ref[...], g_ref[...])
        dx_ref[...] = dxi_ref[...] + dx
        dg_ref[...] += _colsum(dgr)

    in_specs, args = [], []
    for a, a_spec, w, _ in sources:
        in_specs += [a_spec(tm), _resident(w)]
        args += [a, w]
    row = bs((tm, D_MODEL), lambda i: (i, 0))
    vec = bs((1, D_MODEL), lambda i: (0, 0))
    return _call(body, name, (t // tm,), in_specs + [row, row, vec], [row, vec],
                 [_sds((t, D_MODEL), F32), _sds((1, D_MODEL), F32)])(*args, dx_in, x, g)


def ple_bwd(dx, gate, pp, wpg, x, g, after=None):
    t = dx.shape[0]
    tm = min(512, t)

    def body(dx_ref, gt_ref, p_ref, w_ref, x_ref, g_ref, dpre_ref, dpp_ref, o_ref, dg_ref):
        @pl.when(pl.program_id(0) == 0)
        def _():
            dg_ref[...] = jnp.zeros_like(dg_ref)

        d = dx_ref[...]
        gt = gt_ref[...].astype(F32)
        dpre = (d * p_ref[...].astype(F32) * gt * (1.0 - gt)).astype(BF16)
        dpre_ref[...] = dpre
        dpp_ref[...] = (d * gt).astype(BF16)
        dxn, dgr = _rms_bwd(_nt(dpre, w_ref[...]), x_ref[...], g_ref[...])
        o_ref[...] = d + dxn
        dg_ref[...] += _colsum(dgr)

    row = bs((tm, D_MODEL), lambda i: (i, 0))
    vec = bs((1, D_MODEL), lambda i: (0, 0))
    return _call(body, "ple_bwd", (t // tm,), [row, row, row, _resident(wpg), row, vec], [row, row, row, vec],
                 [_sds((t, D_MODEL), BF16), _sds((t, D_MODEL), BF16), _sds((t, D_MODEL), F32), _sds((1, D_MODEL), F32)],
                 after=after)(dx, gate, pp, wpg, x, g)


def ffn_bwd(dx, wfo, fgu, wfi, x, g, after=None):
    t = dx.shape[0]
    tm = min(256, t)
    ns = FFN_H // 2

    def body(dx_ref, wo_ref, f_ref, wi_ref, x_ref, g_ref, df_ref, o_ref, dg_ref):
        @pl.when(pl.program_id(0) == 0)
        def _():
            dg_ref[...] = jnp.zeros_like(dg_ref)

        d = dx_ref[...]
        dxb = d.astype(BF16)
        dh = None
        for j in range(2):
            cs = slice(j * ns, (j + 1) * ns)
            dact = _nt(dxb, wo_ref[cs, :])
            fg = f_ref[0, :, cs].astype(F32)
            fu = f_ref[1, :, cs].astype(F32)
            s = _sigmoid(fg)
            dfg = (dact * fu * (s * (1.0 + fg * (1.0 - s)))).astype(BF16)
            dfu = (dact * fg * s).astype(BF16)
            df_ref[0, :, cs] = dfg
            df_ref[1, :, cs] = dfu
            part = _nt(dfg, wi_ref[j]) + _nt(dfu, wi_ref[j + 2])
            dh = part if dh is None else dh + part
        dxn, dgr = _rms_bwd(dh, x_ref[...], g_ref[...])
        o_ref[...] = d + dxn
        dg_ref[...] += _colsum(dgr)

    blk = bs((2, tm, FFN_H), lambda i: (0, i, 0))
    row = bs((tm, D_MODEL), lambda i: (i, 0))
    vec = bs((1, D_MODEL), lambda i: (0, 0))
    return _call(body, "ffn_bwd", (t // tm,), [row, _resident(wfo), blk, _resident(wfi), row, vec], [blk, row, vec],
                 [_sds((2, t, FFN_H), BF16), _sds((t, D_MODEL), F32), _sds((1, D_MODEL), F32)],
                 after=after)(dx, wfo, fgu, wfi, x, g)


def merge_bwd(dx, wout, gates, ybr, wbr):
    t = dx.shape[0]
    tm = min(256, t)

    def body(dx_ref, w_ref, g_ref, b_ref, wb_ref, dpre_ref, dyb_ref, dy_ref):
        dm = _nt(dx_ref[...].astype(BF16), w_ref[...])
        for k in range(N_BR):
            g = g_ref[k].astype(F32)
            dpre_ref[k] = (dm * b_ref[k].astype(F32) * g * (1.0 - g)).astype(BF16)
            dyb = (dm * g).astype(BF16)
            dyb_ref[k] = dyb
            acc = None
            for s in range(N_SH):
                part = _nt(dyb[:, s * BW:(s + 1) * BW], wb_ref[s, k])
                acc = part if acc is None else acc + part
            dy_ref[k] = acc

    blk = bs((N_BR, tm, D_MODEL), lambda i: (0, i, 0))
    return _call(body, "merge_bwd", (t // tm,),
                 [bs((tm, D_MODEL), lambda i: (i, 0)), _resident(wout), blk, blk, _resident(wbr)],
                 [blk, blk, bs((N_BR, tm, BW), lambda i: (0, i, 0))],
                 [_sds((N_BR, t, D_MODEL), BF16), _sds((N_BR, t, D_MODEL), BF16), _sds((N_BR, t, BW), F32)])(
                     dx, wout, gates, ybr, wbr)


def conva_fwd(proj, wa):
    t = proj.shape[0]
    tm, halo = min(512, t), 8
    per = tm // halo

    def body(b_ref, c_ref, x_ref, ch_ref, xh_ref, w_ref, y_ref):
        zh = jnp.where(pl.program_id(0) > 0, ch_ref[...] * xh_ref[...], 0.0)
        zext = jnp.concatenate([zh, c_ref[...] * x_ref[...]], axis=0)
        y_ref[...] = (b_ref[...] * _causal_conv(zext, w_ref, CONVA_K, halo)).astype(BF16)

    col = lambda c: bs((tm, BW), lambda i: (i, c))
    hal = lambda c: bs((halo, BW), lambda i: (_prev_blk(i, per), c))
    return _call(body, "conva_fwd", (t // tm,),
                 [col(0), col(1), col(2), hal(1), hal(2), bs((CONVA_K, BW), lambda i: (0, 0))],
                 bs((tm, BW), lambda i: (i, 0)), _sds((t, BW), BF16))(proj, proj, proj, proj, proj, wa)


def conva_bwd(proj, dys, wa, after=None):
    t = proj.shape[0]
    tm, halo = min(512, t), 8
    per = tm // halo
    last = t // halo - 1
    nt = t // tm

    def body(b_ref, c_ref, x_ref, ch_ref, xh_ref, bn_ref, dy_ref, dyn_ref, w_ref, db_ref, dc_ref, dxx_ref, dw_ref):
        i = pl.program_id(0)

        @pl.when(i == 0)
        def _():
            dw_ref[...] = jnp.zeros_like(dw_ref)

        zh = jnp.where(i > 0, ch_ref[...] * xh_ref[...], 0.0)
        cv, xv = c_ref[...], x_ref[...]
        zext = jnp.concatenate([zh, cv * xv], axis=0)
        dy = dy_ref[...]
        dconv = dy * b_ref[...]
        dcn = jnp.where(i < nt - 1, dyn_ref[...] * bn_ref[...], 0.0)
        dz = _anti_conv(jnp.concatenate([dconv, dcn], axis=0), w_ref, CONVA_K, tm)
        db_ref[...] = (dy * _causal_conv(zext, w_ref, CONVA_K, halo)).astype(BF16)
        dc_ref[...] = (dz * xv).astype(BF16)
        dxx_ref[...] = (dz * cv).astype(BF16)
        _conv_wgrad(dw_ref, dconv, zext, CONVA_K, halo)

    col = lambda c: bs((tm, BW), lambda i: (i, c))
    hal = lambda c: bs((halo, BW), lambda i: (_prev_blk(i, per), c))
    nxt = bs((halo, BW), lambda i: (_next_blk(i, per, last), 0))
    wsp = bs((CONVA_K, BW), lambda i: (0, 0))
    outs = _call(body, "conva_bwd", (t // tm,),
                 [col(0), col(1), col(2), hal(1), hal(2), nxt,
                  bs((None, tm, BW), lambda i: (0, i, 0)), bs((None, halo, BW), lambda i: (0, _next_blk(i, per, last), 0)), wsp],
                 [bs((tm, BW), lambda i: (i, 0))] * 3 + [wsp],
                 [_sds((t, BW), BF16)] * 3 + [_sds((CONVA_K, BW), F32)], after=after)(
                     proj, proj, proj, proj, proj, proj, dys, dys, wa)
    return outs[:3], outs[3]


def _head_masks():
    lane = lax.broadcasted_iota(jnp.int32, (1, BW), 1)
    return [(lane >= h * HEAD_D) & (lane < (h + 1) * HEAD_D) for h in range(HEADS)]


def _band_masks():
    qi = lax.broadcasted_iota(jnp.int32, (BLK, BLK), 0)
    ki = lax.broadcasted_iota(jnp.int32, (BLK, BLK), 1)
    return ki >= qi, ki <= qi


def attn_fwd_group(pv, d):
    rows = pv.shape[0]
    qb = min(512, rows)
    nb = qb // BLK
    scale = HEAD_D ** -0.5

    def body(q_ref, k_ref, v_ref, kh_ref, vh_ref, o_ref):
        n = pl.program_id(1)
        hm = _head_masks()
        m_prev, m_cur = _band_masks()
        for b in range(nb):
            rs = slice(b * BLK, (b + 1) * BLK)
            q = q_ref[rs, :]
            if b == 0:
                kp, vp = kh_ref[...], vh_ref[...]
                mp = m_prev & (n > 0)
            else:
                ps = slice((b - 1) * BLK, b * BLK)
                kp, vp = k_ref[ps, :], v_ref[ps, :]
                mp = m_prev
            qs = jnp.concatenate([jnp.where(hm[h], q, 0.0).astype(BF16) for h in range(HEADS)], axis=0)
            kcat = jnp.concatenate([kp, k_ref[rs, :]], axis=0)
            vcat = jnp.concatenate([vp, v_ref[rs, :]], axis=0)
            band = jnp.concatenate([mp, m_cur], axis=1)
            s = jnp.where(jnp.concatenate([band] * HEADS, axis=0), _nt(qs, kcat) * scale, NEG)
            m = jnp.max(s, axis=-1, keepdims=True)
            e = jnp.exp(s - m)
            l = jnp.sum(e, axis=-1, keepdims=True)
            of = _nn(e.astype(BF16), vcat) / l
            lse = m + jnp.log(l)
            o_acc = jnp.zeros((BLK, BW), F32)
            l_acc = jnp.zeros((BLK, BW), F32)
            for h in range(HEADS):
                hs = slice(h * BLK, (h + 1) * BLK)
                o_acc = jnp.where(hm[h], of[hs, :], o_acc)
                l_acc = jnp.where(hm[h], lse[hs, :], l_acc)
            o_ref[rs, :BW] = o_acc
            o_ref[rs, BW:] = l_acc

    per = qb // BLK
    main = lambda c: bs((qb, BW), lambda r, n: (n, r * 3 + c))
    hal = lambda c: bs((BLK, BW), lambda r, n: (_prev_blk(n, per), r * 3 + c))
    return _call(body, f"attn_fwd_d{d}", (d, rows // qb), [main(0), main(1), main(2), hal(1), hal(2)],
                 bs((qb, 2 * BW), lambda r, n: (n, r)), _sds((rows, d * 2 * BW), F32))(pv, pv, pv, pv, pv)


def attn_merge(ols):
    t = ols[0].shape[0]
    tm = min(512, t)
    width = 2 * BW

    def lse3(a, b, c):
        m = jnp.maximum(jnp.maximum(a, b), c)
        return m + jnp.log(jnp.exp(a - m) + jnp.exp(b - m) + jnp.exp(c - m))

    def body(g0, g1, g2, y_ref, o_ref, l_ref, scr, nat1, nat2):
        for src, nat, d in ((g1, nat1, DILATIONS[1]), (g2, nat2, DILATIONS[2])):
            for c in range(width // LANES):
                nat[:, c * LANES:(c + 1) * LANES] = _from_strided_view(src, scr, d, width, c)
        gs = [g0[...], nat1[...], nat2[...]]
        ls = [g[:, BW:] for g in gs]
        tot = lse3(*ls)
        o = (jnp.exp(ls[0] - tot) * gs[0][:, :BW] + jnp.exp(ls[1] - tot) * gs[1][:, :BW]
             + jnp.exp(ls[2] - tot) * gs[2][:, :BW])
        y_ref[...] = o.astype(BF16)
        o_ref[...] = o
        l_ref[...] = tot

    n = bs((tm, BW), lambda i: (i, 0))
    return _call(body, "attn_merge", (t // tm,),
                 [_view_spec(tm, 1, width), _view_spec(tm, DILATIONS[1], width), _view_spec(tm, DILATIONS[2], width)],
                 [n, n, n], [_sds((t, BW), BF16), _sds((t, BW), F32), _sds((t, BW), F32)],
                 scratch=[pltpu.VMEM((tm, LANES), F32), pltpu.VMEM((tm, width), F32), pltpu.VMEM((tm, width), F32)])(*ols)


def attn_delta(dys, o, lse):
    t = o.shape[0]
    tm = min(512, t)

    def body(d_ref, o_ref, l_ref, ld1, ld4, ld16, dy4, dy16, scr):
        hm = _head_masks()
        dy = d_ref[...]
        prod = dy * o_ref[...]
        delta = jnp.zeros_like(prod)
        for h in range(HEADS):
            delta = jnp.where(hm[h], jnp.sum(jnp.where(hm[h], prod, 0.0), axis=-1, keepdims=True), delta)
        ld = jnp.concatenate([l_ref[...], delta], axis=1)
        ld1[...] = ld
        for d, ld_v, dy_v in ((DILATIONS[1], ld4, dy4), (DILATIONS[2], ld16, dy16)):
            _to_strided_view(ld_v, lambda c: ld[:, c * LANES:(c + 1) * LANES], scr, d, 2 * BW)
            _to_strided_view(dy_v, lambda c: dy[:, c * LANES:(c + 1) * LANES], scr, d, BW)

    n = bs((tm, BW), lambda i: (i, 0))
    d4, d16 = DILATIONS[1], DILATIONS[2]
    outs = _call(body, "attn_delta", (t // tm,), [bs((None, tm, BW), lambda i: (1, i, 0)), n, n],
                 [_view_spec(tm, 1, 2 * BW), _view_spec(tm, d4, 2 * BW), _view_spec(tm, d16, 2 * BW),
                  _view_spec(tm, d4, BW), _view_spec(tm, d16, BW)],
                 [_sds((t, 2 * BW), F32), _sds((t // d4, d4 * 2 * BW), F32), _sds((t // d16, d16 * 2 * BW), F32),
                  _sds((t // d4, d4 * BW), F32), _sds((t // d16, d16 * BW), F32)],
                 scratch=[pltpu.VMEM((tm, LANES), F32)])(dys, o, lse)
    return outs[:3], outs[3:]


def attn_bwd_group(pv, dov, ldv, d):
    rows = pv.shape[0]
    qb = min(512, rows)
    nb = qb // BLK
    nsteps = rows // qb
    scale = HEAD_D ** -0.5

    def body(q_ref, qn_ref, k_ref, kh_ref, v_ref, vh_ref, do_ref, don_ref, ld_ref, ldn_ref, o_ref):
        n = pl.program_id(1)
        hm = _head_masks()
        m_prev, m_cur = _band_masks()
        has_prev, has_next = n > 0, n < nsteps - 1
        dq = [None] * nb
        dk = [jnp.zeros((BLK, BW), F32) for _ in range(nb)]
        dvv = [jnp.zeros((BLK, BW), F32) for _ in range(nb)]
        for qi in range(nb + 1):
            rs = slice(qi * BLK, (qi + 1) * BLK)
            ps = slice((qi - 1) * BLK, qi * BLK)
            if qi < nb:
                q, do, ldq = q_ref[rs, :], do_ref[rs, :], ld_ref[rs, :]
            else:
                q, do, ldq = qn_ref[...], don_ref[...], ldn_ref[...]
            kp, vp = (kh_ref[...], vh_ref[...]) if qi == 0 else (k_ref[ps, :], v_ref[ps, :])
            kc, vc = (k_ref[rs, :], v_ref[rs, :]) if qi < nb else (kp, vp)
            mp = m_prev & has_prev if qi == 0 else (m_prev & has_next if qi == nb else m_prev)
            mc = m_cur if qi < nb else jnp.zeros_like(m_cur)
            band = jnp.concatenate([jnp.concatenate([mp, mc], axis=1)] * HEADS, axis=0)
            qs = jnp.concatenate([jnp.where(hm[h], q, 0.0).astype(BF16) for h in range(HEADS)], axis=0)
            dos = jnp.concatenate([jnp.where(hm[h], do, 0.0).astype(BF16) for h in range(HEADS)], axis=0)
            kcat = jnp.concatenate([kp, kc], axis=0)
            vcat = jnp.concatenate([vp, vc], axis=0)
            col = lambda v, h: jnp.broadcast_to(jnp.max(jnp.where(hm[h], v, NEG), axis=-1, keepdims=True), (BLK, 2 * BLK))
            lcols = jnp.concatenate([col(ldq[:, :BW], h) for h in range(HEADS)], axis=0)
            dcols = jnp.concatenate([col(ldq[:, BW:], h) for h in range(HEADS)], axis=0)
            p = jnp.where(band, jnp.exp(_nt(qs, kcat) * scale - lcols), 0.0)
            ds = (p * (_nt(dos, vcat) - dcols) * scale).astype(BF16)
            if qi < nb:
                dqf = _nn(ds, kcat)
                acc_q = jnp.zeros((BLK, BW), F32)
                for h in range(HEADS):
                    acc_q = jnp.where(hm[h], dqf[h * BLK:(h + 1) * BLK, :], acc_q)
                dq[qi] = acc_q
            dkc = _tn(ds, qs)
            dvc = _tn(p.astype(BF16), dos)
            if qi >= 1:
                dk[qi - 1] = dk[qi - 1] + dkc[:BLK]
                dvv[qi - 1] = dvv[qi - 1] + dvc[:BLK]
            if qi < nb:
                dk[qi] = dk[qi] + dkc[BLK:]
                dvv[qi] = dvv[qi] + dvc[BLK:]
        for b in range(nb):
            rs = slice(b * BLK, (b + 1) * BLK)
            for c, val in enumerate((dq[b], dk[b], dvv[b])):
                cs = slice(c * BW, (c + 1) * BW)
                o_ref[rs, cs] = val

    per = qb // BLK
    last = rows // BLK - 1
    main = lambda c: bs((qb, BW), lambda r, n: (n, r * 3 + c))
    prv = lambda c: bs((BLK, BW), lambda r, n: (_prev_blk(n, per), r * 3 + c))
    nxt = lambda c: bs((BLK, BW), lambda r, n: (_next_blk(n, per, last), r * 3 + c))
    accs = bs((qb, 3 * BW), lambda r, n: (n, r))
    in_specs = [main(0), nxt(0), main(1), prv(1), main(2), prv(2),
                bs((qb, BW), lambda r, n: (n, r)), bs((BLK, BW), lambda r, n: (_next_blk(n, per, last), r)),
                bs((qb, 2 * BW), lambda r, n: (n, r)), bs((BLK, 2 * BW), lambda r, n: (_next_blk(n, per, last), r))]
    args = [pv, pv, pv, pv, pv, pv, dov, dov, ldv, ldv]
    return _call(body, f"attn_bwd_d{d}", (d, nsteps), in_specs, accs, _sds((rows, d * 3 * BW), F32))(*args)


def attn_bwd_finish(parts):
    t = parts[0].shape[0]
    tm = min(512, t)
    width = 3 * BW

    def body(g0, g1, g2, o_ref, scr):
        for c in range(width // LANES):
            cs = slice(c * LANES, (c + 1) * LANES)
            acc = g0[:, cs]
            acc = acc + _from_strided_view(g1, scr, DILATIONS[1], width, c)
            acc = acc + _from_strided_view(g2, scr, DILATIONS[2], width, c)
            o_ref[:, cs] = acc.astype(BF16)

    return _call(body, "attn_bwd_finish", (t // tm,),
                 [_view_spec(tm, 1, width), _view_spec(tm, DILATIONS[1], width), _view_spec(tm, DILATIONS[2], width)],
                 bs((tm, width), lambda i: (i, 0)), _sds((t, width), BF16),
                 scratch=[pltpu.VMEM((tm, LANES), F32)])(*parts)


def _group_masks():
    lane = lax.broadcasted_iota(jnp.int32, (1, BW), 1)
    return [(lane >= g * HEAD_D) & (lane < (g + 1) * HEAD_D) for g in range(4)]


def sgu_fwd(proj, ln_g, ln_b, w_tril, b_full):
    t = proj.shape[0]
    tm = min(512, t)

    def body(u_ref, v_ref, g_ref, b_ref, w_ref, bf_ref, y_ref):
        gm = _group_masks()
        xhat, _ = _ln_hat(v_ref[...])
        vb = (xhat * g_ref[...] + b_ref[...]).astype(BF16)
        for c in range(tm // BLK):
            rs = slice(c * BLK, (c + 1) * BLK)
            vc = vb[rs, :]
            mixed = bf_ref[...]
            for g in range(4):
                mixed = mixed + jnp.where(gm[g], _nn(w_ref[g], vc), 0.0)
            y_ref[rs, :] = (u_ref[rs, :] * mixed).astype(BF16)

    vec = bs((1, BW), lambda i: (0, 0))
    return _call(body, "sgu_fwd", (t // tm,),
                 [bs((tm, BW), lambda i: (i, 6)), bs((tm, BW), lambda i: (i, 7)), vec, vec,
                  bs((4, BLK, BLK), lambda i: (0, 0, 0)), bs((BLK, BW), lambda i: (0, 0))],
                 bs((tm, BW), lambda i: (i, 0)), _sds((t, BW), BF16))(proj, proj, ln_g, ln_b, w_tril, b_full)


def sgu_bwd(proj, dys, ln_g, ln_b, w_tril, b_full):
    t = proj.shape[0]
    tm = min(512, t)

    def body(u_ref, v_ref, dy_ref, g_ref, b_ref, w_ref, bf_ref, du_ref, dv_ref, dw_ref, dbf_ref, dg_ref, db_ref, dvl_ref):
        @pl.when(pl.program_id(0) == 0)
        def _():
            dw_ref[...] = jnp.zeros_like(dw_ref)
            dbf_ref[...] = jnp.zeros_like(dbf_ref)
            dg_ref[...] = jnp.zeros_like(dg_ref)
            db_ref[...] = jnp.zeros_like(db_ref)

        gm = _group_masks()
        xhat, r = _ln_hat(v_ref[...])
        gv = g_ref[...]
        vb = (xhat * gv + b_ref[...]).astype(BF16)
        for c in range(tm // BLK):
            rs = slice(c * BLK, (c + 1) * BLK)
            vc = vb[rs, :]
            dy = dy_ref[rs, :]
            mixed = bf_ref[...]
            for g in range(4):
                mixed = mixed + jnp.where(gm[g], _nn(w_ref[g], vc), 0.0)
            du_ref[rs, :] = (dy * mixed).astype(BF16)
            dm = dy * u_ref[rs, :]
            dbf_ref[...] += dm
            dvl = jnp.zeros((BLK, BW), F32)
            for g in range(4):
                dmg = jnp.where(gm[g], dm, 0.0).astype(BF16)
                dw_ref[g] += _nt(dmg, vc)
                dvl = dvl + _tn(w_ref[g], dmg)
            dvl_ref[rs, :] = dvl
        dvl = dvl_ref[...]
        dv_ref[...] = _ln_bwd(dvl, xhat, r, gv).astype(BF16)
        dg_ref[...] += _colsum(dvl * xhat)
        db_ref[...] += _colsum(dvl)

    vec = bs((1, BW), lambda i: (0, 0))
    row = bs((tm, BW), lambda i: (i, 0))
    wsp = bs((4, BLK, BLK), lambda i: (0, 0, 0))
    bfs = bs((BLK, BW), lambda i: (0, 0))
    return _call(body, "sgu_bwd", (t // tm,),
                 [bs((tm, BW), lambda i: (i, 6)), bs((tm, BW), lambda i: (i, 7)), bs((None, tm, BW), lambda i: (2, i, 0)),
                  vec, vec, wsp, bfs],
                 [row, row, wsp, bfs, vec, vec],
                 [_sds((t, BW), BF16), _sds((t, BW), BF16), _sds((4, BLK, BLK), F32), _sds((BLK, BW), F32),
                  _sds((1, BW), F32), _sds((1, BW), F32)],
                 scratch=[pltpu.VMEM((tm, BW), F32)])(proj, proj, dys, ln_g, ln_b, w_tril, b_full)


CONF_HALO = 32


def conf_fwd(proj, dw, ln_g, ln_b, after=None):
    t = proj.shape[0]
    tm, halo = min(512, t), CONF_HALO
    per = tm // halo

    def body(v_ref, gt_ref, vh_ref, gh_ref, w_ref, g_ref, b_ref, y_ref, z_ref):
        yh = jnp.where(pl.program_id(0) > 0, vh_ref[...] * _sigmoid(gh_ref[...]), 0.0)
        yext = jnp.concatenate([yh, v_ref[...] * _sigmoid(gt_ref[...])], axis=0)
        z = _causal_conv(yext, w_ref, CONF_K, halo)
        z_ref[...] = z
        xhat, _ = _ln_hat(z)
        ln = xhat * g_ref[...] + b_ref[...]
        y_ref[...] = (ln * _sigmoid(ln)).astype(BF16)

    vec = bs((1, BW), lambda i: (0, 0))
    col = lambda c: bs((tm, BW), lambda i: (i, c))
    hal = lambda c: bs((halo, BW), lambda i: (_prev_blk(i, per), c))
    row = bs((tm, BW), lambda i: (i, 0))
    return _call(body, "conf_fwd", (t // tm,),
                 [col(8), col(9), hal(8), hal(9), bs((CONF_K, BW), lambda i: (0, 0)), vec, vec],
                 [row, row], [_sds((t, BW), BF16), _sds((t, BW), F32)], after=after)(
                     proj, proj, proj, proj, dw, ln_g, ln_b)


def conf_bwd_ln(z, dys, ln_g, ln_b):
    t = z.shape[0]
    tm = min(1024, t)

    def body(z_ref, dy_ref, g_ref, b_ref, dz_ref, dg_ref, db_ref):
        @pl.when(pl.program_id(0) == 0)
        def _():
            dg_ref[...] = jnp.zeros_like(dg_ref)
            db_ref[...] = jnp.zeros_like(db_ref)

        gv = g_ref[...]
        xhat, r = _ln_hat(z_ref[...])
        ln = xhat * gv + b_ref[...]
        s = _sigmoid(ln)
        dln = dy_ref[...] * (s * (1.0 + ln * (1.0 - s)))
        dz_ref[...] = _ln_bwd(dln, xhat, r, gv)
        dg_ref[...] += _colsum(dln * xhat)
        db_ref[...] += _colsum(dln)

    vec = bs((1, BW), lambda i: (0, 0))
    row = bs((tm, BW), lambda i: (i, 0))
    return _call(body, "conf_bwd_ln", (t // tm,), [row, bs((None, tm, BW), lambda i: (3, i, 0)), vec, vec],
                 [row, vec, vec], [_sds((t, BW), F32), _sds((1, BW), F32), _sds((1, BW), F32)])(z, dys, ln_g, ln_b)


def conf_bwd_conv(proj, dz, dw):
    t = proj.shape[0]
    tm, halo = min(512, t), CONF_HALO
    per = tm // halo
    last = t // halo - 1
    nt = t // tm

    def body(v_ref, gt_ref, vh_ref, gh_ref, dz_ref, dzn_ref, w_ref, dv_ref, dg_ref, dw_ref):
        i = pl.program_id(0)

        @pl.when(i == 0)
        def _():
            dw_ref[...] = jnp.zeros_like(dw_ref)

        val = v_ref[...]
        sg = _sigmoid(gt_ref[...])
        yh = jnp.where(i > 0, vh_ref[...] * _sigmoid(gh_ref[...]), 0.0)
        yext = jnp.concatenate([yh, val * sg], axis=0)
        dz = dz_ref[...]
        dzn = jnp.where(i < nt - 1, dzn_ref[...], 0.0)
        dy0 = _anti_conv(jnp.concatenate([dz, dzn], axis=0), w_ref, CONF_K, tm)
        dv_ref[...] = (dy0 * sg).astype(BF16)
        dg_ref[...] = (dy0 * val * sg * (1.0 - sg)).astype(BF16)
        _conv_wgrad(dw_ref, dz, yext, CONF_K, halo)

    col = lambda c: bs((tm, BW), lambda i: (i, c))
    hal = lambda c: bs((halo, BW), lambda i: (_prev_blk(i, per), c))
    row = bs((tm, BW), lambda i: (i, 0))
    wsp = bs((CONF_K, BW), lambda i: (0, 0))
    return _call(body, "conf_bwd_conv", (t // tm,),
                 [col(8), col(9), hal(8), hal(9), row, bs((halo, BW), lambda i: (_next_blk(i, per, last), 0)), wsp],
                 [row, row, wsp], [_sds((t, BW), BF16), _sds((t, BW), BF16), _sds((CONF_K, BW), F32)])(
                     proj, proj, proj, proj, dz, dz, dw)


def _place():
    return lax.axis_index("x"), lax.axis_index("y"), lax.axis_index("c")


def _comm_call(body, name, n_in, out_shape, scratch, aliases=None):
    return pl.pallas_call(body, name=name, in_specs=[ANY] * n_in, out_specs=[ANY] * len(out_shape), out_shape=out_shape,
                          scratch_shapes=scratch, input_output_aliases=aliases or {},
                          compiler_params=pltpu.CompilerParams(has_side_effects=True, vmem_limit_bytes=VMEM_LIMIT))


HBM_SPEC = pl.BlockSpec(memory_space=pltpu.HBM)
SEM_SPEC = pl.BlockSpec(memory_space=pltpu.SEMAPHORE)
EFFECT = pltpu.SideEffectType.DATAFLOW_SIDE_EFFECTING


class SplitExchange:
    def __init__(self, name, bufs, plan, n_copies):
        self.name, self.bufs, self.plan, self.n = name, list(bufs), plan, n_copies

    def start(self, after):
        nb, n, plan = len(self.bufs), self.n, self.plan

        def body(*refs):
            send, recv, token = refs[nb + 1], refs[nb + 2], refs[-1]
            for k, (src, dst, _, dev) in enumerate(plan(refs[:nb])):
                pltpu.make_async_remote_copy(src_ref=src, dst_ref=dst, send_sem=send.at[k], recv_sem=recv.at[k],
                                             device_id=dev, device_id_type=MESH).start()
            token[...] = jnp.zeros_like(token)

        outs = pl.pallas_call(
            body, name=self.name + "_start",
            out_shape=(pltpu.SemaphoreType.DMA((n,)), pltpu.SemaphoreType.DMA((n,)),
                       *[pltpu.HBM(b.shape, b.dtype) for b in self.bufs], _sds((8, 128), F32)),
            in_specs=[HBM_SPEC] * nb + [ANY],
            out_specs=(SEM_SPEC, SEM_SPEC, *[HBM_SPEC] * nb, pl.BlockSpec(memory_space=pltpu.VMEM)),
            input_output_aliases={i: 2 + i for i in range(nb)},
            compiler_params=pltpu.CompilerParams(has_side_effects=EFFECT))(
                *[pltpu.with_memory_space_constraint(b, pltpu.HBM) for b in self.bufs], after)
        self.send, self.recv, self.bufs = outs[0], outs[1], list(outs[2:2 + nb])
        return outs[-1]

    def wait(self, after):
        nb, plan = len(self.bufs), self.plan
        after = list(after) if isinstance(after, (list, tuple)) else [after]

        def body(*refs):
            send, recv = refs[nb], refs[nb + 1]
            for k, (src, _, land, dev) in enumerate(plan(refs[:nb])):
                cp = pltpu.make_async_remote_copy(src_ref=src, dst_ref=land, send_sem=send.at[k], recv_sem=recv.at[k],
                                                  device_id=dev, device_id_type=MESH)
                cp.wait_send()
                cp.wait_recv()

        outs = pl.pallas_call(
            body, name=self.name + "_wait", out_shape=tuple(pltpu.HBM(b.shape, b.dtype) for b in self.bufs),
            in_specs=[HBM_SPEC] * nb + [SEM_SPEC, SEM_SPEC] + [ANY] * len(after), out_specs=[HBM_SPEC] * nb,
            input_output_aliases={i: i for i in range(nb)},
            compiler_params=pltpu.CompilerParams(has_side_effects=EFFECT))(*self.bufs, self.send, self.recv, *after)
        return list(outs)


def _chips_of(x, y):
    return [(1 - x, y), (x, 1 - y), (1 - x, 1 - y)]


def allgather_ici_plan(shapes):
    def plan(refs):
        x, y, c = _place()
        out = []
        for a, ref in enumerate(refs):
            hl = shapes[a][1] // 2
            half = pl.ds(c * hl, hl)
            for cx, cy in _chips_of(x, y):
                mine = ref.at[2 * x + y, half]
                out.append((mine, mine, ref.at[2 * cx + cy, half], (cx, cy, c)))
        return out
    return plan


def allgather_d2d_plan(shapes):
    def plan(refs):
        x, y, c = _place()
        out = []
        for a, ref in enumerate(refs):
            hl = shapes[a][1] // 2
            for cx, cy in _chips_of(x, y):
                got = ref.at[2 * cx + cy, pl.ds(c * hl, hl)]
                out.append((got, got, ref.at[2 * cx + cy, pl.ds((1 - c) * hl, hl)], (x, y, 1 - c)))
        return out
    return plan


def gather8(v, reduce):
    rows, cols = v.shape

    def body(v_ref, o_ref, land_ref, send, recv, lsem):
        x, y, c = _place()
        me = 4 * x + 2 * y + c
        land = land_ref if reduce else o_ref
        mine = pltpu.make_async_copy(v_ref, land.at[me], lsem)
        mine.start()
        sent = []
        for j in range(1, 8):
            fx, fy, fc = (j >> 2) & 1, (j >> 1) & 1, j & 1
            tgt = (1 - x if fx else x, 1 - y if fy else y, 1 - c if fc else c)
            cp = pltpu.make_async_remote_copy(src_ref=v_ref, dst_ref=land.at[me], send_sem=send.at[j - 1],
                                              recv_sem=recv.at[j - 1], device_id=tgt, device_id_type=MESH)
            cp.start()
            sent.append(cp)
        for j in range(1, 8):
            fx, fy, fc = (j >> 2) & 1, (j >> 1) & 1, j & 1
            peer = 4 * (1 - x if fx else x) + 2 * (1 - y if fy else y) + (1 - c if fc else c)
            pltpu.make_async_remote_copy(src_ref=v_ref, dst_ref=land.at[peer], send_sem=send.at[j - 1],
                                         recv_sem=recv.at[j - 1], device_id=(x, y, c), device_id_type=MESH).wait_recv()
        for cp in sent:
            cp.wait_send()
        mine.wait()
        if reduce:
            acc = land_ref[0]
            for k in range(1, 8):
                acc = acc + land_ref[k]
            o_ref[...] = acc

    vm = pl.BlockSpec(memory_space=pltpu.VMEM)
    out_shape = _sds((rows, cols), F32) if reduce else _sds((8, rows, cols), F32)
    land_shape = (8, rows, cols) if reduce else (8, 128)
    return pl.pallas_call(
        body, name="allreduce8" if reduce else "allgather8", in_specs=[vm], out_specs=vm, out_shape=out_shape,
        scratch_shapes=[pltpu.VMEM(land_shape, F32), pltpu.SemaphoreType.DMA((7,)), pltpu.SemaphoreType.DMA((7,)),
                        pltpu.SemaphoreType.DMA],
        compiler_params=pltpu.CompilerParams(has_side_effects=True, vmem_limit_bytes=VMEM_LIMIT))(v)


def allgather_weights(bufs):
    n = len(bufs)

    def body(*refs):
        ins, outs = refs[:n], refs[n:2 * n]
        send, recv = refs[2 * n:]
        x, y, c = _place()
        s_me = 2 * x + y
        chips = [(1 - x, y), (x, 1 - y), (1 - x, 1 - y)]
        sibling = (x, y, 1 - c)
        started = []
        for a in range(n):
            hl = bufs[a].shape[1] // 2
            half = pl.ds(c * hl, hl)
            for j, chip in enumerate(chips):
                cp = pltpu.make_async_remote_copy(src_ref=ins[a].at[s_me, half], dst_ref=outs[a].at[s_me, half],
                                                  send_sem=send.at[6 * a + j], recv_sem=recv.at[6 * a + j],
                                                  device_id=(chip[0], chip[1], c), device_id_type=MESH)
                cp.start()
                started.append(cp)
        for a in range(n):
            hl = bufs[a].shape[1] // 2
            half = pl.ds(c * hl, hl)
            for j, chip in enumerate(chips):
                s_j = 2 * chip[0] + chip[1]
                landed = outs[a].at[s_j, half]
                pltpu.make_async_remote_copy(src_ref=landed, dst_ref=landed, send_sem=send.at[6 * a + j],
                                             recv_sem=recv.at[6 * a + j], device_id=sibling, device_id_type=MESH).wait_recv()
                fw = pltpu.make_async_remote_copy(src_ref=landed, dst_ref=landed, send_sem=send.at[6 * a + 3 + j],
                                                  recv_sem=recv.at[6 * a + 3 + j], device_id=sibling, device_id_type=MESH)
                fw.start()
                started.append(fw)
        for a in range(n):
            hl = bufs[a].shape[1] // 2
            other = pl.ds((1 - c) * hl, hl)
            for j, chip in enumerate(chips):
                s_j = 2 * chip[0] + chip[1]
                theirs = outs[a].at[s_j, other]
                pltpu.make_async_remote_copy(src_ref=theirs, dst_ref=theirs, send_sem=send.at[6 * a + 3 + j],
                                             recv_sem=recv.at[6 * a + 3 + j], device_id=sibling, device_id_type=MESH).wait_recv()
        for cp in started:
            cp.wait_send()

    out_shape = [_sds(b.shape, b.dtype) for b in bufs]
    scratch = [pltpu.SemaphoreType.DMA((6 * n,)), pltpu.SemaphoreType.DMA((6 * n,))]
    return _comm_call(body, "allgather_weights", n, out_shape, scratch, aliases={a: a for a in range(n)})(*bufs)


def _row_tile(rows, cols):
    best = 16
    for t in range(16, rows + 1, 16):
        if rows % t == 0 and t * cols * 4 <= 2 * 1024 * 1024:
            best = t
    return best


def _rs_add_sibling(scal, g, ra, hr):
    cols = g.shape[2]
    tr = _row_tile(hr, cols)
    nr = hr // tr

    def body(s_ref, g_ref, r_ref, p32_ref, p16_ref):
        v = g_ref[...] + r_ref[...]
        p16_ref[...] = v.astype(BF16)

        @pl.when(pl.program_id(1) == s_ref[0])
        def _():
            p32_ref[...] = v

    blk = lambda f: bs((None, tr, cols), f)
    own = blk(lambda i, s, sr: (s, i, 0))
    spec = pltpu.PrefetchScalarGridSpec(num_scalar_prefetch=1, grid=(nr, N_SH),
                                        in_specs=[blk(lambda i, s, sr: (s, sr[1] * nr + i, 0)), own],
                                        out_specs=[bs((tr, cols), lambda i, s, sr: (i, 0)), own])
    return pl.pallas_call(body, name="rs_add_sibling", grid_spec=spec,
                          out_shape=[_sds((hr, cols), F32), _sds((N_SH, hr, cols), BF16)],
                          compiler_params=pltpu.CompilerParams(dimension_semantics=("arbitrary",) * 2,
                                                               vmem_limit_bytes=VMEM_LIMIT))(scal, g, ra)


def _rs_add_chips(scal, p32, rb, hr):
    cols = p32.shape[1]
    tr = _row_tile(hr, cols)
    nr = hr // tr

    def body(s_ref, p_ref, r0, r1, r2, o_ref):
        o_ref[...] = ((p_ref[...] + r0[...].astype(F32)) + r1[...].astype(F32)) + r2[...].astype(F32)

    blk = lambda f: bs((None, tr, cols), f)
    spec = pltpu.PrefetchScalarGridSpec(
        num_scalar_prefetch=1, grid=(nr,),
        in_specs=[bs((tr, cols), lambda i, sr: (i, 0))] + [blk(functools.partial(lambda i, sr, j: (j, i, 0), j=j))
                                                            for j in range(3)],
        out_specs=blk(lambda i, sr: (sr[1], i, 0)))
    return pl.pallas_call(body, name="rs_add_chips", grid_spec=spec, out_shape=_sds((2, hr, cols), F32),
                          compiler_params=pltpu.CompilerParams(dimension_semantics=("arbitrary",),
                                                               vmem_limit_bytes=VMEM_LIMIT))(scal, p32, rb, rb, rb)


class SplitReduceScatter:
    def __init__(self, gs):
        x, y, c = _place()
        self.scal = jnp.stack([2 * x + y, c]).astype(jnp.int32)
        self.gs, self.n = list(gs), len(gs)
        self.hrs = [g.shape[1] // 2 for g in gs]

    def swap_start(self, after):
        n, hrs = self.n, self.hrs

        def plan(refs):
            x, y, c = _place()
            return [(refs[a].at[:, pl.ds((1 - c) * hrs[a], hrs[a])], refs[n + a], refs[n + a], (x, y, 1 - c))
                    for a in range(n)]

        lands = [lax.empty((N_SH, hrs[a], g.shape[2]), F32) for a, g in enumerate(self.gs)]
        self.ex = SplitExchange("rs_swap_halves", self.gs + lands, plan, n)
        return self.ex.start(after)

    def swap_wait_send_start(self, after):
        n, hrs = self.n, self.hrs
        bufs = self.ex.wait(after)
        parts = [_rs_add_sibling(self.scal, bufs[a], bufs[n + a], hrs[a]) for a in range(n)]
        self.p32 = [p[0] for p in parts]

        def plan(refs):
            x, y, c = _place()
            return [(refs[a].at[2 * cx + cy], refs[n + a].at[j], refs[n + a].at[j], (cx, cy, c))
                    for a in range(n) for j, (cx, cy) in enumerate(_chips_of(x, y))]

        lands = [lax.empty((3, hrs[a], g.shape[2]), BF16) for a, g in enumerate(self.gs)]
        self.ex = SplitExchange("rs_send_partials", [p[1] for p in parts] + lands, plan, 3 * n)
        return self.ex.start(parts[-1][1])

    def send_wait_share_start(self, after):
        n, hrs = self.n, self.hrs
        bufs = self.ex.wait(after)
        fins = [_rs_add_chips(self.scal, self.p32[a], bufs[n + a], hrs[a]) for a in range(n)]

        def plan(refs):
            x, y, c = _place()
            return [(refs[a].at[c], refs[a].at[c], refs[a].at[1 - c], (x, y, 1 - c)) for a in range(n)]

        self.ex = SplitExchange("rs_share_halves", fins, plan, n)
        return self.ex.start(fins[-1])

    def share_wait(self, after):
        fulls = self.ex.wait(after)
        return [f.reshape(2 * hr, f.shape[2]) for f, hr in zip(fulls, self.hrs)]


def reduce_scatter_grads(gs):
    n = len(gs)
    x, y, c = _place()
    scal = jnp.stack([2 * x + y, c]).astype(jnp.int32)
    hrs = [g.shape[1] // 2 for g in gs]

    def swap_body(*refs):
        ins, outs = refs[:n], refs[n:2 * n]
        send, recv = refs[2 * n:]
        xx, yy, cc = _place()
        cps = []
        for a in range(n):
            cp = pltpu.make_async_remote_copy(src_ref=ins[a].at[:, pl.ds((1 - cc) * hrs[a], hrs[a])], dst_ref=outs[a],
                                              send_sem=send.at[a], recv_sem=recv.at[a],
                                              device_id=(xx, yy, 1 - cc), device_id_type=MESH)
            cp.start()
            cps.append(cp)
        for cp in cps:
            cp.wait()

    ras = _comm_call(swap_body, "rs_swap_halves", n, [_sds((N_SH, hrs[a], gs[a].shape[2]), F32) for a in range(n)],
                     [pltpu.SemaphoreType.DMA((n,)), pltpu.SemaphoreType.DMA((n,))])(*gs)

    parts = [_rs_add_sibling(scal, gs[a], ras[a], hrs[a]) for a in range(n)]

    def ici_body(*refs):
        ins, outs = refs[:n], refs[n:2 * n]
        send, recv = refs[2 * n:]
        xx, yy, cc = _place()
        chips = [(1 - xx, yy), (xx, 1 - yy), (1 - xx, 1 - yy)]
        cps = []
        for a in range(n):
            for j, chip in enumerate(chips):
                cp = pltpu.make_async_remote_copy(src_ref=ins[a].at[2 * chip[0] + chip[1]], dst_ref=outs[a].at[j],
                                                  send_sem=send.at[3 * a + j], recv_sem=recv.at[3 * a + j],
                                                  device_id=(chip[0], chip[1], cc), device_id_type=MESH)
                cp.start()
                cps.append(cp)
        for cp in cps:
            cp.wait()

    rbs = _comm_call(ici_body, "rs_send_partials", n, [_sds((3, hrs[a], gs[a].shape[2]), BF16) for a in range(n)],
                     [pltpu.SemaphoreType.DMA((3 * n,)), pltpu.SemaphoreType.DMA((3 * n,))])(*[p[1] for p in parts])

    fins = [_rs_add_chips(scal, parts[a][0], rbs[a], hrs[a]) for a in range(n)]

    def share_body(*refs):
        ins, outs = refs[:n], refs[n:2 * n]
        send, recv = refs[2 * n:]
        xx, yy, cc = _place()
        sib = (xx, yy, 1 - cc)
        cps = []
        for a in range(n):
            cp = pltpu.make_async_remote_copy(src_ref=ins[a].at[cc], dst_ref=outs[a].at[cc], send_sem=send.at[a],
                                              recv_sem=recv.at[a], device_id=sib, device_id_type=MESH)
            cp.start()
            cps.append(cp)
        for a in range(n):
            pltpu.make_async_remote_copy(src_ref=ins[a].at[cc], dst_ref=outs[a].at[1 - cc], send_sem=send.at[a],
                                         recv_sem=recv.at[a], device_id=sib, device_id_type=MESH).wait_recv()
        for cp in cps:
            cp.wait_send()

    fulls = _comm_call(share_body, "rs_share_halves", n, [_sds(f.shape, F32) for f in fins],
                       [pltpu.SemaphoreType.DMA((n,)), pltpu.SemaphoreType.DMA((n,))],
                       aliases={a: a for a in range(n)})(*fins)
    return [f.reshape(2 * hr, f.shape[2]) for f, hr in zip(fulls, hrs)]


def adamw(w, g, m, v):
    shape = w.shape
    cols = shape[-1]
    rows = math.prod(shape[:-1]) if len(shape) > 1 else 1
    tr = 256 if rows % 256 == 0 and rows > 256 else rows
    c1 = 1.0 - ADAM_B1 ** ADAM_STEP
    c2 = 1.0 - ADAM_B2 ** ADAM_STEP

    def body(w_ref, g_ref, m_ref, v_ref, d_ref, nm_ref, nv_ref):
        gv = g_ref[...]
        nm = ADAM_B1 * m_ref[...] + (1.0 - ADAM_B1) * gv
        nv = ADAM_B2 * v_ref[...] + (1.0 - ADAM_B2) * (gv * gv)
        nm_ref[...] = nm
        nv_ref[...] = nv
        d_ref[...] = -ADAM_LR * ((nm / c1) / (jnp.sqrt(nv / c2) + ADAM_EPS) + ADAM_WD * w_ref[...])

    row = bs((tr, cols), lambda i: (i, 0))
    outs = _call(body, "adamw", (rows // tr,), [row] * 4, [row] * 3, [_sds((rows, cols), F32)] * 3)(
        *[a.reshape(rows, cols) for a in (w, g, m, v)])
    return [o.reshape(shape) for o in outs]


def adamw_layers(w, gs, m, v, lo, into=None, after=None):
    shape = w.shape
    cols = shape[-1]
    rl = math.prod(shape[1:-1])
    tr = max(t_ for t_ in range(8, rl + 1, 8) if rl % t_ == 0 and t_ * cols * 4 <= 1024 * 1024)
    nb = rl // tr
    n = len(gs)
    c1 = 1.0 - ADAM_B1 ** ADAM_STEP
    c2 = 1.0 - ADAM_B2 ** ADAM_STEP

    def body(*refs):
        w_ref, m_ref, v_ref = refs[:3]
        g_refs = refs[3:3 + n]
        d_ref, nm_ref, nv_ref, go_ref = refs[-4:]
        layer = pl.program_id(0) // nb
        for k in range(n):
            @pl.when(layer == k)
            def _(k=k):
                gv = g_refs[k][...]
                nm = ADAM_B1 * m_ref[...] + (1.0 - ADAM_B1) * gv
                nv = ADAM_B2 * v_ref[...] + (1.0 - ADAM_B2) * (gv * gv)
                nm_ref[...] = nm
                nv_ref[...] = nv
                go_ref[...] = gv
                d_ref[...] = -ADAM_LR * ((nm / c1) / (jnp.sqrt(nv / c2) + ADAM_EPS) + ADAM_WD * w_ref[...])

    row = bs((tr, cols), lambda b: (lo * nb + b, 0))
    g_specs = [bs((tr, cols), functools.partial(lambda b, k: (jnp.clip(b - k * nb, 0, nb - 1), 0), k=k)) for k in range(n)]
    flat = lambda a: a.reshape(-1, cols)
    in_specs = [row] * 3 + g_specs
    args = [flat(w), flat(m), flat(v)] + [flat(g) for g in gs]
    aliases = None
    if into is not None:
        aliases = {len(in_specs) + k: k for k in range(4)}
        in_specs = in_specs + [ANY] * 4
        args = args + [flat(a) for a in into]
    outs = _call(body, "adamw_layers", (n * nb,), in_specs, [row] * 4, [_sds((shape[0] * rl, cols), F32)] * 4,
                 aliases=aliases, after=after)(*args)
    return [o.reshape(shape) for o in outs]


def allreduce8_split(vec):
    rows, cols = vec.shape

    def plan(refs):
        x, y, c = _place()
        me = 4 * x + 2 * y + c
        out = []
        for j in range(1, 8):
            px, py, pc = (1 - x if j & 4 else x), (1 - y if j & 2 else y), (1 - c if j & 1 else c)
            out.append((refs[0], refs[1].at[me], refs[1].at[4 * px + 2 * py + pc], (px, py, pc)))
        return out

    ex = SplitExchange("allreduce8", [vec, lax.empty((8, rows, cols), F32)], plan, 7)

    def finish(after):
        v, land = ex.wait(after)
        x, y, c = _place()
        me = jnp.reshape(4 * x + 2 * y + c, (1,)).astype(jnp.int32)

        def body(me_ref, v_ref, l_ref, o_ref):
            o_ref[...] = jnp.zeros_like(o_ref)
            for k in range(8):
                @pl.when(me_ref[0] == k)
                def _():
                    o_ref[...] += v_ref[...]

                @pl.when(me_ref[0] != k)
                def _(k=k):
                    o_ref[...] += l_ref[k]

        spec = pltpu.PrefetchScalarGridSpec(
            num_scalar_prefetch=1, grid=(1,),
            in_specs=[bs((rows, cols), lambda i, mr: (0, 0)), bs((8, rows, cols), lambda i, mr: (0, 0, 0))],
            out_specs=bs((rows, cols), lambda i, mr: (0, 0)))
        return pl.pallas_call(body, name="allreduce8_sum", grid_spec=spec, out_shape=_sds((rows, cols), F32),
                              compiler_params=pltpu.CompilerParams(dimension_semantics=("arbitrary",),
                                                                   vmem_limit_bytes=VMEM_LIMIT))(me, v, land)

    return ex, finish


class Hooks:
    def __init__(self):
        self.steps = {}

    def add(self, point, fn):
        self.steps.setdefault(point, []).append(fn)

    def run(self, point, arr, env=None):
        tok = None
        for fn in self.steps.get(point, ()):
            got = fn(arr if tok is None else tok, env)
            tok = tok if got is None else got
        return tok


def layer_fwd(x, p_i, w, hooks):
    h, proj, *qkv = norm_in_proj(x, w["g_mix"], w["win"], after=hooks.run("start", x))
    ya = conva_fwd(proj, w["conv_a"])
    yb, o32, lse = attn_merge([attn_fwd_group(pv, d) for pv, d in zip(qkv, DILATIONS)])
    yc = sgu_fwd(proj, w["sgu_ln_g"], w["sgu_ln_b"], w["sgu_wt"], w["sgu_bf"])
    yd, z = conf_fwd(proj, w["conf_dw"], w["conf_ln_g"], w["conf_ln_b"], after=hooks.run("pre_conf", [ya, yb, yc]))
    ys = (ya, yb, yc, yd)
    tok = hooks.run("pre_merge", yd)
    merged, gates, ybr = merge_fwd(h, ys, w["wg"], w["wbr"], after=tok)
    x1, h2, fgu, act = ffn_in(merged, w["wout"], x, w["g_ffn"], w["wfi"], after=hooks.run("post_merge", merged))
    x2, h3, gate, pp, x3 = ple_fwd(act, w["wfo"], x1, w["g_ple"], w["wpg"], p_i, w["wpp"],
                                   after=hooks.run("post_ffn_in", act))
    saved = dict(x=x, h=h, proj=proj, qkv=qkv, ys=ys, o32=o32, lse=lse, z=z, merged=merged, gates=gates, ybr=ybr, x1=x1,
                 h2=h2, fgu=fgu, act=act, x2=x2, h3=h3, gate=gate, pp=pp)
    return x3, saved


def layer_bwd(dx3, p_i, w, s, hooks):
    t = dx3.shape[0]
    tr = min(1024, t)
    nr = t // tr
    ns_fi = FFN_H // 2
    small = {}

    dpre, dpp, dx2, small["g_ple"] = ple_bwd(dx3, s["gate"], s["pp"], w["wpg"], s["x2"], w["g_ple"],
                                             after=hooks.run("start", dx3))
    ga_shape, gb_shape = _sds((N_SH, 6 * BW, D_MODEL), F32), _sds((N_SH, 5 * BW, BW), F32)
    ga_blk = lambda idx: bs((N_SH, BW, D_MODEL), idx)
    ga = tn_matmul("dw_ple_gate", s["h3"], dpre, (nr,), bs((tr, D_MODEL), lambda r: (r, 0)),
                   bs((tr, D_MODEL), lambda r: (r, 0)), ga_blk(lambda r: (0, 5, 0)), ga_shape, split=N_SH)
    gb = tn_matmul("dw_ple_proj", p_i, dpp, (N_SH, nr), bs((tr, BW), lambda j, r: (r, 0)),
                   bs((tr, BW), lambda j, r: (r, j)), bs((None, BW, BW), lambda j, r: (j, 4, 0)), gb_shape)

    df, dx1, small["g_ffn"] = ffn_bwd(dx2, w["wfo"], s["fgu"], w["wfi"], s["x1"], w["g_ffn"],
                                      after=hooks.run("pre_ffn", dx2))
    gfo = tn_matmul("dw_ffn_out", s["act"], dx2, (2, nr), bs((tr, ns_fi), lambda j, r: (r, j)),
                    bs((tr, D_MODEL), lambda j, r: (r, 0)), bs((2, FFN_H // N_SH, D_MODEL), lambda j, r: (j, 0, 0)),
                    _sds((N_SH, FFN_H // N_SH, D_MODEL), F32), split=2)
    gfi = tn_matmul("dw_ffn_in", s["h2"], df, (N_SH, nr), bs((tr, D_MODEL), lambda j, r: (r, 0)),
                    bs((None, tr, ns_fi), lambda j, r: (j // 2, r, j % 2)),
                    bs((None, D_MODEL, ns_fi), lambda j, r: (j, 0, 0)), _sds((N_SH, D_MODEL, ns_fi), F32))

    dpre_m, dyb, dys = merge_bwd(dx1, w["wout"], s["gates"], s["ybr"], w["wbr"])
    ga = tn_matmul("dw_out", s["merged"], dx1, (nr,), bs((tr, D_MODEL), lambda r: (r, 0)),
                   bs((tr, D_MODEL), lambda r: (r, 0)), ga_blk(lambda r: (0, 4, 0)), ga_shape, split=N_SH, into=ga,
                   after=hooks.run("pre_dw_out", dyb, dict(gfo=gfo, gfi=gfi)))
    ga = tn_matmul("dw_merge_gate", s["h"], dpre_m, (N_BR, nr), bs((tr, D_MODEL), lambda k, r: (r, 0)),
                   bs((None, tr, D_MODEL), lambda k, r: (k, r, 0)), ga_blk(lambda k, r: (0, k, 0)), ga_shape,
                   split=N_SH, into=ga)
    for k in range(N_BR):
        gb = tn_matmul("dw_branch", s["ys"][k], dyb, (nr,), bs((tr, BW), lambda r: (r, 0)),
                       bs((None, tr, D_MODEL), functools.partial(lambda r, kk: (kk, r, 0), kk=k)),
                       bs((N_SH, BW, BW), functools.partial(lambda r, kk: (0, kk, 0), kk=k)), gb_shape,
                       split_cols=N_SH, into=gb)

    (dab, dac, dax), small["conv_a"] = conva_bwd(s["proj"], dys, w["conv_a"], after=hooks.run("pre_conva", gb))
    lds, dy_views = attn_delta(dys, s["o32"], s["lse"])
    dy_views = [dys[1]] + list(dy_views)
    dqkv = attn_bwd_finish([attn_bwd_group(pv, dov, ldv, d)
                            for pv, dov, ldv, d in zip(s["qkv"], dy_views, lds, DILATIONS)])
    du, dv, d_sw, d_sbf, small["sgu_ln_g"], small["sgu_ln_b"] = sgu_bwd(
        s["proj"], dys, w["sgu_ln_g"], w["sgu_ln_b"], w["sgu_wt"], w["sgu_bf"])
    small["sgu_w"] = jnp.where(jnp.tril(jnp.ones((BLK, BLK), bool))[None], d_sw, 0.0)
    small["sgu_b"] = jnp.sum(d_sbf.reshape(BLK, 4, HEAD_D), axis=-1).T
    dz, small["conf_ln_g"], small["conf_ln_b"] = conf_bwd_ln(s["z"], dys, w["conf_ln_g"], w["conf_ln_b"])
    dval, dgate, small["conf_dw"] = conf_bwd_conv(s["proj"], dz, w["conf_dw"])
    dproj = jnp.concatenate([dab, dac, dax, dqkv, du, dv, dval, dgate], axis=1)

    ns_in = N_IN // N_SH
    gin = tn_matmul("dw_in", s["h"], dproj, (N_SH, nr), bs((tr, D_MODEL), lambda j, r: (r, 0)),
                    bs((tr, ns_in), lambda j, r: (r, j)), bs((None, D_MODEL, ns_in), lambda j, r: (j, 0, 0)),
                    _sds((N_SH, D_MODEL, ns_in), F32), after=hooks.run("pre_dw_in", dproj))
    hooks.run("end", gin)
    big = [ga, gfo, gb, gin, gfi]
    dx, small["g_mix"] = norm_bwd(
        "mix_norm_bwd",
        [(dpre_m, lambda tm: bs((N_BR, tm, D_MODEL), lambda i: (0, i, 0)), w["wg"],
          lambda a, wr: [(a[k], wr[k]) for k in range(N_BR)]),
         (dproj, lambda tm: bs((tm, N_IN), lambda i: (i, 0)), w["win"],
          lambda a, wr: [(a[:, k * ns_in:(k + 1) * ns_in], wr[k]) for k in range(N_SH)])],
        dx1, s["x"], w["g_mix"])
    return dx, big, small


BIG_NAMES = ("w_in", "w_branch", "w_merge_gate", "w_out", "w_ffn_in", "w_ffn_out", "w_ple_gate", "w_ple_proj")


def unpack_big_grads(ga, gfo, gb, gin, gfi):
    return dict(w_in=gin, w_ffn_in=gfi, w_ffn_out=gfo,
                w_merge_gate=ga[:N_BR * BW].reshape(N_BR, BW, D_MODEL), w_out=ga[N_BR * BW:5 * BW], w_ple_gate=ga[5 * BW:],
                w_branch=gb[:N_BR * BW].reshape(N_BR, BW, BW), w_ple_proj=gb[N_BR * BW:])


SMALL_NAMES = ("g_mix", "conv_a", "sgu_ln_g", "sgu_ln_b", "sgu_w", "sgu_b", "conf_dw", "conf_ln_g", "conf_ln_b",
               "g_ffn", "g_ple")


def _pack_rows(arrays, rows):
    flat = jnp.concatenate([a.reshape(-1) for a in arrays])
    return jnp.pad(flat, (0, rows * D_MODEL - flat.shape[0])).reshape(rows, D_MODEL)


def _unpack_rows(packed, shapes):
    flat, out, pos = packed.reshape(-1), [], 0
    for shape in shapes:
        n = math.prod(shape)
        out.append(flat[pos:pos + n].reshape(shape))
        pos += n
    return out


def kernel(x, p, g_mix, w_in, conv_a, sgu_ln_g, sgu_ln_b, sgu_w, sgu_b, conf_dw, conf_ln_g, conf_ln_b, w_branch, w_merge_gate, w_out, g_ffn, w_ffn_in, w_ffn_out, g_ple, w_ple_gate, w_ple_proj, g_final, loss_target, m_g_mix, m_w_in, m_conv_a, m_sgu_ln_g, m_sgu_ln_b, m_sgu_w, m_sgu_b, m_conf_dw, m_conf_ln_g, m_conf_ln_b, m_w_branch, m_w_merge_gate, m_w_out, m_g_ffn, m_w_ffn_in, m_w_ffn_out, m_g_ple, m_w_ple_gate, m_w_ple_proj, m_g_final, v_g_mix, v_w_in, v_conv_a, v_sgu_ln_g, v_sgu_ln_b, v_sgu_w, v_sgu_b, v_conf_dw, v_conf_ln_g, v_conf_ln_b, v_w_branch, v_w_merge_gate, v_w_out, v_g_ffn, v_w_ffn_in, v_w_ffn_out, v_g_ple, v_w_ple_gate, v_w_ple_proj, v_g_final):
    weights = dict(g_mix=g_mix, w_in=w_in, conv_a=conv_a, sgu_ln_g=sgu_ln_g, sgu_ln_b=sgu_ln_b, sgu_w=sgu_w, sgu_b=sgu_b,
                   conf_dw=conf_dw, conf_ln_g=conf_ln_g, conf_ln_b=conf_ln_b, w_branch=w_branch, w_merge_gate=w_merge_gate,
                   w_out=w_out, g_ffn=g_ffn, w_ffn_in=w_ffn_in, w_ffn_out=w_ffn_out, g_ple=g_ple, w_ple_gate=w_ple_gate,
                   w_ple_proj=w_ple_proj, g_final=g_final)
    m_in = dict(g_mix=m_g_mix, w_in=m_w_in, conv_a=m_conv_a, sgu_ln_g=m_sgu_ln_g, sgu_ln_b=m_sgu_ln_b, sgu_w=m_sgu_w,
                sgu_b=m_sgu_b, conf_dw=m_conf_dw, conf_ln_g=m_conf_ln_g, conf_ln_b=m_conf_ln_b, w_branch=m_w_branch,
                w_merge_gate=m_w_merge_gate, w_out=m_w_out, g_ffn=m_g_ffn, w_ffn_in=m_w_ffn_in, w_ffn_out=m_w_ffn_out,
                g_ple=m_g_ple, w_ple_gate=m_w_ple_gate, w_ple_proj=m_w_ple_proj, g_final=m_g_final)
    v_in = dict(g_mix=v_g_mix, w_in=v_w_in, conv_a=v_conv_a, sgu_ln_g=v_sgu_ln_g, sgu_ln_b=v_sgu_ln_b, sgu_w=v_sgu_w,
                sgu_b=v_sgu_b, conf_dw=v_conf_dw, conf_ln_g=v_conf_ln_g, conf_ln_b=v_conf_ln_b, w_branch=v_w_branch,
                w_merge_gate=v_w_merge_gate, w_out=v_w_out, g_ffn=v_g_ffn, w_ffn_in=v_w_ffn_in, w_ffn_out=v_w_ffn_out,
                g_ple=v_g_ple, w_ple_gate=v_w_ple_gate, w_ple_proj=v_w_ple_proj, g_final=v_g_final)
    order = ("g_mix", "w_in", "conv_a", "sgu_ln_g", "sgu_ln_b", "sgu_w", "sgu_b", "conf_dw", "conf_ln_g", "conf_ln_b",
             "w_branch", "w_merge_gate", "w_out", "g_ffn", "w_ffn_in", "w_ffn_out", "g_ple", "w_ple_gate", "w_ple_proj",
             "g_final")
    depth = g_mix.shape[0]
    xs, tgt = x[0], loss_target[0]
    cw = BW // N_SH
    my_shard = 2 * lax.axis_index("x") + lax.axis_index("y")

    conv_rows = 16
    allc = gather8(_pack_rows([conv_a, conf_dw], conv_rows), reduce=False)
    shards = [_unpack_rows(allc[2 * s], [conv_a.shape, conf_dw.shape]) for s in range(N_SH)]
    conv_a_full = jnp.concatenate([sh[0] for sh in shards], axis=-1)
    conf_dw_full = jnp.concatenate([sh[1] for sh in shards], axis=-1)

    tril = jnp.tril(jnp.ones((BLK, BLK), bool))
    def placed_shards(i):
        shards = ([w_in[i], w_branch[i]] + [w_merge_gate[i, k] for k in range(N_BR)]
                  + [w_out[i], w_ffn_in[i], w_ffn_out[i], w_ple_gate[i], w_ple_proj[i]])
        return [lax.dynamic_update_slice(lax.empty((N_SH,) + sh.shape, BF16), sh.astype(BF16)[None],
                                         (my_shard,) + (0,) * sh.ndim) for sh in shards]

    def small_weights(i, win):
        vec = lambda a: a[i].reshape(1, -1)
        return dict(
            win=win, g_mix=vec(g_mix), g_ffn=vec(g_ffn), g_ple=vec(g_ple), conv_a=conv_a_full[i], conf_dw=conf_dw_full[i],
            sgu_ln_g=vec(sgu_ln_g), sgu_ln_b=vec(sgu_ln_b), conf_ln_g=vec(conf_ln_g), conf_ln_b=vec(conf_ln_b),
            sgu_wt=jnp.where(tril[None], sgu_w[i], 0.0).astype(BF16),
            sgu_bf=jnp.repeat(sgu_b[i].T, HEAD_D, axis=1))

    def late_weights(got):
        return dict(wbr=got[0], wg=jnp.stack([g.reshape(D_MODEL, D_MODEL) for g in got[1:5]]),
                    wout=got[5].reshape(D_MODEL, D_MODEL), wfi=got[6], wfo=got[7].reshape(FFN_H, D_MODEL),
                    wpg=got[8].reshape(D_MODEL, D_MODEL), wpp=got[9])

    class SplitAllGather:
        def __init__(self, bufs):
            self.shapes = [b.shape for b in bufs]
            self.ici = SplitExchange("allgather_ici", bufs, allgather_ici_plan(self.shapes), 3 * len(bufs))

        def ici_start(self, after, env=None):
            return self.ici.start(after)

        def ici_wait_d2d_start(self, after, env=None):
            landed = self.ici.wait(after)
            self.d2d = SplitExchange("allgather_d2d", landed, allgather_d2d_plan(self.shapes), 3 * len(landed))
            return self.d2d.start(landed[-1])

        def d2d_wait(self, after, env=None):
            self.got = self.d2d.wait(after)
            return None

    bufs0 = placed_shards(0)
    first = SplitAllGather(bufs0[:1])
    first.d2d_wait(first.ici_wait_d2d_start(first.ici_start(xs)))
    rest = SplitAllGather(bufs0[1:])
    layers = [small_weights(0, first.got[0])]
    act, saved = xs, []
    nxt_done = None
    for i in range(depth):
        hooks = Hooks()
        if i == 0:
            hooks.add("start", rest.ici_start)
            hooks.add("pre_conf", rest.ici_wait_d2d_start)
            hooks.add("pre_merge", rest.d2d_wait)
            hooks.add("pre_merge", lambda after, env: layers[0].update(late_weights(rest.got)))
        if i + 1 < depth:
            nxt = SplitAllGather(placed_shards(i + 1))
            points = ("pre_merge", "post_ffn_in", None) if i == 0 else ("start", "post_merge", "post_ffn_in")
            hooks.add(points[0], nxt.ici_start)
            hooks.add(points[1], nxt.ici_wait_d2d_start)
            if points[2]:
                hooks.add(points[2], nxt.d2d_wait)
        act, sv = layer_fwd(act, p[i, 0], layers[i], hooks)
        saved.append(sv)
        if i + 1 < depth:
            if i == 0:
                nxt.d2d_wait(act)
            layers.append({**small_weights(i + 1, nxt.got[0]), **late_weights(nxt.got[1:])})
    loss_part, dx, dg_final = loss_head(act, g_final.reshape(1, -1), tgt)

    big_red = [None] * depth
    small_red = [None] * depth
    small_rows = 80
    pending = None

    def small_vector(i, small):
        parts = [small[n] for n in SMALL_NAMES]
        return _pack_rows(parts + ([dg_final, loss_part[0, :1]] if i == 0 else []), small_rows)

    for i in reversed(range(depth)):
        hooks = Hooks()
        result = {}
        if pending is not None:
            rs, j, (small_ex, small_finish) = pending
            hooks.add("start", lambda after, env, ex=small_ex: ex.start(after))
            hooks.add("start", lambda after, env, rs=rs: rs.swap_start(after))
            hooks.add("pre_ffn", lambda after, env, rs=rs: rs.swap_wait_send_start(after))
            hooks.add("pre_dw_out", lambda after, env, rs=rs: rs.send_wait_share_start(after))
            hooks.add("pre_conva", lambda after, env, rs=rs, result=result: result.update(prev=rs.share_wait(after)))
            hooks.add("pre_conva", lambda after, env, fin=small_finish, result=result: result.update(small=fin(after)))
        if i == 0:
            def early_start(after, env, result=result):
                result["rs"] = SplitReduceScatter([env["gfo"], env["gfi"]])
                return result["rs"].swap_start(after)

            hooks.add("pre_dw_out", early_start)
            hooks.add("pre_conva", lambda after, env, result=result: result["rs"].swap_wait_send_start(after))
            hooks.add("pre_dw_in", lambda after, env, result=result: result["rs"].send_wait_share_start(after))
            hooks.add("end", lambda after, env, result=result: result.update(early=result["rs"].share_wait(after)))
        dx, big, small = layer_bwd(dx, p[i, 0], layers[i], saved[i], hooks)
        if pending is not None:
            big_red[pending[1]] = unpack_big_grads(*result["prev"])
            small_red[pending[1]] = result["small"]
        if i > 0:
            pending = (SplitReduceScatter(big), i, allreduce8_split(small_vector(i, small)))

    ga, _, gb, gin, _ = big
    late = SplitReduceScatter([ga, gb, gin])
    small_ex, small_finish = allreduce8_split(small_vector(0, small))
    upd = {}

    def update_upper(names, after):
        for name in names:
            upd[name] = adamw_layers(weights[name], [big_red[i][name] for i in range(1, depth)], m_in[name], v_in[name],
                                     1, after=after)
        return [upd[name][0] for name in names]

    done = update_upper(("w_in",), late.swap_start(small_ex.start(dx)))
    done = update_upper(("w_ffn_in", "w_merge_gate", "w_ffn_out"), late.swap_wait_send_start(done))
    small_red[0] = small_finish(done)
    done = update_upper(("w_branch", "w_out", "w_ple_gate", "w_ple_proj"), late.send_wait_share_start(done))
    ga, gb, gin = late.share_wait(done)
    gfo, gfi = result["early"]
    big_red[0] = unpack_big_grads(ga, gfo, gb, gin, gfi)

    layer_shapes = [small[n].shape for n in SMALL_NAMES]
    per_layer = [_unpack_rows(small_red[i], layer_shapes + ([dg_final.shape, (1,)] if i == 0 else []))
                 for i in range(depth)]
    grads = {n: jnp.stack([per_layer[i][k].reshape(weights[n].shape[1:] if n not in ("conv_a", "conf_dw")
                                                   else per_layer[i][k].shape) for i in range(depth)])
             for k, n in enumerate(SMALL_NAMES)}
    grads["g_final"] = per_layer[0][-2].reshape(-1)
    loss = per_layer[0][-1].reshape(())
    for n in ("conv_a", "conf_dw"):
        grads[n] = lax.dynamic_slice_in_dim(grads[n], my_shard * cw, cw, axis=2)

    small_all = [n for n in order if n not in BIG_NAMES]
    sm_shapes = [weights[n].shape for n in small_all]
    n_sm = sum(math.prod(sh) for sh in sm_shapes)
    sm_rows = -(-n_sm // (8 * D_MODEL)) * 8
    packed = [_pack_rows([src[n] for n in small_all], sm_rows) for src in (weights, grads, m_in, v_in)]
    sm_out = [_unpack_rows(o, sm_shapes) for o in adamw(*packed)]
    delta, new_m, new_v = ({n: o[k] for k, n in enumerate(small_all)} for o in sm_out)
    for name in BIG_NAMES:
        delta[name], new_m[name], new_v[name], grads[name] = adamw_layers(
            weights[name], [big_red[0][name]], m_in[name], v_in[name], 0, into=upd[name])

    return (loss, dx[None], *[grads[n] for n in order], *[delta[n] for n in order], *[new_m[n] for n in order],
            *[new_v[n] for n in order])
```

```python
import functools
import math

import jax
import jax.numpy as jnp
from jax import lax
from jax.experimental import pallas as pl
from jax.experimental.pallas import tpu as pltpu

F32 = jnp.float32
BF16 = jnp.bfloat16
EPS = 1e-6
D_MODEL = 1024
BW = 256
N_BR = 4
N_IN = 10 * BW
FFN_H = 2816
N_SH = 4
HEADS = 4
HEAD_D = 64
BLK = 128
DILATIONS = (1, 4, 16)
CONF_K = 31
CONVA_K = 3
NEG = -1e30
VMEM_LIMIT = 56 * 1024 * 1024
MESH = pl.DeviceIdType.MESH

ADAM_LR, ADAM_B1, ADAM_B2, ADAM_EPS, ADAM_WD, ADAM_STEP = 0.001, 0.9, 0.999, 1e-08, 0.01, 10

bs = pl.BlockSpec
ANY = pl.BlockSpec(memory_space=pl.ANY)


def _call(body, name, grid, in_specs, out_specs, out_shape, scratch=(), aliases=None, after=None):
    n_in = len(in_specs)
    kernel_body = body
    if after is not None:
        in_specs = list(in_specs) + [ANY]

        def kernel_body(*refs):
            return body(*refs[:n_in], *refs[n_in + 1:])

    call = pl.pallas_call(
        kernel_body, name=name, grid=grid, in_specs=in_specs, out_specs=out_specs, out_shape=out_shape,
        scratch_shapes=list(scratch), input_output_aliases=aliases or {},
        compiler_params=pltpu.CompilerParams(dimension_semantics=("arbitrary",) * len(grid),
                                             vmem_limit_bytes=VMEM_LIMIT))
    return call if after is None else (lambda *args: call(*args, after))


def _sds(shape, dtype):
    return jax.ShapeDtypeStruct(shape, dtype)


def _nn(a, b):
    return jnp.dot(a, b, preferred_element_type=F32)


def _nt(a, b):
    return lax.dot_general(a, b, (((1,), (1,)), ((), ())), preferred_element_type=F32)


def _tn(a, b):
    return lax.dot_general(a, b, (((0,), (0,)), ((), ())), preferred_element_type=F32)


def _sigmoid(x):
    return 1.0 / (1.0 + jnp.exp(-x))


def _rms_fwd(x, g):
    r = lax.rsqrt(jnp.mean(x * x, axis=-1, keepdims=True) + EPS)
    return x * r * g


def _rms_bwd(dh, x, g):
    r = lax.rsqrt(jnp.mean(x * x, axis=-1, keepdims=True) + EPS)
    xr = x * r
    dxr = dh * g
    dx = r * (dxr - xr * jnp.mean(dxr * xr, axis=-1, keepdims=True))
    return dx, dh * xr


def _ln_hat(x):
    mu = jnp.mean(x, axis=-1, keepdims=True)
    xc = x - mu
    r = lax.rsqrt(jnp.mean(xc * xc, axis=-1, keepdims=True) + EPS)
    return xc * r, r


def _ln_bwd(dy, xhat, r, g):
    dxh = dy * g
    return r * (dxh - jnp.mean(dxh, axis=-1, keepdims=True) - xhat * jnp.mean(dxh * xhat, axis=-1, keepdims=True))


def _colsum(v):
    return jnp.sum(v, axis=0, keepdims=True)


def _causal_conv(zext, w_ref, k_taps, halo):
    acc = zext[halo:] * w_ref[k_taps - 1:k_taps, :]
    for k in range(k_taps - 1):
        acc = acc + pltpu.roll(zext, k_taps - 1 - k, 0)[halo:] * w_ref[k:k + 1, :]
    return acc


def _anti_conv(dext, w_ref, k_taps, tm):
    n = dext.shape[0]
    acc = dext[:tm] * w_ref[k_taps - 1:k_taps, :]
    for s in range(1, k_taps):
        acc = acc + pltpu.roll(dext, n - s, 0)[:tm] * w_ref[k_taps - 1 - s:k_taps - s, :]
    return acc


def _conv_wgrad(dw_ref, dc, zext, k_taps, halo):
    dw_ref[k_taps - 1:k_taps, :] += _colsum(dc * zext[halo:])
    for k in range(k_taps - 1):
        dw_ref[k:k + 1, :] += _colsum(dc * pltpu.roll(zext, k_taps - 1 - k, 0)[halo:])


LANES = 128


def _to_strided_view(dst_ref, chunk, scr, d, width):
    n = scr.shape[0] // d
    for c in range(width // LANES):
        scr[...] = chunk(c)
        for r in range(d):
            dst_ref[:, r * width + c * LANES:r * width + (c + 1) * LANES] = scr[pl.ds(r, n, stride=d), :].astype(dst_ref.dtype)


def _from_strided_view(src_ref, scr, d, width, c):
    n = scr.shape[0] // d
    for r in range(d):
        scr[pl.ds(r, n, stride=d), :] = src_ref[:, r * width + c * LANES:r * width + (c + 1) * LANES].astype(F32)
    return scr[...]


def _view_spec(tm, d, width):
    return bs((tm // d, d * width), lambda i: (i, 0))


def _prev_blk(i, per):
    return jnp.maximum(i * per - 1, 0)


def _next_blk(i, per, last):
    return jnp.minimum((i + 1) * per, last)


def norm_in_proj(x, g, win, after=None):
    t = x.shape[0]
    tm = min(512, t)
    ns = win.shape[2]

    def body(x_ref, g_ref, w_ref, h_ref, o_ref, q_ref, q4_ref, q16_ref, scr):
        h = _rms_fwd(x_ref[...], g_ref[...]).astype(BF16)
        h_ref[...] = h
        parts = []
        for s in range(N_SH):
            r = _nn(h, w_ref[s])
            o_ref[:, s * ns:(s + 1) * ns] = r
            if s == 1:
                parts.append(r[:, 3 * BW - ns:])
            if s == 2:
                parts.append(r[:, :6 * BW - 2 * ns])
        qf = jnp.concatenate(parts, axis=1)
        q_ref[...] = qf.astype(BF16)
        chunk = lambda c: qf[:, c * LANES:(c + 1) * LANES]
        _to_strided_view(q4_ref, chunk, scr, 4, 3 * BW)
        _to_strided_view(q16_ref, chunk, scr, 16, 3 * BW)

    row = lambda c: bs((tm, c), lambda i: (i, 0))
    return _call(
        body, "norm_in_proj", (t // tm,), [row(D_MODEL), bs((1, D_MODEL), lambda i: (0, 0)), _resident(win)],
        [row(D_MODEL), row(N_IN), row(3 * BW), _view_spec(tm, 4, 3 * BW), _view_spec(tm, 16, 3 * BW)],
        [_sds((t, D_MODEL), BF16), _sds((t, N_IN), F32), _sds((t, 3 * BW), BF16),
         _sds((t // 4, 4 * 3 * BW), BF16), _sds((t // 16, 16 * 3 * BW), BF16)],
        scratch=[pltpu.VMEM((tm, LANES), F32)], after=after)(x, g, win)


def merge_fwd(h, ys, wg, wbr, after=None):
    t = h.shape[0]
    tm = min(512, t)

    def body(h_ref, ya, yb, yc, yd, wg_ref, wb_ref, m_ref, g_ref, b_ref):
        hh = h_ref[...]
        for j in range(N_SH):
            cs = slice(j * BW, (j + 1) * BW)
            acc = None
            for k, y_ref in enumerate((ya, yb, yc, yd)):
                g = _sigmoid(_nn(hh, wg_ref[k, :, cs]))
                b = _nn(y_ref[...], wb_ref[j, k])
                g_ref[k, :, cs] = g.astype(BF16)
                b_ref[k, :, cs] = b.astype(BF16)
                acc = g * b if acc is None else acc + g * b
            m_ref[:, cs] = acc.astype(BF16)

    ysp = bs((tm, BW), lambda i: (i, 0))
    big = bs((N_BR, tm, D_MODEL), lambda i: (0, i, 0))
    return _call(
        body, "merge_fwd", (t // tm,),
        [bs((tm, D_MODEL), lambda i: (i, 0)), ysp, ysp, ysp, ysp, _resident(wg), _resident(wbr)],
        [bs((tm, D_MODEL), lambda i: (i, 0)), big, big],
        [_sds((t, D_MODEL), BF16), _sds((N_BR, t, D_MODEL), BF16), _sds((N_BR, t, D_MODEL), BF16)], after=after)(
            h, *ys, wg, wbr)


def ffn_in(a, wout, res, g, wfi, after=None):
    t = res.shape[0]
    tm = min(256, t)
    ns = wfi.shape[2]

    def body(a_ref, wo_ref, r_ref, g_ref, w_ref, x_ref, h_ref, f_ref, act_ref):
        xv = r_ref[...] + _nn(a_ref[...], wo_ref[...])
        x_ref[...] = xv
        h = _rms_fwd(xv, g_ref[...]).astype(BF16)
        h_ref[...] = h
        for j in range(2):
            cs = slice(j * ns, (j + 1) * ns)
            fg = _nn(h, w_ref[j])
            fu = _nn(h, w_ref[j + 2])
            f_ref[0, :, cs] = fg.astype(BF16)
            f_ref[1, :, cs] = fu.astype(BF16)
            act_ref[:, cs] = (fg * _sigmoid(fg) * fu).astype(BF16)

    row = lambda c: bs((tm, c), lambda i: (i, 0))
    return _call(
        body, "ffn_in", (t // tm,),
        [row(D_MODEL), _resident(wout), row(D_MODEL), bs((1, D_MODEL), lambda i: (0, 0)), _resident(wfi)],
        [row(D_MODEL), row(D_MODEL), bs((2, tm, FFN_H), lambda i: (0, i, 0)), row(FFN_H)],
        [_sds((t, D_MODEL), F32), _sds((t, D_MODEL), BF16), _sds((2, t, FFN_H), BF16), _sds((t, FFN_H), BF16)],
        after=after)(a, wout, res, g, wfi)


def ple_fwd(a, wfo, res, g, wpg, p_i, wpp, after=None):
    t = res.shape[0]
    tm = min(512, t)

    def body(a_ref, wo_ref, r_ref, g_ref, wg_ref, p_ref, wp_ref, x_ref, h_ref, gt_ref, pp_ref, o_ref):
        xv = r_ref[...] + _nn(a_ref[...], wo_ref[...])
        x_ref[...] = xv
        h = _rms_fwd(xv, g_ref[...]).astype(BF16)
        h_ref[...] = h
        gate = _sigmoid(_nn(h, wg_ref[...]))
        pb = p_ref[...].astype(BF16)
        pp = jnp.concatenate([_nn(pb, wp_ref[j]) for j in range(N_SH)], axis=1)
        gt_ref[...] = gate.astype(BF16)
        pp_ref[...] = pp.astype(BF16)
        o_ref[...] = xv + gate * pp

    row = bs((tm, D_MODEL), lambda i: (i, 0))
    return _call(
        body, "ple_fwd", (t // tm,),
        [bs((tm, FFN_H), lambda i: (i, 0)), _resident(wfo), row, bs((1, D_MODEL), lambda i: (0, 0)), _resident(wpg),
         bs((tm, BW), lambda i: (i, 0)), _resident(wpp)],
        [row, row, row, row, row],
        [_sds((t, D_MODEL), F32), _sds((t, D_MODEL), BF16), _sds((t, D_MODEL), BF16), _sds((t, D_MODEL), BF16),
         _sds((t, D_MODEL), F32)], after=after)(a, wfo, res, g, wpg, p_i, wpp)


def loss_head(x, g, tgt):
    t = x.shape[0]
    tm = min(512, t)

    def body(x_ref, g_ref, t_ref, l_ref, dx_ref, dg_ref):
        @pl.when(pl.program_id(0) == 0)
        def _():
            l_ref[...] = jnp.zeros_like(l_ref)
            dg_ref[...] = jnp.zeros_like(dg_ref)

        xv, gv = x_ref[...], g_ref[...]
        err = _rms_fwd(xv, gv) - t_ref[...]
        part = 0.5 * jnp.sum(jnp.mean(err * err, axis=-1, keepdims=True), axis=0, keepdims=True)
        l_ref[...] += jnp.broadcast_to(part, l_ref.shape)
        dx, dgr = _rms_bwd(err * (1.0 / D_MODEL), xv, gv)
        dx_ref[...] = dx
        dg_ref[...] += _colsum(dgr)

    row = bs((tm, D_MODEL), lambda i: (i, 0))
    vec = bs((1, D_MODEL), lambda i: (0, 0))
    return _call(body, "loss_head", (t // tm,), [row, vec, row],
                 [bs((1, 128), lambda i: (0, 0)), row, vec],
                 [_sds((1, 128), F32), _sds((t, D_MODEL), F32), _sds((1, D_MODEL), F32)])(x, g, tgt)


def tn_matmul(name, a, b, grid, a_spec, b_spec, out_spec, out_shape, split=0, split_cols=0, into=None, after=None):
    last = len(grid) - 1

    def body(a_ref, b_ref, *rest):
        o_ref = rest[-1]

        @pl.when(pl.program_id(last) == 0)
        def _():
            o_ref[...] = jnp.zeros_like(o_ref)

        res = _tn(a_ref[...].astype(BF16), b_ref[...].astype(BF16))
        if split_cols:
            cols = res.shape[1] // split_cols
            for s in range(split_cols):
                o_ref[s] += res[:, s * cols:(s + 1) * cols]
        elif split:
            rows = res.shape[0] // split
            for s in range(split):
                o_ref[s] += res[s * rows:(s + 1) * rows]
        else:
            o_ref[...] += res

    if into is None:
        return _call(body, name, grid, [a_spec, b_spec], out_spec, out_shape, after=after)(a, b)
    return _call(body, name, grid, [a_spec, b_spec, ANY], out_spec, out_shape, aliases={2: 0}, after=after)(a, b, into)


def _resident(w):
    zeros = (0,) * w.ndim
    return bs(w.shape, lambda i: zeros, pipeline_mode=pl.Buffered(1))


def norm_bwd(name, sources, dx_in, x, g):
    t = x.shape[0]
    tm = min(512, t)
    n_src = len(sources)

    def body(*refs):
        dxi_ref, x_ref, g_ref, dx_ref, dg_ref = refs[2 * n_src:]

        @pl.when(pl.program_id(0) == 0)
        def _():
            dg_ref[...] = jnp.zeros_like(dg_ref)

        dh = None
        for si in range(n_src):
            for av, wv in sources[si][3](refs[2 * si], refs[2 * si + 1]):
                part = _nt(av, wv)
                dh = part if dh is None else dh + part
        dx, dgr = _rms_bwd(dh, x_ref[...], g_ref[...])
        dx_ref[...] = dxi_ref[...] + dx
        dg_ref[...] += _colsum(dgr)

    in_specs, args = [], []
    for a, a_spec, w, _ in sources:
        in_specs += [a_spec(tm), _resident(w)]
        args += [a, w]
    row = bs((tm, D_MODEL), lambda i: (i, 0))
    vec = bs((1, D_MODEL), lambda i: (0, 0))
    return _call(body, name, (t // tm,), in_specs + [row, row, vec], [row, vec],
                 [_sds((t, D_MODEL), F32), _sds((1, D_MODEL), F32)])(*args, dx_in, x, g)


def ple_bwd(dx, gate, pp, wpg, x, g, after=None):
    t = dx.shape[0]
    tm = min(512, t)

    def body(dx_ref, gt_ref, p_ref, w_ref, x_ref, g_ref, dpre_ref, dpp_ref, o_ref, dg_ref):
        @pl.when(pl.program_id(0) == 0)
        def _():
            dg_ref[...] = jnp.zeros_like(dg_ref)

        d = dx_ref[...]
        gt = gt_ref[...].astype(F32)
        dpre = (d * p_ref[...].astype(F32) * gt * (1.0 - gt)).astype(BF16)
        dpre_ref[...] = dpre
        dpp_ref[...] = (d * gt).astype(BF16)
        dxn, dgr = _rms_bwd(_nt(dpre, w_ref[...]), x_ref[...], g_ref[...])
        o_ref[...] = d + dxn
        dg_ref[...] += _colsum(dgr)

    row = bs((tm, D_MODEL), lambda i: (i, 0))
    vec = bs((1, D_MODEL), lambda i: (0, 0))
    return _call(body, "ple_bwd", (t // tm,), [row, row, row, _resident(wpg), row, vec], [row, row, row, vec],
                 [_sds((t, D_MODEL), BF16), _sds((t, D_MODEL), BF16), _sds((t, D_MODEL), F32), _sds((1, D_MODEL), F32)],
                 after=after)(dx, gate, pp, wpg, x, g)


def ffn_bwd(dx, wfo, fgu, wfi, x, g, after=None):
    t = dx.shape[0]
    tm = min(256, t)
    ns = FFN_H // 2

    def body(dx_ref, wo_ref, f_ref, wi_ref, x_ref, g_ref, df_ref, o_ref, dg_ref):
        @pl.when(pl.program_id(0) == 0)
        def _():
            dg_ref[...] = jnp.zeros_like(dg_ref)

        d = dx_ref[...]
        dxb = d.astype(BF16)
        dh = None
        for j in range(2):
            cs = slice(j * ns, (j + 1) * ns)
            dact = _nt(dxb, wo_ref[cs, :])
            fg = f_ref[0, :, cs].astype(F32)
            fu = f_ref[1, :, cs].astype(F32)
            s = _sigmoid(fg)
            dfg = (dact * fu * (s * (1.0 + fg * (1.0 - s)))).astype(BF16)
            dfu = (dact * fg * s).astype(BF16)
            df_ref[0, :, cs] = dfg
            df_ref[1, :, cs] = dfu
            part = _nt(dfg, wi_ref[j]) + _nt(dfu, wi_ref[j + 2])
            dh = part if dh is None else dh + part
        dxn, dgr = _rms_bwd(dh, x_ref[...], g_ref[...])
        o_ref[...] = d + dxn
        dg_ref[...] += _colsum(dgr)

    blk = bs((2, tm, FFN_H), lambda i: (0, i, 0))
    row = bs((tm, D_MODEL), lambda i: (i, 0))
    vec = bs((1, D_MODEL), lambda i: (0, 0))
    return _call(body, "ffn_bwd", (t // tm,), [row, _resident(wfo), blk, _resident(wfi), row, vec], [blk, row, vec],
                 [_sds((2, t, FFN_H), BF16), _sds((t, D_MODEL), F32), _sds((1, D_MODEL), F32)],
                 after=after)(dx, wfo, fgu, wfi, x, g)


def merge_bwd(dx, wout, gates, ybr, wbr):
    t = dx.shape[0]
    tm = min(256, t)

    def body(dx_ref, w_ref, g_ref, b_ref, wb_ref, dpre_ref, dyb_ref, dy_ref):
        dm = _nt(dx_ref[...].astype(BF16), w_ref[...])
        for k in range(N_BR):
            g = g_ref[k].astype(F32)
            dpre_ref[k] = (dm * b_ref[k].astype(F32) * g * (1.0 - g)).astype(BF16)
            dyb = (dm * g).astype(BF16)
            dyb_ref[k] = dyb
            acc = None
            for s in range(N_SH):
                part = _nt(dyb[:, s * BW:(s + 1) * BW], wb_ref[s, k])
                acc = part if acc is None else acc + part
            dy_ref[k] = acc

    blk = bs((N_BR, tm, D_MODEL), lambda i: (0, i, 0))
    return _call(body, "merge_bwd", (t // tm,),
                 [bs((tm, D_MODEL), lambda i: (i, 0)), _resident(wout), blk, blk, _resident(wbr)],
                 [blk, blk, bs((N_BR, tm, BW), lambda i: (0, i, 0))],
                 [_sds((N_BR, t, D_MODEL), BF16), _sds((N_BR, t, D_MODEL), BF16), _sds((N_BR, t, BW), F32)])(
                     dx, wout, gates, ybr, wbr)


def conva_fwd(proj, wa):
    t = proj.shape[0]
    tm, halo = min(512, t), 8
    per = tm // halo

    def body(b_ref, c_ref, x_ref, ch_ref, xh_ref, w_ref, y_ref):
        zh = jnp.where(pl.program_id(0) > 0, ch_ref[...] * xh_ref[...], 0.0)
        zext = jnp.concatenate([zh, c_ref[...] * x_ref[...]], axis=0)
        y_ref[...] = (b_ref[...] * _causal_conv(zext, w_ref, CONVA_K, halo)).astype(BF16)

    col = lambda c: bs((tm, BW), lambda i: (i, c))
    hal = lambda c: bs((halo, BW), lambda i: (_prev_blk(i, per), c))
    return _call(body, "conva_fwd", (t // tm,),
                 [col(0), col(1), col(2), hal(1), hal(2), bs((CONVA_K, BW), lambda i: (0, 0))],
                 bs((tm, BW), lambda i: (i, 0)), _sds((t, BW), BF16))(proj, proj, proj, proj, proj, wa)


def conva_bwd(proj, dys, wa, after=None):
    t = proj.shape[0]
    tm, halo = min(512, t), 8
    per = tm // halo
    last = t // halo - 1
    nt = t // tm

    def body(b_ref, c_ref, x_ref, ch_ref, xh_ref, bn_ref, dy_ref, dyn_ref, w_ref, o_ref, dw_ref):
        i = pl.program_id(0)

        @pl.when(i == 0)
        def _():
            dw_ref[...] = jnp.zeros_like(dw_ref)

        zh = jnp.where(i > 0, ch_ref[...] * xh_ref[...], 0.0)
        cv, xv = c_ref[...], x_ref[...]
        zext = jnp.concatenate([zh, cv * xv], axis=0)
        dy = dy_ref[...]
        dconv = dy * b_ref[...]
        dcn = jnp.where(i < nt - 1, dyn_ref[...] * bn_ref[...], 0.0)
        dz = _anti_conv(jnp.concatenate([dconv, dcn], axis=0), w_ref, CONVA_K, tm)
        o_ref[:, :BW] = (dy * _causal_conv(zext, w_ref, CONVA_K, halo)).astype(BF16)
        o_ref[:, BW:2 * BW] = (dz * xv).astype(BF16)
        o_ref[:, 2 * BW:] = (dz * cv).astype(BF16)
        _conv_wgrad(dw_ref, dconv, zext, CONVA_K, halo)

    col = lambda c: bs((tm, BW), lambda i: (i, c))
    hal = lambda c: bs((halo, BW), lambda i: (_prev_blk(i, per), c))
    nxt = bs((halo, BW), lambda i: (_next_blk(i, per, last), 0))
    wsp = bs((CONVA_K, BW), lambda i: (0, 0))
    outs = _call(body, "conva_bwd", (t // tm,),
                 [col(0), col(1), col(2), hal(1), hal(2), nxt,
                  bs((None, tm, BW), lambda i: (0, i, 0)), bs((None, halo, BW), lambda i: (0, _next_blk(i, per, last), 0)), wsp],
                 [bs((tm, 3 * BW), lambda i: (i, 0)), wsp],
                 [_sds((t, N_IN), BF16), _sds((CONVA_K, BW), F32)], after=after)(
                     proj, proj, proj, proj, proj, proj, dys, dys, wa)
    return outs[0], outs[1]


def _head_masks():
    lane = lax.broadcasted_iota(jnp.int32, (1, BW), 1)
    return [(lane >= h * HEAD_D) & (lane < (h + 1) * HEAD_D) for h in range(HEADS)]


def _band_masks():
    qi = lax.broadcasted_iota(jnp.int32, (BLK, BLK), 0)
    ki = lax.broadcasted_iota(jnp.int32, (BLK, BLK), 1)
    return ki >= qi, ki <= qi


def attn_fwd_group(pv, d):
    rows = pv.shape[0]
    qb = min(512, rows)
    nb = qb // BLK
    scale = HEAD_D ** -0.5

    def body(q_ref, k_ref, v_ref, kh_ref, vh_ref, o_ref):
        n = pl.program_id(1)
        hm = _head_masks()
        m_prev, m_cur = _band_masks()
        for b in range(nb):
            rs = slice(b * BLK, (b + 1) * BLK)
            q = q_ref[rs, :]
            if b == 0:
                kp, vp = kh_ref[...], vh_ref[...]
                mp = m_prev & (n > 0)
            else:
                ps = slice((b - 1) * BLK, b * BLK)
                kp, vp = k_ref[ps, :], v_ref[ps, :]
                mp = m_prev
            qs = jnp.concatenate([jnp.where(hm[h], q, 0.0).astype(BF16) for h in range(HEADS)], axis=0)
            kcat = jnp.concatenate([kp, k_ref[rs, :]], axis=0)
            vcat = jnp.concatenate([vp, v_ref[rs, :]], axis=0)
            band = jnp.concatenate([mp, m_cur], axis=1)
            s = jnp.where(jnp.concatenate([band] * HEADS, axis=0), _nt(qs, kcat) * scale, NEG)
            m = jnp.max(s, axis=-1, keepdims=True)
            e = jnp.exp(s - m)
            l = jnp.sum(e, axis=-1, keepdims=True)
            of = _nn(e.astype(BF16), vcat) / l
            lse = m + jnp.log(l)
            o_acc = jnp.zeros((BLK, BW), F32)
            l_acc = jnp.zeros((BLK, BW), F32)
            for h in range(HEADS):
                hs = slice(h * BLK, (h + 1) * BLK)
                o_acc = jnp.where(hm[h], of[hs, :], o_acc)
                l_acc = jnp.where(hm[h], lse[hs, :], l_acc)
            o_ref[rs, :BW] = o_acc
            o_ref[rs, BW:] = l_acc

    per = qb // BLK
    main = lambda c: bs((qb, BW), lambda r, n: (n, r * 3 + c))
    hal = lambda c: bs((BLK, BW), lambda r, n: (_prev_blk(n, per), r * 3 + c))
    return _call(body, f"attn_fwd_d{d}", (d, rows // qb), [main(0), main(1), main(2), hal(1), hal(2)],
                 bs((qb, 2 * BW), lambda r, n: (n, r)), _sds((rows, d * 2 * BW), F32))(pv, pv, pv, pv, pv)


def attn_merge(ols):
    t = ols[0].shape[0]
    tm = min(512, t)
    width = 2 * BW

    def lse3(a, b, c):
        m = jnp.maximum(jnp.maximum(a, b), c)
        return m + jnp.log(jnp.exp(a - m) + jnp.exp(b - m) + jnp.exp(c - m))

    def body(g0, g1, g2, y_ref, o_ref, l_ref, scr, nat1, nat2):
        for src, nat, d in ((g1, nat1, DILATIONS[1]), (g2, nat2, DILATIONS[2])):
            for c in range(width // LANES):
                nat[:, c * LANES:(c + 1) * LANES] = _from_strided_view(src, scr, d, width, c)
        gs = [g0[...], nat1[...], nat2[...]]
        ls = [g[:, BW:] for g in gs]
        tot = lse3(*ls)
        o = (jnp.exp(ls[0] - tot) * gs[0][:, :BW] + jnp.exp(ls[1] - tot) * gs[1][:, :BW]
             + jnp.exp(ls[2] - tot) * gs[2][:, :BW])
        y_ref[...] = o.astype(BF16)
        o_ref[...] = o
        l_ref[...] = tot

    n = bs((tm, BW), lambda i: (i, 0))
    return _call(body, "attn_merge", (t // tm,),
                 [_view_spec(tm, 1, width), _view_spec(tm, DILATIONS[1], width), _view_spec(tm, DILATIONS[2], width)],
                 [n, n, n], [_sds((t, BW), BF16), _sds((t, BW), F32), _sds((t, BW), F32)],
                 scratch=[pltpu.VMEM((tm, LANES), F32), pltpu.VMEM((tm, width), F32), pltpu.VMEM((tm, width), F32)])(*ols)


def attn_delta(dys, o, lse):
    t = o.shape[0]
    tm = min(512, t)

    def body(d_ref, o_ref, l_ref, ld1, ld4, ld16, dy4, dy16, scr):
        hm = _head_masks()
        dy = d_ref[...]
        prod = dy * o_ref[...]
        delta = jnp.zeros_like(prod)
        for h in range(HEADS):
            delta = jnp.where(hm[h], jnp.sum(jnp.where(hm[h], prod, 0.0), axis=-1, keepdims=True), delta)
        ld = jnp.concatenate([l_ref[...], delta], axis=1)
        ld1[...] = ld
        for d, ld_v, dy_v in ((DILATIONS[1], ld4, dy4), (DILATIONS[2], ld16, dy16)):
            _to_strided_view(ld_v, lambda c: ld[:, c * LANES:(c + 1) * LANES], scr, d, 2 * BW)
            _to_strided_view(dy_v, lambda c: dy[:, c * LANES:(c + 1) * LANES], scr, d, BW)

    n = bs((tm, BW), lambda i: (i, 0))
    d4, d16 = DILATIONS[1], DILATIONS[2]
    outs = _call(body, "attn_delta", (t // tm,), [bs((None, tm, BW), lambda i: (1, i, 0)), n, n],
                 [_view_spec(tm, 1, 2 * BW), _view_spec(tm, d4, 2 * BW), _view_spec(tm, d16, 2 * BW),
                  _view_spec(tm, d4, BW), _view_spec(tm, d16, BW)],
                 [_sds((t, 2 * BW), F32), _sds((t // d4, d4 * 2 * BW), F32), _sds((t // d16, d16 * 2 * BW), F32),
                  _sds((t // d4, d4 * BW), F32), _sds((t // d16, d16 * BW), F32)],
                 scratch=[pltpu.VMEM((tm, LANES), F32)])(dys, o, lse)
    return outs[:3], outs[3:]


def attn_bwd_group(pv, dov, ldv, d):
    rows = pv.shape[0]
    qb = min(512, rows)
    nb = qb // BLK
    nsteps = rows // qb
    scale = HEAD_D ** -0.5

    def body(q_ref, qn_ref, k_ref, kh_ref, v_ref, vh_ref, do_ref, don_ref, ld_ref, ldn_ref, o_ref):
        n = pl.program_id(1)
        hm = _head_masks()
        m_prev, m_cur = _band_masks()
        has_prev, has_next = n > 0, n < nsteps - 1
        dq = [None] * nb
        dk = [jnp.zeros((BLK, BW), F32) for _ in range(nb)]
        dvv = [jnp.zeros((BLK, BW), F32) for _ in range(nb)]
        for qi in range(nb + 1):
            rs = slice(qi * BLK, (qi + 1) * BLK)
            ps = slice((qi - 1) * BLK, qi * BLK)
            if qi < nb:
                q, do, ldq = q_ref[rs, :], do_ref[rs, :], ld_ref[rs, :]
            else:
                q, do, ldq = qn_ref[...], don_ref[...], ldn_ref[...]
            kp, vp = (kh_ref[...], vh_ref[...]) if qi == 0 else (k_ref[ps, :], v_ref[ps, :])
            kc, vc = (k_ref[rs, :], v_ref[rs, :]) if qi < nb else (kp, vp)
            mp = m_prev & has_prev if qi == 0 else (m_prev & has_next if qi == nb else m_prev)
            mc = m_cur if qi < nb else jnp.zeros_like(m_cur)
            band = jnp.concatenate([jnp.concatenate([mp, mc], axis=1)] * HEADS, axis=0)
            qs = jnp.concatenate([jnp.where(hm[h], q, 0.0).astype(BF16) for h in range(HEADS)], axis=0)
            dos = jnp.concatenate([jnp.where(hm[h], do, 0.0).astype(BF16) for h in range(HEADS)], axis=0)
            kcat = jnp.concatenate([kp, kc], axis=0)
            vcat = jnp.concatenate([vp, vc], axis=0)
            col = lambda v, h: jnp.broadcast_to(jnp.max(jnp.where(hm[h], v, NEG), axis=-1, keepdims=True), (BLK, 2 * BLK))
            lcols = jnp.concatenate([col(ldq[:, :BW], h) for h in range(HEADS)], axis=0)
            dcols = jnp.concatenate([col(ldq[:, BW:], h) for h in range(HEADS)], axis=0)
            p = jnp.where(band, jnp.exp(_nt(qs, kcat) * scale - lcols), 0.0)
            ds = (p * (_nt(dos, vcat) - dcols) * scale).astype(BF16)
            if qi < nb:
                dqf = _nn(ds, kcat)
                acc_q = jnp.zeros((BLK, BW), F32)
                for h in range(HEADS):
                    acc_q = jnp.where(hm[h], dqf[h * BLK:(h + 1) * BLK, :], acc_q)
                dq[qi] = acc_q
            dkc = _tn(ds, qs)
            dvc = _tn(p.astype(BF16), dos)
            if qi >= 1:
                dk[qi - 1] = dk[qi - 1] + dkc[:BLK]
                dvv[qi - 1] = dvv[qi - 1] + dvc[:BLK]
            if qi < nb:
                dk[qi] = dk[qi] + dkc[BLK:]
                dvv[qi] = dvv[qi] + dvc[BLK:]
        for b in range(nb):
            rs = slice(b * BLK, (b + 1) * BLK)
            for c, val in enumerate((dq[b], dk[b], dvv[b])):
                cs = slice(c * BW, (c + 1) * BW)
                o_ref[rs, cs] = val

    per = qb // BLK
    last = rows // BLK - 1
    main = lambda c: bs((qb, BW), lambda r, n: (n, r * 3 + c))
    prv = lambda c: bs((BLK, BW), lambda r, n: (_prev_blk(n, per), r * 3 + c))
    nxt = lambda c: bs((BLK, BW), lambda r, n: (_next_blk(n, per, last), r * 3 + c))
    accs = bs((qb, 3 * BW), lambda r, n: (n, r))
    in_specs = [main(0), nxt(0), main(1), prv(1), main(2), prv(2),
                bs((qb, BW), lambda r, n: (n, r)), bs((BLK, BW), lambda r, n: (_next_blk(n, per, last), r)),
                bs((qb, 2 * BW), lambda r, n: (n, r)), bs((BLK, 2 * BW), lambda r, n: (_next_blk(n, per, last), r))]
    args = [pv, pv, pv, pv, pv, pv, dov, dov, ldv, ldv]
    return _call(body, f"attn_bwd_d{d}", (d, nsteps), in_specs, accs, _sds((rows, d * 3 * BW), F32))(*args)


def attn_bwd_finish(parts, into):
    t = parts[0].shape[0]
    tm = min(512, t)
    width = 3 * BW

    def body(g0, g1, g2, _, o_ref, scr):
        for c in range(width // LANES):
            cs = slice(c * LANES, (c + 1) * LANES)
            acc = g0[:, cs]
            acc = acc + _from_strided_view(g1, scr, DILATIONS[1], width, c)
            acc = acc + _from_strided_view(g2, scr, DILATIONS[2], width, c)
            o_ref[:, cs] = acc.astype(BF16)

    return _call(body, "attn_bwd_finish", (t // tm,),
                 [_view_spec(tm, 1, width), _view_spec(tm, DILATIONS[1], width), _view_spec(tm, DILATIONS[2], width), ANY],
                 bs((tm, width), lambda i: (i, 1)), _sds(into.shape, BF16),
                 scratch=[pltpu.VMEM((tm, LANES), F32)], aliases={3: 0})(*parts, into)


def _group_masks():
    lane = lax.broadcasted_iota(jnp.int32, (1, BW), 1)
    return [(lane >= g * HEAD_D) & (lane < (g + 1) * HEAD_D) for g in range(4)]


def sgu_fwd(proj, ln_g, ln_b, w_tril, b_full):
    t = proj.shape[0]
    tm = min(512, t)

    def body(u_ref, v_ref, g_ref, b_ref, w_ref, bf_ref, y_ref):
        gm = _group_masks()
        xhat, _ = _ln_hat(v_ref[...])
        vb = (xhat * g_ref[...] + b_ref[...]).astype(BF16)
        for c in range(tm // BLK):
            rs = slice(c * BLK, (c + 1) * BLK)
            vc = vb[rs, :]
            mixed = bf_ref[...]
            for g in range(4):
                mixed = mixed + jnp.where(gm[g], _nn(w_ref[g], vc), 0.0)
            y_ref[rs, :] = (u_ref[rs, :] * mixed).astype(BF16)

    vec = bs((1, BW), lambda i: (0, 0))
    return _call(body, "sgu_fwd", (t // tm,),
                 [bs((tm, BW), lambda i: (i, 6)), bs((tm, BW), lambda i: (i, 7)), vec, vec,
                  bs((4, BLK, BLK), lambda i: (0, 0, 0)), bs((BLK, BW), lambda i: (0, 0))],
                 bs((tm, BW), lambda i: (i, 0)), _sds((t, BW), BF16))(proj, proj, ln_g, ln_b, w_tril, b_full)


def sgu_bwd(proj, dys, ln_g, ln_b, w_tril, b_full, into):
    t = proj.shape[0]
    tm = min(512, t)

    def body(u_ref, v_ref, dy_ref, g_ref, b_ref, w_ref, bf_ref, _, o_ref, dw_ref, dbf_ref, dg_ref, db_ref, dvl_ref):
        @pl.when(pl.program_id(0) == 0)
        def _():
            dw_ref[...] = jnp.zeros_like(dw_ref)
            dbf_ref[...] = jnp.zeros_like(dbf_ref)
            dg_ref[...] = jnp.zeros_like(dg_ref)
            db_ref[...] = jnp.zeros_like(db_ref)

        gm = _group_masks()
        xhat, r = _ln_hat(v_ref[...])
        gv = g_ref[...]
        vb = (xhat * gv + b_ref[...]).astype(BF16)
        for c in range(tm // BLK):
            rs = slice(c * BLK, (c + 1) * BLK)
            vc = vb[rs, :]
            dy = dy_ref[rs, :]
            mixed = bf_ref[...]
            for g in range(4):
                mixed = mixed + jnp.where(gm[g], _nn(w_ref[g], vc), 0.0)
            o_ref[rs, :BW] = (dy * mixed).astype(BF16)
            dm = dy * u_ref[rs, :]
            dbf_ref[...] += dm
            dvl = jnp.zeros((BLK, BW), F32)
            for g in range(4):
                dmg = jnp.where(gm[g], dm, 0.0).astype(BF16)
                dw_ref[g] += _nt(dmg, vc)
                dvl = dvl + _tn(w_ref[g], dmg)
            dvl_ref[rs, :] = dvl
        dvl = dvl_ref[...]
        o_ref[:, BW:] = _ln_bwd(dvl, xhat, r, gv).astype(BF16)
        dg_ref[...] += _colsum(dvl * xhat)
        db_ref[...] += _colsum(dvl)

    vec = bs((1, BW), lambda i: (0, 0))
    row = bs((tm, BW), lambda i: (i, 0))
    wsp = bs((4, BLK, BLK), lambda i: (0, 0, 0))
    bfs = bs((BLK, BW), lambda i: (0, 0))
    return _call(body, "sgu_bwd", (t // tm,),
                 [bs((tm, BW), lambda i: (i, 6)), bs((tm, BW), lambda i: (i, 7)), bs((None, tm, BW), lambda i: (2, i, 0)),
                  vec, vec, wsp, bfs, ANY],
                 [bs((tm, 2 * BW), lambda i: (i, 3)), wsp, bfs, vec, vec],
                 [_sds(into.shape, BF16), _sds((4, BLK, BLK), F32), _sds((BLK, BW), F32),
                  _sds((1, BW), F32), _sds((1, BW), F32)],
                 scratch=[pltpu.VMEM((tm, BW), F32)], aliases={7: 0})(proj, proj, dys, ln_g, ln_b, w_tril, b_full, into)


CONF_HALO = 32


def conf_fwd(proj, dw, ln_g, ln_b, after=None):
    t = proj.shape[0]
    tm, halo = min(512, t), CONF_HALO
    per = tm // halo

    def body(v_ref, gt_ref, vh_ref, gh_ref, w_ref, g_ref, b_ref, y_ref, z_ref):
        yh = jnp.where(pl.program_id(0) > 0, vh_ref[...] * _sigmoid(gh_ref[...]), 0.0)
        yext = jnp.concatenate([yh, v_ref[...] * _sigmoid(gt_ref[...])], axis=0)
        z = _causal_conv(yext, w_ref, CONF_K, halo)
        z_ref[...] = z
        xhat, _ = _ln_hat(z)
        ln = xhat * g_ref[...] + b_ref[...]
        y_ref[...] = (ln * _sigmoid(ln)).astype(BF16)

    vec = bs((1, BW), lambda i: (0, 0))
    col = lambda c: bs((tm, BW), lambda i: (i, c))
    hal = lambda c: bs((halo, BW), lambda i: (_prev_blk(i, per), c))
    row = bs((tm, BW), lambda i: (i, 0))
    return _call(body, "conf_fwd", (t // tm,),
                 [col(8), col(9), hal(8), hal(9), bs((CONF_K, BW), lambda i: (0, 0)), vec, vec],
                 [row, row], [_sds((t, BW), BF16), _sds((t, BW), F32)], after=after)(
                     proj, proj, proj, proj, dw, ln_g, ln_b)


def conf_bwd_ln(z, dys, ln_g, ln_b):
    t = z.shape[0]
    tm = min(1024, t)

    def body(z_ref, dy_ref, g_ref, b_ref, dz_ref, dg_ref, db_ref):
        @pl.when(pl.program_id(0) == 0)
        def _():
            dg_ref[...] = jnp.zeros_like(dg_ref)
            db_ref[...] = jnp.zeros_like(db_ref)

        gv = g_ref[...]
        xhat, r = _ln_hat(z_ref[...])
        ln = xhat * gv + b_ref[...]
        s = _sigmoid(ln)
        dln = dy_ref[...] * (s * (1.0 + ln * (1.0 - s)))
        dz_ref[...] = _ln_bwd(dln, xhat, r, gv)
        dg_ref[...] += _colsum(dln * xhat)
        db_ref[...] += _colsum(dln)

    vec = bs((1, BW), lambda i: (0, 0))
    row = bs((tm, BW), lambda i: (i, 0))
    return _call(body, "conf_bwd_ln", (t // tm,), [row, bs((None, tm, BW), lambda i: (3, i, 0)), vec, vec],
                 [row, vec, vec], [_sds((t, BW), F32), _sds((1, BW), F32), _sds((1, BW), F32)])(z, dys, ln_g, ln_b)


def conf_bwd_conv(proj, dz, dw, into):
    t = proj.shape[0]
    tm, halo = min(512, t), CONF_HALO
    per = tm // halo
    last = t // halo - 1
    nt = t // tm

    def body(v_ref, gt_ref, vh_ref, gh_ref, dz_ref, dzn_ref, w_ref, _, o_ref, dw_ref):
        i = pl.program_id(0)

        @pl.when(i == 0)
        def _():
            dw_ref[...] = jnp.zeros_like(dw_ref)

        val = v_ref[...]
        sg = _sigmoid(gt_ref[...])
        yh = jnp.where(i > 0, vh_ref[...] * _sigmoid(gh_ref[...]), 0.0)
        yext = jnp.concatenate([yh, val * sg], axis=0)
        dz = dz_ref[...]
        dzn = jnp.where(i < nt - 1, dzn_ref[...], 0.0)
        dy0 = _anti_conv(jnp.concatenate([dz, dzn], axis=0), w_ref, CONF_K, tm)
        o_ref[:, :BW] = (dy0 * sg).astype(BF16)
        o_ref[:, BW:] = (dy0 * val * sg * (1.0 - sg)).astype(BF16)
        _conv_wgrad(dw_ref, dz, yext, CONF_K, halo)

    col = lambda c: bs((tm, BW), lambda i: (i, c))
    hal = lambda c: bs((halo, BW), lambda i: (_prev_blk(i, per), c))
    row = bs((tm, BW), lambda i: (i, 0))
    wsp = bs((CONF_K, BW), lambda i: (0, 0))
    return _call(body, "conf_bwd_conv", (t // tm,),
                 [col(8), col(9), hal(8), hal(9), row, bs((halo, BW), lambda i: (_next_blk(i, per, last), 0)), wsp, ANY],
                 [bs((tm, 2 * BW), lambda i: (i, 4)), wsp], [_sds(into.shape, BF16), _sds((CONF_K, BW), F32)],
                 aliases={7: 0})(proj, proj, proj, proj, dz, dz, dw, into)


def _place():
    return lax.axis_index("x"), lax.axis_index("y"), lax.axis_index("c")


def _comm_call(body, name, n_in, out_shape, scratch, aliases=None):
    return pl.pallas_call(body, name=name, in_specs=[ANY] * n_in, out_specs=[ANY] * len(out_shape), out_shape=out_shape,
                          scratch_shapes=scratch, input_output_aliases=aliases or {},
                          compiler_params=pltpu.CompilerParams(has_side_effects=True, vmem_limit_bytes=VMEM_LIMIT))


HBM_SPEC = pl.BlockSpec(memory_space=pltpu.HBM)
SEM_SPEC = pl.BlockSpec(memory_space=pltpu.SEMAPHORE)
EFFECT = pltpu.SideEffectType.DATAFLOW_SIDE_EFFECTING


class SplitExchange:
    def __init__(self, name, bufs, plan, n_copies):
        self.name, self.bufs, self.plan, self.n = name, list(bufs), plan, n_copies

    def start(self, after):
        nb, n, plan = len(self.bufs), self.n, self.plan

        def body(*refs):
            send, recv, token = refs[nb + 1], refs[nb + 2], refs[-1]
            for k, (src, dst, _, dev) in enumerate(plan(refs[:nb])):
                pltpu.make_async_remote_copy(src_ref=src, dst_ref=dst, send_sem=send.at[k], recv_sem=recv.at[k],
                                             device_id=dev, device_id_type=MESH).start()
            token[...] = jnp.zeros_like(token)

        outs = pl.pallas_call(
            body, name=self.name + "_start",
            out_shape=(pltpu.SemaphoreType.DMA((n,)), pltpu.SemaphoreType.DMA((n,)),
                       *[pltpu.HBM(b.shape, b.dtype) for b in self.bufs], _sds((8, 128), F32)),
            in_specs=[HBM_SPEC] * nb + [ANY],
            out_specs=(SEM_SPEC, SEM_SPEC, *[HBM_SPEC] * nb, pl.BlockSpec(memory_space=pltpu.VMEM)),
            input_output_aliases={i: 2 + i for i in range(nb)},
            compiler_params=pltpu.CompilerParams(has_side_effects=EFFECT))(
                *[pltpu.with_memory_space_constraint(b, pltpu.HBM) for b in self.bufs], after)
        self.send, self.recv, self.bufs = outs[0], outs[1], list(outs[2:2 + nb])
        return outs[-1]

    def wait(self, after):
        nb, plan = len(self.bufs), self.plan
        after = list(after) if isinstance(after, (list, tuple)) else [after]

        def body(*refs):
            send, recv = refs[nb], refs[nb + 1]
            for k, (src, _, land, dev) in enumerate(plan(refs[:nb])):
                cp = pltpu.make_async_remote_copy(src_ref=src, dst_ref=land, send_sem=send.at[k], recv_sem=recv.at[k],
                                                  device_id=dev, device_id_type=MESH)
                cp.wait_send()
                cp.wait_recv()

        outs = pl.pallas_call(
            body, name=self.name + "_wait", out_shape=tuple(pltpu.HBM(b.shape, b.dtype) for b in self.bufs),
            in_specs=[HBM_SPEC] * nb + [SEM_SPEC, SEM_SPEC] + [ANY] * len(after), out_specs=[HBM_SPEC] * nb,
            input_output_aliases={i: i for i in range(nb)},
            compiler_params=pltpu.CompilerParams(has_side_effects=EFFECT))(*self.bufs, self.send, self.recv, *after)
        return list(outs)


def _chips_of(x, y):
    return [(1 - x, y), (x, 1 - y), (1 - x, 1 - y)]


def allgather_ici_plan(shapes):
    def plan(refs):
        x, y, c = _place()
        out = []
        for a, ref in enumerate(refs):
            hl = shapes[a][1] // 2
            half = pl.ds(c * hl, hl)
            for cx, cy in _chips_of(x, y):
                mine = ref.at[2 * x + y, half]
                out.append((mine, mine, ref.at[2 * cx + cy, half], (cx, cy, c)))
        return out
    return plan


def allgather_d2d_plan(shapes):
    def plan(refs):
        x, y, c = _place()
        out = []
        for a, ref in enumerate(refs):
            hl = shapes[a][1] // 2
            for cx, cy in _chips_of(x, y):
                got = ref.at[2 * cx + cy, pl.ds(c * hl, hl)]
                out.append((got, got, ref.at[2 * cx + cy, pl.ds((1 - c) * hl, hl)], (x, y, 1 - c)))
        return out
    return plan


def gather8(v, reduce):
    rows, cols = v.shape

    def body(v_ref, o_ref, land_ref, send, recv, lsem):
        x, y, c = _place()
        me = 4 * x + 2 * y + c
        land = land_ref if reduce else o_ref
        mine = pltpu.make_async_copy(v_ref, land.at[me], lsem)
        mine.start()
        sent = []
        for j in range(1, 8):
            fx, fy, fc = (j >> 2) & 1, (j >> 1) & 1, j & 1
            tgt = (1 - x if fx else x, 1 - y if fy else y, 1 - c if fc else c)
            cp = pltpu.make_async_remote_copy(src_ref=v_ref, dst_ref=land.at[me], send_sem=send.at[j - 1],
                                              recv_sem=recv.at[j - 1], device_id=tgt, device_id_type=MESH)
            cp.start()
            sent.append(cp)
        for j in range(1, 8):
            fx, fy, fc = (j >> 2) & 1, (j >> 1) & 1, j & 1
            peer = 4 * (1 - x if fx else x) + 2 * (1 - y if fy else y) + (1 - c if fc else c)
            pltpu.make_async_remote_copy(src_ref=v_ref, dst_ref=land.at[peer], send_sem=send.at[j - 1],
                                         recv_sem=recv.at[j - 1], device_id=(x, y, c), device_id_type=MESH).wait_recv()
        for cp in sent:
            cp.wait_send()
        mine.wait()
        if reduce:
            acc = land_ref[0]
            for k in range(1, 8):
                acc = acc + land_ref[k]
            o_ref[...] = acc

    vm = pl.BlockSpec(memory_space=pltpu.VMEM)
    out_shape = _sds((rows, cols), F32) if reduce else _sds((8, rows, cols), F32)
    land_shape = (8, rows, cols) if reduce else (8, 128)
    return pl.pallas_call(
        body, name="allreduce8" if reduce else "allgather8", in_specs=[vm], out_specs=vm, out_shape=out_shape,
        scratch_shapes=[pltpu.VMEM(land_shape, F32), pltpu.SemaphoreType.DMA((7,)), pltpu.SemaphoreType.DMA((7,)),
                        pltpu.SemaphoreType.DMA],
        compiler_params=pltpu.CompilerParams(has_side_effects=True, vmem_limit_bytes=VMEM_LIMIT))(v)


def allgather_weights(bufs):
    n = len(bufs)

    def body(*refs):
        ins, outs = refs[:n], refs[n:2 * n]
        send, recv = refs[2 * n:]
        x, y, c = _place()
        s_me = 2 * x + y
        chips = [(1 - x, y), (x, 1 - y), (1 - x, 1 - y)]
        sibling = (x, y, 1 - c)
        started = []
        for a in range(n):
            hl = bufs[a].shape[1] // 2
            half = pl.ds(c * hl, hl)
            for j, chip in enumerate(chips):
                cp = pltpu.make_async_remote_copy(src_ref=ins[a].at[s_me, half], dst_ref=outs[a].at[s_me, half],
                                                  send_sem=send.at[6 * a + j], recv_sem=recv.at[6 * a + j],
                                                  device_id=(chip[0], chip[1], c), device_id_type=MESH)
                cp.start()
                started.append(cp)
        for a in range(n):
            hl = bufs[a].shape[1] // 2
            half = pl.ds(c * hl, hl)
            for j, chip in enumerate(chips):
                s_j = 2 * chip[0] + chip[1]
                landed = outs[a].at[s_j, half]
                pltpu.make_async_remote_copy(src_ref=landed, dst_ref=landed, send_sem=send.at[6 * a + j],
                                             recv_sem=recv.at[6 * a + j], device_id=sibling, device_id_type=MESH).wait_recv()
                fw = pltpu.make_async_remote_copy(src_ref=landed, dst_ref=landed, send_sem=send.at[6 * a + 3 + j],
                                                  recv_sem=recv.at[6 * a + 3 + j], device_id=sibling, device_id_type=MESH)
                fw.start()
                started.append(fw)
        for a in range(n):
            hl = bufs[a].shape[1] // 2
            other = pl.ds((1 - c) * hl, hl)
            for j, chip in enumerate(chips):
                s_j = 2 * chip[0] + chip[1]
                theirs = outs[a].at[s_j, other]
                pltpu.make_async_remote_copy(src_ref=theirs, dst_ref=theirs, send_sem=send.at[6 * a + 3 + j],
                                             recv_sem=recv.at[6 * a + 3 + j], device_id=sibling, device_id_type=MESH).wait_recv()
        for cp in started:
            cp.wait_send()

    out_shape = [_sds(b.shape, b.dtype) for b in bufs]
    scratch = [pltpu.SemaphoreType.DMA((6 * n,)), pltpu.SemaphoreType.DMA((6 * n,))]
    return _comm_call(body, "allgather_weights", n, out_shape, scratch, aliases={a: a for a in range(n)})(*bufs)


def _row_tile(rows, cols):
    best = 16
    for t in range(16, rows + 1, 16):
        if rows % t == 0 and t * cols * 4 <= 2 * 1024 * 1024:
            best = t
    return best


def _rs_add_sibling(scal, g, ra, hr):
    cols = g.shape[2]
    tr = _row_tile(hr, cols)
    nr = hr // tr

    def body(s_ref, g_ref, r_ref, p32_ref, p16_ref):
        v = g_ref[...] + r_ref[...]
        p16_ref[...] = v.astype(BF16)

        @pl.when(pl.program_id(1) == s_ref[0])
        def _():
            p32_ref[...] = v

    blk = lambda f: bs((None, tr, cols), f)
    own = blk(lambda i, s, sr: (s, i, 0))
    spec = pltpu.PrefetchScalarGridSpec(num_scalar_prefetch=1, grid=(nr, N_SH),
                                        in_specs=[blk(lambda i, s, sr: (s, sr[1] * nr + i, 0)), own],
                                        out_specs=[bs((tr, cols), lambda i, s, sr: (i, 0)), own])
    return pl.pallas_call(body, name="rs_add_sibling", grid_spec=spec,
                          out_shape=[_sds((hr, cols), F32), _sds((N_SH, hr, cols), BF16)],
                          compiler_params=pltpu.CompilerParams(dimension_semantics=("arbitrary",) * 2,
                                                               vmem_limit_bytes=VMEM_LIMIT))(scal, g, ra)


def _rs_add_chips(scal, p32, rb, hr):
    cols = p32.shape[1]
    tr = _row_tile(hr, cols)
    nr = hr // tr

    def body(s_ref, p_ref, r0, r1, r2, o_ref):
        o_ref[...] = ((p_ref[...] + r0[...].astype(F32)) + r1[...].astype(F32)) + r2[...].astype(F32)

    blk = lambda f: bs((None, tr, cols), f)
    spec = pltpu.PrefetchScalarGridSpec(
        num_scalar_prefetch=1, grid=(nr,),
        in_specs=[bs((tr, cols), lambda i, sr: (i, 0))] + [blk(functools.partial(lambda i, sr, j: (j, i, 0), j=j))
                                                            for j in range(3)],
        out_specs=blk(lambda i, sr: (sr[1], i, 0)))
    return pl.pallas_call(body, name="rs_add_chips", grid_spec=spec, out_shape=_sds((2, hr, cols), F32),
                          compiler_params=pltpu.CompilerParams(dimension_semantics=("arbitrary",),
                                                               vmem_limit_bytes=VMEM_LIMIT))(scal, p32, rb, rb, rb)


class SplitReduceScatter:
    def __init__(self, gs):
        x, y, c = _place()
        self.scal = jnp.stack([2 * x + y, c]).astype(jnp.int32)
        self.gs, self.n = list(gs), len(gs)
        self.hrs = [g.shape[1] // 2 for g in gs]

    def swap_start(self, after):
        n, hrs = self.n, self.hrs

        def plan(refs):
            x, y, c = _place()
            return [(refs[a].at[:, pl.ds((1 - c) * hrs[a], hrs[a])], refs[n + a], refs[n + a], (x, y, 1 - c))
                    for a in range(n)]

        lands = [lax.empty((N_SH, hrs[a], g.shape[2]), F32) for a, g in enumerate(self.gs)]
        self.ex = SplitExchange("rs_swap_halves", self.gs + lands, plan, n)
        return self.ex.start(after)

    def swap_wait_send_start(self, after):
        n, hrs = self.n, self.hrs
        bufs = self.ex.wait(after)
        parts = [_rs_add_sibling(self.scal, bufs[a], bufs[n + a], hrs[a]) for a in range(n)]
        self.p32 = [p[0] for p in parts]

        def plan(refs):
            x, y, c = _place()
            return [(refs[a].at[2 * cx + cy], refs[n + a].at[j], refs[n + a].at[j], (cx, cy, c))
                    for a in range(n) for j, (cx, cy) in enumerate(_chips_of(x, y))]

        lands = [lax.empty((3, hrs[a], g.shape[2]), BF16) for a, g in enumerate(self.gs)]
        self.ex = SplitExchange("rs_send_partials", [p[1] for p in parts] + lands, plan, 3 * n)
        return self.ex.start(parts[-1][1])

    def send_wait_share_start(self, after):
        n, hrs = self.n, self.hrs
        bufs = self.ex.wait(after)
        fins = [_rs_add_chips(self.scal, self.p32[a], bufs[n + a], hrs[a]) for a in range(n)]

        def plan(refs):
            x, y, c = _place()
            return [(refs[a].at[c], refs[a].at[c], refs[a].at[1 - c], (x, y, 1 - c)) for a in range(n)]

        self.ex = SplitExchange("rs_share_halves", fins, plan, n)
        return self.ex.start(fins[-1])

    def share_wait(self, after):
        fulls = self.ex.wait(after)
        return [f.reshape(2 * hr, f.shape[2]) for f, hr in zip(fulls, self.hrs)]


def reduce_scatter_grads(gs):
    n = len(gs)
    x, y, c = _place()
    scal = jnp.stack([2 * x + y, c]).astype(jnp.int32)
    hrs = [g.shape[1] // 2 for g in gs]

    def swap_body(*refs):
        ins, outs = refs[:n], refs[n:2 * n]
        send, recv = refs[2 * n:]
        xx, yy, cc = _place()
        cps = []
        for a in range(n):
            cp = pltpu.make_async_remote_copy(src_ref=ins[a].at[:, pl.ds((1 - cc) * hrs[a], hrs[a])], dst_ref=outs[a],
                                              send_sem=send.at[a], recv_sem=recv.at[a],
                                              device_id=(xx, yy, 1 - cc), device_id_type=MESH)
            cp.start()
            cps.append(cp)
        for cp in cps:
            cp.wait()

    ras = _comm_call(swap_body, "rs_swap_halves", n, [_sds((N_SH, hrs[a], gs[a].shape[2]), F32) for a in range(n)],
                     [pltpu.SemaphoreType.DMA((n,)), pltpu.SemaphoreType.DMA((n,))])(*gs)

    parts = [_rs_add_sibling(scal, gs[a], ras[a], hrs[a]) for a in range(n)]

    def ici_body(*refs):
        ins, outs = refs[:n], refs[n:2 * n]
        send, recv = refs[2 * n:]
        xx, yy, cc = _place()
        chips = [(1 - xx, yy), (xx, 1 - yy), (1 - xx, 1 - yy)]
        cps = []
        for a in range(n):
            for j, chip in enumerate(chips):
                cp = pltpu.make_async_remote_copy(src_ref=ins[a].at[2 * chip[0] + chip[1]], dst_ref=outs[a].at[j],
                                                  send_sem=send.at[3 * a + j], recv_sem=recv.at[3 * a + j],
                                                  device_id=(chip[0], chip[1], cc), device_id_type=MESH)
                cp.start()
                cps.append(cp)
        for cp in cps:
            cp.wait()

    rbs = _comm_call(ici_body, "rs_send_partials", n, [_sds((3, hrs[a], gs[a].shape[2]), BF16) for a in range(n)],
                     [pltpu.SemaphoreType.DMA((3 * n,)), pltpu.SemaphoreType.DMA((3 * n,))])(*[p[1] for p in parts])

    fins = [_rs_add_chips(scal, parts[a][0], rbs[a], hrs[a]) for a in range(n)]

    def share_body(*refs):
        ins, outs = refs[:n], refs[n:2 * n]
        send, recv = refs[2 * n:]
        xx, yy, cc = _place()
        sib = (xx, yy, 1 - cc)
        cps = []
        for a in range(n):
            cp = pltpu.make_async_remote_copy(src_ref=ins[a].at[cc], dst_ref=outs[a].at[cc], send_sem=send.at[a],
                                              recv_sem=recv.at[a], device_id=sib, device_id_type=MESH)
            cp.start()
            cps.append(cp)
        for a in range(n):
            pltpu.make_async_remote_copy(src_ref=ins[a].at[cc], dst_ref=outs[a].at[1 - cc], send_sem=send.at[a],
                                         recv_sem=recv.at[a], device_id=sib, device_id_type=MESH).wait_recv()
        for cp in cps:
            cp.wait_send()

    fulls = _comm_call(share_body, "rs_share_halves", n, [_sds(f.shape, F32) for f in fins],
                       [pltpu.SemaphoreType.DMA((n,)), pltpu.SemaphoreType.DMA((n,))],
                       aliases={a: a for a in range(n)})(*fins)
    return [f.reshape(2 * hr, f.shape[2]) for f, hr in zip(fulls, hrs)]


def adamw(w, g, m, v):
    shape = w.shape
    cols = shape[-1]
    rows = math.prod(shape[:-1]) if len(shape) > 1 else 1
    tr = 256 if rows % 256 == 0 and rows > 256 else rows
    c1 = 1.0 - ADAM_B1 ** ADAM_STEP
    c2 = 1.0 - ADAM_B2 ** ADAM_STEP

    def body(w_ref, g_ref, m_ref, v_ref, d_ref, nm_ref, nv_ref):
        gv = g_ref[...]
        nm = ADAM_B1 * m_ref[...] + (1.0 - ADAM_B1) * gv
        nv = ADAM_B2 * v_ref[...] + (1.0 - ADAM_B2) * (gv * gv)
        nm_ref[...] = nm
        nv_ref[...] = nv
        d_ref[...] = -ADAM_LR * ((nm / c1) / (jnp.sqrt(nv / c2) + ADAM_EPS) + ADAM_WD * w_ref[...])

    row = bs((tr, cols), lambda i: (i, 0))
    outs = _call(body, "adamw", (rows // tr,), [row] * 4, [row] * 3, [_sds((rows, cols), F32)] * 3)(
        *[a.reshape(rows, cols) for a in (w, g, m, v)])
    return [o.reshape(shape) for o in outs]


def adamw_layers(w, gs, m, v, lo, into=None, after=None):
    shape = w.shape
    cols = shape[-1]
    rl = math.prod(shape[1:-1])
    tr = max(t_ for t_ in range(8, rl + 1, 8) if rl % t_ == 0 and t_ * cols * 4 <= 1024 * 1024)
    nb = rl // tr
    n = len(gs)
    c1 = 1.0 - ADAM_B1 ** ADAM_STEP
    c2 = 1.0 - ADAM_B2 ** ADAM_STEP

    def body(*refs):
        w_ref, m_ref, v_ref = refs[:3]
        g_refs = refs[3:3 + n]
        d_ref, nm_ref, nv_ref, go_ref = refs[-4:]
        layer = pl.program_id(0) // nb
        for k in range(n):
            @pl.when(layer == k)
            def _(k=k):
                gv = g_refs[k][...]
                nm = ADAM_B1 * m_ref[...] + (1.0 - ADAM_B1) * gv
                nv = ADAM_B2 * v_ref[...] + (1.0 - ADAM_B2) * (gv * gv)
                nm_ref[...] = nm
                nv_ref[...] = nv
                go_ref[...] = gv
                d_ref[...] = -ADAM_LR * ((nm / c1) / (jnp.sqrt(nv / c2) + ADAM_EPS) + ADAM_WD * w_ref[...])

    row = bs((tr, cols), lambda b: (lo * nb + b, 0))
    g_specs = [bs((tr, cols), functools.partial(lambda b, k: (jnp.clip(b - k * nb, 0, nb - 1), 0), k=k)) for k in range(n)]
    flat = lambda a: a.reshape(-1, cols)
    in_specs = [row] * 3 + g_specs
    args = [flat(w), flat(m), flat(v)] + [flat(g) for g in gs]
    aliases = None
    if into is not None:
        aliases = {len(in_specs) + k: k for k in range(4)}
        in_specs = in_specs + [ANY] * 4
        args = args + [flat(a) for a in into]
    outs = _call(body, "adamw_layers", (n * nb,), in_specs, [row] * 4, [_sds((shape[0] * rl, cols), F32)] * 4,
                 aliases=aliases, after=after)(*args)
    return [o.reshape(shape) for o in outs]


def allreduce8_split(vec):
    rows, cols = vec.shape

    def plan(refs):
        x, y, c = _place()
        me = 4 * x + 2 * y + c
        out = []
        for j in range(1, 8):
            px, py, pc = (1 - x if j & 4 else x), (1 - y if j & 2 else y), (1 - c if j & 1 else c)
            out.append((refs[0], refs[1].at[me], refs[1].at[4 * px + 2 * py + pc], (px, py, pc)))
        return out

    ex = SplitExchange("allreduce8", [vec, lax.empty((8, rows, cols), F32)], plan, 7)

    def finish(after):
        v, land = ex.wait(after)
        x, y, c = _place()
        me = jnp.reshape(4 * x + 2 * y + c, (1,)).astype(jnp.int32)

        def body(me_ref, v_ref, l_ref, o_ref):
            o_ref[...] = jnp.zeros_like(o_ref)
            for k in range(8):
                @pl.when(me_ref[0] == k)
                def _():
                    o_ref[...] += v_ref[...]

                @pl.when(me_ref[0] != k)
                def _(k=k):
                    o_ref[...] += l_ref[k]

        spec = pltpu.PrefetchScalarGridSpec(
            num_scalar_prefetch=1, grid=(1,),
            in_specs=[bs((rows, cols), lambda i, mr: (0, 0)), bs((8, rows, cols), lambda i, mr: (0, 0, 0))],
            out_specs=bs((rows, cols), lambda i, mr: (0, 0)))
        return pl.pallas_call(body, name="allreduce8_sum", grid_spec=spec, out_shape=_sds((rows, cols), F32),
                              compiler_params=pltpu.CompilerParams(dimension_semantics=("arbitrary",),
                                                                   vmem_limit_bytes=VMEM_LIMIT))(me, v, land)

    return ex, finish


class Hooks:
    def __init__(self):
        self.steps = {}

    def add(self, point, fn):
        self.steps.setdefault(point, []).append(fn)

    def run(self, point, arr, env=None):
        tok = None
        for fn in self.steps.get(point, ()):
            got = fn(arr if tok is None else tok, env)
            tok = tok if got is None else got
        return tok


def layer_fwd(x, p_i, w, hooks):
    h, proj, *qkv = norm_in_proj(x, w["g_mix"], w["win"], after=hooks.run("start", x))
    ya = conva_fwd(proj, w["conv_a"])
    yb, o32, lse = attn_merge([attn_fwd_group(pv, d) for pv, d in zip(qkv, DILATIONS)])
    yc = sgu_fwd(proj, w["sgu_ln_g"], w["sgu_ln_b"], w["sgu_wt"], w["sgu_bf"])
    yd, z = conf_fwd(proj, w["conf_dw"], w["conf_ln_g"], w["conf_ln_b"], after=hooks.run("pre_conf", [ya, yb, yc]))
    ys = (ya, yb, yc, yd)
    tok = hooks.run("pre_merge", yd)
    merged, gates, ybr = merge_fwd(h, ys, w["wg"], w["wbr"], after=tok)
    x1, h2, fgu, act = ffn_in(merged, w["wout"], x, w["g_ffn"], w["wfi"], after=hooks.run("post_merge", merged))
    x2, h3, gate, pp, x3 = ple_fwd(act, w["wfo"], x1, w["g_ple"], w["wpg"], p_i, w["wpp"],
                                   after=hooks.run("post_ffn_in", act))
    saved = dict(x=x, h=h, proj=proj, qkv=qkv, ys=ys, o32=o32, lse=lse, z=z, merged=merged, gates=gates, ybr=ybr, x1=x1,
                 h2=h2, fgu=fgu, act=act, x2=x2, h3=h3, gate=gate, pp=pp)
    return x3, saved


def layer_bwd(dx3, p_i, w, s, hooks):
    t = dx3.shape[0]
    tr = min(1024, t)
    nr = t // tr
    ns_fi = FFN_H // 2
    small = {}

    dpre, dpp, dx2, small["g_ple"] = ple_bwd(dx3, s["gate"], s["pp"], w["wpg"], s["x2"], w["g_ple"],
                                             after=hooks.run("start", dx3))
    ga_shape, gb_shape = _sds((N_SH, 6 * BW, D_MODEL), F32), _sds((N_SH, 5 * BW, BW), F32)
    ga_blk = lambda idx: bs((N_SH, BW, D_MODEL), idx)
    ga = tn_matmul("dw_ple_gate", s["h3"], dpre, (nr,), bs((tr, D_MODEL), lambda r: (r, 0)),
                   bs((tr, D_MODEL), lambda r: (r, 0)), ga_blk(lambda r: (0, 5, 0)), ga_shape, split=N_SH)
    gb = tn_matmul("dw_ple_proj", p_i, dpp, (N_SH, nr), bs((tr, BW), lambda j, r: (r, 0)),
                   bs((tr, BW), lambda j, r: (r, j)), bs((None, BW, BW), lambda j, r: (j, 4, 0)), gb_shape)

    df, dx1, small["g_ffn"] = ffn_bwd(dx2, w["wfo"], s["fgu"], w["wfi"], s["x1"], w["g_ffn"],
                                      after=hooks.run("pre_ffn", dx2))
    gfo = tn_matmul("dw_ffn_out", s["act"], dx2, (2, nr), bs((tr, ns_fi), lambda j, r: (r, j)),
                    bs((tr, D_MODEL), lambda j, r: (r, 0)), bs((2, FFN_H // N_SH, D_MODEL), lambda j, r: (j, 0, 0)),
                    _sds((N_SH, FFN_H // N_SH, D_MODEL), F32), split=2)
    gfi = tn_matmul("dw_ffn_in", s["h2"], df, (N_SH, nr), bs((tr, D_MODEL), lambda j, r: (r, 0)),
                    bs((None, tr, ns_fi), lambda j, r: (j // 2, r, j % 2)),
                    bs((None, D_MODEL, ns_fi), lambda j, r: (j, 0, 0)), _sds((N_SH, D_MODEL, ns_fi), F32))

    dpre_m, dyb, dys = merge_bwd(dx1, w["wout"], s["gates"], s["ybr"], w["wbr"])
    ga = tn_matmul("dw_out", s["merged"], dx1, (nr,), bs((tr, D_MODEL), lambda r: (r, 0)),
                   bs((tr, D_MODEL), lambda r: (r, 0)), ga_blk(lambda r: (0, 4, 0)), ga_shape, split=N_SH, into=ga,
                   after=hooks.run("pre_dw_out", dyb, dict(gfo=gfo, gfi=gfi)))
    ga = tn_matmul("dw_merge_gate", s["h"], dpre_m, (N_BR, nr), bs((tr, D_MODEL), lambda k, r: (r, 0)),
                   bs((None, tr, D_MODEL), lambda k, r: (k, r, 0)), ga_blk(lambda k, r: (0, k, 0)), ga_shape,
                   split=N_SH, into=ga)
    for k in range(N_BR):
        gb = tn_matmul("dw_branch", s["ys"][k], dyb, (nr,), bs((tr, BW), lambda r: (r, 0)),
                       bs((None, tr, D_MODEL), functools.partial(lambda r, kk: (kk, r, 0), kk=k)),
                       bs((N_SH, BW, BW), functools.partial(lambda r, kk: (0, kk, 0), kk=k)), gb_shape,
                       split_cols=N_SH, into=gb)

    dproj, small["conv_a"] = conva_bwd(s["proj"], dys, w["conv_a"], after=hooks.run("pre_conva", gb))
    lds, dy_views = attn_delta(dys, s["o32"], s["lse"])
    dy_views = [dys[1]] + list(dy_views)
    dproj = attn_bwd_finish([attn_bwd_group(pv, dov, ldv, d)
                             for pv, dov, ldv, d in zip(s["qkv"], dy_views, lds, DILATIONS)], dproj)
    dproj, d_sw, d_sbf, small["sgu_ln_g"], small["sgu_ln_b"] = sgu_bwd(
        s["proj"], dys, w["sgu_ln_g"], w["sgu_ln_b"], w["sgu_wt"], w["sgu_bf"], dproj)
    small["sgu_w"] = jnp.where(jnp.tril(jnp.ones((BLK, BLK), bool))[None], d_sw, 0.0)
    small["sgu_b"] = jnp.sum(d_sbf.reshape(BLK, 4, HEAD_D), axis=-1).T
    dz, small["conf_ln_g"], small["conf_ln_b"] = conf_bwd_ln(s["z"], dys, w["conf_ln_g"], w["conf_ln_b"])
    dproj, small["conf_dw"] = conf_bwd_conv(s["proj"], dz, w["conf_dw"], dproj)

    ns_in = N_IN // N_SH
    gin = tn_matmul("dw_in", s["h"], dproj, (N_SH, nr), bs((tr, D_MODEL), lambda j, r: (r, 0)),
                    bs((tr, ns_in), lambda j, r: (r, j)), bs((None, D_MODEL, ns_in), lambda j, r: (j, 0, 0)),
                    _sds((N_SH, D_MODEL, ns_in), F32), after=hooks.run("pre_dw_in", dproj))
    hooks.run("end", gin)
    big = [ga, gfo, gb, gin, gfi]
    dx, small["g_mix"] = norm_bwd(
        "mix_norm_bwd",
        [(dpre_m, lambda tm: bs((N_BR, tm, D_MODEL), lambda i: (0, i, 0)), w["wg"],
          lambda a, wr: [(a[k], wr[k]) for k in range(N_BR)]),
         (dproj, lambda tm: bs((tm, N_IN), lambda i: (i, 0)), w["win"],
          lambda a, wr: [(a[:, k * ns_in:(k + 1) * ns_in], wr[k]) for k in range(N_SH)])],
        dx1, s["x"], w["g_mix"])
    return dx, big, small


BIG_NAMES = ("w_in", "w_branch", "w_merge_gate", "w_out", "w_ffn_in", "w_ffn_out", "w_ple_gate", "w_ple_proj")


def unpack_big_grads(ga, gfo, gb, gin, gfi):
    return dict(w_in=gin, w_ffn_in=gfi, w_ffn_out=gfo,
                w_merge_gate=ga[:N_BR * BW].reshape(N_BR, BW, D_MODEL), w_out=ga[N_BR * BW:5 * BW], w_ple_gate=ga[5 * BW:],
                w_branch=gb[:N_BR * BW].reshape(N_BR, BW, BW), w_ple_proj=gb[N_BR * BW:])


SMALL_NAMES = ("g_mix", "conv_a", "sgu_ln_g", "sgu_ln_b", "sgu_w", "sgu_b", "conf_dw", "conf_ln_g", "conf_ln_b",
               "g_ffn", "g_ple")


def _pack_rows(arrays, rows):
    flat = jnp.concatenate([a.reshape(-1) for a in arrays])
    return jnp.pad(flat, (0, rows * D_MODEL - flat.shape[0])).reshape(rows, D_MODEL)


def _unpack_rows(packed, shapes):
    flat, out, pos = packed.reshape(-1), [], 0
    for shape in shapes:
        n = math.prod(shape)
        out.append(flat[pos:pos + n].reshape(shape))
        pos += n
    return out


def kernel(x, p, g_mix, w_in, conv_a, sgu_ln_g, sgu_ln_b, sgu_w, sgu_b, conf_dw, conf_ln_g, conf_ln_b, w_branch, w_merge_gate, w_out, g_ffn, w_ffn_in, w_ffn_out, g_ple, w_ple_gate, w_ple_proj, g_final, loss_target, m_g_mix, m_w_in, m_conv_a, m_sgu_ln_g, m_sgu_ln_b, m_sgu_w, m_sgu_b, m_conf_dw, m_conf_ln_g, m_conf_ln_b, m_w_branch, m_w_merge_gate, m_w_out, m_g_ffn, m_w_ffn_in, m_w_ffn_out, m_g_ple, m_w_ple_gate, m_w_ple_proj, m_g_final, v_g_mix, v_w_in, v_conv_a, v_sgu_ln_g, v_sgu_ln_b, v_sgu_w, v_sgu_b, v_conf_dw, v_conf_ln_g, v_conf_ln_b, v_w_branch, v_w_merge_gate, v_w_out, v_g_ffn, v_w_ffn_in, v_w_ffn_out, v_g_ple, v_w_ple_gate, v_w_ple_proj, v_g_final):
    weights = dict(g_mix=g_mix, w_in=w_in, conv_a=conv_a, sgu_ln_g=sgu_ln_g, sgu_ln_b=sgu_ln_b, sgu_w=sgu_w, sgu_b=sgu_b,
                   conf_dw=conf_dw, conf_ln_g=conf_ln_g, conf_ln_b=conf_ln_b, w_branch=w_branch, w_merge_gate=w_merge_gate,
                   w_out=w_out, g_ffn=g_ffn, w_ffn_in=w_ffn_in, w_ffn_out=w_ffn_out, g_ple=g_ple, w_ple_gate=w_ple_gate,
                   w_ple_proj=w_ple_proj, g_final=g_final)
    m_in = dict(g_mix=m_g_mix, w_in=m_w_in, conv_a=m_conv_a, sgu_ln_g=m_sgu_ln_g, sgu_ln_b=m_sgu_ln_b, sgu_w=m_sgu_w,
                sgu_b=m_sgu_b, conf_dw=m_conf_dw, conf_ln_g=m_conf_ln_g, conf_ln_b=m_conf_ln_b, w_branch=m_w_branch,
                w_merge_gate=m_w_merge_gate, w_out=m_w_out, g_ffn=m_g_ffn, w_ffn_in=m_w_ffn_in, w_ffn_out=m_w_ffn_out,
                g_ple=m_g_ple, w_ple_gate=m_w_ple_gate, w_ple_proj=m_w_ple_proj, g_final=m_g_final)
    v_in = dict(g_mix=v_g_mix, w_in=v_w_in, conv_a=v_conv_a, sgu_ln_g=v_sgu_ln_g, sgu_ln_b=v_sgu_ln_b, sgu_w=v_sgu_w,
                sgu_b=v_sgu_b, conf_dw=v_conf_dw, conf_ln_g=v_conf_ln_g, conf_ln_b=v_conf_ln_b, w_branch=v_w_branch,
                w_merge_gate=v_w_merge_gate, w_out=v_w_out, g_ffn=v_g_ffn, w_ffn_in=v_w_ffn_in, w_ffn_out=v_w_ffn_out,
                g_ple=v_g_ple, w_ple_gate=v_w_ple_gate, w_ple_proj=v_w_ple_proj, g_final=v_g_final)
    order = ("g_mix", "w_in", "conv_a", "sgu_ln_g", "sgu_ln_b", "sgu_w", "sgu_b", "conf_dw", "conf_ln_g", "conf_ln_b",
             "w_branch", "w_merge_gate", "w_out", "g_ffn", "w_ffn_in", "w_ffn_out", "g_ple", "w_ple_gate", "w_ple_proj",
             "g_final")
    depth = g_mix.shape[0]
    xs, tgt = x[0], loss_target[0]
    cw = BW // N_SH
    my_shard = 2 * lax.axis_index("x") + lax.axis_index("y")

    conv_rows = 16
    allc = gather8(_pack_rows([conv_a, conf_dw], conv_rows), reduce=False)
    shards = [_unpack_rows(allc[2 * s], [conv_a.shape, conf_dw.shape]) for s in range(N_SH)]
    conv_a_full = jnp.concatenate([sh[0] for sh in shards], axis=-1)
    conf_dw_full = jnp.concatenate([sh[1] for sh in shards], axis=-1)

    tril = jnp.tril(jnp.ones((BLK, BLK), bool))
    def placed_shards(i):
        shards = ([w_in[i], w_branch[i]] + [w_merge_gate[i, k] for k in range(N_BR)]
                  + [w_out[i], w_ffn_in[i], w_ffn_out[i], w_ple_gate[i], w_ple_proj[i]])
        return [lax.dynamic_update_slice(lax.empty((N_SH,) + sh.shape, BF16), sh.astype(BF16)[None],
                                         (my_shard,) + (0,) * sh.ndim) for sh in shards]

    def small_weights(i, win):
        vec = lambda a: a[i].reshape(1, -1)
        return dict(
            win=win, g_mix=vec(g_mix), g_ffn=vec(g_ffn), g_ple=vec(g_ple), conv_a=conv_a_full[i], conf_dw=conf_dw_full[i],
            sgu_ln_g=vec(sgu_ln_g), sgu_ln_b=vec(sgu_ln_b), conf_ln_g=vec(conf_ln_g), conf_ln_b=vec(conf_ln_b),
            sgu_wt=jnp.where(tril[None], sgu_w[i], 0.0).astype(BF16),
            sgu_bf=jnp.repeat(sgu_b[i].T, HEAD_D, axis=1))

    def late_weights(got):
        return dict(wbr=got[0], wg=jnp.stack([g.reshape(D_MODEL, D_MODEL) for g in got[1:5]]),
                    wout=got[5].reshape(D_MODEL, D_MODEL), wfi=got[6], wfo=got[7].reshape(FFN_H, D_MODEL),
                    wpg=got[8].reshape(D_MODEL, D_MODEL), wpp=got[9])

    class SplitAllGather:
        def __init__(self, bufs):
            self.shapes = [b.shape for b in bufs]
            self.ici = SplitExchange("allgather_ici", bufs, allgather_ici_plan(self.shapes), 3 * len(bufs))

        def ici_start(self, after, env=None):
            return self.ici.start(after)

        def ici_wait_d2d_start(self, after, env=None):
            landed = self.ici.wait(after)
            self.d2d = SplitExchange("allgather_d2d", landed, allgather_d2d_plan(self.shapes), 3 * len(landed))
            return self.d2d.start(landed[-1])

        def d2d_wait(self, after, env=None):
            self.got = self.d2d.wait(after)
            return None

    bufs0 = placed_shards(0)
    first = SplitAllGather(bufs0[:1])
    first.d2d_wait(first.ici_wait_d2d_start(first.ici_start(xs)))
    rest = SplitAllGather(bufs0[1:])
    layers = [small_weights(0, first.got[0])]
    act, saved = xs, []
    nxt_done = None
    for i in range(depth):
        hooks = Hooks()
        if i == 0:
            hooks.add("start", rest.ici_start)
            hooks.add("pre_conf", rest.ici_wait_d2d_start)
            hooks.add("pre_merge", rest.d2d_wait)
            hooks.add("pre_merge", lambda after, env: layers[0].update(late_weights(rest.got)))
        if i + 1 < depth:
            nxt = SplitAllGather(placed_shards(i + 1))
            points = ("pre_merge", "post_ffn_in", None) if i == 0 else ("start", "post_merge", "post_ffn_in")
            hooks.add(points[0], nxt.ici_start)
            hooks.add(points[1], nxt.ici_wait_d2d_start)
            if points[2]:
                hooks.add(points[2], nxt.d2d_wait)
        act, sv = layer_fwd(act, p[i, 0], layers[i], hooks)
        saved.append(sv)
        if i + 1 < depth:
            if i == 0:
                nxt.d2d_wait(act)
            layers.append({**small_weights(i + 1, nxt.got[0]), **late_weights(nxt.got[1:])})
    loss_part, dx, dg_final = loss_head(act, g_final.reshape(1, -1), tgt)

    big_red = [None] * depth
    small_red = [None] * depth
    small_rows = 80
    pending = None

    def small_vector(i, small):
        parts = [small[n] for n in SMALL_NAMES]
        return _pack_rows(parts + ([dg_final, loss_part[0, :1]] if i == 0 else []), small_rows)

    for i in reversed(range(depth)):
        hooks = Hooks()
        result = {}
        if pending is not None:
            rs, j, (small_ex, small_finish) = pending
            hooks.add("start", lambda after, env, ex=small_ex: ex.start(after))
            hooks.add("start", lambda after, env, rs=rs: rs.swap_start(after))
            hooks.add("pre_ffn", lambda after, env, rs=rs: rs.swap_wait_send_start(after))
            hooks.add("pre_dw_out", lambda after, env, rs=rs: rs.send_wait_share_start(after))
            hooks.add("pre_conva", lambda after, env, rs=rs, result=result: result.update(prev=rs.share_wait(after)))
            hooks.add("pre_conva", lambda after, env, fin=small_finish, result=result: result.update(small=fin(after)))
        if i == 0:
            def early_start(after, env, result=result):
                result["rs"] = SplitReduceScatter([env["gfo"], env["gfi"]])
                return result["rs"].swap_start(after)

            hooks.add("pre_dw_out", early_start)
            hooks.add("pre_conva", lambda after, env, result=result: result["rs"].swap_wait_send_start(after))
            hooks.add("pre_dw_in", lambda after, env, result=result: result["rs"].send_wait_share_start(after))
            hooks.add("end", lambda after, env, result=result: result.update(early=result["rs"].share_wait(after)))
        dx, big, small = layer_bwd(dx, p[i, 0], layers[i], saved[i], hooks)
        if pending is not None:
            big_red[pending[1]] = unpack_big_grads(*result["prev"])
            small_red[pending[1]] = result["small"]
        if i > 0:
            pending = (SplitReduceScatter(big), i, allreduce8_split(small_vector(i, small)))

    ga, _, gb, gin, _ = big
    late = SplitReduceScatter([ga, gb, gin])
    small_ex, small_finish = allreduce8_split(small_vector(0, small))
    upd = {}

    def update_upper(names, after):
        for name in names:
            upd[name] = adamw_layers(weights[name], [big_red[i][name] for i in range(1, depth)], m_in[name], v_in[name],
                                     1, after=after)
        return [upd[name][0] for name in names]

    done = update_upper(("w_in",), late.swap_start(small_ex.start(dx)))
    done = update_upper(("w_ffn_in", "w_merge_gate", "w_ffn_out"), late.swap_wait_send_start(done))
    small_red[0] = small_finish(done)
    done = update_upper(("w_branch", "w_out", "w_ple_gate", "w_ple_proj"), late.send_wait_share_start(done))
    ga, gb, gin = late.share_wait(done)
    gfo, gfi = result["early"]
    big_red[0] = unpack_big_grads(ga, gfo, gb, gin, gfi)

    layer_shapes = [small[n].shape for n in SMALL_NAMES]
    per_layer = [_unpack_rows(small_red[i], layer_shapes + ([dg_final.shape, (1,)] if i == 0 else []))
                 for i in range(depth)]
    grads = {n: jnp.stack([per_layer[i][k].reshape(weights[n].shape[1:] if n not in ("conv_a", "conf_dw")
                                                   else per_layer[i][k].shape) for i in range(depth)])
             for k, n in enumerate(SMALL_NAMES)}
    grads["g_final"] = per_layer[0][-2].reshape(-1)
    loss = per_layer[0][-1].reshape(())
    for n in ("conv_a", "conf_dw"):
        grads[n] = lax.dynamic_slice_in_dim(grads[n], my_shard * cw, cw, axis=2)

    small_all = [n for n in order if n not in BIG_NAMES]
    sm_shapes = [weights[n].shape for n in small_all]
    n_sm = sum(math.prod(sh) for sh in sm_shapes)
    sm_rows = -(-n_sm // (8 * D_MODEL)) * 8
    packed = [_pack_rows([src[n] for n in small_all], sm_rows) for src in (weights, grads, m_in, v_in)]
    sm_out = [_unpack_rows(o, sm_shapes) for o in adamw(*packed)]
    delta, new_m, new_v = ({n: o[k] for k, n in enumerate(small_all)} for o in sm_out)
    for name in BIG_NAMES:
        delta[name], new_m[name], new_v[name], grads[name] = adamw_layers(
            weights[name], [big_red[0][name]], m_in[name], v_in[name], 0, into=upd[name])

    return (loss, dx[None], *[grads[n] for n in order], *[delta[n] for n in order], *[new_m[n] for n in order],
            *[new_v[n] for n in order])
```

```python
import functools
import math

import jax
import jax.numpy as jnp
from jax import lax
from jax.experimental import pallas as pl
from jax.experimental.pallas import tpu as pltpu

F32 = jnp.float32
BF16 = jnp.bfloat16
EPS = 1e-6
D_MODEL = 1024
BW = 256
N_BR = 4
N_IN = 10 * BW
FFN_H = 2816
N_SH = 4
HEADS = 4
HEAD_D = 64
BLK = 128
DILATIONS = (1, 4, 16)
CONF_K = 31
CONVA_K = 3
NEG = -1e30
VMEM_LIMIT = 56 * 1024 * 1024
MESH = pl.DeviceIdType.MESH

ADAM_LR, ADAM_B1, ADAM_B2, ADAM_EPS, ADAM_WD, ADAM_STEP = 0.001, 0.9, 0.999, 1e-08, 0.01, 10

bs = pl.BlockSpec
ANY = pl.BlockSpec(memory_space=pl.ANY)


def _call(body, name, grid, in_specs, out_specs, out_shape, scratch=(), aliases=None, after=None):
    n_in = len(in_specs)
    kernel_body = body
    if after is not None:
        in_specs = list(in_specs) + [ANY]

        def kernel_body(*refs):
            return body(*refs[:n_in], *refs[n_in + 1:])

    call = pl.pallas_call(
        kernel_body, name=name, grid=grid, in_specs=in_specs, out_specs=out_specs, out_shape=out_shape,
        scratch_shapes=list(scratch), input_output_aliases=aliases or {},
        compiler_params=pltpu.CompilerParams(dimension_semantics=("arbitrary",) * len(grid),
                                             vmem_limit_bytes=VMEM_LIMIT))
    return call if after is None else (lambda *args: call(*args, after))


def _sds(shape, dtype):
    return jax.ShapeDtypeStruct(shape, dtype)


def _nn(a, b):
    return jnp.dot(a, b, preferred_element_type=F32)


def _nt(a, b):
    return lax.dot_general(a, b, (((1,), (1,)), ((), ())), preferred_element_type=F32)


def _tn(a, b):
    return lax.dot_general(a, b, (((0,), (0,)), ((), ())), preferred_element_type=F32)


def _sigmoid(x):
    return 1.0 / (1.0 + jnp.exp(-x))


def _rms_fwd(x, g):
    r = lax.rsqrt(jnp.mean(x * x, axis=-1, keepdims=True) + EPS)
    return x * r * g


def _rms_bwd(dh, x, g):
    r = lax.rsqrt(jnp.mean(x * x, axis=-1, keepdims=True) + EPS)
    xr = x * r
    dxr = dh * g
    dx = r * (dxr - xr * jnp.mean(dxr * xr, axis=-1, keepdims=True))
    return dx, dh * xr


def _ln_hat(x):
    mu = jnp.mean(x, axis=-1, keepdims=True)
    xc = x - mu
    r = lax.rsqrt(jnp.mean(xc * xc, axis=-1, keepdims=True) + EPS)
    return xc * r, r


def _ln_bwd(dy, xhat, r, g):
    dxh = dy * g
    return r * (dxh - jnp.mean(dxh, axis=-1, keepdims=True) - xhat * jnp.mean(dxh * xhat, axis=-1, keepdims=True))


def _colsum(v):
    return jnp.sum(v, axis=0, keepdims=True)


def _causal_conv(zext, w_ref, k_taps, halo):
    acc = zext[halo:] * w_ref[k_taps - 1:k_taps, :]
    for k in range(k_taps - 1):
        acc = acc + pltpu.roll(zext, k_taps - 1 - k, 0)[halo:] * w_ref[k:k + 1, :]
    return acc


def _anti_conv(dext, w_ref, k_taps, tm):
    n = dext.shape[0]
    acc = dext[:tm] * w_ref[k_taps - 1:k_taps, :]
    for s in range(1, k_taps):
        acc = acc + pltpu.roll(dext, n - s, 0)[:tm] * w_ref[k_taps - 1 - s:k_taps - s, :]
    return acc


def _conv_wgrad(dw_ref, dc, zext, k_taps, halo):
    dw_ref[k_taps - 1:k_taps, :] += _colsum(dc * zext[halo:])
    for k in range(k_taps - 1):
        dw_ref[k:k + 1, :] += _colsum(dc * pltpu.roll(zext, k_taps - 1 - k, 0)[halo:])


LANES = 128


def _to_strided_view(dst_ref, chunk, scr, d, width):
    n = scr.shape[0] // d
    for c in range(width // LANES):
        scr[...] = chunk(c)
        for r in range(d):
            dst_ref[:, r * width + c * LANES:r * width + (c + 1) * LANES] = scr[pl.ds(r, n, stride=d), :].astype(dst_ref.dtype)


def _from_strided_view(src_ref, scr, d, width, c):
    n = scr.shape[0] // d
    for r in range(d):
        scr[pl.ds(r, n, stride=d), :] = src_ref[:, r * width + c * LANES:r * width + (c + 1) * LANES].astype(F32)
    return scr[...]


def _view_spec(tm, d, width):
    return bs((tm // d, d * width), lambda i: (i, 0))


def _prev_blk(i, per):
    return jnp.maximum(i * per - 1, 0)


def _next_blk(i, per, last):
    return jnp.minimum((i + 1) * per, last)


def norm_in_proj(x, g, win, after=None):
    t = x.shape[0]
    tm = min(512, t)
    ns = win.shape[2]

    def body(x_ref, g_ref, w_ref, h_ref, o_ref, q_ref, q4_ref, q16_ref, scr):
        h = _rms_fwd(x_ref[...], g_ref[...]).astype(BF16)
        h_ref[...] = h
        parts = []
        for s in range(N_SH):
            r = _nn(h, w_ref[s])
            o_ref[:, s * ns:(s + 1) * ns] = r
            if s == 1:
                parts.append(r[:, 3 * BW - ns:])
            if s == 2:
                parts.append(r[:, :6 * BW - 2 * ns])
        qf = jnp.concatenate(parts, axis=1)
        q_ref[...] = qf.astype(BF16)
        chunk = lambda c: qf[:, c * LANES:(c + 1) * LANES]
        _to_strided_view(q4_ref, chunk, scr, 4, 3 * BW)
        _to_strided_view(q16_ref, chunk, scr, 16, 3 * BW)

    row = lambda c: bs((tm, c), lambda i: (i, 0))
    return _call(
        body, "norm_in_proj", (t // tm,), [row(D_MODEL), bs((1, D_MODEL), lambda i: (0, 0)), _resident(win)],
        [row(D_MODEL), row(N_IN), row(3 * BW), _view_spec(tm, 4, 3 * BW), _view_spec(tm, 16, 3 * BW)],
        [_sds((t, D_MODEL), BF16), _sds((t, N_IN), F32), _sds((t, 3 * BW), BF16),
         _sds((t // 4, 4 * 3 * BW), BF16), _sds((t // 16, 16 * 3 * BW), BF16)],
        scratch=[pltpu.VMEM((tm, LANES), F32)], after=after)(x, g, win)


def merge_fwd(h, ys, wg, wbr, after=None):
    t = h.shape[0]
    tm = min(512, t)

    def body(h_ref, ya, yb, yc, yd, wg_ref, wb_ref, m_ref, g_ref, b_ref):
        hh = h_ref[...]
        for j in range(N_SH):
            cs = slice(j * BW, (j + 1) * BW)
            acc = None
            for k, y_ref in enumerate((ya, yb, yc, yd)):
                g = _sigmoid(_nn(hh, wg_ref[k, :, cs]))
                b = _nn(y_ref[...], wb_ref[j, k])
                g_ref[k, :, cs] = g.astype(BF16)
                b_ref[k, :, cs] = b.astype(BF16)
                acc = g * b if acc is None else acc + g * b
            m_ref[:, cs] = acc.astype(BF16)

    ysp = bs((tm, BW), lambda i: (i, 0))
    big = bs((N_BR, tm, D_MODEL), lambda i: (0, i, 0))
    return _call(
        body, "merge_fwd", (t // tm,),
        [bs((tm, D_MODEL), lambda i: (i, 0)), ysp, ysp, ysp, ysp, _resident(wg), _resident(wbr)],
        [bs((tm, D_MODEL), lambda i: (i, 0)), big, big],
        [_sds((t, D_MODEL), BF16), _sds((N_BR, t, D_MODEL), BF16), _sds((N_BR, t, D_MODEL), BF16)], after=after)(
            h, *ys, wg, wbr)


def ffn_in(a, wout, res, g, wfi, after=None):
    t = res.shape[0]
    tm = min(256, t)
    ns = wfi.shape[2]

    def body(a_ref, wo_ref, r_ref, g_ref, w_ref, x_ref, h_ref, f_ref, act_ref):
        xv = r_ref[...] + _nn(a_ref[...], wo_ref[...])
        x_ref[...] = xv
        h = _rms_fwd(xv, g_ref[...]).astype(BF16)
        h_ref[...] = h
        for j in range(2):
            cs = slice(j * ns, (j + 1) * ns)
            fg = _nn(h, w_ref[j])
            fu = _nn(h, w_ref[j + 2])
            f_ref[0, :, cs] = fg.astype(BF16)
            f_ref[1, :, cs] = fu.astype(BF16)
            act_ref[:, cs] = (fg * _sigmoid(fg) * fu).astype(BF16)

    row = lambda c: bs((tm, c), lambda i: (i, 0))
    return _call(
        body, "ffn_in", (t // tm,),
        [row(D_MODEL), _resident(wout), row(D_MODEL), bs((1, D_MODEL), lambda i: (0, 0)), _resident(wfi)],
        [row(D_MODEL), row(D_MODEL), bs((2, tm, FFN_H), lambda i: (0, i, 0)), row(FFN_H)],
        [_sds((t, D_MODEL), F32), _sds((t, D_MODEL), BF16), _sds((2, t, FFN_H), BF16), _sds((t, FFN_H), BF16)],
        after=after)(a, wout, res, g, wfi)


def ple_fwd(a, wfo, res, g, wpg, p_i, wpp, after=None):
    t = res.shape[0]
    tm = min(512, t)

    def body(a_ref, wo_ref, r_ref, g_ref, wg_ref, p_ref, wp_ref, x_ref, h_ref, gt_ref, pp_ref, o_ref):
        xv = r_ref[...] + _nn(a_ref[...], wo_ref[...])
        x_ref[...] = xv
        h = _rms_fwd(xv, g_ref[...]).astype(BF16)
        h_ref[...] = h
        gate = _sigmoid(_nn(h, wg_ref[...]))
        pb = p_ref[...].astype(BF16)
        pp = jnp.concatenate([_nn(pb, wp_ref[j]) for j in range(N_SH)], axis=1)
        gt_ref[...] = gate.astype(BF16)
        pp_ref[...] = pp.astype(BF16)
        o_ref[...] = xv + gate * pp

    row = bs((tm, D_MODEL), lambda i: (i, 0))
    return _call(
        body, "ple_fwd", (t // tm,),
        [bs((tm, FFN_H), lambda i: (i, 0)), _resident(wfo), row, bs((1, D_MODEL), lambda i: (0, 0)), _resident(wpg),
         bs((tm, BW), lambda i: (i, 0)), _resident(wpp)],
        [row, row, row, row, row],
        [_sds((t, D_MODEL), F32), _sds((t, D_MODEL), BF16), _sds((t, D_MODEL), BF16), _sds((t, D_MODEL), BF16),
         _sds((t, D_MODEL), F32)], after=after)(a, wfo, res, g, wpg, p_i, wpp)


def loss_head(x, g, tgt):
    t = x.shape[0]
    tm = min(512, t)

    def body(x_ref, g_ref, t_ref, l_ref, dx_ref, dg_ref):
        @pl.when(pl.program_id(0) == 0)
        def _():
            l_ref[...] = jnp.zeros_like(l_ref)
            dg_ref[...] = jnp.zeros_like(dg_ref)

        xv, gv = x_ref[...], g_ref[...]
        err = _rms_fwd(xv, gv) - t_ref[...]
        part = 0.5 * jnp.sum(jnp.mean(err * err, axis=-1, keepdims=True), axis=0, keepdims=True)
        l_ref[...] += jnp.broadcast_to(part, l_ref.shape)
        dx, dgr = _rms_bwd(err * (1.0 / D_MODEL), xv, gv)
        dx_ref[...] = dx
        dg_ref[...] += _colsum(dgr)

    row = bs((tm, D_MODEL), lambda i: (i, 0))
    vec = bs((1, D_MODEL), lambda i: (0, 0))
    return _call(body, "loss_head", (t // tm,), [row, vec, row],
                 [bs((1, 128), lambda i: (0, 0)), row, vec],
                 [_sds((1, 128), F32), _sds((t, D_MODEL), F32), _sds((1, D_MODEL), F32)])(x, g, tgt)


def tn_matmul(name, a, b, grid, a_spec, b_spec, out_spec, out_shape, split=0, split_cols=0, into=None, after=None):
    last = len(grid) - 1

    def body(a_ref, b_ref, *rest):
        o_ref = rest[-1]

        @pl.when(pl.program_id(last) == 0)
        def _():
            o_ref[...] = jnp.zeros_like(o_ref)

        res = _tn(a_ref[...].astype(BF16), b_ref[...].astype(BF16))
        if split_cols:
            cols = res.shape[1] // split_cols
            for s in range(split_cols):
                o_ref[s] += res[:, s * cols:(s + 1) * cols]
        elif split:
            rows = res.shape[0] // split
            for s in range(split):
                o_ref[s] += res[s * rows:(s + 1) * rows]
        else:
            o_ref[...] += res

    if into is None:
        return _call(body, name, grid, [a_spec, b_spec], out_spec, out_shape, after=after)(a, b)
    return _call(body, name, grid, [a_spec, b_spec, ANY], out_spec, out_shape, aliases={2: 0}, after=after)(a, b, into)


def _resident(w):
    zeros = (0,) * w.ndim
    return bs(w.shape, lambda i: zeros, pipeline_mode=pl.Buffered(1))


def norm_bwd(name, sources, dx_in, x, g):
    t = x.shape[0]
    tm = min(512, t)
    n_src = len(sources)

    def body(*refs):
        dxi_ref, x_ref, g_ref, dx_ref, dg_ref = refs[2 * n_src:]

        @pl.when(pl.program_id(0) == 0)
        def _():
            dg_ref[...] = jnp.zeros_like(dg_ref)

        dh = None
        for si in range(n_src):
            for av, wv in sources[si][3](refs[2 * si], refs[2 * si + 1]):
                part = _nt(av, wv)
                dh = part if dh is None else dh + part
        dx, dgr = _rms_bwd(dh, x_ref[...], g_ref[...])
        dx_ref[...] = dxi_ref[...] + dx
        dg_ref[...] += _colsum(dgr)

    in_specs, args = [], []
    for a, a_spec, w, _ in sources:
        in_specs += [a_spec(tm), _resident(w)]
        args += [a, w]
    row = bs((tm, D_MODEL), lambda i: (i, 0))
    vec = bs((1, D_MODEL), lambda i: (0, 0))
    return _call(body, name, (t // tm,), in_specs + [row, row, vec], [row, vec],
                 [_sds((t, D_MODEL), F32), _sds((1, D_MODEL), F32)])(*args, dx_in, x, g)


def ple_bwd(dx, gate, pp, wpg, x, g, after=None):
    t = dx.shape[0]
    tm = min(512, t)

    def body(dx_ref, gt_ref, p_ref, w_ref, x_ref, g_ref, dpre_ref, dpp_ref, o_ref, dg_ref):
        @pl.when(pl.program_id(0) == 0)
        def _():
            dg_ref[...] = jnp.zeros_like(dg_ref)

        d = dx_ref[...]
        gt = gt_ref[...].astype(F32)
        dpre = (d * p_ref[...].astype(F32) * gt * (1.0 - gt)).astype(BF16)
        dpre_ref[...] = dpre
        dpp_ref[...] = (d * gt).astype(BF16)
        dxn, dgr = _rms_bwd(_nt(dpre, w_ref[...]), x_ref[...], g_ref[...])
        o_ref[...] = d + dxn
        dg_ref[...] += _colsum(dgr)

    row = bs((tm, D_MODEL), lambda i: (i, 0))
    vec = bs((1, D_MODEL), lambda i: (0, 0))
    return _call(body, "ple_bwd", (t // tm,), [row, row, row, _resident(wpg), row, vec], [row, row, row, vec],
                 [_sds((t, D_MODEL), BF16), _sds((t, D_MODEL), BF16), _sds((t, D_MODEL), F32), _sds((1, D_MODEL), F32)],
                 after=after)(dx, gate, pp, wpg, x, g)


def ffn_bwd(dx, wfo, fgu, wfi, x, g, after=None):
    t = dx.shape[0]
    tm = min(256, t)
    ns = FFN_H // 2

    def body(dx_ref, wo_ref, f_ref, wi_ref, x_ref, g_ref, df_ref, o_ref, dg_ref):
        @pl.when(pl.program_id(0) == 0)
        def _():
            dg_ref[...] = jnp.zeros_like(dg_ref)

        d = dx_ref[...]
        dxb = d.astype(BF16)
        dh = None
        for j in range(2):
            cs = slice(j * ns, (j + 1) * ns)
            dact = _nt(dxb, wo_ref[cs, :])
            fg = f_ref[0, :, cs].astype(F32)
            fu = f_ref[1, :, cs].astype(F32)
            s = _sigmoid(fg)
            dfg = (dact * fu * (s * (1.0 + fg * (1.0 - s)))).astype(BF16)
            dfu = (dact * fg * s).astype(BF16)
            df_ref[0, :, cs] = dfg
            df_ref[1, :, cs] = dfu
            part = _nt(dfg, wi_ref[j]) + _nt(dfu, wi_ref[j + 2])
            dh = part if dh is None else dh + part
        dxn, dgr = _rms_bwd(dh, x_ref[...], g_ref[...])
        o_ref[...] = d + dxn
        dg_ref[...] += _colsum(dgr)

    blk = bs((2, tm, FFN_H), lambda i: (0, i, 0))
    row = bs((tm, D_MODEL), lambda i: (i, 0))
    vec = bs((1, D_MODEL), lambda i: (0, 0))
    return _call(body, "ffn_bwd", (t // tm,), [row, _resident(wfo), blk, _resident(wfi), row, vec], [blk, row, vec],
                 [_sds((2, t, FFN_H), BF16), _sds((t, D_MODEL), F32), _sds((1, D_MODEL), F32)],
                 after=after)(dx, wfo, fgu, wfi, x, g)


def merge_bwd(dx, wout, gates, ybr, wbr, ys, gb):
    t = dx.shape[0]
    tm = min(256, t)

    def body(dx_ref, w_ref, g_ref, b_ref, wb_ref, ya, yb, yc, yd, _, dpre_ref, dy_ref, gb_ref):
        @pl.when(pl.program_id(0) == 0)
        def _():
            gb_ref[...] = jnp.zeros_like(gb_ref)

        dm = _nt(dx_ref[...].astype(BF16), w_ref[...])
        for k, y_ref in enumerate((ya, yb, yc, yd)):
            g = g_ref[k].astype(F32)
            dpre_ref[k] = (dm * b_ref[k].astype(F32) * g * (1.0 - g)).astype(BF16)
            dyb = (dm * g).astype(BF16)
            acc = None
            for s in range(N_SH):
                part = _nt(dyb[:, s * BW:(s + 1) * BW], wb_ref[s, k])
                acc = part if acc is None else acc + part
            dy_ref[k] = acc
            dwb = _tn(y_ref[...], dyb)
            for s in range(N_SH):
                gb_ref[s, k * BW:(k + 1) * BW, :] += dwb[:, s * BW:(s + 1) * BW]

    blk = bs((N_BR, tm, D_MODEL), lambda i: (0, i, 0))
    ysp = bs((tm, BW), lambda i: (i, 0))
    return _call(body, "merge_bwd", (t // tm,),
                 [bs((tm, D_MODEL), lambda i: (i, 0)), _resident(wout), blk, blk, _resident(wbr), ysp, ysp, ysp, ysp, ANY],
                 [blk, bs((N_BR, tm, BW), lambda i: (0, i, 0)), bs((N_SH, N_BR * BW, BW), lambda i: (0, 0, 0))],
                 [_sds((N_BR, t, D_MODEL), BF16), _sds((N_BR, t, BW), F32), _sds(gb.shape, F32)],
                 aliases={9: 2})(dx, wout, gates, ybr, wbr, *ys, gb)


def conva_fwd(proj, wa):
    t = proj.shape[0]
    tm, halo = min(512, t), 8
    per = tm // halo

    def body(b_ref, c_ref, x_ref, ch_ref, xh_ref, w_ref, y_ref):
        zh = jnp.where(pl.program_id(0) > 0, ch_ref[...] * xh_ref[...], 0.0)
        zext = jnp.concatenate([zh, c_ref[...] * x_ref[...]], axis=0)
        y_ref[...] = (b_ref[...] * _causal_conv(zext, w_ref, CONVA_K, halo)).astype(BF16)

    col = lambda c: bs((tm, BW), lambda i: (i, c))
    hal = lambda c: bs((halo, BW), lambda i: (_prev_blk(i, per), c))
    return _call(body, "conva_fwd", (t // tm,),
                 [col(0), col(1), col(2), hal(1), hal(2), bs((CONVA_K, BW), lambda i: (0, 0))],
                 bs((tm, BW), lambda i: (i, 0)), _sds((t, BW), BF16))(proj, proj, proj, proj, proj, wa)


def conva_bwd(proj, dys, wa, after=None):
    t = proj.shape[0]
    tm, halo = min(512, t), 8
    per = tm // halo
    last = t // halo - 1
    nt = t // tm

    def body(b_ref, c_ref, x_ref, ch_ref, xh_ref, bn_ref, dy_ref, dyn_ref, w_ref, o_ref, dw_ref):
        i = pl.program_id(0)

        @pl.when(i == 0)
        def _():
            dw_ref[...] = jnp.zeros_like(dw_ref)

        zh = jnp.where(i > 0, ch_ref[...] * xh_ref[...], 0.0)
        cv, xv = c_ref[...], x_ref[...]
        zext = jnp.concatenate([zh, cv * xv], axis=0)
        dy = dy_ref[...]
        dconv = dy * b_ref[...]
        dcn = jnp.where(i < nt - 1, dyn_ref[...] * bn_ref[...], 0.0)
        dz = _anti_conv(jnp.concatenate([dconv, dcn], axis=0), w_ref, CONVA_K, tm)
        o_ref[:, :BW] = (dy * _causal_conv(zext, w_ref, CONVA_K, halo)).astype(BF16)
        o_ref[:, BW:2 * BW] = (dz * xv).astype(BF16)
        o_ref[:, 2 * BW:] = (dz * cv).astype(BF16)
        _conv_wgrad(dw_ref, dconv, zext, CONVA_K, halo)

    col = lambda c: bs((tm, BW), lambda i: (i, c))
    hal = lambda c: bs((halo, BW), lambda i: (_prev_blk(i, per), c))
    nxt = bs((halo, BW), lambda i: (_next_blk(i, per, last), 0))
    wsp = bs((CONVA_K, BW), lambda i: (0, 0))
    outs = _call(body, "conva_bwd", (t // tm,),
                 [col(0), col(1), col(2), hal(1), hal(2), nxt,
                  bs((None, tm, BW), lambda i: (0, i, 0)), bs((None, halo, BW), lambda i: (0, _next_blk(i, per, last), 0)), wsp],
                 [bs((tm, 3 * BW), lambda i: (i, 0)), wsp],
                 [_sds((t, N_IN), BF16), _sds((CONVA_K, BW), F32)], after=after)(
                     proj, proj, proj, proj, proj, proj, dys, dys, wa)
    return outs[0], outs[1]


def _head_masks():
    lane = lax.broadcasted_iota(jnp.int32, (1, BW), 1)
    return [(lane >= h * HEAD_D) & (lane < (h + 1) * HEAD_D) for h in range(HEADS)]


def _band_masks():
    qi = lax.broadcasted_iota(jnp.int32, (BLK, BLK), 0)
    ki = lax.broadcasted_iota(jnp.int32, (BLK, BLK), 1)
    return ki >= qi, ki <= qi


def attn_fwd_group(pv, d):
    rows = pv.shape[0]
    qb = min(512, rows)
    nb = qb // BLK
    scale = HEAD_D ** -0.5

    def body(q_ref, k_ref, v_ref, kh_ref, vh_ref, o_ref):
        n = pl.program_id(1)
        hm = _head_masks()
        m_prev, m_cur = _band_masks()
        for b in range(nb):
            rs = slice(b * BLK, (b + 1) * BLK)
            q = q_ref[rs, :]
            if b == 0:
                kp, vp = kh_ref[...], vh_ref[...]
                mp = m_prev & (n > 0)
            else:
                ps = slice((b - 1) * BLK, b * BLK)
                kp, vp = k_ref[ps, :], v_ref[ps, :]
                mp = m_prev
            qs = jnp.concatenate([jnp.where(hm[h], q, 0.0).astype(BF16) for h in range(HEADS)], axis=0)
            kcat = jnp.concatenate([kp, k_ref[rs, :]], axis=0)
            vcat = jnp.concatenate([vp, v_ref[rs, :]], axis=0)
            band = jnp.concatenate([mp, m_cur], axis=1)
            s = jnp.where(jnp.concatenate([band] * HEADS, axis=0), _nt(qs, kcat) * scale, NEG)
            m = jnp.max(s, axis=-1, keepdims=True)
            e = jnp.exp(s - m)
            l = jnp.sum(e, axis=-1, keepdims=True)
            of = _nn(e.astype(BF16), vcat) / l
            lse = m + jnp.log(l)
            o_acc = jnp.zeros((BLK, BW), F32)
            l_acc = jnp.zeros((BLK, BW), F32)
            for h in range(HEADS):
                hs = slice(h * BLK, (h + 1) * BLK)
                o_acc = jnp.where(hm[h], of[hs, :], o_acc)
                l_acc = jnp.where(hm[h], lse[hs, :], l_acc)
            o_ref[rs, :BW] = o_acc
            o_ref[rs, BW:] = l_acc

    per = qb // BLK
    main = lambda c: bs((qb, BW), lambda r, n: (n, r * 3 + c))
    hal = lambda c: bs((BLK, BW), lambda r, n: (_prev_blk(n, per), r * 3 + c))
    return _call(body, f"attn_fwd_d{d}", (d, rows // qb), [main(0), main(1), main(2), hal(1), hal(2)],
                 bs((qb, 2 * BW), lambda r, n: (n, r)), _sds((rows, d * 2 * BW), F32))(pv, pv, pv, pv, pv)


def attn_merge(ols):
    t = ols[0].shape[0]
    tm = min(512, t)
    width = 2 * BW

    def lse3(a, b, c):
        m = jnp.maximum(jnp.maximum(a, b), c)
        return m + jnp.log(jnp.exp(a - m) + jnp.exp(b - m) + jnp.exp(c - m))

    def body(g0, g1, g2, y_ref, o_ref, l_ref, scr, nat1, nat2):
        for src, nat, d in ((g1, nat1, DILATIONS[1]), (g2, nat2, DILATIONS[2])):
            for c in range(width // LANES):
                nat[:, c * LANES:(c + 1) * LANES] = _from_strided_view(src, scr, d, width, c)
        gs = [g0[...], nat1[...], nat2[...]]
        ls = [g[:, BW:] for g in gs]
        tot = lse3(*ls)
        o = (jnp.exp(ls[0] - tot) * gs[0][:, :BW] + jnp.exp(ls[1] - tot) * gs[1][:, :BW]
             + jnp.exp(ls[2] - tot) * gs[2][:, :BW])
        y_ref[...] = o.astype(BF16)
        o_ref[...] = o
        l_ref[...] = tot

    n = bs((tm, BW), lambda i: (i, 0))
    return _call(body, "attn_merge", (t // tm,),
                 [_view_spec(tm, 1, width), _view_spec(tm, DILATIONS[1], width), _view_spec(tm, DILATIONS[2], width)],
                 [n, n, n], [_sds((t, BW), BF16), _sds((t, BW), F32), _sds((t, BW), F32)],
                 scratch=[pltpu.VMEM((tm, LANES), F32), pltpu.VMEM((tm, width), F32), pltpu.VMEM((tm, width), F32)])(*ols)


def attn_delta(dys, o, lse):
    t = o.shape[0]
    tm = min(512, t)

    def body(d_ref, o_ref, l_ref, ld1, ld4, ld16, dy4, dy16, scr):
        hm = _head_masks()
        dy = d_ref[...]
        prod = dy * o_ref[...]
        delta = jnp.zeros_like(prod)
        for h in range(HEADS):
            delta = jnp.where(hm[h], jnp.sum(jnp.where(hm[h], prod, 0.0), axis=-1, keepdims=True), delta)
        ld = jnp.concatenate([l_ref[...], delta], axis=1)
        ld1[...] = ld
        for d, ld_v, dy_v in ((DILATIONS[1], ld4, dy4), (DILATIONS[2], ld16, dy16)):
            _to_strided_view(ld_v, lambda c: ld[:, c * LANES:(c + 1) * LANES], scr, d, 2 * BW)
            _to_strided_view(dy_v, lambda c: dy[:, c * LANES:(c + 1) * LANES], scr, d, BW)

    n = bs((tm, BW), lambda i: (i, 0))
    d4, d16 = DILATIONS[1], DILATIONS[2]
    outs = _call(body, "attn_delta", (t // tm,), [bs((None, tm, BW), lambda i: (1, i, 0)), n, n],
                 [_view_spec(tm, 1, 2 * BW), _view_spec(tm, d4, 2 * BW), _view_spec(tm, d16, 2 * BW),
                  _view_spec(tm, d4, BW), _view_spec(tm, d16, BW)],
                 [_sds((t, 2 * BW), F32), _sds((t // d4, d4 * 2 * BW), F32), _sds((t // d16, d16 * 2 * BW), F32),
                  _sds((t // d4, d4 * BW), F32), _sds((t // d16, d16 * BW), F32)],
                 scratch=[pltpu.VMEM((tm, LANES), F32)])(dys, o, lse)
    return outs[:3], outs[3:]


def attn_bwd_group(pv, dov, ldv, d):
    rows = pv.shape[0]
    qb = min(512, rows)
    nb = qb // BLK
    nsteps = rows // qb
    scale = HEAD_D ** -0.5

    def body(q_ref, qn_ref, k_ref, kh_ref, v_ref, vh_ref, do_ref, don_ref, ld_ref, ldn_ref, o_ref):
        n = pl.program_id(1)
        hm = _head_masks()
        m_prev, m_cur = _band_masks()
        has_prev, has_next = n > 0, n < nsteps - 1
        dq = [None] * nb
        dk = [jnp.zeros((BLK, BW), F32) for _ in range(nb)]
        dvv = [jnp.zeros((BLK, BW), F32) for _ in range(nb)]
        for qi in range(nb + 1):
            rs = slice(qi * BLK, (qi + 1) * BLK)
            ps = slice((qi - 1) * BLK, qi * BLK)
            if qi < nb:
                q, do, ldq = q_ref[rs, :], do_ref[rs, :], ld_ref[rs, :]
            else:
                q, do, ldq = qn_ref[...], don_ref[...], ldn_ref[...]
            kp, vp = (kh_ref[...], vh_ref[...]) if qi == 0 else (k_ref[ps, :], v_ref[ps, :])
            kc, vc = (k_ref[rs, :], v_ref[rs, :]) if qi < nb else (kp, vp)
            mp = m_prev & has_prev if qi == 0 else (m_prev & has_next if qi == nb else m_prev)
            mc = m_cur if qi < nb else jnp.zeros_like(m_cur)
            band = jnp.concatenate([jnp.concatenate([mp, mc], axis=1)] * HEADS, axis=0)
            qs = jnp.concatenate([jnp.where(hm[h], q, 0.0).astype(BF16) for h in range(HEADS)], axis=0)
            dos = jnp.concatenate([jnp.where(hm[h], do, 0.0).astype(BF16) for h in range(HEADS)], axis=0)
            kcat = jnp.concatenate([kp, kc], axis=0)
            vcat = jnp.concatenate([vp, vc], axis=0)
            col = lambda v, h: jnp.broadcast_to(jnp.max(jnp.where(hm[h], v, NEG), axis=-1, keepdims=True), (BLK, 2 * BLK))
            lcols = jnp.concatenate([col(ldq[:, :BW], h) for h in range(HEADS)], axis=0)
            dcols = jnp.concatenate([col(ldq[:, BW:], h) for h in range(HEADS)], axis=0)
            p = jnp.where(band, jnp.exp(_nt(qs, kcat) * scale - lcols), 0.0)
            ds = (p * (_nt(dos, vcat) - dcols) * scale).astype(BF16)
            if qi < nb:
                dqf = _nn(ds, kcat)
                acc_q = jnp.zeros((BLK, BW), F32)
                for h in range(HEADS):
                    acc_q = jnp.where(hm[h], dqf[h * BLK:(h + 1) * BLK, :], acc_q)
                dq[qi] = acc_q
            dkc = _tn(ds, qs)
            dvc = _tn(p.astype(BF16), dos)
            if qi >= 1:
                dk[qi - 1] = dk[qi - 1] + dkc[:BLK]
                dvv[qi - 1] = dvv[qi - 1] + dvc[:BLK]
            if qi < nb:
                dk[qi] = dk[qi] + dkc[BLK:]
                dvv[qi] = dvv[qi] + dvc[BLK:]
        for b in range(nb):
            rs = slice(b * BLK, (b + 1) * BLK)
            for c, val in enumerate((dq[b], dk[b], dvv[b])):
                cs = slice(c * BW, (c + 1) * BW)
                o_ref[rs, cs] = val

    per = qb // BLK
    last = rows // BLK - 1
    main = lambda c: bs((qb, BW), lambda r, n: (n, r * 3 + c))
    prv = lambda c: bs((BLK, BW), lambda r, n: (_prev_blk(n, per), r * 3 + c))
    nxt = lambda c: bs((BLK, BW), lambda r, n: (_next_blk(n, per, last), r * 3 + c))
    accs = bs((qb, 3 * BW), lambda r, n: (n, r))
    in_specs = [main(0), nxt(0), main(1), prv(1), main(2), prv(2),
                bs((qb, BW), lambda r, n: (n, r)), bs((BLK, BW), lambda r, n: (_next_blk(n, per, last), r)),
                bs((qb, 2 * BW), lambda r, n: (n, r)), bs((BLK, 2 * BW), lambda r, n: (_next_blk(n, per, last), r))]
    args = [pv, pv, pv, pv, pv, pv, dov, dov, ldv, ldv]
    return _call(body, f"attn_bwd_d{d}", (d, nsteps), in_specs, accs, _sds((rows, d * 3 * BW), F32))(*args)


def attn_bwd_finish(parts, into):
    t = parts[0].shape[0]
    tm = min(512, t)
    width = 3 * BW

    def body(g0, g1, g2, _, o_ref, scr):
        for c in range(width // LANES):
            cs = slice(c * LANES, (c + 1) * LANES)
            acc = g0[:, cs]
            acc = acc + _from_strided_view(g1, scr, DILATIONS[1], width, c)
            acc = acc + _from_strided_view(g2, scr, DILATIONS[2], width, c)
            o_ref[:, cs] = acc.astype(BF16)

    return _call(body, "attn_bwd_finish", (t // tm,),
                 [_view_spec(tm, 1, width), _view_spec(tm, DILATIONS[1], width), _view_spec(tm, DILATIONS[2], width), ANY],
                 bs((tm, width), lambda i: (i, 1)), _sds(into.shape, BF16),
                 scratch=[pltpu.VMEM((tm, LANES), F32)], aliases={3: 0})(*parts, into)


def _group_masks():
    lane = lax.broadcasted_iota(jnp.int32, (1, BW), 1)
    return [(lane >= g * HEAD_D) & (lane < (g + 1) * HEAD_D) for g in range(4)]


def sgu_fwd(proj, ln_g, ln_b, w_tril, b_full):
    t = proj.shape[0]
    tm = min(512, t)

    def body(u_ref, v_ref, g_ref, b_ref, w_ref, bf_ref, y_ref):
        gm = _group_masks()
        xhat, _ = _ln_hat(v_ref[...])
        vb = (xhat * g_ref[...] + b_ref[...]).astype(BF16)
        for c in range(tm // BLK):
            rs = slice(c * BLK, (c + 1) * BLK)
            vc = vb[rs, :]
            mixed = bf_ref[...]
            for g in range(4):
                mixed = mixed + jnp.where(gm[g], _nn(w_ref[g], vc), 0.0)
            y_ref[rs, :] = (u_ref[rs, :] * mixed).astype(BF16)

    vec = bs((1, BW), lambda i: (0, 0))
    return _call(body, "sgu_fwd", (t // tm,),
                 [bs((tm, BW), lambda i: (i, 6)), bs((tm, BW), lambda i: (i, 7)), vec, vec,
                  bs((4, BLK, BLK), lambda i: (0, 0, 0)), bs((BLK, BW), lambda i: (0, 0))],
                 bs((tm, BW), lambda i: (i, 0)), _sds((t, BW), BF16))(proj, proj, ln_g, ln_b, w_tril, b_full)


def sgu_bwd(proj, dys, ln_g, ln_b, w_tril, b_full, into):
    t = proj.shape[0]
    tm = min(512, t)

    def body(u_ref, v_ref, dy_ref, g_ref, b_ref, w_ref, bf_ref, _, o_ref, dw_ref, dbf_ref, dg_ref, db_ref, dvl_ref):
        @pl.when(pl.program_id(0) == 0)
        def _():
            dw_ref[...] = jnp.zeros_like(dw_ref)
            dbf_ref[...] = jnp.zeros_like(dbf_ref)
            dg_ref[...] = jnp.zeros_like(dg_ref)
            db_ref[...] = jnp.zeros_like(db_ref)

        gm = _group_masks()
        xhat, r = _ln_hat(v_ref[...])
        gv = g_ref[...]
        vb = (xhat * gv + b_ref[...]).astype(BF16)
        for c in range(tm // BLK):
            rs = slice(c * BLK, (c + 1) * BLK)
            vc = vb[rs, :]
            dy = dy_ref[rs, :]
            mixed = bf_ref[...]
            for g in range(4):
                mixed = mixed + jnp.where(gm[g], _nn(w_ref[g], vc), 0.0)
            o_ref[rs, :BW] = (dy * mixed).astype(BF16)
            dm = dy * u_ref[rs, :]
            dbf_ref[...] += dm
            dvl = jnp.zeros((BLK, BW), F32)
            for g in range(4):
                dmg = jnp.where(gm[g], dm, 0.0).astype(BF16)
                dw_ref[g] += _nt(dmg, vc)
                dvl = dvl + _tn(w_ref[g], dmg)
            dvl_ref[rs, :] = dvl
        dvl = dvl_ref[...]
        o_ref[:, BW:] = _ln_bwd(dvl, xhat, r, gv).astype(BF16)
        dg_ref[...] += _colsum(dvl * xhat)
        db_ref[...] += _colsum(dvl)

    vec = bs((1, BW), lambda i: (0, 0))
    row = bs((tm, BW), lambda i: (i, 0))
    wsp = bs((4, BLK, BLK), lambda i: (0, 0, 0))
    bfs = bs((BLK, BW), lambda i: (0, 0))
    return _call(body, "sgu_bwd", (t // tm,),
                 [bs((tm, BW), lambda i: (i, 6)), bs((tm, BW), lambda i: (i, 7)), bs((None, tm, BW), lambda i: (2, i, 0)),
                  vec, vec, wsp, bfs, ANY],
                 [bs((tm, 2 * BW), lambda i: (i, 3)), wsp, bfs, vec, vec],
                 [_sds(into.shape, BF16), _sds((4, BLK, BLK), F32), _sds((BLK, BW), F32),
                  _sds((1, BW), F32), _sds((1, BW), F32)],
                 scratch=[pltpu.VMEM((tm, BW), F32)], aliases={7: 0})(proj, proj, dys, ln_g, ln_b, w_tril, b_full, into)


CONF_HALO = 32


def conf_fwd(proj, dw, ln_g, ln_b, after=None):
    t = proj.shape[0]
    tm, halo = min(512, t), CONF_HALO
    per = tm // halo

    def body(v_ref, gt_ref, vh_ref, gh_ref, w_ref, g_ref, b_ref, y_ref, z_ref):
        yh = jnp.where(pl.program_id(0) > 0, vh_ref[...] * _sigmoid(gh_ref[...]), 0.0)
        yext = jnp.concatenate([yh, v_ref[...] * _sigmoid(gt_ref[...])], axis=0)
        z = _causal_conv(yext, w_ref, CONF_K, halo)
        z_ref[...] = z
        xhat, _ = _ln_hat(z)
        ln = xhat * g_ref[...] + b_ref[...]
        y_ref[...] = (ln * _sigmoid(ln)).astype(BF16)

    vec = bs((1, BW), lambda i: (0, 0))
    col = lambda c: bs((tm, BW), lambda i: (i, c))
    hal = lambda c: bs((halo, BW), lambda i: (_prev_blk(i, per), c))
    row = bs((tm, BW), lambda i: (i, 0))
    return _call(body, "conf_fwd", (t // tm,),
                 [col(8), col(9), hal(8), hal(9), bs((CONF_K, BW), lambda i: (0, 0)), vec, vec],
                 [row, row], [_sds((t, BW), BF16), _sds((t, BW), F32)], after=after)(
                     proj, proj, proj, proj, dw, ln_g, ln_b)


def conf_bwd_ln(z, dys, ln_g, ln_b):
    t = z.shape[0]
    tm = min(1024, t)

    def body(z_ref, dy_ref, g_ref, b_ref, dz_ref, dg_ref, db_ref):
        @pl.when(pl.program_id(0) == 0)
        def _():
            dg_ref[...] = jnp.zeros_like(dg_ref)
            db_ref[...] = jnp.zeros_like(db_ref)

        gv = g_ref[...]
        xhat, r = _ln_hat(z_ref[...])
        ln = xhat * gv + b_ref[...]
        s = _sigmoid(ln)
        dln = dy_ref[...] * (s * (1.0 + ln * (1.0 - s)))
        dz_ref[...] = _ln_bwd(dln, xhat, r, gv)
        dg_ref[...] += _colsum(dln * xhat)
        db_ref[...] += _colsum(dln)

    vec = bs((1, BW), lambda i: (0, 0))
    row = bs((tm, BW), lambda i: (i, 0))
    return _call(body, "conf_bwd_ln", (t // tm,), [row, bs((None, tm, BW), lambda i: (3, i, 0)), vec, vec],
                 [row, vec, vec], [_sds((t, BW), F32), _sds((1, BW), F32), _sds((1, BW), F32)])(z, dys, ln_g, ln_b)


def conf_bwd_conv(proj, dz, dw, into):
    t = proj.shape[0]
    tm, halo = min(512, t), CONF_HALO
    per = tm // halo
    last = t // halo - 1
    nt = t // tm

    def body(v_ref, gt_ref, vh_ref, gh_ref, dz_ref, dzn_ref, w_ref, _, o_ref, dw_ref):
        i = pl.program_id(0)

        @pl.when(i == 0)
        def _():
            dw_ref[...] = jnp.zeros_like(dw_ref)

        val = v_ref[...]
        sg = _sigmoid(gt_ref[...])
        yh = jnp.where(i > 0, vh_ref[...] * _sigmoid(gh_ref[...]), 0.0)
        yext = jnp.concatenate([yh, val * sg], axis=0)
        dz = dz_ref[...]
        dzn = jnp.where(i < nt - 1, dzn_ref[...], 0.0)
        dy0 = _anti_conv(jnp.concatenate([dz, dzn], axis=0), w_ref, CONF_K, tm)
        o_ref[:, :BW] = (dy0 * sg).astype(BF16)
        o_ref[:, BW:] = (dy0 * val * sg * (1.0 - sg)).astype(BF16)
        _conv_wgrad(dw_ref, dz, yext, CONF_K, halo)

    col = lambda c: bs((tm, BW), lambda i: (i, c))
    hal = lambda c: bs((halo, BW), lambda i: (_prev_blk(i, per), c))
    row = bs((tm, BW), lambda i: (i, 0))
    wsp = bs((CONF_K, BW), lambda i: (0, 0))
    return _call(body, "conf_bwd_conv", (t // tm,),
                 [col(8), col(9), hal(8), hal(9), row, bs((halo, BW), lambda i: (_next_blk(i, per, last), 0)), wsp, ANY],
                 [bs((tm, 2 * BW), lambda i: (i, 4)), wsp], [_sds(into.shape, BF16), _sds((CONF_K, BW), F32)],
                 aliases={7: 0})(proj, proj, proj, proj, dz, dz, dw, into)


def _place():
    return lax.axis_index("x"), lax.axis_index("y"), lax.axis_index("c")


HBM_SPEC = pl.BlockSpec(memory_space=pltpu.HBM)
SEM_SPEC = pl.BlockSpec(memory_space=pltpu.SEMAPHORE)
EFFECT = pltpu.SideEffectType.DATAFLOW_SIDE_EFFECTING


class SplitExchange:
    def __init__(self, name, bufs, plan, n_copies):
        self.name, self.bufs, self.plan, self.n = name, list(bufs), plan, n_copies

    def start(self, after):
        nb, n, plan = len(self.bufs), self.n, self.plan

        def body(*refs):
            send, recv, token = refs[nb + 1], refs[nb + 2], refs[-1]
            for k, (src, dst, _, dev) in enumerate(plan(refs[:nb])):
                pltpu.make_async_remote_copy(src_ref=src, dst_ref=dst, send_sem=send.at[k], recv_sem=recv.at[k],
                                             device_id=dev, device_id_type=MESH).start()
            token[...] = jnp.zeros_like(token)

        outs = pl.pallas_call(
            body, name=self.name + "_start",
            out_shape=(pltpu.SemaphoreType.DMA((n,)), pltpu.SemaphoreType.DMA((n,)),
                       *[pltpu.HBM(b.shape, b.dtype) for b in self.bufs], _sds((8, 128), F32)),
            in_specs=[HBM_SPEC] * nb + [ANY],
            out_specs=(SEM_SPEC, SEM_SPEC, *[HBM_SPEC] * nb, pl.BlockSpec(memory_space=pltpu.VMEM)),
            input_output_aliases={i: 2 + i for i in range(nb)},
            compiler_params=pltpu.CompilerParams(has_side_effects=EFFECT))(
                *[pltpu.with_memory_space_constraint(b, pltpu.HBM) for b in self.bufs], after)
        self.send, self.recv, self.bufs = outs[0], outs[1], list(outs[2:2 + nb])
        return outs[-1]

    def wait(self, after):
        nb, plan = len(self.bufs), self.plan
        after = list(after) if isinstance(after, (list, tuple)) else [after]

        def body(*refs):
            send, recv = refs[nb], refs[nb + 1]
            for k, (src, _, land, dev) in enumerate(plan(refs[:nb])):
                cp = pltpu.make_async_remote_copy(src_ref=src, dst_ref=land, send_sem=send.at[k], recv_sem=recv.at[k],
                                                  device_id=dev, device_id_type=MESH)
                cp.wait_send()
                cp.wait_recv()

        outs = pl.pallas_call(
            body, name=self.name + "_wait", out_shape=tuple(pltpu.HBM(b.shape, b.dtype) for b in self.bufs),
            in_specs=[HBM_SPEC] * nb + [SEM_SPEC, SEM_SPEC] + [ANY] * len(after), out_specs=[HBM_SPEC] * nb,
            input_output_aliases={i: i for i in range(nb)},
            compiler_params=pltpu.CompilerParams(has_side_effects=EFFECT))(*self.bufs, self.send, self.recv, *after)
        return list(outs)


def _chips_of(x, y):
    return [(1 - x, y), (x, 1 - y), (1 - x, 1 - y)]


def allgather_ici_plan(shapes):
    def plan(refs):
        x, y, c = _place()
        out = []
        for a, ref in enumerate(refs):
            hl = shapes[a][1] // 2
            half = pl.ds(c * hl, hl)
            for cx, cy in _chips_of(x, y):
                mine = ref.at[2 * x + y, half]
                out.append((mine, mine, ref.at[2 * cx + cy, half], (cx, cy, c)))
        return out
    return plan


def allgather_d2d_plan(shapes):
    def plan(refs):
        x, y, c = _place()
        out = []
        for a, ref in enumerate(refs):
            hl = shapes[a][1] // 2
            for cx, cy in _chips_of(x, y):
                got = ref.at[2 * cx + cy, pl.ds(c * hl, hl)]
                out.append((got, got, ref.at[2 * cx + cy, pl.ds((1 - c) * hl, hl)], (x, y, 1 - c)))
        return out
    return plan


def gather8(v):
    rows, cols = v.shape

    def body(v_ref, land, send, recv, lsem):
        x, y, c = _place()
        me = 4 * x + 2 * y + c
        mine = pltpu.make_async_copy(v_ref, land.at[me], lsem)
        mine.start()
        sent = []
        for j in range(1, 8):
            fx, fy, fc = (j >> 2) & 1, (j >> 1) & 1, j & 1
            tgt = (1 - x if fx else x, 1 - y if fy else y, 1 - c if fc else c)
            cp = pltpu.make_async_remote_copy(src_ref=v_ref, dst_ref=land.at[me], send_sem=send.at[j - 1],
                                              recv_sem=recv.at[j - 1], device_id=tgt, device_id_type=MESH)
            cp.start()
            sent.append(cp)
        for j in range(1, 8):
            fx, fy, fc = (j >> 2) & 1, (j >> 1) & 1, j & 1
            peer = 4 * (1 - x if fx else x) + 2 * (1 - y if fy else y) + (1 - c if fc else c)
            pltpu.make_async_remote_copy(src_ref=v_ref, dst_ref=land.at[peer], send_sem=send.at[j - 1],
                                         recv_sem=recv.at[j - 1], device_id=(x, y, c), device_id_type=MESH).wait_recv()
        for cp in sent:
            cp.wait_send()
        mine.wait()

    vm = pl.BlockSpec(memory_space=pltpu.VMEM)
    return pl.pallas_call(
        body, name="allgather8", in_specs=[vm], out_specs=vm, out_shape=_sds((8, rows, cols), F32),
        scratch_shapes=[pltpu.SemaphoreType.DMA((7,)), pltpu.SemaphoreType.DMA((7,)), pltpu.SemaphoreType.DMA],
        compiler_params=pltpu.CompilerParams(has_side_effects=True, vmem_limit_bytes=VMEM_LIMIT))(v)


def _row_tile(rows, cols):
    best = 16
    for t in range(16, rows + 1, 16):
        if rows % t == 0 and t * cols * 4 <= 2 * 1024 * 1024:
            best = t
    return best


def _rs_add_sibling(scal, g, ra, hr):
    cols = g.shape[2]
    tr = _row_tile(hr, cols)
    nr = hr // tr

    def body(s_ref, g_ref, r_ref, p32_ref, p16_ref):
        v = g_ref[...] + r_ref[...]
        p16_ref[...] = v.astype(BF16)

        @pl.when(pl.program_id(1) == s_ref[0])
        def _():
            p32_ref[...] = v

    blk = lambda f: bs((None, tr, cols), f)
    own = blk(lambda i, s, sr: (s, i, 0))
    spec = pltpu.PrefetchScalarGridSpec(num_scalar_prefetch=1, grid=(nr, N_SH),
                                        in_specs=[blk(lambda i, s, sr: (s, sr[1] * nr + i, 0)), own],
                                        out_specs=[bs((tr, cols), lambda i, s, sr: (i, 0)), own])
    return pl.pallas_call(body, name="rs_add_sibling", grid_spec=spec,
                          out_shape=[_sds((hr, cols), F32), _sds((N_SH, hr, cols), BF16)],
                          compiler_params=pltpu.CompilerParams(dimension_semantics=("arbitrary",) * 2,
                                                               vmem_limit_bytes=VMEM_LIMIT))(scal, g, ra)


def _rs_add_chips(scal, p32, rb, hr):
    cols = p32.shape[1]
    tr = _row_tile(hr, cols)
    nr = hr // tr

    def body(s_ref, p_ref, r0, r1, r2, o_ref):
        o_ref[...] = ((p_ref[...] + r0[...].astype(F32)) + r1[...].astype(F32)) + r2[...].astype(F32)

    blk = lambda f: bs((None, tr, cols), f)
    spec = pltpu.PrefetchScalarGridSpec(
        num_scalar_prefetch=1, grid=(nr,),
        in_specs=[bs((tr, cols), lambda i, sr: (i, 0))] + [blk(functools.partial(lambda i, sr, j: (j, i, 0), j=j))
                                                            for j in range(3)],
        out_specs=blk(lambda i, sr: (sr[1], i, 0)))
    return pl.pallas_call(body, name="rs_add_chips", grid_spec=spec, out_shape=_sds((2, hr, cols), F32),
                          compiler_params=pltpu.CompilerParams(dimension_semantics=("arbitrary",),
                                                               vmem_limit_bytes=VMEM_LIMIT))(scal, p32, rb, rb, rb)


class SplitReduceScatter:
    def __init__(self, gs):
        x, y, c = _place()
        self.scal = jnp.stack([2 * x + y, c]).astype(jnp.int32)
        self.gs, self.n = list(gs), len(gs)
        self.hrs = [g.shape[1] // 2 for g in gs]

    def swap_start(self, after):
        n, hrs = self.n, self.hrs

        def plan(refs):
            x, y, c = _place()
            return [(refs[a].at[:, pl.ds((1 - c) * hrs[a], hrs[a])], refs[n + a], refs[n + a], (x, y, 1 - c))
                    for a in range(n)]

        lands = [lax.empty((N_SH, hrs[a], g.shape[2]), F32) for a, g in enumerate(self.gs)]
        self.ex = SplitExchange("rs_swap_halves", self.gs + lands, plan, n)
        return self.ex.start(after)

    def swap_wait_send_start(self, after):
        n, hrs = self.n, self.hrs
        bufs = self.ex.wait(after)
        parts = [_rs_add_sibling(self.scal, bufs[a], bufs[n + a], hrs[a]) for a in range(n)]
        self.p32 = [p[0] for p in parts]

        def plan(refs):
            x, y, c = _place()
            return [(refs[a].at[2 * cx + cy], refs[n + a].at[j], refs[n + a].at[j], (cx, cy, c))
                    for a in range(n) for j, (cx, cy) in enumerate(_chips_of(x, y))]

        lands = [lax.empty((3, hrs[a], g.shape[2]), BF16) for a, g in enumerate(self.gs)]
        self.ex = SplitExchange("rs_send_partials", [p[1] for p in parts] + lands, plan, 3 * n)
        return self.ex.start(parts[-1][1])

    def send_wait_share_start(self, after):
        n, hrs = self.n, self.hrs
        bufs = self.ex.wait(after)
        fins = [_rs_add_chips(self.scal, self.p32[a], bufs[n + a], hrs[a]) for a in range(n)]

        def plan(refs):
            x, y, c = _place()
            return [(refs[a].at[c], refs[a].at[c], refs[a].at[1 - c], (x, y, 1 - c)) for a in range(n)]

        self.ex = SplitExchange("rs_share_halves", fins, plan, n)
        return self.ex.start(fins[-1])

    def share_wait(self, after):
        fulls = self.ex.wait(after)
        return [f.reshape(2 * hr, f.shape[2]) for f, hr in zip(fulls, self.hrs)]


def adamw(w, g, m, v):
    shape = w.shape
    cols = shape[-1]
    rows = math.prod(shape[:-1]) if len(shape) > 1 else 1
    tr = 256 if rows % 256 == 0 and rows > 256 else rows
    c1 = 1.0 - ADAM_B1 ** ADAM_STEP
    c2 = 1.0 - ADAM_B2 ** ADAM_STEP

    def body(w_ref, g_ref, m_ref, v_ref, d_ref, nm_ref, nv_ref):
        gv = g_ref[...]
        nm = ADAM_B1 * m_ref[...] + (1.0 - ADAM_B1) * gv
        nv = ADAM_B2 * v_ref[...] + (1.0 - ADAM_B2) * (gv * gv)
        nm_ref[...] = nm
        nv_ref[...] = nv
        d_ref[...] = -ADAM_LR * ((nm / c1) / (jnp.sqrt(nv / c2) + ADAM_EPS) + ADAM_WD * w_ref[...])

    row = bs((tr, cols), lambda i: (i, 0))
    outs = _call(body, "adamw", (rows // tr,), [row] * 4, [row] * 3, [_sds((rows, cols), F32)] * 3)(
        *[a.reshape(rows, cols) for a in (w, g, m, v)])
    return [o.reshape(shape) for o in outs]


def adamw_layers(w, gs, m, v, lo, into=None, after=None):
    shape = w.shape
    cols = shape[-1]
    rl = math.prod(shape[1:-1])
    tr = max(t_ for t_ in range(8, rl + 1, 8) if rl % t_ == 0 and t_ * cols * 4 <= 1024 * 1024)
    nb = rl // tr
    n = len(gs)
    c1 = 1.0 - ADAM_B1 ** ADAM_STEP
    c2 = 1.0 - ADAM_B2 ** ADAM_STEP

    def body(*refs):
        w_ref, m_ref, v_ref = refs[:3]
        g_refs = refs[3:3 + n]
        d_ref, nm_ref, nv_ref, go_ref = refs[-4:]
        layer = pl.program_id(0) // nb
        for k in range(n):
            @pl.when(layer == k)
            def _(k=k):
                gv = g_refs[k][...]
                nm = ADAM_B1 * m_ref[...] + (1.0 - ADAM_B1) * gv
                nv = ADAM_B2 * v_ref[...] + (1.0 - ADAM_B2) * (gv * gv)
                nm_ref[...] = nm
                nv_ref[...] = nv
                go_ref[...] = gv
                d_ref[...] = -ADAM_LR * ((nm / c1) / (jnp.sqrt(nv / c2) + ADAM_EPS) + ADAM_WD * w_ref[...])

    row = bs((tr, cols), lambda b: (lo * nb + b, 0))
    g_specs = [bs((tr, cols), functools.partial(lambda b, k: (jnp.clip(b - k * nb, 0, nb - 1), 0), k=k)) for k in range(n)]
    flat = lambda a: a.reshape(-1, cols)
    in_specs = [row] * 3 + g_specs
    args = [flat(w), flat(m), flat(v)] + [flat(g) for g in gs]
    aliases = None
    if into is not None:
        aliases = {len(in_specs) + k: k for k in range(4)}
        in_specs = in_specs + [ANY] * 4
        args = args + [flat(a) for a in into]
    outs = _call(body, "adamw_layers", (n * nb,), in_specs, [row] * 4, [_sds((shape[0] * rl, cols), F32)] * 4,
                 aliases=aliases, after=after)(*args)
    return [o.reshape(shape) for o in outs]


def allreduce8_split(vec):
    rows, cols = vec.shape

    def plan(refs):
        x, y, c = _place()
        me = 4 * x + 2 * y + c
        out = []
        for j in range(1, 8):
            px, py, pc = (1 - x if j & 4 else x), (1 - y if j & 2 else y), (1 - c if j & 1 else c)
            out.append((refs[0], refs[1].at[me], refs[1].at[4 * px + 2 * py + pc], (px, py, pc)))
        return out

    ex = SplitExchange("allreduce8", [vec, lax.empty((8, rows, cols), F32)], plan, 7)

    def finish(after):
        v, land = ex.wait(after)
        x, y, c = _place()
        me = jnp.reshape(4 * x + 2 * y + c, (1,)).astype(jnp.int32)

        def body(me_ref, v_ref, l_ref, o_ref):
            o_ref[...] = jnp.zeros_like(o_ref)
            for k in range(8):
                @pl.when(me_ref[0] == k)
                def _():
                    o_ref[...] += v_ref[...]

                @pl.when(me_ref[0] != k)
                def _(k=k):
                    o_ref[...] += l_ref[k]

        spec = pltpu.PrefetchScalarGridSpec(
            num_scalar_prefetch=1, grid=(1,),
            in_specs=[bs((rows, cols), lambda i, mr: (0, 0)), bs((8, rows, cols), lambda i, mr: (0, 0, 0))],
            out_specs=bs((rows, cols), lambda i, mr: (0, 0)))
        return pl.pallas_call(body, name="allreduce8_sum", grid_spec=spec, out_shape=_sds((rows, cols), F32),
                              compiler_params=pltpu.CompilerParams(dimension_semantics=("arbitrary",),
                                                                   vmem_limit_bytes=VMEM_LIMIT))(me, v, land)

    return ex, finish


class Hooks:
    def __init__(self):
        self.steps = {}

    def add(self, point, fn):
        self.steps.setdefault(point, []).append(fn)

    def run(self, point, arr, env=None):
        tok = None
        for fn in self.steps.get(point, ()):
            got = fn(arr if tok is None else tok, env)
            tok = tok if got is None else got
        return tok


def layer_fwd(x, p_i, w, hooks):
    h, proj, *qkv = norm_in_proj(x, w["g_mix"], w["win"], after=hooks.run("start", x))
    ya = conva_fwd(proj, w["conv_a"])
    yb, o32, lse = attn_merge([attn_fwd_group(pv, d) for pv, d in zip(qkv, DILATIONS)])
    yc = sgu_fwd(proj, w["sgu_ln_g"], w["sgu_ln_b"], w["sgu_wt"], w["sgu_bf"])
    yd, z = conf_fwd(proj, w["conf_dw"], w["conf_ln_g"], w["conf_ln_b"], after=hooks.run("pre_conf", [ya, yb, yc]))
    ys = (ya, yb, yc, yd)
    tok = hooks.run("pre_merge", yd)
    merged, gates, ybr = merge_fwd(h, ys, w["wg"], w["wbr"], after=tok)
    x1, h2, fgu, act = ffn_in(merged, w["wout"], x, w["g_ffn"], w["wfi"], after=hooks.run("post_merge", merged))
    x2, h3, gate, pp, x3 = ple_fwd(act, w["wfo"], x1, w["g_ple"], w["wpg"], p_i, w["wpp"],
                                   after=hooks.run("post_ffn_in", act))
    saved = dict(x=x, h=h, proj=proj, qkv=qkv, ys=ys, o32=o32, lse=lse, z=z, merged=merged, gates=gates, ybr=ybr, x1=x1,
                 h2=h2, fgu=fgu, act=act, x2=x2, h3=h3, gate=gate, pp=pp)
    return x3, saved


def layer_bwd(dx3, p_i, w, s, hooks):
    t = dx3.shape[0]
    tr = min(1024, t)
    nr = t // tr
    ns_fi = FFN_H // 2
    small = {}

    dpre, dpp, dx2, small["g_ple"] = ple_bwd(dx3, s["gate"], s["pp"], w["wpg"], s["x2"], w["g_ple"],
                                             after=hooks.run("start", dx3))
    ga_shape, gb_shape = _sds((N_SH, 6 * BW, D_MODEL), F32), _sds((N_SH, 5 * BW, BW), F32)
    ga_blk = lambda idx: bs((N_SH, BW, D_MODEL), idx)
    ga = tn_matmul("dw_ple_gate", s["h3"], dpre, (nr,), bs((tr, D_MODEL), lambda r: (r, 0)),
                   bs((tr, D_MODEL), lambda r: (r, 0)), ga_blk(lambda r: (0, 5, 0)), ga_shape, split=N_SH)
    gb = tn_matmul("dw_ple_proj", p_i, dpp, (nr,), bs((tr, BW), lambda r: (r, 0)),
                   bs((tr, D_MODEL), lambda r: (r, 0)), bs((N_SH, BW, BW), lambda r: (0, 4, 0)), gb_shape,
                   split_cols=N_SH)

    df, dx1, small["g_ffn"] = ffn_bwd(dx2, w["wfo"], s["fgu"], w["wfi"], s["x1"], w["g_ffn"],
                                      after=hooks.run("pre_ffn", dx2))
    gfo = tn_matmul("dw_ffn_out", s["act"], dx2, (2, nr), bs((tr, ns_fi), lambda j, r: (r, j)),
                    bs((tr, D_MODEL), lambda j, r: (r, 0)), bs((2, FFN_H // N_SH, D_MODEL), lambda j, r: (j, 0, 0)),
                    _sds((N_SH, FFN_H // N_SH, D_MODEL), F32), split=2)
    gfi = tn_matmul("dw_ffn_in", s["h2"], df, (N_SH, nr), bs((tr, D_MODEL), lambda j, r: (r, 0)),
                    bs((None, tr, ns_fi), lambda j, r: (j // 2, r, j % 2)),
                    bs((None, D_MODEL, ns_fi), lambda j, r: (j, 0, 0)), _sds((N_SH, D_MODEL, ns_fi), F32))

    dpre_m, dys, gb = merge_bwd(dx1, w["wout"], s["gates"], s["ybr"], w["wbr"], s["ys"], gb)
    ga = tn_matmul("dw_out", s["merged"], dx1, (nr,), bs((tr, D_MODEL), lambda r: (r, 0)),
                   bs((tr, D_MODEL), lambda r: (r, 0)), ga_blk(lambda r: (0, 4, 0)), ga_shape, split=N_SH, into=ga,
                   after=hooks.run("pre_dw_out", dys, dict(gfo=gfo, gfi=gfi)))
    ga = tn_matmul("dw_merge_gate", s["h"], dpre_m, (N_BR, nr), bs((tr, D_MODEL), lambda k, r: (r, 0)),
                   bs((None, tr, D_MODEL), lambda k, r: (k, r, 0)), ga_blk(lambda k, r: (0, k, 0)), ga_shape,
                   split=N_SH, into=ga)

    dproj, small["conv_a"] = conva_bwd(s["proj"], dys, w["conv_a"], after=hooks.run("pre_conva", gb))
    lds, dy_views = attn_delta(dys, s["o32"], s["lse"])
    dy_views = [dys[1]] + list(dy_views)
    dproj = attn_bwd_finish([attn_bwd_group(pv, dov, ldv, d)
                             for pv, dov, ldv, d in zip(s["qkv"], dy_views, lds, DILATIONS)], dproj)
    dproj, d_sw, d_sbf, small["sgu_ln_g"], small["sgu_ln_b"] = sgu_bwd(
        s["proj"], dys, w["sgu_ln_g"], w["sgu_ln_b"], w["sgu_wt"], w["sgu_bf"], dproj)
    small["sgu_w"] = jnp.where(jnp.tril(jnp.ones((BLK, BLK), bool))[None], d_sw, 0.0)
    small["sgu_b"] = jnp.sum(d_sbf.reshape(BLK, 4, HEAD_D), axis=-1).T
    dz, small["conf_ln_g"], small["conf_ln_b"] = conf_bwd_ln(s["z"], dys, w["conf_ln_g"], w["conf_ln_b"])
    dproj, small["conf_dw"] = conf_bwd_conv(s["proj"], dz, w["conf_dw"], dproj)

    ns_in = N_IN // N_SH
    gin = tn_matmul("dw_in", s["h"], dproj, (N_SH, nr), bs((tr, D_MODEL), lambda j, r: (r, 0)),
                    bs((tr, ns_in), lambda j, r: (r, j)), bs((None, D_MODEL, ns_in), lambda j, r: (j, 0, 0)),
                    _sds((N_SH, D_MODEL, ns_in), F32), after=hooks.run("pre_dw_in", dproj))
    hooks.run("end", gin)
    big = [ga, gfo, gb, gin, gfi]
    dx, small["g_mix"] = norm_bwd(
        "mix_norm_bwd",
        [(dpre_m, lambda tm: bs((N_BR, tm, D_MODEL), lambda i: (0, i, 0)), w["wg"],
          lambda a, wr: [(a[k], wr[k]) for k in range(N_BR)]),
         (dproj, lambda tm: bs((tm, N_IN), lambda i: (i, 0)), w["win"],
          lambda a, wr: [(a[:, k * ns_in:(k + 1) * ns_in], wr[k]) for k in range(N_SH)])],
        dx1, s["x"], w["g_mix"])
    return dx, big, small


BIG_NAMES = ("w_in", "w_branch", "w_merge_gate", "w_out", "w_ffn_in", "w_ffn_out", "w_ple_gate", "w_ple_proj")


def unpack_big_grads(ga, gfo, gb, gin, gfi):
    return dict(w_in=gin, w_ffn_in=gfi, w_ffn_out=gfo,
                w_merge_gate=ga[:N_BR * BW].reshape(N_BR, BW, D_MODEL), w_out=ga[N_BR * BW:5 * BW], w_ple_gate=ga[5 * BW:],
                w_branch=gb[:N_BR * BW].reshape(N_BR, BW, BW), w_ple_proj=gb[N_BR * BW:])


SMALL_NAMES = ("g_mix", "conv_a", "sgu_ln_g", "sgu_ln_b", "sgu_w", "sgu_b", "conf_dw", "conf_ln_g", "conf_ln_b",
               "g_ffn", "g_ple")


def _pack_rows(arrays, rows):
    flat = jnp.concatenate([a.reshape(-1) for a in arrays])
    return jnp.pad(flat, (0, rows * D_MODEL - flat.shape[0])).reshape(rows, D_MODEL)


def _unpack_rows(packed, shapes):
    flat, out, pos = packed.reshape(-1), [], 0
    for shape in shapes:
        n = math.prod(shape)
        out.append(flat[pos:pos + n].reshape(shape))
        pos += n
    return out


def kernel(x, p, g_mix, w_in, conv_a, sgu_ln_g, sgu_ln_b, sgu_w, sgu_b, conf_dw, conf_ln_g, conf_ln_b, w_branch, w_merge_gate, w_out, g_ffn, w_ffn_in, w_ffn_out, g_ple, w_ple_gate, w_ple_proj, g_final, loss_target, m_g_mix, m_w_in, m_conv_a, m_sgu_ln_g, m_sgu_ln_b, m_sgu_w, m_sgu_b, m_conf_dw, m_conf_ln_g, m_conf_ln_b, m_w_branch, m_w_merge_gate, m_w_out, m_g_ffn, m_w_ffn_in, m_w_ffn_out, m_g_ple, m_w_ple_gate, m_w_ple_proj, m_g_final, v_g_mix, v_w_in, v_conv_a, v_sgu_ln_g, v_sgu_ln_b, v_sgu_w, v_sgu_b, v_conf_dw, v_conf_ln_g, v_conf_ln_b, v_w_branch, v_w_merge_gate, v_w_out, v_g_ffn, v_w_ffn_in, v_w_ffn_out, v_g_ple, v_w_ple_gate, v_w_ple_proj, v_g_final):
    weights = dict(g_mix=g_mix, w_in=w_in, conv_a=conv_a, sgu_ln_g=sgu_ln_g, sgu_ln_b=sgu_ln_b, sgu_w=sgu_w, sgu_b=sgu_b,
                   conf_dw=conf_dw, conf_ln_g=conf_ln_g, conf_ln_b=conf_ln_b, w_branch=w_branch, w_merge_gate=w_merge_gate,
                   w_out=w_out, g_ffn=g_ffn, w_ffn_in=w_ffn_in, w_ffn_out=w_ffn_out, g_ple=g_ple, w_ple_gate=w_ple_gate,
                   w_ple_proj=w_ple_proj, g_final=g_final)
    m_in = dict(g_mix=m_g_mix, w_in=m_w_in, conv_a=m_conv_a, sgu_ln_g=m_sgu_ln_g, sgu_ln_b=m_sgu_ln_b, sgu_w=m_sgu_w,
                sgu_b=m_sgu_b, conf_dw=m_conf_dw, conf_ln_g=m_conf_ln_g, conf_ln_b=m_conf_ln_b, w_branch=m_w_branch,
                w_merge_gate=m_w_merge_gate, w_out=m_w_out, g_ffn=m_g_ffn, w_ffn_in=m_w_ffn_in, w_ffn_out=m_w_ffn_out,
                g_ple=m_g_ple, w_ple_gate=m_w_ple_gate, w_ple_proj=m_w_ple_proj, g_final=m_g_final)
    v_in = dict(g_mix=v_g_mix, w_in=v_w_in, conv_a=v_conv_a, sgu_ln_g=v_sgu_ln_g, sgu_ln_b=v_sgu_ln_b, sgu_w=v_sgu_w,
                sgu_b=v_sgu_b, conf_dw=v_conf_dw, conf_ln_g=v_conf_ln_g, conf_ln_b=v_conf_ln_b, w_branch=v_w_branch,
                w_merge_gate=v_w_merge_gate, w_out=v_w_out, g_ffn=v_g_ffn, w_ffn_in=v_w_ffn_in, w_ffn_out=v_w_ffn_out,
                g_ple=v_g_ple, w_ple_gate=v_w_ple_gate, w_ple_proj=v_w_ple_proj, g_final=v_g_final)
    order = ("g_mix", "w_in", "conv_a", "sgu_ln_g", "sgu_ln_b", "sgu_w", "sgu_b", "conf_dw", "conf_ln_g", "conf_ln_b",
             "w_branch", "w_merge_gate", "w_out", "g_ffn", "w_ffn_in", "w_ffn_out", "g_ple", "w_ple_gate", "w_ple_proj",
             "g_final")
    depth = g_mix.shape[0]
    xs, tgt = x[0], loss_target[0]
    cw = BW // N_SH
    my_shard = 2 * lax.axis_index("x") + lax.axis_index("y")

    conv_rows = 16
    allc = gather8(_pack_rows([conv_a, conf_dw], conv_rows))
    shards = [_unpack_rows(allc[2 * s], [conv_a.shape, conf_dw.shape]) for s in range(N_SH)]
    conv_a_full = jnp.concatenate([sh[0] for sh in shards], axis=-1)
    conf_dw_full = jnp.concatenate([sh[1] for sh in shards], axis=-1)

    tril = jnp.tril(jnp.ones((BLK, BLK), bool))
    def placed_shards(i):
        shards = ([w_in[i], w_branch[i]] + [w_merge_gate[i, k] for k in range(N_BR)]
                  + [w_out[i], w_ffn_in[i], w_ffn_out[i], w_ple_gate[i], w_ple_proj[i]])
        return [lax.dynamic_update_slice(lax.empty((N_SH,) + sh.shape, BF16), sh.astype(BF16)[None],
                                         (my_shard,) + (0,) * sh.ndim) for sh in shards]

    def small_weights(i, win):
        vec = lambda a: a[i].reshape(1, -1)
        return dict(
            win=win, g_mix=vec(g_mix), g_ffn=vec(g_ffn), g_ple=vec(g_ple), conv_a=conv_a_full[i], conf_dw=conf_dw_full[i],
            sgu_ln_g=vec(sgu_ln_g), sgu_ln_b=vec(sgu_ln_b), conf_ln_g=vec(conf_ln_g), conf_ln_b=vec(conf_ln_b),
            sgu_wt=jnp.where(tril[None], sgu_w[i], 0.0).astype(BF16),
            sgu_bf=jnp.repeat(sgu_b[i].T, HEAD_D, axis=1))

    def late_weights(got):
        return dict(wbr=got[0], wg=jnp.stack([g.reshape(D_MODEL, D_MODEL) for g in got[1:5]]),
                    wout=got[5].reshape(D_MODEL, D_MODEL), wfi=got[6], wfo=got[7].reshape(FFN_H, D_MODEL),
                    wpg=got[8].reshape(D_MODEL, D_MODEL), wpp=got[9])

    class SplitAllGather:
        def __init__(self, bufs):
            self.shapes = [b.shape for b in bufs]
            self.ici = SplitExchange("allgather_ici", bufs, allgather_ici_plan(self.shapes), 3 * len(bufs))

        def ici_start(self, after, env=None):
            return self.ici.start(after)

        def ici_wait_d2d_start(self, after, env=None):
            landed = self.ici.wait(after)
            self.d2d = SplitExchange("allgather_d2d", landed, allgather_d2d_plan(self.shapes), 3 * len(landed))
            return self.d2d.start(landed[-1])

        def d2d_wait(self, after, env=None):
            self.got = self.d2d.wait(after)
            return None

    bufs0 = placed_shards(0)
    first = SplitAllGather(bufs0[:1])
    first.d2d_wait(first.ici_wait_d2d_start(first.ici_start(xs)))
    rest = SplitAllGather(bufs0[1:])
    layers = [small_weights(0, first.got[0])]
    act, saved = xs, []
    nxt_done = None
    for i in range(depth):
        hooks = Hooks()
        if i == 0:
            hooks.add("start", rest.ici_start)
            hooks.add("pre_conf", rest.ici_wait_d2d_start)
            hooks.add("pre_merge", rest.d2d_wait)
            hooks.add("pre_merge", lambda after, env: layers[0].update(late_weights(rest.got)))
        if i + 1 < depth:
            nxt = SplitAllGather(placed_shards(i + 1))
            points = ("pre_merge", "post_ffn_in", None) if i == 0 else ("start", "post_merge", "post_ffn_in")
            hooks.add(points[0], nxt.ici_start)
            hooks.add(points[1], nxt.ici_wait_d2d_start)
            if points[2]:
                hooks.add(points[2], nxt.d2d_wait)
        act, sv = layer_fwd(act, p[i, 0], layers[i], hooks)
        saved.append(sv)
        if i + 1 < depth:
            if i == 0:
                nxt.d2d_wait(act)
            layers.append({**small_weights(i + 1, nxt.got[0]), **late_weights(nxt.got[1:])})
    loss_part, dx, dg_final = loss_head(act, g_final.reshape(1, -1), tgt)

    big_red = [None] * depth
    small_red = [None] * depth
    small_rows = 80
    pending = None

    def small_vector(i, small):
        parts = [small[n] for n in SMALL_NAMES]
        return _pack_rows(parts + ([dg_final, loss_part[0, :1]] if i == 0 else []), small_rows)

    for i in reversed(range(depth)):
        hooks = Hooks()
        result = {}
        if pending is not None:
            rs, j, (small_ex, small_finish) = pending
            hooks.add("start", lambda after, env, ex=small_ex: ex.start(after))
            hooks.add("start", lambda after, env, rs=rs: rs.swap_start(after))
            hooks.add("pre_ffn", lambda after, env, rs=rs: rs.swap_wait_send_start(after))
            hooks.add("pre_dw_out", lambda after, env, rs=rs: rs.send_wait_share_start(after))
            hooks.add("pre_conva", lambda after, env, rs=rs, result=result: result.update(prev=rs.share_wait(after)))
            hooks.add("pre_conva", lambda after, env, fin=small_finish, result=result: result.update(small=fin(after)))
        if i == 0:
            def early_start(after, env, result=result):
                result["rs"] = SplitReduceScatter([env["gfo"], env["gfi"]])
                return result["rs"].swap_start(after)

            hooks.add("pre_dw_out", early_start)
            hooks.add("pre_conva", lambda after, env, result=result: result["rs"].swap_wait_send_start(after))
            hooks.add("pre_dw_in", lambda after, env, result=result: result["rs"].send_wait_share_start(after))
            hooks.add("end", lambda after, env, result=result: result.update(early=result["rs"].share_wait(after)))
        dx, big, small = layer_bwd(dx, p[i, 0], layers[i], saved[i], hooks)
        if pending is not None:
            big_red[pending[1]] = unpack_big_grads(*result["prev"])
            small_red[pending[1]] = result["small"]
        if i > 0:
            pending = (SplitReduceScatter(big), i, allreduce8_split(small_vector(i, small)))

    ga, _, gb, gin, _ = big
    late = SplitReduceScatter([ga, gb, gin])
    small_ex, small_finish = allreduce8_split(small_vector(0, small))
    upd = {}

    def update_upper(names, after):
        for name in names:
            upd[name] = adamw_layers(weights[name], [big_red[i][name] for i in range(1, depth)], m_in[name], v_in[name],
                                     1, after=after)
        return [upd[name][0] for name in names]

    done = update_upper(("w_in",), late.swap_start(small_ex.start(dx)))
    done = update_upper(("w_ffn_in", "w_merge_gate", "w_ffn_out"), late.swap_wait_send_start(done))
    small_red[0] = small_finish(done)
    done = update_upper(("w_branch", "w_out", "w_ple_gate", "w_ple_proj"), late.send_wait_share_start(done))
    ga, gb, gin = late.share_wait(done)
    gfo, gfi = result["early"]
    big_red[0] = unpack_big_grads(ga, gfo, gb, gin, gfi)

    layer_shapes = [small[n].shape for n in SMALL_NAMES]
    per_layer = [_unpack_rows(small_red[i], layer_shapes + ([dg_final.shape, (1,)] if i == 0 else []))
                 for i in range(depth)]
    grads = {n: jnp.stack([per_layer[i][k].reshape(weights[n].shape[1:] if n not in ("conv_a", "conf_dw")
                                                   else per_layer[i][k].shape) for i in range(depth)])
             for k, n in enumerate(SMALL_NAMES)}
    grads["g_final"] = per_layer[0][-2].reshape(-1)
    loss = per_layer[0][-1].reshape(())
    for n in ("conv_a", "conf_dw"):
        grads[n] = lax.dynamic_slice_in_dim(grads[n], my_shard * cw, cw, axis=2)

    small_all = [n for n in order if n not in BIG_NAMES]
    sm_shapes = [weights[n].shape for n in small_all]
    n_sm = sum(math.prod(sh) for sh in sm_shapes)
    sm_rows = -(-n_sm // (8 * D_MODEL)) * 8
    packed = [_pack_rows([src[n] for n in small_all], sm_rows) for src in (weights, grads, m_in, v_in)]
    sm_out = [_unpack_rows(o, sm_shapes) for o in adamw(*packed)]
    delta, new_m, new_v = ({n: o[k] for k, n in enumerate(small_all)} for o in sm_out)
    for name in BIG_NAMES:
        delta[name], new_m[name], new_v[name], grads[name] = adamw_layers(
            weights[name], [big_red[0][name]], m_in[name], v_in[name], 0, into=upd[name])

    return (loss, dx[None], *[grads[n] for n in order], *[delta[n] for n in order], *[new_m[n] for n in order],
            *[new_v[n] for n in order])
```

```python
import functools
import math

import jax
import jax.numpy as jnp
from jax import lax
from jax.experimental import pallas as pl
from jax.experimental.pallas import tpu as pltpu

F32 = jnp.float32
BF16 = jnp.bfloat16
EPS = 1e-6
D_MODEL = 1024
BW = 256
N_BR = 4
N_IN = 10 * BW
FFN_H = 2816
N_SH = 4
HEADS = 4
HEAD_D = 64
BLK = 128
DILATIONS = (1, 4, 16)
CONF_K = 31
CONVA_K = 3
NEG = -1e30
VMEM_LIMIT = 56 * 1024 * 1024
MESH = pl.DeviceIdType.MESH

ADAM_LR, ADAM_B1, ADAM_B2, ADAM_EPS, ADAM_WD, ADAM_STEP = 0.001, 0.9, 0.999, 1e-08, 0.01, 10

bs = pl.BlockSpec
ANY = pl.BlockSpec(memory_space=pl.ANY)


def _call(body, name, grid, in_specs, out_specs, out_shape, scratch=(), aliases=None, after=None):
    n_in = len(in_specs)
    kernel_body = body
    if after is not None:
        in_specs = list(in_specs) + [ANY]

        def kernel_body(*refs):
            return body(*refs[:n_in], *refs[n_in + 1:])

    call = pl.pallas_call(
        kernel_body, name=name, grid=grid, in_specs=in_specs, out_specs=out_specs, out_shape=out_shape,
        scratch_shapes=list(scratch), input_output_aliases=aliases or {},
        compiler_params=pltpu.CompilerParams(dimension_semantics=("arbitrary",) * len(grid),
                                             vmem_limit_bytes=VMEM_LIMIT))
    return call if after is None else (lambda *args: call(*args, after))


def _sds(shape, dtype):
    return jax.ShapeDtypeStruct(shape, dtype)


def _nn(a, b):
    return jnp.dot(a, b, preferred_element_type=F32)


def _nt(a, b):
    return lax.dot_general(a, b, (((1,), (1,)), ((), ())), preferred_element_type=F32)


def _tn(a, b):
    return lax.dot_general(a, b, (((0,), (0,)), ((), ())), preferred_element_type=F32)


def _sigmoid(x):
    return 1.0 / (1.0 + jnp.exp(-x))


def _rms_fwd(x, g):
    r = lax.rsqrt(jnp.mean(x * x, axis=-1, keepdims=True) + EPS)
    return x * r * g


def _rms_bwd(dh, x, g):
    r = lax.rsqrt(jnp.mean(x * x, axis=-1, keepdims=True) + EPS)
    xr = x * r
    dxr = dh * g
    dx = r * (dxr - xr * jnp.mean(dxr * xr, axis=-1, keepdims=True))
    return dx, dh * xr


def _ln_hat(x):
    mu = jnp.mean(x, axis=-1, keepdims=True)
    xc = x - mu
    r = lax.rsqrt(jnp.mean(xc * xc, axis=-1, keepdims=True) + EPS)
    return xc * r, r


def _ln_bwd(dy, xhat, r, g):
    dxh = dy * g
    return r * (dxh - jnp.mean(dxh, axis=-1, keepdims=True) - xhat * jnp.mean(dxh * xhat, axis=-1, keepdims=True))


def _colsum(v):
    return jnp.sum(v, axis=0, keepdims=True)


def _causal_conv(zext, w_ref, k_taps, halo):
    acc = zext[halo:] * w_ref[k_taps - 1:k_taps, :]
    for k in range(k_taps - 1):
        acc = acc + pltpu.roll(zext, k_taps - 1 - k, 0)[halo:] * w_ref[k:k + 1, :]
    return acc


def _anti_conv(dext, w_ref, k_taps, tm):
    n = dext.shape[0]
    acc = dext[:tm] * w_ref[k_taps - 1:k_taps, :]
    for s in range(1, k_taps):
        acc = acc + pltpu.roll(dext, n - s, 0)[:tm] * w_ref[k_taps - 1 - s:k_taps - s, :]
    return acc


def _conv_wgrad(dw_ref, dc, zext, k_taps, halo):
    dw_ref[k_taps - 1:k_taps, :] += _colsum(dc * zext[halo:])
    for k in range(k_taps - 1):
        dw_ref[k:k + 1, :] += _colsum(dc * pltpu.roll(zext, k_taps - 1 - k, 0)[halo:])


LANES = 128


def _to_strided_view(dst_ref, chunk, scr, d, width):
    n = scr.shape[0] // d
    for c in range(width // LANES):
        scr[...] = chunk(c)
        for r in range(d):
            dst_ref[:, r * width + c * LANES:r * width + (c + 1) * LANES] = scr[pl.ds(r, n, stride=d), :].astype(dst_ref.dtype)


def _from_strided_view(src_ref, scr, d, width, c):
    n = scr.shape[0] // d
    for r in range(d):
        scr[pl.ds(r, n, stride=d), :] = src_ref[:, r * width + c * LANES:r * width + (c + 1) * LANES].astype(F32)
    return scr[...]


def _view_spec(tm, d, width):
    return bs((tm // d, d * width), lambda i: (i, 0))


def _prev_blk(i, per):
    return jnp.maximum(i * per - 1, 0)


def _next_blk(i, per, last):
    return jnp.minimum((i + 1) * per, last)


def norm_in_proj(x, g, win, after=None):
    t = x.shape[0]
    tm = min(512, t)
    ns = win.shape[2]

    def body(x_ref, g_ref, w_ref, h_ref, o_ref, q_ref, q4_ref, q16_ref, scr):
        h = _rms_fwd(x_ref[...], g_ref[...]).astype(BF16)
        h_ref[...] = h
        parts = []
        for s in range(N_SH):
            r = _nn(h, w_ref[s])
            o_ref[:, s * ns:(s + 1) * ns] = r
            if s == 1:
                parts.append(r[:, 3 * BW - ns:])
            if s == 2:
                parts.append(r[:, :6 * BW - 2 * ns])
        qf = jnp.concatenate(parts, axis=1)
        q_ref[...] = qf.astype(BF16)
        chunk = lambda c: qf[:, c * LANES:(c + 1) * LANES]
        _to_strided_view(q4_ref, chunk, scr, 4, 3 * BW)
        _to_strided_view(q16_ref, chunk, scr, 16, 3 * BW)

    row = lambda c: bs((tm, c), lambda i: (i, 0))
    return _call(
        body, "norm_in_proj", (t // tm,), [row(D_MODEL), bs((1, D_MODEL), lambda i: (0, 0)), _resident(win)],
        [row(D_MODEL), row(N_IN), row(3 * BW), _view_spec(tm, 4, 3 * BW), _view_spec(tm, 16, 3 * BW)],
        [_sds((t, D_MODEL), BF16), _sds((t, N_IN), F32), _sds((t, 3 * BW), BF16),
         _sds((t // 4, 4 * 3 * BW), BF16), _sds((t // 16, 16 * 3 * BW), BF16)],
        scratch=[pltpu.VMEM((tm, LANES), F32)], after=after)(x, g, win)


def merge_fwd(h, ys, wg, wbr, after=None):
    t = h.shape[0]
    tm = min(512, t)

    def body(h_ref, ya, yb, yc, yd, wg_ref, wb_ref, m_ref, g_ref, b_ref):
        hh = h_ref[...]
        for j in range(N_SH):
            cs = slice(j * BW, (j + 1) * BW)
            acc = None
            for k, y_ref in enumerate((ya, yb, yc, yd)):
                g = _sigmoid(_nn(hh, wg_ref[k, :, cs]))
                b = _nn(y_ref[...], wb_ref[j, k])
                g_ref[k, :, cs] = g.astype(BF16)
                b_ref[k, :, cs] = b.astype(BF16)
                acc = g * b if acc is None else acc + g * b
            m_ref[:, cs] = acc.astype(BF16)

    ysp = bs((tm, BW), lambda i: (i, 0))
    big = bs((N_BR, tm, D_MODEL), lambda i: (0, i, 0))
    return _call(
        body, "merge_fwd", (t // tm,),
        [bs((tm, D_MODEL), lambda i: (i, 0)), ysp, ysp, ysp, ysp, _resident(wg), _resident(wbr)],
        [bs((tm, D_MODEL), lambda i: (i, 0)), big, big],
        [_sds((t, D_MODEL), BF16), _sds((N_BR, t, D_MODEL), BF16), _sds((N_BR, t, D_MODEL), BF16)], after=after)(
            h, *ys, wg, wbr)


def ffn_in(a, wout, res, g, wfi, after=None):
    t = res.shape[0]
    tm = min(256, t)
    ns = wfi.shape[2]

    def body(a_ref, wo_ref, r_ref, g_ref, w_ref, x_ref, h_ref, f_ref, act_ref):
        xv = r_ref[...] + _nn(a_ref[...], wo_ref[...])
        x_ref[...] = xv
        h = _rms_fwd(xv, g_ref[...]).astype(BF16)
        h_ref[...] = h
        for j in range(2):
            cs = slice(j * ns, (j + 1) * ns)
            fg = _nn(h, w_ref[j])
            fu = _nn(h, w_ref[j + 2])
            f_ref[0, :, cs] = fg.astype(BF16)
            f_ref[1, :, cs] = fu.astype(BF16)
            act_ref[:, cs] = (fg * _sigmoid(fg) * fu).astype(BF16)

    row = lambda c: bs((tm, c), lambda i: (i, 0))
    return _call(
        body, "ffn_in", (t // tm,),
        [row(D_MODEL), _resident(wout), row(D_MODEL), bs((1, D_MODEL), lambda i: (0, 0)), _resident(wfi)],
        [row(D_MODEL), row(D_MODEL), bs((2, tm, FFN_H), lambda i: (0, i, 0)), row(FFN_H)],
        [_sds((t, D_MODEL), F32), _sds((t, D_MODEL), BF16), _sds((2, t, FFN_H), BF16), _sds((t, FFN_H), BF16)],
        after=after)(a, wout, res, g, wfi)


def ple_fwd(a, wfo, res, g, wpg, p_i, wpp, after=None):
    t = res.shape[0]
    tm = min(512, t)

    def body(a_ref, wo_ref, r_ref, g_ref, wg_ref, p_ref, wp_ref, x_ref, h_ref, gt_ref, pp_ref, o_ref):
        xv = r_ref[...] + _nn(a_ref[...], wo_ref[...])
        x_ref[...] = xv
        h = _rms_fwd(xv, g_ref[...]).astype(BF16)
        h_ref[...] = h
        gate = _sigmoid(_nn(h, wg_ref[...]))
        pb = p_ref[...].astype(BF16)
        pp = jnp.concatenate([_nn(pb, wp_ref[j]) for j in range(N_SH)], axis=1)
        gt_ref[...] = gate.astype(BF16)
        pp_ref[...] = pp.astype(BF16)
        o_ref[...] = xv + gate * pp

    row = bs((tm, D_MODEL), lambda i: (i, 0))
    return _call(
        body, "ple_fwd", (t // tm,),
        [bs((tm, FFN_H), lambda i: (i, 0)), _resident(wfo), row, bs((1, D_MODEL), lambda i: (0, 0)), _resident(wpg),
         bs((tm, BW), lambda i: (i, 0)), _resident(wpp)],
        [row, row, row, row, row],
        [_sds((t, D_MODEL), F32), _sds((t, D_MODEL), BF16), _sds((t, D_MODEL), BF16), _sds((t, D_MODEL), BF16),
         _sds((t, D_MODEL), F32)], after=after)(a, wfo, res, g, wpg, p_i, wpp)


def loss_head(x, g, tgt):
    t = x.shape[0]
    tm = min(512, t)

    def body(x_ref, g_ref, t_ref, l_ref, dx_ref, dg_ref):
        @pl.when(pl.program_id(0) == 0)
        def _():
            l_ref[...] = jnp.zeros_like(l_ref)
            dg_ref[...] = jnp.zeros_like(dg_ref)

        xv, gv = x_ref[...], g_ref[...]
        err = _rms_fwd(xv, gv) - t_ref[...]
        part = 0.5 * jnp.sum(jnp.mean(err * err, axis=-1, keepdims=True), axis=0, keepdims=True)
        l_ref[...] += jnp.broadcast_to(part, l_ref.shape)
        dx, dgr = _rms_bwd(err * (1.0 / D_MODEL), xv, gv)
        dx_ref[...] = dx
        dg_ref[...] += _colsum(dgr)

    row = bs((tm, D_MODEL), lambda i: (i, 0))
    vec = bs((1, D_MODEL), lambda i: (0, 0))
    return _call(body, "loss_head", (t // tm,), [row, vec, row],
                 [bs((1, 128), lambda i: (0, 0)), row, vec],
                 [_sds((1, 128), F32), _sds((t, D_MODEL), F32), _sds((1, D_MODEL), F32)])(x, g, tgt)


def tn_matmul(name, a, b, grid, a_spec, b_spec, out_spec, out_shape, split=0, split_cols=0, into=None, after=None):
    last = len(grid) - 1

    def body(a_ref, b_ref, *rest):
        o_ref = rest[-1]

        @pl.when(pl.program_id(last) == 0)
        def _():
            o_ref[...] = jnp.zeros_like(o_ref)

        res = _tn(a_ref[...].astype(BF16), b_ref[...].astype(BF16))
        if split_cols:
            cols = res.shape[1] // split_cols
            for s in range(split_cols):
                o_ref[s] += res[:, s * cols:(s + 1) * cols]
        elif split:
            rows = res.shape[0] // split
            for s in range(split):
                o_ref[s] += res[s * rows:(s + 1) * rows]
        else:
            o_ref[...] += res

    if into is None:
        return _call(body, name, grid, [a_spec, b_spec], out_spec, out_shape, after=after)(a, b)
    return _call(body, name, grid, [a_spec, b_spec, ANY], out_spec, out_shape, aliases={2: 0}, after=after)(a, b, into)


def _resident(w):
    zeros = (0,) * w.ndim
    return bs(w.shape, lambda i: zeros, pipeline_mode=pl.Buffered(1))


def norm_bwd(name, sources, dx_in, x, g):
    t = x.shape[0]
    tm = min(512, t)
    n_src = len(sources)

    def body(*refs):
        dxi_ref, x_ref, g_ref, dx_ref, dg_ref = refs[2 * n_src:]

        @pl.when(pl.program_id(0) == 0)
        def _():
            dg_ref[...] = jnp.zeros_like(dg_ref)

        dh = None
        for si in range(n_src):
            for av, wv in sources[si][3](refs[2 * si], refs[2 * si + 1]):
                part = _nt(av, wv)
                dh = part if dh is None else dh + part
        dx, dgr = _rms_bwd(dh, x_ref[...], g_ref[...])
        dx_ref[...] = dxi_ref[...] + dx
        dg_ref[...] += _colsum(dgr)

    in_specs, args = [], []
    for a, a_spec, w, _ in sources:
        in_specs += [a_spec(tm), _resident(w)]
        args += [a, w]
    row = bs((tm, D_MODEL), lambda i: (i, 0))
    vec = bs((1, D_MODEL), lambda i: (0, 0))
    return _call(body, name, (t // tm,), in_specs + [row, row, vec], [row, vec],
                 [_sds((t, D_MODEL), F32), _sds((1, D_MODEL), F32)])(*args, dx_in, x, g)


def ple_bwd(dx, gate, pp, wpg, x, g, h3, p_i, ga_shape, gb_shape, after=None):
    t = dx.shape[0]
    tm = min(512, t)

    def body(dx_ref, gt_ref, p_ref, w_ref, x_ref, g_ref, h_ref, pi_ref, o_ref, dg_ref, ga_ref, gb_ref):
        @pl.when(pl.program_id(0) == 0)
        def _():
            dg_ref[...] = jnp.zeros_like(dg_ref)
            ga_ref[...] = jnp.zeros_like(ga_ref)
            gb_ref[...] = jnp.zeros_like(gb_ref)

        d = dx_ref[...]
        gt = gt_ref[...].astype(F32)
        dpre = (d * p_ref[...].astype(F32) * gt * (1.0 - gt)).astype(BF16)
        dpp = (d * gt).astype(BF16)
        dxn, dgr = _rms_bwd(_nt(dpre, w_ref[...]), x_ref[...], g_ref[...])
        o_ref[...] = d + dxn
        dg_ref[...] += _colsum(dgr)
        dwg = _tn(h_ref[...], dpre)
        dwp = _tn(pi_ref[...].astype(BF16), dpp)
        for s in range(N_SH):
            ga_ref[s] += dwg[s * BW:(s + 1) * BW]
            gb_ref[s] += dwp[:, s * BW:(s + 1) * BW]

    row = bs((tm, D_MODEL), lambda i: (i, 0))
    vec = bs((1, D_MODEL), lambda i: (0, 0))
    return _call(body, "ple_bwd", (t // tm,),
                 [row, row, row, _resident(wpg), row, vec, row, bs((tm, BW), lambda i: (i, 0))],
                 [row, vec, bs((N_SH, BW, D_MODEL), lambda i: (0, 5, 0)), bs((N_SH, BW, BW), lambda i: (0, 4, 0))],
                 [_sds((t, D_MODEL), F32), _sds((1, D_MODEL), F32), ga_shape, gb_shape],
                 after=after)(dx, gate, pp, wpg, x, g, h3, p_i)


def ffn_bwd(dx, wfo, fgu, wfi, x, g, after=None):
    t = dx.shape[0]
    tm = min(256, t)
    ns = FFN_H // 2

    def body(dx_ref, wo_ref, f_ref, wi_ref, x_ref, g_ref, df_ref, o_ref, dg_ref):
        @pl.when(pl.program_id(0) == 0)
        def _():
            dg_ref[...] = jnp.zeros_like(dg_ref)

        d = dx_ref[...]
        dxb = d.astype(BF16)
        dh = None
        for j in range(2):
            cs = slice(j * ns, (j + 1) * ns)
            dact = _nt(dxb, wo_ref[cs, :])
            fg = f_ref[0, :, cs].astype(F32)
            fu = f_ref[1, :, cs].astype(F32)
            s = _sigmoid(fg)
            dfg = (dact * fu * (s * (1.0 + fg * (1.0 - s)))).astype(BF16)
            dfu = (dact * fg * s).astype(BF16)
            df_ref[0, :, cs] = dfg
            df_ref[1, :, cs] = dfu
            part = _nt(dfg, wi_ref[j]) + _nt(dfu, wi_ref[j + 2])
            dh = part if dh is None else dh + part
        dxn, dgr = _rms_bwd(dh, x_ref[...], g_ref[...])
        o_ref[...] = d + dxn
        dg_ref[...] += _colsum(dgr)

    blk = bs((2, tm, FFN_H), lambda i: (0, i, 0))
    row = bs((tm, D_MODEL), lambda i: (i, 0))
    vec = bs((1, D_MODEL), lambda i: (0, 0))
    return _call(body, "ffn_bwd", (t // tm,), [row, _resident(wfo), blk, _resident(wfi), row, vec], [blk, row, vec],
                 [_sds((2, t, FFN_H), BF16), _sds((t, D_MODEL), F32), _sds((1, D_MODEL), F32)],
                 after=after)(dx, wfo, fgu, wfi, x, g)


def merge_bwd(dx, wout, gates, ybr, wbr, ys, gb):
    t = dx.shape[0]
    tm = min(256, t)

    def body(dx_ref, w_ref, g_ref, b_ref, wb_ref, ya, yb, yc, yd, _, dpre_ref, dy_ref, gb_ref):
        @pl.when(pl.program_id(0) == 0)
        def _():
            gb_ref[...] = jnp.zeros_like(gb_ref)

        dm = _nt(dx_ref[...].astype(BF16), w_ref[...])
        for k, y_ref in enumerate((ya, yb, yc, yd)):
            g = g_ref[k].astype(F32)
            dpre_ref[k] = (dm * b_ref[k].astype(F32) * g * (1.0 - g)).astype(BF16)
            dyb = (dm * g).astype(BF16)
            acc = None
            for s in range(N_SH):
                part = _nt(dyb[:, s * BW:(s + 1) * BW], wb_ref[s, k])
                acc = part if acc is None else acc + part
            dy_ref[k] = acc
            dwb = _tn(y_ref[...], dyb)
            for s in range(N_SH):
                gb_ref[s, k * BW:(k + 1) * BW, :] += dwb[:, s * BW:(s + 1) * BW]

    blk = bs((N_BR, tm, D_MODEL), lambda i: (0, i, 0))
    ysp = bs((tm, BW), lambda i: (i, 0))
    return _call(body, "merge_bwd", (t // tm,),
                 [bs((tm, D_MODEL), lambda i: (i, 0)), _resident(wout), blk, blk, _resident(wbr), ysp, ysp, ysp, ysp, ANY],
                 [blk, bs((N_BR, tm, BW), lambda i: (0, i, 0)), bs((N_SH, N_BR * BW, BW), lambda i: (0, 0, 0))],
                 [_sds((N_BR, t, D_MODEL), BF16), _sds((N_BR, t, BW), F32), _sds(gb.shape, F32)],
                 aliases={9: 2})(dx, wout, gates, ybr, wbr, *ys, gb)


def conva_fwd(proj, wa):
    t = proj.shape[0]
    tm, halo = min(512, t), 8
    per = tm // halo

    def body(b_ref, c_ref, x_ref, ch_ref, xh_ref, w_ref, y_ref):
        zh = jnp.where(pl.program_id(0) > 0, ch_ref[...] * xh_ref[...], 0.0)
        zext = jnp.concatenate([zh, c_ref[...] * x_ref[...]], axis=0)
        y_ref[...] = (b_ref[...] * _causal_conv(zext, w_ref, CONVA_K, halo)).astype(BF16)

    col = lambda c: bs((tm, BW), lambda i: (i, c))
    hal = lambda c: bs((halo, BW), lambda i: (_prev_blk(i, per), c))
    return _call(body, "conva_fwd", (t // tm,),
                 [col(0), col(1), col(2), hal(1), hal(2), bs((CONVA_K, BW), lambda i: (0, 0))],
                 bs((tm, BW), lambda i: (i, 0)), _sds((t, BW), BF16))(proj, proj, proj, proj, proj, wa)


def conva_bwd(proj, dys, wa, after=None):
    t = proj.shape[0]
    tm, halo = min(512, t), 8
    per = tm // halo
    last = t // halo - 1
    nt = t // tm

    def body(b_ref, c_ref, x_ref, ch_ref, xh_ref, bn_ref, dy_ref, dyn_ref, w_ref, o_ref, dw_ref):
        i = pl.program_id(0)

        @pl.when(i == 0)
        def _():
            dw_ref[...] = jnp.zeros_like(dw_ref)

        zh = jnp.where(i > 0, ch_ref[...] * xh_ref[...], 0.0)
        cv, xv = c_ref[...], x_ref[...]
        zext = jnp.concatenate([zh, cv * xv], axis=0)
        dy = dy_ref[...]
        dconv = dy * b_ref[...]
        dcn = jnp.where(i < nt - 1, dyn_ref[...] * bn_ref[...], 0.0)
        dz = _anti_conv(jnp.concatenate([dconv, dcn], axis=0), w_ref, CONVA_K, tm)
        o_ref[:, :BW] = (dy * _causal_conv(zext, w_ref, CONVA_K, halo)).astype(BF16)
        o_ref[:, BW:2 * BW] = (dz * xv).astype(BF16)
        o_ref[:, 2 * BW:] = (dz * cv).astype(BF16)
        _conv_wgrad(dw_ref, dconv, zext, CONVA_K, halo)

    col = lambda c: bs((tm, BW), lambda i: (i, c))
    hal = lambda c: bs((halo, BW), lambda i: (_prev_blk(i, per), c))
    nxt = bs((halo, BW), lambda i: (_next_blk(i, per, last), 0))
    wsp = bs((CONVA_K, BW), lambda i: (0, 0))
    outs = _call(body, "conva_bwd", (t // tm,),
                 [col(0), col(1), col(2), hal(1), hal(2), nxt,
                  bs((None, tm, BW), lambda i: (0, i, 0)), bs((None, halo, BW), lambda i: (0, _next_blk(i, per, last), 0)), wsp],
                 [bs((tm, 3 * BW), lambda i: (i, 0)), wsp],
                 [_sds((t, N_IN), BF16), _sds((CONVA_K, BW), F32)], after=after)(
                     proj, proj, proj, proj, proj, proj, dys, dys, wa)
    return outs[0], outs[1]


def _head_masks():
    lane = lax.broadcasted_iota(jnp.int32, (1, BW), 1)
    return [(lane >= h * HEAD_D) & (lane < (h + 1) * HEAD_D) for h in range(HEADS)]


def _band_masks():
    qi = lax.broadcasted_iota(jnp.int32, (BLK, BLK), 0)
    ki = lax.broadcasted_iota(jnp.int32, (BLK, BLK), 1)
    return ki >= qi, ki <= qi


def attn_fwd_group(pv, d):
    rows = pv.shape[0]
    qb = min(512, rows)
    nb = qb // BLK
    scale = HEAD_D ** -0.5

    def body(q_ref, k_ref, v_ref, kh_ref, vh_ref, o_ref):
        n = pl.program_id(1)
        hm = _head_masks()
        m_prev, m_cur = _band_masks()
        for b in range(nb):
            rs = slice(b * BLK, (b + 1) * BLK)
            q = q_ref[rs, :]
            if b == 0:
                kp, vp = kh_ref[...], vh_ref[...]
                mp = m_prev & (n > 0)
            else:
                ps = slice((b - 1) * BLK, b * BLK)
                kp, vp = k_ref[ps, :], v_ref[ps, :]
                mp = m_prev
            qs = jnp.concatenate([jnp.where(hm[h], q, 0.0).astype(BF16) for h in range(HEADS)], axis=0)
            kcat = jnp.concatenate([kp, k_ref[rs, :]], axis=0)
            vcat = jnp.concatenate([vp, v_ref[rs, :]], axis=0)
            band = jnp.concatenate([mp, m_cur], axis=1)
            s = jnp.where(jnp.concatenate([band] * HEADS, axis=0), _nt(qs, kcat) * scale, NEG)
            m = jnp.max(s, axis=-1, keepdims=True)
            e = jnp.exp(s - m)
            l = jnp.sum(e, axis=-1, keepdims=True)
            of = _nn(e.astype(BF16), vcat) / l
            lse = m + jnp.log(l)
            o_acc = jnp.zeros((BLK, BW), F32)
            l_acc = jnp.zeros((BLK, BW), F32)
            for h in range(HEADS):
                hs = slice(h * BLK, (h + 1) * BLK)
                o_acc = jnp.where(hm[h], of[hs, :], o_acc)
                l_acc = jnp.where(hm[h], lse[hs, :], l_acc)
            o_ref[rs, :BW] = o_acc
            o_ref[rs, BW:] = l_acc

    per = qb // BLK
    main = lambda c: bs((qb, BW), lambda r, n: (n, r * 3 + c))
    hal = lambda c: bs((BLK, BW), lambda r, n: (_prev_blk(n, per), r * 3 + c))
    return _call(body, f"attn_fwd_d{d}", (d, rows // qb), [main(0), main(1), main(2), hal(1), hal(2)],
                 bs((qb, 2 * BW), lambda r, n: (n, r)), _sds((rows, d * 2 * BW), F32))(pv, pv, pv, pv, pv)


def attn_merge(ols):
    t = ols[0].shape[0]
    tm = min(512, t)
    width = 2 * BW

    def lse3(a, b, c):
        m = jnp.maximum(jnp.maximum(a, b), c)
        return m + jnp.log(jnp.exp(a - m) + jnp.exp(b - m) + jnp.exp(c - m))

    def body(g0, g1, g2, y_ref, o_ref, l_ref, scr, nat1, nat2):
        for src, nat, d in ((g1, nat1, DILATIONS[1]), (g2, nat2, DILATIONS[2])):
            for c in range(width // LANES):
                nat[:, c * LANES:(c + 1) * LANES] = _from_strided_view(src, scr, d, width, c)
        gs = [g0[...], nat1[...], nat2[...]]
        ls = [g[:, BW:] for g in gs]
        tot = lse3(*ls)
        o = (jnp.exp(ls[0] - tot) * gs[0][:, :BW] + jnp.exp(ls[1] - tot) * gs[1][:, :BW]
             + jnp.exp(ls[2] - tot) * gs[2][:, :BW])
        y_ref[...] = o.astype(BF16)
        o_ref[...] = o
        l_ref[...] = tot

    n = bs((tm, BW), lambda i: (i, 0))
    return _call(body, "attn_merge", (t // tm,),
                 [_view_spec(tm, 1, width), _view_spec(tm, DILATIONS[1], width), _view_spec(tm, DILATIONS[2], width)],
                 [n, n, n], [_sds((t, BW), BF16), _sds((t, BW), F32), _sds((t, BW), F32)],
                 scratch=[pltpu.VMEM((tm, LANES), F32), pltpu.VMEM((tm, width), F32), pltpu.VMEM((tm, width), F32)])(*ols)


def attn_delta(dys, o, lse):
    t = o.shape[0]
    tm = min(512, t)

    def body(d_ref, o_ref, l_ref, ld1, ld4, ld16, dy4, dy16, scr):
        hm = _head_masks()
        dy = d_ref[...]
        prod = dy * o_ref[...]
        delta = jnp.zeros_like(prod)
        for h in range(HEADS):
            delta = jnp.where(hm[h], jnp.sum(jnp.where(hm[h], prod, 0.0), axis=-1, keepdims=True), delta)
        ld = jnp.concatenate([l_ref[...], delta], axis=1)
        ld1[...] = ld
        for d, ld_v, dy_v in ((DILATIONS[1], ld4, dy4), (DILATIONS[2], ld16, dy16)):
            _to_strided_view(ld_v, lambda c: ld[:, c * LANES:(c + 1) * LANES], scr, d, 2 * BW)
            _to_strided_view(dy_v, lambda c: dy[:, c * LANES:(c + 1) * LANES], scr, d, BW)

    n = bs((tm, BW), lambda i: (i, 0))
    d4, d16 = DILATIONS[1], DILATIONS[2]
    outs = _call(body, "attn_delta", (t // tm,), [bs((None, tm, BW), lambda i: (1, i, 0)), n, n],
                 [_view_spec(tm, 1, 2 * BW), _view_spec(tm, d4, 2 * BW), _view_spec(tm, d16, 2 * BW),
                  _view_spec(tm, d4, BW), _view_spec(tm, d16, BW)],
                 [_sds((t, 2 * BW), F32), _sds((t // d4, d4 * 2 * BW), F32), _sds((t // d16, d16 * 2 * BW), F32),
                  _sds((t // d4, d4 * BW), F32), _sds((t // d16, d16 * BW), F32)],
                 scratch=[pltpu.VMEM((tm, LANES), F32)])(dys, o, lse)
    return outs[:3], outs[3:]


def attn_bwd_group(pv, dov, ldv, d):
    rows = pv.shape[0]
    qb = min(512, rows)
    nb = qb // BLK
    nsteps = rows // qb
    scale = HEAD_D ** -0.5

    def body(q_ref, qn_ref, k_ref, kh_ref, v_ref, vh_ref, do_ref, don_ref, ld_ref, ldn_ref, o_ref):
        n = pl.program_id(1)
        hm = _head_masks()
        m_prev, m_cur = _band_masks()
        has_prev, has_next = n > 0, n < nsteps - 1
        dq = [None] * nb
        dk = [jnp.zeros((BLK, BW), F32) for _ in range(nb)]
        dvv = [jnp.zeros((BLK, BW), F32) for _ in range(nb)]
        for qi in range(nb + 1):
            rs = slice(qi * BLK, (qi + 1) * BLK)
            ps = slice((qi - 1) * BLK, qi * BLK)
            if qi < nb:
                q, do, ldq = q_ref[rs, :], do_ref[rs, :], ld_ref[rs, :]
            else:
                q, do, ldq = qn_ref[...], don_ref[...], ldn_ref[...]
            kp, vp = (kh_ref[...], vh_ref[...]) if qi == 0 else (k_ref[ps, :], v_ref[ps, :])
            kc, vc = (k_ref[rs, :], v_ref[rs, :]) if qi < nb else (kp, vp)
            mp = m_prev & has_prev if qi == 0 else (m_prev & has_next if qi == nb else m_prev)
            mc = m_cur if qi < nb else jnp.zeros_like(m_cur)
            band = jnp.concatenate([jnp.concatenate([mp, mc], axis=1)] * HEADS, axis=0)
            qs = jnp.concatenate([jnp.where(hm[h], q, 0.0).astype(BF16) for h in range(HEADS)], axis=0)
            dos = jnp.concatenate([jnp.where(hm[h], do, 0.0).astype(BF16) for h in range(HEADS)], axis=0)
            kcat = jnp.concatenate([kp, kc], axis=0)
            vcat = jnp.concatenate([vp, vc], axis=0)
            col = lambda v, h: jnp.broadcast_to(jnp.max(jnp.where(hm[h], v, NEG), axis=-1, keepdims=True), (BLK, 2 * BLK))
            lcols = jnp.concatenate([col(ldq[:, :BW], h) for h in range(HEADS)], axis=0)
            dcols = jnp.concatenate([col(ldq[:, BW:], h) for h in range(HEADS)], axis=0)
            p = jnp.where(band, jnp.exp(_nt(qs, kcat) * scale - lcols), 0.0)
            ds = (p * (_nt(dos, vcat) - dcols) * scale).astype(BF16)
            if qi < nb:
                dqf = _nn(ds, kcat)
                acc_q = jnp.zeros((BLK, BW), F32)
                for h in range(HEADS):
                    acc_q = jnp.where(hm[h], dqf[h * BLK:(h + 1) * BLK, :], acc_q)
                dq[qi] = acc_q
            dkc = _tn(ds, qs)
            dvc = _tn(p.astype(BF16), dos)
            if qi >= 1:
                dk[qi - 1] = dk[qi - 1] + dkc[:BLK]
                dvv[qi - 1] = dvv[qi - 1] + dvc[:BLK]
            if qi < nb:
                dk[qi] = dk[qi] + dkc[BLK:]
                dvv[qi] = dvv[qi] + dvc[BLK:]
        for b in range(nb):
            rs = slice(b * BLK, (b + 1) * BLK)
            for c, val in enumerate((dq[b], dk[b], dvv[b])):
                cs = slice(c * BW, (c + 1) * BW)
                o_ref[rs, cs] = val

    per = qb // BLK
    last = rows // BLK - 1
    main = lambda c: bs((qb, BW), lambda r, n: (n, r * 3 + c))
    prv = lambda c: bs((BLK, BW), lambda r, n: (_prev_blk(n, per), r * 3 + c))
    nxt = lambda c: bs((BLK, BW), lambda r, n: (_next_blk(n, per, last), r * 3 + c))
    accs = bs((qb, 3 * BW), lambda r, n: (n, r))
    in_specs = [main(0), nxt(0), main(1), prv(1), main(2), prv(2),
                bs((qb, BW), lambda r, n: (n, r)), bs((BLK, BW), lambda r, n: (_next_blk(n, per, last), r)),
                bs((qb, 2 * BW), lambda r, n: (n, r)), bs((BLK, 2 * BW), lambda r, n: (_next_blk(n, per, last), r))]
    args = [pv, pv, pv, pv, pv, pv, dov, dov, ldv, ldv]
    return _call(body, f"attn_bwd_d{d}", (d, nsteps), in_specs, accs, _sds((rows, d * 3 * BW), F32))(*args)


def attn_bwd_finish(parts, into):
    t = parts[0].shape[0]
    tm = min(512, t)
    width = 3 * BW

    def body(g0, g1, g2, _, o_ref, scr):
        for c in range(width // LANES):
            cs = slice(c * LANES, (c + 1) * LANES)
            acc = g0[:, cs]
            acc = acc + _from_strided_view(g1, scr, DILATIONS[1], width, c)
            acc = acc + _from_strided_view(g2, scr, DILATIONS[2], width, c)
            o_ref[:, cs] = acc.astype(BF16)

    return _call(body, "attn_bwd_finish", (t // tm,),
                 [_view_spec(tm, 1, width), _view_spec(tm, DILATIONS[1], width), _view_spec(tm, DILATIONS[2], width), ANY],
                 bs((tm, width), lambda i: (i, 1)), _sds(into.shape, BF16),
                 scratch=[pltpu.VMEM((tm, LANES), F32)], aliases={3: 0})(*parts, into)


def _group_masks():
    lane = lax.broadcasted_iota(jnp.int32, (1, BW), 1)
    return [(lane >= g * HEAD_D) & (lane < (g + 1) * HEAD_D) for g in range(4)]


def sgu_fwd(proj, ln_g, ln_b, w_tril, b_full):
    t = proj.shape[0]
    tm = min(512, t)

    def body(u_ref, v_ref, g_ref, b_ref, w_ref, bf_ref, y_ref):
        gm = _group_masks()
        xhat, _ = _ln_hat(v_ref[...])
        vb = (xhat * g_ref[...] + b_ref[...]).astype(BF16)
        for c in range(tm // BLK):
            rs = slice(c * BLK, (c + 1) * BLK)
            vc = vb[rs, :]
            mixed = bf_ref[...]
            for g in range(4):
                mixed = mixed + jnp.where(gm[g], _nn(w_ref[g], vc), 0.0)
            y_ref[rs, :] = (u_ref[rs, :] * mixed).astype(BF16)

    vec = bs((1, BW), lambda i: (0, 0))
    return _call(body, "sgu_fwd", (t // tm,),
                 [bs((tm, BW), lambda i: (i, 6)), bs((tm, BW), lambda i: (i, 7)), vec, vec,
                  bs((4, BLK, BLK), lambda i: (0, 0, 0)), bs((BLK, BW), lambda i: (0, 0))],
                 bs((tm, BW), lambda i: (i, 0)), _sds((t, BW), BF16))(proj, proj, ln_g, ln_b, w_tril, b_full)


def sgu_bwd(proj, dys, ln_g, ln_b, w_tril, b_full, into):
    t = proj.shape[0]
    tm = min(512, t)

    def body(u_ref, v_ref, dy_ref, g_ref, b_ref, w_ref, bf_ref, _, o_ref, dw_ref, dbf_ref, dg_ref, db_ref, dvl_ref):
        @pl.when(pl.program_id(0) == 0)
        def _():
            dw_ref[...] = jnp.zeros_like(dw_ref)
            dbf_ref[...] = jnp.zeros_like(dbf_ref)
            dg_ref[...] = jnp.zeros_like(dg_ref)
            db_ref[...] = jnp.zeros_like(db_ref)

        gm = _group_masks()
        xhat, r = _ln_hat(v_ref[...])
        gv = g_ref[...]
        vb = (xhat * gv + b_ref[...]).astype(BF16)
        for c in range(tm // BLK):
            rs = slice(c * BLK, (c + 1) * BLK)
            vc = vb[rs, :]
            dy = dy_ref[rs, :]
            mixed = bf_ref[...]
            for g in range(4):
                mixed = mixed + jnp.where(gm[g], _nn(w_ref[g], vc), 0.0)
            o_ref[rs, :BW] = (dy * mixed).astype(BF16)
            dm = dy * u_ref[rs, :]
            dbf_ref[...] += dm
            dvl = jnp.zeros((BLK, BW), F32)
            for g in range(4):
                dmg = jnp.where(gm[g], dm, 0.0).astype(BF16)
                dw_ref[g] += _nt(dmg, vc)
                dvl = dvl + _tn(w_ref[g], dmg)
            dvl_ref[rs, :] = dvl
        dvl = dvl_ref[...]
        o_ref[:, BW:] = _ln_bwd(dvl, xhat, r, gv).astype(BF16)
        dg_ref[...] += _colsum(dvl * xhat)
        db_ref[...] += _colsum(dvl)

    vec = bs((1, BW), lambda i: (0, 0))
    row = bs((tm, BW), lambda i: (i, 0))
    wsp = bs((4, BLK, BLK), lambda i: (0, 0, 0))
    bfs = bs((BLK, BW), lambda i: (0, 0))
    return _call(body, "sgu_bwd", (t // tm,),
                 [bs((tm, BW), lambda i: (i, 6)), bs((tm, BW), lambda i: (i, 7)), bs((None, tm, BW), lambda i: (2, i, 0)),
                  vec, vec, wsp, bfs, ANY],
                 [bs((tm, 2 * BW), lambda i: (i, 3)), wsp, bfs, vec, vec],
                 [_sds(into.shape, BF16), _sds((4, BLK, BLK), F32), _sds((BLK, BW), F32),
                  _sds((1, BW), F32), _sds((1, BW), F32)],
                 scratch=[pltpu.VMEM((tm, BW), F32)], aliases={7: 0})(proj, proj, dys, ln_g, ln_b, w_tril, b_full, into)


CONF_HALO = 32


def conf_fwd(proj, dw, ln_g, ln_b, after=None):
    t = proj.shape[0]
    tm, halo = min(512, t), CONF_HALO
    per = tm // halo

    def body(v_ref, gt_ref, vh_ref, gh_ref, w_ref, g_ref, b_ref, y_ref, z_ref):
        yh = jnp.where(pl.program_id(0) > 0, vh_ref[...] * _sigmoid(gh_ref[...]), 0.0)
        yext = jnp.concatenate([yh, v_ref[...] * _sigmoid(gt_ref[...])], axis=0)
        z = _causal_conv(yext, w_ref, CONF_K, halo)
        z_ref[...] = z
        xhat, _ = _ln_hat(z)
        ln = xhat * g_ref[...] + b_ref[...]
        y_ref[...] = (ln * _sigmoid(ln)).astype(BF16)

    vec = bs((1, BW), lambda i: (0, 0))
    col = lambda c: bs((tm, BW), lambda i: (i, c))
    hal = lambda c: bs((halo, BW), lambda i: (_prev_blk(i, per), c))
    row = bs((tm, BW), lambda i: (i, 0))
    return _call(body, "conf_fwd", (t // tm,),
                 [col(8), col(9), hal(8), hal(9), bs((CONF_K, BW), lambda i: (0, 0)), vec, vec],
                 [row, row], [_sds((t, BW), BF16), _sds((t, BW), F32)], after=after)(
                     proj, proj, proj, proj, dw, ln_g, ln_b)


def conf_bwd_ln(z, dys, ln_g, ln_b):
    t = z.shape[0]
    tm = min(1024, t)

    def body(z_ref, dy_ref, g_ref, b_ref, dz_ref, dg_ref, db_ref):
        @pl.when(pl.program_id(0) == 0)
        def _():
            dg_ref[...] = jnp.zeros_like(dg_ref)
            db_ref[...] = jnp.zeros_like(db_ref)

        gv = g_ref[...]
        xhat, r = _ln_hat(z_ref[...])
        ln = xhat * gv + b_ref[...]
        s = _sigmoid(ln)
        dln = dy_ref[...] * (s * (1.0 + ln * (1.0 - s)))
        dz_ref[...] = _ln_bwd(dln, xhat, r, gv)
        dg_ref[...] += _colsum(dln * xhat)
        db_ref[...] += _colsum(dln)

    vec = bs((1, BW), lambda i: (0, 0))
    row = bs((tm, BW), lambda i: (i, 0))
    return _call(body, "conf_bwd_ln", (t // tm,), [row, bs((None, tm, BW), lambda i: (3, i, 0)), vec, vec],
                 [row, vec, vec], [_sds((t, BW), F32), _sds((1, BW), F32), _sds((1, BW), F32)])(z, dys, ln_g, ln_b)


def conf_bwd_conv(proj, dz, dw, into):
    t = proj.shape[0]
    tm, halo = min(512, t), CONF_HALO
    per = tm // halo
    last = t // halo - 1
    nt = t // tm

    def body(v_ref, gt_ref, vh_ref, gh_ref, dz_ref, dzn_ref, w_ref, _, o_ref, dw_ref):
        i = pl.program_id(0)

        @pl.when(i == 0)
        def _():
            dw_ref[...] = jnp.zeros_like(dw_ref)

        val = v_ref[...]
        sg = _sigmoid(gt_ref[...])
        yh = jnp.where(i > 0, vh_ref[...] * _sigmoid(gh_ref[...]), 0.0)
        yext = jnp.concatenate([yh, val * sg], axis=0)
        dz = dz_ref[...]
        dzn = jnp.where(i < nt - 1, dzn_ref[...], 0.0)
        dy0 = _anti_conv(jnp.concatenate([dz, dzn], axis=0), w_ref, CONF_K, tm)
        o_ref[:, :BW] = (dy0 * sg).astype(BF16)
        o_ref[:, BW:] = (dy0 * val * sg * (1.0 - sg)).astype(BF16)
        _conv_wgrad(dw_ref, dz, yext, CONF_K, halo)

    col = lambda c: bs((tm, BW), lambda i: (i, c))
    hal = lambda c: bs((halo, BW), lambda i: (_prev_blk(i, per), c))
    row = bs((tm, BW), lambda i: (i, 0))
    wsp = bs((CONF_K, BW), lambda i: (0, 0))
    return _call(body, "conf_bwd_conv", (t // tm,),
                 [col(8), col(9), hal(8), hal(9), row, bs((halo, BW), lambda i: (_next_blk(i, per, last), 0)), wsp, ANY],
                 [bs((tm, 2 * BW), lambda i: (i, 4)), wsp], [_sds(into.shape, BF16), _sds((CONF_K, BW), F32)],
                 aliases={7: 0})(proj, proj, proj, proj, dz, dz, dw, into)


def _place():
    return lax.axis_index("x"), lax.axis_index("y"), lax.axis_index("c")


HBM_SPEC = pl.BlockSpec(memory_space=pltpu.HBM)
SEM_SPEC = pl.BlockSpec(memory_space=pltpu.SEMAPHORE)
EFFECT = pltpu.SideEffectType.DATAFLOW_SIDE_EFFECTING


class SplitExchange:
    def __init__(self, name, bufs, plan, n_copies):
        self.name, self.bufs, self.plan, self.n = name, list(bufs), plan, n_copies

    def start(self, after):
        nb, n, plan = len(self.bufs), self.n, self.plan

        def body(*refs):
            send, recv, token = refs[nb + 1], refs[nb + 2], refs[-1]
            for k, (src, dst, _, dev) in enumerate(plan(refs[:nb])):
                pltpu.make_async_remote_copy(src_ref=src, dst_ref=dst, send_sem=send.at[k], recv_sem=recv.at[k],
                                             device_id=dev, device_id_type=MESH).start()
            token[...] = jnp.zeros_like(token)

        outs = pl.pallas_call(
            body, name=self.name + "_start",
            out_shape=(pltpu.SemaphoreType.DMA((n,)), pltpu.SemaphoreType.DMA((n,)),
                       *[pltpu.HBM(b.shape, b.dtype) for b in self.bufs], _sds((8, 128), F32)),
            in_specs=[HBM_SPEC] * nb + [ANY],
            out_specs=(SEM_SPEC, SEM_SPEC, *[HBM_SPEC] * nb, pl.BlockSpec(memory_space=pltpu.VMEM)),
            input_output_aliases={i: 2 + i for i in range(nb)},
            compiler_params=pltpu.CompilerParams(has_side_effects=EFFECT))(
                *[pltpu.with_memory_space_constraint(b, pltpu.HBM) for b in self.bufs], after)
        self.send, self.recv, self.bufs = outs[0], outs[1], list(outs[2:2 + nb])
        return outs[-1]

    def wait(self, after):
        nb, plan = len(self.bufs), self.plan
        after = list(after) if isinstance(after, (list, tuple)) else [after]

        def body(*refs):
            send, recv = refs[nb], refs[nb + 1]
            for k, (src, _, land, dev) in enumerate(plan(refs[:nb])):
                cp = pltpu.make_async_remote_copy(src_ref=src, dst_ref=land, send_sem=send.at[k], recv_sem=recv.at[k],
                                                  device_id=dev, device_id_type=MESH)
                cp.wait_send()
                cp.wait_recv()

        outs = pl.pallas_call(
            body, name=self.name + "_wait", out_shape=tuple(pltpu.HBM(b.shape, b.dtype) for b in self.bufs),
            in_specs=[HBM_SPEC] * nb + [SEM_SPEC, SEM_SPEC] + [ANY] * len(after), out_specs=[HBM_SPEC] * nb,
            input_output_aliases={i: i for i in range(nb)},
            compiler_params=pltpu.CompilerParams(has_side_effects=EFFECT))(*self.bufs, self.send, self.recv, *after)
        return list(outs)


def _chips_of(x, y):
    return [(1 - x, y), (x, 1 - y), (1 - x, 1 - y)]


def allgather_ici_plan(shapes):
    def plan(refs):
        x, y, c = _place()
        out = []
        for a, ref in enumerate(refs):
            hl = shapes[a][1] // 2
            half = pl.ds(c * hl, hl)
            for cx, cy in _chips_of(x, y):
                mine = ref.at[2 * x + y, half]
                out.append((mine, mine, ref.at[2 * cx + cy, half], (cx, cy, c)))
        return out
    return plan


def allgather_d2d_plan(shapes):
    def plan(refs):
        x, y, c = _place()
        out = []
        for a, ref in enumerate(refs):
            hl = shapes[a][1] // 2
            for cx, cy in _chips_of(x, y):
                got = ref.at[2 * cx + cy, pl.ds(c * hl, hl)]
                out.append((got, got, ref.at[2 * cx + cy, pl.ds((1 - c) * hl, hl)], (x, y, 1 - c)))
        return out
    return plan


def gather8(v):
    rows, cols = v.shape

    def body(v_ref, land, send, recv, lsem):
        x, y, c = _place()
        me = 4 * x + 2 * y + c
        mine = pltpu.make_async_copy(v_ref, land.at[me], lsem)
        mine.start()
        sent = []
        for j in range(1, 8):
            fx, fy, fc = (j >> 2) & 1, (j >> 1) & 1, j & 1
            tgt = (1 - x if fx else x, 1 - y if fy else y, 1 - c if fc else c)
            cp = pltpu.make_async_remote_copy(src_ref=v_ref, dst_ref=land.at[me], send_sem=send.at[j - 1],
                                              recv_sem=recv.at[j - 1], device_id=tgt, device_id_type=MESH)
            cp.start()
            sent.append(cp)
        for j in range(1, 8):
            fx, fy, fc = (j >> 2) & 1, (j >> 1) & 1, j & 1
            peer = 4 * (1 - x if fx else x) + 2 * (1 - y if fy else y) + (1 - c if fc else c)
            pltpu.make_async_remote_copy(src_ref=v_ref, dst_ref=land.at[peer], send_sem=send.at[j - 1],
                                         recv_sem=recv.at[j - 1], device_id=(x, y, c), device_id_type=MESH).wait_recv()
        for cp in sent:
            cp.wait_send()
        mine.wait()

    vm = pl.BlockSpec(memory_space=pltpu.VMEM)
    return pl.pallas_call(
        body, name="allgather8", in_specs=[vm], out_specs=vm, out_shape=_sds((8, rows, cols), F32),
        scratch_shapes=[pltpu.SemaphoreType.DMA((7,)), pltpu.SemaphoreType.DMA((7,)), pltpu.SemaphoreType.DMA],
        compiler_params=pltpu.CompilerParams(has_side_effects=True, vmem_limit_bytes=VMEM_LIMIT))(v)


def _row_tile(rows, cols):
    best = 16
    for t in range(16, rows + 1, 16):
        if rows % t == 0 and t * cols * 4 <= 2 * 1024 * 1024:
            best = t
    return best


def _rs_add_sibling(scal, g, ra, hr):
    cols = g.shape[2]
    tr = _row_tile(hr, cols)
    nr = hr // tr

    def body(s_ref, g_ref, r_ref, p32_ref, p16_ref):
        v = g_ref[...] + r_ref[...]
        p16_ref[...] = v.astype(BF16)

        @pl.when(pl.program_id(1) == s_ref[0])
        def _():
            p32_ref[...] = v

    blk = lambda f: bs((None, tr, cols), f)
    own = blk(lambda i, s, sr: (s, i, 0))
    spec = pltpu.PrefetchScalarGridSpec(num_scalar_prefetch=1, grid=(nr, N_SH),
                                        in_specs=[blk(lambda i, s, sr: (s, sr[1] * nr + i, 0)), own],
                                        out_specs=[bs((tr, cols), lambda i, s, sr: (i, 0)), own])
    return pl.pallas_call(body, name="rs_add_sibling", grid_spec=spec,
                          out_shape=[_sds((hr, cols), F32), _sds((N_SH, hr, cols), BF16)],
                          compiler_params=pltpu.CompilerParams(dimension_semantics=("arbitrary",) * 2,
                                                               vmem_limit_bytes=VMEM_LIMIT))(scal, g, ra)


def _rs_add_chips(scal, p32, rb, hr):
    cols = p32.shape[1]
    tr = _row_tile(hr, cols)
    nr = hr // tr

    def body(s_ref, p_ref, r0, r1, r2, o_ref):
        o_ref[...] = ((p_ref[...] + r0[...].astype(F32)) + r1[...].astype(F32)) + r2[...].astype(F32)

    blk = lambda f: bs((None, tr, cols), f)
    spec = pltpu.PrefetchScalarGridSpec(
        num_scalar_prefetch=1, grid=(nr,),
        in_specs=[bs((tr, cols), lambda i, sr: (i, 0))] + [blk(functools.partial(lambda i, sr, j: (j, i, 0), j=j))
                                                            for j in range(3)],
        out_specs=blk(lambda i, sr: (sr[1], i, 0)))
    return pl.pallas_call(body, name="rs_add_chips", grid_spec=spec, out_shape=_sds((2, hr, cols), F32),
                          compiler_params=pltpu.CompilerParams(dimension_semantics=("arbitrary",),
                                                               vmem_limit_bytes=VMEM_LIMIT))(scal, p32, rb, rb, rb)


class SplitReduceScatter:
    def __init__(self, gs):
        x, y, c = _place()
        self.scal = jnp.stack([2 * x + y, c]).astype(jnp.int32)
        self.gs, self.n = list(gs), len(gs)
        self.hrs = [g.shape[1] // 2 for g in gs]

    def swap_start(self, after):
        n, hrs = self.n, self.hrs

        def plan(refs):
            x, y, c = _place()
            return [(refs[a].at[:, pl.ds((1 - c) * hrs[a], hrs[a])], refs[n + a], refs[n + a], (x, y, 1 - c))
                    for a in range(n)]

        lands = [lax.empty((N_SH, hrs[a], g.shape[2]), F32) for a, g in enumerate(self.gs)]
        self.ex = SplitExchange("rs_swap_halves", self.gs + lands, plan, n)
        return self.ex.start(after)

    def swap_wait_send_start(self, after):
        n, hrs = self.n, self.hrs
        bufs = self.ex.wait(after)
        parts = [_rs_add_sibling(self.scal, bufs[a], bufs[n + a], hrs[a]) for a in range(n)]
        self.p32 = [p[0] for p in parts]

        def plan(refs):
            x, y, c = _place()
            return [(refs[a].at[2 * cx + cy], refs[n + a].at[j], refs[n + a].at[j], (cx, cy, c))
                    for a in range(n) for j, (cx, cy) in enumerate(_chips_of(x, y))]

        lands = [lax.empty((3, hrs[a], g.shape[2]), BF16) for a, g in enumerate(self.gs)]
        self.ex = SplitExchange("rs_send_partials", [p[1] for p in parts] + lands, plan, 3 * n)
        return self.ex.start(parts[-1][1])

    def send_wait_share_start(self, after):
        n, hrs = self.n, self.hrs
        bufs = self.ex.wait(after)
        fins = [_rs_add_chips(self.scal, self.p32[a], bufs[n + a], hrs[a]) for a in range(n)]

        def plan(refs):
            x, y, c = _place()
            return [(refs[a].at[c], refs[a].at[c], refs[a].at[1 - c], (x, y, 1 - c)) for a in range(n)]

        self.ex = SplitExchange("rs_share_halves", fins, plan, n)
        return self.ex.start(fins[-1])

    def share_wait(self, after):
        fulls = self.ex.wait(after)
        return [f.reshape(2 * hr, f.shape[2]) for f, hr in zip(fulls, self.hrs)]


def adamw(w, g, m, v):
    shape = w.shape
    cols = shape[-1]
    rows = math.prod(shape[:-1]) if len(shape) > 1 else 1
    tr = 256 if rows % 256 == 0 and rows > 256 else rows
    c1 = 1.0 - ADAM_B1 ** ADAM_STEP
    c2 = 1.0 - ADAM_B2 ** ADAM_STEP

    def body(w_ref, g_ref, m_ref, v_ref, d_ref, nm_ref, nv_ref):
        gv = g_ref[...]
        nm = ADAM_B1 * m_ref[...] + (1.0 - ADAM_B1) * gv
        nv = ADAM_B2 * v_ref[...] + (1.0 - ADAM_B2) * (gv * gv)
        nm_ref[...] = nm
        nv_ref[...] = nv
        d_ref[...] = -ADAM_LR * ((nm / c1) / (jnp.sqrt(nv / c2) + ADAM_EPS) + ADAM_WD * w_ref[...])

    row = bs((tr, cols), lambda i: (i, 0))
    outs = _call(body, "adamw", (rows // tr,), [row] * 4, [row] * 3, [_sds((rows, cols), F32)] * 3)(
        *[a.reshape(rows, cols) for a in (w, g, m, v)])
    return [o.reshape(shape) for o in outs]


def adamw_layers(w, gs, m, v, lo, into=None, after=None):
    shape = w.shape
    cols = shape[-1]
    rl = math.prod(shape[1:-1])
    tr = max(t_ for t_ in range(8, rl + 1, 8) if rl % t_ == 0 and t_ * cols * 4 <= 1024 * 1024)
    nb = rl // tr
    n = len(gs)
    c1 = 1.0 - ADAM_B1 ** ADAM_STEP
    c2 = 1.0 - ADAM_B2 ** ADAM_STEP

    def body(*refs):
        w_ref, m_ref, v_ref = refs[:3]
        g_refs = refs[3:3 + n]
        d_ref, nm_ref, nv_ref, go_ref = refs[-4:]
        layer = pl.program_id(0) // nb
        for k in range(n):
            @pl.when(layer == k)
            def _(k=k):
                gv = g_refs[k][...]
                nm = ADAM_B1 * m_ref[...] + (1.0 - ADAM_B1) * gv
                nv = ADAM_B2 * v_ref[...] + (1.0 - ADAM_B2) * (gv * gv)
                nm_ref[...] = nm
                nv_ref[...] = nv
                go_ref[...] = gv
                d_ref[...] = -ADAM_LR * ((nm / c1) / (jnp.sqrt(nv / c2) + ADAM_EPS) + ADAM_WD * w_ref[...])

    row = bs((tr, cols), lambda b: (lo * nb + b, 0))
    g_specs = [bs((tr, cols), functools.partial(lambda b, k: (jnp.clip(b - k * nb, 0, nb - 1), 0), k=k)) for k in range(n)]
    flat = lambda a: a.reshape(-1, cols)
    in_specs = [row] * 3 + g_specs
    args = [flat(w), flat(m), flat(v)] + [flat(g) for g in gs]
    aliases = None
    if into is not None:
        aliases = {len(in_specs) + k: k for k in range(4)}
        in_specs = in_specs + [ANY] * 4
        args = args + [flat(a) for a in into]
    outs = _call(body, "adamw_layers", (n * nb,), in_specs, [row] * 4, [_sds((shape[0] * rl, cols), F32)] * 4,
                 aliases=aliases, after=after)(*args)
    return [o.reshape(shape) for o in outs]


def allreduce8_split(vec):
    rows, cols = vec.shape

    def plan(refs):
        x, y, c = _place()
        me = 4 * x + 2 * y + c
        out = []
        for j in range(1, 8):
            px, py, pc = (1 - x if j & 4 else x), (1 - y if j & 2 else y), (1 - c if j & 1 else c)
            out.append((refs[0], refs[1].at[me], refs[1].at[4 * px + 2 * py + pc], (px, py, pc)))
        return out

    ex = SplitExchange("allreduce8", [vec, lax.empty((8, rows, cols), F32)], plan, 7)

    def finish(after):
        v, land = ex.wait(after)
        x, y, c = _place()
        me = jnp.reshape(4 * x + 2 * y + c, (1,)).astype(jnp.int32)

        def body(me_ref, v_ref, l_ref, o_ref):
            o_ref[...] = jnp.zeros_like(o_ref)
            for k in range(8):
                @pl.when(me_ref[0] == k)
                def _():
                    o_ref[...] += v_ref[...]

                @pl.when(me_ref[0] != k)
                def _(k=k):
                    o_ref[...] += l_ref[k]

        spec = pltpu.PrefetchScalarGridSpec(
            num_scalar_prefetch=1, grid=(1,),
            in_specs=[bs((rows, cols), lambda i, mr: (0, 0)), bs((8, rows, cols), lambda i, mr: (0, 0, 0))],
            out_specs=bs((rows, cols), lambda i, mr: (0, 0)))
        return pl.pallas_call(body, name="allreduce8_sum", grid_spec=spec, out_shape=_sds((rows, cols), F32),
                              compiler_params=pltpu.CompilerParams(dimension_semantics=("arbitrary",),
                                                                   vmem_limit_bytes=VMEM_LIMIT))(me, v, land)

    return ex, finish


class Hooks:
    def __init__(self):
        self.steps = {}

    def add(self, point, fn):
        self.steps.setdefault(point, []).append(fn)

    def run(self, point, arr, env=None):
        tok = None
        for fn in self.steps.get(point, ()):
            got = fn(arr if tok is None else tok, env)
            tok = tok if got is None else got
        return tok


def layer_fwd(x, p_i, w, hooks):
    h, proj, *qkv = norm_in_proj(x, w["g_mix"], w["win"], after=hooks.run("start", x))
    ya = conva_fwd(proj, w["conv_a"])
    yb, o32, lse = attn_merge([attn_fwd_group(pv, d) for pv, d in zip(qkv, DILATIONS)])
    yc = sgu_fwd(proj, w["sgu_ln_g"], w["sgu_ln_b"], w["sgu_wt"], w["sgu_bf"])
    yd, z = conf_fwd(proj, w["conf_dw"], w["conf_ln_g"], w["conf_ln_b"], after=hooks.run("pre_conf", [ya, yb, yc]))
    ys = (ya, yb, yc, yd)
    tok = hooks.run("pre_merge", yd)
    merged, gates, ybr = merge_fwd(h, ys, w["wg"], w["wbr"], after=tok)
    x1, h2, fgu, act = ffn_in(merged, w["wout"], x, w["g_ffn"], w["wfi"], after=hooks.run("post_merge", merged))
    x2, h3, gate, pp, x3 = ple_fwd(act, w["wfo"], x1, w["g_ple"], w["wpg"], p_i, w["wpp"],
                                   after=hooks.run("post_ffn_in", act))
    saved = dict(x=x, h=h, proj=proj, qkv=qkv, ys=ys, o32=o32, lse=lse, z=z, merged=merged, gates=gates, ybr=ybr, x1=x1,
                 h2=h2, fgu=fgu, act=act, x2=x2, h3=h3, gate=gate, pp=pp)
    return x3, saved


def layer_bwd(dx3, p_i, w, s, hooks):
    t = dx3.shape[0]
    tr = min(1024, t)
    nr = t // tr
    ns_fi = FFN_H // 2
    small = {}

    ga_shape, gb_shape = _sds((N_SH, 6 * BW, D_MODEL), F32), _sds((N_SH, 5 * BW, BW), F32)
    ga_blk = lambda idx: bs((N_SH, BW, D_MODEL), idx)
    dx2, small["g_ple"], ga, gb = ple_bwd(dx3, s["gate"], s["pp"], w["wpg"], s["x2"], w["g_ple"], s["h3"], p_i,
                                          ga_shape, gb_shape, after=hooks.run("start", dx3))

    df, dx1, small["g_ffn"] = ffn_bwd(dx2, w["wfo"], s["fgu"], w["wfi"], s["x1"], w["g_ffn"],
                                      after=hooks.run("pre_ffn", dx2))
    gfo = tn_matmul("dw_ffn_out", s["act"], dx2, (2, nr), bs((tr, ns_fi), lambda j, r: (r, j)),
                    bs((tr, D_MODEL), lambda j, r: (r, 0)), bs((2, FFN_H // N_SH, D_MODEL), lambda j, r: (j, 0, 0)),
                    _sds((N_SH, FFN_H // N_SH, D_MODEL), F32), split=2)
    gfi = tn_matmul("dw_ffn_in", s["h2"], df, (N_SH, nr), bs((tr, D_MODEL), lambda j, r: (r, 0)),
                    bs((None, tr, ns_fi), lambda j, r: (j // 2, r, j % 2)),
                    bs((None, D_MODEL, ns_fi), lambda j, r: (j, 0, 0)), _sds((N_SH, D_MODEL, ns_fi), F32))

    dpre_m, dys, gb = merge_bwd(dx1, w["wout"], s["gates"], s["ybr"], w["wbr"], s["ys"], gb)
    ga = tn_matmul("dw_out", s["merged"], dx1, (nr,), bs((tr, D_MODEL), lambda r: (r, 0)),
                   bs((tr, D_MODEL), lambda r: (r, 0)), ga_blk(lambda r: (0, 4, 0)), ga_shape, split=N_SH, into=ga,
                   after=hooks.run("pre_dw_out", dys, dict(gfo=gfo, gfi=gfi)))
    ga = tn_matmul("dw_merge_gate", s["h"], dpre_m, (N_BR, nr), bs((tr, D_MODEL), lambda k, r: (r, 0)),
                   bs((None, tr, D_MODEL), lambda k, r: (k, r, 0)), ga_blk(lambda k, r: (0, k, 0)), ga_shape,
                   split=N_SH, into=ga)

    dproj, small["conv_a"] = conva_bwd(s["proj"], dys, w["conv_a"], after=hooks.run("pre_conva", gb))
    lds, dy_views = attn_delta(dys, s["o32"], s["lse"])
    dy_views = [dys[1]] + list(dy_views)
    dproj = attn_bwd_finish([attn_bwd_group(pv, dov, ldv, d)
                             for pv, dov, ldv, d in zip(s["qkv"], dy_views, lds, DILATIONS)], dproj)
    dproj, d_sw, d_sbf, small["sgu_ln_g"], small["sgu_ln_b"] = sgu_bwd(
        s["proj"], dys, w["sgu_ln_g"], w["sgu_ln_b"], w["sgu_wt"], w["sgu_bf"], dproj)
    small["sgu_w"] = jnp.where(jnp.tril(jnp.ones((BLK, BLK), bool))[None], d_sw, 0.0)
    small["sgu_b"] = jnp.sum(d_sbf.reshape(BLK, 4, HEAD_D), axis=-1).T
    dz, small["conf_ln_g"], small["conf_ln_b"] = conf_bwd_ln(s["z"], dys, w["conf_ln_g"], w["conf_ln_b"])
    dproj, small["conf_dw"] = conf_bwd_conv(s["proj"], dz, w["conf_dw"], dproj)

    ns_in = N_IN // N_SH
    gin = tn_matmul("dw_in", s["h"], dproj, (N_SH, nr), bs((tr, D_MODEL), lambda j, r: (r, 0)),
                    bs((tr, ns_in), lambda j, r: (r, j)), bs((None, D_MODEL, ns_in), lambda j, r: (j, 0, 0)),
                    _sds((N_SH, D_MODEL, ns_in), F32), after=hooks.run("pre_dw_in", dproj))
    hooks.run("end", gin)
    big = [ga, gfo, gb, gin, gfi]
    dx, small["g_mix"] = norm_bwd(
        "mix_norm_bwd",
        [(dpre_m, lambda tm: bs((N_BR, tm, D_MODEL), lambda i: (0, i, 0)), w["wg"],
          lambda a, wr: [(a[k], wr[k]) for k in range(N_BR)]),
         (dproj, lambda tm: bs((tm, N_IN), lambda i: (i, 0)), w["win"],
          lambda a, wr: [(a[:, k * ns_in:(k + 1) * ns_in], wr[k]) for k in range(N_SH)])],
        dx1, s["x"], w["g_mix"])
    return dx, big, small


BIG_NAMES = ("w_in", "w_branch", "w_merge_gate", "w_out", "w_ffn_in", "w_ffn_out", "w_ple_gate", "w_ple_proj")


def unpack_big_grads(ga, gfo, gb, gin, gfi):
    return dict(w_in=gin, w_ffn_in=gfi, w_ffn_out=gfo,
                w_merge_gate=ga[:N_BR * BW].reshape(N_BR, BW, D_MODEL), w_out=ga[N_BR * BW:5 * BW], w_ple_gate=ga[5 * BW:],
                w_branch=gb[:N_BR * BW].reshape(N_BR, BW, BW), w_ple_proj=gb[N_BR * BW:])


SMALL_NAMES = ("g_mix", "conv_a", "sgu_ln_g", "sgu_ln_b", "sgu_w", "sgu_b", "conf_dw", "conf_ln_g", "conf_ln_b",
               "g_ffn", "g_ple")


def _pack_rows(arrays, rows):
    flat = jnp.concatenate([a.reshape(-1) for a in arrays])
    return jnp.pad(flat, (0, rows * D_MODEL - flat.shape[0])).reshape(rows, D_MODEL)


def _unpack_rows(packed, shapes):
    flat, out, pos = packed.reshape(-1), [], 0
    for shape in shapes:
        n = math.prod(shape)
        out.append(flat[pos:pos + n].reshape(shape))
        pos += n
    return out


def kernel(x, p, g_mix, w_in, conv_a, sgu_ln_g, sgu_ln_b, sgu_w, sgu_b, conf_dw, conf_ln_g, conf_ln_b, w_branch, w_merge_gate, w_out, g_ffn, w_ffn_in, w_ffn_out, g_ple, w_ple_gate, w_ple_proj, g_final, loss_target, m_g_mix, m_w_in, m_conv_a, m_sgu_ln_g, m_sgu_ln_b, m_sgu_w, m_sgu_b, m_conf_dw, m_conf_ln_g, m_conf_ln_b, m_w_branch, m_w_merge_gate, m_w_out, m_g_ffn, m_w_ffn_in, m_w_ffn_out, m_g_ple, m_w_ple_gate, m_w_ple_proj, m_g_final, v_g_mix, v_w_in, v_conv_a, v_sgu_ln_g, v_sgu_ln_b, v_sgu_w, v_sgu_b, v_conf_dw, v_conf_ln_g, v_conf_ln_b, v_w_branch, v_w_merge_gate, v_w_out, v_g_ffn, v_w_ffn_in, v_w_ffn_out, v_g_ple, v_w_ple_gate, v_w_ple_proj, v_g_final):
    weights = dict(g_mix=g_mix, w_in=w_in, conv_a=conv_a, sgu_ln_g=sgu_ln_g, sgu_ln_b=sgu_ln_b, sgu_w=sgu_w, sgu_b=sgu_b,
                   conf_dw=conf_dw, conf_ln_g=conf_ln_g, conf_ln_b=conf_ln_b, w_branch=w_branch, w_merge_gate=w_merge_gate,
                   w_out=w_out, g_ffn=g_ffn, w_ffn_in=w_ffn_in, w_ffn_out=w_ffn_out, g_ple=g_ple, w_ple_gate=w_ple_gate,
                   w_ple_proj=w_ple_proj, g_final=g_final)
    m_in = dict(g_mix=m_g_mix, w_in=m_w_in, conv_a=m_conv_a, sgu_ln_g=m_sgu_ln_g, sgu_ln_b=m_sgu_ln_b, sgu_w=m_sgu_w,
                sgu_b=m_sgu_b, conf_dw=m_conf_dw, conf_ln_g=m_conf_ln_g, conf_ln_b=m_conf_ln_b, w_branch=m_w_branch,
                w_merge_gate=m_w_merge_gate, w_out=m_w_out, g_ffn=m_g_ffn, w_ffn_in=m_w_ffn_in, w_ffn_out=m_w_ffn_out,
                g_ple=m_g_ple, w_ple_gate=m_w_ple_gate, w_ple_proj=m_w_ple_proj, g_final=m_g_final)
    v_in = dict(g_mix=v_g_mix, w_in=v_w_in, conv_a=v_conv_a, sgu_ln_g=v_sgu_ln_g, sgu_ln_b=v_sgu_ln_b, sgu_w=v_sgu_w,
                sgu_b=v_sgu_b, conf_dw=v_conf_dw, conf_ln_g=v_conf_ln_g, conf_ln_b=v_conf_ln_b, w_branch=v_w_branch,
                w_merge_gate=v_w_merge_gate, w_out=v_w_out, g_ffn=v_g_ffn, w_ffn_in=v_w_ffn_in, w_ffn_out=v_w_ffn_out,
                g_ple=v_g_ple, w_ple_gate=v_w_ple_gate, w_ple_proj=v_w_ple_proj, g_final=v_g_final)
    order = ("g_mix", "w_in", "conv_a", "sgu_ln_g", "sgu_ln_b", "sgu_w", "sgu_b", "conf_dw", "conf_ln_g", "conf_ln_b",
             "w_branch", "w_merge_gate", "w_out", "g_ffn", "w_ffn_in", "w_ffn_out", "g_ple", "w_ple_gate", "w_ple_proj",
             "g_final")
    depth = g_mix.shape[0]
    xs, tgt = x[0], loss_target[0]
    cw = BW // N_SH
    my_shard = 2 * lax.axis_index("x") + lax.axis_index("y")

    conv_rows = 16
    allc = gather8(_pack_rows([conv_a, conf_dw], conv_rows))
    shards = [_unpack_rows(allc[2 * s], [conv_a.shape, conf_dw.shape]) for s in range(N_SH)]
    conv_a_full = jnp.concatenate([sh[0] for sh in shards], axis=-1)
    conf_dw_full = jnp.concatenate([sh[1] for sh in shards], axis=-1)

    tril = jnp.tril(jnp.ones((BLK, BLK), bool))
    def placed_shards(i):
        shards = ([w_in[i], w_branch[i]] + [w_merge_gate[i, k] for k in range(N_BR)]
                  + [w_out[i], w_ffn_in[i], w_ffn_out[i], w_ple_gate[i], w_ple_proj[i]])
        return [lax.dynamic_update_slice(lax.empty((N_SH,) + sh.shape, BF16), sh.astype(BF16)[None],
                                         (my_shard,) + (0,) * sh.ndim) for sh in shards]

    def small_weights(i, win):
        vec = lambda a: a[i].reshape(1, -1)
        return dict(
            win=win, g_mix=vec(g_mix), g_ffn=vec(g_ffn), g_ple=vec(g_ple), conv_a=conv_a_full[i], conf_dw=conf_dw_full[i],
            sgu_ln_g=vec(sgu_ln_g), sgu_ln_b=vec(sgu_ln_b), conf_ln_g=vec(conf_ln_g), conf_ln_b=vec(conf_ln_b),
            sgu_wt=jnp.where(tril[None], sgu_w[i], 0.0).astype(BF16),
            sgu_bf=jnp.repeat(sgu_b[i].T, HEAD_D, axis=1))

    def late_weights(got):
        return dict(wbr=got[0], wg=jnp.stack([g.reshape(D_MODEL, D_MODEL) for g in got[1:5]]),
                    wout=got[5].reshape(D_MODEL, D_MODEL), wfi=got[6], wfo=got[7].reshape(FFN_H, D_MODEL),
                    wpg=got[8].reshape(D_MODEL, D_MODEL), wpp=got[9])

    class SplitAllGather:
        def __init__(self, bufs):
            self.shapes = [b.shape for b in bufs]
            self.ici = SplitExchange("allgather_ici", bufs, allgather_ici_plan(self.shapes), 3 * len(bufs))

        def ici_start(self, after, env=None):
            return self.ici.start(after)

        def ici_wait_d2d_start(self, after, env=None):
            landed = self.ici.wait(after)
            self.d2d = SplitExchange("allgather_d2d", landed, allgather_d2d_plan(self.shapes), 3 * len(landed))
            return self.d2d.start(landed[-1])

        def d2d_wait(self, after, env=None):
            self.got = self.d2d.wait(after)
            return None

    bufs0 = placed_shards(0)
    first = SplitAllGather(bufs0[:1])
    first.d2d_wait(first.ici_wait_d2d_start(first.ici_start(xs)))
    rest = SplitAllGather(bufs0[1:])
    layers = [small_weights(0, first.got[0])]
    act, saved = xs, []
    nxt_done = None
    for i in range(depth):
        hooks = Hooks()
        if i == 0:
            hooks.add("start", rest.ici_start)
            hooks.add("pre_conf", rest.ici_wait_d2d_start)
            hooks.add("pre_merge", rest.d2d_wait)
            hooks.add("pre_merge", lambda after, env: layers[0].update(late_weights(rest.got)))
        if i + 1 < depth:
            nxt = SplitAllGather(placed_shards(i + 1))
            points = ("pre_merge", "post_ffn_in", None) if i == 0 else ("start", "post_merge", "post_ffn_in")
            hooks.add(points[0], nxt.ici_start)
            hooks.add(points[1], nxt.ici_wait_d2d_start)
            if points[2]:
                hooks.add(points[2], nxt.d2d_wait)
        act, sv = layer_fwd(act, p[i, 0], layers[i], hooks)
        saved.append(sv)
        if i + 1 < depth:
            if i == 0:
                nxt.d2d_wait(act)
            layers.append({**small_weights(i + 1, nxt.got[0]), **late_weights(nxt.got[1:])})
    loss_part, dx, dg_final = loss_head(act, g_final.reshape(1, -1), tgt)

    big_red = [None] * depth
    small_red = [None] * depth
    small_rows = 80
    pending = None

    def small_vector(i, small):
        parts = [small[n] for n in SMALL_NAMES]
        return _pack_rows(parts + ([dg_final, loss_part[0, :1]] if i == 0 else []), small_rows)

    for i in reversed(range(depth)):
        hooks = Hooks()
        result = {}
        if pending is not None:
            rs, j, (small_ex, small_finish) = pending
            hooks.add("start", lambda after, env, ex=small_ex: ex.start(after))
            hooks.add("start", lambda after, env, rs=rs: rs.swap_start(after))
            hooks.add("pre_ffn", lambda after, env, rs=rs: rs.swap_wait_send_start(after))
            hooks.add("pre_dw_out", lambda after, env, rs=rs: rs.send_wait_share_start(after))
            hooks.add("pre_conva", lambda after, env, rs=rs, result=result: result.update(prev=rs.share_wait(after)))
            hooks.add("pre_conva", lambda after, env, fin=small_finish, result=result: result.update(small=fin(after)))
        if i == 0:
            def early_start(after, env, result=result):
                result["rs"] = SplitReduceScatter([env["gfo"], env["gfi"]])
                return result["rs"].swap_start(after)

            hooks.add("pre_dw_out", early_start)
            hooks.add("pre_conva", lambda after, env, result=result: result["rs"].swap_wait_send_start(after))
            hooks.add("pre_dw_in", lambda after, env, result=result: result["rs"].send_wait_share_start(after))
            hooks.add("end", lambda after, env, result=result: result.update(early=result["rs"].share_wait(after)))
        dx, big, small = layer_bwd(dx, p[i, 0], layers[i], saved[i], hooks)
        if pending is not None:
            big_red[pending[1]] = unpack_big_grads(*result["prev"])
            small_red[pending[1]] = result["small"]
        if i > 0:
            pending = (SplitReduceScatter(big), i, allreduce8_split(small_vector(i, small)))

    ga, _, gb, gin, _ = big
    late = SplitReduceScatter([ga, gb, gin])
    small_ex, small_finish = allreduce8_split(small_vector(0, small))
    upd = {}

    def update_upper(names, after):
        for name in names:
            upd[name] = adamw_layers(weights[name], [big_red[i][name] for i in range(1, depth)], m_in[name], v_in[name],
                                     1, after=after)
        return [upd[name][0] for name in names]

    done = update_upper(("w_in",), late.swap_start(small_ex.start(dx)))
    done = update_upper(("w_ffn_in", "w_merge_gate", "w_ffn_out"), late.swap_wait_send_start(done))
    small_red[0] = small_finish(done)
    done = update_upper(("w_branch", "w_out", "w_ple_gate", "w_ple_proj"), late.send_wait_share_start(done))
    ga, gb, gin = late.share_wait(done)
    gfo, gfi = result["early"]
    big_red[0] = unpack_big_grads(ga, gfo, gb, gin, gfi)

    layer_shapes = [small[n].shape for n in SMALL_NAMES]
    per_layer = [_unpack_rows(small_red[i], layer_shapes + ([dg_final.shape, (1,)] if i == 0 else []))
                 for i in range(depth)]
    grads = {n: jnp.stack([per_layer[i][k].reshape(weights[n].shape[1:] if n not in ("conv_a", "conf_dw")
                                                   else per_layer[i][k].shape) for i in range(depth)])
             for k, n in enumerate(SMALL_NAMES)}
    grads["g_final"] = per_layer[0][-2].reshape(-1)
    loss = per_layer[0][-1].reshape(())
    for n in ("conv_a", "conf_dw"):
        grads[n] = lax.dynamic_slice_in_dim(grads[n], my_shard * cw, cw, axis=2)

    small_all = [n for n in order if n not in BIG_NAMES]
    sm_shapes = [weights[n].shape for n in small_all]
    n_sm = sum(math.prod(sh) for sh in sm_shapes)
    sm_rows = -(-n_sm // (8 * D_MODEL)) * 8
    packed = [_pack_rows([src[n] for n in small_all], sm_rows) for src in (weights, grads, m_in, v_in)]
    sm_out = [_unpack_rows(o, sm_shapes) for o in adamw(*packed)]
    delta, new_m, new_v = ({n: o[k] for k, n in enumerate(small_all)} for o in sm_out)
    for name in BIG_NAMES:
        delta[name], new_m[name], new_v[name], grads[name] = adamw_layers(
            weights[name], [big_red[0][name]], m_in[name], v_in[name], 0, into=upd[name])

    return (loss, dx[None], *[grads[n] for n in order], *[delta[n] for n in order], *[new_m[n] for n in order],
            *[new_v[n] for n in order])
```

```python
import functools
import math

import jax
import jax.numpy as jnp
from jax import lax
from jax.experimental import pallas as pl
from jax.experimental.pallas import tpu as pltpu

F32 = jnp.float32
BF16 = jnp.bfloat16
EPS = 1e-6
D_MODEL = 1024
BW = 256
N_BR = 4
N_IN = 10 * BW
FFN_H = 2816
N_SH = 4
HEADS = 4
HEAD_D = 64
BLK = 128
DILATIONS = (1, 4, 16)
CONF_K = 31
CONVA_K = 3
NEG = -1e30
VMEM_LIMIT = 56 * 1024 * 1024
MESH = pl.DeviceIdType.MESH

ADAM_LR, ADAM_B1, ADAM_B2, ADAM_EPS, ADAM_WD, ADAM_STEP = 0.001, 0.9, 0.999, 1e-08, 0.01, 10

bs = pl.BlockSpec
ANY = pl.BlockSpec(memory_space=pl.ANY)


def _call(body, name, grid, in_specs, out_specs, out_shape, scratch=(), aliases=None, after=None):
    n_in = len(in_specs)
    kernel_body = body
    if after is not None:
        in_specs = list(in_specs) + [ANY]

        def kernel_body(*refs):
            return body(*refs[:n_in], *refs[n_in + 1:])

    call = pl.pallas_call(
        kernel_body, name=name, grid=grid, in_specs=in_specs, out_specs=out_specs, out_shape=out_shape,
        scratch_shapes=list(scratch), input_output_aliases=aliases or {},
        compiler_params=pltpu.CompilerParams(dimension_semantics=("arbitrary",) * len(grid),
                                             vmem_limit_bytes=VMEM_LIMIT))
    return call if after is None else (lambda *args: call(*args, after))


def _sds(shape, dtype):
    return jax.ShapeDtypeStruct(shape, dtype)


def _nn(a, b):
    return jnp.dot(a, b, preferred_element_type=F32)


def _nt(a, b):
    return lax.dot_general(a, b, (((1,), (1,)), ((), ())), preferred_element_type=F32)


def _tn(a, b):
    return lax.dot_general(a, b, (((0,), (0,)), ((), ())), preferred_element_type=F32)


def _sigmoid(x):
    return 1.0 / (1.0 + jnp.exp(-x))


def _rms_fwd(x, g):
    r = lax.rsqrt(jnp.mean(x * x, axis=-1, keepdims=True) + EPS)
    return x * r * g


def _rms_bwd(dh, x, g):
    r = lax.rsqrt(jnp.mean(x * x, axis=-1, keepdims=True) + EPS)
    xr = x * r
    dxr = dh * g
    dx = r * (dxr - xr * jnp.mean(dxr * xr, axis=-1, keepdims=True))
    return dx, dh * xr


def _ln_hat(x):
    mu = jnp.mean(x, axis=-1, keepdims=True)
    xc = x - mu
    r = lax.rsqrt(jnp.mean(xc * xc, axis=-1, keepdims=True) + EPS)
    return xc * r, r


def _ln_bwd(dy, xhat, r, g):
    dxh = dy * g
    return r * (dxh - jnp.mean(dxh, axis=-1, keepdims=True) - xhat * jnp.mean(dxh * xhat, axis=-1, keepdims=True))


def _colsum(v):
    return jnp.sum(v, axis=0, keepdims=True)


def _causal_conv(zext, w_ref, k_taps, halo):
    acc = zext[halo:] * w_ref[k_taps - 1:k_taps, :]
    for k in range(k_taps - 1):
        acc = acc + pltpu.roll(zext, k_taps - 1 - k, 0)[halo:] * w_ref[k:k + 1, :]
    return acc


def _anti_conv(dext, w_ref, k_taps, tm):
    n = dext.shape[0]
    acc = dext[:tm] * w_ref[k_taps - 1:k_taps, :]
    for s in range(1, k_taps):
        acc = acc + pltpu.roll(dext, n - s, 0)[:tm] * w_ref[k_taps - 1 - s:k_taps - s, :]
    return acc


def _conv_wgrad(dw_ref, dc, zext, k_taps, halo):
    dw_ref[k_taps - 1:k_taps, :] += _colsum(dc * zext[halo:])
    for k in range(k_taps - 1):
        dw_ref[k:k + 1, :] += _colsum(dc * pltpu.roll(zext, k_taps - 1 - k, 0)[halo:])


LANES = 128


def _to_strided_view(dst_ref, chunk, scr, d, width):
    n = scr.shape[0] // d
    for c in range(width // LANES):
        scr[...] = chunk(c)
        for r in range(d):
            dst_ref[:, r * width + c * LANES:r * width + (c + 1) * LANES] = scr[pl.ds(r, n, stride=d), :].astype(dst_ref.dtype)


def _from_strided_view(src_ref, scr, d, width, c):
    n = scr.shape[0] // d
    for r in range(d):
        scr[pl.ds(r, n, stride=d), :] = src_ref[:, r * width + c * LANES:r * width + (c + 1) * LANES].astype(F32)
    return scr[...]


def _view_spec(tm, d, width):
    return bs((tm // d, d * width), lambda i: (i, 0))


def _prev_blk(i, per):
    return jnp.maximum(i * per - 1, 0)


def _next_blk(i, per, last):
    return jnp.minimum((i + 1) * per, last)


def norm_in_proj(x, g, win, after=None):
    t = x.shape[0]
    tm = min(512, t)
    ns = win.shape[2]

    def body(x_ref, g_ref, w_ref, h_ref, o_ref, q_ref, q4_ref, q16_ref, scr):
        h = _rms_fwd(x_ref[...], g_ref[...]).astype(BF16)
        h_ref[...] = h
        parts = []
        for s in range(N_SH):
            r = _nn(h, w_ref[s])
            o_ref[:, s * ns:(s + 1) * ns] = r
            if s == 1:
                parts.append(r[:, 3 * BW - ns:])
            if s == 2:
                parts.append(r[:, :6 * BW - 2 * ns])
        qf = jnp.concatenate(parts, axis=1)
        q_ref[...] = qf.astype(BF16)
        chunk = lambda c: qf[:, c * LANES:(c + 1) * LANES]
        _to_strided_view(q4_ref, chunk, scr, 4, 3 * BW)
        _to_strided_view(q16_ref, chunk, scr, 16, 3 * BW)

    row = lambda c: bs((tm, c), lambda i: (i, 0))
    return _call(
        body, "norm_in_proj", (t // tm,), [row(D_MODEL), bs((1, D_MODEL), lambda i: (0, 0)), _resident(win)],
        [row(D_MODEL), row(N_IN), row(3 * BW), _view_spec(tm, 4, 3 * BW), _view_spec(tm, 16, 3 * BW)],
        [_sds((t, D_MODEL), BF16), _sds((t, N_IN), F32), _sds((t, 3 * BW), BF16),
         _sds((t // 4, 4 * 3 * BW), BF16), _sds((t // 16, 16 * 3 * BW), BF16)],
        scratch=[pltpu.VMEM((tm, LANES), F32)], after=after)(x, g, win)


def merge_fwd(h, ys, wg, wbr, after=None):
    t = h.shape[0]
    tm = min(512, t)

    def body(h_ref, ya, yb, yc, yd, wg_ref, wb_ref, m_ref, g_ref, b_ref):
        hh = h_ref[...]
        for j in range(N_SH):
            cs = slice(j * BW, (j + 1) * BW)
            acc = None
            for k, y_ref in enumerate((ya, yb, yc, yd)):
                g = _sigmoid(_nn(hh, wg_ref[k, :, cs]))
                b = _nn(y_ref[...], wb_ref[j, k])
                g_ref[k, :, cs] = g.astype(BF16)
                b_ref[k, :, cs] = b.astype(BF16)
                acc = g * b if acc is None else acc + g * b
            m_ref[:, cs] = acc.astype(BF16)

    ysp = bs((tm, BW), lambda i: (i, 0))
    big = bs((N_BR, tm, D_MODEL), lambda i: (0, i, 0))
    return _call(
        body, "merge_fwd", (t // tm,),
        [bs((tm, D_MODEL), lambda i: (i, 0)), ysp, ysp, ysp, ysp, _resident(wg), _resident(wbr)],
        [bs((tm, D_MODEL), lambda i: (i, 0)), big, big],
        [_sds((t, D_MODEL), BF16), _sds((N_BR, t, D_MODEL), BF16), _sds((N_BR, t, D_MODEL), BF16)], after=after)(
            h, *ys, wg, wbr)


def ffn_in(a, wout, res, g, wfi, after=None):
    t = res.shape[0]
    tm = min(256, t)
    ns = wfi.shape[2]

    def body(a_ref, wo_ref, r_ref, g_ref, w_ref, x_ref, h_ref, f_ref, act_ref):
        xv = r_ref[...] + _nn(a_ref[...], wo_ref[...])
        x_ref[...] = xv
        h = _rms_fwd(xv, g_ref[...]).astype(BF16)
        h_ref[...] = h
        for j in range(2):
            cs = slice(j * ns, (j + 1) * ns)
            fg = _nn(h, w_ref[j])
            fu = _nn(h, w_ref[j + 2])
            f_ref[0, :, cs] = fg.astype(BF16)
            f_ref[1, :, cs] = fu.astype(BF16)
            act_ref[:, cs] = (fg * _sigmoid(fg) * fu).astype(BF16)

    row = lambda c: bs((tm, c), lambda i: (i, 0))
    return _call(
        body, "ffn_in", (t // tm,),
        [row(D_MODEL), _resident(wout), row(D_MODEL), bs((1, D_MODEL), lambda i: (0, 0)), _resident(wfi)],
        [row(D_MODEL), row(D_MODEL), bs((2, tm, FFN_H), lambda i: (0, i, 0)), row(FFN_H)],
        [_sds((t, D_MODEL), F32), _sds((t, D_MODEL), BF16), _sds((2, t, FFN_H), BF16), _sds((t, FFN_H), BF16)],
        after=after)(a, wout, res, g, wfi)


def ple_fwd(a, wfo, res, g, wpg, p_i, wpp, after=None):
    t = res.shape[0]
    tm = min(512, t)

    def body(a_ref, wo_ref, r_ref, g_ref, wg_ref, p_ref, wp_ref, x_ref, h_ref, gt_ref, pp_ref, o_ref):
        xv = r_ref[...] + _nn(a_ref[...], wo_ref[...])
        x_ref[...] = xv
        h = _rms_fwd(xv, g_ref[...]).astype(BF16)
        h_ref[...] = h
        gate = _sigmoid(_nn(h, wg_ref[...]))
        pb = p_ref[...].astype(BF16)
        pp = jnp.concatenate([_nn(pb, wp_ref[j]) for j in range(N_SH)], axis=1)
        gt_ref[...] = gate.astype(BF16)
        pp_ref[...] = pp.astype(BF16)
        o_ref[...] = xv + gate * pp

    row = bs((tm, D_MODEL), lambda i: (i, 0))
    return _call(
        body, "ple_fwd", (t // tm,),
        [bs((tm, FFN_H), lambda i: (i, 0)), _resident(wfo), row, bs((1, D_MODEL), lambda i: (0, 0)), _resident(wpg),
         bs((tm, BW), lambda i: (i, 0)), _resident(wpp)],
        [row, row, row, row, row],
        [_sds((t, D_MODEL), F32), _sds((t, D_MODEL), BF16), _sds((t, D_MODEL), BF16), _sds((t, D_MODEL), BF16),
         _sds((t, D_MODEL), F32)], after=after)(a, wfo, res, g, wpg, p_i, wpp)


def loss_head(x, g, tgt):
    t = x.shape[0]
    tm = min(512, t)

    def body(x_ref, g_ref, t_ref, l_ref, dx_ref, dg_ref):
        @pl.when(pl.program_id(0) == 0)
        def _():
            l_ref[...] = jnp.zeros_like(l_ref)
            dg_ref[...] = jnp.zeros_like(dg_ref)

        xv, gv = x_ref[...], g_ref[...]
        err = _rms_fwd(xv, gv) - t_ref[...]
        part = 0.5 * jnp.sum(jnp.mean(err * err, axis=-1, keepdims=True), axis=0, keepdims=True)
        l_ref[...] += jnp.broadcast_to(part, l_ref.shape)
        dx, dgr = _rms_bwd(err * (1.0 / D_MODEL), xv, gv)
        dx_ref[...] = dx
        dg_ref[...] += _colsum(dgr)

    row = bs((tm, D_MODEL), lambda i: (i, 0))
    vec = bs((1, D_MODEL), lambda i: (0, 0))
    return _call(body, "loss_head", (t // tm,), [row, vec, row],
                 [bs((1, 128), lambda i: (0, 0)), row, vec],
                 [_sds((1, 128), F32), _sds((t, D_MODEL), F32), _sds((1, D_MODEL), F32)])(x, g, tgt)


def tn_matmul(name, a, b, grid, a_spec, b_spec, out_spec, out_shape, split=0, split_cols=0, into=None, after=None):
    last = len(grid) - 1

    def body(a_ref, b_ref, *rest):
        o_ref = rest[-1]

        @pl.when(pl.program_id(last) == 0)
        def _():
            o_ref[...] = jnp.zeros_like(o_ref)

        res = _tn(a_ref[...].astype(BF16), b_ref[...].astype(BF16))
        if split_cols:
            cols = res.shape[1] // split_cols
            for s in range(split_cols):
                o_ref[s] += res[:, s * cols:(s + 1) * cols]
        elif split:
            rows = res.shape[0] // split
            for s in range(split):
                o_ref[s] += res[s * rows:(s + 1) * rows]
        else:
            o_ref[...] += res

    if into is None:
        return _call(body, name, grid, [a_spec, b_spec], out_spec, out_shape, after=after)(a, b)
    return _call(body, name, grid, [a_spec, b_spec, ANY], out_spec, out_shape, aliases={2: 0}, after=after)(a, b, into)


def _resident(w):
    zeros = (0,) * w.ndim
    return bs(w.shape, lambda i: zeros, pipeline_mode=pl.Buffered(1))


def norm_bwd(name, sources, dx_in, x, g):
    t = x.shape[0]
    tm = min(512, t)
    n_src = len(sources)

    def body(*refs):
        dxi_ref, x_ref, g_ref, dx_ref, dg_ref = refs[2 * n_src:]

        @pl.when(pl.program_id(0) == 0)
        def _():
            dg_ref[...] = jnp.zeros_like(dg_ref)

        dh = None
        for si in range(n_src):
            for av, wv in sources[si][3](refs[2 * si], refs[2 * si + 1]):
                part = _nt(av, wv)
                dh = part if dh is None else dh + part
        dx, dgr = _rms_bwd(dh, x_ref[...], g_ref[...])
        dx_ref[...] = dxi_ref[...] + dx
        dg_ref[...] += _colsum(dgr)

    in_specs, args = [], []
    for a, a_spec, w, _ in sources:
        in_specs += [a_spec(tm), _resident(w)]
        args += [a, w]
    row = bs((tm, D_MODEL), lambda i: (i, 0))
    vec = bs((1, D_MODEL), lambda i: (0, 0))
    return _call(body, name, (t // tm,), in_specs + [row, row, vec], [row, vec],
                 [_sds((t, D_MODEL), F32), _sds((1, D_MODEL), F32)])(*args, dx_in, x, g)


def ple_bwd(dx, gate, pp, wpg, x, g, h3, p_i, ga_shape, gb_shape, after=None):
    t = dx.shape[0]
    tm = min(512, t)

    def body(dx_ref, gt_ref, p_ref, w_ref, x_ref, g_ref, h_ref, pi_ref, o_ref, dg_ref, ga_ref, gb_ref):
        @pl.when(pl.program_id(0) == 0)
        def _():
            dg_ref[...] = jnp.zeros_like(dg_ref)
            ga_ref[...] = jnp.zeros_like(ga_ref)
            gb_ref[...] = jnp.zeros_like(gb_ref)

        d = dx_ref[...]
        gt = gt_ref[...].astype(F32)
        dpre = (d * p_ref[...].astype(F32) * gt * (1.0 - gt)).astype(BF16)
        dpp = (d * gt).astype(BF16)
        dxn, dgr = _rms_bwd(_nt(dpre, w_ref[...]), x_ref[...], g_ref[...])
        o_ref[...] = d + dxn
        dg_ref[...] += _colsum(dgr)
        dwg = _tn(h_ref[...], dpre)
        dwp = _tn(pi_ref[...].astype(BF16), dpp)
        for s in range(N_SH):
            ga_ref[s] += dwg[s * BW:(s + 1) * BW]
            gb_ref[s] += dwp[:, s * BW:(s + 1) * BW]

    row = bs((tm, D_MODEL), lambda i: (i, 0))
    vec = bs((1, D_MODEL), lambda i: (0, 0))
    return _call(body, "ple_bwd", (t // tm,),
                 [row, row, row, _resident(wpg), row, vec, row, bs((tm, BW), lambda i: (i, 0))],
                 [row, vec, bs((N_SH, BW, D_MODEL), lambda i: (0, 5, 0)), bs((N_SH, BW, BW), lambda i: (0, 4, 0))],
                 [_sds((t, D_MODEL), F32), _sds((1, D_MODEL), F32), ga_shape, gb_shape],
                 after=after)(dx, gate, pp, wpg, x, g, h3, p_i)


def ffn_bwd(dx, wfo, fgu, wfi, x, g, after=None):
    t = dx.shape[0]
    tm = min(256, t)
    ns = FFN_H // 2

    def body(dx_ref, wo_ref, f_ref, wi_ref, x_ref, g_ref, df_ref, o_ref, dg_ref):
        @pl.when(pl.program_id(0) == 0)
        def _():
            dg_ref[...] = jnp.zeros_like(dg_ref)

        d = dx_ref[...]
        dxb = d.astype(BF16)
        dh = None
        for j in range(2):
            cs = slice(j * ns, (j + 1) * ns)
            dact = _nt(dxb, wo_ref[cs, :])
            fg = f_ref[0, :, cs].astype(F32)
            fu = f_ref[1, :, cs].astype(F32)
            s = _sigmoid(fg)
            dfg = (dact * fu * (s * (1.0 + fg * (1.0 - s)))).astype(BF16)
            dfu = (dact * fg * s).astype(BF16)
            df_ref[0, :, cs] = dfg
            df_ref[1, :, cs] = dfu
            part = _nt(dfg, wi_ref[j]) + _nt(dfu, wi_ref[j + 2])
            dh = part if dh is None else dh + part
        dxn, dgr = _rms_bwd(dh, x_ref[...], g_ref[...])
        o_ref[...] = d + dxn
        dg_ref[...] += _colsum(dgr)

    blk = bs((2, tm, FFN_H), lambda i: (0, i, 0))
    row = bs((tm, D_MODEL), lambda i: (i, 0))
    vec = bs((1, D_MODEL), lambda i: (0, 0))
    return _call(body, "ffn_bwd", (t // tm,), [row, _resident(wfo), blk, _resident(wfi), row, vec], [blk, row, vec],
                 [_sds((2, t, FFN_H), BF16), _sds((t, D_MODEL), F32), _sds((1, D_MODEL), F32)],
                 after=after)(dx, wfo, fgu, wfi, x, g)


def merge_bwd(dx, wout, gates, ybr, wbr, ys, gb, merged, ga):
    t = dx.shape[0]
    tm = min(256, t)

    def body(dx_ref, w_ref, g_ref, b_ref, wb_ref, ya, yb, yc, yd, _, m_ref, __, dpre_ref, dy_ref, gb_ref, ga_ref):
        @pl.when(pl.program_id(0) == 0)
        def _():
            gb_ref[...] = jnp.zeros_like(gb_ref)
            ga_ref[...] = jnp.zeros_like(ga_ref)

        dxb = dx_ref[...].astype(BF16)
        dwo = _tn(m_ref[...], dxb)
        for s in range(N_SH):
            ga_ref[s] += dwo[s * BW:(s + 1) * BW]
        dm = _nt(dxb, w_ref[...])
        for k, y_ref in enumerate((ya, yb, yc, yd)):
            g = g_ref[k].astype(F32)
            dpre_ref[k] = (dm * b_ref[k].astype(F32) * g * (1.0 - g)).astype(BF16)
            dyb = (dm * g).astype(BF16)
            acc = None
            for s in range(N_SH):
                part = _nt(dyb[:, s * BW:(s + 1) * BW], wb_ref[s, k])
                acc = part if acc is None else acc + part
            dy_ref[k] = acc
            dwb = _tn(y_ref[...], dyb)
            for s in range(N_SH):
                gb_ref[s, k * BW:(k + 1) * BW, :] += dwb[:, s * BW:(s + 1) * BW]

    blk = bs((N_BR, tm, D_MODEL), lambda i: (0, i, 0))
    ysp = bs((tm, BW), lambda i: (i, 0))
    return _call(body, "merge_bwd", (t // tm,),
                 [bs((tm, D_MODEL), lambda i: (i, 0)), _resident(wout), blk, blk, _resident(wbr), ysp, ysp, ysp, ysp, ANY,
                  bs((tm, D_MODEL), lambda i: (i, 0)), ANY],
                 [blk, bs((N_BR, tm, BW), lambda i: (0, i, 0)), bs((N_SH, N_BR * BW, BW), lambda i: (0, 0, 0)),
                  bs((N_SH, BW, D_MODEL), lambda i: (0, 4, 0))],
                 [_sds((N_BR, t, D_MODEL), BF16), _sds((N_BR, t, BW), F32), _sds(gb.shape, F32), _sds(ga.shape, F32)],
                 aliases={9: 2, 11: 3})(dx, wout, gates, ybr, wbr, *ys, gb, merged, ga)


def conva_fwd(proj, wa):
    t = proj.shape[0]
    tm, halo = min(512, t), 8
    per = tm // halo

    def body(b_ref, c_ref, x_ref, ch_ref, xh_ref, w_ref, y_ref):
        zh = jnp.where(pl.program_id(0) > 0, ch_ref[...] * xh_ref[...], 0.0)
        zext = jnp.concatenate([zh, c_ref[...] * x_ref[...]], axis=0)
        y_ref[...] = (b_ref[...] * _causal_conv(zext, w_ref, CONVA_K, halo)).astype(BF16)

    col = lambda c: bs((tm, BW), lambda i: (i, c))
    hal = lambda c: bs((halo, BW), lambda i: (_prev_blk(i, per), c))
    return _call(body, "conva_fwd", (t // tm,),
                 [col(0), col(1), col(2), hal(1), hal(2), bs((CONVA_K, BW), lambda i: (0, 0))],
                 bs((tm, BW), lambda i: (i, 0)), _sds((t, BW), BF16))(proj, proj, proj, proj, proj, wa)


def conva_bwd(proj, dys, wa, after=None):
    t = proj.shape[0]
    tm, halo = min(512, t), 8
    per = tm // halo
    last = t // halo - 1
    nt = t // tm

    def body(b_ref, c_ref, x_ref, ch_ref, xh_ref, bn_ref, dy_ref, dyn_ref, w_ref, o_ref, dw_ref):
        i = pl.program_id(0)

        @pl.when(i == 0)
        def _():
            dw_ref[...] = jnp.zeros_like(dw_ref)

        zh = jnp.where(i > 0, ch_ref[...] * xh_ref[...], 0.0)
        cv, xv = c_ref[...], x_ref[...]
        zext = jnp.concatenate([zh, cv * xv], axis=0)
        dy = dy_ref[...]
        dconv = dy * b_ref[...]
        dcn = jnp.where(i < nt - 1, dyn_ref[...] * bn_ref[...], 0.0)
        dz = _anti_conv(jnp.concatenate([dconv, dcn], axis=0), w_ref, CONVA_K, tm)
        o_ref[:, :BW] = (dy * _causal_conv(zext, w_ref, CONVA_K, halo)).astype(BF16)
        o_ref[:, BW:2 * BW] = (dz * xv).astype(BF16)
        o_ref[:, 2 * BW:] = (dz * cv).astype(BF16)
        _conv_wgrad(dw_ref, dconv, zext, CONVA_K, halo)

    col = lambda c: bs((tm, BW), lambda i: (i, c))
    hal = lambda c: bs((halo, BW), lambda i: (_prev_blk(i, per), c))
    nxt = bs((halo, BW), lambda i: (_next_blk(i, per, last), 0))
    wsp = bs((CONVA_K, BW), lambda i: (0, 0))
    outs = _call(body, "conva_bwd", (t // tm,),
                 [col(0), col(1), col(2), hal(1), hal(2), nxt,
                  bs((None, tm, BW), lambda i: (0, i, 0)), bs((None, halo, BW), lambda i: (0, _next_blk(i, per, last), 0)), wsp],
                 [bs((tm, 3 * BW), lambda i: (i, 0)), wsp],
                 [_sds((t, N_IN), BF16), _sds((CONVA_K, BW), F32)], after=after)(
                     proj, proj, proj, proj, proj, proj, dys, dys, wa)
    return outs[0], outs[1]


def _head_masks():
    lane = lax.broadcasted_iota(jnp.int32, (1, BW), 1)
    return [(lane >= h * HEAD_D) & (lane < (h + 1) * HEAD_D) for h in range(HEADS)]


def _band_masks():
    qi = lax.broadcasted_iota(jnp.int32, (BLK, BLK), 0)
    ki = lax.broadcasted_iota(jnp.int32, (BLK, BLK), 1)
    return ki >= qi, ki <= qi


def attn_fwd_group(pv, d):
    rows = pv.shape[0]
    qb = min(512, rows)
    nb = qb // BLK
    scale = HEAD_D ** -0.5

    def body(q_ref, k_ref, v_ref, kh_ref, vh_ref, o_ref):
        n = pl.program_id(1)
        hm = _head_masks()
        m_prev, m_cur = _band_masks()
        for b in range(nb):
            rs = slice(b * BLK, (b + 1) * BLK)
            q = q_ref[rs, :]
            if b == 0:
                kp, vp = kh_ref[...], vh_ref[...]
                mp = m_prev & (n > 0)
            else:
                ps = slice((b - 1) * BLK, b * BLK)
                kp, vp = k_ref[ps, :], v_ref[ps, :]
                mp = m_prev
            qs = jnp.concatenate([jnp.where(hm[h], q, 0.0).astype(BF16) for h in range(HEADS)], axis=0)
            kcat = jnp.concatenate([kp, k_ref[rs, :]], axis=0)
            vcat = jnp.concatenate([vp, v_ref[rs, :]], axis=0)
            band = jnp.concatenate([mp, m_cur], axis=1)
            s = jnp.where(jnp.concatenate([band] * HEADS, axis=0), _nt(qs, kcat) * scale, NEG)
            m = jnp.max(s, axis=-1, keepdims=True)
            e = jnp.exp(s - m)
            l = jnp.sum(e, axis=-1, keepdims=True)
            of = _nn(e.astype(BF16), vcat) / l
            lse = m + jnp.log(l)
            o_acc = jnp.zeros((BLK, BW), F32)
            l_acc = jnp.zeros((BLK, BW), F32)
            for h in range(HEADS):
                hs = slice(h * BLK, (h + 1) * BLK)
                o_acc = jnp.where(hm[h], of[hs, :], o_acc)
                l_acc = jnp.where(hm[h], lse[hs, :], l_acc)
            o_ref[rs, :BW] = o_acc
            o_ref[rs, BW:] = l_acc

    per = qb // BLK
    main = lambda c: bs((qb, BW), lambda r, n: (n, r * 3 + c))
    hal = lambda c: bs((BLK, BW), lambda r, n: (_prev_blk(n, per), r * 3 + c))
    return _call(body, f"attn_fwd_d{d}", (d, rows // qb), [main(0), main(1), main(2), hal(1), hal(2)],
                 bs((qb, 2 * BW), lambda r, n: (n, r)), _sds((rows, d * 2 * BW), F32))(pv, pv, pv, pv, pv)


def attn_merge(ols):
    t = ols[0].shape[0]
    tm = min(512, t)
    width = 2 * BW

    def lse3(a, b, c):
        m = jnp.maximum(jnp.maximum(a, b), c)
        return m + jnp.log(jnp.exp(a - m) + jnp.exp(b - m) + jnp.exp(c - m))

    def body(g0, g1, g2, y_ref, o_ref, l_ref, scr, nat1, nat2):
        for src, nat, d in ((g1, nat1, DILATIONS[1]), (g2, nat2, DILATIONS[2])):
            for c in range(width // LANES):
                nat[:, c * LANES:(c + 1) * LANES] = _from_strided_view(src, scr, d, width, c)
        gs = [g0[...], nat1[...], nat2[...]]
        ls = [g[:, BW:] for g in gs]
        tot = lse3(*ls)
        o = (jnp.exp(ls[0] - tot) * gs[0][:, :BW] + jnp.exp(ls[1] - tot) * gs[1][:, :BW]
             + jnp.exp(ls[2] - tot) * gs[2][:, :BW])
        y_ref[...] = o.astype(BF16)
        o_ref[...] = o
        l_ref[...] = tot

    n = bs((tm, BW), lambda i: (i, 0))
    return _call(body, "attn_merge", (t // tm,),
                 [_view_spec(tm, 1, width), _view_spec(tm, DILATIONS[1], width), _view_spec(tm, DILATIONS[2], width)],
                 [n, n, n], [_sds((t, BW), BF16), _sds((t, BW), F32), _sds((t, BW), F32)],
                 scratch=[pltpu.VMEM((tm, LANES), F32), pltpu.VMEM((tm, width), F32), pltpu.VMEM((tm, width), F32)])(*ols)


def attn_delta(dys, o, lse):
    t = o.shape[0]
    tm = min(512, t)

    def body(d_ref, o_ref, l_ref, ld1, ld4, ld16, dy4, dy16, scr):
        hm = _head_masks()
        dy = d_ref[...]
        prod = dy * o_ref[...]
        delta = jnp.zeros_like(prod)
        for h in range(HEADS):
            delta = jnp.where(hm[h], jnp.sum(jnp.where(hm[h], prod, 0.0), axis=-1, keepdims=True), delta)
        ld = jnp.concatenate([l_ref[...], delta], axis=1)
        ld1[...] = ld
        for d, ld_v, dy_v in ((DILATIONS[1], ld4, dy4), (DILATIONS[2], ld16, dy16)):
            _to_strided_view(ld_v, lambda c: ld[:, c * LANES:(c + 1) * LANES], scr, d, 2 * BW)
            _to_strided_view(dy_v, lambda c: dy[:, c * LANES:(c + 1) * LANES], scr, d, BW)

    n = bs((tm, BW), lambda i: (i, 0))
    d4, d16 = DILATIONS[1], DILATIONS[2]
    outs = _call(body, "attn_delta", (t // tm,), [bs((None, tm, BW), lambda i: (1, i, 0)), n, n],
                 [_view_spec(tm, 1, 2 * BW), _view_spec(tm, d4, 2 * BW), _view_spec(tm, d16, 2 * BW),
                  _view_spec(tm, d4, BW), _view_spec(tm, d16, BW)],
                 [_sds((t, 2 * BW), F32), _sds((t // d4, d4 * 2 * BW), F32), _sds((t // d16, d16 * 2 * BW), F32),
                  _sds((t // d4, d4 * BW), F32), _sds((t // d16, d16 * BW), F32)],
                 scratch=[pltpu.VMEM((tm, LANES), F32)])(dys, o, lse)
    return outs[:3], outs[3:]


def attn_bwd_group(pv, dov, ldv, d):
    rows = pv.shape[0]
    qb = min(512, rows)
    nb = qb // BLK
    nsteps = rows // qb
    scale = HEAD_D ** -0.5

    def body(q_ref, qn_ref, k_ref, kh_ref, v_ref, vh_ref, do_ref, don_ref, ld_ref, ldn_ref, o_ref):
        n = pl.program_id(1)
        hm = _head_masks()
        m_prev, m_cur = _band_masks()
        has_prev, has_next = n > 0, n < nsteps - 1
        dq = [None] * nb
        dk = [jnp.zeros((BLK, BW), F32) for _ in range(nb)]
        dvv = [jnp.zeros((BLK, BW), F32) for _ in range(nb)]
        for qi in range(nb + 1):
            rs = slice(qi * BLK, (qi + 1) * BLK)
            ps = slice((qi - 1) * BLK, qi * BLK)
            if qi < nb:
                q, do, ldq = q_ref[rs, :], do_ref[rs, :], ld_ref[rs, :]
            else:
                q, do, ldq = qn_ref[...], don_ref[...], ldn_ref[...]
            kp, vp = (kh_ref[...], vh_ref[...]) if qi == 0 else (k_ref[ps, :], v_ref[ps, :])
            kc, vc = (k_ref[rs, :], v_ref[rs, :]) if qi < nb else (kp, vp)
            mp = m_prev & has_prev if qi == 0 else (m_prev & has_next if qi == nb else m_prev)
            mc = m_cur if qi < nb else jnp.zeros_like(m_cur)
            band = jnp.concatenate([jnp.concatenate([mp, mc], axis=1)] * HEADS, axis=0)
            qs = jnp.concatenate([jnp.where(hm[h], q, 0.0).astype(BF16) for h in range(HEADS)], axis=0)
            dos = jnp.concatenate([jnp.where(hm[h], do, 0.0).astype(BF16) for h in range(HEADS)], axis=0)
            kcat = jnp.concatenate([kp, kc], axis=0)
            vcat = jnp.concatenate([vp, vc], axis=0)
            col = lambda v, h: jnp.broadcast_to(jnp.max(jnp.where(hm[h], v, NEG), axis=-1, keepdims=True), (BLK, 2 * BLK))
            lcols = jnp.concatenate([col(ldq[:, :BW], h) for h in range(HEADS)], axis=0)
            dcols = jnp.concatenate([col(ldq[:, BW:], h) for h in range(HEADS)], axis=0)
            p = jnp.where(band, jnp.exp(_nt(qs, kcat) * scale - lcols), 0.0)
            ds = (p * (_nt(dos, vcat) - dcols) * scale).astype(BF16)
            if qi < nb:
                dqf = _nn(ds, kcat)
                acc_q = jnp.zeros((BLK, BW), F32)
                for h in range(HEADS):
                    acc_q = jnp.where(hm[h], dqf[h * BLK:(h + 1) * BLK, :], acc_q)
                dq[qi] = acc_q
            dkc = _tn(ds, qs)
            dvc = _tn(p.astype(BF16), dos)
            if qi >= 1:
                dk[qi - 1] = dk[qi - 1] + dkc[:BLK]
                dvv[qi - 1] = dvv[qi - 1] + dvc[:BLK]
            if qi < nb:
                dk[qi] = dk[qi] + dkc[BLK:]
                dvv[qi] = dvv[qi] + dvc[BLK:]
        for b in range(nb):
            rs = slice(b * BLK, (b + 1) * BLK)
            for c, val in enumerate((dq[b], dk[b], dvv[b])):
                cs = slice(c * BW, (c + 1) * BW)
                o_ref[rs, cs] = val

    per = qb // BLK
    last = rows // BLK - 1
    main = lambda c: bs((qb, BW), lambda r, n: (n, r * 3 + c))
    prv = lambda c: bs((BLK, BW), lambda r, n: (_prev_blk(n, per), r * 3 + c))
    nxt = lambda c: bs((BLK, BW), lambda r, n: (_next_blk(n, per, last), r * 3 + c))
    accs = bs((qb, 3 * BW), lambda r, n: (n, r))
    in_specs = [main(0), nxt(0), main(1), prv(1), main(2), prv(2),
                bs((qb, BW), lambda r, n: (n, r)), bs((BLK, BW), lambda r, n: (_next_blk(n, per, last), r)),
                bs((qb, 2 * BW), lambda r, n: (n, r)), bs((BLK, 2 * BW), lambda r, n: (_next_blk(n, per, last), r))]
    args = [pv, pv, pv, pv, pv, pv, dov, dov, ldv, ldv]
    return _call(body, f"attn_bwd_d{d}", (d, nsteps), in_specs, accs, _sds((rows, d * 3 * BW), F32))(*args)


def attn_bwd_finish(parts, into):
    t = parts[0].shape[0]
    tm = min(512, t)
    width = 3 * BW

    def body(g0, g1, g2, _, o_ref, scr):
        for c in range(width // LANES):
            cs = slice(c * LANES, (c + 1) * LANES)
            acc = g0[:, cs]
            acc = acc + _from_strided_view(g1, scr, DILATIONS[1], width, c)
            acc = acc + _from_strided_view(g2, scr, DILATIONS[2], width, c)
            o_ref[:, cs] = acc.astype(BF16)

    return _call(body, "attn_bwd_finish", (t // tm,),
                 [_view_spec(tm, 1, width), _view_spec(tm, DILATIONS[1], width), _view_spec(tm, DILATIONS[2], width), ANY],
                 bs((tm, width), lambda i: (i, 1)), _sds(into.shape, BF16),
                 scratch=[pltpu.VMEM((tm, LANES), F32)], aliases={3: 0})(*parts, into)


def _group_masks():
    lane = lax.broadcasted_iota(jnp.int32, (1, BW), 1)
    return [(lane >= g * HEAD_D) & (lane < (g + 1) * HEAD_D) for g in range(4)]


def sgu_fwd(proj, ln_g, ln_b, w_tril, b_full):
    t = proj.shape[0]
    tm = min(512, t)

    def body(u_ref, v_ref, g_ref, b_ref, w_ref, bf_ref, y_ref):
        gm = _group_masks()
        xhat, _ = _ln_hat(v_ref[...])
        vb = (xhat * g_ref[...] + b_ref[...]).astype(BF16)
        for c in range(tm // BLK):
            rs = slice(c * BLK, (c + 1) * BLK)
            vc = vb[rs, :]
            mixed = bf_ref[...]
            for g in range(4):
                mixed = mixed + jnp.where(gm[g], _nn(w_ref[g], vc), 0.0)
            y_ref[rs, :] = (u_ref[rs, :] * mixed).astype(BF16)

    vec = bs((1, BW), lambda i: (0, 0))
    return _call(body, "sgu_fwd", (t // tm,),
                 [bs((tm, BW), lambda i: (i, 6)), bs((tm, BW), lambda i: (i, 7)), vec, vec,
                  bs((4, BLK, BLK), lambda i: (0, 0, 0)), bs((BLK, BW), lambda i: (0, 0))],
                 bs((tm, BW), lambda i: (i, 0)), _sds((t, BW), BF16))(proj, proj, ln_g, ln_b, w_tril, b_full)


def sgu_bwd(proj, dys, ln_g, ln_b, w_tril, b_full, into):
    t = proj.shape[0]
    tm = min(512, t)

    def body(u_ref, v_ref, dy_ref, g_ref, b_ref, w_ref, bf_ref, _, o_ref, dw_ref, dbf_ref, dg_ref, db_ref, dvl_ref):
        @pl.when(pl.program_id(0) == 0)
        def _():
            dw_ref[...] = jnp.zeros_like(dw_ref)
            dbf_ref[...] = jnp.zeros_like(dbf_ref)
            dg_ref[...] = jnp.zeros_like(dg_ref)
            db_ref[...] = jnp.zeros_like(db_ref)

        gm = _group_masks()
        xhat, r = _ln_hat(v_ref[...])
        gv = g_ref[...]
        vb = (xhat * gv + b_ref[...]).astype(BF16)
        for c in range(tm // BLK):
            rs = slice(c * BLK, (c + 1) * BLK)
            vc = vb[rs, :]
            dy = dy_ref[rs, :]
            mixed = bf_ref[...]
            for g in range(4):
                mixed = mixed + jnp.where(gm[g], _nn(w_ref[g], vc), 0.0)
            o_ref[rs, :BW] = (dy * mixed).astype(BF16)
            dm = dy * u_ref[rs, :]
            dbf_ref[...] += dm
            dvl = jnp.zeros((BLK, BW), F32)
            for g in range(4):
                dmg = jnp.where(gm[g], dm, 0.0).astype(BF16)
                dw_ref[g] += _nt(dmg, vc)
                dvl = dvl + _tn(w_ref[g], dmg)
            dvl_ref[rs, :] = dvl
        dvl = dvl_ref[...]
        o_ref[:, BW:] = _ln_bwd(dvl, xhat, r, gv).astype(BF16)
        dg_ref[...] += _colsum(dvl * xhat)
        db_ref[...] += _colsum(dvl)

    vec = bs((1, BW), lambda i: (0, 0))
    row = bs((tm, BW), lambda i: (i, 0))
    wsp = bs((4, BLK, BLK), lambda i: (0, 0, 0))
    bfs = bs((BLK, BW), lambda i: (0, 0))
    return _call(body, "sgu_bwd", (t // tm,),
                 [bs((tm, BW), lambda i: (i, 6)), bs((tm, BW), lambda i: (i, 7)), bs((None, tm, BW), lambda i: (2, i, 0)),
                  vec, vec, wsp, bfs, ANY],
                 [bs((tm, 2 * BW), lambda i: (i, 3)), wsp, bfs, vec, vec],
                 [_sds(into.shape, BF16), _sds((4, BLK, BLK), F32), _sds((BLK, BW), F32),
                  _sds((1, BW), F32), _sds((1, BW), F32)],
                 scratch=[pltpu.VMEM((tm, BW), F32)], aliases={7: 0})(proj, proj, dys, ln_g, ln_b, w_tril, b_full, into)


CONF_HALO = 32


def conf_fwd(proj, dw, ln_g, ln_b, after=None):
    t = proj.shape[0]
    tm, halo = min(512, t), CONF_HALO
    per = tm // halo

    def body(v_ref, gt_ref, vh_ref, gh_ref, w_ref, g_ref, b_ref, y_ref, z_ref):
        yh = jnp.where(pl.program_id(0) > 0, vh_ref[...] * _sigmoid(gh_ref[...]), 0.0)
        yext = jnp.concatenate([yh, v_ref[...] * _sigmoid(gt_ref[...])], axis=0)
        z = _causal_conv(yext, w_ref, CONF_K, halo)
        z_ref[...] = z
        xhat, _ = _ln_hat(z)
        ln = xhat * g_ref[...] + b_ref[...]
        y_ref[...] = (ln * _sigmoid(ln)).astype(BF16)

    vec = bs((1, BW), lambda i: (0, 0))
    col = lambda c: bs((tm, BW), lambda i: (i, c))
    hal = lambda c: bs((halo, BW), lambda i: (_prev_blk(i, per), c))
    row = bs((tm, BW), lambda i: (i, 0))
    return _call(body, "conf_fwd", (t // tm,),
                 [col(8), col(9), hal(8), hal(9), bs((CONF_K, BW), lambda i: (0, 0)), vec, vec],
                 [row, row], [_sds((t, BW), BF16), _sds((t, BW), F32)], after=after)(
                     proj, proj, proj, proj, dw, ln_g, ln_b)


def conf_bwd_ln(z, dys, ln_g, ln_b):
    t = z.shape[0]
    tm = min(1024, t)

    def body(z_ref, dy_ref, g_ref, b_ref, dz_ref, dg_ref, db_ref):
        @pl.when(pl.program_id(0) == 0)
        def _():
            dg_ref[...] = jnp.zeros_like(dg_ref)
            db_ref[...] = jnp.zeros_like(db_ref)

        gv = g_ref[...]
        xhat, r = _ln_hat(z_ref[...])
        ln = xhat * gv + b_ref[...]
        s = _sigmoid(ln)
        dln = dy_ref[...] * (s * (1.0 + ln * (1.0 - s)))
        dz_ref[...] = _ln_bwd(dln, xhat, r, gv)
        dg_ref[...] += _colsum(dln * xhat)
        db_ref[...] += _colsum(dln)

    vec = bs((1, BW), lambda i: (0, 0))
    row = bs((tm, BW), lambda i: (i, 0))
    return _call(body, "conf_bwd_ln", (t // tm,), [row, bs((None, tm, BW), lambda i: (3, i, 0)), vec, vec],
                 [row, vec, vec], [_sds((t, BW), F32), _sds((1, BW), F32), _sds((1, BW), F32)])(z, dys, ln_g, ln_b)


def conf_bwd_conv(proj, dz, dw, into):
    t = proj.shape[0]
    tm, halo = min(512, t), CONF_HALO
    per = tm // halo
    last = t // halo - 1
    nt = t // tm

    def body(v_ref, gt_ref, vh_ref, gh_ref, dz_ref, dzn_ref, w_ref, _, o_ref, dw_ref):
        i = pl.program_id(0)

        @pl.when(i == 0)
        def _():
            dw_ref[...] = jnp.zeros_like(dw_ref)

        val = v_ref[...]
        sg = _sigmoid(gt_ref[...])
        yh = jnp.where(i > 0, vh_ref[...] * _sigmoid(gh_ref[...]), 0.0)
        yext = jnp.concatenate([yh, val * sg], axis=0)
        dz = dz_ref[...]
        dzn = jnp.where(i < nt - 1, dzn_ref[...], 0.0)
        dy0 = _anti_conv(jnp.concatenate([dz, dzn], axis=0), w_ref, CONF_K, tm)
        o_ref[:, :BW] = (dy0 * sg).astype(BF16)
        o_ref[:, BW:] = (dy0 * val * sg * (1.0 - sg)).astype(BF16)
        _conv_wgrad(dw_ref, dz, yext, CONF_K, halo)

    col = lambda c: bs((tm, BW), lambda i: (i, c))
    hal = lambda c: bs((halo, BW), lambda i: (_prev_blk(i, per), c))
    row = bs((tm, BW), lambda i: (i, 0))
    wsp = bs((CONF_K, BW), lambda i: (0, 0))
    return _call(body, "conf_bwd_conv", (t // tm,),
                 [col(8), col(9), hal(8), hal(9), row, bs((halo, BW), lambda i: (_next_blk(i, per, last), 0)), wsp, ANY],
                 [bs((tm, 2 * BW), lambda i: (i, 4)), wsp], [_sds(into.shape, BF16), _sds((CONF_K, BW), F32)],
                 aliases={7: 0})(proj, proj, proj, proj, dz, dz, dw, into)


def _place():
    return lax.axis_index("x"), lax.axis_index("y"), lax.axis_index("c")


HBM_SPEC = pl.BlockSpec(memory_space=pltpu.HBM)
SEM_SPEC = pl.BlockSpec(memory_space=pltpu.SEMAPHORE)
EFFECT = pltpu.SideEffectType.DATAFLOW_SIDE_EFFECTING


class SplitExchange:
    def __init__(self, name, bufs, plan, n_copies):
        self.name, self.bufs, self.plan, self.n = name, list(bufs), plan, n_copies

    def start(self, after):
        nb, n, plan = len(self.bufs), self.n, self.plan

        def body(*refs):
            send, recv, token = refs[nb + 1], refs[nb + 2], refs[-1]
            for k, (src, dst, _, dev) in enumerate(plan(refs[:nb])):
                pltpu.make_async_remote_copy(src_ref=src, dst_ref=dst, send_sem=send.at[k], recv_sem=recv.at[k],
                                             device_id=dev, device_id_type=MESH).start()
            token[...] = jnp.zeros_like(token)

        outs = pl.pallas_call(
            body, name=self.name + "_start",
            out_shape=(pltpu.SemaphoreType.DMA((n,)), pltpu.SemaphoreType.DMA((n,)),
                       *[pltpu.HBM(b.shape, b.dtype) for b in self.bufs], _sds((8, 128), F32)),
            in_specs=[HBM_SPEC] * nb + [ANY],
            out_specs=(SEM_SPEC, SEM_SPEC, *[HBM_SPEC] * nb, pl.BlockSpec(memory_space=pltpu.VMEM)),
            input_output_aliases={i: 2 + i for i in range(nb)},
            compiler_params=pltpu.CompilerParams(has_side_effects=EFFECT))(
                *[pltpu.with_memory_space_constraint(b, pltpu.HBM) for b in self.bufs], after)
        self.send, self.recv, self.bufs = outs[0], outs[1], list(outs[2:2 + nb])
        return outs[-1]

    def wait(self, after):
        nb, plan = len(self.bufs), self.plan
        after = list(after) if isinstance(after, (list, tuple)) else [after]

        def body(*refs):
            send, recv = refs[nb], refs[nb + 1]
            for k, (src, _, land, dev) in enumerate(plan(refs[:nb])):
                cp = pltpu.make_async_remote_copy(src_ref=src, dst_ref=land, send_sem=send.at[k], recv_sem=recv.at[k],
                                                  device_id=dev, device_id_type=MESH)
                cp.wait_send()
                cp.wait_recv()

        outs = pl.pallas_call(
            body, name=self.name + "_wait", out_shape=tuple(pltpu.HBM(b.shape, b.dtype) for b in self.bufs),
            in_specs=[HBM_SPEC] * nb + [SEM_SPEC, SEM_SPEC] + [ANY] * len(after), out_specs=[HBM_SPEC] * nb,
            input_output_aliases={i: i for i in range(nb)},
            compiler_params=pltpu.CompilerParams(has_side_effects=EFFECT))(*self.bufs, self.send, self.recv, *after)
        return list(outs)


def _chips_of(x, y):
    return [(1 - x, y), (x, 1 - y), (1 - x, 1 - y)]


def allgather_ici_plan(shapes):
    def plan(refs):
        x, y, c = _place()
        out = []
        for a, ref in enumerate(refs):
            hl = shapes[a][1] // 2
            half = pl.ds(c * hl, hl)
            for cx, cy in _chips_of(x, y):
                mine = ref.at[2 * x + y, half]
                out.append((mine, mine, ref.at[2 * cx + cy, half], (cx, cy, c)))
        return out
    return plan


def allgather_d2d_plan(shapes):
    def plan(refs):
        x, y, c = _place()
        out = []
        for a, ref in enumerate(refs):
            hl = shapes[a][1] // 2
            for cx, cy in _chips_of(x, y):
                got = ref.at[2 * cx + cy, pl.ds(c * hl, hl)]
                out.append((got, got, ref.at[2 * cx + cy, pl.ds((1 - c) * hl, hl)], (x, y, 1 - c)))
        return out
    return plan


def gather8(v):
    rows, cols = v.shape

    def body(v_ref, land, send, recv, lsem):
        x, y, c = _place()
        me = 4 * x + 2 * y + c
        mine = pltpu.make_async_copy(v_ref, land.at[me], lsem)
        mine.start()
        sent = []
        for j in range(1, 8):
            fx, fy, fc = (j >> 2) & 1, (j >> 1) & 1, j & 1
            tgt = (1 - x if fx else x, 1 - y if fy else y, 1 - c if fc else c)
            cp = pltpu.make_async_remote_copy(src_ref=v_ref, dst_ref=land.at[me], send_sem=send.at[j - 1],
                                              recv_sem=recv.at[j - 1], device_id=tgt, device_id_type=MESH)
            cp.start()
            sent.append(cp)
        for j in range(1, 8):
            fx, fy, fc = (j >> 2) & 1, (j >> 1) & 1, j & 1
            peer = 4 * (1 - x if fx else x) + 2 * (1 - y if fy else y) + (1 - c if fc else c)
            pltpu.make_async_remote_copy(src_ref=v_ref, dst_ref=land.at[peer], send_sem=send.at[j - 1],
                                         recv_sem=recv.at[j - 1], device_id=(x, y, c), device_id_type=MESH).wait_recv()
        for cp in sent:
            cp.wait_send()
        mine.wait()

    vm = pl.BlockSpec(memory_space=pltpu.VMEM)
    return pl.pallas_call(
        body, name="allgather8", in_specs=[vm], out_specs=vm, out_shape=_sds((8, rows, cols), F32),
        scratch_shapes=[pltpu.SemaphoreType.DMA((7,)), pltpu.SemaphoreType.DMA((7,)), pltpu.SemaphoreType.DMA],
        compiler_params=pltpu.CompilerParams(has_side_effects=True, vmem_limit_bytes=VMEM_LIMIT))(v)


def _row_tile(rows, cols):
    best = 16
    for t in range(16, rows + 1, 16):
        if rows % t == 0 and t * cols * 4 <= 2 * 1024 * 1024:
            best = t
    return best


def _rs_add_sibling(scal, g, ra, hr):
    cols = g.shape[2]
    tr = _row_tile(hr, cols)
    nr = hr // tr

    def body(s_ref, g_ref, r_ref, p32_ref, p16_ref):
        v = g_ref[...] + r_ref[...]
        p16_ref[...] = v.astype(BF16)

        @pl.when(pl.program_id(1) == s_ref[0])
        def _():
            p32_ref[...] = v

    blk = lambda f: bs((None, tr, cols), f)
    own = blk(lambda i, s, sr: (s, i, 0))
    spec = pltpu.PrefetchScalarGridSpec(num_scalar_prefetch=1, grid=(nr, N_SH),
                                        in_specs=[blk(lambda i, s, sr: (s, sr[1] * nr + i, 0)), own],
                                        out_specs=[bs((tr, cols), lambda i, s, sr: (i, 0)), own])
    return pl.pallas_call(body, name="rs_add_sibling", grid_spec=spec,
                          out_shape=[_sds((hr, cols), F32), _sds((N_SH, hr, cols), BF16)],
                          compiler_params=pltpu.CompilerParams(dimension_semantics=("arbitrary",) * 2,
                                                               vmem_limit_bytes=VMEM_LIMIT))(scal, g, ra)


def _rs_add_chips(scal, p32, rb, hr):
    cols = p32.shape[1]
    tr = _row_tile(hr, cols)
    nr = hr // tr

    def body(s_ref, p_ref, r0, r1, r2, o_ref):
        o_ref[...] = ((p_ref[...] + r0[...].astype(F32)) + r1[...].astype(F32)) + r2[...].astype(F32)

    blk = lambda f: bs((None, tr, cols), f)
    spec = pltpu.PrefetchScalarGridSpec(
        num_scalar_prefetch=1, grid=(nr,),
        in_specs=[bs((tr, cols), lambda i, sr: (i, 0))] + [blk(functools.partial(lambda i, sr, j: (j, i, 0), j=j))
                                                            for j in range(3)],
        out_specs=blk(lambda i, sr: (sr[1], i, 0)))
    return pl.pallas_call(body, name="rs_add_chips", grid_spec=spec, out_shape=_sds((2, hr, cols), F32),
                          compiler_params=pltpu.CompilerParams(dimension_semantics=("arbitrary",),
                                                               vmem_limit_bytes=VMEM_LIMIT))(scal, p32, rb, rb, rb)


class SplitReduceScatter:
    def __init__(self, gs):
        x, y, c = _place()
        self.scal = jnp.stack([2 * x + y, c]).astype(jnp.int32)
        self.gs, self.n = list(gs), len(gs)
        self.hrs = [g.shape[1] // 2 for g in gs]

    def swap_start(self, after):
        n, hrs = self.n, self.hrs

        def plan(refs):
            x, y, c = _place()
            return [(refs[a].at[:, pl.ds((1 - c) * hrs[a], hrs[a])], refs[n + a], refs[n + a], (x, y, 1 - c))
                    for a in range(n)]

        lands = [lax.empty((N_SH, hrs[a], g.shape[2]), F32) for a, g in enumerate(self.gs)]
        self.ex = SplitExchange("rs_swap_halves", self.gs + lands, plan, n)
        return self.ex.start(after)

    def swap_wait_send_start(self, after):
        n, hrs = self.n, self.hrs
        bufs = self.ex.wait(after)
        parts = [_rs_add_sibling(self.scal, bufs[a], bufs[n + a], hrs[a]) for a in range(n)]
        self.p32 = [p[0] for p in parts]

        def plan(refs):
            x, y, c = _place()
            return [(refs[a].at[2 * cx + cy], refs[n + a].at[j], refs[n + a].at[j], (cx, cy, c))
                    for a in range(n) for j, (cx, cy) in enumerate(_chips_of(x, y))]

        lands = [lax.empty((3, hrs[a], g.shape[2]), BF16) for a, g in enumerate(self.gs)]
        self.ex = SplitExchange("rs_send_partials", [p[1] for p in parts] + lands, plan, 3 * n)
        return self.ex.start(parts[-1][1])

    def send_wait_share_start(self, after):
        n, hrs = self.n, self.hrs
        bufs = self.ex.wait(after)
        fins = [_rs_add_chips(self.scal, self.p32[a], bufs[n + a], hrs[a]) for a in range(n)]

        def plan(refs):
            x, y, c = _place()
            return [(refs[a].at[c], refs[a].at[c], refs[a].at[1 - c], (x, y, 1 - c)) for a in range(n)]

        self.ex = SplitExchange("rs_share_halves", fins, plan, n)
        return self.ex.start(fins[-1])

    def share_wait(self, after):
        fulls = self.ex.wait(after)
        return [f.reshape(2 * hr, f.shape[2]) for f, hr in zip(fulls, self.hrs)]


def adamw(w, g, m, v):
    shape = w.shape
    cols = shape[-1]
    rows = math.prod(shape[:-1]) if len(shape) > 1 else 1
    tr = 256 if rows % 256 == 0 and rows > 256 else rows
    c1 = 1.0 - ADAM_B1 ** ADAM_STEP
    c2 = 1.0 - ADAM_B2 ** ADAM_STEP

    def body(w_ref, g_ref, m_ref, v_ref, d_ref, nm_ref, nv_ref):
        gv = g_ref[...]
        nm = ADAM_B1 * m_ref[...] + (1.0 - ADAM_B1) * gv
        nv = ADAM_B2 * v_ref[...] + (1.0 - ADAM_B2) * (gv * gv)
        nm_ref[...] = nm
        nv_ref[...] = nv
        d_ref[...] = -ADAM_LR * ((nm / c1) / (jnp.sqrt(nv / c2) + ADAM_EPS) + ADAM_WD * w_ref[...])

    row = bs((tr, cols), lambda i: (i, 0))
    outs = _call(body, "adamw", (rows // tr,), [row] * 4, [row] * 3, [_sds((rows, cols), F32)] * 3)(
        *[a.reshape(rows, cols) for a in (w, g, m, v)])
    return [o.reshape(shape) for o in outs]


def adamw_layers(w, gs, m, v, lo, into=None, after=None):
    shape = w.shape
    cols = shape[-1]
    rl = math.prod(shape[1:-1])
    tr = max(t_ for t_ in range(8, rl + 1, 8) if rl % t_ == 0 and t_ * cols * 4 <= 1024 * 1024)
    nb = rl // tr
    n = len(gs)
    c1 = 1.0 - ADAM_B1 ** ADAM_STEP
    c2 = 1.0 - ADAM_B2 ** ADAM_STEP

    def body(*refs):
        w_ref, m_ref, v_ref = refs[:3]
        g_refs = refs[3:3 + n]
        d_ref, nm_ref, nv_ref, go_ref = refs[-4:]
        layer = pl.program_id(0) // nb
        for k in range(n):
            @pl.when(layer == k)
            def _(k=k):
                gv = g_refs[k][...]
                nm = ADAM_B1 * m_ref[...] + (1.0 - ADAM_B1) * gv
                nv = ADAM_B2 * v_ref[...] + (1.0 - ADAM_B2) * (gv * gv)
                nm_ref[...] = nm
                nv_ref[...] = nv
                go_ref[...] = gv
                d_ref[...] = -ADAM_LR * ((nm / c1) / (jnp.sqrt(nv / c2) + ADAM_EPS) + ADAM_WD * w_ref[...])

    row = bs((tr, cols), lambda b: (lo * nb + b, 0))
    g_specs = [bs((tr, cols), functools.partial(lambda b, k: (jnp.clip(b - k * nb, 0, nb - 1), 0), k=k)) for k in range(n)]
    flat = lambda a: a.reshape(-1, cols)
    in_specs = [row] * 3 + g_specs
    args = [flat(w), flat(m), flat(v)] + [flat(g) for g in gs]
    aliases = None
    if into is not None:
        aliases = {len(in_specs) + k: k for k in range(4)}
        in_specs = in_specs + [ANY] * 4
        args = args + [flat(a) for a in into]
    outs = _call(body, "adamw_layers", (n * nb,), in_specs, [row] * 4, [_sds((shape[0] * rl, cols), F32)] * 4,
                 aliases=aliases, after=after)(*args)
    return [o.reshape(shape) for o in outs]


def allreduce8_split(vec):
    rows, cols = vec.shape

    def plan(refs):
        x, y, c = _place()
        me = 4 * x + 2 * y + c
        out = []
        for j in range(1, 8):
            px, py, pc = (1 - x if j & 4 else x), (1 - y if j & 2 else y), (1 - c if j & 1 else c)
            out.append((refs[0], refs[1].at[me], refs[1].at[4 * px + 2 * py + pc], (px, py, pc)))
        return out

    ex = SplitExchange("allreduce8", [vec, lax.empty((8, rows, cols), F32)], plan, 7)

    def finish(after):
        v, land = ex.wait(after)
        x, y, c = _place()
        me = jnp.reshape(4 * x + 2 * y + c, (1,)).astype(jnp.int32)

        def body(me_ref, v_ref, l_ref, o_ref):
            o_ref[...] = jnp.zeros_like(o_ref)
            for k in range(8):
                @pl.when(me_ref[0] == k)
                def _():
                    o_ref[...] += v_ref[...]

                @pl.when(me_ref[0] != k)
                def _(k=k):
                    o_ref[...] += l_ref[k]

        spec = pltpu.PrefetchScalarGridSpec(
            num_scalar_prefetch=1, grid=(1,),
            in_specs=[bs((rows, cols), lambda i, mr: (0, 0)), bs((8, rows, cols), lambda i, mr: (0, 0, 0))],
            out_specs=bs((rows, cols), lambda i, mr: (0, 0)))
        return pl.pallas_call(body, name="allreduce8_sum", grid_spec=spec, out_shape=_sds((rows, cols), F32),
                              compiler_params=pltpu.CompilerParams(dimension_semantics=("arbitrary",),
                                                                   vmem_limit_bytes=VMEM_LIMIT))(me, v, land)

    return ex, finish


class Hooks:
    def __init__(self):
        self.steps = {}

    def add(self, point, fn):
        self.steps.setdefault(point, []).append(fn)

    def run(self, point, arr, env=None):
        tok = None
        for fn in self.steps.get(point, ()):
            got = fn(arr if tok is None else tok, env)
            tok = tok if got is None else got
        return tok


def layer_fwd(x, p_i, w, hooks):
    h, proj, *qkv = norm_in_proj(x, w["g_mix"], w["win"], after=hooks.run("start", x))
    ya = conva_fwd(proj, w["conv_a"])
    yb, o32, lse = attn_merge([attn_fwd_group(pv, d) for pv, d in zip(qkv, DILATIONS)])
    yc = sgu_fwd(proj, w["sgu_ln_g"], w["sgu_ln_b"], w["sgu_wt"], w["sgu_bf"])
    yd, z = conf_fwd(proj, w["conf_dw"], w["conf_ln_g"], w["conf_ln_b"], after=hooks.run("pre_conf", [ya, yb, yc]))
    ys = (ya, yb, yc, yd)
    tok = hooks.run("pre_merge", yd)
    merged, gates, ybr = merge_fwd(h, ys, w["wg"], w["wbr"], after=tok)
    x1, h2, fgu, act = ffn_in(merged, w["wout"], x, w["g_ffn"], w["wfi"], after=hooks.run("post_merge", merged))
    x2, h3, gate, pp, x3 = ple_fwd(act, w["wfo"], x1, w["g_ple"], w["wpg"], p_i, w["wpp"],
                                   after=hooks.run("post_ffn_in", act))
    saved = dict(x=x, h=h, proj=proj, qkv=qkv, ys=ys, o32=o32, lse=lse, z=z, merged=merged, gates=gates, ybr=ybr, x1=x1,
                 h2=h2, fgu=fgu, act=act, x2=x2, h3=h3, gate=gate, pp=pp)
    return x3, saved


def layer_bwd(dx3, p_i, w, s, hooks):
    t = dx3.shape[0]
    tr = min(1024, t)
    nr = t // tr
    ns_fi = FFN_H // 2
    small = {}

    ga_shape, gb_shape = _sds((N_SH, 6 * BW, D_MODEL), F32), _sds((N_SH, 5 * BW, BW), F32)
    ga_blk = lambda idx: bs((N_SH, BW, D_MODEL), idx)
    dx2, small["g_ple"], ga, gb = ple_bwd(dx3, s["gate"], s["pp"], w["wpg"], s["x2"], w["g_ple"], s["h3"], p_i,
                                          ga_shape, gb_shape, after=hooks.run("start", dx3))

    df, dx1, small["g_ffn"] = ffn_bwd(dx2, w["wfo"], s["fgu"], w["wfi"], s["x1"], w["g_ffn"],
                                      after=hooks.run("pre_ffn", dx2))
    gfo = tn_matmul("dw_ffn_out", s["act"], dx2, (2, nr), bs((tr, ns_fi), lambda j, r: (r, j)),
                    bs((tr, D_MODEL), lambda j, r: (r, 0)), bs((2, FFN_H // N_SH, D_MODEL), lambda j, r: (j, 0, 0)),
                    _sds((N_SH, FFN_H // N_SH, D_MODEL), F32), split=2)
    gfi = tn_matmul("dw_ffn_in", s["h2"], df, (N_SH, nr), bs((tr, D_MODEL), lambda j, r: (r, 0)),
                    bs((None, tr, ns_fi), lambda j, r: (j // 2, r, j % 2)),
                    bs((None, D_MODEL, ns_fi), lambda j, r: (j, 0, 0)), _sds((N_SH, D_MODEL, ns_fi), F32))

    dpre_m, dys, gb, ga = merge_bwd(dx1, w["wout"], s["gates"], s["ybr"], w["wbr"], s["ys"], gb, s["merged"], ga)
    ga = tn_matmul("dw_merge_gate", s["h"], dpre_m, (N_BR, nr), bs((tr, D_MODEL), lambda k, r: (r, 0)),
                   bs((None, tr, D_MODEL), lambda k, r: (k, r, 0)), ga_blk(lambda k, r: (0, k, 0)), ga_shape,
                   split=N_SH, into=ga, after=hooks.run("pre_dw_out", dys, dict(gfo=gfo, gfi=gfi)))

    dproj, small["conv_a"] = conva_bwd(s["proj"], dys, w["conv_a"], after=hooks.run("pre_conva", gb))
    lds, dy_views = attn_delta(dys, s["o32"], s["lse"])
    dy_views = [dys[1]] + list(dy_views)
    dproj = attn_bwd_finish([attn_bwd_group(pv, dov, ldv, d)
                             for pv, dov, ldv, d in zip(s["qkv"], dy_views, lds, DILATIONS)], dproj)
    dproj, d_sw, d_sbf, small["sgu_ln_g"], small["sgu_ln_b"] = sgu_bwd(
        s["proj"], dys, w["sgu_ln_g"], w["sgu_ln_b"], w["sgu_wt"], w["sgu_bf"], dproj)
    small["sgu_w"] = jnp.where(jnp.tril(jnp.ones((BLK, BLK), bool))[None], d_sw, 0.0)
    small["sgu_b"] = jnp.sum(d_sbf.reshape(BLK, 4, HEAD_D), axis=-1).T
    dz, small["conf_ln_g"], small["conf_ln_b"] = conf_bwd_ln(s["z"], dys, w["conf_ln_g"], w["conf_ln_b"])
    dproj, small["conf_dw"] = conf_bwd_conv(s["proj"], dz, w["conf_dw"], dproj)

    ns_in = N_IN // N_SH
    gin = tn_matmul("dw_in", s["h"], dproj, (N_SH, nr), bs((tr, D_MODEL), lambda j, r: (r, 0)),
                    bs((tr, ns_in), lambda j, r: (r, j)), bs((None, D_MODEL, ns_in), lambda j, r: (j, 0, 0)),
                    _sds((N_SH, D_MODEL, ns_in), F32), after=hooks.run("pre_dw_in", dproj))
    hooks.run("end", gin)
    big = [ga, gfo, gb, gin, gfi]
    dx, small["g_mix"] = norm_bwd(
        "mix_norm_bwd",
        [(dpre_m, lambda tm: bs((N_BR, tm, D_MODEL), lambda i: (0, i, 0)), w["wg"],
          lambda a, wr: [(a[k], wr[k]) for k in range(N_BR)]),
         (dproj, lambda tm: bs((tm, N_IN), lambda i: (i, 0)), w["win"],
          lambda a, wr: [(a[:, k * ns_in:(k + 1) * ns_in], wr[k]) for k in range(N_SH)])],
        dx1, s["x"], w["g_mix"])
    return dx, big, small


BIG_NAMES = ("w_in", "w_branch", "w_merge_gate", "w_out", "w_ffn_in", "w_ffn_out", "w_ple_gate", "w_ple_proj")


def unpack_big_grads(ga, gfo, gb, gin, gfi):
    return dict(w_in=gin, w_ffn_in=gfi, w_ffn_out=gfo,
                w_merge_gate=ga[:N_BR * BW].reshape(N_BR, BW, D_MODEL), w_out=ga[N_BR * BW:5 * BW], w_ple_gate=ga[5 * BW:],
                w_branch=gb[:N_BR * BW].reshape(N_BR, BW, BW), w_ple_proj=gb[N_BR * BW:])


SMALL_NAMES = ("g_mix", "conv_a", "sgu_ln_g", "sgu_ln_b", "sgu_w", "sgu_b", "conf_dw", "conf_ln_g", "conf_ln_b",
               "g_ffn", "g_ple")


def _pack_rows(arrays, rows):
    flat = jnp.concatenate([a.reshape(-1) for a in arrays])
    return jnp.pad(flat, (0, rows * D_MODEL - flat.shape[0])).reshape(rows, D_MODEL)


def _unpack_rows(packed, shapes):
    flat, out, pos = packed.reshape(-1), [], 0
    for shape in shapes:
        n = math.prod(shape)
        out.append(flat[pos:pos + n].reshape(shape))
        pos += n
    return out


def kernel(x, p, g_mix, w_in, conv_a, sgu_ln_g, sgu_ln_b, sgu_w, sgu_b, conf_dw, conf_ln_g, conf_ln_b, w_branch, w_merge_gate, w_out, g_ffn, w_ffn_in, w_ffn_out, g_ple, w_ple_gate, w_ple_proj, g_final, loss_target, m_g_mix, m_w_in, m_conv_a, m_sgu_ln_g, m_sgu_ln_b, m_sgu_w, m_sgu_b, m_conf_dw, m_conf_ln_g, m_conf_ln_b, m_w_branch, m_w_merge_gate, m_w_out, m_g_ffn, m_w_ffn_in, m_w_ffn_out, m_g_ple, m_w_ple_gate, m_w_ple_proj, m_g_final, v_g_mix, v_w_in, v_conv_a, v_sgu_ln_g, v_sgu_ln_b, v_sgu_w, v_sgu_b, v_conf_dw, v_conf_ln_g, v_conf_ln_b, v_w_branch, v_w_merge_gate, v_w_out, v_g_ffn, v_w_ffn_in, v_w_ffn_out, v_g_ple, v_w_ple_gate, v_w_ple_proj, v_g_final):
    weights = dict(g_mix=g_mix, w_in=w_in, conv_a=conv_a, sgu_ln_g=sgu_ln_g, sgu_ln_b=sgu_ln_b, sgu_w=sgu_w, sgu_b=sgu_b,
                   conf_dw=conf_dw, conf_ln_g=conf_ln_g, conf_ln_b=conf_ln_b, w_branch=w_branch, w_merge_gate=w_merge_gate,
                   w_out=w_out, g_ffn=g_ffn, w_ffn_in=w_ffn_in, w_ffn_out=w_ffn_out, g_ple=g_ple, w_ple_gate=w_ple_gate,
                   w_ple_proj=w_ple_proj, g_final=g_final)
    m_in = dict(g_mix=m_g_mix, w_in=m_w_in, conv_a=m_conv_a, sgu_ln_g=m_sgu_ln_g, sgu_ln_b=m_sgu_ln_b, sgu_w=m_sgu_w,
                sgu_b=m_sgu_b, conf_dw=m_conf_dw, conf_ln_g=m_conf_ln_g, conf_ln_b=m_conf_ln_b, w_branch=m_w_branch,
                w_merge_gate=m_w_merge_gate, w_out=m_w_out, g_ffn=m_g_ffn, w_ffn_in=m_w_ffn_in, w_ffn_out=m_w_ffn_out,
                g_ple=m_g_ple, w_ple_gate=m_w_ple_gate, w_ple_proj=m_w_ple_proj, g_final=m_g_final)
    v_in = dict(g_mix=v_g_mix, w_in=v_w_in, conv_a=v_conv_a, sgu_ln_g=v_sgu_ln_g, sgu_ln_b=v_sgu_ln_b, sgu_w=v_sgu_w,
                sgu_b=v_sgu_b, conf_dw=v_conf_dw, conf_ln_g=v_conf_ln_g, conf_ln_b=v_conf_ln_b, w_branch=v_w_branch,
                w_merge_gate=v_w_merge_gate, w_out=v_w_out, g_ffn=v_g_ffn, w_ffn_in=v_w_ffn_in, w_ffn_out=v_w_ffn_out,
                g_ple=v_g_ple, w_ple_gate=v_w_ple_gate, w_ple_proj=v_w_ple_proj, g_final=v_g_final)
    order = ("g_mix", "w_in", "conv_a", "sgu_ln_g", "sgu_ln_b", "sgu_w", "sgu_b", "conf_dw", "conf_ln_g", "conf_ln_b",
             "w_branch", "w_merge_gate", "w_out", "g_ffn", "w_ffn_in", "w_ffn_out", "g_ple", "w_ple_gate", "w_ple_proj",
             "g_final")
    depth = g_mix.shape[0]
    xs, tgt = x[0], loss_target[0]
    cw = BW // N_SH
    my_shard = 2 * lax.axis_index("x") + lax.axis_index("y")

    conv_rows = 16
    allc = gather8(_pack_rows([conv_a, conf_dw], conv_rows))
    shards = [_unpack_rows(allc[2 * s], [conv_a.shape, conf_dw.shape]) for s in range(N_SH)]
    conv_a_full = jnp.concatenate([sh[0] for sh in shards], axis=-1)
    conf_dw_full = jnp.concatenate([sh[1] for sh in shards], axis=-1)

    tril = jnp.tril(jnp.ones((BLK, BLK), bool))
    def placed_shards(i):
        shards = ([w_in[i], w_branch[i]] + [w_merge_gate[i, k] for k in range(N_BR)]
                  + [w_out[i], w_ffn_in[i], w_ffn_out[i], w_ple_gate[i], w_ple_proj[i]])
        return [lax.dynamic_update_slice(lax.empty((N_SH,) + sh.shape, BF16), sh.astype(BF16)[None],
                                         (my_shard,) + (0,) * sh.ndim) for sh in shards]

    def small_weights(i, win):
        vec = lambda a: a[i].reshape(1, -1)
        return dict(
            win=win, g_mix=vec(g_mix), g_ffn=vec(g_ffn), g_ple=vec(g_ple), conv_a=conv_a_full[i], conf_dw=conf_dw_full[i],
            sgu_ln_g=vec(sgu_ln_g), sgu_ln_b=vec(sgu_ln_b), conf_ln_g=vec(conf_ln_g), conf_ln_b=vec(conf_ln_b),
            sgu_wt=jnp.where(tril[None], sgu_w[i], 0.0).astype(BF16),
            sgu_bf=jnp.repeat(sgu_b[i].T, HEAD_D, axis=1))

    def late_weights(got):
        return dict(wbr=got[0], wg=jnp.stack([g.reshape(D_MODEL, D_MODEL) for g in got[1:5]]),
                    wout=got[5].reshape(D_MODEL, D_MODEL), wfi=got[6], wfo=got[7].reshape(FFN_H, D_MODEL),
                    wpg=got[8].reshape(D_MODEL, D_MODEL), wpp=got[9])

    class SplitAllGather:
        def __init__(self, bufs):
            self.shapes = [b.shape for b in bufs]
            self.ici = SplitExchange("allgather_ici", bufs, allgather_ici_plan(self.shapes), 3 * len(bufs))

        def ici_start(self, after, env=None):
            return self.ici.start(after)

        def ici_wait_d2d_start(self, after, env=None):
            landed = self.ici.wait(after)
            self.d2d = SplitExchange("allgather_d2d", landed, allgather_d2d_plan(self.shapes), 3 * len(landed))
            return self.d2d.start(landed[-1])

        def d2d_wait(self, after, env=None):
            self.got = self.d2d.wait(after)
            return None

    bufs0 = placed_shards(0)
    first = SplitAllGather(bufs0[:1])
    first.d2d_wait(first.ici_wait_d2d_start(first.ici_start(xs)))
    rest = SplitAllGather(bufs0[1:])
    layers = [small_weights(0, first.got[0])]
    act, saved = xs, []
    nxt_done = None
    for i in range(depth):
        hooks = Hooks()
        if i == 0:
            hooks.add("start", rest.ici_start)
            hooks.add("pre_conf", rest.ici_wait_d2d_start)
            hooks.add("pre_merge", rest.d2d_wait)
            hooks.add("pre_merge", lambda after, env: layers[0].update(late_weights(rest.got)))
        if i + 1 < depth:
            nxt = SplitAllGather(placed_shards(i + 1))
            points = ("pre_merge", "post_ffn_in", None) if i == 0 else ("start", "post_merge", "post_ffn_in")
            hooks.add(points[0], nxt.ici_start)
            hooks.add(points[1], nxt.ici_wait_d2d_start)
            if points[2]:
                hooks.add(points[2], nxt.d2d_wait)
        act, sv = layer_fwd(act, p[i, 0], layers[i], hooks)
        saved.append(sv)
        if i + 1 < depth:
            if i == 0:
                nxt.d2d_wait(act)
            layers.append({**small_weights(i + 1, nxt.got[0]), **late_weights(nxt.got[1:])})
    loss_part, dx, dg_final = loss_head(act, g_final.reshape(1, -1), tgt)

    big_red = [None] * depth
    small_red = [None] * depth
    small_rows = 80
    pending = None

    def small_vector(i, small):
        parts = [small[n] for n in SMALL_NAMES]
        return _pack_rows(parts + ([dg_final, loss_part[0, :1]] if i == 0 else []), small_rows)

    for i in reversed(range(depth)):
        hooks = Hooks()
        result = {}
        if pending is not None:
            rs, j, (small_ex, small_finish) = pending
            hooks.add("start", lambda after, env, ex=small_ex: ex.start(after))
            hooks.add("start", lambda after, env, rs=rs: rs.swap_start(after))
            hooks.add("pre_ffn", lambda after, env, rs=rs: rs.swap_wait_send_start(after))
            hooks.add("pre_dw_out", lambda after, env, rs=rs: rs.send_wait_share_start(after))
            hooks.add("pre_conva", lambda after, env, rs=rs, result=result: result.update(prev=rs.share_wait(after)))
            hooks.add("pre_conva", lambda after, env, fin=small_finish, result=result: result.update(small=fin(after)))
        if i == 0:
            def early_start(after, env, result=result):
                result["rs"] = SplitReduceScatter([env["gfo"], env["gfi"]])
                return result["rs"].swap_start(after)

            hooks.add("pre_dw_out", early_start)
            hooks.add("pre_conva", lambda after, env, result=result: result["rs"].swap_wait_send_start(after))
            hooks.add("pre_dw_in", lambda after, env, result=result: result["rs"].send_wait_share_start(after))
            hooks.add("end", lambda after, env, result=result: result.update(early=result["rs"].share_wait(after)))
        dx, big, small = layer_bwd(dx, p[i, 0], layers[i], saved[i], hooks)
        if pending is not None:
            big_red[pending[1]] = unpack_big_grads(*result["prev"])
            small_red[pending[1]] = result["small"]
        if i > 0:
            pending = (SplitReduceScatter(big), i, allreduce8_split(small_vector(i, small)))

    ga, _, gb, gin, _ = big
    late = SplitReduceScatter([ga, gb, gin])
    small_ex, small_finish = allreduce8_split(small_vector(0, small))
    upd = {}

    def update_upper(names, after):
        for name in names:
            upd[name] = adamw_layers(weights[name], [big_red[i][name] for i in range(1, depth)], m_in[name], v_in[name],
                                     1, after=after)
        return [upd[name][0] for name in names]

    done = update_upper(("w_in",), late.swap_start(small_ex.start(dx)))
    done = update_upper(("w_ffn_in", "w_merge_gate", "w_ffn_out"), late.swap_wait_send_start(done))
    small_red[0] = small_finish(done)
    done = update_upper(("w_branch", "w_out", "w_ple_gate", "w_ple_proj"), late.send_wait_share_start(done))
    ga, gb, gin = late.share_wait(done)
    gfo, gfi = result["early"]
    big_red[0] = unpack_big_grads(ga, gfo, gb, gin, gfi)

    layer_shapes = [small[n].shape for n in SMALL_NAMES]
    per_layer = [_unpack_rows(small_red[i], layer_shapes + ([dg_final.shape, (1,)] if i == 0 else []))
                 for i in range(depth)]
    grads = {n: jnp.stack([per_layer[i][k].reshape(weights[n].shape[1:] if n not in ("conv_a", "conf_dw")
                                                   else per_layer[i][k].shape) for i in range(depth)])
             for k, n in enumerate(SMALL_NAMES)}
    grads["g_final"] = per_layer[0][-2].reshape(-1)
    loss = per_layer[0][-1].reshape(())
    for n in ("conv_a", "conf_dw"):
        grads[n] = lax.dynamic_slice_in_dim(grads[n], my_shard * cw, cw, axis=2)

    small_all = [n for n in order if n not in BIG_NAMES]
    sm_shapes = [weights[n].shape for n in small_all]
    n_sm = sum(math.prod(sh) for sh in sm_shapes)
    sm_rows = -(-n_sm // (8 * D_MODEL)) * 8
    packed = [_pack_rows([src[n] for n in small_all], sm_rows) for src in (weights, grads, m_in, v_in)]
    sm_out = [_unpack_rows(o, sm_shapes) for o in adamw(*packed)]
    delta, new_m, new_v = ({n: o[k] for k, n in enumerate(small_all)} for o in sm_out)
    for name in BIG_NAMES:
        delta[name], new_m[name], new_v[name], grads[name] = adamw_layers(
            weights[name], [big_red[0][name]], m_in[name], v_in[name], 0, into=upd[name])

    return (loss, dx[None], *[grads[n] for n in order], *[delta[n] for n in order], *[new_m[n] for n in order],
            *[new_v[n] for n in order])
```

```python
import functools
import math

import jax
import jax.numpy as jnp
from jax import lax
from jax.experimental import pallas as pl
from jax.experimental.pallas import tpu as pltpu

F32 = jnp.float32
BF16 = jnp.bfloat16
EPS = 1e-6
D_MODEL = 1024
BW = 256
N_BR = 4
N_IN = 10 * BW
FFN_H = 2816
N_SH = 4
HEADS = 4
HEAD_D = 64
BLK = 128
DILATIONS = (1, 4, 16)
CONF_K = 31
CONVA_K = 3
NEG = -1e30
VMEM_LIMIT = 56 * 1024 * 1024
MESH = pl.DeviceIdType.MESH

ADAM_LR, ADAM_B1, ADAM_B2, ADAM_EPS, ADAM_WD, ADAM_STEP = 0.001, 0.9, 0.999, 1e-08, 0.01, 10

bs = pl.BlockSpec
ANY = pl.BlockSpec(memory_space=pl.ANY)


def _call(body, name, grid, in_specs, out_specs, out_shape, scratch=(), aliases=None, after=None):
    n_in = len(in_specs)
    kernel_body = body
    if after is not None:
        in_specs = list(in_specs) + [ANY]

        def kernel_body(*refs):
            return body(*refs[:n_in], *refs[n_in + 1:])

    call = pl.pallas_call(
        kernel_body, name=name, grid=grid, in_specs=in_specs, out_specs=out_specs, out_shape=out_shape,
        scratch_shapes=list(scratch), input_output_aliases=aliases or {},
        compiler_params=pltpu.CompilerParams(dimension_semantics=("arbitrary",) * len(grid),
                                             vmem_limit_bytes=VMEM_LIMIT))
    return call if after is None else (lambda *args: call(*args, after))


def _sds(shape, dtype):
    return jax.ShapeDtypeStruct(shape, dtype)


def _nn(a, b):
    return jnp.dot(a, b, preferred_element_type=F32)


def _nt(a, b):
    return lax.dot_general(a, b, (((1,), (1,)), ((), ())), preferred_element_type=F32)


def _tn(a, b):
    return lax.dot_general(a, b, (((0,), (0,)), ((), ())), preferred_element_type=F32)


def _sigmoid(x):
    return pl.reciprocal(1.0 + jnp.exp(-x), approx=True)


def _rms_fwd(x, g):
    r = lax.rsqrt(jnp.mean(x * x, axis=-1, keepdims=True) + EPS)
    return x * r * g


def _rms_bwd(dh, x, g):
    r = lax.rsqrt(jnp.mean(x * x, axis=-1, keepdims=True) + EPS)
    xr = x * r
    dxr = dh * g
    dx = r * (dxr - xr * jnp.mean(dxr * xr, axis=-1, keepdims=True))
    return dx, dh * xr


def _ln_hat(x):
    mu = jnp.mean(x, axis=-1, keepdims=True)
    xc = x - mu
    r = lax.rsqrt(jnp.mean(xc * xc, axis=-1, keepdims=True) + EPS)
    return xc * r, r


def _ln_bwd(dy, xhat, r, g):
    dxh = dy * g
    return r * (dxh - jnp.mean(dxh, axis=-1, keepdims=True) - xhat * jnp.mean(dxh * xhat, axis=-1, keepdims=True))


def _colsum(v):
    return jnp.sum(v, axis=0, keepdims=True)


def _causal_conv(zext, w_ref, k_taps, halo):
    acc = zext[halo:] * w_ref[k_taps - 1:k_taps, :]
    for k in range(k_taps - 1):
        acc = acc + pltpu.roll(zext, k_taps - 1 - k, 0)[halo:] * w_ref[k:k + 1, :]
    return acc


def _anti_conv(dext, w_ref, k_taps, tm):
    n = dext.shape[0]
    acc = dext[:tm] * w_ref[k_taps - 1:k_taps, :]
    for s in range(1, k_taps):
        acc = acc + pltpu.roll(dext, n - s, 0)[:tm] * w_ref[k_taps - 1 - s:k_taps - s, :]
    return acc


def _conv_wgrad(dw_ref, dc, zext, k_taps, halo):
    dw_ref[k_taps - 1:k_taps, :] += _colsum(dc * zext[halo:])
    for k in range(k_taps - 1):
        dw_ref[k:k + 1, :] += _colsum(dc * pltpu.roll(zext, k_taps - 1 - k, 0)[halo:])


LANES = 128


def _to_strided_view(dst_ref, chunk, scr, d, width):
    n = scr.shape[0] // d
    for c in range(width // LANES):
        scr[...] = chunk(c)
        for r in range(d):
            dst_ref[:, r * width + c * LANES:r * width + (c + 1) * LANES] = scr[pl.ds(r, n, stride=d), :].astype(dst_ref.dtype)


def _from_strided_view(src_ref, scr, d, width, c):
    n = scr.shape[0] // d
    for r in range(d):
        scr[pl.ds(r, n, stride=d), :] = src_ref[:, r * width + c * LANES:r * width + (c + 1) * LANES].astype(F32)
    return scr[...]


def _view_spec(tm, d, width):
    return bs((tm // d, d * width), lambda i: (i, 0))


def _prev_blk(i, per):
    return jnp.maximum(i * per - 1, 0)


def _next_blk(i, per, last):
    return jnp.minimum((i + 1) * per, last)


def norm_in_proj(x, g, win, after=None):
    t = x.shape[0]
    tm = min(512, t)
    ns = win.shape[2]

    def body(x_ref, g_ref, w_ref, h_ref, o_ref, q_ref, q4_ref, q16_ref, scr):
        h = _rms_fwd(x_ref[...], g_ref[...]).astype(BF16)
        h_ref[...] = h
        parts = []
        for s in range(N_SH):
            r = _nn(h, w_ref[s])
            o_ref[:, s * ns:(s + 1) * ns] = r
            if s == 1:
                parts.append(r[:, 3 * BW - ns:])
            if s == 2:
                parts.append(r[:, :6 * BW - 2 * ns])
        qf = jnp.concatenate(parts, axis=1)
        q_ref[...] = qf.astype(BF16)
        chunk = lambda c: qf[:, c * LANES:(c + 1) * LANES]
        _to_strided_view(q4_ref, chunk, scr, 4, 3 * BW)
        _to_strided_view(q16_ref, chunk, scr, 16, 3 * BW)

    row = lambda c: bs((tm, c), lambda i: (i, 0))
    return _call(
        body, "norm_in_proj", (t // tm,), [row(D_MODEL), bs((1, D_MODEL), lambda i: (0, 0)), _resident(win)],
        [row(D_MODEL), row(N_IN), row(3 * BW), _view_spec(tm, 4, 3 * BW), _view_spec(tm, 16, 3 * BW)],
        [_sds((t, D_MODEL), BF16), _sds((t, N_IN), F32), _sds((t, 3 * BW), BF16),
         _sds((t // 4, 4 * 3 * BW), BF16), _sds((t // 16, 16 * 3 * BW), BF16)],
        scratch=[pltpu.VMEM((tm, LANES), F32)], after=after)(x, g, win)


def merge_fwd(h, ys, wg, wbr, after=None):
    t = h.shape[0]
    tm = min(512, t)

    def body(h_ref, ya, yb, yc, yd, wg_ref, wb_ref, m_ref, g_ref, b_ref):
        hh = h_ref[...]
        for j in range(N_SH):
            cs = slice(j * BW, (j + 1) * BW)
            acc = None
            for k, y_ref in enumerate((ya, yb, yc, yd)):
                g = _sigmoid(_nn(hh, wg_ref[k, :, cs]))
                b = _nn(y_ref[...], wb_ref[j, k])
                g_ref[k, :, cs] = g.astype(BF16)
                b_ref[k, :, cs] = b.astype(BF16)
                acc = g * b if acc is None else acc + g * b
            m_ref[:, cs] = acc.astype(BF16)

    ysp = bs((tm, BW), lambda i: (i, 0))
    big = bs((N_BR, tm, D_MODEL), lambda i: (0, i, 0))
    return _call(
        body, "merge_fwd", (t // tm,),
        [bs((tm, D_MODEL), lambda i: (i, 0)), ysp, ysp, ysp, ysp, _resident(wg), _resident(wbr)],
        [bs((tm, D_MODEL), lambda i: (i, 0)), big, big],
        [_sds((t, D_MODEL), BF16), _sds((N_BR, t, D_MODEL), BF16), _sds((N_BR, t, D_MODEL), BF16)], after=after)(
            h, *ys, wg, wbr)


def ffn_in(a, wout, res, g, wfi, after=None):
    t = res.shape[0]
    tm = min(256, t)
    ns = wfi.shape[2]

    def body(a_ref, wo_ref, r_ref, g_ref, w_ref, x_ref, h_ref, f_ref, act_ref):
        xv = r_ref[...] + _nn(a_ref[...], wo_ref[...])
        x_ref[...] = xv
        h = _rms_fwd(xv, g_ref[...]).astype(BF16)
        h_ref[...] = h
        for j in range(2):
            cs = slice(j * ns, (j + 1) * ns)
            fg = _nn(h, w_ref[j])
            fu = _nn(h, w_ref[j + 2])
            f_ref[0, :, cs] = fg.astype(BF16)
            f_ref[1, :, cs] = fu.astype(BF16)
            act_ref[:, cs] = (fg * _sigmoid(fg) * fu).astype(BF16)

    row = lambda c: bs((tm, c), lambda i: (i, 0))
    return _call(
        body, "ffn_in", (t // tm,),
        [row(D_MODEL), _resident(wout), row(D_MODEL), bs((1, D_MODEL), lambda i: (0, 0)), _resident(wfi)],
        [row(D_MODEL), row(D_MODEL), bs((2, tm, FFN_H), lambda i: (0, i, 0)), row(FFN_H)],
        [_sds((t, D_MODEL), F32), _sds((t, D_MODEL), BF16), _sds((2, t, FFN_H), BF16), _sds((t, FFN_H), BF16)],
        after=after)(a, wout, res, g, wfi)


def ple_fwd(a, wfo, res, g, wpg, p_i, wpp, after=None):
    t = res.shape[0]
    tm = min(512, t)

    def body(a_ref, wo_ref, r_ref, g_ref, wg_ref, p_ref, wp_ref, x_ref, h_ref, gt_ref, pp_ref, o_ref):
        xv = r_ref[...] + _nn(a_ref[...], wo_ref[...])
        x_ref[...] = xv
        h = _rms_fwd(xv, g_ref[...]).astype(BF16)
        h_ref[...] = h
        gate = _sigmoid(_nn(h, wg_ref[...]))
        pb = p_ref[...].astype(BF16)
        pp = jnp.concatenate([_nn(pb, wp_ref[j]) for j in range(N_SH)], axis=1)
        gt_ref[...] = gate.astype(BF16)
        pp_ref[...] = pp.astype(BF16)
        o_ref[...] = xv + gate * pp

    row = bs((tm, D_MODEL), lambda i: (i, 0))
    return _call(
        body, "ple_fwd", (t // tm,),
        [bs((tm, FFN_H), lambda i: (i, 0)), _resident(wfo), row, bs((1, D_MODEL), lambda i: (0, 0)), _resident(wpg),
         bs((tm, BW), lambda i: (i, 0)), _resident(wpp)],
        [row, row, row, row, row],
        [_sds((t, D_MODEL), F32), _sds((t, D_MODEL), BF16), _sds((t, D_MODEL), BF16), _sds((t, D_MODEL), BF16),
         _sds((t, D_MODEL), F32)], after=after)(a, wfo, res, g, wpg, p_i, wpp)


def loss_head(x, g, tgt):
    t = x.shape[0]
    tm = min(512, t)

    def body(x_ref, g_ref, t_ref, l_ref, dx_ref, dg_ref):
        @pl.when(pl.program_id(0) == 0)
        def _():
            l_ref[...] = jnp.zeros_like(l_ref)
            dg_ref[...] = jnp.zeros_like(dg_ref)

        xv, gv = x_ref[...], g_ref[...]
        err = _rms_fwd(xv, gv) - t_ref[...]
        part = 0.5 * jnp.sum(jnp.mean(err * err, axis=-1, keepdims=True), axis=0, keepdims=True)
        l_ref[...] += jnp.broadcast_to(part, l_ref.shape)
        dx, dgr = _rms_bwd(err * (1.0 / D_MODEL), xv, gv)
        dx_ref[...] = dx
        dg_ref[...] += _colsum(dgr)

    row = bs((tm, D_MODEL), lambda i: (i, 0))
    vec = bs((1, D_MODEL), lambda i: (0, 0))
    return _call(body, "loss_head", (t // tm,), [row, vec, row],
                 [bs((1, 128), lambda i: (0, 0)), row, vec],
                 [_sds((1, 128), F32), _sds((t, D_MODEL), F32), _sds((1, D_MODEL), F32)])(x, g, tgt)


def tn_matmul(name, a, b, grid, a_spec, b_spec, out_spec, out_shape, split=0, split_cols=0, into=None, after=None):
    last = len(grid) - 1

    def body(a_ref, b_ref, *rest):
        o_ref = rest[-1]

        @pl.when(pl.program_id(last) == 0)
        def _():
            o_ref[...] = jnp.zeros_like(o_ref)

        res = _tn(a_ref[...].astype(BF16), b_ref[...].astype(BF16))
        if split_cols:
            cols = res.shape[1] // split_cols
            for s in range(split_cols):
                o_ref[s] += res[:, s * cols:(s + 1) * cols]
        elif split:
            rows = res.shape[0] // split
            for s in range(split):
                o_ref[s] += res[s * rows:(s + 1) * rows]
        else:
            o_ref[...] += res

    if into is None:
        return _call(body, name, grid, [a_spec, b_spec], out_spec, out_shape, after=after)(a, b)
    return _call(body, name, grid, [a_spec, b_spec, ANY], out_spec, out_shape, aliases={2: 0}, after=after)(a, b, into)


def _resident(w):
    zeros = (0,) * w.ndim
    return bs(w.shape, lambda i: zeros, pipeline_mode=pl.Buffered(1))


def norm_bwd(name, sources, dx_in, x, g):
    t = x.shape[0]
    tm = min(512, t)
    n_src = len(sources)

    def body(*refs):
        dxi_ref, x_ref, g_ref, dx_ref, dg_ref = refs[2 * n_src:]

        @pl.when(pl.program_id(0) == 0)
        def _():
            dg_ref[...] = jnp.zeros_like(dg_ref)

        dh = None
        for si in range(n_src):
            for av, wv in sources[si][3](refs[2 * si], refs[2 * si + 1]):
                part = _nt(av, wv)
                dh = part if dh is None else dh + part
        dx, dgr = _rms_bwd(dh, x_ref[...], g_ref[...])
        dx_ref[...] = dxi_ref[...] + dx
        dg_ref[...] += _colsum(dgr)

    in_specs, args = [], []
    for a, a_spec, w, _ in sources:
        in_specs += [a_spec(tm), _resident(w)]
        args += [a, w]
    row = bs((tm, D_MODEL), lambda i: (i, 0))
    vec = bs((1, D_MODEL), lambda i: (0, 0))
    return _call(body, name, (t // tm,), in_specs + [row, row, vec], [row, vec],
                 [_sds((t, D_MODEL), F32), _sds((1, D_MODEL), F32)])(*args, dx_in, x, g)


def ple_bwd(dx, gate, pp, wpg, x, g, h3, p_i, ga_shape, gb_shape, after=None):
    t = dx.shape[0]
    tm = min(512, t)

    def body(dx_ref, gt_ref, p_ref, w_ref, x_ref, g_ref, h_ref, pi_ref, o_ref, dg_ref, ga_ref, gb_ref):
        @pl.when(pl.program_id(0) == 0)
        def _():
            dg_ref[...] = jnp.zeros_like(dg_ref)
            ga_ref[...] = jnp.zeros_like(ga_ref)
            gb_ref[...] = jnp.zeros_like(gb_ref)

        d = dx_ref[...]
        gt = gt_ref[...].astype(F32)
        dpre = (d * p_ref[...].astype(F32) * gt * (1.0 - gt)).astype(BF16)
        dpp = (d * gt).astype(BF16)
        dxn, dgr = _rms_bwd(_nt(dpre, w_ref[...]), x_ref[...], g_ref[...])
        o_ref[...] = d + dxn
        dg_ref[...] += _colsum(dgr)
        dwg = _tn(h_ref[...], dpre)
        dwp = _tn(pi_ref[...].astype(BF16), dpp)
        for s in range(N_SH):
            ga_ref[s] += dwg[s * BW:(s + 1) * BW]
            gb_ref[s] += dwp[:, s * BW:(s + 1) * BW]

    row = bs((tm, D_MODEL), lambda i: (i, 0))
    vec = bs((1, D_MODEL), lambda i: (0, 0))
    return _call(body, "ple_bwd", (t // tm,),
                 [row, row, row, _resident(wpg), row, vec, row, bs((tm, BW), lambda i: (i, 0))],
                 [row, vec, bs((N_SH, BW, D_MODEL), lambda i: (0, 5, 0)), bs((N_SH, BW, BW), lambda i: (0, 4, 0))],
                 [_sds((t, D_MODEL), F32), _sds((1, D_MODEL), F32), ga_shape, gb_shape],
                 after=after)(dx, gate, pp, wpg, x, g, h3, p_i)


def ffn_bwd(dx, wfo, fgu, wfi, x, g, after=None):
    t = dx.shape[0]
    tm = min(256, t)
    ns = FFN_H // 2

    def body(dx_ref, wo_ref, f_ref, wi_ref, x_ref, g_ref, df_ref, o_ref, dg_ref):
        @pl.when(pl.program_id(0) == 0)
        def _():
            dg_ref[...] = jnp.zeros_like(dg_ref)

        d = dx_ref[...]
        dxb = d.astype(BF16)
        dh = None
        for j in range(2):
            cs = slice(j * ns, (j + 1) * ns)
            dact = _nt(dxb, wo_ref[cs, :])
            fg = f_ref[0, :, cs].astype(F32)
            fu = f_ref[1, :, cs].astype(F32)
            s = _sigmoid(fg)
            dfg = (dact * fu * (s * (1.0 + fg * (1.0 - s)))).astype(BF16)
            dfu = (dact * fg * s).astype(BF16)
            df_ref[0, :, cs] = dfg
            df_ref[1, :, cs] = dfu
            part = _nt(dfg, wi_ref[j]) + _nt(dfu, wi_ref[j + 2])
            dh = part if dh is None else dh + part
        dxn, dgr = _rms_bwd(dh, x_ref[...], g_ref[...])
        o_ref[...] = d + dxn
        dg_ref[...] += _colsum(dgr)

    blk = bs((2, tm, FFN_H), lambda i: (0, i, 0))
    row = bs((tm, D_MODEL), lambda i: (i, 0))
    vec = bs((1, D_MODEL), lambda i: (0, 0))
    return _call(body, "ffn_bwd", (t // tm,), [row, _resident(wfo), blk, _resident(wfi), row, vec], [blk, row, vec],
                 [_sds((2, t, FFN_H), BF16), _sds((t, D_MODEL), F32), _sds((1, D_MODEL), F32)],
                 after=after)(dx, wfo, fgu, wfi, x, g)


def merge_bwd(dx, wout, gates, ybr, wbr, ys, gb, merged, ga):
    t = dx.shape[0]
    tm = min(256, t)

    def body(dx_ref, w_ref, g_ref, b_ref, wb_ref, ya, yb, yc, yd, _, m_ref, __, dpre_ref, dy_ref, gb_ref, ga_ref):
        @pl.when(pl.program_id(0) == 0)
        def _():
            gb_ref[...] = jnp.zeros_like(gb_ref)
            ga_ref[...] = jnp.zeros_like(ga_ref)

        dxb = dx_ref[...].astype(BF16)
        dwo = _tn(m_ref[...], dxb)
        for s in range(N_SH):
            ga_ref[s] += dwo[s * BW:(s + 1) * BW]
        dm = _nt(dxb, w_ref[...])
        for k, y_ref in enumerate((ya, yb, yc, yd)):
            g = g_ref[k].astype(F32)
            dpre_ref[k] = (dm * b_ref[k].astype(F32) * g * (1.0 - g)).astype(BF16)
            dyb = (dm * g).astype(BF16)
            acc = None
            for s in range(N_SH):
                part = _nt(dyb[:, s * BW:(s + 1) * BW], wb_ref[s, k])
                acc = part if acc is None else acc + part
            dy_ref[k] = acc
            dwb = _tn(y_ref[...], dyb)
            for s in range(N_SH):
                gb_ref[s, k * BW:(k + 1) * BW, :] += dwb[:, s * BW:(s + 1) * BW]

    blk = bs((N_BR, tm, D_MODEL), lambda i: (0, i, 0))
    ysp = bs((tm, BW), lambda i: (i, 0))
    return _call(body, "merge_bwd", (t // tm,),
                 [bs((tm, D_MODEL), lambda i: (i, 0)), _resident(wout), blk, blk, _resident(wbr), ysp, ysp, ysp, ysp, ANY,
                  bs((tm, D_MODEL), lambda i: (i, 0)), ANY],
                 [blk, bs((N_BR, tm, BW), lambda i: (0, i, 0)), bs((N_SH, N_BR * BW, BW), lambda i: (0, 0, 0)),
                  bs((N_SH, BW, D_MODEL), lambda i: (0, 4, 0))],
                 [_sds((N_BR, t, D_MODEL), BF16), _sds((N_BR, t, BW), F32), _sds(gb.shape, F32), _sds(ga.shape, F32)],
                 aliases={9: 2, 11: 3})(dx, wout, gates, ybr, wbr, *ys, gb, merged, ga)


def conva_fwd(proj, wa):
    t = proj.shape[0]
    tm, halo = min(512, t), 8
    per = tm // halo

    def body(b_ref, c_ref, x_ref, ch_ref, xh_ref, w_ref, y_ref):
        zh = jnp.where(pl.program_id(0) > 0, ch_ref[...] * xh_ref[...], 0.0)
        zext = jnp.concatenate([zh, c_ref[...] * x_ref[...]], axis=0)
        y_ref[...] = (b_ref[...] * _causal_conv(zext, w_ref, CONVA_K, halo)).astype(BF16)

    col = lambda c: bs((tm, BW), lambda i: (i, c))
    hal = lambda c: bs((halo, BW), lambda i: (_prev_blk(i, per), c))
    return _call(body, "conva_fwd", (t // tm,),
                 [col(0), col(1), col(2), hal(1), hal(2), bs((CONVA_K, BW), lambda i: (0, 0))],
                 bs((tm, BW), lambda i: (i, 0)), _sds((t, BW), BF16))(proj, proj, proj, proj, proj, wa)


def conva_bwd(proj, dys, wa, after=None):
    t = proj.shape[0]
    tm, halo = min(512, t), 8
    per = tm // halo
    last = t // halo - 1
    nt = t // tm

    def body(b_ref, c_ref, x_ref, ch_ref, xh_ref, bn_ref, dy_ref, dyn_ref, w_ref, o_ref, dw_ref):
        i = pl.program_id(0)

        @pl.when(i == 0)
        def _():
            dw_ref[...] = jnp.zeros_like(dw_ref)

        zh = jnp.where(i > 0, ch_ref[...] * xh_ref[...], 0.0)
        cv, xv = c_ref[...], x_ref[...]
        zext = jnp.concatenate([zh, cv * xv], axis=0)
        dy = dy_ref[...]
        dconv = dy * b_ref[...]
        dcn = jnp.where(i < nt - 1, dyn_ref[...] * bn_ref[...], 0.0)
        dz = _anti_conv(jnp.concatenate([dconv, dcn], axis=0), w_ref, CONVA_K, tm)
        o_ref[:, :BW] = (dy * _causal_conv(zext, w_ref, CONVA_K, halo)).astype(BF16)
        o_ref[:, BW:2 * BW] = (dz * xv).astype(BF16)
        o_ref[:, 2 * BW:] = (dz * cv).astype(BF16)
        _conv_wgrad(dw_ref, dconv, zext, CONVA_K, halo)

    col = lambda c: bs((tm, BW), lambda i: (i, c))
    hal = lambda c: bs((halo, BW), lambda i: (_prev_blk(i, per), c))
    nxt = bs((halo, BW), lambda i: (_next_blk(i, per, last), 0))
    wsp = bs((CONVA_K, BW), lambda i: (0, 0))
    outs = _call(body, "conva_bwd", (t // tm,),
                 [col(0), col(1), col(2), hal(1), hal(2), nxt,
                  bs((None, tm, BW), lambda i: (0, i, 0)), bs((None, halo, BW), lambda i: (0, _next_blk(i, per, last), 0)), wsp],
                 [bs((tm, 3 * BW), lambda i: (i, 0)), wsp],
                 [_sds((t, N_IN), BF16), _sds((CONVA_K, BW), F32)], after=after)(
                     proj, proj, proj, proj, proj, proj, dys, dys, wa)
    return outs[0], outs[1]


def _head_masks():
    lane = lax.broadcasted_iota(jnp.int32, (1, BW), 1)
    return [(lane >= h * HEAD_D) & (lane < (h + 1) * HEAD_D) for h in range(HEADS)]


def _band_masks():
    qi = lax.broadcasted_iota(jnp.int32, (BLK, BLK), 0)
    ki = lax.broadcasted_iota(jnp.int32, (BLK, BLK), 1)
    return ki >= qi, ki <= qi


def attn_fwd_group(pv, d):
    rows = pv.shape[0]
    qb = min(512, rows)
    nb = qb // BLK
    scale = HEAD_D ** -0.5

    def body(q_ref, k_ref, v_ref, kh_ref, vh_ref, o_ref):
        n = pl.program_id(1)
        hm = _head_masks()
        m_prev, m_cur = _band_masks()
        for b in range(nb):
            rs = slice(b * BLK, (b + 1) * BLK)
            q = q_ref[rs, :]
            if b == 0:
                kp, vp = kh_ref[...], vh_ref[...]
                mp = m_prev & (n > 0)
            else:
                ps = slice((b - 1) * BLK, b * BLK)
                kp, vp = k_ref[ps, :], v_ref[ps, :]
                mp = m_prev
            qs = jnp.concatenate([jnp.where(hm[h], q, 0.0).astype(BF16) for h in range(HEADS)], axis=0)
            kcat = jnp.concatenate([kp, k_ref[rs, :]], axis=0)
            vcat = jnp.concatenate([vp, v_ref[rs, :]], axis=0)
            band = jnp.concatenate([mp, m_cur], axis=1)
            s = jnp.where(jnp.concatenate([band] * HEADS, axis=0), _nt(qs, kcat) * scale, NEG)
            m = jnp.max(s, axis=-1, keepdims=True)
            e = jnp.exp(s - m)
            l = jnp.sum(e, axis=-1, keepdims=True)
            of = _nn(e.astype(BF16), vcat) / l
            lse = m + jnp.log(l)
            o_acc = jnp.zeros((BLK, BW), F32)
            l_acc = jnp.zeros((BLK, BW), F32)
            for h in range(HEADS):
                hs = slice(h * BLK, (h + 1) * BLK)
                o_acc = jnp.where(hm[h], of[hs, :], o_acc)
                l_acc = jnp.where(hm[h], lse[hs, :], l_acc)
            o_ref[rs, :BW] = o_acc
            o_ref[rs, BW:] = l_acc

    per = qb // BLK
    main = lambda c: bs((qb, BW), lambda r, n: (n, r * 3 + c))
    hal = lambda c: bs((BLK, BW), lambda r, n: (_prev_blk(n, per), r * 3 + c))
    return _call(body, f"attn_fwd_d{d}", (d, rows // qb), [main(0), main(1), main(2), hal(1), hal(2)],
                 bs((qb, 2 * BW), lambda r, n: (n, r)), _sds((rows, d * 2 * BW), F32))(pv, pv, pv, pv, pv)


def attn_merge(ols):
    t = ols[0].shape[0]
    tm = min(512, t)
    width = 2 * BW

    def lse3(a, b, c):
        m = jnp.maximum(jnp.maximum(a, b), c)
        return m + jnp.log(jnp.exp(a - m) + jnp.exp(b - m) + jnp.exp(c - m))

    def body(g0, g1, g2, y_ref, o_ref, l_ref, scr, nat1, nat2):
        for src, nat, d in ((g1, nat1, DILATIONS[1]), (g2, nat2, DILATIONS[2])):
            for c in range(width // LANES):
                nat[:, c * LANES:(c + 1) * LANES] = _from_strided_view(src, scr, d, width, c)
        gs = [g0[...], nat1[...], nat2[...]]
        ls = [g[:, BW:] for g in gs]
        tot = lse3(*ls)
        o = (jnp.exp(ls[0] - tot) * gs[0][:, :BW] + jnp.exp(ls[1] - tot) * gs[1][:, :BW]
             + jnp.exp(ls[2] - tot) * gs[2][:, :BW])
        y_ref[...] = o.astype(BF16)
        o_ref[...] = o
        l_ref[...] = tot

    n = bs((tm, BW), lambda i: (i, 0))
    return _call(body, "attn_merge", (t // tm,),
                 [_view_spec(tm, 1, width), _view_spec(tm, DILATIONS[1], width), _view_spec(tm, DILATIONS[2], width)],
                 [n, n, n], [_sds((t, BW), BF16), _sds((t, BW), F32), _sds((t, BW), F32)],
                 scratch=[pltpu.VMEM((tm, LANES), F32), pltpu.VMEM((tm, width), F32), pltpu.VMEM((tm, width), F32)])(*ols)


def attn_delta(dys, o, lse):
    t = o.shape[0]
    tm = min(512, t)

    def body(d_ref, o_ref, l_ref, ld1, ld4, ld16, dy4, dy16, scr):
        hm = _head_masks()
        dy = d_ref[...]
        prod = dy * o_ref[...]
        delta = jnp.zeros_like(prod)
        for h in range(HEADS):
            delta = jnp.where(hm[h], jnp.sum(jnp.where(hm[h], prod, 0.0), axis=-1, keepdims=True), delta)
        ld = jnp.concatenate([l_ref[...], delta], axis=1)
        ld1[...] = ld
        for d, ld_v, dy_v in ((DILATIONS[1], ld4, dy4), (DILATIONS[2], ld16, dy16)):
            _to_strided_view(ld_v, lambda c: ld[:, c * LANES:(c + 1) * LANES], scr, d, 2 * BW)
            _to_strided_view(dy_v, lambda c: dy[:, c * LANES:(c + 1) * LANES], scr, d, BW)

    n = bs((tm, BW), lambda i: (i, 0))
    d4, d16 = DILATIONS[1], DILATIONS[2]
    outs = _call(body, "attn_delta", (t // tm,), [bs((None, tm, BW), lambda i: (1, i, 0)), n, n],
                 [_view_spec(tm, 1, 2 * BW), _view_spec(tm, d4, 2 * BW), _view_spec(tm, d16, 2 * BW),
                  _view_spec(tm, d4, BW), _view_spec(tm, d16, BW)],
                 [_sds((t, 2 * BW), F32), _sds((t // d4, d4 * 2 * BW), F32), _sds((t // d16, d16 * 2 * BW), F32),
                  _sds((t // d4, d4 * BW), F32), _sds((t // d16, d16 * BW), F32)],
                 scratch=[pltpu.VMEM((tm, LANES), F32)])(dys, o, lse)
    return outs[:3], outs[3:]


def attn_bwd_group(pv, dov, ldv, d):
    rows = pv.shape[0]
    qb = min(512, rows)
    nb = qb // BLK
    nsteps = rows // qb
    scale = HEAD_D ** -0.5

    def body(q_ref, qn_ref, k_ref, kh_ref, v_ref, vh_ref, do_ref, don_ref, ld_ref, ldn_ref, o_ref):
        n = pl.program_id(1)
        hm = _head_masks()
        m_prev, m_cur = _band_masks()
        has_prev, has_next = n > 0, n < nsteps - 1
        dq = [None] * nb
        dk = [jnp.zeros((BLK, BW), F32) for _ in range(nb)]
        dvv = [jnp.zeros((BLK, BW), F32) for _ in range(nb)]
        for qi in range(nb + 1):
            rs = slice(qi * BLK, (qi + 1) * BLK)
            ps = slice((qi - 1) * BLK, qi * BLK)
            if qi < nb:
                q, do, ldq = q_ref[rs, :], do_ref[rs, :], ld_ref[rs, :]
            else:
                q, do, ldq = qn_ref[...], don_ref[...], ldn_ref[...]
            kp, vp = (kh_ref[...], vh_ref[...]) if qi == 0 else (k_ref[ps, :], v_ref[ps, :])
            kc, vc = (k_ref[rs, :], v_ref[rs, :]) if qi < nb else (kp, vp)
            mp = m_prev & has_prev if qi == 0 else (m_prev & has_next if qi == nb else m_prev)
            mc = m_cur if qi < nb else jnp.zeros_like(m_cur)
            band = jnp.concatenate([jnp.concatenate([mp, mc], axis=1)] * HEADS, axis=0)
            qs = jnp.concatenate([jnp.where(hm[h], q, 0.0).astype(BF16) for h in range(HEADS)], axis=0)
            dos = jnp.concatenate([jnp.where(hm[h], do, 0.0).astype(BF16) for h in range(HEADS)], axis=0)
            kcat = jnp.concatenate([kp, kc], axis=0)
            vcat = jnp.concatenate([vp, vc], axis=0)
            col = lambda v, h: jnp.broadcast_to(jnp.max(jnp.where(hm[h], v, NEG), axis=-1, keepdims=True), (BLK, 2 * BLK))
            lcols = jnp.concatenate([col(ldq[:, :BW], h) for h in range(HEADS)], axis=0)
            dcols = jnp.concatenate([col(ldq[:, BW:], h) for h in range(HEADS)], axis=0)
            p = jnp.where(band, jnp.exp(_nt(qs, kcat) * scale - lcols), 0.0)
            ds = (p * (_nt(dos, vcat) - dcols) * scale).astype(BF16)
            if qi < nb:
                dqf = _nn(ds, kcat)
                acc_q = jnp.zeros((BLK, BW), F32)
                for h in range(HEADS):
                    acc_q = jnp.where(hm[h], dqf[h * BLK:(h + 1) * BLK, :], acc_q)
                dq[qi] = acc_q
            dkc = _tn(ds, qs)
            dvc = _tn(p.astype(BF16), dos)
            if qi >= 1:
                dk[qi - 1] = dk[qi - 1] + dkc[:BLK]
                dvv[qi - 1] = dvv[qi - 1] + dvc[:BLK]
            if qi < nb:
                dk[qi] = dk[qi] + dkc[BLK:]
                dvv[qi] = dvv[qi] + dvc[BLK:]
        for b in range(nb):
            rs = slice(b * BLK, (b + 1) * BLK)
            for c, val in enumerate((dq[b], dk[b], dvv[b])):
                cs = slice(c * BW, (c + 1) * BW)
                o_ref[rs, cs] = val

    per = qb // BLK
    last = rows // BLK - 1
    main = lambda c: bs((qb, BW), lambda r, n: (n, r * 3 + c))
    prv = lambda c: bs((BLK, BW), lambda r, n: (_prev_blk(n, per), r * 3 + c))
    nxt = lambda c: bs((BLK, BW), lambda r, n: (_next_blk(n, per, last), r * 3 + c))
    accs = bs((qb, 3 * BW), lambda r, n: (n, r))
    in_specs = [main(0), nxt(0), main(1), prv(1), main(2), prv(2),
                bs((qb, BW), lambda r, n: (n, r)), bs((BLK, BW), lambda r, n: (_next_blk(n, per, last), r)),
                bs((qb, 2 * BW), lambda r, n: (n, r)), bs((BLK, 2 * BW), lambda r, n: (_next_blk(n, per, last), r))]
    args = [pv, pv, pv, pv, pv, pv, dov, dov, ldv, ldv]
    return _call(body, f"attn_bwd_d{d}", (d, nsteps), in_specs, accs, _sds((rows, d * 3 * BW), F32))(*args)


def attn_bwd_finish(parts, into):
    t = parts[0].shape[0]
    tm = min(512, t)
    width = 3 * BW

    def body(g0, g1, g2, _, o_ref, scr):
        for c in range(width // LANES):
            cs = slice(c * LANES, (c + 1) * LANES)
            acc = g0[:, cs]
            acc = acc + _from_strided_view(g1, scr, DILATIONS[1], width, c)
            acc = acc + _from_strided_view(g2, scr, DILATIONS[2], width, c)
            o_ref[:, cs] = acc.astype(BF16)

    return _call(body, "attn_bwd_finish", (t // tm,),
                 [_view_spec(tm, 1, width), _view_spec(tm, DILATIONS[1], width), _view_spec(tm, DILATIONS[2], width), ANY],
                 bs((tm, width), lambda i: (i, 1)), _sds(into.shape, BF16),
                 scratch=[pltpu.VMEM((tm, LANES), F32)], aliases={3: 0})(*parts, into)


def _group_masks():
    lane = lax.broadcasted_iota(jnp.int32, (1, BW), 1)
    return [(lane >= g * HEAD_D) & (lane < (g + 1) * HEAD_D) for g in range(4)]


def sgu_fwd(proj, ln_g, ln_b, w_tril, b_full):
    t = proj.shape[0]
    tm = min(512, t)

    def body(u_ref, v_ref, g_ref, b_ref, w_ref, bf_ref, y_ref):
        gm = _group_masks()
        xhat, _ = _ln_hat(v_ref[...])
        vb = (xhat * g_ref[...] + b_ref[...]).astype(BF16)
        for c in range(tm // BLK):
            rs = slice(c * BLK, (c + 1) * BLK)
            vc = vb[rs, :]
            mixed = bf_ref[...]
            for g in range(4):
                mixed = mixed + jnp.where(gm[g], _nn(w_ref[g], vc), 0.0)
            y_ref[rs, :] = (u_ref[rs, :] * mixed).astype(BF16)

    vec = bs((1, BW), lambda i: (0, 0))
    return _call(body, "sgu_fwd", (t // tm,),
                 [bs((tm, BW), lambda i: (i, 6)), bs((tm, BW), lambda i: (i, 7)), vec, vec,
                  bs((4, BLK, BLK), lambda i: (0, 0, 0)), bs((BLK, BW), lambda i: (0, 0))],
                 bs((tm, BW), lambda i: (i, 0)), _sds((t, BW), BF16))(proj, proj, ln_g, ln_b, w_tril, b_full)


def sgu_bwd(proj, dys, ln_g, ln_b, w_tril, b_full, into):
    t = proj.shape[0]
    tm = min(512, t)

    def body(u_ref, v_ref, dy_ref, g_ref, b_ref, w_ref, bf_ref, _, o_ref, dw_ref, dbf_ref, dg_ref, db_ref, dvl_ref):
        @pl.when(pl.program_id(0) == 0)
        def _():
            dw_ref[...] = jnp.zeros_like(dw_ref)
            dbf_ref[...] = jnp.zeros_like(dbf_ref)
            dg_ref[...] = jnp.zeros_like(dg_ref)
            db_ref[...] = jnp.zeros_like(db_ref)

        gm = _group_masks()
        xhat, r = _ln_hat(v_ref[...])
        gv = g_ref[...]
        vb = (xhat * gv + b_ref[...]).astype(BF16)
        for c in range(tm // BLK):
            rs = slice(c * BLK, (c + 1) * BLK)
            vc = vb[rs, :]
            dy = dy_ref[rs, :]
            mixed = bf_ref[...]
            for g in range(4):
                mixed = mixed + jnp.where(gm[g], _nn(w_ref[g], vc), 0.0)
            o_ref[rs, :BW] = (dy * mixed).astype(BF16)
            dm = dy * u_ref[rs, :]
            dbf_ref[...] += dm
            dvl = jnp.zeros((BLK, BW), F32)
            for g in range(4):
                dmg = jnp.where(gm[g], dm, 0.0).astype(BF16)
                dw_ref[g] += _nt(dmg, vc)
                dvl = dvl + _tn(w_ref[g], dmg)
            dvl_ref[rs, :] = dvl
        dvl = dvl_ref[...]
        o_ref[:, BW:] = _ln_bwd(dvl, xhat, r, gv).astype(BF16)
        dg_ref[...] += _colsum(dvl * xhat)
        db_ref[...] += _colsum(dvl)

    vec = bs((1, BW), lambda i: (0, 0))
    row = bs((tm, BW), lambda i: (i, 0))
    wsp = bs((4, BLK, BLK), lambda i: (0, 0, 0))
    bfs = bs((BLK, BW), lambda i: (0, 0))
    return _call(body, "sgu_bwd", (t // tm,),
                 [bs((tm, BW), lambda i: (i, 6)), bs((tm, BW), lambda i: (i, 7)), bs((None, tm, BW), lambda i: (2, i, 0)),
                  vec, vec, wsp, bfs, ANY],
                 [bs((tm, 2 * BW), lambda i: (i, 3)), wsp, bfs, vec, vec],
                 [_sds(into.shape, BF16), _sds((4, BLK, BLK), F32), _sds((BLK, BW), F32),
                  _sds((1, BW), F32), _sds((1, BW), F32)],
                 scratch=[pltpu.VMEM((tm, BW), F32)], aliases={7: 0})(proj, proj, dys, ln_g, ln_b, w_tril, b_full, into)


CONF_HALO = 32


def conf_fwd(proj, dw, ln_g, ln_b, after=None):
    t = proj.shape[0]
    tm, halo = min(512, t), CONF_HALO
    per = tm // halo

    def body(v_ref, gt_ref, vh_ref, gh_ref, w_ref, g_ref, b_ref, y_ref, z_ref):
        yh = jnp.where(pl.program_id(0) > 0, vh_ref[...] * _sigmoid(gh_ref[...]), 0.0)
        yext = jnp.concatenate([yh, v_ref[...] * _sigmoid(gt_ref[...])], axis=0)
        z = _causal_conv(yext, w_ref, CONF_K, halo)
        z_ref[...] = z
        xhat, _ = _ln_hat(z)
        ln = xhat * g_ref[...] + b_ref[...]
        y_ref[...] = (ln * _sigmoid(ln)).astype(BF16)

    vec = bs((1, BW), lambda i: (0, 0))
    col = lambda c: bs((tm, BW), lambda i: (i, c))
    hal = lambda c: bs((halo, BW), lambda i: (_prev_blk(i, per), c))
    row = bs((tm, BW), lambda i: (i, 0))
    return _call(body, "conf_fwd", (t // tm,),
                 [col(8), col(9), hal(8), hal(9), bs((CONF_K, BW), lambda i: (0, 0)), vec, vec],
                 [row, row], [_sds((t, BW), BF16), _sds((t, BW), F32)], after=after)(
                     proj, proj, proj, proj, dw, ln_g, ln_b)


def conf_bwd_ln(z, dys, ln_g, ln_b):
    t = z.shape[0]
    tm = min(1024, t)

    def body(z_ref, dy_ref, g_ref, b_ref, dz_ref, dg_ref, db_ref):
        @pl.when(pl.program_id(0) == 0)
        def _():
            dg_ref[...] = jnp.zeros_like(dg_ref)
            db_ref[...] = jnp.zeros_like(db_ref)

        gv = g_ref[...]
        xhat, r = _ln_hat(z_ref[...])
        ln = xhat * gv + b_ref[...]
        s = _sigmoid(ln)
        dln = dy_ref[...] * (s * (1.0 + ln * (1.0 - s)))
        dz_ref[...] = _ln_bwd(dln, xhat, r, gv)
        dg_ref[...] += _colsum(dln * xhat)
        db_ref[...] += _colsum(dln)

    vec = bs((1, BW), lambda i: (0, 0))
    row = bs((tm, BW), lambda i: (i, 0))
    return _call(body, "conf_bwd_ln", (t // tm,), [row, bs((None, tm, BW), lambda i: (3, i, 0)), vec, vec],
                 [row, vec, vec], [_sds((t, BW), F32), _sds((1, BW), F32), _sds((1, BW), F32)])(z, dys, ln_g, ln_b)


def conf_bwd_conv(proj, dz, dw, into):
    t = proj.shape[0]
    tm, halo = min(512, t), CONF_HALO
    per = tm // halo
    last = t // halo - 1
    nt = t // tm

    def body(v_ref, gt_ref, vh_ref, gh_ref, dz_ref, dzn_ref, w_ref, _, o_ref, dw_ref):
        i = pl.program_id(0)

        @pl.when(i == 0)
        def _():
            dw_ref[...] = jnp.zeros_like(dw_ref)

        val = v_ref[...]
        sg = _sigmoid(gt_ref[...])
        yh = jnp.where(i > 0, vh_ref[...] * _sigmoid(gh_ref[...]), 0.0)
        yext = jnp.concatenate([yh, val * sg], axis=0)
        dz = dz_ref[...]
        dzn = jnp.where(i < nt - 1, dzn_ref[...], 0.0)
        dy0 = _anti_conv(jnp.concatenate([dz, dzn], axis=0), w_ref, CONF_K, tm)
        o_ref[:, :BW] = (dy0 * sg).astype(BF16)
        o_ref[:, BW:] = (dy0 * val * sg * (1.0 - sg)).astype(BF16)
        _conv_wgrad(dw_ref, dz, yext, CONF_K, halo)

    col = lambda c: bs((tm, BW), lambda i: (i, c))
    hal = lambda c: bs((halo, BW), lambda i: (_prev_blk(i, per), c))
    row = bs((tm, BW), lambda i: (i, 0))
    wsp = bs((CONF_K, BW), lambda i: (0, 0))
    return _call(body, "conf_bwd_conv", (t // tm,),
                 [col(8), col(9), hal(8), hal(9), row, bs((halo, BW), lambda i: (_next_blk(i, per, last), 0)), wsp, ANY],
                 [bs((tm, 2 * BW), lambda i: (i, 4)), wsp], [_sds(into.shape, BF16), _sds((CONF_K, BW), F32)],
                 aliases={7: 0})(proj, proj, proj, proj, dz, dz, dw, into)


def _place():
    return lax.axis_index("x"), lax.axis_index("y"), lax.axis_index("c")


HBM_SPEC = pl.BlockSpec(memory_space=pltpu.HBM)
SEM_SPEC = pl.BlockSpec(memory_space=pltpu.SEMAPHORE)
EFFECT = pltpu.SideEffectType.DATAFLOW_SIDE_EFFECTING


class SplitExchange:
    def __init__(self, name, bufs, plan, n_copies):
        self.name, self.bufs, self.plan, self.n = name, list(bufs), plan, n_copies

    def start(self, after):
        nb, n, plan = len(self.bufs), self.n, self.plan

        def body(*refs):
            send, recv, token = refs[nb + 1], refs[nb + 2], refs[-1]
            for k, (src, dst, _, dev) in enumerate(plan(refs[:nb])):
                pltpu.make_async_remote_copy(src_ref=src, dst_ref=dst, send_sem=send.at[k], recv_sem=recv.at[k],
                                             device_id=dev, device_id_type=MESH).start()
            token[...] = jnp.zeros_like(token)

        outs = pl.pallas_call(
            body, name=self.name + "_start",
            out_shape=(pltpu.SemaphoreType.DMA((n,)), pltpu.SemaphoreType.DMA((n,)),
                       *[pltpu.HBM(b.shape, b.dtype) for b in self.bufs], _sds((8, 128), F32)),
            in_specs=[HBM_SPEC] * nb + [ANY],
            out_specs=(SEM_SPEC, SEM_SPEC, *[HBM_SPEC] * nb, pl.BlockSpec(memory_space=pltpu.VMEM)),
            input_output_aliases={i: 2 + i for i in range(nb)},
            compiler_params=pltpu.CompilerParams(has_side_effects=EFFECT))(
                *[pltpu.with_memory_space_constraint(b, pltpu.HBM) for b in self.bufs], after)
        self.send, self.recv, self.bufs = outs[0], outs[1], list(outs[2:2 + nb])
        return outs[-1]

    def wait(self, after):
        nb, plan = len(self.bufs), self.plan
        after = list(after) if isinstance(after, (list, tuple)) else [after]

        def body(*refs):
            send, recv = refs[nb], refs[nb + 1]
            for k, (src, _, land, dev) in enumerate(plan(refs[:nb])):
                cp = pltpu.make_async_remote_copy(src_ref=src, dst_ref=land, send_sem=send.at[k], recv_sem=recv.at[k],
                                                  device_id=dev, device_id_type=MESH)
                cp.wait_send()
                cp.wait_recv()

        outs = pl.pallas_call(
            body, name=self.name + "_wait", out_shape=tuple(pltpu.HBM(b.shape, b.dtype) for b in self.bufs),
            in_specs=[HBM_SPEC] * nb + [SEM_SPEC, SEM_SPEC] + [ANY] * len(after), out_specs=[HBM_SPEC] * nb,
            input_output_aliases={i: i for i in range(nb)},
            compiler_params=pltpu.CompilerParams(has_side_effects=EFFECT))(*self.bufs, self.send, self.recv, *after)
        return list(outs)


def _chips_of(x, y):
    return [(1 - x, y), (x, 1 - y), (1 - x, 1 - y)]


def allgather_ici_plan(shapes):
    def plan(refs):
        x, y, c = _place()
        out = []
        for a, ref in enumerate(refs):
            hl = shapes[a][1] // 2
            half = pl.ds(c * hl, hl)
            for cx, cy in _chips_of(x, y):
                mine = ref.at[2 * x + y, half]
                out.append((mine, mine, ref.at[2 * cx + cy, half], (cx, cy, c)))
        return out
    return plan


def allgather_d2d_plan(shapes):
    def plan(refs):
        x, y, c = _place()
        out = []
        for a, ref in enumerate(refs):
            hl = shapes[a][1] // 2
            for cx, cy in _chips_of(x, y):
                got = ref.at[2 * cx + cy, pl.ds(c * hl, hl)]
                out.append((got, got, ref.at[2 * cx + cy, pl.ds((1 - c) * hl, hl)], (x, y, 1 - c)))
        return out
    return plan


def gather8(v):
    rows, cols = v.shape

    def body(v_ref, land, send, recv, lsem):
        x, y, c = _place()
        me = 4 * x + 2 * y + c
        mine = pltpu.make_async_copy(v_ref, land.at[me], lsem)
        mine.start()
        sent = []
        for j in range(1, 8):
            fx, fy, fc = (j >> 2) & 1, (j >> 1) & 1, j & 1
            tgt = (1 - x if fx else x, 1 - y if fy else y, 1 - c if fc else c)
            cp = pltpu.make_async_remote_copy(src_ref=v_ref, dst_ref=land.at[me], send_sem=send.at[j - 1],
                                              recv_sem=recv.at[j - 1], device_id=tgt, device_id_type=MESH)
            cp.start()
            sent.append(cp)
        for j in range(1, 8):
            fx, fy, fc = (j >> 2) & 1, (j >> 1) & 1, j & 1
            peer = 4 * (1 - x if fx else x) + 2 * (1 - y if fy else y) + (1 - c if fc else c)
            pltpu.make_async_remote_copy(src_ref=v_ref, dst_ref=land.at[peer], send_sem=send.at[j - 1],
                                         recv_sem=recv.at[j - 1], device_id=(x, y, c), device_id_type=MESH).wait_recv()
        for cp in sent:
            cp.wait_send()
        mine.wait()

    vm = pl.BlockSpec(memory_space=pltpu.VMEM)
    return pl.pallas_call(
        body, name="allgather8", in_specs=[vm], out_specs=vm, out_shape=_sds((8, rows, cols), F32),
        scratch_shapes=[pltpu.SemaphoreType.DMA((7,)), pltpu.SemaphoreType.DMA((7,)), pltpu.SemaphoreType.DMA],
        compiler_params=pltpu.CompilerParams(has_side_effects=True, vmem_limit_bytes=VMEM_LIMIT))(v)


def _row_tile(rows, cols):
    best = 16
    for t in range(16, rows + 1, 16):
        if rows % t == 0 and t * cols * 4 <= 2 * 1024 * 1024:
            best = t
    return best


def _rs_add_sibling(scal, g, ra, hr):
    cols = g.shape[2]
    tr = _row_tile(hr, cols)
    nr = hr // tr

    def body(s_ref, g_ref, r_ref, p32_ref, p16_ref):
        v = g_ref[...] + r_ref[...]
        p16_ref[...] = v.astype(BF16)

        @pl.when(pl.program_id(1) == s_ref[0])
        def _():
            p32_ref[...] = v

    blk = lambda f: bs((None, tr, cols), f)
    own = blk(lambda i, s, sr: (s, i, 0))
    spec = pltpu.PrefetchScalarGridSpec(num_scalar_prefetch=1, grid=(nr, N_SH),
                                        in_specs=[blk(lambda i, s, sr: (s, sr[1] * nr + i, 0)), own],
                                        out_specs=[bs((tr, cols), lambda i, s, sr: (i, 0)), own])
    return pl.pallas_call(body, name="rs_add_sibling", grid_spec=spec,
                          out_shape=[_sds((hr, cols), F32), _sds((N_SH, hr, cols), BF16)],
                          compiler_params=pltpu.CompilerParams(dimension_semantics=("arbitrary",) * 2,
                                                               vmem_limit_bytes=VMEM_LIMIT))(scal, g, ra)


def _rs_add_chips(scal, p32, rb, hr):
    cols = p32.shape[1]
    tr = _row_tile(hr, cols)
    nr = hr // tr

    def body(s_ref, p_ref, r0, r1, r2, o_ref):
        o_ref[...] = ((p_ref[...] + r0[...].astype(F32)) + r1[...].astype(F32)) + r2[...].astype(F32)

    blk = lambda f: bs((None, tr, cols), f)
    spec = pltpu.PrefetchScalarGridSpec(
        num_scalar_prefetch=1, grid=(nr,),
        in_specs=[bs((tr, cols), lambda i, sr: (i, 0))] + [blk(functools.partial(lambda i, sr, j: (j, i, 0), j=j))
                                                            for j in range(3)],
        out_specs=blk(lambda i, sr: (sr[1], i, 0)))
    return pl.pallas_call(body, name="rs_add_chips", grid_spec=spec, out_shape=_sds((2, hr, cols), F32),
                          compiler_params=pltpu.CompilerParams(dimension_semantics=("arbitrary",),
                                                               vmem_limit_bytes=VMEM_LIMIT))(scal, p32, rb, rb, rb)


class SplitReduceScatter:
    def __init__(self, gs):
        x, y, c = _place()
        self.scal = jnp.stack([2 * x + y, c]).astype(jnp.int32)
        self.gs, self.n = list(gs), len(gs)
        self.hrs = [g.shape[1] // 2 for g in gs]

    def swap_start(self, after):
        n, hrs = self.n, self.hrs

        def plan(refs):
            x, y, c = _place()
            return [(refs[a].at[:, pl.ds((1 - c) * hrs[a], hrs[a])], refs[n + a], refs[n + a], (x, y, 1 - c))
                    for a in range(n)]

        lands = [lax.empty((N_SH, hrs[a], g.shape[2]), F32) for a, g in enumerate(self.gs)]
        self.ex = SplitExchange("rs_swap_halves", self.gs + lands, plan, n)
        return self.ex.start(after)

    def swap_wait_send_start(self, after):
        n, hrs = self.n, self.hrs
        bufs = self.ex.wait(after)
        parts = [_rs_add_sibling(self.scal, bufs[a], bufs[n + a], hrs[a]) for a in range(n)]
        self.p32 = [p[0] for p in parts]

        def plan(refs):
            x, y, c = _place()
            return [(refs[a].at[2 * cx + cy], refs[n + a].at[j], refs[n + a].at[j], (cx, cy, c))
                    for a in range(n) for j, (cx, cy) in enumerate(_chips_of(x, y))]

        lands = [lax.empty((3, hrs[a], g.shape[2]), BF16) for a, g in enumerate(self.gs)]
        self.ex = SplitExchange("rs_send_partials", [p[1] for p in parts] + lands, plan, 3 * n)
        return self.ex.start(parts[-1][1])

    def send_wait_share_start(self, after):
        n, hrs = self.n, self.hrs
        bufs = self.ex.wait(after)
        fins = [_rs_add_chips(self.scal, self.p32[a], bufs[n + a], hrs[a]) for a in range(n)]

        def plan(refs):
            x, y, c = _place()
            return [(refs[a].at[c], refs[a].at[c], refs[a].at[1 - c], (x, y, 1 - c)) for a in range(n)]

        self.ex = SplitExchange("rs_share_halves", fins, plan, n)
        return self.ex.start(fins[-1])

    def share_wait(self, after):
        fulls = self.ex.wait(after)
        return [f.reshape(2 * hr, f.shape[2]) for f, hr in zip(fulls, self.hrs)]


def adamw(w, g, m, v):
    shape = w.shape
    cols = shape[-1]
    rows = math.prod(shape[:-1]) if len(shape) > 1 else 1
    tr = 256 if rows % 256 == 0 and rows > 256 else rows
    c1 = 1.0 - ADAM_B1 ** ADAM_STEP
    c2 = 1.0 - ADAM_B2 ** ADAM_STEP

    def body(w_ref, g_ref, m_ref, v_ref, d_ref, nm_ref, nv_ref):
        gv = g_ref[...]
        nm = ADAM_B1 * m_ref[...] + (1.0 - ADAM_B1) * gv
        nv = ADAM_B2 * v_ref[...] + (1.0 - ADAM_B2) * (gv * gv)
        nm_ref[...] = nm
        nv_ref[...] = nv
        d_ref[...] = -ADAM_LR * ((nm / c1) / (jnp.sqrt(nv / c2) + ADAM_EPS) + ADAM_WD * w_ref[...])

    row = bs((tr, cols), lambda i: (i, 0))
    outs = _call(body, "adamw", (rows // tr,), [row] * 4, [row] * 3, [_sds((rows, cols), F32)] * 3)(
        *[a.reshape(rows, cols) for a in (w, g, m, v)])
    return [o.reshape(shape) for o in outs]


def adamw_layers(w, gs, m, v, lo, into=None, after=None):
    shape = w.shape
    cols = shape[-1]
    rl = math.prod(shape[1:-1])
    tr = max(t_ for t_ in range(8, rl + 1, 8) if rl % t_ == 0 and t_ * cols * 4 <= 1024 * 1024)
    nb = rl // tr
    n = len(gs)
    c1 = 1.0 - ADAM_B1 ** ADAM_STEP
    c2 = 1.0 - ADAM_B2 ** ADAM_STEP

    def body(*refs):
        w_ref, m_ref, v_ref = refs[:3]
        g_refs = refs[3:3 + n]
        d_ref, nm_ref, nv_ref, go_ref = refs[-4:]
        layer = pl.program_id(0) // nb
        for k in range(n):
            @pl.when(layer == k)
            def _(k=k):
                gv = g_refs[k][...]
                nm = ADAM_B1 * m_ref[...] + (1.0 - ADAM_B1) * gv
                nv = ADAM_B2 * v_ref[...] + (1.0 - ADAM_B2) * (gv * gv)
                nm_ref[...] = nm
                nv_ref[...] = nv
                go_ref[...] = gv
                d_ref[...] = -ADAM_LR * ((nm / c1) / (jnp.sqrt(nv / c2) + ADAM_EPS) + ADAM_WD * w_ref[...])

    row = bs((tr, cols), lambda b: (lo * nb + b, 0))
    g_specs = [bs((tr, cols), functools.partial(lambda b, k: (jnp.clip(b - k * nb, 0, nb - 1), 0), k=k)) for k in range(n)]
    flat = lambda a: a.reshape(-1, cols)
    in_specs = [row] * 3 + g_specs
    args = [flat(w), flat(m), flat(v)] + [flat(g) for g in gs]
    aliases = None
    if into is not None:
        aliases = {len(in_specs) + k: k for k in range(4)}
        in_specs = in_specs + [ANY] * 4
        args = args + [flat(a) for a in into]
    outs = _call(body, "adamw_layers", (n * nb,), in_specs, [row] * 4, [_sds((shape[0] * rl, cols), F32)] * 4,
                 aliases=aliases, after=after)(*args)
    return [o.reshape(shape) for o in outs]


def allreduce8_split(vec):
    rows, cols = vec.shape

    def plan(refs):
        x, y, c = _place()
        me = 4 * x + 2 * y + c
        out = []
        for j in range(1, 8):
            px, py, pc = (1 - x if j & 4 else x), (1 - y if j & 2 else y), (1 - c if j & 1 else c)
            out.append((refs[0], refs[1].at[me], refs[1].at[4 * px + 2 * py + pc], (px, py, pc)))
        return out

    ex = SplitExchange("allreduce8", [vec, lax.empty((8, rows, cols), F32)], plan, 7)

    def finish(after):
        v, land = ex.wait(after)
        x, y, c = _place()
        me = jnp.reshape(4 * x + 2 * y + c, (1,)).astype(jnp.int32)

        def body(me_ref, v_ref, l_ref, o_ref):
            o_ref[...] = jnp.zeros_like(o_ref)
            for k in range(8):
                @pl.when(me_ref[0] == k)
                def _():
                    o_ref[...] += v_ref[...]

                @pl.when(me_ref[0] != k)
                def _(k=k):
                    o_ref[...] += l_ref[k]

        spec = pltpu.PrefetchScalarGridSpec(
            num_scalar_prefetch=1, grid=(1,),
            in_specs=[bs((rows, cols), lambda i, mr: (0, 0)), bs((8, rows, cols), lambda i, mr: (0, 0, 0))],
            out_specs=bs((rows, cols), lambda i, mr: (0, 0)))
        return pl.pallas_call(body, name="allreduce8_sum", grid_spec=spec, out_shape=_sds((rows, cols), F32),
                              compiler_params=pltpu.CompilerParams(dimension_semantics=("arbitrary",),
                                                                   vmem_limit_bytes=VMEM_LIMIT))(me, v, land)

    return ex, finish


class Hooks:
    def __init__(self):
        self.steps = {}

    def add(self, point, fn):
        self.steps.setdefault(point, []).append(fn)

    def run(self, point, arr, env=None):
        tok = None
        for fn in self.steps.get(point, ()):
            got = fn(arr if tok is None else tok, env)
            tok = tok if got is None else got
        return tok


def layer_fwd(x, p_i, w, hooks):
    h, proj, *qkv = norm_in_proj(x, w["g_mix"], w["win"], after=hooks.run("start", x))
    ya = conva_fwd(proj, w["conv_a"])
    yb, o32, lse = attn_merge([attn_fwd_group(pv, d) for pv, d in zip(qkv, DILATIONS)])
    yc = sgu_fwd(proj, w["sgu_ln_g"], w["sgu_ln_b"], w["sgu_wt"], w["sgu_bf"])
    yd, z = conf_fwd(proj, w["conf_dw"], w["conf_ln_g"], w["conf_ln_b"], after=hooks.run("pre_conf", [ya, yb, yc]))
    ys = (ya, yb, yc, yd)
    tok = hooks.run("pre_merge", yd)
    merged, gates, ybr = merge_fwd(h, ys, w["wg"], w["wbr"], after=tok)
    x1, h2, fgu, act = ffn_in(merged, w["wout"], x, w["g_ffn"], w["wfi"], after=hooks.run("post_merge", merged))
    x2, h3, gate, pp, x3 = ple_fwd(act, w["wfo"], x1, w["g_ple"], w["wpg"], p_i, w["wpp"],
                                   after=hooks.run("post_ffn_in", act))
    saved = dict(x=x, h=h, proj=proj, qkv=qkv, ys=ys, o32=o32, lse=lse, z=z, merged=merged, gates=gates, ybr=ybr, x1=x1,
                 h2=h2, fgu=fgu, act=act, x2=x2, h3=h3, gate=gate, pp=pp)
    return x3, saved


def layer_bwd(dx3, p_i, w, s, hooks):
    t = dx3.shape[0]
    tr = min(1024, t)
    nr = t // tr
    ns_fi = FFN_H // 2
    small = {}

    ga_shape, gb_shape = _sds((N_SH, 6 * BW, D_MODEL), F32), _sds((N_SH, 5 * BW, BW), F32)
    ga_blk = lambda idx: bs((N_SH, BW, D_MODEL), idx)
    dx2, small["g_ple"], ga, gb = ple_bwd(dx3, s["gate"], s["pp"], w["wpg"], s["x2"], w["g_ple"], s["h3"], p_i,
                                          ga_shape, gb_shape, after=hooks.run("start", dx3))

    df, dx1, small["g_ffn"] = ffn_bwd(dx2, w["wfo"], s["fgu"], w["wfi"], s["x1"], w["g_ffn"],
                                      after=hooks.run("pre_ffn", dx2))
    gfo = tn_matmul("dw_ffn_out", s["act"], dx2, (2, nr), bs((tr, ns_fi), lambda j, r: (r, j)),
                    bs((tr, D_MODEL), lambda j, r: (r, 0)), bs((2, FFN_H // N_SH, D_MODEL), lambda j, r: (j, 0, 0)),
                    _sds((N_SH, FFN_H // N_SH, D_MODEL), F32), split=2)
    gfi = tn_matmul("dw_ffn_in", s["h2"], df, (N_SH, nr), bs((tr, D_MODEL), lambda j, r: (r, 0)),
                    bs((None, tr, ns_fi), lambda j, r: (j // 2, r, j % 2)),
                    bs((None, D_MODEL, ns_fi), lambda j, r: (j, 0, 0)), _sds((N_SH, D_MODEL, ns_fi), F32))

    dpre_m, dys, gb, ga = merge_bwd(dx1, w["wout"], s["gates"], s["ybr"], w["wbr"], s["ys"], gb, s["merged"], ga)
    ga = tn_matmul("dw_merge_gate", s["h"], dpre_m, (N_BR, nr), bs((tr, D_MODEL), lambda k, r: (r, 0)),
                   bs((None, tr, D_MODEL), lambda k, r: (k, r, 0)), ga_blk(lambda k, r: (0, k, 0)), ga_shape,
                   split=N_SH, into=ga, after=hooks.run("pre_dw_out", dys, dict(gfo=gfo, gfi=gfi)))

    dproj, small["conv_a"] = conva_bwd(s["proj"], dys, w["conv_a"], after=hooks.run("pre_conva", gb))
    lds, dy_views = attn_delta(dys, s["o32"], s["lse"])
    dy_views = [dys[1]] + list(dy_views)
    dproj = attn_bwd_finish([attn_bwd_group(pv, dov, ldv, d)
                             for pv, dov, ldv, d in zip(s["qkv"], dy_views, lds, DILATIONS)], dproj)
    dproj, d_sw, d_sbf, small["sgu_ln_g"], small["sgu_ln_b"] = sgu_bwd(
        s["proj"], dys, w["sgu_ln_g"], w["sgu_ln_b"], w["sgu_wt"], w["sgu_bf"], dproj)
    small["sgu_w"] = jnp.where(jnp.tril(jnp.ones((BLK, BLK), bool))[None], d_sw, 0.0)
    small["sgu_b"] = jnp.sum(d_sbf.reshape(BLK, 4, HEAD_D), axis=-1).T
    dz, small["conf_ln_g"], small["conf_ln_b"] = conf_bwd_ln(s["z"], dys, w["conf_ln_g"], w["conf_ln_b"])
    dproj, small["conf_dw"] = conf_bwd_conv(s["proj"], dz, w["conf_dw"], dproj)

    ns_in = N_IN // N_SH
    gin = tn_matmul("dw_in", s["h"], dproj, (N_SH, nr), bs((tr, D_MODEL), lambda j, r: (r, 0)),
                    bs((tr, ns_in), lambda j, r: (r, j)), bs((None, D_MODEL, ns_in), lambda j, r: (j, 0, 0)),
                    _sds((N_SH, D_MODEL, ns_in), F32), after=hooks.run("pre_dw_in", dproj))
    hooks.run("end", gin)
    big = [ga, gfo, gb, gin, gfi]
    dx, small["g_mix"] = norm_bwd(
        "mix_norm_bwd",
        [(dpre_m, lambda tm: bs((N_BR, tm, D_MODEL), lambda i: (0, i, 0)), w["wg"],
          lambda a, wr: [(a[k], wr[k]) for k in range(N_BR)]),
         (dproj, lambda tm: bs((tm, N_IN), lambda i: (i, 0)), w["win"],
          lambda a, wr: [(a[:, k * ns_in:(k + 1) * ns_in], wr[k]) for k in range(N_SH)])],
        dx1, s["x"], w["g_mix"])
    return dx, big, small


BIG_NAMES = ("w_in", "w_branch", "w_merge_gate", "w_out", "w_ffn_in", "w_ffn_out", "w_ple_gate", "w_ple_proj")


def unpack_big_grads(ga, gfo, gb, gin, gfi):
    return dict(w_in=gin, w_ffn_in=gfi, w_ffn_out=gfo,
                w_merge_gate=ga[:N_BR * BW].reshape(N_BR, BW, D_MODEL), w_out=ga[N_BR * BW:5 * BW], w_ple_gate=ga[5 * BW:],
                w_branch=gb[:N_BR * BW].reshape(N_BR, BW, BW), w_ple_proj=gb[N_BR * BW:])


SMALL_NAMES = ("g_mix", "conv_a", "sgu_ln_g", "sgu_ln_b", "sgu_w", "sgu_b", "conf_dw", "conf_ln_g", "conf_ln_b",
               "g_ffn", "g_ple")


def _pack_rows(arrays, rows):
    flat = jnp.concatenate([a.reshape(-1) for a in arrays])
    return jnp.pad(flat, (0, rows * D_MODEL - flat.shape[0])).reshape(rows, D_MODEL)


def _unpack_rows(packed, shapes):
    flat, out, pos = packed.reshape(-1), [], 0
    for shape in shapes:
        n = math.prod(shape)
        out.append(flat[pos:pos + n].reshape(shape))
        pos += n
    return out


def kernel(x, p, g_mix, w_in, conv_a, sgu_ln_g, sgu_ln_b, sgu_w, sgu_b, conf_dw, conf_ln_g, conf_ln_b, w_branch, w_merge_gate, w_out, g_ffn, w_ffn_in, w_ffn_out, g_ple, w_ple_gate, w_ple_proj, g_final, loss_target, m_g_mix, m_w_in, m_conv_a, m_sgu_ln_g, m_sgu_ln_b, m_sgu_w, m_sgu_b, m_conf_dw, m_conf_ln_g, m_conf_ln_b, m_w_branch, m_w_merge_gate, m_w_out, m_g_ffn, m_w_ffn_in, m_w_ffn_out, m_g_ple, m_w_ple_gate, m_w_ple_proj, m_g_final, v_g_mix, v_w_in, v_conv_a, v_sgu_ln_g, v_sgu_ln_b, v_sgu_w, v_sgu_b, v_conf_dw, v_conf_ln_g, v_conf_ln_b, v_w_branch, v_w_merge_gate, v_w_out, v_g_ffn, v_w_ffn_in, v_w_ffn_out, v_g_ple, v_w_ple_gate, v_w_ple_proj, v_g_final):
    weights = dict(g_mix=g_mix, w_in=w_in, conv_a=conv_a, sgu_ln_g=sgu_ln_g, sgu_ln_b=sgu_ln_b, sgu_w=sgu_w, sgu_b=sgu_b,
                   conf_dw=conf_dw, conf_ln_g=conf_ln_g, conf_ln_b=conf_ln_b, w_branch=w_branch, w_merge_gate=w_merge_gate,
                   w_out=w_out, g_ffn=g_ffn, w_ffn_in=w_ffn_in, w_ffn_out=w_ffn_out, g_ple=g_ple, w_ple_gate=w_ple_gate,
                   w_ple_proj=w_ple_proj, g_final=g_final)
    m_in = dict(g_mix=m_g_mix, w_in=m_w_in, conv_a=m_conv_a, sgu_ln_g=m_sgu_ln_g, sgu_ln_b=m_sgu_ln_b, sgu_w=m_sgu_w,
                sgu_b=m_sgu_b, conf_dw=m_conf_dw, conf_ln_g=m_conf_ln_g, conf_ln_b=m_conf_ln_b, w_branch=m_w_branch,
                w_merge_gate=m_w_merge_gate, w_out=m_w_out, g_ffn=m_g_ffn, w_ffn_in=m_w_ffn_in, w_ffn_out=m_w_ffn_out,
                g_ple=m_g_ple, w_ple_gate=m_w_ple_gate, w_ple_proj=m_w_ple_proj, g_final=m_g_final)
    v_in = dict(g_mix=v_g_mix, w_in=v_w_in, conv_a=v_conv_a, sgu_ln_g=v_sgu_ln_g, sgu_ln_b=v_sgu_ln_b, sgu_w=v_sgu_w,
                sgu_b=v_sgu_b, conf_dw=v_conf_dw, conf_ln_g=v_conf_ln_g, conf_ln_b=v_conf_ln_b, w_branch=v_w_branch,
                w_merge_gate=v_w_merge_gate, w_out=v_w_out, g_ffn=v_g_ffn, w_ffn_in=v_w_ffn_in, w_ffn_out=v_w_ffn_out,
                g_ple=v_g_ple, w_ple_gate=v_w_ple_gate, w_ple_proj=v_w_ple_proj, g_final=v_g_final)
    order = ("g_mix", "w_in", "conv_a", "sgu_ln_g", "sgu_ln_b", "sgu_w", "sgu_b", "conf_dw", "conf_ln_g", "conf_ln_b",
             "w_branch", "w_merge_gate", "w_out", "g_ffn", "w_ffn_in", "w_ffn_out", "g_ple", "w_ple_gate", "w_ple_proj",
             "g_final")
    depth = g_mix.shape[0]
    xs, tgt = x[0], loss_target[0]
    cw = BW // N_SH
    my_shard = 2 * lax.axis_index("x") + lax.axis_index("y")

    conv_rows = 16
    allc = gather8(_pack_rows([conv_a, conf_dw], conv_rows))
    shards = [_unpack_rows(allc[2 * s], [conv_a.shape, conf_dw.shape]) for s in range(N_SH)]
    conv_a_full = jnp.concatenate([sh[0] for sh in shards], axis=-1)
    conf_dw_full = jnp.concatenate([sh[1] for sh in shards], axis=-1)

    tril = jnp.tril(jnp.ones((BLK, BLK), bool))
    def placed_shards(i):
        shards = ([w_in[i], w_branch[i]] + [w_merge_gate[i, k] for k in range(N_BR)]
                  + [w_out[i], w_ffn_in[i], w_ffn_out[i], w_ple_gate[i], w_ple_proj[i]])
        return [lax.dynamic_update_slice(lax.empty((N_SH,) + sh.shape, BF16), sh.astype(BF16)[None],
                                         (my_shard,) + (0,) * sh.ndim) for sh in shards]

    def small_weights(i, win):
        vec = lambda a: a[i].reshape(1, -1)
        return dict(
            win=win, g_mix=vec(g_mix), g_ffn=vec(g_ffn), g_ple=vec(g_ple), conv_a=conv_a_full[i], conf_dw=conf_dw_full[i],
            sgu_ln_g=vec(sgu_ln_g), sgu_ln_b=vec(sgu_ln_b), conf_ln_g=vec(conf_ln_g), conf_ln_b=vec(conf_ln_b),
            sgu_wt=jnp.where(tril[None], sgu_w[i], 0.0).astype(BF16),
            sgu_bf=jnp.repeat(sgu_b[i].T, HEAD_D, axis=1))

    def late_weights(got):
        return dict(wbr=got[0], wg=jnp.stack([g.reshape(D_MODEL, D_MODEL) for g in got[1:5]]),
                    wout=got[5].reshape(D_MODEL, D_MODEL), wfi=got[6], wfo=got[7].reshape(FFN_H, D_MODEL),
                    wpg=got[8].reshape(D_MODEL, D_MODEL), wpp=got[9])

    class SplitAllGather:
        def __init__(self, bufs):
            self.shapes = [b.shape for b in bufs]
            self.ici = SplitExchange("allgather_ici", bufs, allgather_ici_plan(self.shapes), 3 * len(bufs))

        def ici_start(self, after, env=None):
            return self.ici.start(after)

        def ici_wait_d2d_start(self, after, env=None):
            landed = self.ici.wait(after)
            self.d2d = SplitExchange("allgather_d2d", landed, allgather_d2d_plan(self.shapes), 3 * len(landed))
            return self.d2d.start(landed[-1])

        def d2d_wait(self, after, env=None):
            self.got = self.d2d.wait(after)
            return None

    bufs0 = placed_shards(0)
    first = SplitAllGather(bufs0[:1])
    first.d2d_wait(first.ici_wait_d2d_start(first.ici_start(xs)))
    rest = SplitAllGather(bufs0[1:])
    layers = [small_weights(0, first.got[0])]
    act, saved = xs, []
    nxt_done = None
    for i in range(depth):
        hooks = Hooks()
        if i == 0:
            hooks.add("start", rest.ici_start)
            hooks.add("pre_conf", rest.ici_wait_d2d_start)
            hooks.add("pre_merge", rest.d2d_wait)
            hooks.add("pre_merge", lambda after, env: layers[0].update(late_weights(rest.got)))
        if i + 1 < depth:
            nxt = SplitAllGather(placed_shards(i + 1))
            points = ("pre_merge", "post_ffn_in", None) if i == 0 else ("start", "post_merge", "post_ffn_in")
            hooks.add(points[0], nxt.ici_start)
            hooks.add(points[1], nxt.ici_wait_d2d_start)
            if points[2]:
                hooks.add(points[2], nxt.d2d_wait)
        act, sv = layer_fwd(act, p[i, 0], layers[i], hooks)
        saved.append(sv)
        if i + 1 < depth:
            if i == 0:
                nxt.d2d_wait(act)
            layers.append({**small_weights(i + 1, nxt.got[0]), **late_weights(nxt.got[1:])})
    loss_part, dx, dg_final = loss_head(act, g_final.reshape(1, -1), tgt)

    big_red = [None] * depth
    small_red = [None] * depth
    small_rows = 80
    pending = None

    def small_vector(i, small):
        parts = [small[n] for n in SMALL_NAMES]
        return _pack_rows(parts + ([dg_final, loss_part[0, :1]] if i == 0 else []), small_rows)

    for i in reversed(range(depth)):
        hooks = Hooks()
        result = {}
        if pending is not None:
            rs, j, (small_ex, small_finish) = pending
            hooks.add("start", lambda after, env, ex=small_ex: ex.start(after))
            hooks.add("start", lambda after, env, rs=rs: rs.swap_start(after))
            hooks.add("pre_ffn", lambda after, env, rs=rs: rs.swap_wait_send_start(after))
            hooks.add("pre_dw_out", lambda after, env, rs=rs: rs.send_wait_share_start(after))
            hooks.add("pre_conva", lambda after, env, rs=rs, result=result: result.update(prev=rs.share_wait(after)))
            hooks.add("pre_conva", lambda after, env, fin=small_finish, result=result: result.update(small=fin(after)))
        if i == 0:
            def early_start(after, env, result=result):
                result["rs"] = SplitReduceScatter([env["gfo"], env["gfi"]])
                return result["rs"].swap_start(after)

            hooks.add("pre_dw_out", early_start)
            hooks.add("pre_conva", lambda after, env, result=result: result["rs"].swap_wait_send_start(after))
            hooks.add("pre_dw_in", lambda after, env, result=result: result["rs"].send_wait_share_start(after))
            hooks.add("end", lambda after, env, result=result: result.update(early=result["rs"].share_wait(after)))
        dx, big, small = layer_bwd(dx, p[i, 0], layers[i], saved[i], hooks)
        if pending is not None:
            big_red[pending[1]] = unpack_big_grads(*result["prev"])
            small_red[pending[1]] = result["small"]
        if i > 0:
            pending = (SplitReduceScatter(big), i, allreduce8_split(small_vector(i, small)))

    ga, _, gb, gin, _ = big
    late = SplitReduceScatter([ga, gb, gin])
    small_ex, small_finish = allreduce8_split(small_vector(0, small))
    upd = {}

    def update_upper(names, after):
        for name in names:
            upd[name] = adamw_layers(weights[name], [big_red[i][name] for i in range(1, depth)], m_in[name], v_in[name],
                                     1, after=after)
        return [upd[name][0] for name in names]

    done = update_upper(("w_in",), late.swap_start(small_ex.start(dx)))
    done = update_upper(("w_ffn_in", "w_merge_gate", "w_ffn_out"), late.swap_wait_send_start(done))
    small_red[0] = small_finish(done)
    done = update_upper(("w_branch", "w_out", "w_ple_gate", "w_ple_proj"), late.send_wait_share_start(done))
    ga, gb, gin = late.share_wait(done)
    gfo, gfi = result["early"]
    big_red[0] = unpack_big_grads(ga, gfo, gb, gin, gfi)

    layer_shapes = [small[n].shape for n in SMALL_NAMES]
    per_layer = [_unpack_rows(small_red[i], layer_shapes + ([dg_final.shape, (1,)] if i == 0 else []))
                 for i in range(depth)]
    grads = {n: jnp.stack([per_layer[i][k].reshape(weights[n].shape[1:] if n not in ("conv_a", "conf_dw")
                                                   else per_layer[i][k].shape) for i in range(depth)])
             for k, n in enumerate(SMALL_NAMES)}
    grads["g_final"] = per_layer[0][-2].reshape(-1)
    loss = per_layer[0][-1].reshape(())
    for n in ("conv_a", "conf_dw"):
        grads[n] = lax.dynamic_slice_in_dim(grads[n], my_shard * cw, cw, axis=2)

    small_all = [n for n in order if n not in BIG_NAMES]
    sm_shapes = [weights[n].shape for n in small_all]
    n_sm = sum(math.prod(sh) for sh in sm_shapes)
    sm_rows = -(-n_sm // (8 * D_MODEL)) * 8
    packed = [_pack_rows([src[n] for n in small_all], sm_rows) for src in (weights, grads, m_in, v_in)]
    sm_out = [_unpack_rows(o, sm_shapes) for o in adamw(*packed)]
    delta, new_m, new_v = ({n: o[k] for k, n in enumerate(small_all)} for o in sm_out)
    for name in BIG_NAMES:
        delta[name], new_m[name], new_v[name], grads[name] = adamw_layers(
            weights[name], [big_red[0][name]], m_in[name], v_in[name], 0, into=upd[name])

    return (loss, dx[None], *[grads[n] for n in order], *[delta[n] for n in order], *[new_m[n] for n in order],
            *[new_v[n] for n in order])
```
